```python
import jax, jax.numpy as jnp
from jax import lax
import numpy as np

D_MODEL = 1024
BATCH = 1
SEQ = 16384
DEPTH = 2
DEC_BATCH = 128
DEC_SEQ = 8
PAST_LEN = 16384
PAGE_SIZE = 128

N_A_LAYERS = DEPTH // 2
N_B_LAYERS = DEPTH - N_A_LAYERS
HEAD_DIM = 64
POOL_W = (3 * D_MODEL) // 4
POOL_WINDOWS = (2, 4, 8, 16)
POOL_GROUPS = len(POOL_WINDOWS)
POOL_GW = POOL_W // POOL_GROUPS
POOL_PAD = max(POOL_WINDOWS) - 1
N_Q_HEADS = POOL_W // HEAD_DIM
N_KV_HEADS = 4
GROUP = N_Q_HEADS // N_KV_HEADS
SWA_W = N_Q_HEADS * HEAD_DIM
KV_W = N_KV_HEADS * HEAD_DIM
WINDOW = 128
BLOCK = WINDOW
N_MEM = 256
MEM_HEADS = 4
MEM_W = MEM_HEADS * HEAD_DIM
ROT_DIM = HEAD_DIM // 4
ROPE_THETA = 500000.0
EPS = 1e-6

kernel_name = "yoco_pool_swa_sink_memxattn_step"


def rms_norm(x, g):
    xf = x.astype(jnp.float32)
    y = xf * lax.rsqrt(jnp.mean(xf * xf, axis=-1, keepdims=True) + EPS)
    return (y * g.astype(jnp.float32)).astype(x.dtype)


def rope_partial(x, pos):
    half = ROT_DIM // 2
    inv = ROPE_THETA ** (-jnp.arange(half, dtype=jnp.float32) * 2.0 / ROT_DIM)
    ang = pos.astype(jnp.float32)[:, None] * inv[None, :]
    cos = jnp.cos(ang)[:, None, :]
    sin = jnp.sin(ang)[:, None, :]
    xf = x.astype(jnp.float32)
    x1, x2, rest = xf[..., :half], xf[..., half:ROT_DIM], xf[..., ROT_DIM:]
    out = jnp.concatenate([x1 * cos - x2 * sin, x2 * cos + x1 * sin, rest], axis=-1)
    return out.astype(x.dtype)


def pool_mixer(u_ext, pos, mix_w, scale):
    T = u_ext.shape[1] - POOL_PAD
    uf = u_ext.astype(jnp.float32)
    cs = jnp.concatenate([jnp.zeros_like(uf[:, :1]), jnp.cumsum(uf, axis=1)], axis=1)
    u_new = uf[:, POOL_PAD:]
    outs = []
    for g, w in enumerate(POOL_WINDOWS):
        sl = slice(g * POOL_GW, (g + 1) * POOL_GW)
        win = cs[:, POOL_PAD + 1:POOL_PAD + 1 + T, sl] - cs[:, POOL_PAD + 1 - w:POOL_PAD + 1 - w + T, sl]
        cnt = jnp.minimum(pos + 1, w).astype(jnp.float32)[None, :, None]
        d = win / cnt - u_new[:, :, sl]
        outs.append(jnp.einsum('btc,cd->btd', d, mix_w[g].astype(jnp.float32)))
    y = jnp.concatenate(outs, axis=-1) * scale.astype(jnp.float32)
    return y.astype(u_ext.dtype)


def memory_kv(mem, g, w, k_gain):
    B, M, _ = mem.shape
    k, v = jnp.split(rms_norm(mem, g) @ w, [MEM_W], axis=-1)
    k = rms_norm(k.reshape(B, M, MEM_HEADS, HEAD_DIM), k_gain)
    return k, v.reshape(B, M, MEM_HEADS, HEAD_DIM)


def memory_branch(qm, gm, q_gain, mk, mv):
    B, T, _ = qm.shape
    q = rms_norm(qm.reshape(B, T, MEM_HEADS, HEAD_DIM), q_gain)
    s = jnp.einsum('bthd,bmhd->bhtm', q.astype(jnp.float32), mk.astype(jnp.float32)) * HEAD_DIM ** -0.5
    p = jax.nn.softmax(s, axis=-1)
    o = jnp.einsum('bhtm,bmhd->bthd', p, mv.astype(jnp.float32)).astype(qm.dtype)
    return o.reshape(B, T, MEM_W) * jax.nn.silu(gm)


def sink_softmax(s, mask, sinks):
    s = jnp.where(mask, s, -jnp.inf)
    sk = sinks.astype(jnp.float32).reshape(N_KV_HEADS, GROUP, 1, 1)
    m = jnp.maximum(jnp.max(s, axis=-1, keepdims=True), sk)
    p = jnp.exp(s - m)
    return p / (jnp.sum(p, axis=-1, keepdims=True) + jnp.exp(sk - m))


def swa_banded(q, k, v, sinks):
    B, S = q.shape[:2]
    nb = S // BLOCK
    qb = q.reshape(B, nb, BLOCK, N_KV_HEADS, GROUP, HEAD_DIM).astype(jnp.float32)

    def band(t):
        tb = t.reshape(B, nb, BLOCK, N_KV_HEADS, HEAD_DIM).astype(jnp.float32)
        prev = jnp.concatenate([jnp.zeros_like(tb[:, :1]), tb[:, :-1]], axis=1)
        return jnp.concatenate([prev, tb], axis=2)

    kk, vv = band(k), band(v)
    s = jnp.einsum('bnqhgd,bnkhd->bnhgqk', qb, kk) * HEAD_DIM ** -0.5
    qpos = jnp.arange(BLOCK)
    kpos = jnp.arange(2 * BLOCK) - BLOCK
    diff = qpos[:, None] - kpos[None, :]
    blk = jnp.arange(nb)[:, None, None] * BLOCK
    mask = (diff >= 0) & (diff < WINDOW) & (blk + kpos[None, None, :] >= 0)
    p = sink_softmax(s, mask[None, :, None, None], sinks)
    o = jnp.einsum('bnhgqk,bnkhd->bnqhgd', p, vv)
    return o.reshape(B, S, SWA_W).astype(q.dtype)


def swa_window(q, k_all, v_all, q_pos, k_pos, sinks):
    B, T = q.shape[:2]
    qg = q.reshape(B, T, N_KV_HEADS, GROUP, HEAD_DIM).astype(jnp.float32)
    s = jnp.einsum('bqhgd,bkhd->bhgqk', qg, k_all.astype(jnp.float32)) * HEAD_DIM ** -0.5
    diff = q_pos[:, None] - k_pos[None, :]
    mask = (diff >= 0) & (diff < WINDOW)
    p = sink_softmax(s, mask, sinks)
    o = jnp.einsum('bhgqk,bkhd->bqhgd', p, v_all.astype(jnp.float32))
    return o.reshape(B, T, SWA_W).astype(q.dtype)


def forward(x, pos, pool_prefix, swa_past_k, swa_past_v, mem_k, mem_v,
            norm_a, w_in_a, pool_mix_w, pool_scale, w_out_a, kv_norm, w_kv, k_norm,
            norm_b, w_in_b, q_norm, sinks, w_out_b, mem_q_norm):
    B, T, _ = x.shape
    new_pool = []
    for i in range(N_A_LAYERS):
        z = rms_norm(x, norm_a[i]) @ w_in_a[i]
        u, gp, qm, gm = jnp.split(z, [POOL_W, 2 * POOL_W, 2 * POOL_W + MEM_W], axis=-1)
        u_ext = jnp.concatenate([pool_prefix[i].astype(u.dtype), u], axis=1)
        yp = pool_mixer(u_ext, pos, pool_mix_w[i], pool_scale[i]) * jax.nn.silu(gp)
        ym = memory_branch(qm, gm, mem_q_norm[i], mem_k[i], mem_v[i])
        x = x + jnp.concatenate([yp, ym], axis=-1) @ w_out_a[i]
        new_pool.append(u_ext[:, -POOL_PAD:])
    k, v = jnp.split(rms_norm(x, kv_norm) @ w_kv, [KV_W], axis=-1)
    k = rope_partial(rms_norm(k.reshape(B, T, N_KV_HEADS, HEAD_DIM), k_norm), pos)
    v = v.reshape(B, T, N_KV_HEADS, HEAD_DIM)
    if swa_past_k is None:
        keep = min(WINDOW, T)
        new_k, new_v = k[:, -keep:], v[:, -keep:]
    else:
        L = swa_past_k.shape[1]
        k_all = jnp.concatenate([swa_past_k.astype(k.dtype), k], axis=1)
        v_all = jnp.concatenate([swa_past_v.astype(v.dtype), v], axis=1)
        k_pos = pos[0] - L + jnp.arange(L + T)
        new_k, new_v = k_all[:, -L:], v_all[:, -L:]
    for j in range(N_B_LAYERS):
        li = N_A_LAYERS + j
        z = rms_norm(x, norm_b[j]) @ w_in_b[j]
        q, gq, qm, gm = jnp.split(z, [SWA_W, 2 * SWA_W, 2 * SWA_W + MEM_W], axis=-1)
        q = rope_partial(rms_norm(q.reshape(B, T, N_Q_HEADS, HEAD_DIM), q_norm[j]), pos)
        if swa_past_k is None:
            o = swa_banded(q, k, v, sinks[j])
        else:
            o = swa_window(q, k_all, v_all, pos, k_pos, sinks[j])
        ys = o * jax.nn.silu(gq)
        ym = memory_branch(qm, gm, mem_q_norm[li], mem_k[li], mem_v[li])
        x = x + jnp.concatenate([ys, ym], axis=-1) @ w_out_b[j]
    return x, jnp.stack(new_pool), new_k, new_v


def setup_inputs(seed: int = 0) -> dict:
    key = jax.random.key(seed)
    ks = jax.random.split(key, 26)
    f32 = jnp.float32

    def nrm(k, shape, scale=1.0):
        return jax.random.normal(k, shape, f32) * scale

    def gain(k, shape):
        return 1.0 + 0.02 * jax.random.normal(k, shape, f32)

    w_buf = min(WINDOW, PAST_LEN)
    return {
        "x_prompt": nrm(ks[0], (BATCH, SEQ, D_MODEL)),
        "x_sample": nrm(ks[1], (DEC_BATCH, DEC_SEQ, D_MODEL)),
        "state_pool": nrm(ks[2], (N_A_LAYERS, DEC_BATCH, POOL_PAD, POOL_W)),
        "cache_swa_k": nrm(ks[3], (DEC_BATCH, w_buf, N_KV_HEADS, HEAD_DIM)),
        "cache_swa_v": nrm(ks[4], (DEC_BATCH, w_buf, N_KV_HEADS, HEAD_DIM)),
        "cache_mem_k": nrm(ks[5], (DEPTH, DEC_BATCH, N_MEM, MEM_HEADS, HEAD_DIM)),
        "cache_mem_v": nrm(ks[6], (DEPTH, DEC_BATCH, N_MEM, MEM_HEADS, HEAD_DIM)),
        "mem_prompt": nrm(ks[7], (BATCH, N_MEM, D_MODEL)),
        "norm_a": gain(ks[8], (N_A_LAYERS, D_MODEL)),
        "w_in_a": nrm(ks[9], (N_A_LAYERS, D_MODEL, 2 * POOL_W + 2 * MEM_W), D_MODEL ** -0.5),
        "pool_mix_w": nrm(ks[10], (N_A_LAYERS, POOL_GROUPS, POOL_GW, POOL_GW), POOL_GW ** -0.5),
        "pool_scale": 1.0 + 0.1 * jax.random.normal(ks[11], (N_A_LAYERS, POOL_W), f32),
        "w_out_a": nrm(ks[12], (N_A_LAYERS, POOL_W + MEM_W, D_MODEL), (POOL_W + MEM_W) ** -0.5),
        "kv_norm": gain(ks[13], (D_MODEL,)),
        "w_kv": nrm(ks[14], (D_MODEL, 2 * KV_W), D_MODEL ** -0.5),
        "k_norm": gain(ks[15], (HEAD_DIM,)),
        "norm_b": gain(ks[16], (N_B_LAYERS, D_MODEL)),
        "w_in_b": nrm(ks[17], (N_B_LAYERS, D_MODEL, 2 * SWA_W + 2 * MEM_W), D_MODEL ** -0.5),
        "q_norm": gain(ks[18], (N_B_LAYERS, HEAD_DIM)),
        "sinks": nrm(ks[19], (N_B_LAYERS, N_Q_HEADS), 1.0),
        "w_out_b": nrm(ks[20], (N_B_LAYERS, SWA_W + MEM_W, D_MODEL), (SWA_W + MEM_W) ** -0.5),
        "mem_norm": gain(ks[21], (DEPTH, D_MODEL)),
        "w_mem_kv": nrm(ks[22], (DEPTH, D_MODEL, 2 * MEM_W), D_MODEL ** -0.5),
        "mem_q_norm": gain(ks[23], (DEPTH, HEAD_DIM)),
        "mem_k_norm": gain(ks[24], (DEPTH, HEAD_DIM)),
    }


def reference(x_prompt, x_sample, state_pool, cache_swa_k, cache_swa_v, cache_mem_k, cache_mem_v, mem_prompt,
              norm_a, w_in_a, pool_mix_w, pool_scale, w_out_a, kv_norm, w_kv, k_norm,
              norm_b, w_in_b, q_norm, sinks, w_out_b, mem_norm, w_mem_kv, mem_q_norm, mem_k_norm):
    mk_list, mv_list = [], []
    for l in range(DEPTH):
        mk, mv = memory_kv(mem_prompt, mem_norm[l], w_mem_kv[l], mem_k_norm[l])
        mk_list.append(mk)
        mv_list.append(mv)
    mem_k_prompt = jnp.stack(mk_list)
    mem_v_prompt = jnp.stack(mv_list)

    weights = (norm_a, w_in_a, pool_mix_w, pool_scale, w_out_a, kv_norm, w_kv, k_norm,
               norm_b, w_in_b, q_norm, sinks, w_out_b, mem_q_norm)

    bp, s_len, _ = x_prompt.shape
    pool_zero = jnp.zeros((N_A_LAYERS, bp, POOL_PAD, POOL_W), x_prompt.dtype)
    y_prompt, pool_p, swa_k_p, swa_v_p = forward(
        x_prompt, jnp.arange(s_len), pool_zero, None, None, mem_k_prompt, mem_v_prompt, *weights)

    pos_s = PAST_LEN + jnp.arange(x_sample.shape[1])
    y_sample, pool_s, swa_k_s, swa_v_s = forward(
        x_sample, pos_s, state_pool, cache_swa_k, cache_swa_v, cache_mem_k, cache_mem_v, *weights)

    return (y_prompt, y_sample, pool_p, pool_s, swa_k_p, swa_v_p, swa_k_s, swa_v_s, mem_k_prompt, mem_v_prompt)
```

```python
import functools

import jax
import jax.numpy as jnp
from jax import lax
from jax.experimental import pallas as pl
from jax.experimental.pallas import tpu as pltpu

D_MODEL = 1024
SEQ = 16384
DEPTH = 2
DEC_BATCH = 128
DEC_SEQ = 8
PAST_LEN = 16384
HEAD_DIM = 64
POOL_W = 768
POOL_WINDOWS = (2, 4, 8, 16)
POOL_GW = 192
POOL_PAD = 15
N_Q_HEADS = 12
N_KV_HEADS = 4
GROUP = 3
SWA_W = 768
KV_W = 256
WINDOW = 128
N_MEM = 256
MEM_HEADS = 4
MEM_W = 256
ROT_DIM = 16
ROPE_THETA = 500000.0
EPS = 1e-6

F32 = jnp.float32
BF16 = jnp.bfloat16
NEG_INF = float("-inf")
Q_SCALE = HEAD_DIM ** -0.5

LANES = 128
SUBLANES = 8
MXU_DIM = 256
CARRY_ROWS = 16
PROMPT_TILE = 256
SAMPLE_BLOCK = 8
VMEM_LIMIT = 56 * 1024 * 1024


def _dot(a, b):
    return jnp.dot(a, b, preferred_element_type=F32)


def _dot_nt(a, b):
    return lax.dot_general(a, b, (((1,), (1,)), ((), ())), preferred_element_type=F32)


def _rms_unit(x):
    return x * lax.rsqrt(jnp.mean(x * x, axis=-1, keepdims=True) + EPS)


def _silu(g):
    return g / (1.0 + jnp.exp(-g))


def _head_rms(y, gain, bseg):
    parts = []
    for c in range(y.shape[1] // MXU_DIM):
        yc = y[:, c * MXU_DIM:(c + 1) * MXU_DIM]
        ms = _dot((yc * yc).astype(BF16), bseg)
        parts.append(yc * lax.rsqrt(ms + EPS) * gain[:, c * MXU_DIM:(c + 1) * MXU_DIM])
    return parts[0] if len(parts) == 1 else jnp.concatenate(parts, axis=1)


def _rope(y, cos_t, sin_t, first8):
    parts = []
    for c in range(y.shape[1] // LANES):
        yc = y[:, c * LANES:(c + 1) * LANES]
        partner = jnp.where(first8, pltpu.roll(yc, LANES - 8, 1), pltpu.roll(yc, 8, 1))
        parts.append(yc * cos_t + partner * sin_t)
    return parts[0] if len(parts) == 1 else jnp.concatenate(parts, axis=1)


def _pool_window_sums(u_hist):
    def back(a, k):
        return pltpu.roll(a, k, 0)

    lane = lax.broadcasted_iota(jnp.int32, (1, LANES), 1)
    s2 = u_hist + back(u_hist, 1)
    t = s2[:, LANES:]
    s4 = t + back(t, 2)
    t = s4[:, 2 * LANES:]
    s8 = t + back(t, 4)
    t = s8[:, LANES:]
    s16 = t + back(t, 8)
    tiles = [
        s2[:, :LANES],
        jnp.where(lane < 64, s2[:, LANES:2 * LANES], s4[:, :LANES]),
        s4[:, LANES:2 * LANES],
        s8[:, :LANES],
        jnp.where(lane < 64, s8[:, LANES:2 * LANES], s16[:, :LANES]),
        s16[:, LANES:],
    ]
    return jnp.concatenate(tiles, axis=1)


def _head_masks():
    lane = lax.broadcasted_iota(jnp.int32, (1, MXU_DIM), 1)
    return [(lane // HEAD_DIM) == j for j in range(MXU_DIM // HEAD_DIM)]


def _first8_mask():
    lane = lax.broadcasted_iota(jnp.int32, (1, LANES), 1)
    return (lane % HEAD_DIM) < (ROT_DIM // 2)


def _softmax_pv(s, v, sink):
    m = jnp.max(s, axis=-1, keepdims=True)
    if sink is not None:
        m = jnp.maximum(m, sink)
    p = jnp.exp(s - m)
    den = jnp.sum(p, axis=-1, keepdims=True)
    if sink is not None:
        den = den + jnp.exp(sink - m)
    return _dot(p.astype(BF16), v) * (1.0 / den)


def _memkv_kernel(mem_ref, norm_ref, w_ref, kgain_ref, bseg_ref, mk_ref, mv_ref, mkb_ref, mvb_ref):
    xn = (_rms_unit(mem_ref[...]) * norm_ref[0]).astype(BF16)
    kv = _dot(xn, w_ref[0])
    k = _head_rms(kv[:, :MEM_W], kgain_ref[0], bseg_ref[...])
    v = kv[:, MEM_W:]
    mk_ref[0] = k
    mv_ref[0] = v
    mkb_ref[0] = k.astype(BF16)
    mvb_ref[0] = v.astype(BF16)


def _memkv_call(mem, mem_norm, w_mem_kv, mem_kgain, bseg):
    const2 = lambda l: (0, 0)
    per_layer3 = lambda l: (l, 0, 0)
    out_f = jax.ShapeDtypeStruct((DEPTH, N_MEM, MEM_W), F32)
    out_b = jax.ShapeDtypeStruct((DEPTH, N_MEM, MEM_W), BF16)
    return pl.pallas_call(
        _memkv_kernel,
        grid=(DEPTH,),
        in_specs=[
            pl.BlockSpec((N_MEM, D_MODEL), const2),
            pl.BlockSpec((1, 1, D_MODEL), per_layer3),
            pl.BlockSpec((1, D_MODEL, 2 * MEM_W), per_layer3),
            pl.BlockSpec((1, 1, MEM_W), per_layer3),
            pl.BlockSpec((MXU_DIM, MXU_DIM), const2),
        ],
        out_specs=[pl.BlockSpec((1, N_MEM, MEM_W), per_layer3)] * 4,
        out_shape=[out_f, out_f, out_b, out_b],
        compiler_params=pltpu.CompilerParams(dimension_semantics=("arbitrary",)),
        name="memkv",
    )(mem, mem_norm, w_mem_kv, mem_kgain, bseg)


def _mem_attn_rows(qm, gm, gain, mk, mv, bseg, hmask):
    q = _head_rms(qm, gain, bseg) * Q_SCALE
    acc = None
    for h in range(MEM_HEADS):
        qh = jnp.where(hmask[h], q, 0.0).astype(BF16)
        o = _softmax_pv(_dot_nt(qh, mk), mv, None)
        o = jnp.where(hmask[h], o, 0.0)
        acc = o if acc is None else acc + o
    return acc * _silu(gm)


def _swa_block(qb, kc, vc, bias, sinks_ref, hmask):
    outs = []
    for c in range(SWA_W // MXU_DIM):
        qc = qb[:, c * MXU_DIM:(c + 1) * MXU_DIM]
        kcc = kc[:, c * MXU_DIM:(c + 1) * MXU_DIM]
        vcc = vc[:, c * MXU_DIM:(c + 1) * MXU_DIM]
        acc = None
        for jj in range(MXU_DIM // HEAD_DIM):
            sink = sinks_ref[c * (MXU_DIM // HEAD_DIM) + jj]
            qh = jnp.where(hmask[jj], qc, 0.0).astype(BF16)
            o = _softmax_pv(_dot_nt(qh, kcc) + bias, vcc, sink)
            o = jnp.where(hmask[jj], o, 0.0)
            acc = o if acc is None else acc + o
        outs.append(acc)
    return jnp.concatenate(outs, axis=1)


def _prompt_kernel(sinks_ref, x_ref, cos_ref, sin_ref,
                   norm_a_ref, w_in_a_ref, wbd_ref, pscale_ref, wcol_ref, w_out_a_ref,
                   kv_norm_ref, w_kv_ref, kgain_ref,
                   norm_b_ref, w_in_b_ref, qgain_ref, w_out_b_ref,
                   mqg_ref, mk_ref, mv_ref, bseg_ref, srep_ref,
                   y_ref, pool_ref, ko_ref, vo_ref,
                   ucarry, kprev, vprev):
    tq = PROMPT_TILE
    i = pl.program_id(0)

    @pl.when(i == 0)
    def _():
        ucarry[0] = jnp.zeros((CARRY_ROWS, POOL_W), F32)
        kprev[0] = jnp.zeros((WINDOW, SWA_W), BF16)
        vprev[0] = jnp.zeros((WINDOW, SWA_W), BF16)

    rd = i % 2
    wr = (i + 1) % 2

    hmask = _head_masks()
    first8 = _first8_mask()
    bseg = bseg_ref[...]
    cos_t = cos_ref[...]
    sin_t = sin_ref[...]
    x = x_ref[...]

    xn = (_rms_unit(x) * norm_a_ref[...]).astype(BF16)
    u = _dot(xn, w_in_a_ref[:, 0:POOL_W])
    gp = _dot(xn, w_in_a_ref[:, POOL_W:2 * POOL_W])
    qm = _dot(xn, w_in_a_ref[:, 2 * POOL_W:2 * POOL_W + MEM_W])
    gm = _dot(xn, w_in_a_ref[:, 2 * POOL_W + MEM_W:])
    u_hist = jnp.concatenate([ucarry[rd], u], axis=0)
    win = _pool_window_sums(u_hist)[CARRY_ROWS:]
    pos = i * tq + lax.broadcasted_iota(jnp.int32, (tq, 1), 0)
    cnt = jnp.minimum((pos + 1).astype(F32), wcol_ref[...])
    d = win / cnt - u
    yp = _dot(d.astype(BF16), wbd_ref[...]) * pscale_ref[...] * _silu(gp)
    ym = _mem_attn_rows(qm, gm, mqg_ref[0:1, :], mk_ref[0], mv_ref[0], bseg, hmask)
    x1 = x + _dot(jnp.concatenate([yp, ym], axis=1).astype(BF16), w_out_a_ref[...])
    ucarry[wr] = u[tq - CARRY_ROWS:, :]

    r = _rms_unit(x1)
    kv = _dot((r * kv_norm_ref[...]).astype(BF16), w_kv_ref[...])
    k = _rope(_head_rms(kv[:, :KV_W], kgain_ref[...], bseg), cos_t, sin_t, first8)
    v = kv[:, KV_W:]
    srep = srep_ref[...]
    k_all = jnp.concatenate([kprev[rd], _dot(k.astype(BF16), srep).astype(BF16)], axis=0)
    v_all = jnp.concatenate([vprev[rd], _dot(v.astype(BF16), srep).astype(BF16)], axis=0)

    xb = (r * norm_b_ref[...]).astype(BF16)
    zq = _dot(xb, w_in_b_ref[:, 0:SWA_W])
    gq = _dot(xb, w_in_b_ref[:, SWA_W:2 * SWA_W])
    qm2 = _dot(xb, w_in_b_ref[:, 2 * SWA_W:2 * SWA_W + MEM_W])
    gm2 = _dot(xb, w_in_b_ref[:, 2 * SWA_W + MEM_W:])
    q = _rope(_head_rms(zq, qgain_ref[...], bseg), cos_t, sin_t, first8) * Q_SCALE

    qi = lax.broadcasted_iota(jnp.int32, (WINDOW, 2 * WINDOW), 0)
    ci = lax.broadcasted_iota(jnp.int32, (WINDOW, 2 * WINDOW), 1)
    band_bias = jnp.where(ci > qi, jnp.where(ci <= qi + WINDOW, 0.0, NEG_INF), NEG_INF)
    ys_blocks = []
    for b in range(tq // WINDOW):
        bias = band_bias
        if b == 0:
            key_pos = ci + (i * tq - WINDOW)
            bias = jnp.where(key_pos >= 0, band_bias, NEG_INF)
        qb = q[b * WINDOW:(b + 1) * WINDOW, :]
        kc = k_all[b * WINDOW:(b + 2) * WINDOW, :]
        vc = v_all[b * WINDOW:(b + 2) * WINDOW, :]
        ys_blocks.append(_swa_block(qb, kc, vc, bias, sinks_ref, hmask))
    ys = jnp.concatenate(ys_blocks, axis=0) * _silu(gq)
    ym2 = _mem_attn_rows(qm2, gm2, mqg_ref[1:2, :], mk_ref[1], mv_ref[1], bseg, hmask)
    y_ref[...] = x1 + _dot(jnp.concatenate([ys, ym2], axis=1).astype(BF16), w_out_b_ref[...])

    kprev[wr] = k_all[tq:, :]
    vprev[wr] = v_all[tq:, :]

    @pl.when(i == pl.num_programs(0) - 1)
    def _():
        ko_ref[...] = k[tq - WINDOW:, :]
        vo_ref[...] = v[tq - WINDOW:, :]
        pool_ref[...] = pltpu.roll(u[tq - CARRY_ROWS:, :], CARRY_ROWS - 1, 0)[0:POOL_PAD, :]


def _const_spec(shape):
    nd = len(shape)
    return pl.BlockSpec(shape, lambda i: (0,) * nd, pipeline_mode=pl.Buffered(1))


def _prompt_call(sinks, x, cos_t, sin_t, wts, mqg, mk, mv, bseg, srep):
    tq = PROMPT_TILE
    n = x.shape[0]
    row_spec = lambda w: pl.BlockSpec((tq, w), lambda i: (i, 0))
    in_specs = [pl.BlockSpec(memory_space=pltpu.SMEM),
                row_spec(D_MODEL), row_spec(LANES), row_spec(LANES)]
    in_specs += [_const_spec(w.shape) for w in wts]
    in_specs += [_const_spec(a.shape) for a in (mqg, mk, mv, bseg, srep)]
    out_shape = [jax.ShapeDtypeStruct((n, D_MODEL), F32),
                 jax.ShapeDtypeStruct((POOL_PAD, POOL_W), F32),
                 jax.ShapeDtypeStruct((WINDOW, KV_W), F32),
                 jax.ShapeDtypeStruct((WINDOW, KV_W), F32)]
    out_specs = [row_spec(D_MODEL),
                 pl.BlockSpec((POOL_PAD, POOL_W), lambda i: (0, 0)),
                 pl.BlockSpec((WINDOW, KV_W), lambda i: (0, 0)),
                 pl.BlockSpec((WINDOW, KV_W), lambda i: (0, 0))]
    return pl.pallas_call(
        _prompt_kernel,
        grid=(n // tq,),
        in_specs=in_specs,
        out_specs=out_specs,
        out_shape=out_shape,
        scratch_shapes=[pltpu.VMEM((2, CARRY_ROWS, POOL_W), F32),
                        pltpu.VMEM((2, WINDOW, SWA_W), BF16),
                        pltpu.VMEM((2, WINDOW, SWA_W), BF16)],
        compiler_params=pltpu.CompilerParams(dimension_semantics=("arbitrary",),
                                             vmem_limit_bytes=VMEM_LIMIT),
        name="prompt",
    )(sinks, x, cos_t, sin_t, *wts, mqg, mk, mv, bseg, srep)


def _seq_attn(q8, kmat, vmat, bias, sink_col, hmask):
    nh = MXU_DIM // HEAD_DIM
    qrows = jnp.concatenate([jnp.where(hmask[h], q8, 0.0) for h in range(nh)], axis=0).astype(BF16)
    s = _dot_nt(qrows, kmat)
    if bias is not None:
        s = s + bias
    o = _softmax_pv(s, vmat, sink_col)
    acc = None
    for h in range(nh):
        oh = jnp.where(hmask[h], o[h * DEC_SEQ:(h + 1) * DEC_SEQ, :], 0.0)
        acc = oh if acc is None else acc + oh
    return acc


def _sample_kernel(sinks_ref, x_ref, pref_ref, ck_ref, cv_ref, cmk_ref, cmv_ref, cos_ref, sin_ref,
                   norm_a_ref, w_in_a_ref, wbd_ref, pscale_ref, wcol_ref, w_out_a_ref,
                   kv_norm_ref, w_kv_ref, kgain_ref,
                   norm_b_ref, w_in_b_ref, qgain_ref, w_out_b_ref,
                   mqg_ref, bseg_ref, srep_ref,
                   y_ref, pool_ref, ko_ref, vo_ref,
                   q_scr, att_scr, qm_scr, ym_scr, krep, vrep):
    sb = SAMPLE_BLOCK
    m = sb * DEC_SEQ
    nkeys = WINDOW + DEC_SEQ
    hmask = _head_masks()
    first8 = _first8_mask()
    bseg = bseg_ref[...]
    cos_t = cos_ref[...]
    sin_t = sin_ref[...]

    @pl.when(pl.program_id(0) == 0)
    def _():
        krep[:, nkeys:, :] = jnp.zeros((sb, 2 * WINDOW - nkeys, SWA_W), BF16)
        vrep[:, nkeys:, :] = jnp.zeros((sb, 2 * WINDOW - nkeys, SWA_W), BF16)

    x = x_ref[...].reshape(m, D_MODEL)

    xn = (_rms_unit(x) * norm_a_ref[...]).astype(BF16)
    u = _dot(xn, w_in_a_ref[:, 0:POOL_W])
    gp = _dot(xn, w_in_a_ref[:, POOL_W:2 * POOL_W])
    qm = _dot(xn, w_in_a_ref[:, 2 * POOL_W:2 * POOL_W + MEM_W])
    gm = _dot(xn, w_in_a_ref[:, 2 * POOL_W + MEM_W:])

    hist_rows = CARRY_ROWS + DEC_SEQ
    u_hist = jnp.concatenate([pref_ref[...], u.reshape(sb, DEC_SEQ, POOL_W)], axis=1)
    u_hist = u_hist.reshape(sb * hist_rows, POOL_W)
    win = _pool_window_sums(u_hist).reshape(sb, hist_rows, POOL_W)[:, CARRY_ROWS:, :].reshape(m, POOL_W)
    tok = lax.broadcasted_iota(jnp.int32, (sb, DEC_SEQ, 1), 1).reshape(m, 1)
    cnt = jnp.minimum((PAST_LEN + tok + 1).astype(F32), wcol_ref[...])
    d = win / cnt - u
    yp = _dot(d.astype(BF16), wbd_ref[...]) * pscale_ref[...] * _silu(gp)
    shifted = pltpu.roll(u_hist, sb * hist_rows - 1, 0).reshape(sb, hist_rows, POOL_W)
    pool_ref[...] = shifted[:, SUBLANES:, :][:, 0:POOL_PAD, :]

    qm_scr[0] = (_head_rms(qm, mqg_ref[0:1, :], bseg) * Q_SCALE).reshape(sb, DEC_SEQ, MEM_W)

    def mem_body(layer, b, carry):
        mk = cmk_ref[layer, b].astype(BF16)
        mv = cmv_ref[layer, b].astype(BF16)
        ym_scr[layer, b] = _seq_attn(qm_scr[layer, b], mk, mv, None, None, hmask)
        return carry

    lax.fori_loop(0, sb, functools.partial(mem_body, 0), 0)
    ym = ym_scr[0].reshape(m, MEM_W) * _silu(gm)
    x1 = x + _dot(jnp.concatenate([yp, ym], axis=1).astype(BF16), w_out_a_ref[...])

    r = _rms_unit(x1)
    kv = _dot((r * kv_norm_ref[...]).astype(BF16), w_kv_ref[...])
    k = _rope(_head_rms(kv[:, :KV_W], kgain_ref[...], bseg), cos_t, sin_t, first8)
    v = kv[:, KV_W:]
    ck = ck_ref[...]
    cv = cv_ref[...]
    keep = WINDOW - DEC_SEQ
    ko_ref[:, 0:keep, :] = ck[:, DEC_SEQ:, :]
    ko_ref[:, keep:, :] = k.reshape(sb, DEC_SEQ, KV_W)
    vo_ref[:, 0:keep, :] = cv[:, DEC_SEQ:, :]
    vo_ref[:, keep:, :] = v.reshape(sb, DEC_SEQ, KV_W)
    srep = srep_ref[...]
    krep[:, 0:WINDOW, :] = _dot(ck.reshape(sb * WINDOW, KV_W).astype(BF16), srep).astype(BF16).reshape(
        sb, WINDOW, SWA_W)
    vrep[:, 0:WINDOW, :] = _dot(cv.reshape(sb * WINDOW, KV_W).astype(BF16), srep).astype(BF16).reshape(
        sb, WINDOW, SWA_W)
    krep[:, WINDOW:nkeys, :] = _dot(k.astype(BF16), srep).astype(BF16).reshape(sb, DEC_SEQ, SWA_W)
    vrep[:, WINDOW:nkeys, :] = _dot(v.astype(BF16), srep).astype(BF16).reshape(sb, DEC_SEQ, SWA_W)

    xb = (r * norm_b_ref[...]).astype(BF16)
    zq = _dot(xb, w_in_b_ref[:, 0:SWA_W])
    gq = _dot(xb, w_in_b_ref[:, SWA_W:2 * SWA_W])
    qm2 = _dot(xb, w_in_b_ref[:, 2 * SWA_W:2 * SWA_W + MEM_W])
    gm2 = _dot(xb, w_in_b_ref[:, 2 * SWA_W + MEM_W:])
    q = _rope(_head_rms(zq, qgain_ref[...], bseg), cos_t, sin_t, first8) * Q_SCALE
    q_scr[...] = q.reshape(sb, DEC_SEQ, SWA_W)
    qm_scr[1] = (_head_rms(qm2, mqg_ref[1:2, :], bseg) * Q_SCALE).reshape(sb, DEC_SEQ, MEM_W)

    nh = MXU_DIM // HEAD_DIM
    rows = nh * DEC_SEQ
    ti = lax.broadcasted_iota(jnp.int32, (rows, 2 * WINDOW), 0) % DEC_SEQ
    ci = lax.broadcasted_iota(jnp.int32, (rows, 2 * WINDOW), 1)
    bias = jnp.where((ci > ti) & (ci <= ti + WINDOW), 0.0, NEG_INF)
    hrow = lax.broadcasted_iota(jnp.int32, (rows, 1), 0) // DEC_SEQ
    sink_cols = []
    for c in range(SWA_W // MXU_DIM):
        col = jnp.zeros((rows, 1), F32)
        for jj in range(nh):
            col = jnp.where(hrow == jj, sinks_ref[c * nh + jj], col)
        sink_cols.append(col)

    def swa_body(b, carry):
        for c in range(SWA_W // MXU_DIM):
            sl = slice(c * MXU_DIM, (c + 1) * MXU_DIM)
            att_scr[b, :, sl] = _seq_attn(q_scr[b, :, sl], krep[b, :, sl], vrep[b, :, sl], bias,
                                          sink_cols[c], hmask)
        return carry

    lax.fori_loop(0, sb, swa_body, 0)
    lax.fori_loop(0, sb, functools.partial(mem_body, 1), 0)
    ys = att_scr[...].reshape(m, SWA_W) * _silu(gq)
    ym2 = ym_scr[1].reshape(m, MEM_W) * _silu(gm2)
    y = x1 + _dot(jnp.concatenate([ys, ym2], axis=1).astype(BF16), w_out_b_ref[...])
    y_ref[...] = y.reshape(sb, DEC_SEQ, D_MODEL)


def _sample_call(sinks, x, pref, ck, cv, cmk, cmv, cos_t, sin_t, wts, mqg, bseg, srep):
    sb = SAMPLE_BLOCK
    nb = x.shape[0]
    seq3 = lambda a, b_: pl.BlockSpec((sb, a, b_), lambda i: (i, 0, 0))
    in_specs = [pl.BlockSpec(memory_space=pltpu.SMEM),
                seq3(DEC_SEQ, D_MODEL), seq3(CARRY_ROWS, POOL_W), seq3(WINDOW, KV_W), seq3(WINDOW, KV_W),
                pl.BlockSpec((DEPTH, sb, N_MEM, MEM_W), lambda i: (0, i, 0, 0)),
                pl.BlockSpec((DEPTH, sb, N_MEM, MEM_W), lambda i: (0, i, 0, 0)),
                _const_spec(cos_t.shape), _const_spec(sin_t.shape)]
    in_specs += [_const_spec(w.shape) for w in wts]
    in_specs += [_const_spec(a.shape) for a in (mqg, bseg, srep)]
    out_shape = [jax.ShapeDtypeStruct((nb, DEC_SEQ, D_MODEL), F32),
                 jax.ShapeDtypeStruct((nb, POOL_PAD, POOL_W), F32),
                 jax.ShapeDtypeStruct((nb, WINDOW, KV_W), F32),
                 jax.ShapeDtypeStruct((nb, WINDOW, KV_W), F32)]
    out_specs = [seq3(DEC_SEQ, D_MODEL), seq3(POOL_PAD, POOL_W), seq3(WINDOW, KV_W), seq3(WINDOW, KV_W)]
    return pl.pallas_call(
        _sample_kernel,
        grid=(nb // sb,),
        in_specs=in_specs,
        out_specs=out_specs,
        out_shape=out_shape,
        scratch_shapes=[pltpu.VMEM((sb, DEC_SEQ, SWA_W), F32),
                        pltpu.VMEM((sb, DEC_SEQ, SWA_W), F32),
                        pltpu.VMEM((DEPTH, sb, DEC_SEQ, MEM_W), F32),
                        pltpu.VMEM((DEPTH, sb, DEC_SEQ, MEM_W), F32),
                        pltpu.VMEM((sb, 2 * WINDOW, SWA_W), BF16),
                        pltpu.VMEM((sb, 2 * WINDOW, SWA_W), BF16)],
        compiler_params=pltpu.CompilerParams(dimension_semantics=("arbitrary",),
                                             vmem_limit_bytes=VMEM_LIMIT),
        name="sample",
    )(sinks, x, pref, ck, cv, cmk, cmv, cos_t, sin_t, *wts, mqg, bseg, srep)


def _rope_tables(pos):
    half = ROT_DIM // 2
    inv = ROPE_THETA ** (-jnp.arange(half, dtype=F32) * 2.0 / ROT_DIM)
    ang = pos.astype(F32)[:, None] * inv[None, :]
    cos, sin = jnp.cos(ang), jnp.sin(ang)
    t = pos.shape[0]
    rest = HEAD_DIM - ROT_DIM
    cos64 = jnp.concatenate([cos, cos, jnp.ones((t, rest), F32)], axis=1)
    sin64 = jnp.concatenate([-sin, sin, jnp.zeros((t, rest), F32)], axis=1)
    reps = LANES // HEAD_DIM
    return jnp.tile(cos64, (1, reps)), jnp.tile(sin64, (1, reps))


def _head_tile(g, width):
    return jnp.tile(g.astype(F32), width // HEAD_DIM).reshape(1, width)


def kernel(x_prompt, x_sample, state_pool, cache_swa_k, cache_swa_v, cache_mem_k, cache_mem_v, mem_prompt,
           norm_a, w_in_a, pool_mix_w, pool_scale, w_out_a, kv_norm, w_kv, k_norm,
           norm_b, w_in_b, q_norm, sinks, w_out_b, mem_norm, w_mem_kv, mem_q_norm, mem_k_norm):
    seg = jnp.arange(MXU_DIM) // HEAD_DIM
    bseg = ((seg[:, None] == seg[None, :]).astype(F32) / HEAD_DIM).astype(BF16)
    src = (jnp.arange(SWA_W) // HEAD_DIM // GROUP) * HEAD_DIM + jnp.arange(SWA_W) % HEAD_DIM
    srep = (jnp.arange(KV_W)[:, None] == src[None, :]).astype(BF16)
    wcol = jnp.repeat(jnp.asarray(POOL_WINDOWS, F32), POOL_GW).reshape(1, POOL_W)

    wbd = jax.scipy.linalg.block_diag(*[pool_mix_w[0, g] for g in range(len(POOL_WINDOWS))]).astype(BF16)
    wts = (norm_a[0].reshape(1, D_MODEL), w_in_a[0].astype(BF16), wbd, pool_scale[0].reshape(1, POOL_W), wcol,
           w_out_a[0].astype(BF16),
           kv_norm.reshape(1, D_MODEL), w_kv.astype(BF16), _head_tile(k_norm, KV_W),
           norm_b[0].reshape(1, D_MODEL), w_in_b[0].astype(BF16), _head_tile(q_norm[0], SWA_W),
           w_out_b[0].astype(BF16))
    mqg = jnp.concatenate([_head_tile(mem_q_norm[l], MEM_W) for l in range(DEPTH)], axis=0)
    mkg = jnp.stack([_head_tile(mem_k_norm[l], MEM_W) for l in range(DEPTH)])
    sinks1 = sinks[0].astype(F32)

    mk, mv, mk_b, mv_b = _memkv_call(mem_prompt[0], mem_norm.reshape(DEPTH, 1, D_MODEL),
                                     w_mem_kv.astype(BF16), mkg, bseg)

    cos_p, sin_p = _rope_tables(jnp.arange(SEQ))
    y_p, pool_p, k_p, v_p = _prompt_call(sinks1, x_prompt[0], cos_p, sin_p, wts, mqg, mk_b, mv_b, bseg, srep)

    cos_s, sin_s = _rope_tables(PAST_LEN + jnp.arange(DEC_SEQ))
    cos_s = jnp.tile(cos_s, (SAMPLE_BLOCK, 1))
    sin_s = jnp.tile(sin_s, (SAMPLE_BLOCK, 1))
    pref = jnp.pad(state_pool[0], ((0, 0), (CARRY_ROWS - POOL_PAD, 0), (0, 0)))
    y_s, pool_s, k_s, v_s = _sample_call(
        sinks1, x_sample, pref,
        cache_swa_k.reshape(DEC_BATCH, WINDOW, KV_W), cache_swa_v.reshape(DEC_BATCH, WINDOW, KV_W),
        cache_mem_k.reshape(DEPTH, DEC_BATCH, N_MEM, MEM_W), cache_mem_v.reshape(DEPTH, DEC_BATCH, N_MEM, MEM_W),
        cos_s, sin_s, wts, mqg, bseg, srep)

    kv4 = (N_KV_HEADS, HEAD_DIM)
    return (y_p[None], y_s, pool_p[None, None], pool_s[None],
            k_p.reshape(1, WINDOW, *kv4), v_p.reshape(1, WINDOW, *kv4),
            k_s.reshape(DEC_BATCH, WINDOW, *kv4), v_s.reshape(DEC_BATCH, WINDOW, *kv4),
            mk.reshape(DEPTH, 1, N_MEM, MEM_HEADS, HEAD_DIM), mv.reshape(DEPTH, 1, N_MEM, MEM_HEADS, HEAD_DIM))
```

```python
import functools

import jax
import jax.numpy as jnp
from jax import lax
from jax.experimental import pallas as pl
from jax.experimental.pallas import tpu as pltpu

D_MODEL = 1024
SEQ = 16384
DEPTH = 2
DEC_BATCH = 128
DEC_SEQ = 8
PAST_LEN = 16384
HEAD_DIM = 64
POOL_W = 768
POOL_WINDOWS = (2, 4, 8, 16)
POOL_GW = 192
POOL_PAD = 15
N_Q_HEADS = 12
N_KV_HEADS = 4
GROUP = 3
SWA_W = 768
KV_W = 256
WINDOW = 128
N_MEM = 256
MEM_HEADS = 4
MEM_W = 256
ROT_DIM = 16
ROPE_THETA = 500000.0
EPS = 1e-6

F32 = jnp.float32
BF16 = jnp.bfloat16
NEG_INF = float("-inf")
Q_SCALE = HEAD_DIM ** -0.5

LANES = 128
SUBLANES = 8
MXU_DIM = 256
CARRY_ROWS = 16
PROMPT_TILE = 256
SAMPLE_BLOCK = 8
VMEM_LIMIT = 56 * 1024 * 1024


def _dot(a, b):
    return jnp.dot(a, b, preferred_element_type=F32)


def _dot_nt(a, b):
    return lax.dot_general(a, b, (((1,), (1,)), ((), ())), preferred_element_type=F32)


def _rms_unit(x):
    return x * lax.rsqrt(jnp.mean(x * x, axis=-1, keepdims=True) + EPS)


def _silu(g):
    return g / (1.0 + jnp.exp(-g))


def _head_rms(y, gain, bseg):
    parts = []
    for c in range(y.shape[1] // MXU_DIM):
        yc = y[:, c * MXU_DIM:(c + 1) * MXU_DIM]
        ms = _dot((yc * yc).astype(BF16), bseg)
        parts.append(yc * lax.rsqrt(ms + EPS) * gain[:, c * MXU_DIM:(c + 1) * MXU_DIM])
    return parts[0] if len(parts) == 1 else jnp.concatenate(parts, axis=1)


def _rope(y, cos_t, sin_t, first8):
    parts = []
    for c in range(y.shape[1] // LANES):
        yc = y[:, c * LANES:(c + 1) * LANES]
        partner = jnp.where(first8, pltpu.roll(yc, LANES - 8, 1), pltpu.roll(yc, 8, 1))
        parts.append(yc * cos_t + partner * sin_t)
    return parts[0] if len(parts) == 1 else jnp.concatenate(parts, axis=1)


def _pool_window_sums(u_hist):
    def back(a, k):
        return pltpu.roll(a, k, 0)

    lane = lax.broadcasted_iota(jnp.int32, (1, LANES), 1)
    s2 = u_hist + back(u_hist, 1)
    t = s2[:, LANES:]
    s4 = t + back(t, 2)
    t = s4[:, 2 * LANES:]
    s8 = t + back(t, 4)
    t = s8[:, LANES:]
    s16 = t + back(t, 8)
    tiles = [
        s2[:, :LANES],
        jnp.where(lane < 64, s2[:, LANES:2 * LANES], s4[:, :LANES]),
        s4[:, LANES:2 * LANES],
        s8[:, :LANES],
        jnp.where(lane < 64, s8[:, LANES:2 * LANES], s16[:, :LANES]),
        s16[:, LANES:],
    ]
    return jnp.concatenate(tiles, axis=1)


def _head_masks():
    lane = lax.broadcasted_iota(jnp.int32, (1, MXU_DIM), 1)
    return [(lane // HEAD_DIM) == j for j in range(MXU_DIM // HEAD_DIM)]


def _first8_mask():
    lane = lax.broadcasted_iota(jnp.int32, (1, LANES), 1)
    return (lane % HEAD_DIM) < (ROT_DIM // 2)


def _stack_heads(q, hmask):
    return jnp.concatenate([jnp.where(m, q, 0.0) for m in hmask], axis=0).astype(BF16)


def _unstack_heads(o, hmask):
    m = o.shape[0] // len(hmask)
    acc = None
    for h, mask in enumerate(hmask):
        oh = jnp.where(mask, o[h * m:(h + 1) * m, :], 0.0)
        acc = oh if acc is None else acc + oh
    return acc


def _sink_columns(sinks_ref, rows_per_head):
    nh = MXU_DIM // HEAD_DIM
    hrow = lax.broadcasted_iota(jnp.int32, (nh * rows_per_head, 1), 0) // rows_per_head
    cols = []
    for c in range(SWA_W // MXU_DIM):
        col = jnp.zeros((nh * rows_per_head, 1), F32)
        for jj in range(nh):
            col = jnp.where(hrow == jj, sinks_ref[c * nh + jj], col)
        cols.append(col)
    return cols


def _softmax_pv(s, v, sink):
    m = jnp.max(s, axis=-1, keepdims=True)
    if sink is not None:
        m = jnp.maximum(m, sink)
    p = jnp.exp(s - m)
    den = jnp.sum(p, axis=-1, keepdims=True)
    if sink is not None:
        den = den + jnp.exp(sink - m)
    return _dot(p.astype(BF16), v) * (1.0 / den)


def _memkv_kernel(mem_ref, norm_ref, w_ref, kgain_ref, bseg_ref, mk_ref, mv_ref, mkb_ref, mvb_ref):
    xn = (_rms_unit(mem_ref[...]) * norm_ref[0]).astype(BF16)
    kv = _dot(xn, w_ref[0])
    k = _head_rms(kv[:, :MEM_W], kgain_ref[0], bseg_ref[...])
    v = kv[:, MEM_W:]
    mk_ref[0] = k
    mv_ref[0] = v
    mkb_ref[0] = k.astype(BF16)
    mvb_ref[0] = v.astype(BF16)


def _memkv_call(mem, mem_norm, w_mem_kv, mem_kgain, bseg):
    const2 = lambda l: (0, 0)
    per_layer3 = lambda l: (l, 0, 0)
    out_f = jax.ShapeDtypeStruct((DEPTH, N_MEM, MEM_W), F32)
    out_b = jax.ShapeDtypeStruct((DEPTH, N_MEM, MEM_W), BF16)
    return pl.pallas_call(
        _memkv_kernel,
        grid=(DEPTH,),
        in_specs=[
            pl.BlockSpec((N_MEM, D_MODEL), const2),
            pl.BlockSpec((1, 1, D_MODEL), per_layer3),
            pl.BlockSpec((1, D_MODEL, 2 * MEM_W), per_layer3),
            pl.BlockSpec((1, 1, MEM_W), per_layer3),
            pl.BlockSpec((MXU_DIM, MXU_DIM), const2),
        ],
        out_specs=[pl.BlockSpec((1, N_MEM, MEM_W), per_layer3)] * 4,
        out_shape=[out_f, out_f, out_b, out_b],
        compiler_params=pltpu.CompilerParams(dimension_semantics=("arbitrary",)),
        name="memkv",
    )(mem, mem_norm, w_mem_kv, mem_kgain, bseg)


def _mem_attn_rows(qm, gm, gain, mk, mv, bseg, hmask):
    q = _head_rms(qm, gain, bseg) * Q_SCALE
    o = _softmax_pv(_dot_nt(_stack_heads(q, hmask), mk), mv, None)
    return _unstack_heads(o, hmask) * _silu(gm)


def _swa_block(qb, kc, vc, bias, sink_cols, hmask):
    nh = MXU_DIM // HEAD_DIM
    outs = []
    for c in range(SWA_W // MXU_DIM):
        sl = slice(c * MXU_DIM, (c + 1) * MXU_DIM)
        s = _dot_nt(_stack_heads(qb[:, sl], hmask), kc[:, sl])
        s = (s.reshape(nh, WINDOW, 2 * WINDOW) + bias[None]).reshape(nh * WINDOW, 2 * WINDOW)
        outs.append(_unstack_heads(_softmax_pv(s, vc[:, sl], sink_cols[c]), hmask))
    return jnp.concatenate(outs, axis=1)


def _prompt_kernel(sinks_ref, x_ref, cos_ref, sin_ref,
                   norm_a_ref, w_in_a_ref, wbd_ref, pscale_ref, wcol_ref, w_out_a_ref,
                   kv_norm_ref, w_kv_ref, kgain_ref,
                   norm_b_ref, w_in_b_ref, qgain_ref, w_out_b_ref,
                   mqg_ref, mk_ref, mv_ref, bseg_ref, srep_ref,
                   y_ref, pool_ref, ko_ref, vo_ref,
                   ucarry, kprev, vprev):
    tq = PROMPT_TILE
    i = pl.program_id(0)

    @pl.when(i == 0)
    def _():
        ucarry[0] = jnp.zeros((CARRY_ROWS, POOL_W), F32)
        kprev[0] = jnp.zeros((WINDOW, SWA_W), BF16)
        vprev[0] = jnp.zeros((WINDOW, SWA_W), BF16)

    rd = i % 2
    wr = (i + 1) % 2

    hmask = _head_masks()
    first8 = _first8_mask()
    bseg = bseg_ref[...]
    cos_t = cos_ref[...]
    sin_t = sin_ref[...]
    x = x_ref[...]

    xn = (_rms_unit(x) * norm_a_ref[...]).astype(BF16)
    u = _dot(xn, w_in_a_ref[:, 0:POOL_W])
    gp = _dot(xn, w_in_a_ref[:, POOL_W:2 * POOL_W])
    qm = _dot(xn, w_in_a_ref[:, 2 * POOL_W:2 * POOL_W + MEM_W])
    gm = _dot(xn, w_in_a_ref[:, 2 * POOL_W + MEM_W:])
    u_hist = jnp.concatenate([ucarry[rd], u], axis=0)
    win = _pool_window_sums(u_hist)[CARRY_ROWS:]
    pos = i * tq + lax.broadcasted_iota(jnp.int32, (tq, 1), 0)
    cnt = jnp.minimum((pos + 1).astype(F32), wcol_ref[...])
    d = win / cnt - u
    yp = _dot(d.astype(BF16), wbd_ref[...]) * pscale_ref[...] * _silu(gp)
    ym = _mem_attn_rows(qm, gm, mqg_ref[0:1, :], mk_ref[0], mv_ref[0], bseg, hmask)
    x1 = x + _dot(jnp.concatenate([yp, ym], axis=1).astype(BF16), w_out_a_ref[...])
    ucarry[wr] = u[tq - CARRY_ROWS:, :]

    r = _rms_unit(x1)
    kv = _dot((r * kv_norm_ref[...]).astype(BF16), w_kv_ref[...])
    k = _rope(_head_rms(kv[:, :KV_W], kgain_ref[...], bseg), cos_t, sin_t, first8)
    v = kv[:, KV_W:]
    srep = srep_ref[...]
    k_all = jnp.concatenate([kprev[rd], _dot(k.astype(BF16), srep).astype(BF16)], axis=0)
    v_all = jnp.concatenate([vprev[rd], _dot(v.astype(BF16), srep).astype(BF16)], axis=0)

    xb = (r * norm_b_ref[...]).astype(BF16)
    zq = _dot(xb, w_in_b_ref[:, 0:SWA_W])
    gq = _dot(xb, w_in_b_ref[:, SWA_W:2 * SWA_W])
    qm2 = _dot(xb, w_in_b_ref[:, 2 * SWA_W:2 * SWA_W + MEM_W])
    gm2 = _dot(xb, w_in_b_ref[:, 2 * SWA_W + MEM_W:])
    q = _rope(_head_rms(zq, qgain_ref[...], bseg), cos_t, sin_t, first8) * Q_SCALE

    qi = lax.broadcasted_iota(jnp.int32, (WINDOW, 2 * WINDOW), 0)
    ci = lax.broadcasted_iota(jnp.int32, (WINDOW, 2 * WINDOW), 1)
    band_bias = jnp.where(ci > qi, jnp.where(ci <= qi + WINDOW, 0.0, NEG_INF), NEG_INF)
    sink_cols = _sink_columns(sinks_ref, WINDOW)
    ys_blocks = []
    for b in range(tq // WINDOW):
        bias = band_bias
        if b == 0:
            key_pos = ci + (i * tq - WINDOW)
            bias = jnp.where(key_pos >= 0, band_bias, NEG_INF)
        qb = q[b * WINDOW:(b + 1) * WINDOW, :]
        kc = k_all[b * WINDOW:(b + 2) * WINDOW, :]
        vc = v_all[b * WINDOW:(b + 2) * WINDOW, :]
        ys_blocks.append(_swa_block(qb, kc, vc, bias, sink_cols, hmask))
    ys = jnp.concatenate(ys_blocks, axis=0) * _silu(gq)
    ym2 = _mem_attn_rows(qm2, gm2, mqg_ref[1:2, :], mk_ref[1], mv_ref[1], bseg, hmask)
    y_ref[...] = x1 + _dot(jnp.concatenate([ys, ym2], axis=1).astype(BF16), w_out_b_ref[...])

    kprev[wr] = k_all[tq:, :]
    vprev[wr] = v_all[tq:, :]

    @pl.when(i == pl.num_programs(0) - 1)
    def _():
        ko_ref[...] = k[tq - WINDOW:, :]
        vo_ref[...] = v[tq - WINDOW:, :]
        pool_ref[...] = pltpu.roll(u[tq - CARRY_ROWS:, :], CARRY_ROWS - 1, 0)[0:POOL_PAD, :]


def _const_spec(shape):
    nd = len(shape)
    return pl.BlockSpec(shape, lambda i: (0,) * nd, pipeline_mode=pl.Buffered(1))


def _prompt_call(sinks, x, cos_t, sin_t, wts, mqg, mk, mv, bseg, srep):
    tq = PROMPT_TILE
    n = x.shape[0]
    row_spec = lambda w: pl.BlockSpec((tq, w), lambda i: (i, 0))
    in_specs = [pl.BlockSpec(memory_space=pltpu.SMEM),
                row_spec(D_MODEL), row_spec(LANES), row_spec(LANES)]
    in_specs += [_const_spec(w.shape) for w in wts]
    in_specs += [_const_spec(a.shape) for a in (mqg, mk, mv, bseg, srep)]
    out_shape = [jax.ShapeDtypeStruct((n, D_MODEL), F32),
                 jax.ShapeDtypeStruct((POOL_PAD, POOL_W), F32),
                 jax.ShapeDtypeStruct((WINDOW, KV_W), F32),
                 jax.ShapeDtypeStruct((WINDOW, KV_W), F32)]
    out_specs = [row_spec(D_MODEL),
                 pl.BlockSpec((POOL_PAD, POOL_W), lambda i: (0, 0)),
                 pl.BlockSpec((WINDOW, KV_W), lambda i: (0, 0)),
                 pl.BlockSpec((WINDOW, KV_W), lambda i: (0, 0))]
    return pl.pallas_call(
        _prompt_kernel,
        grid=(n // tq,),
        in_specs=in_specs,
        out_specs=out_specs,
        out_shape=out_shape,
        scratch_shapes=[pltpu.VMEM((2, CARRY_ROWS, POOL_W), F32),
                        pltpu.VMEM((2, WINDOW, SWA_W), BF16),
                        pltpu.VMEM((2, WINDOW, SWA_W), BF16)],
        compiler_params=pltpu.CompilerParams(dimension_semantics=("arbitrary",),
                                             vmem_limit_bytes=VMEM_LIMIT),
        name="prompt",
    )(sinks, x, cos_t, sin_t, *wts, mqg, mk, mv, bseg, srep)


def _seq_attn(q8, kmat, vmat, bias, sink_col, hmask):
    s = _dot_nt(_stack_heads(q8, hmask), kmat)
    if bias is not None:
        s = s + bias
    return _unstack_heads(_softmax_pv(s, vmat, sink_col), hmask)


def _sample_kernel(sinks_ref, x_ref, pref_ref, ck_ref, cv_ref, cmk_ref, cmv_ref, cos_ref, sin_ref,
                   norm_a_ref, w_in_a_ref, wbd_ref, pscale_ref, wcol_ref, w_out_a_ref,
                   kv_norm_ref, w_kv_ref, kgain_ref,
                   norm_b_ref, w_in_b_ref, qgain_ref, w_out_b_ref,
                   mqg_ref, bseg_ref, srep_ref,
                   y_ref, pool_ref, ko_ref, vo_ref,
                   q_scr, att_scr, qm_scr, ym_scr, krep, vrep):
    sb = SAMPLE_BLOCK
    m = sb * DEC_SEQ
    nkeys = WINDOW + DEC_SEQ
    hmask = _head_masks()
    first8 = _first8_mask()
    bseg = bseg_ref[...]
    cos_t = cos_ref[...]
    sin_t = sin_ref[...]

    @pl.when(pl.program_id(0) == 0)
    def _():
        krep[:, nkeys:, :] = jnp.zeros((sb, 2 * WINDOW - nkeys, SWA_W), BF16)
        vrep[:, nkeys:, :] = jnp.zeros((sb, 2 * WINDOW - nkeys, SWA_W), BF16)

    x = x_ref[...].reshape(m, D_MODEL)

    xn = (_rms_unit(x) * norm_a_ref[...]).astype(BF16)
    u = _dot(xn, w_in_a_ref[:, 0:POOL_W])
    gp = _dot(xn, w_in_a_ref[:, POOL_W:2 * POOL_W])
    qm = _dot(xn, w_in_a_ref[:, 2 * POOL_W:2 * POOL_W + MEM_W])
    gm = _dot(xn, w_in_a_ref[:, 2 * POOL_W + MEM_W:])

    hist_rows = CARRY_ROWS + DEC_SEQ
    u_hist = jnp.concatenate([pref_ref[...], u.reshape(sb, DEC_SEQ, POOL_W)], axis=1)
    u_hist = u_hist.reshape(sb * hist_rows, POOL_W)
    win = _pool_window_sums(u_hist).reshape(sb, hist_rows, POOL_W)[:, CARRY_ROWS:, :].reshape(m, POOL_W)
    tok = lax.broadcasted_iota(jnp.int32, (sb, DEC_SEQ, 1), 1).reshape(m, 1)
    cnt = jnp.minimum((PAST_LEN + tok + 1).astype(F32), wcol_ref[...])
    d = win / cnt - u
    yp = _dot(d.astype(BF16), wbd_ref[...]) * pscale_ref[...] * _silu(gp)
    shifted = pltpu.roll(u_hist, sb * hist_rows - 1, 0).reshape(sb, hist_rows, POOL_W)
    pool_ref[...] = shifted[:, SUBLANES:, :][:, 0:POOL_PAD, :]

    qm_scr[0] = (_head_rms(qm, mqg_ref[0:1, :], bseg) * Q_SCALE).reshape(sb, DEC_SEQ, MEM_W)

    def mem_body(layer, b, carry):
        mk = cmk_ref[layer, b].astype(BF16)
        mv = cmv_ref[layer, b].astype(BF16)
        ym_scr[layer, b] = _seq_attn(qm_scr[layer, b], mk, mv, None, None, hmask)
        return carry

    lax.fori_loop(0, sb, functools.partial(mem_body, 0), 0)
    ym = ym_scr[0].reshape(m, MEM_W) * _silu(gm)
    x1 = x + _dot(jnp.concatenate([yp, ym], axis=1).astype(BF16), w_out_a_ref[...])

    r = _rms_unit(x1)
    kv = _dot((r * kv_norm_ref[...]).astype(BF16), w_kv_ref[...])
    k = _rope(_head_rms(kv[:, :KV_W], kgain_ref[...], bseg), cos_t, sin_t, first8)
    v = kv[:, KV_W:]
    ck = ck_ref[...]
    cv = cv_ref[...]
    keep = WINDOW - DEC_SEQ
    ko_ref[:, 0:keep, :] = ck[:, DEC_SEQ:, :]
    ko_ref[:, keep:, :] = k.reshape(sb, DEC_SEQ, KV_W)
    vo_ref[:, 0:keep, :] = cv[:, DEC_SEQ:, :]
    vo_ref[:, keep:, :] = v.reshape(sb, DEC_SEQ, KV_W)
    srep = srep_ref[...]
    krep[:, 0:WINDOW, :] = _dot(ck.reshape(sb * WINDOW, KV_W).astype(BF16), srep).astype(BF16).reshape(
        sb, WINDOW, SWA_W)
    vrep[:, 0:WINDOW, :] = _dot(cv.reshape(sb * WINDOW, KV_W).astype(BF16), srep).astype(BF16).reshape(
        sb, WINDOW, SWA_W)
    krep[:, WINDOW:nkeys, :] = _dot(k.astype(BF16), srep).astype(BF16).reshape(sb, DEC_SEQ, SWA_W)
    vrep[:, WINDOW:nkeys, :] = _dot(v.astype(BF16), srep).astype(BF16).reshape(sb, DEC_SEQ, SWA_W)

    xb = (r * norm_b_ref[...]).astype(BF16)
    zq = _dot(xb, w_in_b_ref[:, 0:SWA_W])
    gq = _dot(xb, w_in_b_ref[:, SWA_W:2 * SWA_W])
    qm2 = _dot(xb, w_in_b_ref[:, 2 * SWA_W:2 * SWA_W + MEM_W])
    gm2 = _dot(xb, w_in_b_ref[:, 2 * SWA_W + MEM_W:])
    q = _rope(_head_rms(zq, qgain_ref[...], bseg), cos_t, sin_t, first8) * Q_SCALE
    q_scr[...] = q.reshape(sb, DEC_SEQ, SWA_W)
    qm_scr[1] = (_head_rms(qm2, mqg_ref[1:2, :], bseg) * Q_SCALE).reshape(sb, DEC_SEQ, MEM_W)

    nh = MXU_DIM // HEAD_DIM
    rows = nh * DEC_SEQ
    ti = lax.broadcasted_iota(jnp.int32, (rows, 2 * WINDOW), 0) % DEC_SEQ
    ci = lax.broadcasted_iota(jnp.int32, (rows, 2 * WINDOW), 1)
    bias = jnp.where((ci > ti) & (ci <= ti + WINDOW), 0.0, NEG_INF)
    sink_cols = _sink_columns(sinks_ref, DEC_SEQ)

    def swa_body(b, carry):
        for c in range(SWA_W // MXU_DIM):
            sl = slice(c * MXU_DIM, (c + 1) * MXU_DIM)
            att_scr[b, :, sl] = _seq_attn(q_scr[b, :, sl], krep[b, :, sl], vrep[b, :, sl], bias,
                                          sink_cols[c], hmask)
        return carry

    lax.fori_loop(0, sb, swa_body, 0)
    lax.fori_loop(0, sb, functools.partial(mem_body, 1), 0)
    ys = att_scr[...].reshape(m, SWA_W) * _silu(gq)
    ym2 = ym_scr[1].reshape(m, MEM_W) * _silu(gm2)
    y = x1 + _dot(jnp.concatenate([ys, ym2], axis=1).astype(BF16), w_out_b_ref[...])
    y_ref[...] = y.reshape(sb, DEC_SEQ, D_MODEL)


def _sample_call(sinks, x, pref, ck, cv, cmk, cmv, cos_t, sin_t, wts, mqg, bseg, srep):
    sb = SAMPLE_BLOCK
    nb = x.shape[0]
    seq3 = lambda a, b_: pl.BlockSpec((sb, a, b_), lambda i: (i, 0, 0))
    in_specs = [pl.BlockSpec(memory_space=pltpu.SMEM),
                seq3(DEC_SEQ, D_MODEL), seq3(CARRY_ROWS, POOL_W), seq3(WINDOW, KV_W), seq3(WINDOW, KV_W),
                pl.BlockSpec((DEPTH, sb, N_MEM, MEM_W), lambda i: (0, i, 0, 0)),
                pl.BlockSpec((DEPTH, sb, N_MEM, MEM_W), lambda i: (0, i, 0, 0)),
                _const_spec(cos_t.shape), _const_spec(sin_t.shape)]
    in_specs += [_const_spec(w.shape) for w in wts]
    in_specs += [_const_spec(a.shape) for a in (mqg, bseg, srep)]
    out_shape = [jax.ShapeDtypeStruct((nb, DEC_SEQ, D_MODEL), F32),
                 jax.ShapeDtypeStruct((nb, POOL_PAD, POOL_W), F32),
                 jax.ShapeDtypeStruct((nb, WINDOW, KV_W), F32),
                 jax.ShapeDtypeStruct((nb, WINDOW, KV_W), F32)]
    out_specs = [seq3(DEC_SEQ, D_MODEL), seq3(POOL_PAD, POOL_W), seq3(WINDOW, KV_W), seq3(WINDOW, KV_W)]
    return pl.pallas_call(
        _sample_kernel,
        grid=(nb // sb,),
        in_specs=in_specs,
        out_specs=out_specs,
        out_shape=out_shape,
        scratch_shapes=[pltpu.VMEM((sb, DEC_SEQ, SWA_W), F32),
                        pltpu.VMEM((sb, DEC_SEQ, SWA_W), F32),
                        pltpu.VMEM((DEPTH, sb, DEC_SEQ, MEM_W), F32),
                        pltpu.VMEM((DEPTH, sb, DEC_SEQ, MEM_W), F32),
                        pltpu.VMEM((sb, 2 * WINDOW, SWA_W), BF16),
                        pltpu.VMEM((sb, 2 * WINDOW, SWA_W), BF16)],
        compiler_params=pltpu.CompilerParams(dimension_semantics=("arbitrary",),
                                             vmem_limit_bytes=VMEM_LIMIT),
        name="sample",
    )(sinks, x, pref, ck, cv, cmk, cmv, cos_t, sin_t, *wts, mqg, bseg, srep)


def _rope_tables(pos):
    half = ROT_DIM // 2
    inv = ROPE_THETA ** (-jnp.arange(half, dtype=F32) * 2.0 / ROT_DIM)
    ang = pos.astype(F32)[:, None] * inv[None, :]
    cos, sin = jnp.cos(ang), jnp.sin(ang)
    t = pos.shape[0]
    rest = HEAD_DIM - ROT_DIM
    cos64 = jnp.concatenate([cos, cos, jnp.ones((t, rest), F32)], axis=1)
    sin64 = jnp.concatenate([-sin, sin, jnp.zeros((t, rest), F32)], axis=1)
    reps = LANES // HEAD_DIM
    return jnp.tile(cos64, (1, reps)), jnp.tile(sin64, (1, reps))


def _head_tile(g, width):
    return jnp.tile(g.astype(F32), width // HEAD_DIM).reshape(1, width)


def kernel(x_prompt, x_sample, state_pool, cache_swa_k, cache_swa_v, cache_mem_k, cache_mem_v, mem_prompt,
           norm_a, w_in_a, pool_mix_w, pool_scale, w_out_a, kv_norm, w_kv, k_norm,
           norm_b, w_in_b, q_norm, sinks, w_out_b, mem_norm, w_mem_kv, mem_q_norm, mem_k_norm):
    seg = jnp.arange(MXU_DIM) // HEAD_DIM
    bseg = ((seg[:, None] == seg[None, :]).astype(F32) / HEAD_DIM).astype(BF16)
    src = (jnp.arange(SWA_W) // HEAD_DIM // GROUP) * HEAD_DIM + jnp.arange(SWA_W) % HEAD_DIM
    srep = (jnp.arange(KV_W)[:, None] == src[None, :]).astype(BF16)
    wcol = jnp.repeat(jnp.asarray(POOL_WINDOWS, F32), POOL_GW).reshape(1, POOL_W)

    wbd = jax.scipy.linalg.block_diag(*[pool_mix_w[0, g] for g in range(len(POOL_WINDOWS))]).astype(BF16)
    wts = (norm_a[0].reshape(1, D_MODEL), w_in_a[0].astype(BF16), wbd, pool_scale[0].reshape(1, POOL_W), wcol,
           w_out_a[0].astype(BF16),
           kv_norm.reshape(1, D_MODEL), w_kv.astype(BF16), _head_tile(k_norm, KV_W),
           norm_b[0].reshape(1, D_MODEL), w_in_b[0].astype(BF16), _head_tile(q_norm[0], SWA_W),
           w_out_b[0].astype(BF16))
    mqg = jnp.concatenate([_head_tile(mem_q_norm[l], MEM_W) for l in range(DEPTH)], axis=0)
    mkg = jnp.stack([_head_tile(mem_k_norm[l], MEM_W) for l in range(DEPTH)])
    sinks1 = sinks[0].astype(F32)

    mk, mv, mk_b, mv_b = _memkv_call(mem_prompt[0], mem_norm.reshape(DEPTH, 1, D_MODEL),
                                     w_mem_kv.astype(BF16), mkg, bseg)

    cos_p, sin_p = _rope_tables(jnp.arange(SEQ))
    y_p, pool_p, k_p, v_p = _prompt_call(sinks1, x_prompt[0], cos_p, sin_p, wts, mqg, mk_b, mv_b, bseg, srep)

    cos_s, sin_s = _rope_tables(PAST_LEN + jnp.arange(DEC_SEQ))
    cos_s = jnp.tile(cos_s, (SAMPLE_BLOCK, 1))
    sin_s = jnp.tile(sin_s, (SAMPLE_BLOCK, 1))
    pref = jnp.pad(state_pool[0], ((0, 0), (CARRY_ROWS - POOL_PAD, 0), (0, 0)))
    y_s, pool_s, k_s, v_s = _sample_call(
        sinks1, x_sample, pref,
        cache_swa_k.reshape(DEC_BATCH, WINDOW, KV_W), cache_swa_v.reshape(DEC_BATCH, WINDOW, KV_W),
        cache_mem_k.reshape(DEPTH, DEC_BATCH, N_MEM, MEM_W), cache_mem_v.reshape(DEPTH, DEC_BATCH, N_MEM, MEM_W),
        cos_s, sin_s, wts, mqg, bseg, srep)

    kv4 = (N_KV_HEADS, HEAD_DIM)
    return (y_p[None], y_s, pool_p[None, None], pool_s[None],
            k_p.reshape(1, WINDOW, *kv4), v_p.reshape(1, WINDOW, *kv4),
            k_s.reshape(DEC_BATCH, WINDOW, *kv4), v_s.reshape(DEC_BATCH, WINDOW, *kv4),
            mk.reshape(DEPTH, 1, N_MEM, MEM_HEADS, HEAD_DIM), mv.reshape(DEPTH, 1, N_MEM, MEM_HEADS, HEAD_DIM))
```

```python
import functools

import jax
import jax.numpy as jnp
from jax import lax
from jax.experimental import pallas as pl
from jax.experimental.pallas import tpu as pltpu

D_MODEL = 1024
SEQ = 16384
DEPTH = 2
DEC_BATCH = 128
DEC_SEQ = 8
PAST_LEN = 16384
HEAD_DIM = 64
POOL_W = 768
POOL_WINDOWS = (2, 4, 8, 16)
POOL_GW = 192
POOL_PAD = 15
N_Q_HEADS = 12
N_KV_HEADS = 4
GROUP = 3
SWA_W = 768
KV_W = 256
WINDOW = 128
N_MEM = 256
MEM_HEADS = 4
MEM_W = 256
ROT_DIM = 16
ROPE_THETA = 500000.0
EPS = 1e-6

F32 = jnp.float32
BF16 = jnp.bfloat16
NEG_INF = float("-inf")
Q_SCALE = HEAD_DIM ** -0.5

LANES = 128
SUBLANES = 8
MXU_DIM = 256
HEADS_PER_CHUNK = MXU_DIM // HEAD_DIM
CARRY_ROWS = 16
PROMPT_TILE = 256
SAMPLE_BLOCK = 8
VMEM_LIMIT = 56 * 1024 * 1024


def _dot(a, b):
    return jnp.dot(a, b, preferred_element_type=F32)


def _dot_nt(a, b):
    return lax.dot_general(a, b, (((1,), (1,)), ((), ())), preferred_element_type=F32)


def _rms_unit(x):
    return x * lax.rsqrt(jnp.mean(x * x, axis=-1, keepdims=True) + EPS)


def _silu(g):
    return g / (1.0 + jnp.exp(-g))


def _head_rms(y, gain, bseg):
    parts = []
    for c in range(y.shape[1] // MXU_DIM):
        yc = y[:, c * MXU_DIM:(c + 1) * MXU_DIM]
        ms = _dot((yc * yc).astype(BF16), bseg)
        parts.append(yc * lax.rsqrt(ms + EPS) * gain[:, c * MXU_DIM:(c + 1) * MXU_DIM])
    return parts[0] if len(parts) == 1 else jnp.concatenate(parts, axis=1)


def _rope(y, cos_t, sin_t, first8):
    parts = []
    for c in range(y.shape[1] // LANES):
        yc = y[:, c * LANES:(c + 1) * LANES]
        partner = jnp.where(first8, pltpu.roll(yc, LANES - 8, 1), pltpu.roll(yc, 8, 1))
        parts.append(yc * cos_t + partner * sin_t)
    return parts[0] if len(parts) == 1 else jnp.concatenate(parts, axis=1)


def _pool_window_sums(u_hist):
    def back(a, k):
        return pltpu.roll(a, k, 0)

    lane = lax.broadcasted_iota(jnp.int32, (1, LANES), 1)
    s2 = u_hist + back(u_hist, 1)
    t = s2[:, LANES:]
    s4 = t + back(t, 2)
    t = s4[:, 2 * LANES:]
    s8 = t + back(t, 4)
    t = s8[:, LANES:]
    s16 = t + back(t, 8)
    tiles = [
        s2[:, :LANES],
        jnp.where(lane < 64, s2[:, LANES:2 * LANES], s4[:, :LANES]),
        s4[:, LANES:2 * LANES],
        s8[:, :LANES],
        jnp.where(lane < 64, s8[:, LANES:2 * LANES], s16[:, :LANES]),
        s16[:, LANES:],
    ]
    return jnp.concatenate(tiles, axis=1)


def _head_masks():
    lane = lax.broadcasted_iota(jnp.int32, (1, MXU_DIM), 1)
    return [(lane // HEAD_DIM) == j for j in range(HEADS_PER_CHUNK)]


def _first8_mask():
    lane = lax.broadcasted_iota(jnp.int32, (1, LANES), 1)
    return (lane % HEAD_DIM) < (ROT_DIM // 2)


def _stack_heads(q, hmask):
    return jnp.concatenate([jnp.where(m, q, 0.0) for m in hmask], axis=0).astype(BF16)


def _unstack_heads(o, hmask):
    m = o.shape[0] // len(hmask)
    acc = None
    for h, mask in enumerate(hmask):
        oh = jnp.where(mask, o[h * m:(h + 1) * m, :], 0.0)
        acc = oh if acc is None else acc + oh
    return acc


def _sink_columns(sinks_ref, rows_per_head):
    hrow = lax.broadcasted_iota(jnp.int32, (HEADS_PER_CHUNK * rows_per_head, 1), 0) // rows_per_head
    cols = []
    for c in range(SWA_W // MXU_DIM):
        col = jnp.zeros((HEADS_PER_CHUNK * rows_per_head, 1), F32)
        for jj in range(HEADS_PER_CHUNK):
            col = jnp.where(hrow == jj, sinks_ref[c * HEADS_PER_CHUNK + jj], col)
        cols.append(col)
    return cols


def _softmax_weights(s, sink):
    m = jnp.max(s, axis=-1, keepdims=True)
    if sink is not None:
        m = jnp.maximum(m, sink)
    p = jnp.exp(s - m)
    den = jnp.sum(p, axis=-1, keepdims=True)
    if sink is not None:
        den = den + jnp.exp(sink - m)
    return p.astype(BF16), 1.0 / den


def _mem_attn(q, k_t, v_t, hmask):
    p, rden = _softmax_weights(_dot(_stack_heads(q, hmask), k_t), None)
    return _unstack_heads(_dot_nt(p, v_t) * rden, hmask)


def _memkv_kernel(mem_ref, norm_ref, wk_ref, wv_ref, kgain_ref, mk_ref, mv_ref, mkb_ref, mvb_ref):
    xn = (_rms_unit(mem_ref[...]) * norm_ref[0]).astype(BF16)
    k_t = _dot_nt(wk_ref[0], xn)
    v_t = _dot_nt(wv_ref[0], xn)
    k3 = k_t.reshape(MEM_HEADS, HEAD_DIM, N_MEM)
    ms = jnp.mean(k3 * k3, axis=1, keepdims=True)
    k_t = (k3 * lax.rsqrt(ms + EPS)).reshape(MEM_W, N_MEM) * kgain_ref[0]
    mk_ref[0] = k_t
    mv_ref[0] = v_t
    mkb_ref[0] = k_t.astype(BF16)
    mvb_ref[0] = v_t.astype(BF16)


def _memkv_call(mem, mem_norm, wk_t, wv_t, kgain_col):
    const2 = lambda l: (0, 0)
    per_layer3 = lambda l: (l, 0, 0)
    out_f = jax.ShapeDtypeStruct((DEPTH, MEM_W, N_MEM), F32)
    out_b = jax.ShapeDtypeStruct((DEPTH, MEM_W, N_MEM), BF16)
    return pl.pallas_call(
        _memkv_kernel,
        grid=(DEPTH,),
        in_specs=[
            pl.BlockSpec((N_MEM, D_MODEL), const2),
            pl.BlockSpec((1, 1, D_MODEL), per_layer3),
            pl.BlockSpec((1, MEM_W, D_MODEL), per_layer3),
            pl.BlockSpec((1, MEM_W, D_MODEL), per_layer3),
            pl.BlockSpec((1, MEM_W, 1), per_layer3),
        ],
        out_specs=[pl.BlockSpec((1, MEM_W, N_MEM), per_layer3)] * 4,
        out_shape=[out_f, out_f, out_b, out_b],
        compiler_params=pltpu.CompilerParams(dimension_semantics=("arbitrary",)),
        name="memkv",
    )(mem, mem_norm, wk_t, wv_t, kgain_col)


def _swa_block(qb, kc, vc, bias, sink_cols, hmask):
    outs = []
    for g in range(GROUP):
        s = _dot_nt(_stack_heads(qb[:, g * MXU_DIM:(g + 1) * MXU_DIM], hmask), kc)
        s = (s.reshape(HEADS_PER_CHUNK, WINDOW, 2 * WINDOW) + bias[None]).reshape(HEADS_PER_CHUNK * WINDOW, 2 * WINDOW)
        p, rden = _softmax_weights(s, sink_cols[g])
        outs.append(_unstack_heads(_dot(p, vc) * rden, hmask))
    return jnp.concatenate(outs, axis=1)


def _prompt_kernel(sinks_ref, x_ref, rope_base_ref, rope_res_ref,
                   norm_a_ref, w_in_a_ref, wbd_ref, pscale_ref, wcol_ref, w_out_a_ref,
                   kv_norm_ref, w_kv_ref, kgain_ref,
                   norm_b_ref, w_in_b_ref, qgain_ref, w_out_b_ref,
                   mqg_ref, mk_ref, mv_ref, bseg_ref,
                   y_ref, pool_ref, ko_ref, vo_ref,
                   ucarry, kprev, vprev):
    tq = PROMPT_TILE
    i = pl.program_id(0)

    @pl.when(i == 0)
    def _():
        ucarry[0] = jnp.zeros((CARRY_ROWS, POOL_W), F32)
        kprev[0] = jnp.zeros((WINDOW, KV_W), BF16)
        vprev[0] = jnp.zeros((WINDOW, KV_W), BF16)

    rd = i % 2
    wr = (i + 1) % 2

    hmask = _head_masks()
    first8 = _first8_mask()
    bseg = bseg_ref[...]
    x = x_ref[...]

    base = rope_base_ref[i]
    cb, sb = base[0:1, :], base[1:2, :]
    cr, sr = rope_res_ref[0], rope_res_ref[1]
    cos_t = cb * cr - sb * sr
    sin_t = sb * cr + cb * sr

    xn = (_rms_unit(x) * norm_a_ref[...]).astype(BF16)
    u = _dot(xn, w_in_a_ref[:, 0:POOL_W])
    gp = _dot(xn, w_in_a_ref[:, POOL_W:2 * POOL_W])
    qm = _dot(xn, w_in_a_ref[:, 2 * POOL_W:2 * POOL_W + MEM_W])
    gm = _dot(xn, w_in_a_ref[:, 2 * POOL_W + MEM_W:])
    u_hist = jnp.concatenate([ucarry[rd], u], axis=0)
    win = _pool_window_sums(u_hist)[CARRY_ROWS:]
    pos = i * tq + lax.broadcasted_iota(jnp.int32, (tq, 1), 0)
    cnt = jnp.minimum((pos + 1).astype(F32), wcol_ref[...])
    d = win / cnt - u
    yp = _dot(d.astype(BF16), wbd_ref[...]) * pscale_ref[...] * _silu(gp)
    qmn = _head_rms(qm, mqg_ref[0:1, :], bseg) * Q_SCALE
    ym = _mem_attn(qmn, mk_ref[0], mv_ref[0], hmask) * _silu(gm)
    x1 = x + _dot(jnp.concatenate([yp, ym], axis=1).astype(BF16), w_out_a_ref[...])
    ucarry[wr] = u[tq - CARRY_ROWS:, :]

    r = _rms_unit(x1)
    kv = _dot((r * kv_norm_ref[...]).astype(BF16), w_kv_ref[...])
    k = _rope(_head_rms(kv[:, :KV_W], kgain_ref[...], bseg), cos_t, sin_t, first8)
    v = kv[:, KV_W:]
    k_all = jnp.concatenate([kprev[rd], k.astype(BF16)], axis=0)
    v_all = jnp.concatenate([vprev[rd], v.astype(BF16)], axis=0)

    xb = (r * norm_b_ref[...]).astype(BF16)
    zq = _dot(xb, w_in_b_ref[:, 0:SWA_W])
    gq = _dot(xb, w_in_b_ref[:, SWA_W:2 * SWA_W])
    qm2 = _dot(xb, w_in_b_ref[:, 2 * SWA_W:2 * SWA_W + MEM_W])
    gm2 = _dot(xb, w_in_b_ref[:, 2 * SWA_W + MEM_W:])
    q = _rope(_head_rms(zq, qgain_ref[...], bseg), cos_t, sin_t, first8) * Q_SCALE

    qi = lax.broadcasted_iota(jnp.int32, (WINDOW, 2 * WINDOW), 0)
    ci = lax.broadcasted_iota(jnp.int32, (WINDOW, 2 * WINDOW), 1)
    band_bias = jnp.where(ci > qi, jnp.where(ci <= qi + WINDOW, 0.0, NEG_INF), NEG_INF)
    sink_cols = _sink_columns(sinks_ref, WINDOW)
    ys_blocks = []
    for b in range(tq // WINDOW):
        bias = band_bias
        if b == 0:
            key_pos = ci + (i * tq - WINDOW)
            bias = jnp.where(key_pos >= 0, band_bias, NEG_INF)
        qb = q[b * WINDOW:(b + 1) * WINDOW, :]
        kc = k_all[b * WINDOW:(b + 2) * WINDOW, :]
        vc = v_all[b * WINDOW:(b + 2) * WINDOW, :]
        ys_blocks.append(_swa_block(qb, kc, vc, bias, sink_cols, hmask))
    ys = jnp.concatenate(ys_blocks, axis=0) * _silu(gq)
    qmn2 = _head_rms(qm2, mqg_ref[1:2, :], bseg) * Q_SCALE
    ym2 = _mem_attn(qmn2, mk_ref[1], mv_ref[1], hmask) * _silu(gm2)
    y_ref[...] = x1 + _dot(jnp.concatenate([ys, ym2], axis=1).astype(BF16), w_out_b_ref[...])

    kprev[wr] = k_all[tq:, :]
    vprev[wr] = v_all[tq:, :]

    @pl.when(i == pl.num_programs(0) - 1)
    def _():
        ko_ref[...] = k[tq - WINDOW:, :].T
        vo_ref[...] = v[tq - WINDOW:, :].T
        pool_ref[...] = pltpu.roll(u[tq - CARRY_ROWS:, :], CARRY_ROWS - 1, 0)[0:POOL_PAD, :]


def _const_spec(shape):
    nd = len(shape)
    return pl.BlockSpec(shape, lambda i: (0,) * nd, pipeline_mode=pl.Buffered(1))


def _prompt_call(sinks, x, rope_base, rope_res, wts, mqg, mk, mv, bseg):
    tq = PROMPT_TILE
    n = x.shape[0]
    row_spec = lambda w: pl.BlockSpec((tq, w), lambda i: (i, 0))
    in_specs = [pl.BlockSpec(memory_space=pltpu.SMEM), row_spec(D_MODEL),
                _const_spec(rope_base.shape), _const_spec(rope_res.shape)]
    in_specs += [_const_spec(w.shape) for w in wts]
    in_specs += [_const_spec(a.shape) for a in (mqg, mk, mv, bseg)]
    out_shape = [jax.ShapeDtypeStruct((n, D_MODEL), F32),
                 jax.ShapeDtypeStruct((POOL_PAD, POOL_W), F32),
                 jax.ShapeDtypeStruct((KV_W, WINDOW), F32),
                 jax.ShapeDtypeStruct((KV_W, WINDOW), F32)]
    out_specs = [row_spec(D_MODEL),
                 pl.BlockSpec((POOL_PAD, POOL_W), lambda i: (0, 0)),
                 pl.BlockSpec((KV_W, WINDOW), lambda i: (0, 0)),
                 pl.BlockSpec((KV_W, WINDOW), lambda i: (0, 0))]
    return pl.pallas_call(
        _prompt_kernel,
        grid=(n // tq,),
        in_specs=in_specs,
        out_specs=out_specs,
        out_shape=out_shape,
        scratch_shapes=[pltpu.VMEM((2, CARRY_ROWS, POOL_W), F32),
                        pltpu.VMEM((2, WINDOW, KV_W), BF16),
                        pltpu.VMEM((2, WINDOW, KV_W), BF16)],
        compiler_params=pltpu.CompilerParams(dimension_semantics=("arbitrary",),
                                             vmem_limit_bytes=VMEM_LIMIT),
        name="prompt",
    )(sinks, x, rope_base, rope_res, *wts, mqg, mk, mv, bseg)


def _sample_kernel(sinks_ref, x_ref, pref_ref, ck_ref, cv_ref, cmk_ref, cmv_ref, cos_ref, sin_ref,
                   norm_a_ref, w_in_a_ref, wbd_ref, pscale_ref, wcol_ref, w_out_a_ref,
                   kv_norm_ref, w_kv_ref, kgain_ref,
                   norm_b_ref, w_in_b_ref, qgain_ref, w_out_b_ref,
                   mqg_ref, bseg_ref,
                   y_ref, pool_ref, ko_ref, vo_ref,
                   q_scr, att_scr, qm_scr, ym_scr, kall, vall):
    sb = SAMPLE_BLOCK
    m = sb * DEC_SEQ
    nkeys = WINDOW + DEC_SEQ
    hmask = _head_masks()
    first8 = _first8_mask()
    bseg = bseg_ref[...]
    cos_t = cos_ref[...]
    sin_t = sin_ref[...]

    @pl.when(pl.program_id(0) == 0)
    def _():
        kall[:, nkeys:, :] = jnp.zeros((sb, 2 * WINDOW - nkeys, KV_W), BF16)
        vall[:, nkeys:, :] = jnp.zeros((sb, 2 * WINDOW - nkeys, KV_W), BF16)

    x = x_ref[...].reshape(m, D_MODEL)

    xn = (_rms_unit(x) * norm_a_ref[...]).astype(BF16)
    u = _dot(xn, w_in_a_ref[:, 0:POOL_W])
    gp = _dot(xn, w_in_a_ref[:, POOL_W:2 * POOL_W])
    qm = _dot(xn, w_in_a_ref[:, 2 * POOL_W:2 * POOL_W + MEM_W])
    gm = _dot(xn, w_in_a_ref[:, 2 * POOL_W + MEM_W:])

    hist_rows = CARRY_ROWS + DEC_SEQ
    u_hist = jnp.concatenate([pref_ref[...], u.reshape(sb, DEC_SEQ, POOL_W)], axis=1)
    u_hist = u_hist.reshape(sb * hist_rows, POOL_W)
    win = _pool_window_sums(u_hist).reshape(sb, hist_rows, POOL_W)[:, CARRY_ROWS:, :].reshape(m, POOL_W)
    tok = lax.broadcasted_iota(jnp.int32, (sb, DEC_SEQ, 1), 1).reshape(m, 1)
    cnt = jnp.minimum((PAST_LEN + tok + 1).astype(F32), wcol_ref[...])
    d = win / cnt - u
    yp = _dot(d.astype(BF16), wbd_ref[...]) * pscale_ref[...] * _silu(gp)
    shifted = pltpu.roll(u_hist, sb * hist_rows - 1, 0).reshape(sb, hist_rows, POOL_W)
    pool_ref[...] = shifted[:, SUBLANES:, :][:, 0:POOL_PAD, :]

    qm_scr[0] = (_head_rms(qm, mqg_ref[0:1, :], bseg) * Q_SCALE).reshape(sb, DEC_SEQ, MEM_W)

    def mem_body(layer, b, carry):
        k_t = cmk_ref[layer, b].astype(BF16)
        v_t = cmv_ref[layer, b].astype(BF16)
        ym_scr[layer, b] = _mem_attn(qm_scr[layer, b], k_t, v_t, hmask)
        return carry

    lax.fori_loop(0, sb, functools.partial(mem_body, 0), 0)
    ym = ym_scr[0].reshape(m, MEM_W) * _silu(gm)
    x1 = x + _dot(jnp.concatenate([yp, ym], axis=1).astype(BF16), w_out_a_ref[...])

    r = _rms_unit(x1)
    kv = _dot((r * kv_norm_ref[...]).astype(BF16), w_kv_ref[...])
    k = _rope(_head_rms(kv[:, :KV_W], kgain_ref[...], bseg), cos_t, sin_t, first8)
    v = kv[:, KV_W:]
    ck = ck_ref[...]
    cv = cv_ref[...]
    keep = WINDOW - DEC_SEQ
    ko_ref[:, 0:keep, :] = ck[:, DEC_SEQ:, :]
    ko_ref[:, keep:, :] = k.reshape(sb, DEC_SEQ, KV_W)
    vo_ref[:, 0:keep, :] = cv[:, DEC_SEQ:, :]
    vo_ref[:, keep:, :] = v.reshape(sb, DEC_SEQ, KV_W)
    kall[:, 0:WINDOW, :] = ck.astype(BF16)
    vall[:, 0:WINDOW, :] = cv.astype(BF16)
    kall[:, WINDOW:nkeys, :] = k.astype(BF16).reshape(sb, DEC_SEQ, KV_W)
    vall[:, WINDOW:nkeys, :] = v.astype(BF16).reshape(sb, DEC_SEQ, KV_W)

    xb = (r * norm_b_ref[...]).astype(BF16)
    zq = _dot(xb, w_in_b_ref[:, 0:SWA_W])
    gq = _dot(xb, w_in_b_ref[:, SWA_W:2 * SWA_W])
    qm2 = _dot(xb, w_in_b_ref[:, 2 * SWA_W:2 * SWA_W + MEM_W])
    gm2 = _dot(xb, w_in_b_ref[:, 2 * SWA_W + MEM_W:])
    q = _rope(_head_rms(zq, qgain_ref[...], bseg), cos_t, sin_t, first8) * Q_SCALE
    q_scr[...] = q.reshape(sb, DEC_SEQ, SWA_W)
    qm_scr[1] = (_head_rms(qm2, mqg_ref[1:2, :], bseg) * Q_SCALE).reshape(sb, DEC_SEQ, MEM_W)

    rows = HEADS_PER_CHUNK * DEC_SEQ
    ti = lax.broadcasted_iota(jnp.int32, (rows, 2 * WINDOW), 0) % DEC_SEQ
    ci = lax.broadcasted_iota(jnp.int32, (rows, 2 * WINDOW), 1)
    bias = jnp.where((ci > ti) & (ci <= ti + WINDOW), 0.0, NEG_INF)
    sink_cols = _sink_columns(sinks_ref, DEC_SEQ)

    def swa_body(b, carry):
        kb = kall[b]
        vb = vall[b]
        for g in range(GROUP):
            sl = slice(g * MXU_DIM, (g + 1) * MXU_DIM)
            s = _dot_nt(_stack_heads(q_scr[b, :, sl], hmask), kb) + bias
            p, rden = _softmax_weights(s, sink_cols[g])
            att_scr[b, :, sl] = _unstack_heads(_dot(p, vb) * rden, hmask)
        return carry

    lax.fori_loop(0, sb, swa_body, 0)
    lax.fori_loop(0, sb, functools.partial(mem_body, 1), 0)
    ys = att_scr[...].reshape(m, SWA_W) * _silu(gq)
    ym2 = ym_scr[1].reshape(m, MEM_W) * _silu(gm2)
    y = x1 + _dot(jnp.concatenate([ys, ym2], axis=1).astype(BF16), w_out_b_ref[...])
    y_ref[...] = y.reshape(sb, DEC_SEQ, D_MODEL)


def _sample_call(sinks, x, pref, ck, cv, cmk_t, cmv_t, cos_t, sin_t, wts, mqg, bseg):
    sb = SAMPLE_BLOCK
    nb = x.shape[0]
    seq3 = lambda a, b_: pl.BlockSpec((sb, a, b_), lambda i: (i, 0, 0))
    in_specs = [pl.BlockSpec(memory_space=pltpu.SMEM),
                seq3(DEC_SEQ, D_MODEL), seq3(CARRY_ROWS, POOL_W), seq3(WINDOW, KV_W), seq3(WINDOW, KV_W),
                pl.BlockSpec((DEPTH, sb, MEM_W, N_MEM), lambda i: (0, i, 0, 0)),
                pl.BlockSpec((DEPTH, sb, MEM_W, N_MEM), lambda i: (0, i, 0, 0)),
                _const_spec(cos_t.shape), _const_spec(sin_t.shape)]
    in_specs += [_const_spec(w.shape) for w in wts]
    in_specs += [_const_spec(a.shape) for a in (mqg, bseg)]
    out_shape = [jax.ShapeDtypeStruct((nb, DEC_SEQ, D_MODEL), F32),
                 jax.ShapeDtypeStruct((nb, POOL_PAD, POOL_W), F32),
                 jax.ShapeDtypeStruct((nb, WINDOW, KV_W), F32),
                 jax.ShapeDtypeStruct((nb, WINDOW, KV_W), F32)]
    out_specs = [seq3(DEC_SEQ, D_MODEL), seq3(POOL_PAD, POOL_W), seq3(WINDOW, KV_W), seq3(WINDOW, KV_W)]
    return pl.pallas_call(
        _sample_kernel,
        grid=(nb // sb,),
        in_specs=in_specs,
        out_specs=out_specs,
        out_shape=out_shape,
        scratch_shapes=[pltpu.VMEM((sb, DEC_SEQ, SWA_W), F32),
                        pltpu.VMEM((sb, DEC_SEQ, SWA_W), F32),
                        pltpu.VMEM((DEPTH, sb, DEC_SEQ, MEM_W), F32),
                        pltpu.VMEM((DEPTH, sb, DEC_SEQ, MEM_W), F32),
                        pltpu.VMEM((sb, 2 * WINDOW, KV_W), BF16),
                        pltpu.VMEM((sb, 2 * WINDOW, KV_W), BF16)],
        compiler_params=pltpu.CompilerParams(dimension_semantics=("arbitrary",),
                                             vmem_limit_bytes=VMEM_LIMIT),
        name="sample",
    )(sinks, x, pref, ck, cv, cmk_t, cmv_t, cos_t, sin_t, *wts, mqg, bseg)


def _rope_lane_tables(pos):
    half = ROT_DIM // 2
    inv = ROPE_THETA ** (-jnp.arange(half, dtype=F32) * 2.0 / ROT_DIM)
    ang = pos.astype(F32)[:, None] * inv[None, :]
    cos, sin = jnp.cos(ang), jnp.sin(ang)
    t = pos.shape[0]
    rest = HEAD_DIM - ROT_DIM
    cos64 = jnp.concatenate([cos, cos, jnp.ones((t, rest), F32)], axis=1)
    sin64 = jnp.concatenate([-sin, sin, jnp.zeros((t, rest), F32)], axis=1)
    reps = LANES // HEAD_DIM
    return jnp.tile(cos64, (1, reps)), jnp.tile(sin64, (1, reps))


def _head_tile(g, width):
    return jnp.tile(g.astype(F32), width // HEAD_DIM).reshape(1, width)


def kernel(x_prompt, x_sample, state_pool, cache_swa_k, cache_swa_v, cache_mem_k, cache_mem_v, mem_prompt,
           norm_a, w_in_a, pool_mix_w, pool_scale, w_out_a, kv_norm, w_kv, k_norm,
           norm_b, w_in_b, q_norm, sinks, w_out_b, mem_norm, w_mem_kv, mem_q_norm, mem_k_norm):
    seg = jnp.arange(MXU_DIM) // HEAD_DIM
    bseg = ((seg[:, None] == seg[None, :]).astype(F32) / HEAD_DIM).astype(BF16)
    wcol = jnp.repeat(jnp.asarray(POOL_WINDOWS, F32), POOL_GW).reshape(1, POOL_W)

    lane = jnp.arange(SWA_W)
    g_of, kvh_of, d_of = lane // MXU_DIM, (lane % MXU_DIM) // HEAD_DIM, lane % HEAD_DIM
    perm = (kvh_of * GROUP + g_of) * HEAD_DIM + d_of
    w_in_b_p = jnp.concatenate([w_in_b[0][:, perm], w_in_b[0][:, SWA_W + perm], w_in_b[0][:, 2 * SWA_W:]], axis=1)
    w_out_b_p = jnp.concatenate([w_out_b[0][perm, :], w_out_b[0][SWA_W:, :]], axis=0)
    sinks_p = sinks[0].astype(F32)[perm[::HEAD_DIM] // HEAD_DIM]

    wbd = jax.scipy.linalg.block_diag(*[pool_mix_w[0, g] for g in range(len(POOL_WINDOWS))]).astype(BF16)
    wts = (norm_a[0].reshape(1, D_MODEL), w_in_a[0].astype(BF16), wbd, pool_scale[0].reshape(1, POOL_W), wcol,
           w_out_a[0].astype(BF16),
           kv_norm.reshape(1, D_MODEL), w_kv.astype(BF16), _head_tile(k_norm, KV_W),
           norm_b[0].reshape(1, D_MODEL), w_in_b_p.astype(BF16), _head_tile(q_norm[0], SWA_W),
           w_out_b_p.astype(BF16))
    mqg = jnp.concatenate([_head_tile(mem_q_norm[l], MEM_W) for l in range(DEPTH)], axis=0)
    mkg_col = jnp.stack([jnp.tile(mem_k_norm[l].astype(F32), MEM_HEADS).reshape(MEM_W, 1) for l in range(DEPTH)])

    wk_t = jnp.swapaxes(w_mem_kv[:, :, :MEM_W], 1, 2).astype(BF16)
    wv_t = jnp.swapaxes(w_mem_kv[:, :, MEM_W:], 1, 2).astype(BF16)
    mk_t, mv_t, mk_b, mv_b = _memkv_call(mem_prompt[0], mem_norm.reshape(DEPTH, 1, D_MODEL), wk_t, wv_t, mkg_col)

    n_tiles = SEQ // PROMPT_TILE
    cb, sb_ = _rope_lane_tables(jnp.arange(n_tiles) * PROMPT_TILE)
    cr, sr = _rope_lane_tables(jnp.arange(PROMPT_TILE))
    rope_base = jnp.stack([cb, sb_], axis=1)
    rope_res = jnp.stack([cr, sr], axis=0)
    y_p, pool_p, k_p, v_p = _prompt_call(sinks_p, x_prompt[0], rope_base, rope_res, wts, mqg, mk_b, mv_b, bseg)

    cos_s, sin_s = _rope_lane_tables(PAST_LEN + jnp.arange(DEC_SEQ))
    cos_s = jnp.tile(cos_s, (SAMPLE_BLOCK, 1))
    sin_s = jnp.tile(sin_s, (SAMPLE_BLOCK, 1))
    pref = jnp.pad(state_pool[0], ((0, 0), (CARRY_ROWS - POOL_PAD, 0), (0, 0)))
    cmk_t = jnp.transpose(cache_mem_k, (0, 1, 3, 4, 2)).reshape(DEPTH, DEC_BATCH, MEM_W, N_MEM)
    cmv_t = jnp.transpose(cache_mem_v, (0, 1, 3, 4, 2)).reshape(DEPTH, DEC_BATCH, MEM_W, N_MEM)
    y_s, pool_s, k_s, v_s = _sample_call(
        sinks_p, x_sample, pref,
        cache_swa_k.reshape(DEC_BATCH, WINDOW, KV_W), cache_swa_v.reshape(DEC_BATCH, WINDOW, KV_W),
        cmk_t, cmv_t, cos_s, sin_s, wts, mqg, bseg)

    kv4 = (N_KV_HEADS, HEAD_DIM)

    def mem_out(a):
        return jnp.transpose(a.reshape(DEPTH, 1, MEM_HEADS, HEAD_DIM, N_MEM), (0, 1, 4, 2, 3))

    def swa_out_t(a):
        return jnp.transpose(a.reshape(1, N_KV_HEADS, HEAD_DIM, WINDOW), (0, 3, 1, 2))

    return (y_p[None], y_s, pool_p[None, None], pool_s[None],
            swa_out_t(k_p), swa_out_t(v_p),
            k_s.reshape(DEC_BATCH, WINDOW, *kv4), v_s.reshape(DEC_BATCH, WINDOW, *kv4),
            mem_out(mk_t), mem_out(mv_t))
```

```python
import jax
import jax.numpy as jnp
from jax import lax
from jax.experimental import pallas as pl
from jax.experimental.pallas import tpu as pltpu

D_MODEL = 1024
SEQ = 16384
DEPTH = 2
DEC_BATCH = 128
DEC_SEQ = 8
PAST_LEN = 16384
HEAD_DIM = 64
POOL_W = 768
POOL_WINDOWS = (2, 4, 8, 16)
POOL_GW = 192
POOL_PAD = 15
N_Q_HEADS = 12
N_KV_HEADS = 4
GROUP = 3
SWA_W = 768
KV_W = 256
WINDOW = 128
N_MEM = 256
MEM_HEADS = 4
MEM_W = 256
ROT_DIM = 16
ROPE_THETA = 500000.0
EPS = 1e-6

F32 = jnp.float32
BF16 = jnp.bfloat16
NEG_INF = float("-inf")
Q_SCALE = HEAD_DIM ** -0.5

LANES = 128
SUBLANES = 8
MXU_DIM = 256
HEADS_PER_CHUNK = MXU_DIM // HEAD_DIM
CARRY_ROWS = 16
PROMPT_TILE = 256
SAMPLE_BLOCK = 8
VMEM_LIMIT = 56 * 1024 * 1024


def _dot(a, b):
    return jnp.dot(a, b, preferred_element_type=F32)


def _dot_nt(a, b):
    return lax.dot_general(a, b, (((1,), (1,)), ((), ())), preferred_element_type=F32)


def _rms_unit(x):
    return x * lax.rsqrt(jnp.mean(x * x, axis=-1, keepdims=True) + EPS)


def _silu(g):
    return g / (1.0 + jnp.exp(-g))


def _head_rms(y, gain, bseg):
    parts = []
    for c in range(y.shape[1] // MXU_DIM):
        yc = y[:, c * MXU_DIM:(c + 1) * MXU_DIM]
        ms = _dot((yc * yc).astype(BF16), bseg)
        parts.append(yc * lax.rsqrt(ms + EPS) * gain[:, c * MXU_DIM:(c + 1) * MXU_DIM])
    return parts[0] if len(parts) == 1 else jnp.concatenate(parts, axis=1)


def _rope(y, cos_t, sin_t, first8):
    parts = []
    for c in range(y.shape[1] // LANES):
        yc = y[:, c * LANES:(c + 1) * LANES]
        partner = jnp.where(first8, pltpu.roll(yc, LANES - 8, 1), pltpu.roll(yc, 8, 1))
        parts.append(yc * cos_t + partner * sin_t)
    return parts[0] if len(parts) == 1 else jnp.concatenate(parts, axis=1)


def _pool_window_sums(u_hist):
    def back(a, k):
        return pltpu.roll(a, k, 0)

    lane = lax.broadcasted_iota(jnp.int32, (1, LANES), 1)
    s2 = u_hist + back(u_hist, 1)
    t = s2[:, LANES:]
    s4 = t + back(t, 2)
    t = s4[:, 2 * LANES:]
    s8 = t + back(t, 4)
    t = s8[:, LANES:]
    s16 = t + back(t, 8)
    tiles = [
        s2[:, :LANES],
        jnp.where(lane < 64, s2[:, LANES:2 * LANES], s4[:, :LANES]),
        s4[:, LANES:2 * LANES],
        s8[:, :LANES],
        jnp.where(lane < 64, s8[:, LANES:2 * LANES], s16[:, :LANES]),
        s16[:, LANES:],
    ]
    return jnp.concatenate(tiles, axis=1)


def _head_masks():
    lane = lax.broadcasted_iota(jnp.int32, (1, MXU_DIM), 1)
    return [(lane // HEAD_DIM) == j for j in range(HEADS_PER_CHUNK)]


def _first8_mask():
    lane = lax.broadcasted_iota(jnp.int32, (1, LANES), 1)
    return (lane % HEAD_DIM) < (ROT_DIM // 2)


def _stack_heads(q, hmask):
    return jnp.concatenate([jnp.where(m, q, 0.0) for m in hmask], axis=0).astype(BF16)


def _unstack_heads(o, hmask):
    m = o.shape[0] // len(hmask)
    acc = None
    for h, mask in enumerate(hmask):
        oh = jnp.where(mask, o[h * m:(h + 1) * m, :], 0.0)
        acc = oh if acc is None else acc + oh
    return acc


def _sink_columns(sinks_ref, rows_per_head):
    hrow = lax.broadcasted_iota(jnp.int32, (HEADS_PER_CHUNK * rows_per_head, 1), 0) // rows_per_head
    cols = []
    for c in range(SWA_W // MXU_DIM):
        col = jnp.zeros((HEADS_PER_CHUNK * rows_per_head, 1), F32)
        for jj in range(HEADS_PER_CHUNK):
            col = jnp.where(hrow == jj, sinks_ref[c * HEADS_PER_CHUNK + jj], col)
        cols.append(col)
    return cols


def _softmax_weights(s, sink):
    m = jnp.max(s, axis=-1, keepdims=True)
    if sink is not None:
        m = jnp.maximum(m, sink)
    p = jnp.exp(s - m)
    den = jnp.sum(p, axis=-1, keepdims=True)
    if sink is not None:
        den = den + jnp.exp(sink - m)
    return p.astype(BF16), 1.0 / den


def _mem_attn(q, k_t, v_t, hmask):
    p, rden = _softmax_weights(_dot(_stack_heads(q, hmask), k_t), None)
    return _unstack_heads(_dot_nt(p, v_t) * rden, hmask)


def _memkv_kernel(mem_ref, norm_ref, wk_ref, wv_ref, kgain_ref, mk_ref, mv_ref, mkb_ref, mvb_ref):
    xn = (_rms_unit(mem_ref[...]) * norm_ref[0]).astype(BF16)
    k_t = _dot_nt(wk_ref[0], xn)
    v_t = _dot_nt(wv_ref[0], xn)
    k3 = k_t.reshape(MEM_HEADS, HEAD_DIM, N_MEM)
    ms = jnp.mean(k3 * k3, axis=1, keepdims=True)
    k_t = (k3 * lax.rsqrt(ms + EPS)).reshape(MEM_W, N_MEM) * kgain_ref[0]
    mk_ref[0] = k_t
    mv_ref[0] = v_t
    mkb_ref[0] = k_t.astype(BF16)
    mvb_ref[0] = v_t.astype(BF16)


def _memkv_call(mem, mem_norm, wk_t, wv_t, kgain_col):
    const2 = lambda l: (0, 0)
    per_layer3 = lambda l: (l, 0, 0)
    out_f = jax.ShapeDtypeStruct((DEPTH, MEM_W, N_MEM), F32)
    out_b = jax.ShapeDtypeStruct((DEPTH, MEM_W, N_MEM), BF16)
    return pl.pallas_call(
        _memkv_kernel,
        grid=(DEPTH,),
        in_specs=[
            pl.BlockSpec((N_MEM, D_MODEL), const2),
            pl.BlockSpec((1, 1, D_MODEL), per_layer3),
            pl.BlockSpec((1, MEM_W, D_MODEL), per_layer3),
            pl.BlockSpec((1, MEM_W, D_MODEL), per_layer3),
            pl.BlockSpec((1, MEM_W, 1), per_layer3),
        ],
        out_specs=[pl.BlockSpec((1, MEM_W, N_MEM), per_layer3)] * 4,
        out_shape=[out_f, out_f, out_b, out_b],
        compiler_params=pltpu.CompilerParams(dimension_semantics=("arbitrary",)),
        name="memkv",
    )(mem, mem_norm, wk_t, wv_t, kgain_col)


def _swa_block(qb, kc, vc, bias, sink_cols, hmask):
    outs = []
    for g in range(GROUP):
        s = _dot_nt(_stack_heads(qb[:, g * MXU_DIM:(g + 1) * MXU_DIM], hmask), kc)
        s = (s.reshape(HEADS_PER_CHUNK, WINDOW, 2 * WINDOW) + bias[None]).reshape(HEADS_PER_CHUNK * WINDOW, 2 * WINDOW)
        p, rden = _softmax_weights(s, sink_cols[g])
        outs.append(_unstack_heads(_dot(p, vc) * rden, hmask))
    return jnp.concatenate(outs, axis=1)


def _prompt_kernel(sinks_ref, x_ref, rope_base_ref, rope_res_ref,
                   norm_a_ref, w_in_a_ref, wbd_ref, pscale_ref, wcol_ref, w_out_a_ref,
                   kv_norm_ref, w_kv_ref, kgain_ref,
                   norm_b_ref, w_in_b_ref, qgain_ref, w_out_b_ref,
                   mqg_ref, mk_ref, mv_ref, bseg_ref,
                   y_ref, pool_ref, ko_ref, vo_ref,
                   ucarry, kprev, vprev):
    tq = PROMPT_TILE
    i = pl.program_id(0)

    @pl.when(i == 0)
    def _():
        ucarry[0] = jnp.zeros((CARRY_ROWS, POOL_W), F32)
        kprev[0] = jnp.zeros((WINDOW, KV_W), BF16)
        vprev[0] = jnp.zeros((WINDOW, KV_W), BF16)

    rd = i % 2
    wr = (i + 1) % 2

    hmask = _head_masks()
    first8 = _first8_mask()
    bseg = bseg_ref[...]
    x = x_ref[...]

    base = rope_base_ref[i]
    cb, sb = base[0:1, :], base[1:2, :]
    cr, sr = rope_res_ref[0], rope_res_ref[1]
    cos_t = cb * cr - sb * sr
    sin_t = sb * cr + cb * sr

    xn = (_rms_unit(x) * norm_a_ref[...]).astype(BF16)
    u = _dot(xn, w_in_a_ref[:, 0:POOL_W])
    gp = _dot(xn, w_in_a_ref[:, POOL_W:2 * POOL_W])
    qm = _dot(xn, w_in_a_ref[:, 2 * POOL_W:2 * POOL_W + MEM_W])
    gm = _dot(xn, w_in_a_ref[:, 2 * POOL_W + MEM_W:])
    u_hist = jnp.concatenate([ucarry[rd], u], axis=0)
    win = _pool_window_sums(u_hist)[CARRY_ROWS:]
    pos = i * tq + lax.broadcasted_iota(jnp.int32, (tq, 1), 0)
    cnt = jnp.minimum((pos + 1).astype(F32), wcol_ref[...])
    d = win / cnt - u
    yp = _dot(d.astype(BF16), wbd_ref[...]) * pscale_ref[...] * _silu(gp)
    qmn = _head_rms(qm, mqg_ref[0:1, :], bseg) * Q_SCALE
    ym = _mem_attn(qmn, mk_ref[0], mv_ref[0], hmask) * _silu(gm)
    x1 = x + _dot(jnp.concatenate([yp, ym], axis=1).astype(BF16), w_out_a_ref[...])
    ucarry[wr] = u[tq - CARRY_ROWS:, :]

    r = _rms_unit(x1)
    kv = _dot((r * kv_norm_ref[...]).astype(BF16), w_kv_ref[...])
    k = _rope(_head_rms(kv[:, :KV_W], kgain_ref[...], bseg), cos_t, sin_t, first8)
    v = kv[:, KV_W:]
    k_all = jnp.concatenate([kprev[rd], k.astype(BF16)], axis=0)
    v_all = jnp.concatenate([vprev[rd], v.astype(BF16)], axis=0)

    xb = (r * norm_b_ref[...]).astype(BF16)
    zq = _dot(xb, w_in_b_ref[:, 0:SWA_W])
    gq = _dot(xb, w_in_b_ref[:, SWA_W:2 * SWA_W])
    qm2 = _dot(xb, w_in_b_ref[:, 2 * SWA_W:2 * SWA_W + MEM_W])
    gm2 = _dot(xb, w_in_b_ref[:, 2 * SWA_W + MEM_W:])
    q = _rope(_head_rms(zq, qgain_ref[...], bseg), cos_t, sin_t, first8) * Q_SCALE

    qi = lax.broadcasted_iota(jnp.int32, (WINDOW, 2 * WINDOW), 0)
    ci = lax.broadcasted_iota(jnp.int32, (WINDOW, 2 * WINDOW), 1)
    band_bias = jnp.where(ci > qi, jnp.where(ci <= qi + WINDOW, 0.0, NEG_INF), NEG_INF)
    sink_cols = _sink_columns(sinks_ref, WINDOW)
    ys_blocks = []
    for b in range(tq // WINDOW):
        bias = band_bias
        if b == 0:
            key_pos = ci + (i * tq - WINDOW)
            bias = jnp.where(key_pos >= 0, band_bias, NEG_INF)
        qb = q[b * WINDOW:(b + 1) * WINDOW, :]
        kc = k_all[b * WINDOW:(b + 2) * WINDOW, :]
        vc = v_all[b * WINDOW:(b + 2) * WINDOW, :]
        ys_blocks.append(_swa_block(qb, kc, vc, bias, sink_cols, hmask))
    ys = jnp.concatenate(ys_blocks, axis=0) * _silu(gq)
    qmn2 = _head_rms(qm2, mqg_ref[1:2, :], bseg) * Q_SCALE
    ym2 = _mem_attn(qmn2, mk_ref[1], mv_ref[1], hmask) * _silu(gm2)
    y_ref[...] = x1 + _dot(jnp.concatenate([ys, ym2], axis=1).astype(BF16), w_out_b_ref[...])

    kprev[wr] = k_all[tq:, :]
    vprev[wr] = v_all[tq:, :]

    @pl.when(i == pl.num_programs(0) - 1)
    def _():
        ko_ref[...] = k[tq - WINDOW:, :].T
        vo_ref[...] = v[tq - WINDOW:, :].T
        pool_ref[...] = pltpu.roll(u[tq - CARRY_ROWS:, :], CARRY_ROWS - 1, 0)[0:POOL_PAD, :]


def _const_spec(shape):
    nd = len(shape)
    return pl.BlockSpec(shape, lambda i: (0,) * nd, pipeline_mode=pl.Buffered(1))


def _prompt_call(sinks, x, rope_base, rope_res, wts, mqg, mk, mv, bseg):
    tq = PROMPT_TILE
    n = x.shape[0]
    row_spec = lambda w: pl.BlockSpec((tq, w), lambda i: (i, 0))
    in_specs = [pl.BlockSpec(memory_space=pltpu.SMEM), row_spec(D_MODEL),
                _const_spec(rope_base.shape), _const_spec(rope_res.shape)]
    in_specs += [_const_spec(w.shape) for w in wts]
    in_specs += [_const_spec(a.shape) for a in (mqg, mk, mv, bseg)]
    out_shape = [jax.ShapeDtypeStruct((n, D_MODEL), F32),
                 jax.ShapeDtypeStruct((POOL_PAD, POOL_W), F32),
                 jax.ShapeDtypeStruct((KV_W, WINDOW), F32),
                 jax.ShapeDtypeStruct((KV_W, WINDOW), F32)]
    out_specs = [row_spec(D_MODEL),
                 pl.BlockSpec((POOL_PAD, POOL_W), lambda i: (0, 0)),
                 pl.BlockSpec((KV_W, WINDOW), lambda i: (0, 0)),
                 pl.BlockSpec((KV_W, WINDOW), lambda i: (0, 0))]
    return pl.pallas_call(
        _prompt_kernel,
        grid=(n // tq,),
        in_specs=in_specs,
        out_specs=out_specs,
        out_shape=out_shape,
        scratch_shapes=[pltpu.VMEM((2, CARRY_ROWS, POOL_W), F32),
                        pltpu.VMEM((2, WINDOW, KV_W), BF16),
                        pltpu.VMEM((2, WINDOW, KV_W), BF16)],
        compiler_params=pltpu.CompilerParams(dimension_semantics=("arbitrary",),
                                             vmem_limit_bytes=VMEM_LIMIT),
        name="prompt",
    )(sinks, x, rope_base, rope_res, *wts, mqg, mk, mv, bseg)


def _stack_heads_seq(q, n_seq, hmask):
    q3 = q.reshape(n_seq, DEC_SEQ, MXU_DIM)
    return jnp.stack([jnp.where(m, q3, 0.0) for m in hmask], axis=1)


def _unstack_heads_seq(o, n_seq, hmask):
    o4 = o.reshape(n_seq, HEADS_PER_CHUNK, DEC_SEQ, MXU_DIM)
    acc = None
    for h, mask in enumerate(hmask):
        oh = jnp.where(mask, o4[:, h], 0.0)
        acc = oh if acc is None else acc + oh
    return acc.reshape(n_seq * DEC_SEQ, MXU_DIM)


def _mem_attn_seqs(q, k_ref, v_ref, layer, n_seq, hmask):
    rows = HEADS_PER_CHUNK * DEC_SEQ
    qs = _stack_heads_seq(q, n_seq, hmask).reshape(n_seq * rows, MXU_DIM).astype(BF16)
    s = jnp.concatenate([_dot(qs[b * rows:(b + 1) * rows], k_ref[layer, b].astype(BF16))
                         for b in range(n_seq)], axis=0)
    p, rden = _softmax_weights(s, None)
    o = jnp.concatenate([_dot_nt(p[b * rows:(b + 1) * rows], v_ref[layer, b].astype(BF16))
                         for b in range(n_seq)], axis=0)
    return _unstack_heads_seq(o * rden, n_seq, hmask)


def _sample_kernel(sinks_ref, x_ref, pref_ref, ck_ref, cv_ref, cmk_ref, cmv_ref, cos_ref, sin_ref,
                   norm_a_ref, w_in_a_ref, wbd_ref, pscale_ref, wcol_ref, w_out_a_ref,
                   kv_norm_ref, w_kv_ref, kgain_ref,
                   norm_b_ref, w_in_b_ref, qgain_ref, w_out_b_ref,
                   mqg_ref, bseg_ref,
                   y_ref, pool_ref, ko_ref, vo_ref):
    sb = SAMPLE_BLOCK
    m = sb * DEC_SEQ
    hmask = _head_masks()
    first8 = _first8_mask()
    bseg = bseg_ref[...]
    cos_t = cos_ref[...]
    sin_t = sin_ref[...]
    x = x_ref[...].reshape(m, D_MODEL)

    xn = (_rms_unit(x) * norm_a_ref[...]).astype(BF16)
    u = _dot(xn, w_in_a_ref[:, 0:POOL_W])
    gp = _dot(xn, w_in_a_ref[:, POOL_W:2 * POOL_W])
    qm = _dot(xn, w_in_a_ref[:, 2 * POOL_W:2 * POOL_W + MEM_W])
    gm = _dot(xn, w_in_a_ref[:, 2 * POOL_W + MEM_W:])

    hist_rows = CARRY_ROWS + DEC_SEQ
    u_hist = jnp.concatenate([pref_ref[...], u.reshape(sb, DEC_SEQ, POOL_W)], axis=1)
    u_hist = u_hist.reshape(sb * hist_rows, POOL_W)
    win = _pool_window_sums(u_hist).reshape(sb, hist_rows, POOL_W)[:, CARRY_ROWS:, :].reshape(m, POOL_W)
    tok = lax.broadcasted_iota(jnp.int32, (sb, DEC_SEQ, 1), 1).reshape(m, 1)
    cnt = jnp.minimum((PAST_LEN + tok + 1).astype(F32), wcol_ref[...])
    d = win / cnt - u
    yp = _dot(d.astype(BF16), wbd_ref[...]) * pscale_ref[...] * _silu(gp)
    shifted = pltpu.roll(u_hist, sb * hist_rows - 1, 0).reshape(sb, hist_rows, POOL_W)
    pool_ref[...] = shifted[:, SUBLANES:, :][:, 0:POOL_PAD, :]

    qmn = _head_rms(qm, mqg_ref[0:1, :], bseg) * Q_SCALE
    ym = _mem_attn_seqs(qmn, cmk_ref, cmv_ref, 0, sb, hmask) * _silu(gm)
    x1 = x + _dot(jnp.concatenate([yp, ym], axis=1).astype(BF16), w_out_a_ref[...])

    r = _rms_unit(x1)
    kv = _dot((r * kv_norm_ref[...]).astype(BF16), w_kv_ref[...])
    k = _rope(_head_rms(kv[:, :KV_W], kgain_ref[...], bseg), cos_t, sin_t, first8)
    v = kv[:, KV_W:]
    keep = WINDOW - DEC_SEQ
    ko_ref[:, 0:keep, :] = ck_ref[:, DEC_SEQ:, :]
    ko_ref[:, keep:, :] = k.reshape(sb, DEC_SEQ, KV_W)
    vo_ref[:, 0:keep, :] = cv_ref[:, DEC_SEQ:, :]
    vo_ref[:, keep:, :] = v.reshape(sb, DEC_SEQ, KV_W)

    xb = (r * norm_b_ref[...]).astype(BF16)
    zq = _dot(xb, w_in_b_ref[:, 0:SWA_W])
    gq = _dot(xb, w_in_b_ref[:, SWA_W:2 * SWA_W])
    qm2 = _dot(xb, w_in_b_ref[:, 2 * SWA_W:2 * SWA_W + MEM_W])
    gm2 = _dot(xb, w_in_b_ref[:, 2 * SWA_W + MEM_W:])
    q = _rope(_head_rms(zq, qgain_ref[...], bseg), cos_t, sin_t, first8) * Q_SCALE

    rows = N_Q_HEADS * DEC_SEQ
    qs = jnp.stack([_stack_heads_seq(q[:, g * MXU_DIM:(g + 1) * MXU_DIM], sb, hmask) for g in range(GROUP)], axis=1)
    qs = qs.reshape(sb * rows, KV_W).astype(BF16)
    s_old = jnp.concatenate([_dot_nt(qs[b * rows:(b + 1) * rows], ck_ref[b].astype(BF16))
                             for b in range(sb)], axis=0)
    s_new = _dot_nt(qs, k.astype(BF16))
    tq_old = lax.broadcasted_iota(jnp.int32, (rows, WINDOW), 0) % DEC_SEQ
    key_old = lax.broadcasted_iota(jnp.int32, (rows, WINDOW), 1)
    bias_old = jnp.where(key_old > tq_old, 0.0, NEG_INF)
    row_i = lax.broadcasted_iota(jnp.int32, (sb * rows, m), 0)
    col_i = lax.broadcasted_iota(jnp.int32, (sb * rows, m), 1)
    same_seq = (row_i // rows) == (col_i // DEC_SEQ)
    bias_new = jnp.where(same_seq, jnp.where(col_i % DEC_SEQ <= row_i % DEC_SEQ, 0.0, NEG_INF), NEG_INF)
    s_old = (s_old.reshape(sb, rows, WINDOW) + bias_old[None]).reshape(sb * rows, WINDOW)
    s_new = s_new + bias_new
    sink = jnp.concatenate(_sink_columns(sinks_ref, DEC_SEQ), axis=0)
    sink = jnp.concatenate([sink] * sb, axis=0)
    mx = jnp.maximum(jnp.maximum(jnp.max(s_old, axis=-1, keepdims=True), jnp.max(s_new, axis=-1, keepdims=True)), sink)
    p_old = jnp.exp(s_old - mx)
    p_new = jnp.exp(s_new - mx)
    den = (jnp.sum(p_old, axis=-1, keepdims=True) + jnp.sum(p_new, axis=-1, keepdims=True) + jnp.exp(sink - mx))
    p_old = p_old.astype(BF16)
    o = jnp.concatenate([_dot(p_old[b * rows:(b + 1) * rows], cv_ref[b].astype(BF16)) for b in range(sb)], axis=0)
    o = (o + _dot(p_new.astype(BF16), v.astype(BF16))) * (1.0 / den)
    o5 = o.reshape(sb, GROUP, HEADS_PER_CHUNK * DEC_SEQ, KV_W)
    ys = jnp.concatenate([_unstack_heads_seq(o5[:, g].reshape(sb * HEADS_PER_CHUNK * DEC_SEQ, KV_W), sb, hmask)
                          for g in range(GROUP)], axis=1) * _silu(gq)

    qmn2 = _head_rms(qm2, mqg_ref[1:2, :], bseg) * Q_SCALE
    ym2 = _mem_attn_seqs(qmn2, cmk_ref, cmv_ref, 1, sb, hmask) * _silu(gm2)
    y = x1 + _dot(jnp.concatenate([ys, ym2], axis=1).astype(BF16), w_out_b_ref[...])
    y_ref[...] = y.reshape(sb, DEC_SEQ, D_MODEL)


def _sample_call(sinks, x, pref, ck, cv, cmk_t, cmv_t, cos_t, sin_t, wts, mqg, bseg):
    sb = SAMPLE_BLOCK
    nb = x.shape[0]
    seq3 = lambda a, b_: pl.BlockSpec((sb, a, b_), lambda i: (i, 0, 0))
    in_specs = [pl.BlockSpec(memory_space=pltpu.SMEM),
                seq3(DEC_SEQ, D_MODEL), seq3(CARRY_ROWS, POOL_W), seq3(WINDOW, KV_W), seq3(WINDOW, KV_W),
                pl.BlockSpec((DEPTH, sb, MEM_W, N_MEM), lambda i: (0, i, 0, 0)),
                pl.BlockSpec((DEPTH, sb, MEM_W, N_MEM), lambda i: (0, i, 0, 0)),
                _const_spec(cos_t.shape), _const_spec(sin_t.shape)]
    in_specs += [_const_spec(w.shape) for w in wts]
    in_specs += [_const_spec(a.shape) for a in (mqg, bseg)]
    out_shape = [jax.ShapeDtypeStruct((nb, DEC_SEQ, D_MODEL), F32),
                 jax.ShapeDtypeStruct((nb, POOL_PAD, POOL_W), F32),
                 jax.ShapeDtypeStruct((nb, WINDOW, KV_W), F32),
                 jax.ShapeDtypeStruct((nb, WINDOW, KV_W), F32)]
    out_specs = [seq3(DEC_SEQ, D_MODEL), seq3(POOL_PAD, POOL_W), seq3(WINDOW, KV_W), seq3(WINDOW, KV_W)]
    return pl.pallas_call(
        _sample_kernel,
        grid=(nb // sb,),
        in_specs=in_specs,
        out_specs=out_specs,
        out_shape=out_shape,
        compiler_params=pltpu.CompilerParams(dimension_semantics=("arbitrary",),
                                             vmem_limit_bytes=VMEM_LIMIT),
        name="sample",
    )(sinks, x, pref, ck, cv, cmk_t, cmv_t, cos_t, sin_t, *wts, mqg, bseg)


def _rope_lane_tables(pos):
    half = ROT_DIM // 2
    inv = ROPE_THETA ** (-jnp.arange(half, dtype=F32) * 2.0 / ROT_DIM)
    ang = pos.astype(F32)[:, None] * inv[None, :]
    cos, sin = jnp.cos(ang), jnp.sin(ang)
    t = pos.shape[0]
    rest = HEAD_DIM - ROT_DIM
    cos64 = jnp.concatenate([cos, cos, jnp.ones((t, rest), F32)], axis=1)
    sin64 = jnp.concatenate([-sin, sin, jnp.zeros((t, rest), F32)], axis=1)
    reps = LANES // HEAD_DIM
    return jnp.tile(cos64, (1, reps)), jnp.tile(sin64, (1, reps))


def _head_tile(g, width):
    return jnp.tile(g.astype(F32), width // HEAD_DIM).reshape(1, width)


def kernel(x_prompt, x_sample, state_pool, cache_swa_k, cache_swa_v, cache_mem_k, cache_mem_v, mem_prompt,
           norm_a, w_in_a, pool_mix_w, pool_scale, w_out_a, kv_norm, w_kv, k_norm,
           norm_b, w_in_b, q_norm, sinks, w_out_b, mem_norm, w_mem_kv, mem_q_norm, mem_k_norm):
    seg = jnp.arange(MXU_DIM) // HEAD_DIM
    bseg = ((seg[:, None] == seg[None, :]).astype(F32) / HEAD_DIM).astype(BF16)
    wcol = jnp.repeat(jnp.asarray(POOL_WINDOWS, F32), POOL_GW).reshape(1, POOL_W)

    lane = jnp.arange(SWA_W)
    g_of, kvh_of, d_of = lane // MXU_DIM, (lane % MXU_DIM) // HEAD_DIM, lane % HEAD_DIM
    perm = (kvh_of * GROUP + g_of) * HEAD_DIM + d_of
    w_in_b_p = jnp.concatenate([w_in_b[0][:, perm], w_in_b[0][:, SWA_W + perm], w_in_b[0][:, 2 * SWA_W:]], axis=1)
    w_out_b_p = jnp.concatenate([w_out_b[0][perm, :], w_out_b[0][SWA_W:, :]], axis=0)
    sinks_p = sinks[0].astype(F32)[perm[::HEAD_DIM] // HEAD_DIM]

    wbd = jax.scipy.linalg.block_diag(*[pool_mix_w[0, g] for g in range(len(POOL_WINDOWS))]).astype(BF16)
    wts = (norm_a[0].reshape(1, D_MODEL), w_in_a[0].astype(BF16), wbd, pool_scale[0].reshape(1, POOL_W), wcol,
           w_out_a[0].astype(BF16),
           kv_norm.reshape(1, D_MODEL), w_kv.astype(BF16), _head_tile(k_norm, KV_W),
           norm_b[0].reshape(1, D_MODEL), w_in_b_p.astype(BF16), _head_tile(q_norm[0], SWA_W),
           w_out_b_p.astype(BF16))
    mqg = jnp.concatenate([_head_tile(mem_q_norm[l], MEM_W) for l in range(DEPTH)], axis=0)
    mkg_col = jnp.stack([jnp.tile(mem_k_norm[l].astype(F32), MEM_HEADS).reshape(MEM_W, 1) for l in range(DEPTH)])

    wk_t = jnp.swapaxes(w_mem_kv[:, :, :MEM_W], 1, 2).astype(BF16)
    wv_t = jnp.swapaxes(w_mem_kv[:, :, MEM_W:], 1, 2).astype(BF16)
    mk_t, mv_t, mk_b, mv_b = _memkv_call(mem_prompt[0], mem_norm.reshape(DEPTH, 1, D_MODEL), wk_t, wv_t, mkg_col)

    n_tiles = SEQ // PROMPT_TILE
    cb, sb_ = _rope_lane_tables(jnp.arange(n_tiles) * PROMPT_TILE)
    cr, sr = _rope_lane_tables(jnp.arange(PROMPT_TILE))
    rope_base = jnp.stack([cb, sb_], axis=1)
    rope_res = jnp.stack([cr, sr], axis=0)
    y_p, pool_p, k_p, v_p = _prompt_call(sinks_p, x_prompt[0], rope_base, rope_res, wts, mqg, mk_b, mv_b, bseg)

    cos_s, sin_s = _rope_lane_tables(PAST_LEN + jnp.arange(DEC_SEQ))
    cos_s = jnp.tile(cos_s, (SAMPLE_BLOCK, 1))
    sin_s = jnp.tile(sin_s, (SAMPLE_BLOCK, 1))
    pref = jnp.pad(state_pool[0], ((0, 0), (CARRY_ROWS - POOL_PAD, 0), (0, 0)))
    cmk_t = jnp.transpose(cache_mem_k, (0, 1, 3, 4, 2)).reshape(DEPTH, DEC_BATCH, MEM_W, N_MEM)
    cmv_t = jnp.transpose(cache_mem_v, (0, 1, 3, 4, 2)).reshape(DEPTH, DEC_BATCH, MEM_W, N_MEM)
    y_s, pool_s, k_s, v_s = _sample_call(
        sinks_p, x_sample, pref,
        cache_swa_k.reshape(DEC_BATCH, WINDOW, KV_W), cache_swa_v.reshape(DEC_BATCH, WINDOW, KV_W),
        cmk_t, cmv_t, cos_s, sin_s, wts, mqg, bseg)

    kv4 = (N_KV_HEADS, HEAD_DIM)

    def mem_out(a):
        return jnp.transpose(a.reshape(DEPTH, 1, MEM_HEADS, HEAD_DIM, N_MEM), (0, 1, 4, 2, 3))

    def swa_out_t(a):
        return jnp.transpose(a.reshape(1, N_KV_HEADS, HEAD_DIM, WINDOW), (0, 3, 1, 2))

    return (y_p[None], y_s, pool_p[None, None], pool_s[None],
            swa_out_t(k_p), swa_out_t(v_p),
            k_s.reshape(DEC_BATCH, WINDOW, *kv4), v_s.reshape(DEC_BATCH, WINDOW, *kv4),
            mem_out(mk_t), mem_out(mv_t))
```

```python
import jax
import jax.numpy as jnp
from jax import lax
from jax.experimental import pallas as pl
from jax.experimental.pallas import tpu as pltpu

D_MODEL = 1024
SEQ = 16384
DEPTH = 2
DEC_BATCH = 128
DEC_SEQ = 8
PAST_LEN = 16384
HEAD_DIM = 64
POOL_W = 768
POOL_WINDOWS = (2, 4, 8, 16)
POOL_GW = 192
POOL_PAD = 15
N_Q_HEADS = 12
N_KV_HEADS = 4
GROUP = 3
SWA_W = 768
KV_W = 256
WINDOW = 128
N_MEM = 256
MEM_HEADS = 4
MEM_W = 256
ROT_DIM = 16
ROPE_THETA = 500000.0
EPS = 1e-6

F32 = jnp.float32
BF16 = jnp.bfloat16
NEG_INF = float("-inf")
Q_SCALE = HEAD_DIM ** -0.5

LANES = 128
SUBLANES = 8
MXU_DIM = 256
HEADS_PER_CHUNK = MXU_DIM // HEAD_DIM
CARRY_ROWS = 16
PROMPT_TILE = 512
SAMPLE_BLOCK = 8
VMEM_LIMIT = 56 * 1024 * 1024


def _dot(a, b):
    return jnp.dot(a, b, preferred_element_type=F32)


def _dot_nt(a, b):
    return lax.dot_general(a, b, (((1,), (1,)), ((), ())), preferred_element_type=F32)


def _rms_unit(x):
    return x * lax.rsqrt(jnp.mean(x * x, axis=-1, keepdims=True) + EPS)


def _silu(g):
    return g / (1.0 + jnp.exp(-g))


def _head_rms(y, gain, bseg):
    parts = []
    for c in range(y.shape[1] // MXU_DIM):
        yc = y[:, c * MXU_DIM:(c + 1) * MXU_DIM]
        ms = _dot((yc * yc).astype(BF16), bseg)
        parts.append(yc * lax.rsqrt(ms + EPS) * gain[:, c * MXU_DIM:(c + 1) * MXU_DIM])
    return parts[0] if len(parts) == 1 else jnp.concatenate(parts, axis=1)


def _rope(y, cos_t, sin_t, first8):
    parts = []
    for c in range(y.shape[1] // LANES):
        yc = y[:, c * LANES:(c + 1) * LANES]
        partner = jnp.where(first8, pltpu.roll(yc, LANES - 8, 1), pltpu.roll(yc, 8, 1))
        parts.append(yc * cos_t + partner * sin_t)
    return parts[0] if len(parts) == 1 else jnp.concatenate(parts, axis=1)


def _pool_window_sums(u_hist):
    def back(a, k):
        return pltpu.roll(a, k, 0)

    lane = lax.broadcasted_iota(jnp.int32, (1, LANES), 1)
    s2 = u_hist + back(u_hist, 1)
    t = s2[:, LANES:]
    s4 = t + back(t, 2)
    t = s4[:, 2 * LANES:]
    s8 = t + back(t, 4)
    t = s8[:, LANES:]
    s16 = t + back(t, 8)
    tiles = [
        s2[:, :LANES],
        jnp.where(lane < 64, s2[:, LANES:2 * LANES], s4[:, :LANES]),
        s4[:, LANES:2 * LANES],
        s8[:, :LANES],
        jnp.where(lane < 64, s8[:, LANES:2 * LANES], s16[:, :LANES]),
        s16[:, LANES:],
    ]
    return jnp.concatenate(tiles, axis=1)


def _pool_mix(d, wbd_ref):
    lo, hi = MXU_DIM, 2 * MXU_DIM
    return jnp.concatenate([
        _dot(d[:, :hi], wbd_ref[:hi, :lo]),
        _dot(d, wbd_ref[:, lo:hi]),
        _dot(d[:, lo:], wbd_ref[lo:, hi:]),
    ], axis=1)


def _head_masks():
    lane = lax.broadcasted_iota(jnp.int32, (1, MXU_DIM), 1)
    return [(lane // HEAD_DIM) == j for j in range(HEADS_PER_CHUNK)]


def _first8_mask():
    lane = lax.broadcasted_iota(jnp.int32, (1, LANES), 1)
    return (lane % HEAD_DIM) < (ROT_DIM // 2)


def _stack_heads(q, hmask):
    return jnp.concatenate([jnp.where(m, q, 0.0) for m in hmask], axis=0).astype(BF16)


def _unstack_heads(o, hmask):
    m = o.shape[0] // len(hmask)
    acc = None
    for h, mask in enumerate(hmask):
        oh = jnp.where(mask, o[h * m:(h + 1) * m, :], 0.0)
        acc = oh if acc is None else acc + oh
    return acc


def _sink_columns(sinks_ref, rows_per_head):
    hrow = lax.broadcasted_iota(jnp.int32, (HEADS_PER_CHUNK * rows_per_head, 1), 0) // rows_per_head
    cols = []
    for c in range(SWA_W // MXU_DIM):
        col = jnp.zeros((HEADS_PER_CHUNK * rows_per_head, 1), F32)
        for jj in range(HEADS_PER_CHUNK):
            col = jnp.where(hrow == jj, sinks_ref[c * HEADS_PER_CHUNK + jj], col)
        cols.append(col)
    return cols


def _softmax_weights(s, sink):
    m = jnp.max(s, axis=-1, keepdims=True)
    if sink is not None:
        m = jnp.maximum(m, sink)
    p = jnp.exp(s - m)
    den = jnp.sum(p, axis=-1, keepdims=True)
    if sink is not None:
        den = den + jnp.exp(sink - m)
    return p.astype(BF16), 1.0 / den


def _mem_attn(q, k_t, v_t, hmask):
    p, rden = _softmax_weights(_dot(_stack_heads(q, hmask), k_t), None)
    return _unstack_heads(_dot_nt(p, v_t) * rden, hmask)


def _memkv_kernel(mem_ref, norm_ref, wk_ref, wv_ref, kgain_ref, mk_ref, mv_ref, mkb_ref, mvb_ref):
    xn = (_rms_unit(mem_ref[...]) * norm_ref[0]).astype(BF16)
    k_t = _dot_nt(wk_ref[0], xn)
    v_t = _dot_nt(wv_ref[0], xn)
    k3 = k_t.reshape(MEM_HEADS, HEAD_DIM, N_MEM)
    ms = jnp.mean(k3 * k3, axis=1, keepdims=True)
    k_t = (k3 * lax.rsqrt(ms + EPS)).reshape(MEM_W, N_MEM) * kgain_ref[0]
    mk_ref[0] = k_t
    mv_ref[0] = v_t
    mkb_ref[0] = k_t.astype(BF16)
    mvb_ref[0] = v_t.astype(BF16)


def _memkv_call(mem, mem_norm, wk_t, wv_t, kgain_col):
    const2 = lambda l: (0, 0)
    per_layer3 = lambda l: (l, 0, 0)
    out_f = jax.ShapeDtypeStruct((DEPTH, MEM_W, N_MEM), F32)
    out_b = jax.ShapeDtypeStruct((DEPTH, MEM_W, N_MEM), BF16)
    return pl.pallas_call(
        _memkv_kernel,
        grid=(DEPTH,),
        in_specs=[
            pl.BlockSpec((N_MEM, D_MODEL), const2),
            pl.BlockSpec((1, 1, D_MODEL), per_layer3),
            pl.BlockSpec((1, MEM_W, D_MODEL), per_layer3),
            pl.BlockSpec((1, MEM_W, D_MODEL), per_layer3),
            pl.BlockSpec((1, MEM_W, 1), per_layer3),
        ],
        out_specs=[pl.BlockSpec((1, MEM_W, N_MEM), per_layer3)] * 4,
        out_shape=[out_f, out_f, out_b, out_b],
        compiler_params=pltpu.CompilerParams(dimension_semantics=("arbitrary",)),
        name="memkv",
    )(mem, mem_norm, wk_t, wv_t, kgain_col)


def _swa_block(qb, kc, vc, bias, sink_cols, hmask):
    outs = []
    for g in range(GROUP):
        s = _dot_nt(_stack_heads(qb[:, g * MXU_DIM:(g + 1) * MXU_DIM], hmask), kc)
        s = (s.reshape(HEADS_PER_CHUNK, WINDOW, 2 * WINDOW) + bias[None]).reshape(HEADS_PER_CHUNK * WINDOW, 2 * WINDOW)
        p, rden = _softmax_weights(s, sink_cols[g])
        outs.append(_unstack_heads(_dot(p, vc) * rden, hmask))
    return jnp.concatenate(outs, axis=1)


def _prompt_kernel(sinks_ref, x_ref, rope_base_ref, rope_res_ref,
                   norm_a_ref, w_in_a_ref, wbd_ref, pscale_ref, wcol_ref, w_out_a_ref,
                   kv_norm_ref, w_kv_ref, kgain_ref,
                   norm_b_ref, w_in_b_ref, qgain_ref, w_out_b_ref,
                   mqg_ref, mk_ref, mv_ref, bseg_ref,
                   y_ref, pool_ref, ko_ref, vo_ref,
                   ucarry, kprev, vprev):
    tq = PROMPT_TILE
    i = pl.program_id(0)

    @pl.when(i == 0)
    def _():
        ucarry[0] = jnp.zeros((CARRY_ROWS, POOL_W), F32)
        kprev[0] = jnp.zeros((WINDOW, KV_W), BF16)
        vprev[0] = jnp.zeros((WINDOW, KV_W), BF16)

    rd = i % 2
    wr = (i + 1) % 2

    hmask = _head_masks()
    first8 = _first8_mask()
    bseg = bseg_ref[...]
    x = x_ref[...]

    base = rope_base_ref[i]
    cb, sb = base[0:1, :], base[1:2, :]
    cr, sr = rope_res_ref[0], rope_res_ref[1]
    cos_t = cb * cr - sb * sr
    sin_t = sb * cr + cb * sr

    xn = (_rms_unit(x) * norm_a_ref[...]).astype(BF16)
    u = _dot(xn, w_in_a_ref[:, 0:POOL_W])
    gp = _dot(xn, w_in_a_ref[:, POOL_W:2 * POOL_W])
    qm = _dot(xn, w_in_a_ref[:, 2 * POOL_W:2 * POOL_W + MEM_W])
    gm = _dot(xn, w_in_a_ref[:, 2 * POOL_W + MEM_W:])
    u_hist = jnp.concatenate([ucarry[rd], u], axis=0)
    win = _pool_window_sums(u_hist)[CARRY_ROWS:]
    pos = i * tq + lax.broadcasted_iota(jnp.int32, (tq, 1), 0)
    cnt = jnp.minimum((pos + 1).astype(F32), wcol_ref[...])
    d = win / cnt - u
    yp = _pool_mix(d.astype(BF16), wbd_ref) * pscale_ref[...] * _silu(gp)
    qmn = _head_rms(qm, mqg_ref[0:1, :], bseg) * Q_SCALE
    ym = _mem_attn(qmn, mk_ref[0], mv_ref[0], hmask) * _silu(gm)
    x1 = x + _dot(jnp.concatenate([yp, ym], axis=1).astype(BF16), w_out_a_ref[...])
    ucarry[wr] = u[tq - CARRY_ROWS:, :]

    r = _rms_unit(x1)
    kv = _dot((r * kv_norm_ref[...]).astype(BF16), w_kv_ref[...])
    k = _rope(_head_rms(kv[:, :KV_W], kgain_ref[...], bseg), cos_t, sin_t, first8)
    v = kv[:, KV_W:]
    k_all = jnp.concatenate([kprev[rd], k.astype(BF16)], axis=0)
    v_all = jnp.concatenate([vprev[rd], v.astype(BF16)], axis=0)

    xb = (r * norm_b_ref[...]).astype(BF16)
    zq = _dot(xb, w_in_b_ref[:, 0:SWA_W])
    gq = _dot(xb, w_in_b_ref[:, SWA_W:2 * SWA_W])
    qm2 = _dot(xb, w_in_b_ref[:, 2 * SWA_W:2 * SWA_W + MEM_W])
    gm2 = _dot(xb, w_in_b_ref[:, 2 * SWA_W + MEM_W:])
    q = _rope(_head_rms(zq, qgain_ref[...], bseg), cos_t, sin_t, first8) * Q_SCALE

    qi = lax.broadcasted_iota(jnp.int32, (WINDOW, 2 * WINDOW), 0)
    ci = lax.broadcasted_iota(jnp.int32, (WINDOW, 2 * WINDOW), 1)
    band_bias = jnp.where(ci > qi, jnp.where(ci <= qi + WINDOW, 0.0, NEG_INF), NEG_INF)
    sink_cols = _sink_columns(sinks_ref, WINDOW)
    ys_blocks = []
    for b in range(tq // WINDOW):
        bias = band_bias
        if b == 0:
            key_pos = ci + (i * tq - WINDOW)
            bias = jnp.where(key_pos >= 0, band_bias, NEG_INF)
        qb = q[b * WINDOW:(b + 1) * WINDOW, :]
        kc = k_all[b * WINDOW:(b + 2) * WINDOW, :]
        vc = v_all[b * WINDOW:(b + 2) * WINDOW, :]
        ys_blocks.append(_swa_block(qb, kc, vc, bias, sink_cols, hmask))
    ys = jnp.concatenate(ys_blocks, axis=0) * _silu(gq)
    qmn2 = _head_rms(qm2, mqg_ref[1:2, :], bseg) * Q_SCALE
    ym2 = _mem_attn(qmn2, mk_ref[1], mv_ref[1], hmask) * _silu(gm2)
    y_ref[...] = x1 + _dot(jnp.concatenate([ys, ym2], axis=1).astype(BF16), w_out_b_ref[...])

    kprev[wr] = k_all[tq:, :]
    vprev[wr] = v_all[tq:, :]

    @pl.when(i == pl.num_programs(0) - 1)
    def _():
        ko_ref[...] = k[tq - WINDOW:, :].T
        vo_ref[...] = v[tq - WINDOW:, :].T
        pool_ref[...] = pltpu.roll(u[tq - CARRY_ROWS:, :], CARRY_ROWS - 1, 0)[0:POOL_PAD, :]


def _const_spec(shape):
    nd = len(shape)
    return pl.BlockSpec(shape, lambda i: (0,) * nd, pipeline_mode=pl.Buffered(1))


def _prompt_call(sinks, x, rope_base, rope_res, wts, mqg, mk, mv, bseg):
    tq = PROMPT_TILE
    n = x.shape[0]
    row_spec = lambda w: pl.BlockSpec((tq, w), lambda i: (i, 0))
    in_specs = [pl.BlockSpec(memory_space=pltpu.SMEM), row_spec(D_MODEL),
                _const_spec(rope_base.shape), _const_spec(rope_res.shape)]
    in_specs += [_const_spec(w.shape) for w in wts]
    in_specs += [_const_spec(a.shape) for a in (mqg, mk, mv, bseg)]
    out_shape = [jax.ShapeDtypeStruct((n, D_MODEL), F32),
                 jax.ShapeDtypeStruct((POOL_PAD, POOL_W), F32),
                 jax.ShapeDtypeStruct((KV_W, WINDOW), F32),
                 jax.ShapeDtypeStruct((KV_W, WINDOW), F32)]
    out_specs = [row_spec(D_MODEL),
                 pl.BlockSpec((POOL_PAD, POOL_W), lambda i: (0, 0)),
                 pl.BlockSpec((KV_W, WINDOW), lambda i: (0, 0)),
                 pl.BlockSpec((KV_W, WINDOW), lambda i: (0, 0))]
    return pl.pallas_call(
        _prompt_kernel,
        grid=(n // tq,),
        in_specs=in_specs,
        out_specs=out_specs,
        out_shape=out_shape,
        scratch_shapes=[pltpu.VMEM((2, CARRY_ROWS, POOL_W), F32),
                        pltpu.VMEM((2, WINDOW, KV_W), BF16),
                        pltpu.VMEM((2, WINDOW, KV_W), BF16)],
        compiler_params=pltpu.CompilerParams(dimension_semantics=("arbitrary",),
                                             vmem_limit_bytes=VMEM_LIMIT),
        name="prompt",
    )(sinks, x, rope_base, rope_res, *wts, mqg, mk, mv, bseg)


def _stack_heads_seq(q, n_seq, hmask):
    q3 = q.reshape(n_seq, DEC_SEQ, MXU_DIM)
    return jnp.stack([jnp.where(m, q3, 0.0) for m in hmask], axis=1)


def _unstack_heads_seq(o, n_seq, hmask):
    o4 = o.reshape(n_seq, HEADS_PER_CHUNK, DEC_SEQ, MXU_DIM)
    acc = None
    for h, mask in enumerate(hmask):
        oh = jnp.where(mask, o4[:, h], 0.0)
        acc = oh if acc is None else acc + oh
    return acc.reshape(n_seq * DEC_SEQ, MXU_DIM)


def _mem_attn_seqs(q, k_ref, v_ref, layer, n_seq, hmask):
    rows = HEADS_PER_CHUNK * DEC_SEQ
    qs = _stack_heads_seq(q, n_seq, hmask).reshape(n_seq * rows, MXU_DIM).astype(BF16)
    s = jnp.concatenate([_dot(qs[b * rows:(b + 1) * rows], k_ref[layer, b].astype(BF16))
                         for b in range(n_seq)], axis=0)
    p, rden = _softmax_weights(s, None)
    o = jnp.concatenate([_dot_nt(p[b * rows:(b + 1) * rows], v_ref[layer, b].astype(BF16))
                         for b in range(n_seq)], axis=0)
    return _unstack_heads_seq(o * rden, n_seq, hmask)


def _sample_kernel(sinks_ref, x_ref, pref_ref, ck_ref, cv_ref, cmk_ref, cmv_ref, cos_ref, sin_ref,
                   norm_a_ref, w_in_a_ref, wbd_ref, pscale_ref, wcol_ref, w_out_a_ref,
                   kv_norm_ref, w_kv_ref, kgain_ref,
                   norm_b_ref, w_in_b_ref, qgain_ref, w_out_b_ref,
                   mqg_ref, bseg_ref,
                   y_ref, pool_ref, ko_ref, vo_ref):
    sb = SAMPLE_BLOCK
    m = sb * DEC_SEQ
    hmask = _head_masks()
    first8 = _first8_mask()
    bseg = bseg_ref[...]
    cos_t = cos_ref[...]
    sin_t = sin_ref[...]
    x = x_ref[...].reshape(m, D_MODEL)

    xn = (_rms_unit(x) * norm_a_ref[...]).astype(BF16)
    u = _dot(xn, w_in_a_ref[:, 0:POOL_W])
    gp = _dot(xn, w_in_a_ref[:, POOL_W:2 * POOL_W])
    qm = _dot(xn, w_in_a_ref[:, 2 * POOL_W:2 * POOL_W + MEM_W])
    gm = _dot(xn, w_in_a_ref[:, 2 * POOL_W + MEM_W:])

    hist_rows = CARRY_ROWS + DEC_SEQ
    u_hist = jnp.concatenate([pref_ref[...], u.reshape(sb, DEC_SEQ, POOL_W)], axis=1)
    u_hist = u_hist.reshape(sb * hist_rows, POOL_W)
    win = _pool_window_sums(u_hist).reshape(sb, hist_rows, POOL_W)[:, CARRY_ROWS:, :].reshape(m, POOL_W)
    tok = lax.broadcasted_iota(jnp.int32, (sb, DEC_SEQ, 1), 1).reshape(m, 1)
    cnt = jnp.minimum((PAST_LEN + tok + 1).astype(F32), wcol_ref[...])
    d = win / cnt - u
    yp = _pool_mix(d.astype(BF16), wbd_ref) * pscale_ref[...] * _silu(gp)
    shifted = pltpu.roll(u_hist, sb * hist_rows - 1, 0).reshape(sb, hist_rows, POOL_W)
    pool_ref[...] = shifted[:, SUBLANES:, :][:, 0:POOL_PAD, :]

    qmn = _head_rms(qm, mqg_ref[0:1, :], bseg) * Q_SCALE
    ym = _mem_attn_seqs(qmn, cmk_ref, cmv_ref, 0, sb, hmask) * _silu(gm)
    x1 = x + _dot(jnp.concatenate([yp, ym], axis=1).astype(BF16), w_out_a_ref[...])

    r = _rms_unit(x1)
    kv = _dot((r * kv_norm_ref[...]).astype(BF16), w_kv_ref[...])
    k = _rope(_head_rms(kv[:, :KV_W], kgain_ref[...], bseg), cos_t, sin_t, first8)
    v = kv[:, KV_W:]
    keep = WINDOW - DEC_SEQ
    ko_ref[:, 0:keep, :] = ck_ref[:, DEC_SEQ:, :]
    ko_ref[:, keep:, :] = k.reshape(sb, DEC_SEQ, KV_W)
    vo_ref[:, 0:keep, :] = cv_ref[:, DEC_SEQ:, :]
    vo_ref[:, keep:, :] = v.reshape(sb, DEC_SEQ, KV_W)

    xb = (r * norm_b_ref[...]).astype(BF16)
    zq = _dot(xb, w_in_b_ref[:, 0:SWA_W])
    gq = _dot(xb, w_in_b_ref[:, SWA_W:2 * SWA_W])
    qm2 = _dot(xb, w_in_b_ref[:, 2 * SWA_W:2 * SWA_W + MEM_W])
    gm2 = _dot(xb, w_in_b_ref[:, 2 * SWA_W + MEM_W:])
    q = _rope(_head_rms(zq, qgain_ref[...], bseg), cos_t, sin_t, first8) * Q_SCALE

    rows = N_Q_HEADS * DEC_SEQ
    qs = jnp.stack([_stack_heads_seq(q[:, g * MXU_DIM:(g + 1) * MXU_DIM], sb, hmask) for g in range(GROUP)], axis=1)
    qs = qs.reshape(sb * rows, KV_W).astype(BF16)
    s_old = jnp.concatenate([_dot_nt(qs[b * rows:(b + 1) * rows], ck_ref[b].astype(BF16))
                             for b in range(sb)], axis=0)
    s_new = _dot_nt(qs, k.astype(BF16))
    tq_old = lax.broadcasted_iota(jnp.int32, (rows, WINDOW), 0) % DEC_SEQ
    key_old = lax.broadcasted_iota(jnp.int32, (rows, WINDOW), 1)
    bias_old = jnp.where(key_old > tq_old, 0.0, NEG_INF)
    row_i = lax.broadcasted_iota(jnp.int32, (sb * rows, m), 0)
    col_i = lax.broadcasted_iota(jnp.int32, (sb * rows, m), 1)
    same_seq = (row_i // rows) == (col_i // DEC_SEQ)
    bias_new = jnp.where(same_seq, jnp.where(col_i % DEC_SEQ <= row_i % DEC_SEQ, 0.0, NEG_INF), NEG_INF)
    s_old = (s_old.reshape(sb, rows, WINDOW) + bias_old[None]).reshape(sb * rows, WINDOW)
    s_new = s_new + bias_new
    sink = jnp.concatenate(_sink_columns(sinks_ref, DEC_SEQ), axis=0)
    sink = jnp.concatenate([sink] * sb, axis=0)
    mx = jnp.maximum(jnp.maximum(jnp.max(s_old, axis=-1, keepdims=True), jnp.max(s_new, axis=-1, keepdims=True)), sink)
    p_old = jnp.exp(s_old - mx)
    p_new = jnp.exp(s_new - mx)
    den = (jnp.sum(p_old, axis=-1, keepdims=True) + jnp.sum(p_new, axis=-1, keepdims=True) + jnp.exp(sink - mx))
    p_old = p_old.astype(BF16)
    o = jnp.concatenate([_dot(p_old[b * rows:(b + 1) * rows], cv_ref[b].astype(BF16)) for b in range(sb)], axis=0)
    o = (o + _dot(p_new.astype(BF16), v.astype(BF16))) * (1.0 / den)
    o5 = o.reshape(sb, GROUP, HEADS_PER_CHUNK * DEC_SEQ, KV_W)
    ys = jnp.concatenate([_unstack_heads_seq(o5[:, g].reshape(sb * HEADS_PER_CHUNK * DEC_SEQ, KV_W), sb, hmask)
                          for g in range(GROUP)], axis=1) * _silu(gq)

    qmn2 = _head_rms(qm2, mqg_ref[1:2, :], bseg) * Q_SCALE
    ym2 = _mem_attn_seqs(qmn2, cmk_ref, cmv_ref, 1, sb, hmask) * _silu(gm2)
    y = x1 + _dot(jnp.concatenate([ys, ym2], axis=1).astype(BF16), w_out_b_ref[...])
    y_ref[...] = y.reshape(sb, DEC_SEQ, D_MODEL)


def _sample_call(sinks, x, pref, ck, cv, cmk_t, cmv_t, cos_t, sin_t, wts, mqg, bseg):
    sb = SAMPLE_BLOCK
    nb = x.shape[0]
    seq3 = lambda a, b_: pl.BlockSpec((sb, a, b_), lambda i: (i, 0, 0))
    in_specs = [pl.BlockSpec(memory_space=pltpu.SMEM),
                seq3(DEC_SEQ, D_MODEL), seq3(CARRY_ROWS, POOL_W), seq3(WINDOW, KV_W), seq3(WINDOW, KV_W),
                pl.BlockSpec((DEPTH, sb, MEM_W, N_MEM), lambda i: (0, i, 0, 0)),
                pl.BlockSpec((DEPTH, sb, MEM_W, N_MEM), lambda i: (0, i, 0, 0)),
                _const_spec(cos_t.shape), _const_spec(sin_t.shape)]
    in_specs += [_const_spec(w.shape) for w in wts]
    in_specs += [_const_spec(a.shape) for a in (mqg, bseg)]
    out_shape = [jax.ShapeDtypeStruct((nb, DEC_SEQ, D_MODEL), F32),
                 jax.ShapeDtypeStruct((nb, POOL_PAD, POOL_W), F32),
                 jax.ShapeDtypeStruct((nb, WINDOW, KV_W), F32),
                 jax.ShapeDtypeStruct((nb, WINDOW, KV_W), F32)]
    out_specs = [seq3(DEC_SEQ, D_MODEL), seq3(POOL_PAD, POOL_W), seq3(WINDOW, KV_W), seq3(WINDOW, KV_W)]
    return pl.pallas_call(
        _sample_kernel,
        grid=(nb // sb,),
        in_specs=in_specs,
        out_specs=out_specs,
        out_shape=out_shape,
        compiler_params=pltpu.CompilerParams(dimension_semantics=("arbitrary",),
                                             vmem_limit_bytes=VMEM_LIMIT),
        name="sample",
    )(sinks, x, pref, ck, cv, cmk_t, cmv_t, cos_t, sin_t, *wts, mqg, bseg)


def _rope_lane_tables(pos):
    half = ROT_DIM // 2
    inv = ROPE_THETA ** (-jnp.arange(half, dtype=F32) * 2.0 / ROT_DIM)
    ang = pos.astype(F32)[:, None] * inv[None, :]
    cos, sin = jnp.cos(ang), jnp.sin(ang)
    t = pos.shape[0]
    rest = HEAD_DIM - ROT_DIM
    cos64 = jnp.concatenate([cos, cos, jnp.ones((t, rest), F32)], axis=1)
    sin64 = jnp.concatenate([-sin, sin, jnp.zeros((t, rest), F32)], axis=1)
    reps = LANES // HEAD_DIM
    return jnp.tile(cos64, (1, reps)), jnp.tile(sin64, (1, reps))


def _head_tile(g, width):
    return jnp.tile(g.astype(F32), width // HEAD_DIM).reshape(1, width)


def kernel(x_prompt, x_sample, state_pool, cache_swa_k, cache_swa_v, cache_mem_k, cache_mem_v, mem_prompt,
           norm_a, w_in_a, pool_mix_w, pool_scale, w_out_a, kv_norm, w_kv, k_norm,
           norm_b, w_in_b, q_norm, sinks, w_out_b, mem_norm, w_mem_kv, mem_q_norm, mem_k_norm):
    seg = jnp.arange(MXU_DIM) // HEAD_DIM
    bseg = ((seg[:, None] == seg[None, :]).astype(F32) / HEAD_DIM).astype(BF16)
    wcol = jnp.repeat(jnp.asarray(POOL_WINDOWS, F32), POOL_GW).reshape(1, POOL_W)

    lane = jnp.arange(SWA_W)
    g_of, kvh_of, d_of = lane // MXU_DIM, (lane % MXU_DIM) // HEAD_DIM, lane % HEAD_DIM
    perm = (kvh_of * GROUP + g_of) * HEAD_DIM + d_of
    w_in_b_p = jnp.concatenate([w_in_b[0][:, perm], w_in_b[0][:, SWA_W + perm], w_in_b[0][:, 2 * SWA_W:]], axis=1)
    w_out_b_p = jnp.concatenate([w_out_b[0][perm, :], w_out_b[0][SWA_W:, :]], axis=0)
    sinks_p = sinks[0].astype(F32)[perm[::HEAD_DIM] // HEAD_DIM]

    wbd = jax.scipy.linalg.block_diag(*[pool_mix_w[0, g] for g in range(len(POOL_WINDOWS))]).astype(BF16)
    wts = (norm_a[0].reshape(1, D_MODEL), w_in_a[0].astype(BF16), wbd, pool_scale[0].reshape(1, POOL_W), wcol,
           w_out_a[0].astype(BF16),
           kv_norm.reshape(1, D_MODEL), w_kv.astype(BF16), _head_tile(k_norm, KV_W),
           norm_b[0].reshape(1, D_MODEL), w_in_b_p.astype(BF16), _head_tile(q_norm[0], SWA_W),
           w_out_b_p.astype(BF16))
    mqg = jnp.concatenate([_head_tile(mem_q_norm[l], MEM_W) for l in range(DEPTH)], axis=0)
    mkg_col = jnp.stack([jnp.tile(mem_k_norm[l].astype(F32), MEM_HEADS).reshape(MEM_W, 1) for l in range(DEPTH)])

    wk_t = jnp.swapaxes(w_mem_kv[:, :, :MEM_W], 1, 2).astype(BF16)
    wv_t = jnp.swapaxes(w_mem_kv[:, :, MEM_W:], 1, 2).astype(BF16)
    mk_t, mv_t, mk_b, mv_b = _memkv_call(mem_prompt[0], mem_norm.reshape(DEPTH, 1, D_MODEL), wk_t, wv_t, mkg_col)

    n_tiles = SEQ // PROMPT_TILE
    cb, sb_ = _rope_lane_tables(jnp.arange(n_tiles) * PROMPT_TILE)
    cr, sr = _rope_lane_tables(jnp.arange(PROMPT_TILE))
    rope_base = jnp.stack([cb, sb_], axis=1)
    rope_res = jnp.stack([cr, sr], axis=0)
    y_p, pool_p, k_p, v_p = _prompt_call(sinks_p, x_prompt[0], rope_base, rope_res, wts, mqg, mk_b, mv_b, bseg)

    cos_s, sin_s = _rope_lane_tables(PAST_LEN + jnp.arange(DEC_SEQ))
    cos_s = jnp.tile(cos_s, (SAMPLE_BLOCK, 1))
    sin_s = jnp.tile(sin_s, (SAMPLE_BLOCK, 1))
    pref = jnp.pad(state_pool[0], ((0, 0), (CARRY_ROWS - POOL_PAD, 0), (0, 0)))
    cmk_t = jnp.transpose(cache_mem_k, (0, 1, 3, 4, 2)).reshape(DEPTH, DEC_BATCH, MEM_W, N_MEM)
    cmv_t = jnp.transpose(cache_mem_v, (0, 1, 3, 4, 2)).reshape(DEPTH, DEC_BATCH, MEM_W, N_MEM)
    y_s, pool_s, k_s, v_s = _sample_call(
        sinks_p, x_sample, pref,
        cache_swa_k.reshape(DEC_BATCH, WINDOW, KV_W), cache_swa_v.reshape(DEC_BATCH, WINDOW, KV_W),
        cmk_t, cmv_t, cos_s, sin_s, wts, mqg, bseg)

    kv4 = (N_KV_HEADS, HEAD_DIM)

    def mem_out(a):
        return jnp.transpose(a.reshape(DEPTH, 1, MEM_HEADS, HEAD_DIM, N_MEM), (0, 1, 4, 2, 3))

    def swa_out_t(a):
        return jnp.transpose(a.reshape(1, N_KV_HEADS, HEAD_DIM, WINDOW), (0, 3, 1, 2))

    return (y_p[None], y_s, pool_p[None, None], pool_s[None],
            swa_out_t(k_p), swa_out_t(v_p),
            k_s.reshape(DEC_BATCH, WINDOW, *kv4), v_s.reshape(DEC_BATCH, WINDOW, *kv4),
            mem_out(mk_t), mem_out(mv_t))
```

```python
import jax
import jax.numpy as jnp
from jax import lax
from jax.experimental import pallas as pl
from jax.experimental.pallas import tpu as pltpu

D_MODEL = 1024
SEQ = 16384
DEPTH = 2
DEC_BATCH = 128
DEC_SEQ = 8
PAST_LEN = 16384
HEAD_DIM = 64
POOL_W = 768
POOL_WINDOWS = (2, 4, 8, 16)
POOL_GW = 192
POOL_PAD = 15
N_Q_HEADS = 12
N_KV_HEADS = 4
GROUP = 3
SWA_W = 768
KV_W = 256
WINDOW = 128
N_MEM = 256
MEM_HEADS = 4
MEM_W = 256
ROT_DIM = 16
ROPE_THETA = 500000.0
EPS = 1e-6

F32 = jnp.float32
BF16 = jnp.bfloat16
NEG_INF = float("-inf")
LOG2E = 1.4426950408889634
Q_SCALE = HEAD_DIM ** -0.5 * LOG2E

LANES = 128
SUBLANES = 8
MXU_DIM = 256
HEADS_PER_CHUNK = MXU_DIM // HEAD_DIM
CARRY_ROWS = 16
PROMPT_TILE = 512
SAMPLE_BLOCK = 8
VMEM_LIMIT = 56 * 1024 * 1024


def _dot(a, b):
    return jnp.dot(a, b, preferred_element_type=F32)


def _dot_nt(a, b):
    return lax.dot_general(a, b, (((1,), (1,)), ((), ())), preferred_element_type=F32)


def _rms_unit(x):
    return x * lax.rsqrt(jnp.mean(x * x, axis=-1, keepdims=True) + EPS)


def _silu(g):
    return g / (1.0 + jnp.exp(-g))


def _head_rms(y, gain, bseg):
    parts = []
    for c in range(y.shape[1] // MXU_DIM):
        yc = y[:, c * MXU_DIM:(c + 1) * MXU_DIM]
        ms = _dot((yc * yc).astype(BF16), bseg)
        parts.append(yc * lax.rsqrt(ms + EPS) * gain[:, c * MXU_DIM:(c + 1) * MXU_DIM])
    return parts[0] if len(parts) == 1 else jnp.concatenate(parts, axis=1)


def _rope(y, cos_t, sin_t, first8):
    parts = []
    for c in range(y.shape[1] // LANES):
        yc = y[:, c * LANES:(c + 1) * LANES]
        partner = jnp.where(first8, pltpu.roll(yc, LANES - 8, 1), pltpu.roll(yc, 8, 1))
        parts.append(yc * cos_t + partner * sin_t)
    return parts[0] if len(parts) == 1 else jnp.concatenate(parts, axis=1)


def _pool_window_sums(u_hist):
    def back(a, k):
        return pltpu.roll(a, k, 0)

    lane = lax.broadcasted_iota(jnp.int32, (1, LANES), 1)
    s2 = u_hist + back(u_hist, 1)
    t = s2[:, LANES:]
    s4 = t + back(t, 2)
    t = s4[:, 2 * LANES:]
    s8 = t + back(t, 4)
    t = s8[:, LANES:]
    s16 = t + back(t, 8)
    tiles = [
        s2[:, :LANES],
        jnp.where(lane < 64, s2[:, LANES:2 * LANES], s4[:, :LANES]),
        s4[:, LANES:2 * LANES],
        s8[:, :LANES],
        jnp.where(lane < 64, s8[:, LANES:2 * LANES], s16[:, :LANES]),
        s16[:, LANES:],
    ]
    return jnp.concatenate(tiles, axis=1)


def _pool_mix(d, wbd_ref):
    lo, hi = MXU_DIM, 2 * MXU_DIM
    return jnp.concatenate([
        _dot(d[:, :hi], wbd_ref[:hi, :lo]),
        _dot(d, wbd_ref[:, lo:hi]),
        _dot(d[:, lo:], wbd_ref[lo:, hi:]),
    ], axis=1)


def _head_masks():
    lane = lax.broadcasted_iota(jnp.int32, (1, MXU_DIM), 1)
    return [(lane // HEAD_DIM) == j for j in range(HEADS_PER_CHUNK)]


def _first8_mask():
    lane = lax.broadcasted_iota(jnp.int32, (1, LANES), 1)
    return (lane % HEAD_DIM) < (ROT_DIM // 2)


def _stack_heads(q, hmask):
    return jnp.concatenate([jnp.where(m, q, 0.0) for m in hmask], axis=0).astype(BF16)


def _unstack_heads(o, hmask):
    m = o.shape[0] // len(hmask)
    acc = None
    for h, mask in enumerate(hmask):
        oh = jnp.where(mask, o[h * m:(h + 1) * m, :], 0.0)
        acc = oh if acc is None else acc + oh
    return acc


def _sink_columns(sinks_ref, rows_per_head):
    hrow = lax.broadcasted_iota(jnp.int32, (HEADS_PER_CHUNK * rows_per_head, 1), 0) // rows_per_head
    cols = []
    for c in range(SWA_W // MXU_DIM):
        col = jnp.zeros((HEADS_PER_CHUNK * rows_per_head, 1), F32)
        for jj in range(HEADS_PER_CHUNK):
            col = jnp.where(hrow == jj, sinks_ref[c * HEADS_PER_CHUNK + jj], col)
        cols.append(col)
    return cols


def _softmax_weights(s, sink):
    m = jnp.max(s, axis=-1, keepdims=True)
    if sink is not None:
        m = jnp.maximum(m, sink)
    p = jnp.exp2(s - m)
    den = jnp.sum(p, axis=-1, keepdims=True)
    if sink is not None:
        den = den + jnp.exp2(sink - m)
    return p.astype(BF16), 1.0 / den


def _segment_softmax(s, bias, sinks):
    ps, rdens = [], []
    for h in range(HEADS_PER_CHUNK):
        seg = s[:, h * MXU_DIM:(h + 1) * MXU_DIM]
        if bias is not None:
            seg = seg + bias
        p, rden = _softmax_weights(seg, None if sinks is None else sinks[h])
        ps.append(p)
        rdens.append(rden)
    return jnp.concatenate(ps, axis=1), rdens


def _head_scale(o, rdens, hmask):
    scale = rdens[-1]
    for h in range(HEADS_PER_CHUNK - 2, -1, -1):
        scale = jnp.where(hmask[h], rdens[h], scale)
    return o * scale


def _block_diag_rows(a, hmask):
    return jnp.concatenate([jnp.where(m, a, jnp.zeros_like(a)) for m in hmask], axis=0)


def _mem_attn(q, kbd, vbd, hmask):
    p, rdens = _segment_softmax(_dot(q.astype(BF16), kbd), None, None)
    return _head_scale(_dot(p, vbd), rdens, hmask)


def _memkv_kernel(mem_ref, norm_ref, wk_ref, wv_ref, kgain_ref, mk_ref, mv_ref, kbd_ref, vbd_ref):
    xn = (_rms_unit(mem_ref[...]) * norm_ref[0]).astype(BF16)
    k_t = _dot_nt(wk_ref[0], xn)
    v_t = _dot_nt(wv_ref[0], xn)
    k3 = k_t.reshape(MEM_HEADS, HEAD_DIM, N_MEM)
    ms = jnp.mean(k3 * k3, axis=1, keepdims=True)
    k_t = (k3 * lax.rsqrt(ms + EPS)).reshape(MEM_W, N_MEM) * kgain_ref[0]
    mk_ref[0] = k_t
    mv_ref[0] = v_t
    row_head = lax.broadcasted_iota(jnp.int32, (MEM_W, 1), 0) // HEAD_DIM
    k_b = k_t.astype(BF16)
    kbd_ref[0] = jnp.concatenate([jnp.where(row_head == h, k_b, jnp.zeros_like(k_b)) for h in range(MEM_HEADS)],
                                 axis=1)
    vbd_ref[0] = _block_diag_rows(_dot_nt(xn, wv_ref[0]).astype(BF16), _head_masks())


def _memkv_call(mem, mem_norm, wk_t, wv_t, kgain_col):
    const2 = lambda l: (0, 0)
    per_layer3 = lambda l: (l, 0, 0)
    out_f = jax.ShapeDtypeStruct((DEPTH, MEM_W, N_MEM), F32)
    wide = MEM_HEADS * N_MEM
    return pl.pallas_call(
        _memkv_kernel,
        grid=(DEPTH,),
        in_specs=[
            pl.BlockSpec((N_MEM, D_MODEL), const2),
            pl.BlockSpec((1, 1, D_MODEL), per_layer3),
            pl.BlockSpec((1, MEM_W, D_MODEL), per_layer3),
            pl.BlockSpec((1, MEM_W, D_MODEL), per_layer3),
            pl.BlockSpec((1, MEM_W, 1), per_layer3),
        ],
        out_specs=[pl.BlockSpec((1, MEM_W, N_MEM), per_layer3), pl.BlockSpec((1, MEM_W, N_MEM), per_layer3),
                   pl.BlockSpec((1, MEM_W, wide), per_layer3), pl.BlockSpec((1, wide, MEM_W), per_layer3)],
        out_shape=[out_f, out_f, jax.ShapeDtypeStruct((DEPTH, MEM_W, wide), BF16),
                   jax.ShapeDtypeStruct((DEPTH, wide, MEM_W), BF16)],
        compiler_params=pltpu.CompilerParams(dimension_semantics=("arbitrary",)),
        name="memkv",
    )(mem, mem_norm, wk_t, wv_t, kgain_col)


def _swa_block(qb, kbd, vbd, bias, sink_cols, hmask):
    q3 = jnp.concatenate([qb[:, g * MXU_DIM:(g + 1) * MXU_DIM] for g in range(GROUP)], axis=0).astype(BF16)
    p, rdens = _segment_softmax(_dot_nt(q3, kbd), bias, sink_cols)
    o = _head_scale(_dot(p, vbd), rdens, hmask)
    return jnp.concatenate([o[g * WINDOW:(g + 1) * WINDOW, :] for g in range(GROUP)], axis=1)


def _prompt_kernel(sinks_ref, x_ref, rope_base_ref, rope_res_ref,
                   norm_a_ref, w_in_a_ref, wbd_ref, pscale_ref, wcol_ref, w_out_a_ref,
                   kv_norm_ref, w_kv_ref, kgain_ref,
                   norm_b_ref, w_in_b_ref, qgain_ref, w_out_b_ref,
                   mqg_ref, mkbd_ref, mvbd_ref, bseg_ref,
                   y_ref, pool_ref, ko_ref, vo_ref,
                   ucarry, kprev, vprev):
    tq = PROMPT_TILE
    i = pl.program_id(0)

    @pl.when(i == 0)
    def _():
        ucarry[0] = jnp.zeros((CARRY_ROWS, POOL_W), F32)
        kprev[0] = jnp.zeros((WINDOW, KV_W), BF16)
        vprev[0] = jnp.zeros((WINDOW, KV_W), BF16)

    rd = i % 2
    wr = (i + 1) % 2

    hmask = _head_masks()
    first8 = _first8_mask()
    bseg = bseg_ref[...]
    x = x_ref[...]

    base = rope_base_ref[i]
    cb, sb = base[0:1, :], base[1:2, :]
    cr, sr = rope_res_ref[0], rope_res_ref[1]
    cos_t = cb * cr - sb * sr
    sin_t = sb * cr + cb * sr

    xn = (_rms_unit(x) * norm_a_ref[...]).astype(BF16)
    u = _dot(xn, w_in_a_ref[:, 0:POOL_W])
    gp = _dot(xn, w_in_a_ref[:, POOL_W:2 * POOL_W])
    qm = _dot(xn, w_in_a_ref[:, 2 * POOL_W:2 * POOL_W + MEM_W])
    gm = _dot(xn, w_in_a_ref[:, 2 * POOL_W + MEM_W:])
    u_hist = jnp.concatenate([ucarry[rd], u], axis=0)
    win = _pool_window_sums(u_hist)[CARRY_ROWS:]
    pos = i * tq + lax.broadcasted_iota(jnp.int32, (tq, 1), 0)
    cnt = jnp.minimum((pos + 1).astype(F32), wcol_ref[...])
    d = win / cnt - u
    yp = _pool_mix(d.astype(BF16), wbd_ref) * pscale_ref[...] * _silu(gp)
    qmn = _head_rms(qm, mqg_ref[0:1, :], bseg) * Q_SCALE
    ym = _mem_attn(qmn, mkbd_ref[0], mvbd_ref[0], hmask) * _silu(gm)
    x1 = x + _dot(jnp.concatenate([yp, ym], axis=1).astype(BF16), w_out_a_ref[...])
    ucarry[wr] = u[tq - CARRY_ROWS:, :]

    r = _rms_unit(x1)
    kv = _dot((r * kv_norm_ref[...]).astype(BF16), w_kv_ref[...])
    k = _rope(_head_rms(kv[:, :KV_W], kgain_ref[...], bseg), cos_t, sin_t, first8)
    v = kv[:, KV_W:]
    k_all = jnp.concatenate([kprev[rd], k.astype(BF16)], axis=0)
    v_all = jnp.concatenate([vprev[rd], v.astype(BF16)], axis=0)

    xb = (r * norm_b_ref[...]).astype(BF16)
    zq = _dot(xb, w_in_b_ref[:, 0:SWA_W])
    gq = _dot(xb, w_in_b_ref[:, SWA_W:2 * SWA_W])
    qm2 = _dot(xb, w_in_b_ref[:, 2 * SWA_W:2 * SWA_W + MEM_W])
    gm2 = _dot(xb, w_in_b_ref[:, 2 * SWA_W + MEM_W:])
    q = _rope(_head_rms(zq, qgain_ref[...], bseg), cos_t, sin_t, first8) * Q_SCALE

    qi = lax.broadcasted_iota(jnp.int32, (WINDOW, 2 * WINDOW), 0)
    ci = lax.broadcasted_iota(jnp.int32, (WINDOW, 2 * WINDOW), 1)
    band_bias = jnp.where(ci > qi, jnp.where(ci <= qi + WINDOW, 0.0, NEG_INF), NEG_INF)
    grow = lax.broadcasted_iota(jnp.int32, (GROUP * WINDOW, 1), 0) // WINDOW
    sink_cols = []
    for h in range(HEADS_PER_CHUNK):
        col = jnp.zeros((GROUP * WINDOW, 1), F32)
        for g in range(GROUP):
            col = jnp.where(grow == g, sinks_ref[g * HEADS_PER_CHUNK + h], col)
        sink_cols.append(col)
    n_blk = tq // WINDOW
    k_grp = [[jnp.where(m, k_all[j * WINDOW:(j + 1) * WINDOW, :], jnp.zeros((WINDOW, KV_W), BF16)) for m in hmask]
             for j in range(n_blk + 1)]
    v_grp = [[jnp.where(m, v_all[j * WINDOW:(j + 1) * WINDOW, :], jnp.zeros((WINDOW, KV_W), BF16)) for m in hmask]
             for j in range(n_blk + 1)]
    ys_blocks = []
    for b in range(n_blk):
        bias = band_bias
        if b == 0:
            key_pos = ci + (i * tq - WINDOW)
            bias = jnp.where(key_pos >= 0, band_bias, NEG_INF)
        bias = jnp.concatenate([bias] * GROUP, axis=0)
        qb = q[b * WINDOW:(b + 1) * WINDOW, :]
        kbd = jnp.concatenate([t for h in range(HEADS_PER_CHUNK) for t in (k_grp[b][h], k_grp[b + 1][h])], axis=0)
        vbd = jnp.concatenate([t for h in range(HEADS_PER_CHUNK) for t in (v_grp[b][h], v_grp[b + 1][h])], axis=0)
        ys_blocks.append(_swa_block(qb, kbd, vbd, bias, sink_cols, hmask))
    ys = jnp.concatenate(ys_blocks, axis=0) * _silu(gq)
    qmn2 = _head_rms(qm2, mqg_ref[1:2, :], bseg) * Q_SCALE
    ym2 = _mem_attn(qmn2, mkbd_ref[1], mvbd_ref[1], hmask) * _silu(gm2)
    y_ref[...] = x1 + _dot(jnp.concatenate([ys, ym2], axis=1).astype(BF16), w_out_b_ref[...])

    kprev[wr] = k_all[tq:, :]
    vprev[wr] = v_all[tq:, :]

    @pl.when(i == pl.num_programs(0) - 1)
    def _():
        ko_ref[...] = k[tq - WINDOW:, :].T
        vo_ref[...] = v[tq - WINDOW:, :].T
        pool_ref[...] = pltpu.roll(u[tq - CARRY_ROWS:, :], CARRY_ROWS - 1, 0)[0:POOL_PAD, :]


def _const_spec(shape):
    nd = len(shape)
    return pl.BlockSpec(shape, lambda i: (0,) * nd, pipeline_mode=pl.Buffered(1))


def _prompt_call(sinks, x, rope_base, rope_res, wts, mqg, mk, mv, bseg):
    tq = PROMPT_TILE
    n = x.shape[0]
    row_spec = lambda w: pl.BlockSpec((tq, w), lambda i: (i, 0))
    in_specs = [pl.BlockSpec(memory_space=pltpu.SMEM), row_spec(D_MODEL),
                _const_spec(rope_base.shape), _const_spec(rope_res.shape)]
    in_specs += [_const_spec(w.shape) for w in wts]
    in_specs += [_const_spec(a.shape) for a in (mqg, mk, mv, bseg)]
    out_shape = [jax.ShapeDtypeStruct((n, D_MODEL), F32),
                 jax.ShapeDtypeStruct((POOL_PAD, POOL_W), F32),
                 jax.ShapeDtypeStruct((KV_W, WINDOW), F32),
                 jax.ShapeDtypeStruct((KV_W, WINDOW), F32)]
    out_specs = [row_spec(D_MODEL),
                 pl.BlockSpec((POOL_PAD, POOL_W), lambda i: (0, 0)),
                 pl.BlockSpec((KV_W, WINDOW), lambda i: (0, 0)),
                 pl.BlockSpec((KV_W, WINDOW), lambda i: (0, 0))]
    return pl.pallas_call(
        _prompt_kernel,
        grid=(n // tq,),
        in_specs=in_specs,
        out_specs=out_specs,
        out_shape=out_shape,
        scratch_shapes=[pltpu.VMEM((2, CARRY_ROWS, POOL_W), F32),
                        pltpu.VMEM((2, WINDOW, KV_W), BF16),
                        pltpu.VMEM((2, WINDOW, KV_W), BF16)],
        compiler_params=pltpu.CompilerParams(dimension_semantics=("arbitrary",),
                                             vmem_limit_bytes=VMEM_LIMIT),
        name="prompt",
    )(sinks, x, rope_base, rope_res, *wts, mqg, mk, mv, bseg)


def _stack_heads_seq(q, n_seq, hmask):
    q3 = q.reshape(n_seq, DEC_SEQ, MXU_DIM)
    return jnp.stack([jnp.where(m, q3, 0.0) for m in hmask], axis=1)


def _unstack_heads_seq(o, n_seq, hmask):
    o4 = o.reshape(n_seq, HEADS_PER_CHUNK, DEC_SEQ, MXU_DIM)
    acc = None
    for h, mask in enumerate(hmask):
        oh = jnp.where(mask, o4[:, h], 0.0)
        acc = oh if acc is None else acc + oh
    return acc.reshape(n_seq * DEC_SEQ, MXU_DIM)


def _mem_attn_seqs(q, k_ref, v_ref, layer, n_seq, hmask):
    rows = HEADS_PER_CHUNK * DEC_SEQ
    qs = _stack_heads_seq(q, n_seq, hmask).reshape(n_seq * rows, MXU_DIM).astype(BF16)
    s = jnp.concatenate([_dot(qs[b * rows:(b + 1) * rows], k_ref[layer, b].astype(BF16))
                         for b in range(n_seq)], axis=0)
    p, rden = _softmax_weights(s, None)
    o = jnp.concatenate([_dot_nt(p[b * rows:(b + 1) * rows], v_ref[layer, b].astype(BF16))
                         for b in range(n_seq)], axis=0)
    return _unstack_heads_seq(o * rden, n_seq, hmask)


def _sample_kernel(sinks_ref, x_ref, pref_ref, ck_ref, cv_ref, cmk_ref, cmv_ref, cos_ref, sin_ref,
                   norm_a_ref, w_in_a_ref, wbd_ref, pscale_ref, wcol_ref, w_out_a_ref,
                   kv_norm_ref, w_kv_ref, kgain_ref,
                   norm_b_ref, w_in_b_ref, qgain_ref, w_out_b_ref,
                   mqg_ref, bseg_ref,
                   y_ref, pool_ref, ko_ref, vo_ref):
    sb = SAMPLE_BLOCK
    m = sb * DEC_SEQ
    hmask = _head_masks()
    first8 = _first8_mask()
    bseg = bseg_ref[...]
    cos_t = cos_ref[...]
    sin_t = sin_ref[...]
    x = x_ref[...].reshape(m, D_MODEL)

    xn = (_rms_unit(x) * norm_a_ref[...]).astype(BF16)
    u = _dot(xn, w_in_a_ref[:, 0:POOL_W])
    gp = _dot(xn, w_in_a_ref[:, POOL_W:2 * POOL_W])
    qm = _dot(xn, w_in_a_ref[:, 2 * POOL_W:2 * POOL_W + MEM_W])
    gm = _dot(xn, w_in_a_ref[:, 2 * POOL_W + MEM_W:])

    hist_rows = CARRY_ROWS + DEC_SEQ
    u_hist = jnp.concatenate([pref_ref[...], u.reshape(sb, DEC_SEQ, POOL_W)], axis=1)
    u_hist = u_hist.reshape(sb * hist_rows, POOL_W)
    win = _pool_window_sums(u_hist).reshape(sb, hist_rows, POOL_W)[:, CARRY_ROWS:, :].reshape(m, POOL_W)
    tok = lax.broadcasted_iota(jnp.int32, (sb, DEC_SEQ, 1), 1).reshape(m, 1)
    cnt = jnp.minimum((PAST_LEN + tok + 1).astype(F32), wcol_ref[...])
    d = win / cnt - u
    yp = _pool_mix(d.astype(BF16), wbd_ref) * pscale_ref[...] * _silu(gp)
    shifted = pltpu.roll(u_hist, sb * hist_rows - 1, 0).reshape(sb, hist_rows, POOL_W)
    pool_ref[...] = shifted[:, SUBLANES:, :][:, 0:POOL_PAD, :]

    qmn = _head_rms(qm, mqg_ref[0:1, :], bseg) * Q_SCALE
    ym = _mem_attn_seqs(qmn, cmk_ref, cmv_ref, 0, sb, hmask) * _silu(gm)
    x1 = x + _dot(jnp.concatenate([yp, ym], axis=1).astype(BF16), w_out_a_ref[...])

    r = _rms_unit(x1)
    kv = _dot((r * kv_norm_ref[...]).astype(BF16), w_kv_ref[...])
    k = _rope(_head_rms(kv[:, :KV_W], kgain_ref[...], bseg), cos_t, sin_t, first8)
    v = kv[:, KV_W:]
    keep = WINDOW - DEC_SEQ
    ko_ref[:, 0:keep, :] = ck_ref[:, DEC_SEQ:, :]
    ko_ref[:, keep:, :] = k.reshape(sb, DEC_SEQ, KV_W)
    vo_ref[:, 0:keep, :] = cv_ref[:, DEC_SEQ:, :]
    vo_ref[:, keep:, :] = v.reshape(sb, DEC_SEQ, KV_W)

    xb = (r * norm_b_ref[...]).astype(BF16)
    zq = _dot(xb, w_in_b_ref[:, 0:SWA_W])
    gq = _dot(xb, w_in_b_ref[:, SWA_W:2 * SWA_W])
    qm2 = _dot(xb, w_in_b_ref[:, 2 * SWA_W:2 * SWA_W + MEM_W])
    gm2 = _dot(xb, w_in_b_ref[:, 2 * SWA_W + MEM_W:])
    q = _rope(_head_rms(zq, qgain_ref[...], bseg), cos_t, sin_t, first8) * Q_SCALE

    rows = N_Q_HEADS * DEC_SEQ
    qs = jnp.stack([_stack_heads_seq(q[:, g * MXU_DIM:(g + 1) * MXU_DIM], sb, hmask) for g in range(GROUP)], axis=1)
    qs = qs.reshape(sb * rows, KV_W).astype(BF16)
    s_old = jnp.concatenate([_dot_nt(qs[b * rows:(b + 1) * rows], ck_ref[b].astype(BF16))
                             for b in range(sb)], axis=0)
    s_new = _dot_nt(qs, k.astype(BF16))
    tq_old = lax.broadcasted_iota(jnp.int32, (rows, WINDOW), 0) % DEC_SEQ
    key_old = lax.broadcasted_iota(jnp.int32, (rows, WINDOW), 1)
    bias_old = jnp.where(key_old > tq_old, 0.0, NEG_INF)
    row_i = lax.broadcasted_iota(jnp.int32, (sb * rows, m), 0)
    col_i = lax.broadcasted_iota(jnp.int32, (sb * rows, m), 1)
    same_seq = (row_i // rows) == (col_i // DEC_SEQ)
    bias_new = jnp.where(same_seq, jnp.where(col_i % DEC_SEQ <= row_i % DEC_SEQ, 0.0, NEG_INF), NEG_INF)
    s_old = (s_old.reshape(sb, rows, WINDOW) + bias_old[None]).reshape(sb * rows, WINDOW)
    s_new = s_new + bias_new
    sink = jnp.concatenate(_sink_columns(sinks_ref, DEC_SEQ), axis=0)
    sink = jnp.concatenate([sink] * sb, axis=0)
    mx = jnp.maximum(jnp.maximum(jnp.max(s_old, axis=-1, keepdims=True), jnp.max(s_new, axis=-1, keepdims=True)), sink)
    p_old = jnp.exp2(s_old - mx)
    p_new = jnp.exp2(s_new - mx)
    den = (jnp.sum(p_old, axis=-1, keepdims=True) + jnp.sum(p_new, axis=-1, keepdims=True) + jnp.exp2(sink - mx))
    p_old = p_old.astype(BF16)
    o = jnp.concatenate([_dot(p_old[b * rows:(b + 1) * rows], cv_ref[b].astype(BF16)) for b in range(sb)], axis=0)
    o = (o + _dot(p_new.astype(BF16), v.astype(BF16))) * (1.0 / den)
    o5 = o.reshape(sb, GROUP, HEADS_PER_CHUNK * DEC_SEQ, KV_W)
    ys = jnp.concatenate([_unstack_heads_seq(o5[:, g].reshape(sb * HEADS_PER_CHUNK * DEC_SEQ, KV_W), sb, hmask)
                          for g in range(GROUP)], axis=1) * _silu(gq)

    qmn2 = _head_rms(qm2, mqg_ref[1:2, :], bseg) * Q_SCALE
    ym2 = _mem_attn_seqs(qmn2, cmk_ref, cmv_ref, 1, sb, hmask) * _silu(gm2)
    y = x1 + _dot(jnp.concatenate([ys, ym2], axis=1).astype(BF16), w_out_b_ref[...])
    y_ref[...] = y.reshape(sb, DEC_SEQ, D_MODEL)


def _sample_call(sinks, x, pref, ck, cv, cmk_t, cmv_t, cos_t, sin_t, wts, mqg, bseg):
    sb = SAMPLE_BLOCK
    nb = x.shape[0]
    seq3 = lambda a, b_: pl.BlockSpec((sb, a, b_), lambda i: (i, 0, 0))
    in_specs = [pl.BlockSpec(memory_space=pltpu.SMEM),
                seq3(DEC_SEQ, D_MODEL), seq3(CARRY_ROWS, POOL_W), seq3(WINDOW, KV_W), seq3(WINDOW, KV_W),
                pl.BlockSpec((DEPTH, sb, MEM_W, N_MEM), lambda i: (0, i, 0, 0)),
                pl.BlockSpec((DEPTH, sb, MEM_W, N_MEM), lambda i: (0, i, 0, 0)),
                _const_spec(cos_t.shape), _const_spec(sin_t.shape)]
    in_specs += [_const_spec(w.shape) for w in wts]
    in_specs += [_const_spec(a.shape) for a in (mqg, bseg)]
    out_shape = [jax.ShapeDtypeStruct((nb, DEC_SEQ, D_MODEL), F32),
                 jax.ShapeDtypeStruct((nb, POOL_PAD, POOL_W), F32),
                 jax.ShapeDtypeStruct((nb, WINDOW, KV_W), F32),
                 jax.ShapeDtypeStruct((nb, WINDOW, KV_W), F32)]
    out_specs = [seq3(DEC_SEQ, D_MODEL), seq3(POOL_PAD, POOL_W), seq3(WINDOW, KV_W), seq3(WINDOW, KV_W)]
    return pl.pallas_call(
        _sample_kernel,
        grid=(nb // sb,),
        in_specs=in_specs,
        out_specs=out_specs,
        out_shape=out_shape,
        compiler_params=pltpu.CompilerParams(dimension_semantics=("arbitrary",),
                                             vmem_limit_bytes=VMEM_LIMIT),
        name="sample",
    )(sinks, x, pref, ck, cv, cmk_t, cmv_t, cos_t, sin_t, *wts, mqg, bseg)


def _rope_lane_tables(pos):
    half = ROT_DIM // 2
    inv = ROPE_THETA ** (-jnp.arange(half, dtype=F32) * 2.0 / ROT_DIM)
    ang = pos.astype(F32)[:, None] * inv[None, :]
    cos, sin = jnp.cos(ang), jnp.sin(ang)
    t = pos.shape[0]
    rest = HEAD_DIM - ROT_DIM
    cos64 = jnp.concatenate([cos, cos, jnp.ones((t, rest), F32)], axis=1)
    sin64 = jnp.concatenate([-sin, sin, jnp.zeros((t, rest), F32)], axis=1)
    reps = LANES // HEAD_DIM
    return jnp.tile(cos64, (1, reps)), jnp.tile(sin64, (1, reps))


def _head_tile(g, width):
    return jnp.tile(g.astype(F32), width // HEAD_DIM).reshape(1, width)


def kernel(x_prompt, x_sample, state_pool, cache_swa_k, cache_swa_v, cache_mem_k, cache_mem_v, mem_prompt,
           norm_a, w_in_a, pool_mix_w, pool_scale, w_out_a, kv_norm, w_kv, k_norm,
           norm_b, w_in_b, q_norm, sinks, w_out_b, mem_norm, w_mem_kv, mem_q_norm, mem_k_norm):
    seg = jnp.arange(MXU_DIM) // HEAD_DIM
    bseg = ((seg[:, None] == seg[None, :]).astype(F32) / HEAD_DIM).astype(BF16)
    wcol = jnp.repeat(jnp.asarray(POOL_WINDOWS, F32), POOL_GW).reshape(1, POOL_W)

    lane = jnp.arange(SWA_W)
    g_of, kvh_of, d_of = lane // MXU_DIM, (lane % MXU_DIM) // HEAD_DIM, lane % HEAD_DIM
    perm = (kvh_of * GROUP + g_of) * HEAD_DIM + d_of
    w_in_b_p = jnp.concatenate([w_in_b[0][:, perm], w_in_b[0][:, SWA_W + perm], w_in_b[0][:, 2 * SWA_W:]], axis=1)
    w_out_b_p = jnp.concatenate([w_out_b[0][perm, :], w_out_b[0][SWA_W:, :]], axis=0)
    sinks_p = sinks[0].astype(F32)[perm[::HEAD_DIM] // HEAD_DIM] * LOG2E

    wbd = jax.scipy.linalg.block_diag(*[pool_mix_w[0, g] for g in range(len(POOL_WINDOWS))]).astype(BF16)
    wts = (norm_a[0].reshape(1, D_MODEL), w_in_a[0].astype(BF16), wbd, pool_scale[0].reshape(1, POOL_W), wcol,
           w_out_a[0].astype(BF16),
           kv_norm.reshape(1, D_MODEL), w_kv.astype(BF16), _head_tile(k_norm, KV_W),
           norm_b[0].reshape(1, D_MODEL), w_in_b_p.astype(BF16), _head_tile(q_norm[0], SWA_W),
           w_out_b_p.astype(BF16))
    mqg = jnp.concatenate([_head_tile(mem_q_norm[l], MEM_W) for l in range(DEPTH)], axis=0)
    mkg_col = jnp.stack([jnp.tile(mem_k_norm[l].astype(F32), MEM_HEADS).reshape(MEM_W, 1) for l in range(DEPTH)])

    wk_t = jnp.swapaxes(w_mem_kv[:, :, :MEM_W], 1, 2).astype(BF16)
    wv_t = jnp.swapaxes(w_mem_kv[:, :, MEM_W:], 1, 2).astype(BF16)
    mk_t, mv_t, mk_b, mv_b = _memkv_call(mem_prompt[0], mem_norm.reshape(DEPTH, 1, D_MODEL), wk_t, wv_t, mkg_col)

    n_tiles = SEQ // PROMPT_TILE
    cb, sb_ = _rope_lane_tables(jnp.arange(n_tiles) * PROMPT_TILE)
    cr, sr = _rope_lane_tables(jnp.arange(PROMPT_TILE))
    rope_base = jnp.stack([cb, sb_], axis=1)
    rope_res = jnp.stack([cr, sr], axis=0)
    y_p, pool_p, k_p, v_p = _prompt_call(sinks_p, x_prompt[0], rope_base, rope_res, wts, mqg, mk_b, mv_b, bseg)

    cos_s, sin_s = _rope_lane_tables(PAST_LEN + jnp.arange(DEC_SEQ))
    cos_s = jnp.tile(cos_s, (SAMPLE_BLOCK, 1))
    sin_s = jnp.tile(sin_s, (SAMPLE_BLOCK, 1))
    pref = jnp.pad(state_pool[0], ((0, 0), (CARRY_ROWS - POOL_PAD, 0), (0, 0)))
    cmk_t = jnp.transpose(cache_mem_k, (0, 1, 3, 4, 2)).reshape(DEPTH, DEC_BATCH, MEM_W, N_MEM)
    cmv_t = jnp.transpose(cache_mem_v, (0, 1, 3, 4, 2)).reshape(DEPTH, DEC_BATCH, MEM_W, N_MEM)
    y_s, pool_s, k_s, v_s = _sample_call(
        sinks_p, x_sample, pref,
        cache_swa_k.reshape(DEC_BATCH, WINDOW, KV_W), cache_swa_v.reshape(DEC_BATCH, WINDOW, KV_W),
        cmk_t, cmv_t, cos_s, sin_s, wts, mqg, bseg)

    kv4 = (N_KV_HEADS, HEAD_DIM)

    def mem_out(a):
        return jnp.transpose(a.reshape(DEPTH, 1, MEM_HEADS, HEAD_DIM, N_MEM), (0, 1, 4, 2, 3))

    def swa_out_t(a):
        return jnp.transpose(a.reshape(1, N_KV_HEADS, HEAD_DIM, WINDOW), (0, 3, 1, 2))

    return (y_p[None], y_s, pool_p[None, None], pool_s[None],
            swa_out_t(k_p), swa_out_t(v_p),
            k_s.reshape(DEC_BATCH, WINDOW, *kv4), v_s.reshape(DEC_BATCH, WINDOW, *kv4),
            mem_out(mk_t), mem_out(mv_t))
```

```python
import jax
import jax.numpy as jnp
from jax import lax
from jax.experimental import pallas as pl
from jax.experimental.pallas import tpu as pltpu

D_MODEL = 1024
SEQ = 16384
DEPTH = 2
DEC_BATCH = 128
DEC_SEQ = 8
PAST_LEN = 16384
HEAD_DIM = 64
POOL_W = 768
POOL_WINDOWS = (2, 4, 8, 16)
POOL_GW = 192
POOL_PAD = 15
N_Q_HEADS = 12
N_KV_HEADS = 4
GROUP = 3
SWA_W = 768
KV_W = 256
WINDOW = 128
N_MEM = 256
MEM_HEADS = 4
MEM_W = 256
ROT_DIM = 16
ROPE_THETA = 500000.0
EPS = 1e-6

F32 = jnp.float32
BF16 = jnp.bfloat16
NEG_INF = float("-inf")
LOG2E = 1.4426950408889634
Q_SCALE = HEAD_DIM ** -0.5 * LOG2E

LANES = 128
SUBLANES = 8
MXU_DIM = 256
HEADS_PER_CHUNK = MXU_DIM // HEAD_DIM
CARRY_ROWS = 16
PROMPT_TILE = 512
SAMPLE_BLOCK = 8
VMEM_LIMIT = 56 * 1024 * 1024


def _dot(a, b):
    return jnp.dot(a, b, preferred_element_type=F32)


def _dot_nt(a, b):
    return lax.dot_general(a, b, (((1,), (1,)), ((), ())), preferred_element_type=F32)


def _rms_unit(x):
    return x * lax.rsqrt(jnp.mean(x * x, axis=-1, keepdims=True) + EPS)


def _silu(g):
    return g / (1.0 + jnp.exp(-g))


def _head_rms(y, gain, bseg):
    parts = []
    for c in range(y.shape[1] // MXU_DIM):
        yc = y[:, c * MXU_DIM:(c + 1) * MXU_DIM]
        ms = _dot((yc * yc).astype(BF16), bseg)
        parts.append(yc * lax.rsqrt(ms + EPS) * gain[:, c * MXU_DIM:(c + 1) * MXU_DIM])
    return parts[0] if len(parts) == 1 else jnp.concatenate(parts, axis=1)


def _rope(y, cos_t, sin_t, first8):
    parts = []
    for c in range(y.shape[1] // LANES):
        yc = y[:, c * LANES:(c + 1) * LANES]
        partner = jnp.where(first8, pltpu.roll(yc, LANES - 8, 1), pltpu.roll(yc, 8, 1))
        parts.append(yc * cos_t + partner * sin_t)
    return parts[0] if len(parts) == 1 else jnp.concatenate(parts, axis=1)


def _pool_window_sums(u_hist):
    def back(a, k):
        return pltpu.roll(a, k, 0)

    lane = lax.broadcasted_iota(jnp.int32, (1, LANES), 1)
    s2 = u_hist + back(u_hist, 1)
    t = s2[:, LANES:]
    s4 = t + back(t, 2)
    t = s4[:, 2 * LANES:]
    s8 = t + back(t, 4)
    t = s8[:, LANES:]
    s16 = t + back(t, 8)
    tiles = [
        s2[:, :LANES],
        jnp.where(lane < 64, s2[:, LANES:2 * LANES], s4[:, :LANES]),
        s4[:, LANES:2 * LANES],
        s8[:, :LANES],
        jnp.where(lane < 64, s8[:, LANES:2 * LANES], s16[:, :LANES]),
        s16[:, LANES:],
    ]
    return jnp.concatenate(tiles, axis=1)


def _pool_mix(d, wbd_ref):
    lo, hi = MXU_DIM, 2 * MXU_DIM
    return jnp.concatenate([
        _dot(d[:, :hi], wbd_ref[:hi, :lo]),
        _dot(d, wbd_ref[:, lo:hi]),
        _dot(d[:, lo:], wbd_ref[lo:, hi:]),
    ], axis=1)


def _head_masks():
    lane = lax.broadcasted_iota(jnp.int32, (1, MXU_DIM), 1)
    return [(lane // HEAD_DIM) == j for j in range(HEADS_PER_CHUNK)]


def _first8_mask():
    lane = lax.broadcasted_iota(jnp.int32, (1, LANES), 1)
    return (lane % HEAD_DIM) < (ROT_DIM // 2)


def _stack_heads(q, hmask):
    return jnp.concatenate([jnp.where(m, q, 0.0) for m in hmask], axis=0).astype(BF16)


def _unstack_heads(o, hmask):
    m = o.shape[0] // len(hmask)
    acc = None
    for h, mask in enumerate(hmask):
        oh = jnp.where(mask, o[h * m:(h + 1) * m, :], 0.0)
        acc = oh if acc is None else acc + oh
    return acc


def _sink_columns(sinks_ref, rows_per_head):
    hrow = lax.broadcasted_iota(jnp.int32, (HEADS_PER_CHUNK * rows_per_head, 1), 0) // rows_per_head
    cols = []
    for c in range(SWA_W // MXU_DIM):
        col = jnp.zeros((HEADS_PER_CHUNK * rows_per_head, 1), F32)
        for jj in range(HEADS_PER_CHUNK):
            col = jnp.where(hrow == jj, sinks_ref[c * HEADS_PER_CHUNK + jj], col)
        cols.append(col)
    return cols


def _softmax_weights(s, sink):
    m = jnp.max(s, axis=-1, keepdims=True)
    if sink is not None:
        m = jnp.maximum(m, sink)
    p = jnp.exp2(s - m)
    den = jnp.sum(p, axis=-1, keepdims=True)
    if sink is not None:
        den = den + jnp.exp2(sink - m)
    return p.astype(BF16), 1.0 / den


def _mem_attn(q, k_t, v_t, hmask):
    p, rden = _softmax_weights(_dot(_stack_heads(q, hmask), k_t), None)
    return _unstack_heads(_dot_nt(p, v_t) * rden, hmask)


def _memkv_kernel(mem_ref, norm_ref, wk_ref, wv_ref, kgain_ref, mk_ref, mv_ref, mkb_ref, mvb_ref):
    xn = (_rms_unit(mem_ref[...]) * norm_ref[0]).astype(BF16)
    k_t = _dot_nt(wk_ref[0], xn)
    v_t = _dot_nt(wv_ref[0], xn)
    k3 = k_t.reshape(MEM_HEADS, HEAD_DIM, N_MEM)
    ms = jnp.mean(k3 * k3, axis=1, keepdims=True)
    k_t = (k3 * lax.rsqrt(ms + EPS)).reshape(MEM_W, N_MEM) * kgain_ref[0]
    mk_ref[0] = k_t
    mv_ref[0] = v_t
    mkb_ref[0] = k_t.astype(BF16)
    mvb_ref[0] = v_t.astype(BF16)


def _memkv_call(mem, mem_norm, wk_t, wv_t, kgain_col):
    const2 = lambda l: (0, 0)
    per_layer3 = lambda l: (l, 0, 0)
    out_f = jax.ShapeDtypeStruct((DEPTH, MEM_W, N_MEM), F32)
    out_b = jax.ShapeDtypeStruct((DEPTH, MEM_W, N_MEM), BF16)
    return pl.pallas_call(
        _memkv_kernel,
        grid=(DEPTH,),
        in_specs=[
            pl.BlockSpec((N_MEM, D_MODEL), const2),
            pl.BlockSpec((1, 1, D_MODEL), per_layer3),
            pl.BlockSpec((1, MEM_W, D_MODEL), per_layer3),
            pl.BlockSpec((1, MEM_W, D_MODEL), per_layer3),
            pl.BlockSpec((1, MEM_W, 1), per_layer3),
        ],
        out_specs=[pl.BlockSpec((1, MEM_W, N_MEM), per_layer3)] * 4,
        out_shape=[out_f, out_f, out_b, out_b],
        compiler_params=pltpu.CompilerParams(dimension_semantics=("arbitrary",)),
        name="memkv",
    )(mem, mem_norm, wk_t, wv_t, kgain_col)


def _swa_block(qb, kc, vc, bias, sink_cols, hmask):
    outs = []
    for g in range(GROUP):
        s = _dot_nt(_stack_heads(qb[:, g * MXU_DIM:(g + 1) * MXU_DIM], hmask), kc)
        s = (s.reshape(HEADS_PER_CHUNK, WINDOW, 2 * WINDOW) + bias[None]).reshape(HEADS_PER_CHUNK * WINDOW, 2 * WINDOW)
        p, rden = _softmax_weights(s, sink_cols[g])
        outs.append(_unstack_heads(_dot(p, vc) * rden, hmask))
    return jnp.concatenate(outs, axis=1)


def _prompt_kernel(sinks_ref, x_ref, rope_base_ref, rope_res_ref,
                   norm_a_ref, w_in_a_ref, wbd_ref, pscale_ref, wcol_ref, w_out_a_ref,
                   kv_norm_ref, w_kv_ref, kgain_ref,
                   norm_b_ref, wq_ref, wg_ref, wm_ref, qgain_ref, wos_ref, wom_ref,
                   mqg_ref, mk_ref, mv_ref, bseg_ref,
                   y_ref, pool_ref, ko_ref, vo_ref,
                   ucarry, kprev, vprev):
    tq = PROMPT_TILE
    i = pl.program_id(0)

    @pl.when(i == 0)
    def _():
        ucarry[0] = jnp.zeros((CARRY_ROWS, POOL_W), F32)
        kprev[0] = jnp.zeros((WINDOW, KV_W), BF16)
        vprev[0] = jnp.zeros((WINDOW, KV_W), BF16)

    rd = i % 2
    wr = (i + 1) % 2

    hmask = _head_masks()
    first8 = _first8_mask()
    bseg = bseg_ref[...]
    x = x_ref[...]

    base = rope_base_ref[i]
    cb, sb = base[0:1, :], base[1:2, :]
    cr, sr = rope_res_ref[0], rope_res_ref[1]
    cos_t = cb * cr - sb * sr
    sin_t = sb * cr + cb * sr

    xn = (_rms_unit(x) * norm_a_ref[...]).astype(BF16)
    u = _dot(xn, w_in_a_ref[:, 0:POOL_W])
    gp = _dot(xn, w_in_a_ref[:, POOL_W:2 * POOL_W])
    qm = _dot(xn, w_in_a_ref[:, 2 * POOL_W:2 * POOL_W + MEM_W])
    gm = _dot(xn, w_in_a_ref[:, 2 * POOL_W + MEM_W:])
    u_hist = jnp.concatenate([ucarry[rd], u], axis=0)
    win = _pool_window_sums(u_hist)[CARRY_ROWS:]
    pos = i * tq + lax.broadcasted_iota(jnp.int32, (tq, 1), 0)
    cnt = jnp.minimum((pos + 1).astype(F32), wcol_ref[...])
    d = win / cnt - u
    yp = _pool_mix(d.astype(BF16), wbd_ref) * pscale_ref[...] * _silu(gp)
    qmn = _head_rms(qm, mqg_ref[0:1, :], bseg) * Q_SCALE
    ym = _mem_attn(qmn, mk_ref[0], mv_ref[0], hmask) * _silu(gm)
    x1 = x + _dot(jnp.concatenate([yp, ym], axis=1).astype(BF16), w_out_a_ref[...])
    ucarry[wr] = u[tq - CARRY_ROWS:, :]

    r = _rms_unit(x1)
    kv = _dot((r * kv_norm_ref[...]).astype(BF16), w_kv_ref[...])
    k = _rope(_head_rms(kv[:, :KV_W], kgain_ref[...], bseg), cos_t, sin_t, first8)
    v = kv[:, KV_W:]
    k_all = jnp.concatenate([kprev[rd], k.astype(BF16)], axis=0)
    v_all = jnp.concatenate([vprev[rd], v.astype(BF16)], axis=0)

    xb = (r * norm_b_ref[...]).astype(BF16)
    zq = _dot(xb, wq_ref[...])
    gq = _dot(xb, wg_ref[...])
    qm2 = _dot(xb, wm_ref[:, 0:MEM_W])
    gm2 = _dot(xb, wm_ref[:, MEM_W:])
    q = _rope(_head_rms(zq, qgain_ref[...], bseg), cos_t, sin_t, first8) * Q_SCALE

    qi = lax.broadcasted_iota(jnp.int32, (WINDOW, 2 * WINDOW), 0)
    ci = lax.broadcasted_iota(jnp.int32, (WINDOW, 2 * WINDOW), 1)
    band_bias = jnp.where(ci > qi, jnp.where(ci <= qi + WINDOW, 0.0, NEG_INF), NEG_INF)
    sink_cols = _sink_columns(sinks_ref, WINDOW)
    ys_blocks = []
    for b in range(tq // WINDOW):
        bias = band_bias
        if b == 0:
            key_pos = ci + (i * tq - WINDOW)
            bias = jnp.where(key_pos >= 0, band_bias, NEG_INF)
        qb = q[b * WINDOW:(b + 1) * WINDOW, :]
        kc = k_all[b * WINDOW:(b + 2) * WINDOW, :]
        vc = v_all[b * WINDOW:(b + 2) * WINDOW, :]
        ys_blocks.append(_swa_block(qb, kc, vc, bias, sink_cols, hmask))
    ys = jnp.concatenate(ys_blocks, axis=0) * _silu(gq)
    qmn2 = _head_rms(qm2, mqg_ref[1:2, :], bseg) * Q_SCALE
    ym2 = _mem_attn(qmn2, mk_ref[1], mv_ref[1], hmask) * _silu(gm2)
    y_ref[...] = x1 + _dot(ys.astype(BF16), wos_ref[...]) + _dot(ym2.astype(BF16), wom_ref[...])

    kprev[wr] = k_all[tq:, :]
    vprev[wr] = v_all[tq:, :]

    @pl.when(i == pl.num_programs(0) - 1)
    def _():
        ko_ref[...] = k[tq - WINDOW:, :].T
        vo_ref[...] = v[tq - WINDOW:, :].T
        pool_ref[...] = pltpu.roll(u[tq - CARRY_ROWS:, :], CARRY_ROWS - 1, 0)[0:POOL_PAD, :]


def _const_spec(shape):
    nd = len(shape)
    return pl.BlockSpec(shape, lambda i: (0,) * nd, pipeline_mode=pl.Buffered(1))


def _prompt_call(sinks, x, rope_base, rope_res, wts, mqg, mk, mv, bseg):
    tq = PROMPT_TILE
    n = x.shape[0]
    row_spec = lambda w: pl.BlockSpec((tq, w), lambda i: (i, 0))
    in_specs = [pl.BlockSpec(memory_space=pltpu.SMEM), row_spec(D_MODEL),
                _const_spec(rope_base.shape), _const_spec(rope_res.shape)]
    in_specs += [_const_spec(w.shape) for w in wts]
    in_specs += [_const_spec(a.shape) for a in (mqg, mk, mv, bseg)]
    out_shape = [jax.ShapeDtypeStruct((n, D_MODEL), F32),
                 jax.ShapeDtypeStruct((POOL_PAD, POOL_W), F32),
                 jax.ShapeDtypeStruct((KV_W, WINDOW), F32),
                 jax.ShapeDtypeStruct((KV_W, WINDOW), F32)]
    out_specs = [row_spec(D_MODEL),
                 pl.BlockSpec((POOL_PAD, POOL_W), lambda i: (0, 0)),
                 pl.BlockSpec((KV_W, WINDOW), lambda i: (0, 0)),
                 pl.BlockSpec((KV_W, WINDOW), lambda i: (0, 0))]
    return pl.pallas_call(
        _prompt_kernel,
        grid=(n // tq,),
        in_specs=in_specs,
        out_specs=out_specs,
        out_shape=out_shape,
        scratch_shapes=[pltpu.VMEM((2, CARRY_ROWS, POOL_W), F32),
                        pltpu.VMEM((2, WINDOW, KV_W), BF16),
                        pltpu.VMEM((2, WINDOW, KV_W), BF16)],
        compiler_params=pltpu.CompilerParams(dimension_semantics=("arbitrary",),
                                             vmem_limit_bytes=VMEM_LIMIT),
        name="prompt",
    )(sinks, x, rope_base, rope_res, *wts, mqg, mk, mv, bseg)


def _stack_heads_seq(q, n_seq, hmask):
    q3 = q.reshape(n_seq, DEC_SEQ, MXU_DIM)
    return jnp.stack([jnp.where(m, q3, 0.0) for m in hmask], axis=1)


def _unstack_heads_seq(o, n_seq, hmask):
    o4 = o.reshape(n_seq, HEADS_PER_CHUNK, DEC_SEQ, MXU_DIM)
    acc = None
    for h, mask in enumerate(hmask):
        oh = jnp.where(mask, o4[:, h], 0.0)
        acc = oh if acc is None else acc + oh
    return acc.reshape(n_seq * DEC_SEQ, MXU_DIM)


def _mem_attn_seqs(q, k_ref, v_ref, layer, n_seq, hmask):
    rows = HEADS_PER_CHUNK * DEC_SEQ
    qs = _stack_heads_seq(q, n_seq, hmask).reshape(n_seq * rows, MXU_DIM).astype(BF16)
    s = jnp.concatenate([_dot(qs[b * rows:(b + 1) * rows], k_ref[layer, b].astype(BF16))
                         for b in range(n_seq)], axis=0)
    p, rden = _softmax_weights(s, None)
    o = jnp.concatenate([_dot_nt(p[b * rows:(b + 1) * rows], v_ref[layer, b].astype(BF16))
                         for b in range(n_seq)], axis=0)
    return _unstack_heads_seq(o * rden, n_seq, hmask)


def _sample_kernel(sinks_ref, x_ref, pref_ref, ck_ref, cv_ref, cmk_ref, cmv_ref, cos_ref, sin_ref,
                   norm_a_ref, w_in_a_ref, wbd_ref, pscale_ref, wcol_ref, w_out_a_ref,
                   kv_norm_ref, w_kv_ref, kgain_ref,
                   norm_b_ref, wq_ref, wg_ref, wm_ref, qgain_ref, wos_ref, wom_ref,
                   mqg_ref, bseg_ref,
                   y_ref, pool_ref, ko_ref, vo_ref):
    sb = SAMPLE_BLOCK
    m = sb * DEC_SEQ
    hmask = _head_masks()
    first8 = _first8_mask()
    bseg = bseg_ref[...]
    cos_t = cos_ref[...]
    sin_t = sin_ref[...]
    x = x_ref[...].reshape(m, D_MODEL)

    xn = (_rms_unit(x) * norm_a_ref[...]).astype(BF16)
    u = _dot(xn, w_in_a_ref[:, 0:POOL_W])
    gp = _dot(xn, w_in_a_ref[:, POOL_W:2 * POOL_W])
    qm = _dot(xn, w_in_a_ref[:, 2 * POOL_W:2 * POOL_W + MEM_W])
    gm = _dot(xn, w_in_a_ref[:, 2 * POOL_W + MEM_W:])

    hist_rows = CARRY_ROWS + DEC_SEQ
    u_hist = jnp.concatenate([pref_ref[...], u.reshape(sb, DEC_SEQ, POOL_W)], axis=1)
    u_hist = u_hist.reshape(sb * hist_rows, POOL_W)
    win = _pool_window_sums(u_hist).reshape(sb, hist_rows, POOL_W)[:, CARRY_ROWS:, :].reshape(m, POOL_W)
    tok = lax.broadcasted_iota(jnp.int32, (sb, DEC_SEQ, 1), 1).reshape(m, 1)
    cnt = jnp.minimum((PAST_LEN + tok + 1).astype(F32), wcol_ref[...])
    d = win / cnt - u
    yp = _pool_mix(d.astype(BF16), wbd_ref) * pscale_ref[...] * _silu(gp)
    shifted = pltpu.roll(u_hist, sb * hist_rows - 1, 0).reshape(sb, hist_rows, POOL_W)
    pool_ref[...] = shifted[:, SUBLANES:, :][:, 0:POOL_PAD, :]

    qmn = _head_rms(qm, mqg_ref[0:1, :], bseg) * Q_SCALE
    ym = _mem_attn_seqs(qmn, cmk_ref, cmv_ref, 0, sb, hmask) * _silu(gm)
    x1 = x + _dot(jnp.concatenate([yp, ym], axis=1).astype(BF16), w_out_a_ref[...])

    r = _rms_unit(x1)
    kv = _dot((r * kv_norm_ref[...]).astype(BF16), w_kv_ref[...])
    k = _rope(_head_rms(kv[:, :KV_W], kgain_ref[...], bseg), cos_t, sin_t, first8)
    v = kv[:, KV_W:]
    keep = WINDOW - DEC_SEQ
    ko_ref[:, 0:keep, :] = ck_ref[:, DEC_SEQ:, :]
    ko_ref[:, keep:, :] = k.reshape(sb, DEC_SEQ, KV_W)
    vo_ref[:, 0:keep, :] = cv_ref[:, DEC_SEQ:, :]
    vo_ref[:, keep:, :] = v.reshape(sb, DEC_SEQ, KV_W)

    xb = (r * norm_b_ref[...]).astype(BF16)
    zq = _dot(xb, wq_ref[...])
    gq = _dot(xb, wg_ref[...])
    qm2 = _dot(xb, wm_ref[:, 0:MEM_W])
    gm2 = _dot(xb, wm_ref[:, MEM_W:])
    q = _rope(_head_rms(zq, qgain_ref[...], bseg), cos_t, sin_t, first8) * Q_SCALE

    rows = N_Q_HEADS * DEC_SEQ
    qs = jnp.stack([_stack_heads_seq(q[:, g * MXU_DIM:(g + 1) * MXU_DIM], sb, hmask) for g in range(GROUP)], axis=1)
    qs = qs.reshape(sb * rows, KV_W).astype(BF16)
    s_old = jnp.concatenate([_dot_nt(qs[b * rows:(b + 1) * rows], ck_ref[b].astype(BF16))
                             for b in range(sb)], axis=0)
    s_new = _dot_nt(qs, k.astype(BF16))
    tq_old = lax.broadcasted_iota(jnp.int32, (rows, WINDOW), 0) % DEC_SEQ
    key_old = lax.broadcasted_iota(jnp.int32, (rows, WINDOW), 1)
    bias_old = jnp.where(key_old > tq_old, 0.0, NEG_INF)
    row_i = lax.broadcasted_iota(jnp.int32, (sb * rows, m), 0)
    col_i = lax.broadcasted_iota(jnp.int32, (sb * rows, m), 1)
    same_seq = (row_i // rows) == (col_i // DEC_SEQ)
    bias_new = jnp.where(same_seq, jnp.where(col_i % DEC_SEQ <= row_i % DEC_SEQ, 0.0, NEG_INF), NEG_INF)
    s_old = (s_old.reshape(sb, rows, WINDOW) + bias_old[None]).reshape(sb * rows, WINDOW)
    s_new = s_new + bias_new
    sink = jnp.concatenate(_sink_columns(sinks_ref, DEC_SEQ), axis=0)
    sink = jnp.concatenate([sink] * sb, axis=0)
    mx = jnp.maximum(jnp.maximum(jnp.max(s_old, axis=-1, keepdims=True), jnp.max(s_new, axis=-1, keepdims=True)), sink)
    p_old = jnp.exp2(s_old - mx)
    p_new = jnp.exp2(s_new - mx)
    den = (jnp.sum(p_old, axis=-1, keepdims=True) + jnp.sum(p_new, axis=-1, keepdims=True) + jnp.exp2(sink - mx))
    p_old = p_old.astype(BF16)
    o = jnp.concatenate([_dot(p_old[b * rows:(b + 1) * rows], cv_ref[b].astype(BF16)) for b in range(sb)], axis=0)
    o = (o + _dot(p_new.astype(BF16), v.astype(BF16))) * (1.0 / den)
    o5 = o.reshape(sb, GROUP, HEADS_PER_CHUNK * DEC_SEQ, KV_W)
    ys = jnp.concatenate([_unstack_heads_seq(o5[:, g].reshape(sb * HEADS_PER_CHUNK * DEC_SEQ, KV_W), sb, hmask)
                          for g in range(GROUP)], axis=1) * _silu(gq)

    qmn2 = _head_rms(qm2, mqg_ref[1:2, :], bseg) * Q_SCALE
    ym2 = _mem_attn_seqs(qmn2, cmk_ref, cmv_ref, 1, sb, hmask) * _silu(gm2)
    y = x1 + _dot(ys.astype(BF16), wos_ref[...]) + _dot(ym2.astype(BF16), wom_ref[...])
    y_ref[...] = y.reshape(sb, DEC_SEQ, D_MODEL)


def _sample_call(sinks, x, pref, ck, cv, cmk_t, cmv_t, cos_t, sin_t, wts, mqg, bseg):
    sb = SAMPLE_BLOCK
    nb = x.shape[0]
    seq3 = lambda a, b_: pl.BlockSpec((sb, a, b_), lambda i: (i, 0, 0))
    in_specs = [pl.BlockSpec(memory_space=pltpu.SMEM),
                seq3(DEC_SEQ, D_MODEL), seq3(CARRY_ROWS, POOL_W), seq3(WINDOW, KV_W), seq3(WINDOW, KV_W),
                pl.BlockSpec((DEPTH, sb, MEM_W, N_MEM), lambda i: (0, i, 0, 0)),
                pl.BlockSpec((DEPTH, sb, MEM_W, N_MEM), lambda i: (0, i, 0, 0)),
                _const_spec(cos_t.shape), _const_spec(sin_t.shape)]
    in_specs += [_const_spec(w.shape) for w in wts]
    in_specs += [_const_spec(a.shape) for a in (mqg, bseg)]
    out_shape = [jax.ShapeDtypeStruct((nb, DEC_SEQ, D_MODEL), F32),
                 jax.ShapeDtypeStruct((nb, POOL_PAD, POOL_W), F32),
                 jax.ShapeDtypeStruct((nb, WINDOW, KV_W), F32),
                 jax.ShapeDtypeStruct((nb, WINDOW, KV_W), F32)]
    out_specs = [seq3(DEC_SEQ, D_MODEL), seq3(POOL_PAD, POOL_W), seq3(WINDOW, KV_W), seq3(WINDOW, KV_W)]
    return pl.pallas_call(
        _sample_kernel,
        grid=(nb // sb,),
        in_specs=in_specs,
        out_specs=out_specs,
        out_shape=out_shape,
        compiler_params=pltpu.CompilerParams(dimension_semantics=("arbitrary",),
                                             vmem_limit_bytes=VMEM_LIMIT),
        name="sample",
    )(sinks, x, pref, ck, cv, cmk_t, cmv_t, cos_t, sin_t, *wts, mqg, bseg)


def _rope_lane_tables(pos):
    half = ROT_DIM // 2
    inv = ROPE_THETA ** (-jnp.arange(half, dtype=F32) * 2.0 / ROT_DIM)
    ang = pos.astype(F32)[:, None] * inv[None, :]
    cos, sin = jnp.cos(ang), jnp.sin(ang)
    t = pos.shape[0]
    rest = HEAD_DIM - ROT_DIM
    cos64 = jnp.concatenate([cos, cos, jnp.ones((t, rest), F32)], axis=1)
    sin64 = jnp.concatenate([-sin, sin, jnp.zeros((t, rest), F32)], axis=1)
    reps = LANES // HEAD_DIM
    return jnp.tile(cos64, (1, reps)), jnp.tile(sin64, (1, reps))


def _head_tile(g, width):
    return jnp.tile(g.astype(F32), width // HEAD_DIM).reshape(1, width)


def kernel(x_prompt, x_sample, state_pool, cache_swa_k, cache_swa_v, cache_mem_k, cache_mem_v, mem_prompt,
           norm_a, w_in_a, pool_mix_w, pool_scale, w_out_a, kv_norm, w_kv, k_norm,
           norm_b, w_in_b, q_norm, sinks, w_out_b, mem_norm, w_mem_kv, mem_q_norm, mem_k_norm):
    seg = jnp.arange(MXU_DIM) // HEAD_DIM
    bseg = ((seg[:, None] == seg[None, :]).astype(F32) / HEAD_DIM).astype(BF16)
    wcol = jnp.repeat(jnp.asarray(POOL_WINDOWS, F32), POOL_GW).reshape(1, POOL_W)

    def cols_group_major(w):
        return w.reshape(-1, N_KV_HEADS, GROUP, HEAD_DIM).transpose(0, 2, 1, 3).reshape(-1, SWA_W)

    wq = cols_group_major(w_in_b[0][:, :SWA_W]).astype(BF16)
    wg = cols_group_major(w_in_b[0][:, SWA_W:2 * SWA_W]).astype(BF16)
    wm = w_in_b[0][:, 2 * SWA_W:].astype(BF16)
    wos = w_out_b[0][:SWA_W].reshape(N_KV_HEADS, GROUP, HEAD_DIM, D_MODEL).transpose(1, 0, 2, 3).reshape(
        SWA_W, D_MODEL).astype(BF16)
    wom = w_out_b[0][SWA_W:].astype(BF16)
    sinks_p = sinks[0].astype(F32).reshape(N_KV_HEADS, GROUP).T.reshape(N_Q_HEADS) * LOG2E

    n_grp = len(POOL_WINDOWS)
    wbd = (pool_mix_w[0][:, :, None, :] * jnp.eye(n_grp, dtype=F32)[:, None, :, None]).reshape(
        POOL_W, POOL_W).astype(BF16)
    wts = (norm_a[0].reshape(1, D_MODEL), w_in_a[0].astype(BF16), wbd, pool_scale[0].reshape(1, POOL_W), wcol,
           w_out_a[0].astype(BF16),
           kv_norm.reshape(1, D_MODEL), w_kv.astype(BF16), _head_tile(k_norm, KV_W),
           norm_b[0].reshape(1, D_MODEL), wq, wg, wm, _head_tile(q_norm[0], SWA_W), wos, wom)
    mqg = jnp.concatenate([_head_tile(mem_q_norm[l], MEM_W) for l in range(DEPTH)], axis=0)
    mkg_col = jnp.stack([jnp.tile(mem_k_norm[l].astype(F32), MEM_HEADS).reshape(MEM_W, 1) for l in range(DEPTH)])

    wk_t = jnp.swapaxes(w_mem_kv[:, :, :MEM_W], 1, 2).astype(BF16)
    wv_t = jnp.swapaxes(w_mem_kv[:, :, MEM_W:], 1, 2).astype(BF16)
    mk_t, mv_t, mk_b, mv_b = _memkv_call(mem_prompt[0], mem_norm.reshape(DEPTH, 1, D_MODEL), wk_t, wv_t, mkg_col)

    n_tiles = SEQ // PROMPT_TILE
    cb, sb_ = _rope_lane_tables(jnp.arange(n_tiles) * PROMPT_TILE)
    cr, sr = _rope_lane_tables(jnp.arange(PROMPT_TILE))
    rope_base = jnp.stack([cb, sb_], axis=1)
    rope_res = jnp.stack([cr, sr], axis=0)
    y_p, pool_p, k_p, v_p = _prompt_call(sinks_p, x_prompt[0], rope_base, rope_res, wts, mqg, mk_b, mv_b, bseg)

    cos_s, sin_s = _rope_lane_tables(PAST_LEN + jnp.arange(DEC_SEQ))
    cos_s = jnp.tile(cos_s, (SAMPLE_BLOCK, 1))
    sin_s = jnp.tile(sin_s, (SAMPLE_BLOCK, 1))
    pref = jnp.pad(state_pool[0], ((0, 0), (CARRY_ROWS - POOL_PAD, 0), (0, 0)))
    cmk_t = jnp.transpose(cache_mem_k, (0, 1, 3, 4, 2)).reshape(DEPTH, DEC_BATCH, MEM_W, N_MEM)
    cmv_t = jnp.transpose(cache_mem_v, (0, 1, 3, 4, 2)).reshape(DEPTH, DEC_BATCH, MEM_W, N_MEM)
    y_s, pool_s, k_s, v_s = _sample_call(
        sinks_p, x_sample, pref,
        cache_swa_k.reshape(DEC_BATCH, WINDOW, KV_W), cache_swa_v.reshape(DEC_BATCH, WINDOW, KV_W),
        cmk_t, cmv_t, cos_s, sin_s, wts, mqg, bseg)

    kv4 = (N_KV_HEADS, HEAD_DIM)

    def mem_out(a):
        return jnp.transpose(a.reshape(DEPTH, 1, MEM_HEADS, HEAD_DIM, N_MEM), (0, 1, 4, 2, 3))

    def swa_out_t(a):
        return jnp.transpose(a.reshape(1, N_KV_HEADS, HEAD_DIM, WINDOW), (0, 3, 1, 2))

    return (y_p[None], y_s, pool_p[None, None], pool_s[None],
            swa_out_t(k_p), swa_out_t(v_p),
            k_s.reshape(DEC_BATCH, WINDOW, *kv4), v_s.reshape(DEC_BATCH, WINDOW, *kv4),
            mem_out(mk_t), mem_out(mv_t))
```

```python
import jax
import jax.numpy as jnp
from jax import lax
from jax.experimental import pallas as pl
from jax.experimental.pallas import tpu as pltpu

D_MODEL = 1024
SEQ = 16384
DEPTH = 2
DEC_BATCH = 128
DEC_SEQ = 8
PAST_LEN = 16384
HEAD_DIM = 64
POOL_W = 768
POOL_WINDOWS = (2, 4, 8, 16)
POOL_GW = 192
POOL_PAD = 15
N_Q_HEADS = 12
N_KV_HEADS = 4
GROUP = 3
SWA_W = 768
KV_W = 256
WINDOW = 128
N_MEM = 256
MEM_HEADS = 4
MEM_W = 256
ROT_DIM = 16
ROPE_THETA = 500000.0
EPS = 1e-6

F32 = jnp.float32
BF16 = jnp.bfloat16
NEG_INF = float("-inf")
LOG2E = 1.4426950408889634
Q_SCALE = HEAD_DIM ** -0.5 * LOG2E

LANES = 128
SUBLANES = 8
MXU_DIM = 256
HEADS_PER_CHUNK = MXU_DIM // HEAD_DIM
CARRY_ROWS = 16
PROMPT_TILE = 512
SAMPLE_BLOCK = 8
VMEM_LIMIT = 56 * 1024 * 1024


def _dot(a, b):
    return jnp.dot(a, b, preferred_element_type=F32)


def _dot_nt(a, b):
    return lax.dot_general(a, b, (((1,), (1,)), ((), ())), preferred_element_type=F32)


def _rms_unit(x):
    return x * lax.rsqrt(jnp.mean(x * x, axis=-1, keepdims=True) + EPS)


def _silu(g):
    return g / (1.0 + jnp.exp(-g))


def _head_rms(y, gain, bseg):
    parts = []
    for c in range(y.shape[1] // MXU_DIM):
        yc = y[:, c * MXU_DIM:(c + 1) * MXU_DIM]
        ms = _dot((yc * yc).astype(BF16), bseg)
        parts.append(yc * lax.rsqrt(ms + EPS) * gain[:, c * MXU_DIM:(c + 1) * MXU_DIM])
    return parts[0] if len(parts) == 1 else jnp.concatenate(parts, axis=1)


def _rope(y, cos_t, sin_t, first8):
    parts = []
    for c in range(y.shape[1] // LANES):
        yc = y[:, c * LANES:(c + 1) * LANES]
        partner = jnp.where(first8, pltpu.roll(yc, LANES - 8, 1), pltpu.roll(yc, 8, 1))
        parts.append(yc * cos_t + partner * sin_t)
    return parts[0] if len(parts) == 1 else jnp.concatenate(parts, axis=1)


def _pool_window_sums(u_hist):
    def back(a, k):
        return pltpu.roll(a, k, 0)

    lane = lax.broadcasted_iota(jnp.int32, (1, LANES), 1)
    s2 = u_hist + back(u_hist, 1)
    t = s2[:, LANES:]
    s4 = t + back(t, 2)
    t = s4[:, 2 * LANES:]
    s8 = t + back(t, 4)
    t = s8[:, LANES:]
    s16 = t + back(t, 8)
    tiles = [
        s2[:, :LANES],
        jnp.where(lane < 64, s2[:, LANES:2 * LANES], s4[:, :LANES]),
        s4[:, LANES:2 * LANES],
        s8[:, :LANES],
        jnp.where(lane < 64, s8[:, LANES:2 * LANES], s16[:, :LANES]),
        s16[:, LANES:],
    ]
    return jnp.concatenate(tiles, axis=1)


def _pool_mix(d, wbd_ref):
    lo, hi = MXU_DIM, 2 * MXU_DIM
    return jnp.concatenate([
        _dot(d[:, :hi], wbd_ref[:hi, :lo]),
        _dot(d, wbd_ref[:, lo:hi]),
        _dot(d[:, lo:], wbd_ref[lo:, hi:]),
    ], axis=1)


def _head_masks():
    lane = lax.broadcasted_iota(jnp.int32, (1, MXU_DIM), 1)
    return [(lane // HEAD_DIM) == j for j in range(HEADS_PER_CHUNK)]


def _first8_mask():
    lane = lax.broadcasted_iota(jnp.int32, (1, LANES), 1)
    return (lane % HEAD_DIM) < (ROT_DIM // 2)


def _stack_heads(q, hmask):
    return jnp.concatenate([jnp.where(m, q, 0.0) for m in hmask], axis=0).astype(BF16)


def _unstack_heads(o, hmask):
    m = o.shape[0] // len(hmask)
    acc = None
    for h, mask in enumerate(hmask):
        oh = jnp.where(mask, o[h * m:(h + 1) * m, :], 0.0)
        acc = oh if acc is None else acc + oh
    return acc


def _sink_columns(sinks_ref, rows_per_head):
    hrow = lax.broadcasted_iota(jnp.int32, (HEADS_PER_CHUNK * rows_per_head, 1), 0) // rows_per_head
    cols = []
    for c in range(SWA_W // MXU_DIM):
        col = jnp.zeros((HEADS_PER_CHUNK * rows_per_head, 1), F32)
        for jj in range(HEADS_PER_CHUNK):
            col = jnp.where(hrow == jj, sinks_ref[c * HEADS_PER_CHUNK + jj], col)
        cols.append(col)
    return cols


def _softmax_weights(s, sink):
    m = jnp.max(s, axis=-1, keepdims=True)
    if sink is not None:
        m = jnp.maximum(m, sink)
    p = jnp.exp2(s - m)
    den = jnp.sum(p, axis=-1, keepdims=True)
    if sink is not None:
        den = den + jnp.exp2(sink - m)
    return p.astype(BF16), 1.0 / den


def _mem_attn(q, k_t, v_t, hmask):
    p, rden = _softmax_weights(_dot(_stack_heads(q, hmask), k_t), None)
    return _unstack_heads(_dot_nt(p, v_t) * rden, hmask)


def _memkv_kernel(mem_ref, norm_ref, wk_ref, wv_ref, kgain_ref, mk_ref, mv_ref, mkb_ref, mvb_ref):
    xn = (_rms_unit(mem_ref[...]) * norm_ref[0]).astype(BF16)
    k_t = _dot_nt(wk_ref[0], xn)
    v_t = _dot_nt(wv_ref[0], xn)
    k3 = k_t.reshape(MEM_HEADS, HEAD_DIM, N_MEM)
    ms = jnp.mean(k3 * k3, axis=1, keepdims=True)
    k_t = (k3 * lax.rsqrt(ms + EPS)).reshape(MEM_W, N_MEM) * kgain_ref[0]
    mk_ref[0] = k_t
    mv_ref[0] = v_t
    mkb_ref[0] = k_t.astype(BF16)
    mvb_ref[0] = v_t.astype(BF16)


def _memkv_call(mem, mem_norm, wk_t, wv_t, kgain_col):
    const2 = lambda l: (0, 0)
    per_layer3 = lambda l: (l, 0, 0)
    out_f = jax.ShapeDtypeStruct((DEPTH, MEM_W, N_MEM), F32)
    out_b = jax.ShapeDtypeStruct((DEPTH, MEM_W, N_MEM), BF16)
    return pl.pallas_call(
        _memkv_kernel,
        grid=(DEPTH,),
        in_specs=[
            pl.BlockSpec((N_MEM, D_MODEL), const2),
            pl.BlockSpec((1, 1, D_MODEL), per_layer3),
            pl.BlockSpec((1, MEM_W, D_MODEL), per_layer3),
            pl.BlockSpec((1, MEM_W, D_MODEL), per_layer3),
            pl.BlockSpec((1, MEM_W, 1), per_layer3),
        ],
        out_specs=[pl.BlockSpec((1, MEM_W, N_MEM), per_layer3)] * 4,
        out_shape=[out_f, out_f, out_b, out_b],
        compiler_params=pltpu.CompilerParams(dimension_semantics=("arbitrary",)),
        name="memkv",
    )(mem, mem_norm, wk_t, wv_t, kgain_col)


def _swa_block(qb, kc, vc, bias, sink_cols, hmask):
    outs = []
    for g in range(GROUP):
        s = _dot_nt(_stack_heads(qb[:, g * MXU_DIM:(g + 1) * MXU_DIM], hmask), kc)
        s = (s.reshape(HEADS_PER_CHUNK, WINDOW, 2 * WINDOW) + bias[None]).reshape(HEADS_PER_CHUNK * WINDOW, 2 * WINDOW)
        p, rden = _softmax_weights(s, sink_cols[g])
        outs.append(_unstack_heads(_dot(p, vc) * rden, hmask))
    return jnp.concatenate(outs, axis=1)


def _prompt_kernel(sinks_ref, x_ref, rope_base_ref, rope_res_ref,
                   norm_a_ref, w_in_a_ref, wbd_ref, pscale_ref, wcol_ref, w_out_a_ref,
                   kv_norm_ref, w_kv_ref, kgain_ref,
                   norm_b_ref, wq_ref, wg_ref, wm_ref, qgain_ref, wos_ref, wom_ref,
                   mqg_ref, mk_ref, mv_ref, bseg_ref,
                   y_ref, pool_ref, ko_ref, vo_ref,
                   ucarry, kprev, vprev):
    tq = PROMPT_TILE
    i = pl.program_id(0)

    @pl.when(i == 0)
    def _():
        ucarry[0] = jnp.zeros((CARRY_ROWS, POOL_W), F32)
        kprev[0] = jnp.zeros((WINDOW, KV_W), BF16)
        vprev[0] = jnp.zeros((WINDOW, KV_W), BF16)

    rd = i % 2
    wr = (i + 1) % 2

    hmask = _head_masks()
    first8 = _first8_mask()
    bseg = bseg_ref[...]
    x = x_ref[...]

    base = rope_base_ref[i]
    cb, sb = base[0:1, :], base[1:2, :]
    cr, sr = rope_res_ref[0], rope_res_ref[1]
    cos_t = cb * cr - sb * sr
    sin_t = sb * cr + cb * sr

    xn = (_rms_unit(x) * norm_a_ref[...]).astype(BF16)
    u = _dot(xn, w_in_a_ref[:, 0:POOL_W])
    gp = _dot(xn, w_in_a_ref[:, POOL_W:2 * POOL_W])
    qm = _dot(xn, w_in_a_ref[:, 2 * POOL_W:2 * POOL_W + MEM_W])
    gm = _dot(xn, w_in_a_ref[:, 2 * POOL_W + MEM_W:])
    u_hist = jnp.concatenate([ucarry[rd], u], axis=0)
    win = _pool_window_sums(u_hist)[CARRY_ROWS:]
    pos = i * tq + lax.broadcasted_iota(jnp.int32, (tq, 1), 0)
    cnt = jnp.minimum((pos + 1).astype(F32), wcol_ref[...])
    d = win / cnt - u
    yp = _pool_mix(d.astype(BF16), wbd_ref) * pscale_ref[...] * _silu(gp)
    qmn = _head_rms(qm, mqg_ref[0:1, :], bseg) * Q_SCALE
    ym = _mem_attn(qmn, mk_ref[0], mv_ref[0], hmask) * _silu(gm)
    x1 = x + _dot(jnp.concatenate([yp, ym], axis=1).astype(BF16), w_out_a_ref[...])
    ucarry[wr] = u[tq - CARRY_ROWS:, :]

    r = _rms_unit(x1)
    kv = _dot((r * kv_norm_ref[...]).astype(BF16), w_kv_ref[...])
    k = _rope(_head_rms(kv[:, :KV_W], kgain_ref[...], bseg), cos_t, sin_t, first8)
    v = kv[:, KV_W:]
    k_all = jnp.concatenate([kprev[rd], k.astype(BF16)], axis=0)
    v_all = jnp.concatenate([vprev[rd], v.astype(BF16)], axis=0)

    xb = (r * norm_b_ref[...]).astype(BF16)
    zq = _dot(xb, wq_ref[...])
    gq = _dot(xb, wg_ref[...])
    qm2 = _dot(xb, wm_ref[:, 0:MEM_W])
    gm2 = _dot(xb, wm_ref[:, MEM_W:])
    q = _rope(_head_rms(zq, qgain_ref[...], bseg), cos_t, sin_t, first8) * Q_SCALE

    qi = lax.broadcasted_iota(jnp.int32, (WINDOW, 2 * WINDOW), 0)
    ci = lax.broadcasted_iota(jnp.int32, (WINDOW, 2 * WINDOW), 1)
    band_bias = jnp.where(ci > qi, jnp.where(ci <= qi + WINDOW, 0.0, NEG_INF), NEG_INF)
    sink_cols = _sink_columns(sinks_ref, WINDOW)
    ys_blocks = []
    for b in range(tq // WINDOW):
        bias = band_bias
        if b == 0:
            key_pos = ci + (i * tq - WINDOW)
            bias = jnp.where(key_pos >= 0, band_bias, NEG_INF)
        qb = q[b * WINDOW:(b + 1) * WINDOW, :]
        kc = k_all[b * WINDOW:(b + 2) * WINDOW, :]
        vc = v_all[b * WINDOW:(b + 2) * WINDOW, :]
        ys_blocks.append(_swa_block(qb, kc, vc, bias, sink_cols, hmask))
    ys = jnp.concatenate(ys_blocks, axis=0) * _silu(gq)
    qmn2 = _head_rms(qm2, mqg_ref[1:2, :], bseg) * Q_SCALE
    ym2 = _mem_attn(qmn2, mk_ref[1], mv_ref[1], hmask) * _silu(gm2)
    y_ref[...] = x1 + _dot(ys.astype(BF16), wos_ref[...]) + _dot(ym2.astype(BF16), wom_ref[...])

    kprev[wr] = k_all[tq:, :]
    vprev[wr] = v_all[tq:, :]

    @pl.when(i == pl.num_programs(0) - 1)
    def _():
        ko_ref[...] = k[tq - WINDOW:, :].T
        vo_ref[...] = v[tq - WINDOW:, :].T
        pool_ref[...] = pltpu.roll(u[tq - CARRY_ROWS:, :], CARRY_ROWS - 1, 0)[0:POOL_PAD, :]


def _const_spec(shape):
    nd = len(shape)
    return pl.BlockSpec(shape, lambda i: (0,) * nd, pipeline_mode=pl.Buffered(1))


def _prompt_call(sinks, x, rope_base, rope_res, wts, mqg, mk, mv, bseg):
    tq = PROMPT_TILE
    n = x.shape[0]
    row_spec = lambda w: pl.BlockSpec((tq, w), lambda i: (i, 0))
    in_specs = [pl.BlockSpec(memory_space=pltpu.SMEM), row_spec(D_MODEL),
                _const_spec(rope_base.shape), _const_spec(rope_res.shape)]
    in_specs += [_const_spec(w.shape) for w in wts]
    in_specs += [_const_spec(a.shape) for a in (mqg, mk, mv, bseg)]
    out_shape = [jax.ShapeDtypeStruct((n, D_MODEL), F32),
                 jax.ShapeDtypeStruct((POOL_PAD, POOL_W), F32),
                 jax.ShapeDtypeStruct((KV_W, WINDOW), F32),
                 jax.ShapeDtypeStruct((KV_W, WINDOW), F32)]
    out_specs = [row_spec(D_MODEL),
                 pl.BlockSpec((POOL_PAD, POOL_W), lambda i: (0, 0)),
                 pl.BlockSpec((KV_W, WINDOW), lambda i: (0, 0)),
                 pl.BlockSpec((KV_W, WINDOW), lambda i: (0, 0))]
    return pl.pallas_call(
        _prompt_kernel,
        grid=(n // tq,),
        in_specs=in_specs,
        out_specs=out_specs,
        out_shape=out_shape,
        scratch_shapes=[pltpu.VMEM((2, CARRY_ROWS, POOL_W), F32),
                        pltpu.VMEM((2, WINDOW, KV_W), BF16),
                        pltpu.VMEM((2, WINDOW, KV_W), BF16)],
        compiler_params=pltpu.CompilerParams(dimension_semantics=("arbitrary",),
                                             vmem_limit_bytes=VMEM_LIMIT),
        name="prompt",
    )(sinks, x, rope_base, rope_res, *wts, mqg, mk, mv, bseg)


def _stack_heads_seq(q, n_seq, hmask):
    q3 = q.reshape(n_seq, DEC_SEQ, MXU_DIM)
    return jnp.stack([jnp.where(m, q3, 0.0) for m in hmask], axis=1)


def _unstack_heads_seq(o, n_seq, hmask):
    o4 = o.reshape(n_seq, HEADS_PER_CHUNK, DEC_SEQ, MXU_DIM)
    acc = None
    for h, mask in enumerate(hmask):
        oh = jnp.where(mask, o4[:, h], 0.0)
        acc = oh if acc is None else acc + oh
    return acc.reshape(n_seq * DEC_SEQ, MXU_DIM)


def _mem_attn_seqs(q, k_ref, v_ref, layer, n_seq, hmask):
    rows = HEADS_PER_CHUNK * DEC_SEQ
    qs = _stack_heads_seq(q, n_seq, hmask).reshape(n_seq * rows, MXU_DIM).astype(BF16)
    s = jnp.concatenate([_dot(qs[b * rows:(b + 1) * rows], k_ref[layer, b].astype(BF16))
                         for b in range(n_seq)], axis=0)
    p, rden = _softmax_weights(s, None)
    o = jnp.concatenate([_dot_nt(p[b * rows:(b + 1) * rows], v_ref[layer, b].astype(BF16))
                         for b in range(n_seq)], axis=0)
    return _unstack_heads_seq(o * rden, n_seq, hmask)


def _sample_kernel(sinks_ref, x_ref, pref_ref, ck_ref, cv_ref, cmk_ref, cmv_ref, cos_ref, sin_ref,
                   norm_a_ref, w_in_a_ref, wbd_ref, pscale_ref, wcol_ref, w_out_a_ref,
                   kv_norm_ref, w_kv_ref, kgain_ref,
                   norm_b_ref, wq_ref, wg_ref, wm_ref, qgain_ref, wos_ref, wom_ref,
                   mqg_ref, bseg_ref,
                   y_ref, pool_ref, ko_ref, vo_ref):
    sb = SAMPLE_BLOCK
    m = sb * DEC_SEQ
    hmask = _head_masks()
    first8 = _first8_mask()
    bseg = bseg_ref[...]
    cos_t = cos_ref[...]
    sin_t = sin_ref[...]
    x = x_ref[...].reshape(m, D_MODEL)

    xn = (_rms_unit(x) * norm_a_ref[...]).astype(BF16)
    u = _dot(xn, w_in_a_ref[:, 0:POOL_W])
    gp = _dot(xn, w_in_a_ref[:, POOL_W:2 * POOL_W])
    qm = _dot(xn, w_in_a_ref[:, 2 * POOL_W:2 * POOL_W + MEM_W])
    gm = _dot(xn, w_in_a_ref[:, 2 * POOL_W + MEM_W:])

    hist_rows = CARRY_ROWS + DEC_SEQ
    u_hist = jnp.concatenate([pref_ref[...], u.reshape(sb, DEC_SEQ, POOL_W)], axis=1)
    u_hist = u_hist.reshape(sb * hist_rows, POOL_W)
    win = _pool_window_sums(u_hist).reshape(sb, hist_rows, POOL_W)[:, CARRY_ROWS:, :].reshape(m, POOL_W)
    tok = lax.broadcasted_iota(jnp.int32, (sb, DEC_SEQ, 1), 1).reshape(m, 1)
    cnt = jnp.minimum((PAST_LEN + tok + 1).astype(F32), wcol_ref[...])
    d = win / cnt - u
    yp = _pool_mix(d.astype(BF16), wbd_ref) * pscale_ref[...] * _silu(gp)
    shifted = pltpu.roll(u_hist, sb * hist_rows - 1, 0).reshape(sb, hist_rows, POOL_W)
    pool_ref[...] = shifted[:, SUBLANES:, :][:, 0:POOL_PAD, :]

    qmn = _head_rms(qm, mqg_ref[0:1, :], bseg) * Q_SCALE
    ym = _mem_attn_seqs(qmn, cmk_ref, cmv_ref, 0, sb, hmask) * _silu(gm)
    x1 = x + _dot(jnp.concatenate([yp, ym], axis=1).astype(BF16), w_out_a_ref[...])

    r = _rms_unit(x1)
    kv = _dot((r * kv_norm_ref[...]).astype(BF16), w_kv_ref[...])
    k = _rope(_head_rms(kv[:, :KV_W], kgain_ref[...], bseg), cos_t, sin_t, first8)
    v = kv[:, KV_W:]
    keep = WINDOW - DEC_SEQ
    key_lane = lax.broadcasted_iota(jnp.int32, (1, WINDOW), 1)
    pad_rows = jnp.zeros((WINDOW - m, KV_W), F32)
    for new_rows, c_ref, o_ref in ((k, ck_ref, ko_ref), (v, cv_ref, vo_ref)):
        new_t = jnp.concatenate([new_rows, pad_rows], axis=0).T
        for b in range(sb):
            shifted = pltpu.roll(c_ref[b], keep, 1)
            placed = pltpu.roll(new_t, (keep - b * DEC_SEQ) % WINDOW, 1)
            o_ref[b] = jnp.where(key_lane < keep, shifted, placed)

    xb = (r * norm_b_ref[...]).astype(BF16)
    zq = _dot(xb, wq_ref[...])
    gq = _dot(xb, wg_ref[...])
    qm2 = _dot(xb, wm_ref[:, 0:MEM_W])
    gm2 = _dot(xb, wm_ref[:, MEM_W:])
    q = _rope(_head_rms(zq, qgain_ref[...], bseg), cos_t, sin_t, first8) * Q_SCALE

    rows = N_Q_HEADS * DEC_SEQ
    qs = jnp.stack([_stack_heads_seq(q[:, g * MXU_DIM:(g + 1) * MXU_DIM], sb, hmask) for g in range(GROUP)], axis=1)
    qs = qs.reshape(sb * rows, KV_W).astype(BF16)
    s_old = jnp.concatenate([_dot(qs[b * rows:(b + 1) * rows], ck_ref[b].astype(BF16))
                             for b in range(sb)], axis=0)
    s_new = _dot_nt(qs, k.astype(BF16))
    tq_old = lax.broadcasted_iota(jnp.int32, (rows, WINDOW), 0) % DEC_SEQ
    key_old = lax.broadcasted_iota(jnp.int32, (rows, WINDOW), 1)
    bias_old = jnp.where(key_old > tq_old, 0.0, NEG_INF)
    row_i = lax.broadcasted_iota(jnp.int32, (sb * rows, m), 0)
    col_i = lax.broadcasted_iota(jnp.int32, (sb * rows, m), 1)
    same_seq = (row_i // rows) == (col_i // DEC_SEQ)
    bias_new = jnp.where(same_seq, jnp.where(col_i % DEC_SEQ <= row_i % DEC_SEQ, 0.0, NEG_INF), NEG_INF)
    s_old = (s_old.reshape(sb, rows, WINDOW) + bias_old[None]).reshape(sb * rows, WINDOW)
    s_new = s_new + bias_new
    sink = jnp.concatenate(_sink_columns(sinks_ref, DEC_SEQ), axis=0)
    sink = jnp.concatenate([sink] * sb, axis=0)
    mx = jnp.maximum(jnp.maximum(jnp.max(s_old, axis=-1, keepdims=True), jnp.max(s_new, axis=-1, keepdims=True)), sink)
    p_old = jnp.exp2(s_old - mx)
    p_new = jnp.exp2(s_new - mx)
    den = (jnp.sum(p_old, axis=-1, keepdims=True) + jnp.sum(p_new, axis=-1, keepdims=True) + jnp.exp2(sink - mx))
    p_old = p_old.astype(BF16)
    o = jnp.concatenate([_dot_nt(p_old[b * rows:(b + 1) * rows], cv_ref[b].astype(BF16)) for b in range(sb)], axis=0)
    o = (o + _dot(p_new.astype(BF16), v.astype(BF16))) * (1.0 / den)
    o5 = o.reshape(sb, GROUP, HEADS_PER_CHUNK * DEC_SEQ, KV_W)
    ys = jnp.concatenate([_unstack_heads_seq(o5[:, g].reshape(sb * HEADS_PER_CHUNK * DEC_SEQ, KV_W), sb, hmask)
                          for g in range(GROUP)], axis=1) * _silu(gq)

    qmn2 = _head_rms(qm2, mqg_ref[1:2, :], bseg) * Q_SCALE
    ym2 = _mem_attn_seqs(qmn2, cmk_ref, cmv_ref, 1, sb, hmask) * _silu(gm2)
    y = x1 + _dot(ys.astype(BF16), wos_ref[...]) + _dot(ym2.astype(BF16), wom_ref[...])
    y_ref[...] = y.reshape(sb, DEC_SEQ, D_MODEL)


def _sample_call(sinks, x, pref, ck, cv, cmk_t, cmv_t, cos_t, sin_t, wts, mqg, bseg):
    sb = SAMPLE_BLOCK
    nb = x.shape[0]
    seq3 = lambda a, b_: pl.BlockSpec((sb, a, b_), lambda i: (i, 0, 0))
    in_specs = [pl.BlockSpec(memory_space=pltpu.SMEM),
                seq3(DEC_SEQ, D_MODEL), seq3(CARRY_ROWS, POOL_W), seq3(KV_W, WINDOW), seq3(KV_W, WINDOW),
                pl.BlockSpec((DEPTH, sb, MEM_W, N_MEM), lambda i: (0, i, 0, 0)),
                pl.BlockSpec((DEPTH, sb, MEM_W, N_MEM), lambda i: (0, i, 0, 0)),
                _const_spec(cos_t.shape), _const_spec(sin_t.shape)]
    in_specs += [_const_spec(w.shape) for w in wts]
    in_specs += [_const_spec(a.shape) for a in (mqg, bseg)]
    out_shape = [jax.ShapeDtypeStruct((nb, DEC_SEQ, D_MODEL), F32),
                 jax.ShapeDtypeStruct((nb, POOL_PAD, POOL_W), F32),
                 jax.ShapeDtypeStruct((nb, KV_W, WINDOW), F32),
                 jax.ShapeDtypeStruct((nb, KV_W, WINDOW), F32)]
    out_specs = [seq3(DEC_SEQ, D_MODEL), seq3(POOL_PAD, POOL_W), seq3(KV_W, WINDOW), seq3(KV_W, WINDOW)]
    return pl.pallas_call(
        _sample_kernel,
        grid=(nb // sb,),
        in_specs=in_specs,
        out_specs=out_specs,
        out_shape=out_shape,
        compiler_params=pltpu.CompilerParams(dimension_semantics=("arbitrary",),
                                             vmem_limit_bytes=VMEM_LIMIT),
        name="sample",
    )(sinks, x, pref, ck, cv, cmk_t, cmv_t, cos_t, sin_t, *wts, mqg, bseg)


def _rope_lane_tables(pos):
    half = ROT_DIM // 2
    inv = ROPE_THETA ** (-jnp.arange(half, dtype=F32) * 2.0 / ROT_DIM)
    ang = pos.astype(F32)[:, None] * inv[None, :]
    cos, sin = jnp.cos(ang), jnp.sin(ang)
    t = pos.shape[0]
    rest = HEAD_DIM - ROT_DIM
    cos64 = jnp.concatenate([cos, cos, jnp.ones((t, rest), F32)], axis=1)
    sin64 = jnp.concatenate([-sin, sin, jnp.zeros((t, rest), F32)], axis=1)
    reps = LANES // HEAD_DIM
    return jnp.tile(cos64, (1, reps)), jnp.tile(sin64, (1, reps))


def _head_tile(g, width):
    return jnp.tile(g.astype(F32), width // HEAD_DIM).reshape(1, width)


def kernel(x_prompt, x_sample, state_pool, cache_swa_k, cache_swa_v, cache_mem_k, cache_mem_v, mem_prompt,
           norm_a, w_in_a, pool_mix_w, pool_scale, w_out_a, kv_norm, w_kv, k_norm,
           norm_b, w_in_b, q_norm, sinks, w_out_b, mem_norm, w_mem_kv, mem_q_norm, mem_k_norm):
    seg = jnp.arange(MXU_DIM) // HEAD_DIM
    bseg = ((seg[:, None] == seg[None, :]).astype(F32) / HEAD_DIM).astype(BF16)
    wcol = jnp.repeat(jnp.asarray(POOL_WINDOWS, F32), POOL_GW).reshape(1, POOL_W)

    def cols_group_major(w):
        return w.reshape(-1, N_KV_HEADS, GROUP, HEAD_DIM).transpose(0, 2, 1, 3).reshape(-1, SWA_W)

    wq = cols_group_major(w_in_b[0][:, :SWA_W]).astype(BF16)
    wg = cols_group_major(w_in_b[0][:, SWA_W:2 * SWA_W]).astype(BF16)
    wm = w_in_b[0][:, 2 * SWA_W:].astype(BF16)
    wos = w_out_b[0][:SWA_W].reshape(N_KV_HEADS, GROUP, HEAD_DIM, D_MODEL).transpose(1, 0, 2, 3).reshape(
        SWA_W, D_MODEL).astype(BF16)
    wom = w_out_b[0][SWA_W:].astype(BF16)
    sinks_p = sinks[0].astype(F32).reshape(N_KV_HEADS, GROUP).T.reshape(N_Q_HEADS) * LOG2E

    n_grp = len(POOL_WINDOWS)
    wbd = (pool_mix_w[0][:, :, None, :] * jnp.eye(n_grp, dtype=F32)[:, None, :, None]).reshape(
        POOL_W, POOL_W).astype(BF16)
    wts = (norm_a[0].reshape(1, D_MODEL), w_in_a[0].astype(BF16), wbd, pool_scale[0].reshape(1, POOL_W), wcol,
           w_out_a[0].astype(BF16),
           kv_norm.reshape(1, D_MODEL), w_kv.astype(BF16), _head_tile(k_norm, KV_W),
           norm_b[0].reshape(1, D_MODEL), wq, wg, wm, _head_tile(q_norm[0], SWA_W), wos, wom)
    mqg = jnp.concatenate([_head_tile(mem_q_norm[l], MEM_W) for l in range(DEPTH)], axis=0)
    mkg_col = jnp.stack([jnp.tile(mem_k_norm[l].astype(F32), MEM_HEADS).reshape(MEM_W, 1) for l in range(DEPTH)])

    wk_t = jnp.swapaxes(w_mem_kv[:, :, :MEM_W], 1, 2).astype(BF16)
    wv_t = jnp.swapaxes(w_mem_kv[:, :, MEM_W:], 1, 2).astype(BF16)
    mk_t, mv_t, mk_b, mv_b = _memkv_call(mem_prompt[0], mem_norm.reshape(DEPTH, 1, D_MODEL), wk_t, wv_t, mkg_col)

    n_tiles = SEQ // PROMPT_TILE
    cb, sb_ = _rope_lane_tables(jnp.arange(n_tiles) * PROMPT_TILE)
    cr, sr = _rope_lane_tables(jnp.arange(PROMPT_TILE))
    rope_base = jnp.stack([cb, sb_], axis=1)
    rope_res = jnp.stack([cr, sr], axis=0)
    y_p, pool_p, k_p, v_p = _prompt_call(sinks_p, x_prompt[0], rope_base, rope_res, wts, mqg, mk_b, mv_b, bseg)

    cos_s, sin_s = _rope_lane_tables(PAST_LEN + jnp.arange(DEC_SEQ))
    cos_s = jnp.tile(cos_s, (SAMPLE_BLOCK, 1))
    sin_s = jnp.tile(sin_s, (SAMPLE_BLOCK, 1))
    pref = jnp.pad(state_pool[0], ((0, 0), (CARRY_ROWS - POOL_PAD, 0), (0, 0)))
    cmk_t = jnp.transpose(cache_mem_k, (0, 1, 3, 4, 2)).reshape(DEPTH, DEC_BATCH, MEM_W, N_MEM)
    cmv_t = jnp.transpose(cache_mem_v, (0, 1, 3, 4, 2)).reshape(DEPTH, DEC_BATCH, MEM_W, N_MEM)
    ck_t = jnp.transpose(cache_swa_k, (0, 2, 3, 1)).reshape(DEC_BATCH, KV_W, WINDOW)
    cv_t = jnp.transpose(cache_swa_v, (0, 2, 3, 1)).reshape(DEC_BATCH, KV_W, WINDOW)
    y_s, pool_s, k_s, v_s = _sample_call(sinks_p, x_sample, pref, ck_t, cv_t, cmk_t, cmv_t, cos_s, sin_s,
                                         wts, mqg, bseg)

    def mem_out(a):
        return jnp.transpose(a.reshape(DEPTH, 1, MEM_HEADS, HEAD_DIM, N_MEM), (0, 1, 4, 2, 3))

    def swa_out_t(a):
        return jnp.transpose(a.reshape(-1, N_KV_HEADS, HEAD_DIM, WINDOW), (0, 3, 1, 2))

    return (y_p[None], y_s, pool_p[None, None], pool_s[None],
            swa_out_t(k_p[None]), swa_out_t(v_p[None]), swa_out_t(k_s), swa_out_t(v_s),
            mem_out(mk_t), mem_out(mv_t))
```

```python
import jax
import jax.numpy as jnp
import numpy as np
from jax import lax
from jax.experimental import pallas as pl
from jax.experimental.pallas import tpu as pltpu

D_MODEL = 1024
SEQ = 16384
DEPTH = 2
DEC_BATCH = 128
DEC_SEQ = 8
PAST_LEN = 16384
HEAD_DIM = 64
POOL_W = 768
POOL_WINDOWS = (2, 4, 8, 16)
POOL_GW = 192
POOL_PAD = 15
N_Q_HEADS = 12
N_KV_HEADS = 4
GROUP = 3
SWA_W = 768
KV_W = 256
WINDOW = 128
N_MEM = 256
MEM_HEADS = 4
MEM_W = 256
ROT_DIM = 16
ROPE_THETA = 500000.0
EPS = 1e-6

F32 = jnp.float32
BF16 = jnp.bfloat16
NEG_INF = float("-inf")
LOG2E = 1.4426950408889634
Q_SCALE = HEAD_DIM ** -0.5 * LOG2E

LANES = 128
SUBLANES = 8
MXU_DIM = 256
HEADS_PER_CHUNK = MXU_DIM // HEAD_DIM
CARRY_ROWS = 16
PROMPT_TILE = 512
SAMPLE_BLOCK = 8
VMEM_LIMIT = 56 * 1024 * 1024


_VEC_WIDTHS = (("norm_a", D_MODEL), ("pool_scale", POOL_W), ("kv_norm", D_MODEL), ("k_gain", KV_W),
               ("norm_b", D_MODEL), ("q_gain", SWA_W), ("mem_q_gain0", MEM_W), ("mem_q_gain1", MEM_W))
_VEC_OFFSET = {}
for _name, _width in _VEC_WIDTHS:
    _VEC_OFFSET[_name] = (sum(w for _, w in _VEC_WIDTHS[:len(_VEC_OFFSET)]), _width)


def _vec(vec_ref, name):
    start, width = _VEC_OFFSET[name]
    return vec_ref[:, start:start + width]


def _dot(a, b):
    return jnp.dot(a, b, preferred_element_type=F32)


def _dot_nt(a, b):
    return lax.dot_general(a, b, (((1,), (1,)), ((), ())), preferred_element_type=F32)


def _rms_unit(x):
    return x * lax.rsqrt(jnp.mean(x * x, axis=-1, keepdims=True) + EPS)


def _silu(g):
    return g / (1.0 + jnp.exp(-g))


def _head_rms(y, gain, bseg):
    parts = []
    for c in range(y.shape[1] // MXU_DIM):
        yc = y[:, c * MXU_DIM:(c + 1) * MXU_DIM]
        ms = _dot((yc * yc).astype(BF16), bseg)
        parts.append(yc * lax.rsqrt(ms + EPS) * gain[:, c * MXU_DIM:(c + 1) * MXU_DIM])
    return parts[0] if len(parts) == 1 else jnp.concatenate(parts, axis=1)


def _rope(y, cos_t, sin_t, first8):
    parts = []
    for c in range(y.shape[1] // LANES):
        yc = y[:, c * LANES:(c + 1) * LANES]
        partner = jnp.where(first8, pltpu.roll(yc, LANES - 8, 1), pltpu.roll(yc, 8, 1))
        parts.append(yc * cos_t + partner * sin_t)
    return parts[0] if len(parts) == 1 else jnp.concatenate(parts, axis=1)


def _pool_window_sums(u_hist):
    def back(a, k):
        return pltpu.roll(a, k, 0)

    lane = lax.broadcasted_iota(jnp.int32, (1, LANES), 1)
    s2 = u_hist + back(u_hist, 1)
    t = s2[:, LANES:]
    s4 = t + back(t, 2)
    t = s4[:, 2 * LANES:]
    s8 = t + back(t, 4)
    t = s8[:, LANES:]
    s16 = t + back(t, 8)
    tiles = [
        s2[:, :LANES],
        jnp.where(lane < 64, s2[:, LANES:2 * LANES], s4[:, :LANES]),
        s4[:, LANES:2 * LANES],
        s8[:, :LANES],
        jnp.where(lane < 64, s8[:, LANES:2 * LANES], s16[:, :LANES]),
        s16[:, LANES:],
    ]
    return jnp.concatenate(tiles, axis=1)


def _pool_window_sums_planes(planes):
    n = len(planes)

    def doubled(prev, k, first):
        return [prev[j] + prev[j - k] if j >= first else None for j in range(n)]

    lane = lax.broadcasted_iota(jnp.int32, (1, LANES), 1)
    s2 = doubled(planes, 1, 1)
    s4 = doubled([None if a is None else a[:, LANES:] for a in s2], 2, 3)
    s8 = doubled([None if a is None else a[:, 2 * LANES:] for a in s4], 4, 7)
    s16 = doubled([None if a is None else a[:, LANES:] for a in s8], 8, POOL_PAD)
    out = []
    for j in range(POOL_PAD, n):
        out.append(jnp.concatenate([
            s2[j][:, :LANES],
            jnp.where(lane < 64, s2[j][:, LANES:2 * LANES], s4[j][:, :LANES]),
            s4[j][:, LANES:2 * LANES],
            s8[j][:, :LANES],
            jnp.where(lane < 64, s8[j][:, LANES:2 * LANES], s16[j][:, :LANES]),
            s16[j][:, LANES:],
        ], axis=1))
    return out


def _pool_mix(d, wbd_ref):
    lo, hi = MXU_DIM, 2 * MXU_DIM
    return jnp.concatenate([
        _dot(d[:, :hi], wbd_ref[:hi, :lo]),
        _dot(d, wbd_ref[:, lo:hi]),
        _dot(d[:, lo:], wbd_ref[lo:, hi:]),
    ], axis=1)


def _head_masks():
    lane = lax.broadcasted_iota(jnp.int32, (1, MXU_DIM), 1)
    return [(lane // HEAD_DIM) == j for j in range(HEADS_PER_CHUNK)]


def _first8_mask():
    lane = lax.broadcasted_iota(jnp.int32, (1, LANES), 1)
    return (lane % HEAD_DIM) < (ROT_DIM // 2)


def _stack_heads(q, hmask):
    return jnp.concatenate([jnp.where(m, q, 0.0) for m in hmask], axis=0).astype(BF16)


def _unstack_heads(o, hmask):
    m = o.shape[0] // len(hmask)
    acc = None
    for h, mask in enumerate(hmask):
        oh = jnp.where(mask, o[h * m:(h + 1) * m, :], 0.0)
        acc = oh if acc is None else acc + oh
    return acc


def _sink_columns(sinks_ref, rows_per_head):
    hrow = lax.broadcasted_iota(jnp.int32, (HEADS_PER_CHUNK * rows_per_head, 1), 0) // rows_per_head
    cols = []
    for c in range(SWA_W // MXU_DIM):
        col = jnp.zeros((HEADS_PER_CHUNK * rows_per_head, 1), F32)
        for jj in range(HEADS_PER_CHUNK):
            col = jnp.where(hrow == jj, sinks_ref[c * HEADS_PER_CHUNK + jj], col)
        cols.append(col)
    return cols


def _softmax_weights(s, sink):
    m = jnp.max(s, axis=-1, keepdims=True)
    if sink is not None:
        m = jnp.maximum(m, sink)
    p = jnp.exp2(s - m)
    den = jnp.sum(p, axis=-1, keepdims=True)
    if sink is not None:
        den = den + jnp.exp2(sink - m)
    return p.astype(BF16), 1.0 / den


def _mem_attn(q, k_t, v_t, hmask):
    p, rden = _softmax_weights(_dot(_stack_heads(q, hmask), k_t), None)
    return _unstack_heads(_dot_nt(p, v_t) * rden, hmask)


def _memkv_kernel(mem_ref, norm_ref, wk_ref, wv_ref, kgain_ref, mk_ref, mv_ref, mkb_ref, mvb_ref):
    xn = (_rms_unit(mem_ref[...]) * norm_ref[0]).astype(BF16)
    k_t = _dot_nt(wk_ref[0], xn)
    v_t = _dot_nt(wv_ref[0], xn)
    k3 = k_t.reshape(MEM_HEADS, HEAD_DIM, N_MEM)
    ms = jnp.mean(k3 * k3, axis=1, keepdims=True)
    k_t = (k3 * lax.rsqrt(ms + EPS)).reshape(MEM_W, N_MEM) * kgain_ref[0]
    mk_ref[0] = k_t
    mv_ref[0] = v_t
    mkb_ref[0] = k_t.astype(BF16)
    mvb_ref[0] = v_t.astype(BF16)


def _memkv_call(mem, mem_norm, wk_t, wv_t, kgain_col):
    const2 = lambda l: (0, 0)
    per_layer3 = lambda l: (l, 0, 0)
    out_f = jax.ShapeDtypeStruct((DEPTH, MEM_W, N_MEM), F32)
    out_b = jax.ShapeDtypeStruct((DEPTH, MEM_W, N_MEM), BF16)
    return pl.pallas_call(
        _memkv_kernel,
        grid=(DEPTH,),
        in_specs=[
            pl.BlockSpec((N_MEM, D_MODEL), const2),
            pl.BlockSpec((1, 1, D_MODEL), per_layer3),
            pl.BlockSpec((1, MEM_W, D_MODEL), per_layer3),
            pl.BlockSpec((1, MEM_W, D_MODEL), per_layer3),
            pl.BlockSpec((1, MEM_W, 1), per_layer3),
        ],
        out_specs=[pl.BlockSpec((1, MEM_W, N_MEM), per_layer3)] * 4,
        out_shape=[out_f, out_f, out_b, out_b],
        compiler_params=pltpu.CompilerParams(dimension_semantics=("arbitrary",)),
        name="memkv",
    )(mem, mem_norm, wk_t, wv_t, kgain_col)


def _swa_block(qb, kc, vc, bias, sink_cols, hmask):
    outs = []
    for g in range(GROUP):
        s = _dot_nt(_stack_heads(qb[:, g * MXU_DIM:(g + 1) * MXU_DIM], hmask), kc)
        s = (s.reshape(HEADS_PER_CHUNK, WINDOW, 2 * WINDOW) + bias[None]).reshape(HEADS_PER_CHUNK * WINDOW, 2 * WINDOW)
        p, rden = _softmax_weights(s, sink_cols[g])
        outs.append(_unstack_heads(_dot(p, vc) * rden, hmask))
    return jnp.concatenate(outs, axis=1)


def _prompt_kernel(sinks_ref, x_ref, rope_base_ref, rope_res_ref,
                   vec_ref, w_in_a_ref, wbd_ref, wcol_ref, w_out_a_ref, w_kv_ref,
                   wq_ref, wg_ref, wm_ref, wos_ref, wom_ref,
                   mk_ref, mv_ref, bseg_ref,
                   y_ref, pool_ref, ko_ref, vo_ref,
                   ucarry, kprev, vprev):
    tq = PROMPT_TILE
    i = pl.program_id(0)

    @pl.when(i == 0)
    def _():
        ucarry[0] = jnp.zeros((CARRY_ROWS, POOL_W), F32)
        kprev[0] = jnp.zeros((WINDOW, KV_W), BF16)
        vprev[0] = jnp.zeros((WINDOW, KV_W), BF16)

    rd = i % 2
    wr = (i + 1) % 2

    hmask = _head_masks()
    first8 = _first8_mask()
    bseg = bseg_ref[...]
    x = x_ref[...]

    base = rope_base_ref[i]
    cb, sb = base[0:1, :], base[1:2, :]
    cr, sr = rope_res_ref[0], rope_res_ref[1]
    cos_t = cb * cr - sb * sr
    sin_t = sb * cr + cb * sr

    xn = (_rms_unit(x) * _vec(vec_ref, "norm_a")).astype(BF16)
    u = _dot(xn, w_in_a_ref[:, 0:POOL_W])
    gp = _dot(xn, w_in_a_ref[:, POOL_W:2 * POOL_W])
    qm = _dot(xn, w_in_a_ref[:, 2 * POOL_W:2 * POOL_W + MEM_W])
    gm = _dot(xn, w_in_a_ref[:, 2 * POOL_W + MEM_W:])
    u_hist = jnp.concatenate([ucarry[rd], u], axis=0)
    win = _pool_window_sums(u_hist)[CARRY_ROWS:]
    pos = i * tq + lax.broadcasted_iota(jnp.int32, (tq, 1), 0)
    cnt = jnp.minimum((pos + 1).astype(F32), wcol_ref[...])
    d = win / cnt - u
    yp = _pool_mix(d.astype(BF16), wbd_ref) * _vec(vec_ref, "pool_scale") * _silu(gp)
    qmn = _head_rms(qm, _vec(vec_ref, "mem_q_gain0"), bseg) * Q_SCALE
    ym = _mem_attn(qmn, mk_ref[0], mv_ref[0], hmask) * _silu(gm)
    x1 = x + _dot(jnp.concatenate([yp, ym], axis=1).astype(BF16), w_out_a_ref[...])
    ucarry[wr] = u[tq - CARRY_ROWS:, :]

    r = _rms_unit(x1)
    kv = _dot((r * _vec(vec_ref, "kv_norm")).astype(BF16), w_kv_ref[...])
    k = _rope(_head_rms(kv[:, :KV_W], _vec(vec_ref, "k_gain"), bseg), cos_t, sin_t, first8)
    v = kv[:, KV_W:]
    k_all = jnp.concatenate([kprev[rd], k.astype(BF16)], axis=0)
    v_all = jnp.concatenate([vprev[rd], v.astype(BF16)], axis=0)

    xb = (r * _vec(vec_ref, "norm_b")).astype(BF16)
    zq = _dot(xb, wq_ref[...])
    gq = _dot(xb, wg_ref[...])
    qm2 = _dot(xb, wm_ref[:, 0:MEM_W])
    gm2 = _dot(xb, wm_ref[:, MEM_W:])
    q = _rope(_head_rms(zq, _vec(vec_ref, "q_gain"), bseg), cos_t, sin_t, first8) * Q_SCALE

    qi = lax.broadcasted_iota(jnp.int32, (WINDOW, 2 * WINDOW), 0)
    ci = lax.broadcasted_iota(jnp.int32, (WINDOW, 2 * WINDOW), 1)
    band_bias = jnp.where(ci > qi, jnp.where(ci <= qi + WINDOW, 0.0, NEG_INF), NEG_INF)
    sink_cols = _sink_columns(sinks_ref, WINDOW)
    ys_blocks = []
    for b in range(tq // WINDOW):
        bias = band_bias
        if b == 0:
            key_pos = ci + (i * tq - WINDOW)
            bias = jnp.where(key_pos >= 0, band_bias, NEG_INF)
        qb = q[b * WINDOW:(b + 1) * WINDOW, :]
        kc = k_all[b * WINDOW:(b + 2) * WINDOW, :]
        vc = v_all[b * WINDOW:(b + 2) * WINDOW, :]
        ys_blocks.append(_swa_block(qb, kc, vc, bias, sink_cols, hmask))
    ys = jnp.concatenate(ys_blocks, axis=0) * _silu(gq)
    qmn2 = _head_rms(qm2, _vec(vec_ref, "mem_q_gain1"), bseg) * Q_SCALE
    ym2 = _mem_attn(qmn2, mk_ref[1], mv_ref[1], hmask) * _silu(gm2)
    y_ref[...] = x1 + _dot(ys.astype(BF16), wos_ref[...]) + _dot(ym2.astype(BF16), wom_ref[...])

    kprev[wr] = k_all[tq:, :]
    vprev[wr] = v_all[tq:, :]

    @pl.when(i == pl.num_programs(0) - 1)
    def _():
        ko_ref[...] = k[tq - WINDOW:, :].T
        vo_ref[...] = v[tq - WINDOW:, :].T
        pool_ref[...] = pltpu.roll(u[tq - CARRY_ROWS:, :], CARRY_ROWS - 1, 0)[0:POOL_PAD, :]


def _const_spec(shape):
    nd = len(shape)
    return pl.BlockSpec(shape, lambda i: (0,) * nd, pipeline_mode=pl.Buffered(1))


def _prompt_call(sinks, x, rope_base, rope_res, wts, mk, mv, bseg):
    tq = PROMPT_TILE
    n = x.shape[0]
    row_spec = lambda w: pl.BlockSpec((tq, w), lambda i: (i, 0))
    in_specs = [pl.BlockSpec(memory_space=pltpu.SMEM), row_spec(D_MODEL),
                _const_spec(rope_base.shape), _const_spec(rope_res.shape)]
    in_specs += [_const_spec(w.shape) for w in wts]
    in_specs += [_const_spec(a.shape) for a in (mk, mv, bseg)]
    out_shape = [jax.ShapeDtypeStruct((n, D_MODEL), F32),
                 jax.ShapeDtypeStruct((POOL_PAD, POOL_W), F32),
                 jax.ShapeDtypeStruct((KV_W, WINDOW), F32),
                 jax.ShapeDtypeStruct((KV_W, WINDOW), F32)]
    out_specs = [row_spec(D_MODEL),
                 pl.BlockSpec((POOL_PAD, POOL_W), lambda i: (0, 0)),
                 pl.BlockSpec((KV_W, WINDOW), lambda i: (0, 0)),
                 pl.BlockSpec((KV_W, WINDOW), lambda i: (0, 0))]
    return pl.pallas_call(
        _prompt_kernel,
        grid=(n // tq,),
        in_specs=in_specs,
        out_specs=out_specs,
        out_shape=out_shape,
        scratch_shapes=[pltpu.VMEM((2, CARRY_ROWS, POOL_W), F32),
                        pltpu.VMEM((2, WINDOW, KV_W), BF16),
                        pltpu.VMEM((2, WINDOW, KV_W), BF16)],
        compiler_params=pltpu.CompilerParams(dimension_semantics=("arbitrary",),
                                             vmem_limit_bytes=VMEM_LIMIT),
        name="prompt",
    )(sinks, x, rope_base, rope_res, *wts, mk, mv, bseg)


def _stack_heads_seq(q, n_seq, hmask):
    q3 = q.reshape(n_seq, DEC_SEQ, MXU_DIM)
    return jnp.stack([jnp.where(m, q3, 0.0) for m in hmask], axis=1)


def _unstack_heads_seq(o, n_seq, hmask):
    o4 = o.reshape(n_seq, HEADS_PER_CHUNK, DEC_SEQ, MXU_DIM)
    acc = None
    for h, mask in enumerate(hmask):
        oh = jnp.where(mask, o4[:, h], 0.0)
        acc = oh if acc is None else acc + oh
    return acc.reshape(n_seq * DEC_SEQ, MXU_DIM)


def _mem_attn_seqs(q, k_ref, v_ref, layer, n_seq, hmask):
    rows = HEADS_PER_CHUNK * DEC_SEQ
    qs = _stack_heads_seq(q, n_seq, hmask).reshape(n_seq * rows, MXU_DIM).astype(BF16)
    s = jnp.concatenate([_dot(qs[b * rows:(b + 1) * rows], k_ref[layer, b].astype(BF16))
                         for b in range(n_seq)], axis=0)
    p, rden = _softmax_weights(s, None)
    o = jnp.concatenate([_dot_nt(p[b * rows:(b + 1) * rows], v_ref[layer, b].astype(BF16))
                         for b in range(n_seq)], axis=0)
    return _unstack_heads_seq(o * rden, n_seq, hmask)


def _sample_kernel(sinks_ref, x_ref, pref_ref, ck_ref, cv_ref, cmk_ref, cmv_ref, cos_ref, sin_ref,
                   vec_ref, w_in_a_ref, wbd_ref, wcol_ref, w_out_a_ref, w_kv_ref,
                   wq_ref, wg_ref, wm_ref, wos_ref, wom_ref,
                   bseg_ref,
                   y_ref, pool_ref, ko_ref, vo_ref,
                   u_scr, d_scr):
    sb = SAMPLE_BLOCK
    m = sb * DEC_SEQ
    hmask = _head_masks()
    first8 = _first8_mask()
    bseg = bseg_ref[...]
    cos_t = cos_ref[...]
    sin_t = sin_ref[...]
    x = x_ref[...].reshape(m, D_MODEL)

    xn = (_rms_unit(x) * _vec(vec_ref, "norm_a")).astype(BF16)
    u = _dot(xn, w_in_a_ref[:, 0:POOL_W])
    gp = _dot(xn, w_in_a_ref[:, POOL_W:2 * POOL_W])
    qm = _dot(xn, w_in_a_ref[:, 2 * POOL_W:2 * POOL_W + MEM_W])
    gm = _dot(xn, w_in_a_ref[:, 2 * POOL_W + MEM_W:])

    n_lt = POOL_W // LANES
    for c in range(n_lt):
        u_scr[c] = u[:, c * LANES:(c + 1) * LANES]
    planes = [pref_ref[t] for t in range(POOL_PAD)]
    planes += [jnp.concatenate([u_scr[c, pl.ds(t, sb, stride=DEC_SEQ), :] for c in range(n_lt)], axis=1)
               for t in range(DEC_SEQ)]
    for t, win in enumerate(_pool_window_sums_planes(planes)):
        cnt = jnp.minimum(float(PAST_LEN + t + 1), wcol_ref[...])
        d_t = win / cnt - planes[POOL_PAD + t]
        for c in range(n_lt):
            d_scr[c, pl.ds(t, sb, stride=DEC_SEQ), :] = d_t[:, c * LANES:(c + 1) * LANES]
    d = jnp.concatenate([d_scr[c] for c in range(n_lt)], axis=1)
    yp = _pool_mix(d.astype(BF16), wbd_ref) * _vec(vec_ref, "pool_scale") * _silu(gp)
    for t in range(POOL_PAD):
        pool_ref[t] = planes[DEC_SEQ + t]

    qmn = _head_rms(qm, _vec(vec_ref, "mem_q_gain0"), bseg) * Q_SCALE
    ym = _mem_attn_seqs(qmn, cmk_ref, cmv_ref, 0, sb, hmask) * _silu(gm)
    x1 = x + _dot(jnp.concatenate([yp, ym], axis=1).astype(BF16), w_out_a_ref[...])

    r = _rms_unit(x1)
    kv = _dot((r * _vec(vec_ref, "kv_norm")).astype(BF16), w_kv_ref[...])
    k = _rope(_head_rms(kv[:, :KV_W], _vec(vec_ref, "k_gain"), bseg), cos_t, sin_t, first8)
    v = kv[:, KV_W:]
    keep = WINDOW - DEC_SEQ
    key_lane = lax.broadcasted_iota(jnp.int32, (1, WINDOW), 1)
    pad_rows = jnp.zeros((WINDOW - m, KV_W), F32)
    for new_rows, c_ref, o_ref in ((k, ck_ref, ko_ref), (v, cv_ref, vo_ref)):
        new_t = jnp.concatenate([new_rows, pad_rows], axis=0).T
        for b in range(sb):
            shifted = pltpu.roll(c_ref[b], keep, 1)
            placed = pltpu.roll(new_t, (keep - b * DEC_SEQ) % WINDOW, 1)
            o_ref[b] = jnp.where(key_lane < keep, shifted, placed)

    xb = (r * _vec(vec_ref, "norm_b")).astype(BF16)
    zq = _dot(xb, wq_ref[...])
    gq = _dot(xb, wg_ref[...])
    qm2 = _dot(xb, wm_ref[:, 0:MEM_W])
    gm2 = _dot(xb, wm_ref[:, MEM_W:])
    q = _rope(_head_rms(zq, _vec(vec_ref, "q_gain"), bseg), cos_t, sin_t, first8) * Q_SCALE

    rows = N_Q_HEADS * DEC_SEQ
    qs = jnp.stack([_stack_heads_seq(q[:, g * MXU_DIM:(g + 1) * MXU_DIM], sb, hmask) for g in range(GROUP)], axis=1)
    qs = qs.reshape(sb * rows, KV_W).astype(BF16)
    s_old = jnp.concatenate([_dot(qs[b * rows:(b + 1) * rows], ck_ref[b].astype(BF16))
                             for b in range(sb)], axis=0)
    s_new = _dot_nt(qs, k.astype(BF16))
    tq_old = lax.broadcasted_iota(jnp.int32, (rows, WINDOW), 0) % DEC_SEQ
    key_old = lax.broadcasted_iota(jnp.int32, (rows, WINDOW), 1)
    bias_old = jnp.where(key_old > tq_old, 0.0, NEG_INF)
    row_i = lax.broadcasted_iota(jnp.int32, (sb * rows, m), 0)
    col_i = lax.broadcasted_iota(jnp.int32, (sb * rows, m), 1)
    same_seq = (row_i // rows) == (col_i // DEC_SEQ)
    bias_new = jnp.where(same_seq, jnp.where(col_i % DEC_SEQ <= row_i % DEC_SEQ, 0.0, NEG_INF), NEG_INF)
    s_old = (s_old.reshape(sb, rows, WINDOW) + bias_old[None]).reshape(sb * rows, WINDOW)
    s_new = s_new + bias_new
    sink = jnp.concatenate(_sink_columns(sinks_ref, DEC_SEQ), axis=0)
    sink = jnp.concatenate([sink] * sb, axis=0)
    mx = jnp.maximum(jnp.maximum(jnp.max(s_old, axis=-1, keepdims=True), jnp.max(s_new, axis=-1, keepdims=True)), sink)
    p_old = jnp.exp2(s_old - mx)
    p_new = jnp.exp2(s_new - mx)
    den = (jnp.sum(p_old, axis=-1, keepdims=True) + jnp.sum(p_new, axis=-1, keepdims=True) + jnp.exp2(sink - mx))
    p_old = p_old.astype(BF16)
    o = jnp.concatenate([_dot_nt(p_old[b * rows:(b + 1) * rows], cv_ref[b].astype(BF16)) for b in range(sb)], axis=0)
    o = (o + _dot(p_new.astype(BF16), v.astype(BF16))) * (1.0 / den)
    o5 = o.reshape(sb, GROUP, HEADS_PER_CHUNK * DEC_SEQ, KV_W)
    ys = jnp.concatenate([_unstack_heads_seq(o5[:, g].reshape(sb * HEADS_PER_CHUNK * DEC_SEQ, KV_W), sb, hmask)
                          for g in range(GROUP)], axis=1) * _silu(gq)

    qmn2 = _head_rms(qm2, _vec(vec_ref, "mem_q_gain1"), bseg) * Q_SCALE
    ym2 = _mem_attn_seqs(qmn2, cmk_ref, cmv_ref, 1, sb, hmask) * _silu(gm2)
    y = x1 + _dot(ys.astype(BF16), wos_ref[...]) + _dot(ym2.astype(BF16), wom_ref[...])
    y_ref[...] = y.reshape(sb, DEC_SEQ, D_MODEL)


def _sample_call(sinks, x, pref, ck, cv, cmk_t, cmv_t, cos_t, sin_t, wts, bseg):
    sb = SAMPLE_BLOCK
    nb = x.shape[0]
    seq3 = lambda a, b_: pl.BlockSpec((sb, a, b_), lambda i: (i, 0, 0))
    pool_spec = pl.BlockSpec((POOL_PAD, sb, POOL_W), lambda i: (0, i, 0))
    in_specs = [pl.BlockSpec(memory_space=pltpu.SMEM),
                seq3(DEC_SEQ, D_MODEL), pool_spec, seq3(KV_W, WINDOW), seq3(KV_W, WINDOW),
                pl.BlockSpec((DEPTH, sb, MEM_W, N_MEM), lambda i: (0, i, 0, 0)),
                pl.BlockSpec((DEPTH, sb, MEM_W, N_MEM), lambda i: (0, i, 0, 0)),
                _const_spec(cos_t.shape), _const_spec(sin_t.shape)]
    in_specs += [_const_spec(w.shape) for w in wts]
    in_specs += [_const_spec(bseg.shape)]
    out_shape = [jax.ShapeDtypeStruct((nb, DEC_SEQ, D_MODEL), F32),
                 jax.ShapeDtypeStruct((POOL_PAD, nb, POOL_W), F32),
                 jax.ShapeDtypeStruct((nb, KV_W, WINDOW), F32),
                 jax.ShapeDtypeStruct((nb, KV_W, WINDOW), F32)]
    out_specs = [seq3(DEC_SEQ, D_MODEL), pool_spec, seq3(KV_W, WINDOW), seq3(KV_W, WINDOW)]
    return pl.pallas_call(
        _sample_kernel,
        grid=(nb // sb,),
        in_specs=in_specs,
        out_specs=out_specs,
        out_shape=out_shape,
        scratch_shapes=[pltpu.VMEM((POOL_W // LANES, sb * DEC_SEQ, LANES), F32),
                        pltpu.VMEM((POOL_W // LANES, sb * DEC_SEQ, LANES), F32)],
        compiler_params=pltpu.CompilerParams(dimension_semantics=("arbitrary",),
                                             vmem_limit_bytes=VMEM_LIMIT),
        name="sample",
    )(sinks, x, pref, ck, cv, cmk_t, cmv_t, cos_t, sin_t, *wts, bseg)


def _rope_lane_tables(pos):
    half = ROT_DIM // 2
    inv = (ROPE_THETA ** (-np.arange(half, dtype=np.float32) * 2.0 / ROT_DIM)).astype(np.float32)
    ang = (np.asarray(pos, np.float32)[:, None] * inv[None, :]).astype(np.float64)
    cos, sin = np.cos(ang), np.sin(ang)
    t = ang.shape[0]
    rest = HEAD_DIM - ROT_DIM
    cos64 = np.concatenate([cos, cos, np.ones((t, rest))], axis=1)
    sin64 = np.concatenate([-sin, sin, np.zeros((t, rest))], axis=1)
    reps = LANES // HEAD_DIM
    return np.tile(cos64, (1, reps)).astype(np.float32), np.tile(sin64, (1, reps)).astype(np.float32)


def _head_tile(g, width):
    return jnp.tile(g.astype(F32), width // HEAD_DIM)


def kernel(x_prompt, x_sample, state_pool, cache_swa_k, cache_swa_v, cache_mem_k, cache_mem_v, mem_prompt,
           norm_a, w_in_a, pool_mix_w, pool_scale, w_out_a, kv_norm, w_kv, k_norm,
           norm_b, w_in_b, q_norm, sinks, w_out_b, mem_norm, w_mem_kv, mem_q_norm, mem_k_norm):
    seg = np.arange(MXU_DIM) // HEAD_DIM
    bseg = jnp.asarray((seg[:, None] == seg[None, :]).astype(np.float32) / HEAD_DIM, BF16)
    wcol = jnp.asarray(np.repeat(np.asarray(POOL_WINDOWS, np.float32), POOL_GW).reshape(1, POOL_W))

    def cols_group_major(w):
        return w.reshape(-1, N_KV_HEADS, GROUP, HEAD_DIM).transpose(0, 2, 1, 3).reshape(-1, SWA_W)

    wq = cols_group_major(w_in_b[0][:, :SWA_W]).astype(BF16)
    wg = cols_group_major(w_in_b[0][:, SWA_W:2 * SWA_W]).astype(BF16)
    wm = w_in_b[0][:, 2 * SWA_W:].astype(BF16)
    wos = w_out_b[0][:SWA_W].reshape(N_KV_HEADS, GROUP, HEAD_DIM, D_MODEL).transpose(1, 0, 2, 3).reshape(
        SWA_W, D_MODEL).astype(BF16)
    wom = w_out_b[0][SWA_W:].astype(BF16)
    sinks_p = sinks[0].astype(F32).reshape(N_KV_HEADS, GROUP).T.reshape(N_Q_HEADS) * LOG2E

    n_grp = len(POOL_WINDOWS)
    wbd = (pool_mix_w[0][:, :, None, :] * jnp.eye(n_grp, dtype=F32)[:, None, :, None]).reshape(
        POOL_W, POOL_W).astype(BF16)
    pieces = {"norm_a": norm_a[0], "pool_scale": pool_scale[0], "kv_norm": kv_norm,
              "k_gain": _head_tile(k_norm, KV_W), "norm_b": norm_b[0], "q_gain": _head_tile(q_norm[0], SWA_W),
              "mem_q_gain0": _head_tile(mem_q_norm[0], MEM_W), "mem_q_gain1": _head_tile(mem_q_norm[1], MEM_W)}
    vecs = jnp.concatenate([pieces[name].astype(F32) for name, _ in _VEC_WIDTHS]).reshape(1, -1)
    wts = (vecs, w_in_a[0].astype(BF16), wbd, wcol, w_out_a[0].astype(BF16), w_kv.astype(BF16),
           wq, wg, wm, wos, wom)
    mkg_col = jnp.stack([jnp.tile(mem_k_norm[l].astype(F32), MEM_HEADS).reshape(MEM_W, 1) for l in range(DEPTH)])

    wk_t = jnp.swapaxes(w_mem_kv[:, :, :MEM_W], 1, 2).astype(BF16)
    wv_t = jnp.swapaxes(w_mem_kv[:, :, MEM_W:], 1, 2).astype(BF16)
    mk_t, mv_t, mk_b, mv_b = _memkv_call(mem_prompt[0], mem_norm.reshape(DEPTH, 1, D_MODEL), wk_t, wv_t, mkg_col)

    n_tiles = SEQ // PROMPT_TILE
    cb, sb_ = _rope_lane_tables(np.arange(n_tiles) * PROMPT_TILE)
    cr, sr = _rope_lane_tables(np.arange(PROMPT_TILE))
    rope_base = jnp.asarray(np.stack([cb, sb_], axis=1))
    rope_res = jnp.asarray(np.stack([cr, sr], axis=0))
    y_p, pool_p, k_p, v_p = _prompt_call(sinks_p, x_prompt[0], rope_base, rope_res, wts, mk_b, mv_b, bseg)

    cos_s, sin_s = _rope_lane_tables(PAST_LEN + np.arange(DEC_SEQ))
    cos_s = jnp.asarray(np.tile(cos_s, (SAMPLE_BLOCK, 1)))
    sin_s = jnp.asarray(np.tile(sin_s, (SAMPLE_BLOCK, 1)))
    pref = jnp.transpose(state_pool[0], (1, 0, 2))
    cmk_t = jnp.transpose(cache_mem_k, (0, 1, 3, 4, 2)).reshape(DEPTH, DEC_BATCH, MEM_W, N_MEM)
    cmv_t = jnp.transpose(cache_mem_v, (0, 1, 3, 4, 2)).reshape(DEPTH, DEC_BATCH, MEM_W, N_MEM)
    ck_t = jnp.transpose(cache_swa_k, (0, 2, 3, 1)).reshape(DEC_BATCH, KV_W, WINDOW)
    cv_t = jnp.transpose(cache_swa_v, (0, 2, 3, 1)).reshape(DEC_BATCH, KV_W, WINDOW)
    y_s, pool_s, k_s, v_s = _sample_call(sinks_p, x_sample, pref, ck_t, cv_t, cmk_t, cmv_t, cos_s, sin_s,
                                         wts, bseg)

    def mem_out(a):
        return jnp.transpose(a.reshape(DEPTH, 1, MEM_HEADS, HEAD_DIM, N_MEM), (0, 1, 4, 2, 3))

    def swa_out_t(a):
        return jnp.transpose(a.reshape(-1, N_KV_HEADS, HEAD_DIM, WINDOW), (0, 3, 1, 2))

    return (y_p[None], y_s, pool_p[None, None], jnp.transpose(pool_s, (1, 0, 2))[None],
            swa_out_t(k_p[None]), swa_out_t(v_p[None]), swa_out_t(k_s), swa_out_t(v_s),
            mem_out(mk_t), mem_out(mv_t))
```

```python
import jax
import jax.numpy as jnp
import numpy as np
from jax import lax
from jax.experimental import pallas as pl
from jax.experimental.pallas import tpu as pltpu

D_MODEL = 1024
SEQ = 16384
DEPTH = 2
DEC_BATCH = 128
DEC_SEQ = 8
PAST_LEN = 16384
HEAD_DIM = 64
POOL_W = 768
POOL_WINDOWS = (2, 4, 8, 16)
POOL_GW = 192
POOL_PAD = 15
N_Q_HEADS = 12
N_KV_HEADS = 4
GROUP = 3
SWA_W = 768
KV_W = 256
WINDOW = 128
N_MEM = 256
MEM_HEADS = 4
MEM_W = 256
ROT_DIM = 16
ROPE_THETA = 500000.0
EPS = 1e-6

F32 = jnp.float32
BF16 = jnp.bfloat16
NEG_INF = float("-inf")
LOG2E = 1.4426950408889634
Q_SCALE = HEAD_DIM ** -0.5 * LOG2E

LANES = 128
SUBLANES = 8
MXU_DIM = 256
HEADS_PER_CHUNK = MXU_DIM // HEAD_DIM
CARRY_ROWS = 16
PROMPT_TILE = 512
SAMPLE_BLOCK = 8
VMEM_LIMIT = 56 * 1024 * 1024


_VEC_WIDTHS = (("norm_a", D_MODEL), ("pool_scale", POOL_W), ("kv_norm", D_MODEL), ("k_gain", KV_W),
               ("norm_b", D_MODEL), ("q_gain", SWA_W), ("mem_q_gain0", MEM_W), ("mem_q_gain1", MEM_W))
_VEC_OFFSET = {}
for _name, _width in _VEC_WIDTHS:
    _VEC_OFFSET[_name] = (sum(w for _, w in _VEC_WIDTHS[:len(_VEC_OFFSET)]), _width)


def _vec(vec_ref, name):
    start, width = _VEC_OFFSET[name]
    return vec_ref[:, start:start + width]


def _dot(a, b):
    return jnp.dot(a, b, preferred_element_type=F32)


def _dot_nt(a, b):
    return lax.dot_general(a, b, (((1,), (1,)), ((), ())), preferred_element_type=F32)


def _rms_unit(x):
    return x * lax.rsqrt(jnp.mean(x * x, axis=-1, keepdims=True) + EPS)


def _silu(g):
    return g / (1.0 + jnp.exp(-g))


def _head_rms(y, gain, bseg):
    parts = []
    for c in range(y.shape[1] // MXU_DIM):
        yc = y[:, c * MXU_DIM:(c + 1) * MXU_DIM]
        ms = _dot((yc * yc).astype(BF16), bseg)
        parts.append(yc * lax.rsqrt(ms + EPS) * gain[:, c * MXU_DIM:(c + 1) * MXU_DIM])
    return parts[0] if len(parts) == 1 else jnp.concatenate(parts, axis=1)


def _rope(y, cos_t, sin_t, first8):
    parts = []
    for c in range(y.shape[1] // LANES):
        yc = y[:, c * LANES:(c + 1) * LANES]
        partner = jnp.where(first8, pltpu.roll(yc, LANES - 8, 1), pltpu.roll(yc, 8, 1))
        parts.append(yc * cos_t + partner * sin_t)
    return parts[0] if len(parts) == 1 else jnp.concatenate(parts, axis=1)


def _pool_window_sums(u_hist):
    def back(a, k):
        return pltpu.roll(a, k, 0)

    lane = lax.broadcasted_iota(jnp.int32, (1, LANES), 1)
    s2 = u_hist + back(u_hist, 1)
    t = s2[:, LANES:]
    s4 = t + back(t, 2)
    t = s4[:, 2 * LANES:]
    s8 = t + back(t, 4)
    t = s8[:, LANES:]
    s16 = t + back(t, 8)
    tiles = [
        s2[:, :LANES],
        jnp.where(lane < 64, s2[:, LANES:2 * LANES], s4[:, :LANES]),
        s4[:, LANES:2 * LANES],
        s8[:, :LANES],
        jnp.where(lane < 64, s8[:, LANES:2 * LANES], s16[:, :LANES]),
        s16[:, LANES:],
    ]
    return jnp.concatenate(tiles, axis=1)


def _pool_window_sums_planes(planes):
    n = len(planes)

    def doubled(prev, k, first):
        return [prev[j] + prev[j - k] if j >= first else None for j in range(n)]

    lane = lax.broadcasted_iota(jnp.int32, (1, LANES), 1)
    s2 = doubled(planes, 1, 1)
    s4 = doubled([None if a is None else a[:, LANES:] for a in s2], 2, 3)
    s8 = doubled([None if a is None else a[:, 2 * LANES:] for a in s4], 4, 7)
    s16 = doubled([None if a is None else a[:, LANES:] for a in s8], 8, POOL_PAD)
    out = []
    for j in range(POOL_PAD, n):
        out.append(jnp.concatenate([
            s2[j][:, :LANES],
            jnp.where(lane < 64, s2[j][:, LANES:2 * LANES], s4[j][:, :LANES]),
            s4[j][:, LANES:2 * LANES],
            s8[j][:, :LANES],
            jnp.where(lane < 64, s8[j][:, LANES:2 * LANES], s16[j][:, :LANES]),
            s16[j][:, LANES:],
        ], axis=1))
    return out


def _pool_mix(d, wbd_ref):
    lo, hi = MXU_DIM, 2 * MXU_DIM
    return jnp.concatenate([
        _dot(d[:, :hi], wbd_ref[:hi, :lo]),
        _dot(d, wbd_ref[:, lo:hi]),
        _dot(d[:, lo:], wbd_ref[lo:, hi:]),
    ], axis=1)


def _head_masks():
    lane = lax.broadcasted_iota(jnp.int32, (1, MXU_DIM), 1)
    return [(lane // HEAD_DIM) == j for j in range(HEADS_PER_CHUNK)]


def _first8_mask():
    lane = lax.broadcasted_iota(jnp.int32, (1, LANES), 1)
    return (lane % HEAD_DIM) < (ROT_DIM // 2)


def _stack_heads(q, hmask):
    return jnp.concatenate([jnp.where(m, q, 0.0) for m in hmask], axis=0).astype(BF16)


def _unstack_heads(o, hmask):
    m = o.shape[0] // len(hmask)
    acc = None
    for h, mask in enumerate(hmask):
        oh = jnp.where(mask, o[h * m:(h + 1) * m, :], 0.0)
        acc = oh if acc is None else acc + oh
    return acc


def _sink_columns(sinks_ref, rows_per_head):
    hrow = lax.broadcasted_iota(jnp.int32, (HEADS_PER_CHUNK * rows_per_head, 1), 0) // rows_per_head
    cols = []
    for c in range(SWA_W // MXU_DIM):
        col = jnp.zeros((HEADS_PER_CHUNK * rows_per_head, 1), F32)
        for jj in range(HEADS_PER_CHUNK):
            col = jnp.where(hrow == jj, sinks_ref[c * HEADS_PER_CHUNK + jj], col)
        cols.append(col)
    return cols


def _softmax_weights(s, sink):
    m = jnp.max(s, axis=-1, keepdims=True)
    if sink is not None:
        m = jnp.maximum(m, sink)
    p = jnp.exp2(s - m)
    den = jnp.sum(p, axis=-1, keepdims=True)
    if sink is not None:
        den = den + jnp.exp2(sink - m)
    return p.astype(BF16), 1.0 / den


def _mem_attn(q, k_t, v_t, hmask):
    p, rden = _softmax_weights(_dot(_stack_heads(q, hmask), k_t), None)
    return _unstack_heads(_dot_nt(p, v_t) * rden, hmask)


def _memkv_kernel(mem_ref, norm_ref, wk_ref, wv_ref, kgain_ref, mk_ref, mv_ref, mkb_ref, mvb_ref):
    xn = (_rms_unit(mem_ref[...]) * norm_ref[0]).astype(BF16)
    k_t = _dot_nt(wk_ref[0], xn)
    v_t = _dot_nt(wv_ref[0], xn)
    k3 = k_t.reshape(MEM_HEADS, HEAD_DIM, N_MEM)
    ms = jnp.mean(k3 * k3, axis=1, keepdims=True)
    k_t = (k3 * lax.rsqrt(ms + EPS)).reshape(MEM_W, N_MEM) * kgain_ref[0]
    mk_ref[0] = k_t
    mv_ref[0] = v_t
    mkb_ref[0] = k_t.astype(BF16)
    mvb_ref[0] = v_t.astype(BF16)


def _memkv_call(mem, mem_norm, wk_t, wv_t, kgain_col):
    const2 = lambda l: (0, 0)
    per_layer3 = lambda l: (l, 0, 0)
    out_f = jax.ShapeDtypeStruct((DEPTH, MEM_W, N_MEM), F32)
    out_b = jax.ShapeDtypeStruct((DEPTH, MEM_W, N_MEM), BF16)
    return pl.pallas_call(
        _memkv_kernel,
        grid=(DEPTH,),
        in_specs=[
            pl.BlockSpec((N_MEM, D_MODEL), const2),
            pl.BlockSpec((1, 1, D_MODEL), per_layer3),
            pl.BlockSpec((1, MEM_W, D_MODEL), per_layer3),
            pl.BlockSpec((1, MEM_W, D_MODEL), per_layer3),
            pl.BlockSpec((1, MEM_W, 1), per_layer3),
        ],
        out_specs=[pl.BlockSpec((1, MEM_W, N_MEM), per_layer3)] * 4,
        out_shape=[out_f, out_f, out_b, out_b],
        compiler_params=pltpu.CompilerParams(dimension_semantics=("arbitrary",)),
        name="memkv",
    )(mem, mem_norm, wk_t, wv_t, kgain_col)


def _swa_block(qb, kc, vc, bias, sink_cols, hmask):
    outs = []
    for g in range(GROUP):
        s = _dot_nt(_stack_heads(qb[:, g * MXU_DIM:(g + 1) * MXU_DIM], hmask), kc)
        s = (s.reshape(HEADS_PER_CHUNK, WINDOW, 2 * WINDOW) + bias[None]).reshape(HEADS_PER_CHUNK * WINDOW, 2 * WINDOW)
        p, rden = _softmax_weights(s, sink_cols[g])
        outs.append(_unstack_heads(_dot(p, vc) * rden, hmask))
    return jnp.concatenate(outs, axis=1)


def _prompt_kernel(sinks_ref, x_ref, rope_base_ref, rope_res_ref,
                   vec_ref, w_in_a_ref, wbd_ref, wcol_ref, w_out_a_ref, w_kv_ref,
                   wq_ref, wg_ref, wm_ref, wos_ref, wom_ref,
                   mk_ref, mv_ref, bseg_ref,
                   y_ref, pool_ref, ko_ref, vo_ref,
                   ucarry, kprev, vprev):
    tq = PROMPT_TILE
    i = pl.program_id(0)

    @pl.when(i == 0)
    def _():
        ucarry[0] = jnp.zeros((CARRY_ROWS, POOL_W), F32)
        kprev[0] = jnp.zeros((WINDOW, KV_W), BF16)
        vprev[0] = jnp.zeros((WINDOW, KV_W), BF16)

    rd = i % 2
    wr = (i + 1) % 2

    hmask = _head_masks()
    first8 = _first8_mask()
    bseg = bseg_ref[...]
    x = x_ref[...]

    base = rope_base_ref[i]
    cb, sb = base[0:1, :], base[1:2, :]
    cr, sr = rope_res_ref[0], rope_res_ref[1]
    cos_t = cb * cr - sb * sr
    sin_t = sb * cr + cb * sr

    xn = (_rms_unit(x) * _vec(vec_ref, "norm_a")).astype(BF16)
    u = _dot(xn, w_in_a_ref[:, 0:POOL_W])
    gp = _dot(xn, w_in_a_ref[:, POOL_W:2 * POOL_W])
    qm = _dot(xn, w_in_a_ref[:, 2 * POOL_W:2 * POOL_W + MEM_W])
    gm = _dot(xn, w_in_a_ref[:, 2 * POOL_W + MEM_W:])
    u_hist = jnp.concatenate([ucarry[rd], u], axis=0)
    win = _pool_window_sums(u_hist)[CARRY_ROWS:]
    pos = i * tq + lax.broadcasted_iota(jnp.int32, (tq, 1), 0)
    cnt = jnp.minimum((pos + 1).astype(F32), wcol_ref[...])
    d = win / cnt - u
    yp = _pool_mix(d.astype(BF16), wbd_ref) * _vec(vec_ref, "pool_scale") * _silu(gp)
    qmn = _head_rms(qm, _vec(vec_ref, "mem_q_gain0"), bseg) * Q_SCALE
    ym = _mem_attn(qmn, mk_ref[0], mv_ref[0], hmask) * _silu(gm)
    x1 = x + _dot(jnp.concatenate([yp, ym], axis=1).astype(BF16), w_out_a_ref[...])
    ucarry[wr] = u[tq - CARRY_ROWS:, :]

    r = _rms_unit(x1)
    kv = _dot((r * _vec(vec_ref, "kv_norm")).astype(BF16), w_kv_ref[...])
    k = _rope(_head_rms(kv[:, :KV_W], _vec(vec_ref, "k_gain"), bseg), cos_t, sin_t, first8)
    v = kv[:, KV_W:]
    k_all = jnp.concatenate([kprev[rd], k.astype(BF16)], axis=0)
    v_all = jnp.concatenate([vprev[rd], v.astype(BF16)], axis=0)

    xb = (r * _vec(vec_ref, "norm_b")).astype(BF16)
    zq = _dot(xb, wq_ref[...])
    gq = _dot(xb, wg_ref[...])
    qm2 = _dot(xb, wm_ref[:, 0:MEM_W])
    gm2 = _dot(xb, wm_ref[:, MEM_W:])
    q = _rope(_head_rms(zq, _vec(vec_ref, "q_gain"), bseg), cos_t, sin_t, first8) * Q_SCALE

    qi = lax.broadcasted_iota(jnp.int32, (WINDOW, 2 * WINDOW), 0)
    ci = lax.broadcasted_iota(jnp.int32, (WINDOW, 2 * WINDOW), 1)
    band_bias = jnp.where(ci > qi, jnp.where(ci <= qi + WINDOW, 0.0, NEG_INF), NEG_INF)
    sink_cols = _sink_columns(sinks_ref, WINDOW)
    ys_blocks = []
    for b in range(tq // WINDOW):
        bias = band_bias
        if b == 0:
            key_pos = ci + (i * tq - WINDOW)
            bias = jnp.where(key_pos >= 0, band_bias, NEG_INF)
        qb = q[b * WINDOW:(b + 1) * WINDOW, :]
        kc = k_all[b * WINDOW:(b + 2) * WINDOW, :]
        vc = v_all[b * WINDOW:(b + 2) * WINDOW, :]
        ys_blocks.append(_swa_block(qb, kc, vc, bias, sink_cols, hmask))
    ys = jnp.concatenate(ys_blocks, axis=0) * _silu(gq)
    qmn2 = _head_rms(qm2, _vec(vec_ref, "mem_q_gain1"), bseg) * Q_SCALE
    ym2 = _mem_attn(qmn2, mk_ref[1], mv_ref[1], hmask) * _silu(gm2)
    y_ref[...] = x1 + _dot(ys.astype(BF16), wos_ref[...]) + _dot(ym2.astype(BF16), wom_ref[...])

    kprev[wr] = k_all[tq:, :]
    vprev[wr] = v_all[tq:, :]

    @pl.when(i == pl.num_programs(0) - 1)
    def _():
        ko_ref[...] = k[tq - WINDOW:, :].T
        vo_ref[...] = v[tq - WINDOW:, :].T
        pool_ref[...] = pltpu.roll(u[tq - CARRY_ROWS:, :], CARRY_ROWS - 1, 0)[0:POOL_PAD, :]


def _const_spec(shape):
    nd = len(shape)
    return pl.BlockSpec(shape, lambda i: (0,) * nd, pipeline_mode=pl.Buffered(1))


def _prompt_call(sinks, x, rope_base, rope_res, wts, mk, mv, bseg):
    tq = PROMPT_TILE
    n = x.shape[0]
    row_spec = lambda w: pl.BlockSpec((tq, w), lambda i: (i, 0))
    in_specs = [pl.BlockSpec(memory_space=pltpu.SMEM), row_spec(D_MODEL),
                _const_spec(rope_base.shape), _const_spec(rope_res.shape)]
    in_specs += [_const_spec(w.shape) for w in wts]
    in_specs += [_const_spec(a.shape) for a in (mk, mv, bseg)]
    out_shape = [jax.ShapeDtypeStruct((n, D_MODEL), F32),
                 jax.ShapeDtypeStruct((POOL_PAD, POOL_W), F32),
                 jax.ShapeDtypeStruct((KV_W, WINDOW), F32),
                 jax.ShapeDtypeStruct((KV_W, WINDOW), F32)]
    out_specs = [row_spec(D_MODEL),
                 pl.BlockSpec((POOL_PAD, POOL_W), lambda i: (0, 0)),
                 pl.BlockSpec((KV_W, WINDOW), lambda i: (0, 0)),
                 pl.BlockSpec((KV_W, WINDOW), lambda i: (0, 0))]
    return pl.pallas_call(
        _prompt_kernel,
        grid=(n // tq,),
        in_specs=in_specs,
        out_specs=out_specs,
        out_shape=out_shape,
        scratch_shapes=[pltpu.VMEM((2, CARRY_ROWS, POOL_W), F32),
                        pltpu.VMEM((2, WINDOW, KV_W), BF16),
                        pltpu.VMEM((2, WINDOW, KV_W), BF16)],
        compiler_params=pltpu.CompilerParams(dimension_semantics=("arbitrary",),
                                             vmem_limit_bytes=VMEM_LIMIT),
        name="prompt",
    )(sinks, x, rope_base, rope_res, *wts, mk, mv, bseg)


def _stack_heads_seq(q, n_seq, hmask):
    q3 = q.reshape(n_seq, DEC_SEQ, MXU_DIM)
    return jnp.stack([jnp.where(m, q3, 0.0) for m in hmask], axis=1)


def _unstack_heads_seq(o, n_seq, hmask):
    o4 = o.reshape(n_seq, HEADS_PER_CHUNK, DEC_SEQ, MXU_DIM)
    acc = None
    for h, mask in enumerate(hmask):
        oh = jnp.where(mask, o4[:, h], 0.0)
        acc = oh if acc is None else acc + oh
    return acc.reshape(n_seq * DEC_SEQ, MXU_DIM)


def _mem_attn_seqs(q, k_ref, v_ref, layer, n_seq, hmask):
    rows = HEADS_PER_CHUNK * DEC_SEQ
    qs = _stack_heads_seq(q, n_seq, hmask).reshape(n_seq * rows, MXU_DIM).astype(BF16)
    s = jnp.concatenate([_dot(qs[b * rows:(b + 1) * rows], k_ref[layer, b].astype(BF16))
                         for b in range(n_seq)], axis=0)
    p, rden = _softmax_weights(s, None)
    o = jnp.concatenate([_dot_nt(p[b * rows:(b + 1) * rows], v_ref[layer, b].astype(BF16))
                         for b in range(n_seq)], axis=0)
    return _unstack_heads_seq(o * rden, n_seq, hmask)


def _sample_kernel(sinks_ref, x_ref, pref_ref, ck_ref, cv_ref, cmk_ref, cmv_ref, cos_ref, sin_ref,
                   vec_ref, w_in_a_ref, wbd_ref, wcol_ref, w_out_a_ref, w_kv_ref,
                   wq_ref, wg_ref, wm_ref, wos_ref, wom_ref,
                   bseg_ref,
                   y_ref, pool_ref, ko_ref, vo_ref,
                   u_scr, d_scr):
    sb = SAMPLE_BLOCK
    m = sb * DEC_SEQ
    hmask = _head_masks()
    first8 = _first8_mask()
    bseg = bseg_ref[...]
    cos_t = cos_ref[...]
    sin_t = sin_ref[...]
    x = x_ref[...].reshape(m, D_MODEL)

    xn = (_rms_unit(x) * _vec(vec_ref, "norm_a")).astype(BF16)
    u = _dot(xn, w_in_a_ref[:, 0:POOL_W])
    gp = _dot(xn, w_in_a_ref[:, POOL_W:2 * POOL_W])
    qm = _dot(xn, w_in_a_ref[:, 2 * POOL_W:2 * POOL_W + MEM_W])
    gm = _dot(xn, w_in_a_ref[:, 2 * POOL_W + MEM_W:])

    n_lt = POOL_W // LANES
    for c in range(n_lt):
        u_scr[c] = u[:, c * LANES:(c + 1) * LANES]
    planes = [pref_ref[t] for t in range(POOL_PAD)]
    planes += [jnp.concatenate([u_scr[c, pl.ds(t, sb, stride=DEC_SEQ), :] for c in range(n_lt)], axis=1)
               for t in range(DEC_SEQ)]
    for t, win in enumerate(_pool_window_sums_planes(planes)):
        cnt = jnp.minimum(float(PAST_LEN + t + 1), wcol_ref[...])
        d_t = win / cnt - planes[POOL_PAD + t]
        for c in range(n_lt):
            d_scr[c, pl.ds(t, sb, stride=DEC_SEQ), :] = d_t[:, c * LANES:(c + 1) * LANES]
    d = jnp.concatenate([d_scr[c] for c in range(n_lt)], axis=1)
    yp = _pool_mix(d.astype(BF16), wbd_ref) * _vec(vec_ref, "pool_scale") * _silu(gp)
    for t in range(POOL_PAD):
        pool_ref[t] = planes[DEC_SEQ + t]

    qmn = _head_rms(qm, _vec(vec_ref, "mem_q_gain0"), bseg) * Q_SCALE
    ym = _mem_attn_seqs(qmn, cmk_ref, cmv_ref, 0, sb, hmask) * _silu(gm)
    x1 = x + _dot(jnp.concatenate([yp, ym], axis=1).astype(BF16), w_out_a_ref[...])

    r = _rms_unit(x1)
    kv = _dot((r * _vec(vec_ref, "kv_norm")).astype(BF16), w_kv_ref[...])
    k = _rope(_head_rms(kv[:, :KV_W], _vec(vec_ref, "k_gain"), bseg), cos_t, sin_t, first8)
    v = kv[:, KV_W:]
    keep = WINDOW - DEC_SEQ
    key_lane = lax.broadcasted_iota(jnp.int32, (1, WINDOW), 1)
    pad_rows = jnp.zeros((WINDOW - m, KV_W), F32)
    for new_rows, c_ref, o_ref in ((k, ck_ref, ko_ref), (v, cv_ref, vo_ref)):
        new_t = jnp.concatenate([new_rows, pad_rows], axis=0).T
        for b in range(sb):
            shifted = pltpu.roll(c_ref[b], keep, 1)
            placed = pltpu.roll(new_t, (keep - b * DEC_SEQ) % WINDOW, 1)
            o_ref[b] = jnp.where(key_lane < keep, shifted, placed)

    xb = (r * _vec(vec_ref, "norm_b")).astype(BF16)
    zq = _dot(xb, wq_ref[...])
    gq = _dot(xb, wg_ref[...])
    qm2 = _dot(xb, wm_ref[:, 0:MEM_W])
    gm2 = _dot(xb, wm_ref[:, MEM_W:])
    q = _rope(_head_rms(zq, _vec(vec_ref, "q_gain"), bseg), cos_t, sin_t, first8) * Q_SCALE

    rows = N_Q_HEADS * DEC_SEQ
    qs = jnp.stack([_stack_heads_seq(q[:, g * MXU_DIM:(g + 1) * MXU_DIM], sb, hmask) for g in range(GROUP)], axis=1)
    qs = qs.reshape(sb * rows, KV_W).astype(BF16)
    s_old = jnp.concatenate([_dot(qs[b * rows:(b + 1) * rows], ck_ref[b].astype(BF16))
                             for b in range(sb)], axis=0)
    s_new = _dot_nt(qs, k.astype(BF16))
    tq_old = lax.broadcasted_iota(jnp.int32, (rows, WINDOW), 0) % DEC_SEQ
    key_old = lax.broadcasted_iota(jnp.int32, (rows, WINDOW), 1)
    bias_old = jnp.where(key_old > tq_old, 0.0, NEG_INF)
    row_i = lax.broadcasted_iota(jnp.int32, (sb * rows, m), 0)
    col_i = lax.broadcasted_iota(jnp.int32, (sb * rows, m), 1)
    same_seq = (row_i // rows) == (col_i // DEC_SEQ)
    bias_new = jnp.where(same_seq, jnp.where(col_i % DEC_SEQ <= row_i % DEC_SEQ, 0.0, NEG_INF), NEG_INF)
    s_old = (s_old.reshape(sb, rows, WINDOW) + bias_old[None]).reshape(sb * rows, WINDOW)
    s_new = s_new + bias_new
    sink = jnp.concatenate(_sink_columns(sinks_ref, DEC_SEQ), axis=0)
    sink = jnp.concatenate([sink] * sb, axis=0)
    mx = jnp.maximum(jnp.maximum(jnp.max(s_old, axis=-1, keepdims=True), jnp.max(s_new, axis=-1, keepdims=True)), sink)
    p_old = jnp.exp2(s_old - mx)
    p_new = jnp.exp2(s_new - mx)
    den = (jnp.sum(p_old, axis=-1, keepdims=True) + jnp.sum(p_new, axis=-1, keepdims=True) + jnp.exp2(sink - mx))
    p_old = p_old.astype(BF16)
    o = jnp.concatenate([_dot_nt(p_old[b * rows:(b + 1) * rows], cv_ref[b].astype(BF16)) for b in range(sb)], axis=0)
    o = (o + _dot(p_new.astype(BF16), v.astype(BF16))) * (1.0 / den)
    o5 = o.reshape(sb, GROUP, HEADS_PER_CHUNK * DEC_SEQ, KV_W)
    ys = jnp.concatenate([_unstack_heads_seq(o5[:, g].reshape(sb * HEADS_PER_CHUNK * DEC_SEQ, KV_W), sb, hmask)
                          for g in range(GROUP)], axis=1) * _silu(gq)

    qmn2 = _head_rms(qm2, _vec(vec_ref, "mem_q_gain1"), bseg) * Q_SCALE
    ym2 = _mem_attn_seqs(qmn2, cmk_ref, cmv_ref, 1, sb, hmask) * _silu(gm2)
    y = x1 + _dot(ys.astype(BF16), wos_ref[...]) + _dot(ym2.astype(BF16), wom_ref[...])
    y_ref[...] = y.reshape(sb, DEC_SEQ, D_MODEL)


def _sample_call(sinks, x, pref, ck, cv, cmk_t, cmv_t, cos_t, sin_t, wts, bseg):
    sb = SAMPLE_BLOCK
    nb = x.shape[0]
    seq3 = lambda a, b_: pl.BlockSpec((sb, a, b_), lambda i: (i, 0, 0))
    pool_spec = pl.BlockSpec((POOL_PAD, sb, POOL_W), lambda i: (0, i, 0))
    in_specs = [pl.BlockSpec(memory_space=pltpu.SMEM),
                seq3(DEC_SEQ, D_MODEL), pool_spec, seq3(KV_W, WINDOW), seq3(KV_W, WINDOW),
                pl.BlockSpec((DEPTH, sb, MEM_W, N_MEM), lambda i: (0, i, 0, 0)),
                pl.BlockSpec((DEPTH, sb, MEM_W, N_MEM), lambda i: (0, i, 0, 0)),
                _const_spec(cos_t.shape), _const_spec(sin_t.shape)]
    in_specs += [_const_spec(w.shape) for w in wts]
    in_specs += [_const_spec(bseg.shape)]
    out_shape = [jax.ShapeDtypeStruct((nb, DEC_SEQ, D_MODEL), F32),
                 jax.ShapeDtypeStruct((POOL_PAD, nb, POOL_W), F32),
                 jax.ShapeDtypeStruct((nb, KV_W, WINDOW), F32),
                 jax.ShapeDtypeStruct((nb, KV_W, WINDOW), F32)]
    out_specs = [seq3(DEC_SEQ, D_MODEL), pool_spec, seq3(KV_W, WINDOW), seq3(KV_W, WINDOW)]
    return pl.pallas_call(
        _sample_kernel,
        grid=(nb // sb,),
        in_specs=in_specs,
        out_specs=out_specs,
        out_shape=out_shape,
        scratch_shapes=[pltpu.VMEM((POOL_W // LANES, sb * DEC_SEQ, LANES), F32),
                        pltpu.VMEM((POOL_W // LANES, sb * DEC_SEQ, LANES), F32)],
        compiler_params=pltpu.CompilerParams(dimension_semantics=("arbitrary",),
                                             vmem_limit_bytes=VMEM_LIMIT),
        name="sample",
    )(sinks, x, pref, ck, cv, cmk_t, cmv_t, cos_t, sin_t, *wts, bseg)


def _rope_lane_tables(pos):
    half = ROT_DIM // 2
    inv = (ROPE_THETA ** (-np.arange(half, dtype=np.float32) * 2.0 / ROT_DIM)).astype(np.float32)
    ang = (np.asarray(pos, np.float32)[:, None] * inv[None, :]).astype(np.float64)
    cos, sin = np.cos(ang), np.sin(ang)
    t = ang.shape[0]
    rest = HEAD_DIM - ROT_DIM
    cos64 = np.concatenate([cos, cos, np.ones((t, rest))], axis=1)
    sin64 = np.concatenate([-sin, sin, np.zeros((t, rest))], axis=1)
    reps = LANES // HEAD_DIM
    return np.tile(cos64, (1, reps)).astype(np.float32), np.tile(sin64, (1, reps)).astype(np.float32)


def _head_tile(g, width):
    return jnp.tile(g.astype(F32), width // HEAD_DIM)


def kernel(x_prompt, x_sample, state_pool, cache_swa_k, cache_swa_v, cache_mem_k, cache_mem_v, mem_prompt,
           norm_a, w_in_a, pool_mix_w, pool_scale, w_out_a, kv_norm, w_kv, k_norm,
           norm_b, w_in_b, q_norm, sinks, w_out_b, mem_norm, w_mem_kv, mem_q_norm, mem_k_norm):
    seg = np.arange(MXU_DIM) // HEAD_DIM
    bseg = jnp.asarray((seg[:, None] == seg[None, :]).astype(np.float32) / HEAD_DIM, BF16)
    wcol = jnp.asarray(np.repeat(np.asarray(POOL_WINDOWS, np.float32), POOL_GW).reshape(1, POOL_W))

    def cols_group_major(w):
        heads = [w[:, (kvh * GROUP + g) * HEAD_DIM:(kvh * GROUP + g + 1) * HEAD_DIM]
                 for g in range(GROUP) for kvh in range(N_KV_HEADS)]
        return jnp.concatenate(heads, axis=1)

    wq = cols_group_major(w_in_b[0][:, :SWA_W]).astype(BF16)
    wg = cols_group_major(w_in_b[0][:, SWA_W:2 * SWA_W]).astype(BF16)
    wm = w_in_b[0][:, 2 * SWA_W:].astype(BF16)
    wos = w_out_b[0][:SWA_W].reshape(N_KV_HEADS, GROUP, HEAD_DIM, D_MODEL).transpose(1, 0, 2, 3).reshape(
        SWA_W, D_MODEL).astype(BF16)
    wom = w_out_b[0][SWA_W:].astype(BF16)
    sinks_p = sinks[0].astype(F32).reshape(N_KV_HEADS, GROUP).T.reshape(N_Q_HEADS) * LOG2E

    n_grp = len(POOL_WINDOWS)
    wbd = (pool_mix_w[0][:, :, None, :] * jnp.eye(n_grp, dtype=F32)[:, None, :, None]).reshape(
        POOL_W, POOL_W).astype(BF16)
    pieces = {"norm_a": norm_a[0], "pool_scale": pool_scale[0], "kv_norm": kv_norm,
              "k_gain": _head_tile(k_norm, KV_W), "norm_b": norm_b[0], "q_gain": _head_tile(q_norm[0], SWA_W),
              "mem_q_gain0": _head_tile(mem_q_norm[0], MEM_W), "mem_q_gain1": _head_tile(mem_q_norm[1], MEM_W)}
    vecs = jnp.concatenate([pieces[name].astype(F32) for name, _ in _VEC_WIDTHS]).reshape(1, -1)
    wts = lax.optimization_barrier((vecs, w_in_a[0].astype(BF16), wbd, wcol, w_out_a[0].astype(BF16),
                                    w_kv.astype(BF16), wq, wg, wm, wos, wom))
    mkg_col = jnp.stack([jnp.tile(mem_k_norm[l].astype(F32), MEM_HEADS).reshape(MEM_W, 1) for l in range(DEPTH)])

    wk_t = jnp.swapaxes(w_mem_kv[:, :, :MEM_W], 1, 2).astype(BF16)
    wv_t = jnp.swapaxes(w_mem_kv[:, :, MEM_W:], 1, 2).astype(BF16)
    mk_t, mv_t, mk_b, mv_b = _memkv_call(mem_prompt[0], mem_norm.reshape(DEPTH, 1, D_MODEL), wk_t, wv_t, mkg_col)

    n_tiles = SEQ // PROMPT_TILE
    cb, sb_ = _rope_lane_tables(np.arange(n_tiles) * PROMPT_TILE)
    cr, sr = _rope_lane_tables(np.arange(PROMPT_TILE))
    rope_base = jnp.asarray(np.stack([cb, sb_], axis=1))
    rope_res = jnp.asarray(np.stack([cr, sr], axis=0))
    y_p, pool_p, k_p, v_p = _prompt_call(sinks_p, x_prompt[0], rope_base, rope_res, wts, mk_b, mv_b, bseg)

    cos_s, sin_s = _rope_lane_tables(PAST_LEN + np.arange(DEC_SEQ))
    cos_s = jnp.asarray(np.tile(cos_s, (SAMPLE_BLOCK, 1)))
    sin_s = jnp.asarray(np.tile(sin_s, (SAMPLE_BLOCK, 1)))
    pref = jnp.transpose(state_pool[0], (1, 0, 2))
    cmk_t = jnp.transpose(cache_mem_k, (0, 1, 3, 4, 2)).reshape(DEPTH, DEC_BATCH, MEM_W, N_MEM)
    cmv_t = jnp.transpose(cache_mem_v, (0, 1, 3, 4, 2)).reshape(DEPTH, DEC_BATCH, MEM_W, N_MEM)
    ck_t = jnp.transpose(cache_swa_k, (0, 2, 3, 1)).reshape(DEC_BATCH, KV_W, WINDOW)
    cv_t = jnp.transpose(cache_swa_v, (0, 2, 3, 1)).reshape(DEC_BATCH, KV_W, WINDOW)
    y_s, pool_s, k_s, v_s = _sample_call(sinks_p, x_sample, pref, ck_t, cv_t, cmk_t, cmv_t, cos_s, sin_s,
                                         wts, bseg)

    def mem_out(a):
        return jnp.transpose(a.reshape(DEPTH, 1, MEM_HEADS, HEAD_DIM, N_MEM), (0, 1, 4, 2, 3))

    def swa_out_t(a):
        return jnp.transpose(a.reshape(-1, N_KV_HEADS, HEAD_DIM, WINDOW), (0, 3, 1, 2))

    return (y_p[None], y_s, pool_p[None, None], jnp.transpose(pool_s, (1, 0, 2))[None],
            swa_out_t(k_p[None]), swa_out_t(v_p[None]), swa_out_t(k_s), swa_out_t(v_s),
            mem_out(mk_t), mem_out(mv_t))
```

```python
import jax
import jax.numpy as jnp
import numpy as np
from jax import lax
from jax.experimental import pallas as pl
from jax.experimental.pallas import tpu as pltpu

D_MODEL = 1024
SEQ = 16384
DEPTH = 2
DEC_BATCH = 128
DEC_SEQ = 8
PAST_LEN = 16384
HEAD_DIM = 64
POOL_W = 768
POOL_WINDOWS = (2, 4, 8, 16)
POOL_GW = 192
POOL_PAD = 15
N_Q_HEADS = 12
N_KV_HEADS = 4
GROUP = 3
SWA_W = 768
KV_W = 256
WINDOW = 128
N_MEM = 256
MEM_HEADS = 4
MEM_W = 256
ROT_DIM = 16
ROPE_THETA = 500000.0
EPS = 1e-6

F32 = jnp.float32
BF16 = jnp.bfloat16
NEG_INF = float("-inf")
LOG2E = 1.4426950408889634
Q_SCALE = HEAD_DIM ** -0.5 * LOG2E

LANES = 128
SUBLANES = 8
MXU_DIM = 256
HEADS_PER_CHUNK = MXU_DIM // HEAD_DIM
CARRY_ROWS = 16
PROMPT_TILE = 512
SAMPLE_BLOCK = 8
VMEM_LIMIT = 56 * 1024 * 1024


_VEC_WIDTHS = (("norm_a", D_MODEL), ("pool_scale", POOL_W), ("kv_norm", D_MODEL), ("k_gain", KV_W),
               ("norm_b", D_MODEL), ("q_gain", SWA_W), ("mem_q_gain0", MEM_W), ("mem_q_gain1", MEM_W))
_VEC_OFFSET = {}
for _name, _width in _VEC_WIDTHS:
    _VEC_OFFSET[_name] = (sum(w for _, w in _VEC_WIDTHS[:len(_VEC_OFFSET)]), _width)


def _vec(vec_ref, name):
    start, width = _VEC_OFFSET[name]
    return vec_ref[:, start:start + width]


def _dot(a, b):
    return jnp.dot(a, b, preferred_element_type=F32)


def _dot_nt(a, b):
    return lax.dot_general(a, b, (((1,), (1,)), ((), ())), preferred_element_type=F32)


def _rms_unit(x):
    return x * lax.rsqrt(jnp.mean(x * x, axis=-1, keepdims=True) + EPS)


def _silu(g):
    return g / (1.0 + jnp.exp(-g))


def _head_rms(y, gain, bseg):
    parts = []
    for c in range(y.shape[1] // MXU_DIM):
        yc = y[:, c * MXU_DIM:(c + 1) * MXU_DIM]
        ms = _dot((yc * yc).astype(BF16), bseg)
        parts.append(yc * lax.rsqrt(ms + EPS) * gain[:, c * MXU_DIM:(c + 1) * MXU_DIM])
    return parts[0] if len(parts) == 1 else jnp.concatenate(parts, axis=1)


def _rope(y, cos_t, sin_t, first8):
    parts = []
    for c in range(y.shape[1] // LANES):
        yc = y[:, c * LANES:(c + 1) * LANES]
        partner = jnp.where(first8, pltpu.roll(yc, LANES - 8, 1), pltpu.roll(yc, 8, 1))
        parts.append(yc * cos_t + partner * sin_t)
    return parts[0] if len(parts) == 1 else jnp.concatenate(parts, axis=1)


def _pool_window_sums(u_hist):
    def back(a, k):
        return pltpu.roll(a, k, 0)

    lane = lax.broadcasted_iota(jnp.int32, (1, LANES), 1)
    s2 = u_hist + back(u_hist, 1)
    t = s2[:, LANES:]
    s4 = t + back(t, 2)
    t = s4[:, 2 * LANES:]
    s8 = t + back(t, 4)
    t = s8[:, LANES:]
    s16 = t + back(t, 8)
    tiles = [
        s2[:, :LANES],
        jnp.where(lane < 64, s2[:, LANES:2 * LANES], s4[:, :LANES]),
        s4[:, LANES:2 * LANES],
        s8[:, :LANES],
        jnp.where(lane < 64, s8[:, LANES:2 * LANES], s16[:, :LANES]),
        s16[:, LANES:],
    ]
    return jnp.concatenate(tiles, axis=1)


def _pool_window_sums_planes(planes):
    n = len(planes)

    def doubled(prev, k, first):
        return [prev[j] + prev[j - k] if j >= first else None for j in range(n)]

    lane = lax.broadcasted_iota(jnp.int32, (1, LANES), 1)
    s2 = doubled(planes, 1, 1)
    s4 = doubled([None if a is None else a[:, LANES:] for a in s2], 2, 3)
    s8 = doubled([None if a is None else a[:, 2 * LANES:] for a in s4], 4, 7)
    s16 = doubled([None if a is None else a[:, LANES:] for a in s8], 8, POOL_PAD)
    out = []
    for j in range(POOL_PAD, n):
        out.append(jnp.concatenate([
            s2[j][:, :LANES],
            jnp.where(lane < 64, s2[j][:, LANES:2 * LANES], s4[j][:, :LANES]),
            s4[j][:, LANES:2 * LANES],
            s8[j][:, :LANES],
            jnp.where(lane < 64, s8[j][:, LANES:2 * LANES], s16[j][:, :LANES]),
            s16[j][:, LANES:],
        ], axis=1))
    return out


def _pool_mix(d, wbd_ref):
    lo, hi = MXU_DIM, 2 * MXU_DIM
    return jnp.concatenate([
        _dot(d[:, :hi], wbd_ref[:hi, :lo]),
        _dot(d, wbd_ref[:, lo:hi]),
        _dot(d[:, lo:], wbd_ref[lo:, hi:]),
    ], axis=1)


def _head_masks():
    lane = lax.broadcasted_iota(jnp.int32, (1, MXU_DIM), 1)
    return [(lane // HEAD_DIM) == j for j in range(HEADS_PER_CHUNK)]


def _first8_mask():
    lane = lax.broadcasted_iota(jnp.int32, (1, LANES), 1)
    return (lane % HEAD_DIM) < (ROT_DIM // 2)


def _stack_heads(q, hmask):
    return jnp.concatenate([jnp.where(m, q, 0.0) for m in hmask], axis=0).astype(BF16)


def _unstack_heads(o, hmask):
    m = o.shape[0] // len(hmask)
    acc = None
    for h, mask in enumerate(hmask):
        oh = jnp.where(mask, o[h * m:(h + 1) * m, :], 0.0)
        acc = oh if acc is None else acc + oh
    return acc


def _sink_columns(sinks_ref, rows_per_head):
    hrow = lax.broadcasted_iota(jnp.int32, (HEADS_PER_CHUNK * rows_per_head, 1), 0) // rows_per_head
    cols = []
    for c in range(SWA_W // MXU_DIM):
        col = jnp.zeros((HEADS_PER_CHUNK * rows_per_head, 1), F32)
        for jj in range(HEADS_PER_CHUNK):
            col = jnp.where(hrow == jj, sinks_ref[c * HEADS_PER_CHUNK + jj], col)
        cols.append(col)
    return cols


def _softmax_weights(s, sink):
    m = jnp.max(s, axis=-1, keepdims=True)
    if sink is not None:
        m = jnp.maximum(m, sink)
    p = jnp.exp2(s - m)
    den = jnp.sum(p, axis=-1, keepdims=True)
    if sink is not None:
        den = den + jnp.exp2(sink - m)
    return p.astype(BF16), 1.0 / den


def _mem_attn(q, k_t, v_t, hmask):
    p, rden = _softmax_weights(_dot(_stack_heads(q, hmask), k_t), None)
    return _unstack_heads(_dot_nt(p, v_t) * rden, hmask)


def _memkv_kernel(mem_ref, norm_ref, wk_ref, wv_ref, kgain_ref, mk_ref, mv_ref, mkb_ref, mvb_ref):
    xn = (_rms_unit(mem_ref[...]) * norm_ref[0]).astype(BF16)
    k_t = _dot_nt(wk_ref[0], xn)
    v_t = _dot_nt(wv_ref[0], xn)
    k3 = k_t.reshape(MEM_HEADS, HEAD_DIM, N_MEM)
    ms = jnp.mean(k3 * k3, axis=1, keepdims=True)
    k_t = (k3 * lax.rsqrt(ms + EPS)).reshape(MEM_W, N_MEM) * kgain_ref[0]
    mk_ref[0] = k_t
    mv_ref[0] = v_t
    mkb_ref[0] = k_t.astype(BF16)
    mvb_ref[0] = v_t.astype(BF16)


def _memkv_call(mem, mem_norm, wk_t, wv_t, kgain_col):
    const2 = lambda l: (0, 0)
    per_layer3 = lambda l: (l, 0, 0)
    out_f = jax.ShapeDtypeStruct((DEPTH, MEM_W, N_MEM), F32)
    out_b = jax.ShapeDtypeStruct((DEPTH, MEM_W, N_MEM), BF16)
    return pl.pallas_call(
        _memkv_kernel,
        grid=(DEPTH,),
        in_specs=[
            pl.BlockSpec((N_MEM, D_MODEL), const2),
            pl.BlockSpec((1, 1, D_MODEL), per_layer3),
            pl.BlockSpec((1, MEM_W, D_MODEL), per_layer3),
            pl.BlockSpec((1, MEM_W, D_MODEL), per_layer3),
            pl.BlockSpec((1, MEM_W, 1), per_layer3),
        ],
        out_specs=[pl.BlockSpec((1, MEM_W, N_MEM), per_layer3)] * 4,
        out_shape=[out_f, out_f, out_b, out_b],
        compiler_params=pltpu.CompilerParams(dimension_semantics=("arbitrary",)),
        name="memkv",
    )(mem, mem_norm, wk_t, wv_t, kgain_col)


def _swa_block(qb, kc, vc, bias, sink_cols, hmask):
    outs = []
    for g in range(GROUP):
        s = _dot_nt(_stack_heads(qb[:, g * MXU_DIM:(g + 1) * MXU_DIM], hmask), kc)
        s = (s.reshape(HEADS_PER_CHUNK, WINDOW, 2 * WINDOW) + bias[None]).reshape(HEADS_PER_CHUNK * WINDOW, 2 * WINDOW)
        p, rden = _softmax_weights(s, sink_cols[g])
        outs.append(_unstack_heads(_dot(p, vc) * rden, hmask))
    return jnp.concatenate(outs, axis=1)


def _prompt_kernel(sinks_ref, x_ref, rope_base_ref, rope_res_ref,
                   vec_ref, w_in_a_ref, wbd_ref, wcol_ref, w_out_a_ref, w_kv_ref,
                   wq_ref, wg_ref, wm_ref, wos_ref, wom_ref,
                   mk_ref, mv_ref, bseg_ref,
                   y_ref, pool_ref, ko_ref, vo_ref,
                   ucarry, kprev, vprev):
    tq = PROMPT_TILE
    i = pl.program_id(0)

    @pl.when(i == 0)
    def _():
        ucarry[0] = jnp.zeros((CARRY_ROWS, POOL_W), F32)
        kprev[0] = jnp.zeros((WINDOW, KV_W), BF16)
        vprev[0] = jnp.zeros((WINDOW, KV_W), BF16)

    rd = i % 2
    wr = (i + 1) % 2

    hmask = _head_masks()
    first8 = _first8_mask()
    bseg = bseg_ref[...]
    x = x_ref[...]

    base = rope_base_ref[i]
    cb, sb = base[0:1, :], base[1:2, :]
    cr, sr = rope_res_ref[0], rope_res_ref[1]
    cos_t = cb * cr - sb * sr
    sin_t = sb * cr + cb * sr

    xn = (_rms_unit(x) * _vec(vec_ref, "norm_a")).astype(BF16)
    u = _dot(xn, w_in_a_ref[:, 0:POOL_W])
    gp = _dot(xn, w_in_a_ref[:, POOL_W:2 * POOL_W])
    qm = _dot(xn, w_in_a_ref[:, 2 * POOL_W:2 * POOL_W + MEM_W])
    gm = _dot(xn, w_in_a_ref[:, 2 * POOL_W + MEM_W:])
    u_hist = jnp.concatenate([ucarry[rd], u], axis=0)
    win = _pool_window_sums(u_hist)[CARRY_ROWS:]
    pos = i * tq + lax.broadcasted_iota(jnp.int32, (tq, 1), 0)
    cnt = jnp.minimum((pos + 1).astype(F32), wcol_ref[...])
    d = win / cnt - u
    yp = _pool_mix(d.astype(BF16), wbd_ref) * _vec(vec_ref, "pool_scale") * _silu(gp)
    qmn = _head_rms(qm, _vec(vec_ref, "mem_q_gain0"), bseg) * Q_SCALE
    ym = _mem_attn(qmn, mk_ref[0], mv_ref[0], hmask) * _silu(gm)
    x1 = x + _dot(jnp.concatenate([yp, ym], axis=1).astype(BF16), w_out_a_ref[...])
    ucarry[wr] = u[tq - CARRY_ROWS:, :]

    r = _rms_unit(x1)
    kv = _dot((r * _vec(vec_ref, "kv_norm")).astype(BF16), w_kv_ref[...])
    k = _rope(_head_rms(kv[:, :KV_W], _vec(vec_ref, "k_gain"), bseg), cos_t, sin_t, first8)
    v = kv[:, KV_W:]
    k_all = jnp.concatenate([kprev[rd], k.astype(BF16)], axis=0)
    v_all = jnp.concatenate([vprev[rd], v.astype(BF16)], axis=0)

    xb = (r * _vec(vec_ref, "norm_b")).astype(BF16)
    zq = _dot(xb, wq_ref[...])
    gq = _dot(xb, wg_ref[...])
    qm2 = _dot(xb, wm_ref[:, 0:MEM_W])
    gm2 = _dot(xb, wm_ref[:, MEM_W:])
    q = _rope(_head_rms(zq, _vec(vec_ref, "q_gain"), bseg), cos_t, sin_t, first8) * Q_SCALE

    qi = lax.broadcasted_iota(jnp.int32, (WINDOW, 2 * WINDOW), 0)
    ci = lax.broadcasted_iota(jnp.int32, (WINDOW, 2 * WINDOW), 1)
    band_bias = jnp.where(ci > qi, jnp.where(ci <= qi + WINDOW, 0.0, NEG_INF), NEG_INF)
    sink_cols = _sink_columns(sinks_ref, WINDOW)
    ys_blocks = []
    for b in range(tq // WINDOW):
        bias = band_bias
        if b == 0:
            key_pos = ci + (i * tq - WINDOW)
            bias = jnp.where(key_pos >= 0, band_bias, NEG_INF)
        qb = q[b * WINDOW:(b + 1) * WINDOW, :]
        kc = k_all[b * WINDOW:(b + 2) * WINDOW, :]
        vc = v_all[b * WINDOW:(b + 2) * WINDOW, :]
        ys_blocks.append(_swa_block(qb, kc, vc, bias, sink_cols, hmask))
    ys = jnp.concatenate(ys_blocks, axis=0) * _silu(gq)
    qmn2 = _head_rms(qm2, _vec(vec_ref, "mem_q_gain1"), bseg) * Q_SCALE
    ym2 = _mem_attn(qmn2, mk_ref[1], mv_ref[1], hmask) * _silu(gm2)
    y_ref[...] = x1 + _dot(ys.astype(BF16), wos_ref[...]) + _dot(ym2.astype(BF16), wom_ref[...])

    kprev[wr] = k_all[tq:, :]
    vprev[wr] = v_all[tq:, :]

    @pl.when(i == pl.num_programs(0) - 1)
    def _():
        ko_ref[...] = k[tq - WINDOW:, :].T
        vo_ref[...] = v[tq - WINDOW:, :].T
        pool_ref[...] = pltpu.roll(u[tq - CARRY_ROWS:, :], CARRY_ROWS - 1, 0)[0:POOL_PAD, :]


def _const_spec(shape):
    nd = len(shape)
    return pl.BlockSpec(shape, lambda i: (0,) * nd, pipeline_mode=pl.Buffered(1))


def _prompt_call(sinks, x, rope_base, rope_res, wts, mk, mv, bseg):
    tq = PROMPT_TILE
    n = x.shape[0]
    row_spec = lambda w: pl.BlockSpec((tq, w), lambda i: (i, 0))
    in_specs = [pl.BlockSpec(memory_space=pltpu.SMEM), row_spec(D_MODEL),
                _const_spec(rope_base.shape), _const_spec(rope_res.shape)]
    in_specs += [_const_spec(w.shape) for w in wts]
    in_specs += [_const_spec(a.shape) for a in (mk, mv, bseg)]
    out_shape = [jax.ShapeDtypeStruct((n, D_MODEL), F32),
                 jax.ShapeDtypeStruct((POOL_PAD, POOL_W), F32),
                 jax.ShapeDtypeStruct((KV_W, WINDOW), F32),
                 jax.ShapeDtypeStruct((KV_W, WINDOW), F32)]
    out_specs = [row_spec(D_MODEL),
                 pl.BlockSpec((POOL_PAD, POOL_W), lambda i: (0, 0)),
                 pl.BlockSpec((KV_W, WINDOW), lambda i: (0, 0)),
                 pl.BlockSpec((KV_W, WINDOW), lambda i: (0, 0))]
    return pl.pallas_call(
        _prompt_kernel,
        grid=(n // tq,),
        in_specs=in_specs,
        out_specs=out_specs,
        out_shape=out_shape,
        scratch_shapes=[pltpu.VMEM((2, CARRY_ROWS, POOL_W), F32),
                        pltpu.VMEM((2, WINDOW, KV_W), BF16),
                        pltpu.VMEM((2, WINDOW, KV_W), BF16)],
        compiler_params=pltpu.CompilerParams(dimension_semantics=("arbitrary",),
                                             vmem_limit_bytes=VMEM_LIMIT),
        name="prompt",
    )(sinks, x, rope_base, rope_res, *wts, mk, mv, bseg)


def _stack_heads_seq(q, n_seq, hmask):
    q3 = q.reshape(n_seq, DEC_SEQ, MXU_DIM)
    return jnp.stack([jnp.where(m, q3, 0.0) for m in hmask], axis=1)


def _unstack_heads_seq(o, n_seq, hmask):
    o4 = o.reshape(n_seq, HEADS_PER_CHUNK, DEC_SEQ, MXU_DIM)
    acc = None
    for h, mask in enumerate(hmask):
        oh = jnp.where(mask, o4[:, h], 0.0)
        acc = oh if acc is None else acc + oh
    return acc.reshape(n_seq * DEC_SEQ, MXU_DIM)


def _mem_attn_seqs(q, k_ref, v_ref, layer, n_seq, hmask):
    rows = HEADS_PER_CHUNK * DEC_SEQ
    qs = _stack_heads_seq(q, n_seq, hmask).reshape(n_seq * rows, MXU_DIM).astype(BF16)
    s = jnp.concatenate([_dot(qs[b * rows:(b + 1) * rows], k_ref[layer, b].astype(BF16))
                         for b in range(n_seq)], axis=0)
    p, rden = _softmax_weights(s, None)
    o = jnp.concatenate([_dot_nt(p[b * rows:(b + 1) * rows], v_ref[layer, b].astype(BF16))
                         for b in range(n_seq)], axis=0)
    return _unstack_heads_seq(o * rden, n_seq, hmask)


def _sample_kernel(sinks_ref, x_ref, pref_ref, ck_ref, cv_ref, cmk_ref, cmv_ref, cos_ref, sin_ref,
                   vec_ref, w_in_a_ref, wbd_ref, wcol_ref, w_out_a_ref, w_kv_ref,
                   wq_ref, wg_ref, wm_ref, wos_ref, wom_ref,
                   bseg_ref,
                   y_ref, pool_ref, ko_ref, vo_ref,
                   u_scr, d_scr):
    sb = SAMPLE_BLOCK
    m = sb * DEC_SEQ
    hmask = _head_masks()
    first8 = _first8_mask()
    bseg = bseg_ref[...]
    cos_t = cos_ref[...]
    sin_t = sin_ref[...]
    x = x_ref[...].reshape(m, D_MODEL)

    xn = (_rms_unit(x) * _vec(vec_ref, "norm_a")).astype(BF16)
    u = _dot(xn, w_in_a_ref[:, 0:POOL_W])
    gp = _dot(xn, w_in_a_ref[:, POOL_W:2 * POOL_W])
    qm = _dot(xn, w_in_a_ref[:, 2 * POOL_W:2 * POOL_W + MEM_W])
    gm = _dot(xn, w_in_a_ref[:, 2 * POOL_W + MEM_W:])

    n_lt = POOL_W // LANES
    for c in range(n_lt):
        u_scr[c] = u[:, c * LANES:(c + 1) * LANES]
    planes = [pref_ref[t] for t in range(POOL_PAD)]
    planes += [jnp.concatenate([u_scr[c, pl.ds(t, sb, stride=DEC_SEQ), :] for c in range(n_lt)], axis=1)
               for t in range(DEC_SEQ)]
    for t, win in enumerate(_pool_window_sums_planes(planes)):
        cnt = jnp.minimum(float(PAST_LEN + t + 1), wcol_ref[...])
        d_t = win / cnt - planes[POOL_PAD + t]
        for c in range(n_lt):
            d_scr[c, pl.ds(t, sb, stride=DEC_SEQ), :] = d_t[:, c * LANES:(c + 1) * LANES]
    d = jnp.concatenate([d_scr[c] for c in range(n_lt)], axis=1)
    yp = _pool_mix(d.astype(BF16), wbd_ref) * _vec(vec_ref, "pool_scale") * _silu(gp)
    for t in range(POOL_PAD):
        pool_ref[t] = planes[DEC_SEQ + t]

    qmn = _head_rms(qm, _vec(vec_ref, "mem_q_gain0"), bseg) * Q_SCALE
    ym = _mem_attn_seqs(qmn, cmk_ref, cmv_ref, 0, sb, hmask) * _silu(gm)
    x1 = x + _dot(jnp.concatenate([yp, ym], axis=1).astype(BF16), w_out_a_ref[...])

    r = _rms_unit(x1)
    kv = _dot((r * _vec(vec_ref, "kv_norm")).astype(BF16), w_kv_ref[...])
    k = _rope(_head_rms(kv[:, :KV_W], _vec(vec_ref, "k_gain"), bseg), cos_t, sin_t, first8)
    v = kv[:, KV_W:]
    keep = WINDOW - DEC_SEQ
    key_lane = lax.broadcasted_iota(jnp.int32, (1, WINDOW), 1)
    pad_rows = jnp.zeros((WINDOW - m, KV_W), F32)
    for new_rows, c_ref, o_ref in ((k, ck_ref, ko_ref), (v, cv_ref, vo_ref)):
        new_t = jnp.concatenate([new_rows, pad_rows], axis=0).T
        for b in range(sb):
            shifted = pltpu.roll(c_ref[b], keep, 1)
            placed = pltpu.roll(new_t, (keep - b * DEC_SEQ) % WINDOW, 1)
            o_ref[b] = jnp.where(key_lane < keep, shifted, placed)

    xb = (r * _vec(vec_ref, "norm_b")).astype(BF16)
    zq = _dot(xb, wq_ref[...])
    gq = _dot(xb, wg_ref[...])
    qm2 = _dot(xb, wm_ref[:, 0:MEM_W])
    gm2 = _dot(xb, wm_ref[:, MEM_W:])
    q = _rope(_head_rms(zq, _vec(vec_ref, "q_gain"), bseg), cos_t, sin_t, first8) * Q_SCALE

    rows = N_Q_HEADS * DEC_SEQ
    qs = jnp.stack([_stack_heads_seq(q[:, g * MXU_DIM:(g + 1) * MXU_DIM], sb, hmask) for g in range(GROUP)], axis=1)
    qs = qs.reshape(sb * rows, KV_W).astype(BF16)
    s_old = jnp.concatenate([_dot(qs[b * rows:(b + 1) * rows], ck_ref[b].astype(BF16))
                             for b in range(sb)], axis=0)
    s_new = _dot_nt(qs, k.astype(BF16))
    tq_old = lax.broadcasted_iota(jnp.int32, (rows, WINDOW), 0) % DEC_SEQ
    key_old = lax.broadcasted_iota(jnp.int32, (rows, WINDOW), 1)
    bias_old = jnp.where(key_old > tq_old, 0.0, NEG_INF)
    row_i = lax.broadcasted_iota(jnp.int32, (sb * rows, m), 0)
    col_i = lax.broadcasted_iota(jnp.int32, (sb * rows, m), 1)
    same_seq = (row_i // rows) == (col_i // DEC_SEQ)
    bias_new = jnp.where(same_seq, jnp.where(col_i % DEC_SEQ <= row_i % DEC_SEQ, 0.0, NEG_INF), NEG_INF)
    s_old = (s_old.reshape(sb, rows, WINDOW) + bias_old[None]).reshape(sb * rows, WINDOW)
    s_new = s_new + bias_new
    sink = jnp.concatenate(_sink_columns(sinks_ref, DEC_SEQ), axis=0)
    sink = jnp.concatenate([sink] * sb, axis=0)
    mx = jnp.maximum(jnp.maximum(jnp.max(s_old, axis=-1, keepdims=True), jnp.max(s_new, axis=-1, keepdims=True)), sink)
    p_old = jnp.exp2(s_old - mx)
    p_new = jnp.exp2(s_new - mx)
    den = (jnp.sum(p_old, axis=-1, keepdims=True) + jnp.sum(p_new, axis=-1, keepdims=True) + jnp.exp2(sink - mx))
    p_old = p_old.astype(BF16)
    o = jnp.concatenate([_dot_nt(p_old[b * rows:(b + 1) * rows], cv_ref[b].astype(BF16)) for b in range(sb)], axis=0)
    o = (o + _dot(p_new.astype(BF16), v.astype(BF16))) * (1.0 / den)
    o5 = o.reshape(sb, GROUP, HEADS_PER_CHUNK * DEC_SEQ, KV_W)
    ys = jnp.concatenate([_unstack_heads_seq(o5[:, g].reshape(sb * HEADS_PER_CHUNK * DEC_SEQ, KV_W), sb, hmask)
                          for g in range(GROUP)], axis=1) * _silu(gq)

    qmn2 = _head_rms(qm2, _vec(vec_ref, "mem_q_gain1"), bseg) * Q_SCALE
    ym2 = _mem_attn_seqs(qmn2, cmk_ref, cmv_ref, 1, sb, hmask) * _silu(gm2)
    y = x1 + _dot(ys.astype(BF16), wos_ref[...]) + _dot(ym2.astype(BF16), wom_ref[...])
    y_ref[...] = y.reshape(sb, DEC_SEQ, D_MODEL)


def _sample_call(sinks, x, pref, ck, cv, cmk_t, cmv_t, cos_t, sin_t, wts, bseg):
    sb = SAMPLE_BLOCK
    nb = x.shape[0]
    seq3 = lambda a, b_: pl.BlockSpec((sb, a, b_), lambda i: (i, 0, 0))
    pool_spec = pl.BlockSpec((POOL_PAD, sb, POOL_W), lambda i: (0, i, 0))
    in_specs = [pl.BlockSpec(memory_space=pltpu.SMEM),
                seq3(DEC_SEQ, D_MODEL), pool_spec, seq3(KV_W, WINDOW), seq3(KV_W, WINDOW),
                pl.BlockSpec((DEPTH, sb, MEM_W, N_MEM), lambda i: (0, i, 0, 0)),
                pl.BlockSpec((DEPTH, sb, MEM_W, N_MEM), lambda i: (0, i, 0, 0)),
                _const_spec(cos_t.shape), _const_spec(sin_t.shape)]
    in_specs += [_const_spec(w.shape) for w in wts]
    in_specs += [_const_spec(bseg.shape)]
    out_shape = [jax.ShapeDtypeStruct((nb, DEC_SEQ, D_MODEL), F32),
                 jax.ShapeDtypeStruct((POOL_PAD, nb, POOL_W), F32),
                 jax.ShapeDtypeStruct((nb, KV_W, WINDOW), F32),
                 jax.ShapeDtypeStruct((nb, KV_W, WINDOW), F32)]
    out_specs = [seq3(DEC_SEQ, D_MODEL), pool_spec, seq3(KV_W, WINDOW), seq3(KV_W, WINDOW)]
    return pl.pallas_call(
        _sample_kernel,
        grid=(nb // sb,),
        in_specs=in_specs,
        out_specs=out_specs,
        out_shape=out_shape,
        scratch_shapes=[pltpu.VMEM((POOL_W // LANES, sb * DEC_SEQ, LANES), F32),
                        pltpu.VMEM((POOL_W // LANES, sb * DEC_SEQ, LANES), F32)],
        compiler_params=pltpu.CompilerParams(dimension_semantics=("arbitrary",),
                                             vmem_limit_bytes=VMEM_LIMIT),
        name="sample",
    )(sinks, x, pref, ck, cv, cmk_t, cmv_t, cos_t, sin_t, *wts, bseg)


PREP_ROWS = 128
HEAD_PAIR_BLOCKS = SWA_W // PREP_ROWS


def _cols_group_major(w):
    heads = [w[:, (kvh * GROUP + g) * HEAD_DIM:(kvh * GROUP + g + 1) * HEAD_DIM]
             for g in range(GROUP) for kvh in range(N_KV_HEADS)]
    return jnp.concatenate(heads, axis=1)


def _prep_kernel(w_in_a_ref, w_out_a_ref, w_kv_ref, w_in_b_ref, wob_a_ref, wob_b_ref, wob_m_ref, w_mem_ref,
                 in_a_ref, out_a_ref, kv_ref, wq_ref, wg_ref, wm_ref, wos_ref, wom_ref, wkt_ref, wvt_ref):
    in_a_ref[...] = w_in_a_ref[...].astype(BF16)
    out_a_ref[...] = w_out_a_ref[...].astype(BF16)
    kv_ref[...] = w_kv_ref[...].astype(BF16)
    w_in_b = w_in_b_ref[...]
    wq_ref[...] = _cols_group_major(w_in_b[:, :SWA_W]).astype(BF16)
    wg_ref[...] = _cols_group_major(w_in_b[:, SWA_W:2 * SWA_W]).astype(BF16)
    wm_ref[...] = w_in_b[:, 2 * SWA_W:].astype(BF16)
    wos_ref[...] = jnp.concatenate([wob_a_ref[0], wob_b_ref[0]], axis=0).astype(BF16)
    wom_ref[...] = wob_m_ref[...].astype(BF16)
    for l in range(DEPTH):
        w_mem = w_mem_ref[l]
        wkt_ref[l] = w_mem[:, :MEM_W].T.astype(BF16)
        wvt_ref[l] = w_mem[:, MEM_W:].T.astype(BF16)


def _prep_call(w_in_a, w_out_a, w_kv, w_in_b, w_out_b, w_mem_kv):
    rows = lambda w: pl.BlockSpec((PREP_ROWS, w), lambda i: (i, 0))

    def head_src(second):
        def index(i):
            j = jnp.minimum(i, HEAD_PAIR_BLOCKS - 1)
            return ((2 * (j % 2) + second) * GROUP + j // 2, 0, 0)
        return pl.BlockSpec((1, HEAD_DIM, D_MODEL), index)

    per_mem_block = (D_MODEL // PREP_ROWS) // (MEM_W // PREP_ROWS)
    in_specs = [rows(2 * POOL_W + 2 * MEM_W), rows(D_MODEL), rows(2 * KV_W), rows(2 * SWA_W + 2 * MEM_W),
                head_src(0), head_src(1),
                pl.BlockSpec((PREP_ROWS, D_MODEL), lambda i: (HEAD_PAIR_BLOCKS + i // per_mem_block, 0)),
                pl.BlockSpec((DEPTH, PREP_ROWS, 2 * MEM_W), lambda i: (0, i, 0))]
    out_specs = [rows(2 * POOL_W + 2 * MEM_W), rows(D_MODEL), rows(2 * KV_W), rows(SWA_W), rows(SWA_W),
                 rows(2 * MEM_W),
                 pl.BlockSpec((PREP_ROWS, D_MODEL), lambda i: (jnp.minimum(i, HEAD_PAIR_BLOCKS - 1), 0)),
                 pl.BlockSpec((PREP_ROWS, D_MODEL), lambda i: (i // per_mem_block, 0)),
                 pl.BlockSpec((DEPTH, MEM_W, PREP_ROWS), lambda i: (0, 0, i)),
                 pl.BlockSpec((DEPTH, MEM_W, PREP_ROWS), lambda i: (0, 0, i))]
    bf = lambda *shape: jax.ShapeDtypeStruct(shape, BF16)
    out_shape = [bf(D_MODEL, 2 * POOL_W + 2 * MEM_W), bf(D_MODEL, D_MODEL), bf(D_MODEL, 2 * KV_W),
                 bf(D_MODEL, SWA_W), bf(D_MODEL, SWA_W), bf(D_MODEL, 2 * MEM_W),
                 bf(SWA_W, D_MODEL), bf(MEM_W, D_MODEL),
                 bf(DEPTH, MEM_W, D_MODEL), bf(DEPTH, MEM_W, D_MODEL)]
    w_out_b_heads = w_out_b.reshape(D_MODEL // HEAD_DIM, HEAD_DIM, D_MODEL)
    return pl.pallas_call(
        _prep_kernel,
        grid=(D_MODEL // PREP_ROWS,),
        in_specs=in_specs,
        out_specs=out_specs,
        out_shape=out_shape,
        compiler_params=pltpu.CompilerParams(dimension_semantics=("arbitrary",)),
        name="prep",
    )(w_in_a, w_out_a, w_kv, w_in_b, w_out_b_heads, w_out_b_heads, w_out_b, w_mem_kv)


def _rope_lane_tables(pos):
    half = ROT_DIM // 2
    inv = (ROPE_THETA ** (-np.arange(half, dtype=np.float32) * 2.0 / ROT_DIM)).astype(np.float32)
    ang = (np.asarray(pos, np.float32)[:, None] * inv[None, :]).astype(np.float64)
    cos, sin = np.cos(ang), np.sin(ang)
    t = ang.shape[0]
    rest = HEAD_DIM - ROT_DIM
    cos64 = np.concatenate([cos, cos, np.ones((t, rest))], axis=1)
    sin64 = np.concatenate([-sin, sin, np.zeros((t, rest))], axis=1)
    reps = LANES // HEAD_DIM
    return np.tile(cos64, (1, reps)).astype(np.float32), np.tile(sin64, (1, reps)).astype(np.float32)


def _head_tile(g, width):
    return jnp.tile(g.astype(F32), width // HEAD_DIM)


def kernel(x_prompt, x_sample, state_pool, cache_swa_k, cache_swa_v, cache_mem_k, cache_mem_v, mem_prompt,
           norm_a, w_in_a, pool_mix_w, pool_scale, w_out_a, kv_norm, w_kv, k_norm,
           norm_b, w_in_b, q_norm, sinks, w_out_b, mem_norm, w_mem_kv, mem_q_norm, mem_k_norm):
    seg = np.arange(MXU_DIM) // HEAD_DIM
    bseg = jnp.asarray((seg[:, None] == seg[None, :]).astype(np.float32) / HEAD_DIM, BF16)
    wcol = jnp.asarray(np.repeat(np.asarray(POOL_WINDOWS, np.float32), POOL_GW).reshape(1, POOL_W))

    w_in_a_b, w_out_a_b, w_kv_b, wq, wg, wm, wos, wom, wk_t, wv_t = _prep_call(
        w_in_a[0], w_out_a[0], w_kv, w_in_b[0], w_out_b[0], w_mem_kv)
    sinks_p = sinks[0].astype(F32).reshape(N_KV_HEADS, GROUP).T.reshape(N_Q_HEADS) * LOG2E

    n_grp = len(POOL_WINDOWS)
    wbd = (pool_mix_w[0][:, :, None, :] * jnp.eye(n_grp, dtype=F32)[:, None, :, None]).reshape(
        POOL_W, POOL_W).astype(BF16)
    pieces = {"norm_a": norm_a[0], "pool_scale": pool_scale[0], "kv_norm": kv_norm,
              "k_gain": _head_tile(k_norm, KV_W), "norm_b": norm_b[0], "q_gain": _head_tile(q_norm[0], SWA_W),
              "mem_q_gain0": _head_tile(mem_q_norm[0], MEM_W), "mem_q_gain1": _head_tile(mem_q_norm[1], MEM_W)}
    vecs = jnp.concatenate([pieces[name].astype(F32) for name, _ in _VEC_WIDTHS]).reshape(1, -1)
    wts = (vecs, w_in_a_b, wbd, wcol, w_out_a_b, w_kv_b, wq, wg, wm, wos, wom)
    mkg_col = jnp.stack([jnp.tile(mem_k_norm[l].astype(F32), MEM_HEADS).reshape(MEM_W, 1) for l in range(DEPTH)])

    mk_t, mv_t, mk_b, mv_b = _memkv_call(mem_prompt[0], mem_norm.reshape(DEPTH, 1, D_MODEL), wk_t, wv_t, mkg_col)

    n_tiles = SEQ // PROMPT_TILE
    cb, sb_ = _rope_lane_tables(np.arange(n_tiles) * PROMPT_TILE)
    cr, sr = _rope_lane_tables(np.arange(PROMPT_TILE))
    rope_base = jnp.asarray(np.stack([cb, sb_], axis=1))
    rope_res = jnp.asarray(np.stack([cr, sr], axis=0))
    y_p, pool_p, k_p, v_p = _prompt_call(sinks_p, x_prompt[0], rope_base, rope_res, wts, mk_b, mv_b, bseg)

    cos_s, sin_s = _rope_lane_tables(PAST_LEN + np.arange(DEC_SEQ))
    cos_s = jnp.asarray(np.tile(cos_s, (SAMPLE_BLOCK, 1)))
    sin_s = jnp.asarray(np.tile(sin_s, (SAMPLE_BLOCK, 1)))
    pref = jnp.transpose(state_pool[0], (1, 0, 2))
    cmk_t = jnp.transpose(cache_mem_k, (0, 1, 3, 4, 2)).reshape(DEPTH, DEC_BATCH, MEM_W, N_MEM)
    cmv_t = jnp.transpose(cache_mem_v, (0, 1, 3, 4, 2)).reshape(DEPTH, DEC_BATCH, MEM_W, N_MEM)
    ck_t = jnp.transpose(cache_swa_k, (0, 2, 3, 1)).reshape(DEC_BATCH, KV_W, WINDOW)
    cv_t = jnp.transpose(cache_swa_v, (0, 2, 3, 1)).reshape(DEC_BATCH, KV_W, WINDOW)
    y_s, pool_s, k_s, v_s = _sample_call(sinks_p, x_sample, pref, ck_t, cv_t, cmk_t, cmv_t, cos_s, sin_s,
                                         wts, bseg)

    def mem_out(a):
        return jnp.transpose(a.reshape(DEPTH, 1, MEM_HEADS, HEAD_DIM, N_MEM), (0, 1, 4, 2, 3))

    def swa_out_t(a):
        return jnp.transpose(a.reshape(-1, N_KV_HEADS, HEAD_DIM, WINDOW), (0, 3, 1, 2))

    return (y_p[None], y_s, pool_p[None, None], jnp.transpose(pool_s, (1, 0, 2))[None],
            swa_out_t(k_p[None]), swa_out_t(v_p[None]), swa_out_t(k_s), swa_out_t(v_s),
            mem_out(mk_t), mem_out(mv_t))
```

```python
import jax
import jax.numpy as jnp
import numpy as np
from jax import lax
from jax.experimental import pallas as pl
from jax.experimental.pallas import tpu as pltpu

D_MODEL = 1024
SEQ = 16384
DEPTH = 2
DEC_BATCH = 128
DEC_SEQ = 8
PAST_LEN = 16384
HEAD_DIM = 64
POOL_W = 768
POOL_WINDOWS = (2, 4, 8, 16)
POOL_GW = 192
POOL_PAD = 15
N_Q_HEADS = 12
N_KV_HEADS = 4
GROUP = 3
SWA_W = 768
KV_W = 256
WINDOW = 128
N_MEM = 256
MEM_HEADS = 4
MEM_W = 256
ROT_DIM = 16
ROPE_THETA = 500000.0
EPS = 1e-6

F32 = jnp.float32
BF16 = jnp.bfloat16
NEG_INF = float("-inf")
LOG2E = 1.4426950408889634
Q_SCALE = HEAD_DIM ** -0.5 * LOG2E

LANES = 128
SUBLANES = 8
MXU_DIM = 256
HEADS_PER_CHUNK = MXU_DIM // HEAD_DIM
CARRY_ROWS = 16
PROMPT_TILE = 512
SAMPLE_BLOCK = 8
VMEM_LIMIT = 56 * 1024 * 1024
STEP_SEQS = DEC_BATCH // (SEQ // PROMPT_TILE)
STEP_VMEM_LIMIT = 60 * 1024 * 1024


_VEC_WIDTHS = (("norm_a", D_MODEL), ("pool_scale", POOL_W), ("kv_norm", D_MODEL), ("k_gain", KV_W),
               ("norm_b", D_MODEL), ("q_gain", SWA_W), ("mem_q_gain0", MEM_W), ("mem_q_gain1", MEM_W))
_VEC_OFFSET = {}
for _name, _width in _VEC_WIDTHS:
    _VEC_OFFSET[_name] = (sum(w for _, w in _VEC_WIDTHS[:len(_VEC_OFFSET)]), _width)


def _vec(vec_ref, name):
    start, width = _VEC_OFFSET[name]
    return vec_ref[:, start:start + width]


def _dot(a, b):
    return jnp.dot(a, b, preferred_element_type=F32)


def _dot_nt(a, b):
    return lax.dot_general(a, b, (((1,), (1,)), ((), ())), preferred_element_type=F32)


def _rms_unit(x):
    return x * lax.rsqrt(jnp.mean(x * x, axis=-1, keepdims=True) + EPS)


def _silu(g):
    return g / (1.0 + jnp.exp(-g))


def _head_rms(y, gain, bseg):
    parts = []
    for c in range(y.shape[1] // MXU_DIM):
        yc = y[:, c * MXU_DIM:(c + 1) * MXU_DIM]
        ms = _dot((yc * yc).astype(BF16), bseg)
        parts.append(yc * lax.rsqrt(ms + EPS) * gain[:, c * MXU_DIM:(c + 1) * MXU_DIM])
    return parts[0] if len(parts) == 1 else jnp.concatenate(parts, axis=1)


def _rope(y, cos_t, sin_t, first8):
    parts = []
    for c in range(y.shape[1] // LANES):
        yc = y[:, c * LANES:(c + 1) * LANES]
        partner = jnp.where(first8, pltpu.roll(yc, LANES - 8, 1), pltpu.roll(yc, 8, 1))
        parts.append(yc * cos_t + partner * sin_t)
    return parts[0] if len(parts) == 1 else jnp.concatenate(parts, axis=1)


def _pool_window_sums(u_hist):
    def back(a, k):
        return pltpu.roll(a, k, 0)

    lane = lax.broadcasted_iota(jnp.int32, (1, LANES), 1)
    s2 = u_hist + back(u_hist, 1)
    t = s2[:, LANES:]
    s4 = t + back(t, 2)
    t = s4[:, 2 * LANES:]
    s8 = t + back(t, 4)
    t = s8[:, LANES:]
    s16 = t + back(t, 8)
    tiles = [
        s2[:, :LANES],
        jnp.where(lane < 64, s2[:, LANES:2 * LANES], s4[:, :LANES]),
        s4[:, LANES:2 * LANES],
        s8[:, :LANES],
        jnp.where(lane < 64, s8[:, LANES:2 * LANES], s16[:, :LANES]),
        s16[:, LANES:],
    ]
    return jnp.concatenate(tiles, axis=1)


def _pool_window_sums_planes(planes):
    n = len(planes)

    def doubled(prev, k, first):
        return [prev[j] + prev[j - k] if j >= first else None for j in range(n)]

    lane = lax.broadcasted_iota(jnp.int32, (1, LANES), 1)
    s2 = doubled(planes, 1, 1)
    s4 = doubled([None if a is None else a[:, LANES:] for a in s2], 2, 3)
    s8 = doubled([None if a is None else a[:, 2 * LANES:] for a in s4], 4, 7)
    s16 = doubled([None if a is None else a[:, LANES:] for a in s8], 8, POOL_PAD)
    out = []
    for j in range(POOL_PAD, n):
        out.append(jnp.concatenate([
            s2[j][:, :LANES],
            jnp.where(lane < 64, s2[j][:, LANES:2 * LANES], s4[j][:, :LANES]),
            s4[j][:, LANES:2 * LANES],
            s8[j][:, :LANES],
            jnp.where(lane < 64, s8[j][:, LANES:2 * LANES], s16[j][:, :LANES]),
            s16[j][:, LANES:],
        ], axis=1))
    return out


def _pool_mix(d, wbd_ref):
    lo, hi = MXU_DIM, 2 * MXU_DIM
    return jnp.concatenate([
        _dot(d[:, :hi], wbd_ref[:hi, :lo]),
        _dot(d, wbd_ref[:, lo:hi]),
        _dot(d[:, lo:], wbd_ref[lo:, hi:]),
    ], axis=1)


def _head_masks():
    lane = lax.broadcasted_iota(jnp.int32, (1, MXU_DIM), 1)
    return [(lane // HEAD_DIM) == j for j in range(HEADS_PER_CHUNK)]


def _first8_mask():
    lane = lax.broadcasted_iota(jnp.int32, (1, LANES), 1)
    return (lane % HEAD_DIM) < (ROT_DIM // 2)


def _stack_heads(q, hmask):
    return jnp.concatenate([jnp.where(m, q, 0.0) for m in hmask], axis=0).astype(BF16)


def _unstack_heads(o, hmask):
    m = o.shape[0] // len(hmask)
    acc = None
    for h, mask in enumerate(hmask):
        oh = jnp.where(mask, o[h * m:(h + 1) * m, :], 0.0)
        acc = oh if acc is None else acc + oh
    return acc


def _sink_columns(sinks_ref, rows_per_head):
    hrow = lax.broadcasted_iota(jnp.int32, (HEADS_PER_CHUNK * rows_per_head, 1), 0) // rows_per_head
    cols = []
    for c in range(SWA_W // MXU_DIM):
        col = jnp.zeros((HEADS_PER_CHUNK * rows_per_head, 1), F32)
        for jj in range(HEADS_PER_CHUNK):
            col = jnp.where(hrow == jj, sinks_ref[c * HEADS_PER_CHUNK + jj], col)
        cols.append(col)
    return cols


def _softmax_weights(s, sink):
    m = jnp.max(s, axis=-1, keepdims=True)
    if sink is not None:
        m = jnp.maximum(m, sink)
    p = jnp.exp2(s - m)
    den = jnp.sum(p, axis=-1, keepdims=True)
    if sink is not None:
        den = den + jnp.exp2(sink - m)
    return p.astype(BF16), 1.0 / den


def _mem_attn(q, k_t, v_t, hmask):
    p, rden = _softmax_weights(_dot(_stack_heads(q, hmask), k_t), None)
    return _unstack_heads(_dot_nt(p, v_t) * rden, hmask)


def _memkv_kernel(mem_ref, norm_ref, wk_ref, wv_ref, kgain_ref, mk_ref, mv_ref, mkb_ref, mvb_ref):
    xn = (_rms_unit(mem_ref[...]) * norm_ref[0]).astype(BF16)
    k_t = _dot_nt(wk_ref[0], xn)
    v_t = _dot_nt(wv_ref[0], xn)
    k3 = k_t.reshape(MEM_HEADS, HEAD_DIM, N_MEM)
    ms = jnp.mean(k3 * k3, axis=1, keepdims=True)
    k_t = (k3 * lax.rsqrt(ms + EPS)).reshape(MEM_W, N_MEM) * kgain_ref[0]
    mk_ref[0] = k_t
    mv_ref[0] = v_t
    mkb_ref[0] = k_t.astype(BF16)
    mvb_ref[0] = v_t.astype(BF16)


def _memkv_call(mem, mem_norm, wk_t, wv_t, kgain_col):
    const2 = lambda l: (0, 0)
    per_layer3 = lambda l: (l, 0, 0)
    out_f = jax.ShapeDtypeStruct((DEPTH, MEM_W, N_MEM), F32)
    out_b = jax.ShapeDtypeStruct((DEPTH, MEM_W, N_MEM), BF16)
    return pl.pallas_call(
        _memkv_kernel,
        grid=(DEPTH,),
        in_specs=[
            pl.BlockSpec((N_MEM, D_MODEL), const2),
            pl.BlockSpec((1, 1, D_MODEL), per_layer3),
            pl.BlockSpec((1, MEM_W, D_MODEL), per_layer3),
            pl.BlockSpec((1, MEM_W, D_MODEL), per_layer3),
            pl.BlockSpec((1, MEM_W, 1), per_layer3),
        ],
        out_specs=[pl.BlockSpec((1, MEM_W, N_MEM), per_layer3)] * 4,
        out_shape=[out_f, out_f, out_b, out_b],
        compiler_params=pltpu.CompilerParams(dimension_semantics=("arbitrary",)),
        name="memkv",
    )(mem, mem_norm, wk_t, wv_t, kgain_col)


def _swa_block(qb, kc, vc, bias, sink_cols, hmask):
    outs = []
    for g in range(GROUP):
        s = _dot_nt(_stack_heads(qb[:, g * MXU_DIM:(g + 1) * MXU_DIM], hmask), kc)
        s = (s.reshape(HEADS_PER_CHUNK, WINDOW, 2 * WINDOW) + bias[None]).reshape(HEADS_PER_CHUNK * WINDOW, 2 * WINDOW)
        p, rden = _softmax_weights(s, sink_cols[g])
        outs.append(_unstack_heads(_dot(p, vc) * rden, hmask))
    return jnp.concatenate(outs, axis=1)


def _prompt_kernel(sinks_ref, x_ref, rope_base_ref, rope_res_ref,
                   vec_ref, w_in_a_ref, wbd_ref, wcol_ref, w_out_a_ref, w_kv_ref,
                   wq_ref, wg_ref, wm_ref, wos_ref, wom_ref,
                   mk_ref, mv_ref, bseg_ref,
                   y_ref, pool_ref, ko_ref, vo_ref,
                   ucarry, kprev, vprev):
    tq = PROMPT_TILE
    i = pl.program_id(0)

    @pl.when(i == 0)
    def _():
        ucarry[0] = jnp.zeros((CARRY_ROWS, POOL_W), F32)
        kprev[0] = jnp.zeros((WINDOW, KV_W), BF16)
        vprev[0] = jnp.zeros((WINDOW, KV_W), BF16)

    rd = i % 2
    wr = (i + 1) % 2

    hmask = _head_masks()
    first8 = _first8_mask()
    bseg = bseg_ref[...]
    x = x_ref[...]

    base = rope_base_ref[i]
    cb, sb = base[0:1, :], base[1:2, :]
    cr, sr = rope_res_ref[0], rope_res_ref[1]
    cos_t = cb * cr - sb * sr
    sin_t = sb * cr + cb * sr

    xn = (_rms_unit(x) * _vec(vec_ref, "norm_a")).astype(BF16)
    u = _dot(xn, w_in_a_ref[:, 0:POOL_W])
    gp = _dot(xn, w_in_a_ref[:, POOL_W:2 * POOL_W])
    qm = _dot(xn, w_in_a_ref[:, 2 * POOL_W:2 * POOL_W + MEM_W])
    gm = _dot(xn, w_in_a_ref[:, 2 * POOL_W + MEM_W:])
    u_hist = jnp.concatenate([ucarry[rd], u], axis=0)
    win = _pool_window_sums(u_hist)[CARRY_ROWS:]
    pos = i * tq + lax.broadcasted_iota(jnp.int32, (tq, 1), 0)
    cnt = jnp.minimum((pos + 1).astype(F32), wcol_ref[...])
    d = win / cnt - u
    yp = _pool_mix(d.astype(BF16), wbd_ref) * _vec(vec_ref, "pool_scale") * _silu(gp)
    qmn = _head_rms(qm, _vec(vec_ref, "mem_q_gain0"), bseg) * Q_SCALE
    ym = _mem_attn(qmn, mk_ref[0], mv_ref[0], hmask) * _silu(gm)
    x1 = x + _dot(jnp.concatenate([yp, ym], axis=1).astype(BF16), w_out_a_ref[...])
    ucarry[wr] = u[tq - CARRY_ROWS:, :]

    r = _rms_unit(x1)
    kv = _dot((r * _vec(vec_ref, "kv_norm")).astype(BF16), w_kv_ref[...])
    k = _rope(_head_rms(kv[:, :KV_W], _vec(vec_ref, "k_gain"), bseg), cos_t, sin_t, first8)
    v = kv[:, KV_W:]
    k_all = jnp.concatenate([kprev[rd], k.astype(BF16)], axis=0)
    v_all = jnp.concatenate([vprev[rd], v.astype(BF16)], axis=0)

    xb = (r * _vec(vec_ref, "norm_b")).astype(BF16)
    zq = _dot(xb, wq_ref[...])
    gq = _dot(xb, wg_ref[...])
    qm2 = _dot(xb, wm_ref[:, 0:MEM_W])
    gm2 = _dot(xb, wm_ref[:, MEM_W:])
    q = _rope(_head_rms(zq, _vec(vec_ref, "q_gain"), bseg), cos_t, sin_t, first8) * Q_SCALE

    qi = lax.broadcasted_iota(jnp.int32, (WINDOW, 2 * WINDOW), 0)
    ci = lax.broadcasted_iota(jnp.int32, (WINDOW, 2 * WINDOW), 1)
    band_bias = jnp.where(ci > qi, jnp.where(ci <= qi + WINDOW, 0.0, NEG_INF), NEG_INF)
    sink_cols = _sink_columns(sinks_ref, WINDOW)
    ys_blocks = []
    for b in range(tq // WINDOW):
        bias = band_bias
        if b == 0:
            key_pos = ci + (i * tq - WINDOW)
            bias = jnp.where(key_pos >= 0, band_bias, NEG_INF)
        qb = q[b * WINDOW:(b + 1) * WINDOW, :]
        kc = k_all[b * WINDOW:(b + 2) * WINDOW, :]
        vc = v_all[b * WINDOW:(b + 2) * WINDOW, :]
        ys_blocks.append(_swa_block(qb, kc, vc, bias, sink_cols, hmask))
    ys = jnp.concatenate(ys_blocks, axis=0) * _silu(gq)
    qmn2 = _head_rms(qm2, _vec(vec_ref, "mem_q_gain1"), bseg) * Q_SCALE
    ym2 = _mem_attn(qmn2, mk_ref[1], mv_ref[1], hmask) * _silu(gm2)
    y_ref[...] = x1 + _dot(ys.astype(BF16), wos_ref[...]) + _dot(ym2.astype(BF16), wom_ref[...])

    kprev[wr] = k_all[tq:, :]
    vprev[wr] = v_all[tq:, :]

    @pl.when(i == pl.num_programs(0) - 1)
    def _():
        ko_ref[...] = k[tq - WINDOW:, :].T
        vo_ref[...] = v[tq - WINDOW:, :].T
        pool_ref[...] = pltpu.roll(u[tq - CARRY_ROWS:, :], CARRY_ROWS - 1, 0)[0:POOL_PAD, :]


def _const_spec(shape):
    nd = len(shape)
    return pl.BlockSpec(shape, lambda i: (0,) * nd, pipeline_mode=pl.Buffered(1))


def _prompt_call(sinks, x, rope_base, rope_res, wts, mk, mv, bseg):
    tq = PROMPT_TILE
    n = x.shape[0]
    row_spec = lambda w: pl.BlockSpec((tq, w), lambda i: (i, 0))
    in_specs = [pl.BlockSpec(memory_space=pltpu.SMEM), row_spec(D_MODEL),
                _const_spec(rope_base.shape), _const_spec(rope_res.shape)]
    in_specs += [_const_spec(w.shape) for w in wts]
    in_specs += [_const_spec(a.shape) for a in (mk, mv, bseg)]
    out_shape = [jax.ShapeDtypeStruct((n, D_MODEL), F32),
                 jax.ShapeDtypeStruct((POOL_PAD, POOL_W), F32),
                 jax.ShapeDtypeStruct((KV_W, WINDOW), F32),
                 jax.ShapeDtypeStruct((KV_W, WINDOW), F32)]
    out_specs = [row_spec(D_MODEL),
                 pl.BlockSpec((POOL_PAD, POOL_W), lambda i: (0, 0)),
                 pl.BlockSpec((KV_W, WINDOW), lambda i: (0, 0)),
                 pl.BlockSpec((KV_W, WINDOW), lambda i: (0, 0))]
    return pl.pallas_call(
        _prompt_kernel,
        grid=(n // tq,),
        in_specs=in_specs,
        out_specs=out_specs,
        out_shape=out_shape,
        scratch_shapes=[pltpu.VMEM((2, CARRY_ROWS, POOL_W), F32),
                        pltpu.VMEM((2, WINDOW, KV_W), BF16),
                        pltpu.VMEM((2, WINDOW, KV_W), BF16)],
        compiler_params=pltpu.CompilerParams(dimension_semantics=("arbitrary",),
                                             vmem_limit_bytes=VMEM_LIMIT),
        name="prompt",
    )(sinks, x, rope_base, rope_res, *wts, mk, mv, bseg)


def _stack_heads_seq(q, n_seq, hmask):
    q3 = q.reshape(n_seq, DEC_SEQ, MXU_DIM)
    return jnp.stack([jnp.where(m, q3, 0.0) for m in hmask], axis=1)


def _unstack_heads_seq(o, n_seq, hmask):
    o4 = o.reshape(n_seq, HEADS_PER_CHUNK, DEC_SEQ, MXU_DIM)
    acc = None
    for h, mask in enumerate(hmask):
        oh = jnp.where(mask, o4[:, h], 0.0)
        acc = oh if acc is None else acc + oh
    return acc.reshape(n_seq * DEC_SEQ, MXU_DIM)


def _mem_attn_seqs(q, k_ref, v_ref, layer, n_seq, hmask):
    rows = HEADS_PER_CHUNK * DEC_SEQ
    qs = _stack_heads_seq(q, n_seq, hmask).reshape(n_seq * rows, MXU_DIM).astype(BF16)
    s = jnp.concatenate([_dot(qs[b * rows:(b + 1) * rows], k_ref[layer, b].astype(BF16))
                         for b in range(n_seq)], axis=0)
    p, rden = _softmax_weights(s, None)
    o = jnp.concatenate([_dot_nt(p[b * rows:(b + 1) * rows], v_ref[layer, b].astype(BF16))
                         for b in range(n_seq)], axis=0)
    return _unstack_heads_seq(o * rden, n_seq, hmask)


def _sample_kernel(sinks_ref, x_ref, pref_ref, ck_ref, cv_ref, cmk_ref, cmv_ref, cos_ref, sin_ref,
                   vec_ref, w_in_a_ref, wbd_ref, wcol_ref, w_out_a_ref, w_kv_ref,
                   wq_ref, wg_ref, wm_ref, wos_ref, wom_ref,
                   bseg_ref,
                   y_ref, pool_ref, ko_ref, vo_ref,
                   u_scr, d_scr):
    sb = SAMPLE_BLOCK
    m = sb * DEC_SEQ
    hmask = _head_masks()
    first8 = _first8_mask()
    bseg = bseg_ref[...]
    cos_t = cos_ref[...]
    sin_t = sin_ref[...]
    x = x_ref[...].reshape(m, D_MODEL)

    xn = (_rms_unit(x) * _vec(vec_ref, "norm_a")).astype(BF16)
    u = _dot(xn, w_in_a_ref[:, 0:POOL_W])
    gp = _dot(xn, w_in_a_ref[:, POOL_W:2 * POOL_W])
    qm = _dot(xn, w_in_a_ref[:, 2 * POOL_W:2 * POOL_W + MEM_W])
    gm = _dot(xn, w_in_a_ref[:, 2 * POOL_W + MEM_W:])

    n_lt = POOL_W // LANES
    for c in range(n_lt):
        u_scr[c] = u[:, c * LANES:(c + 1) * LANES]
    planes = [pref_ref[t] for t in range(POOL_PAD)]
    planes += [jnp.concatenate([u_scr[c, pl.ds(t, sb, stride=DEC_SEQ), :] for c in range(n_lt)], axis=1)
               for t in range(DEC_SEQ)]
    for t, win in enumerate(_pool_window_sums_planes(planes)):
        cnt = jnp.minimum(float(PAST_LEN + t + 1), wcol_ref[...])
        d_t = win / cnt - planes[POOL_PAD + t]
        for c in range(n_lt):
            d_scr[c, pl.ds(t, sb, stride=DEC_SEQ), :] = d_t[:, c * LANES:(c + 1) * LANES]
    d = jnp.concatenate([d_scr[c] for c in range(n_lt)], axis=1)
    yp = _pool_mix(d.astype(BF16), wbd_ref) * _vec(vec_ref, "pool_scale") * _silu(gp)
    for t in range(POOL_PAD):
        pool_ref[t] = planes[DEC_SEQ + t]

    qmn = _head_rms(qm, _vec(vec_ref, "mem_q_gain0"), bseg) * Q_SCALE
    ym = _mem_attn_seqs(qmn, cmk_ref, cmv_ref, 0, sb, hmask) * _silu(gm)
    x1 = x + _dot(jnp.concatenate([yp, ym], axis=1).astype(BF16), w_out_a_ref[...])

    r = _rms_unit(x1)
    kv = _dot((r * _vec(vec_ref, "kv_norm")).astype(BF16), w_kv_ref[...])
    k = _rope(_head_rms(kv[:, :KV_W], _vec(vec_ref, "k_gain"), bseg), cos_t, sin_t, first8)
    v = kv[:, KV_W:]
    keep = WINDOW - DEC_SEQ
    key_lane = lax.broadcasted_iota(jnp.int32, (1, WINDOW), 1)
    pad_rows = jnp.zeros((WINDOW - m, KV_W), F32)
    for new_rows, c_ref, o_ref in ((k, ck_ref, ko_ref), (v, cv_ref, vo_ref)):
        new_t = jnp.concatenate([new_rows, pad_rows], axis=0).T
        for b in range(sb):
            shifted = pltpu.roll(c_ref[b], keep, 1)
            placed = pltpu.roll(new_t, (keep - b * DEC_SEQ) % WINDOW, 1)
            o_ref[b] = jnp.where(key_lane < keep, shifted, placed)

    xb = (r * _vec(vec_ref, "norm_b")).astype(BF16)
    zq = _dot(xb, wq_ref[...])
    gq = _dot(xb, wg_ref[...])
    qm2 = _dot(xb, wm_ref[:, 0:MEM_W])
    gm2 = _dot(xb, wm_ref[:, MEM_W:])
    q = _rope(_head_rms(zq, _vec(vec_ref, "q_gain"), bseg), cos_t, sin_t, first8) * Q_SCALE

    rows = N_Q_HEADS * DEC_SEQ
    qs = jnp.stack([_stack_heads_seq(q[:, g * MXU_DIM:(g + 1) * MXU_DIM], sb, hmask) for g in range(GROUP)], axis=1)
    qs = qs.reshape(sb * rows, KV_W).astype(BF16)
    s_old = jnp.concatenate([_dot(qs[b * rows:(b + 1) * rows], ck_ref[b].astype(BF16))
                             for b in range(sb)], axis=0)
    s_new = _dot_nt(qs, k.astype(BF16))
    tq_old = lax.broadcasted_iota(jnp.int32, (rows, WINDOW), 0) % DEC_SEQ
    key_old = lax.broadcasted_iota(jnp.int32, (rows, WINDOW), 1)
    bias_old = jnp.where(key_old > tq_old, 0.0, NEG_INF)
    row_i = lax.broadcasted_iota(jnp.int32, (sb * rows, m), 0)
    col_i = lax.broadcasted_iota(jnp.int32, (sb * rows, m), 1)
    same_seq = (row_i // rows) == (col_i // DEC_SEQ)
    bias_new = jnp.where(same_seq, jnp.where(col_i % DEC_SEQ <= row_i % DEC_SEQ, 0.0, NEG_INF), NEG_INF)
    s_old = (s_old.reshape(sb, rows, WINDOW) + bias_old[None]).reshape(sb * rows, WINDOW)
    s_new = s_new + bias_new
    sink = jnp.concatenate(_sink_columns(sinks_ref, DEC_SEQ), axis=0)
    sink = jnp.concatenate([sink] * sb, axis=0)
    mx = jnp.maximum(jnp.maximum(jnp.max(s_old, axis=-1, keepdims=True), jnp.max(s_new, axis=-1, keepdims=True)), sink)
    p_old = jnp.exp2(s_old - mx)
    p_new = jnp.exp2(s_new - mx)
    den = (jnp.sum(p_old, axis=-1, keepdims=True) + jnp.sum(p_new, axis=-1, keepdims=True) + jnp.exp2(sink - mx))
    p_old = p_old.astype(BF16)
    o = jnp.concatenate([_dot_nt(p_old[b * rows:(b + 1) * rows], cv_ref[b].astype(BF16)) for b in range(sb)], axis=0)
    o = (o + _dot(p_new.astype(BF16), v.astype(BF16))) * (1.0 / den)
    o5 = o.reshape(sb, GROUP, HEADS_PER_CHUNK * DEC_SEQ, KV_W)
    ys = jnp.concatenate([_unstack_heads_seq(o5[:, g].reshape(sb * HEADS_PER_CHUNK * DEC_SEQ, KV_W), sb, hmask)
                          for g in range(GROUP)], axis=1) * _silu(gq)

    qmn2 = _head_rms(qm2, _vec(vec_ref, "mem_q_gain1"), bseg) * Q_SCALE
    ym2 = _mem_attn_seqs(qmn2, cmk_ref, cmv_ref, 1, sb, hmask) * _silu(gm2)
    y = x1 + _dot(ys.astype(BF16), wos_ref[...]) + _dot(ym2.astype(BF16), wom_ref[...])
    y_ref[...] = y.reshape(sb, DEC_SEQ, D_MODEL)


def _sample_call(sinks, x, pref, ck, cv, cmk_t, cmv_t, cos_t, sin_t, wts, bseg):
    sb = SAMPLE_BLOCK
    nb = x.shape[0]
    seq3 = lambda a, b_: pl.BlockSpec((sb, a, b_), lambda i: (i, 0, 0))
    pool_spec = pl.BlockSpec((POOL_PAD, sb, POOL_W), lambda i: (0, i, 0))
    in_specs = [pl.BlockSpec(memory_space=pltpu.SMEM),
                seq3(DEC_SEQ, D_MODEL), pool_spec, seq3(KV_W, WINDOW), seq3(KV_W, WINDOW),
                pl.BlockSpec((DEPTH, sb, MEM_W, N_MEM), lambda i: (0, i, 0, 0)),
                pl.BlockSpec((DEPTH, sb, MEM_W, N_MEM), lambda i: (0, i, 0, 0)),
                _const_spec(cos_t.shape), _const_spec(sin_t.shape)]
    in_specs += [_const_spec(w.shape) for w in wts]
    in_specs += [_const_spec(bseg.shape)]
    out_shape = [jax.ShapeDtypeStruct((nb, DEC_SEQ, D_MODEL), F32),
                 jax.ShapeDtypeStruct((POOL_PAD, nb, POOL_W), F32),
                 jax.ShapeDtypeStruct((nb, KV_W, WINDOW), F32),
                 jax.ShapeDtypeStruct((nb, KV_W, WINDOW), F32)]
    out_specs = [seq3(DEC_SEQ, D_MODEL), pool_spec, seq3(KV_W, WINDOW), seq3(KV_W, WINDOW)]
    return pl.pallas_call(
        _sample_kernel,
        grid=(nb // sb,),
        in_specs=in_specs,
        out_specs=out_specs,
        out_shape=out_shape,
        scratch_shapes=[pltpu.VMEM((POOL_W // LANES, sb * DEC_SEQ, LANES), F32),
                        pltpu.VMEM((POOL_W // LANES, sb * DEC_SEQ, LANES), F32)],
        compiler_params=pltpu.CompilerParams(dimension_semantics=("arbitrary",),
                                             vmem_limit_bytes=VMEM_LIMIT),
        name="sample",
    )(sinks, x, pref, ck, cv, cmk_t, cmv_t, cos_t, sin_t, *wts, bseg)


def _sample_pool(u_s, pref_ref, pool_ref, wcol, parity, u_scr, d_scr):
    sb = STEP_SEQS
    n_lt = POOL_W // LANES
    for c in range(n_lt):
        u_scr[c] = u_s[:, c * LANES:(c + 1) * LANES]
    planes = [jnp.where(parity == 0, pref_ref[t, 0:sb, :], pref_ref[t, sb:2 * sb, :]) for t in range(POOL_PAD)]
    planes += [jnp.concatenate([u_scr[c, pl.ds(t, sb, stride=DEC_SEQ), :] for c in range(n_lt)], axis=1)
               for t in range(DEC_SEQ)]
    for t, win in enumerate(_pool_window_sums_planes(planes)):
        cnt = jnp.minimum(float(PAST_LEN + t + 1), wcol)
        d_t = win / cnt - planes[POOL_PAD + t]
        for c in range(n_lt):
            d_scr[c, pl.ds(t, sb, stride=DEC_SEQ), :] = d_t[:, c * LANES:(c + 1) * LANES]

    @pl.when(parity == 0)
    def _():
        for t in range(POOL_PAD):
            pool_ref[t, 0:sb, :] = planes[DEC_SEQ + t]

    @pl.when(parity == 1)
    def _():
        for t in range(POOL_PAD):
            pool_ref[t, sb:2 * sb, :] = planes[DEC_SEQ + t]

    return jnp.concatenate([d_scr[c] for c in range(n_lt)], axis=1)


def _sample_cache_update(k_s, v_s, ck_ref, cv_ref, ko_ref, vo_ref):
    sb = STEP_SEQS
    keep = WINDOW - DEC_SEQ
    key_lane = lax.broadcasted_iota(jnp.int32, (1, WINDOW), 1)
    pad_rows = jnp.zeros((WINDOW - sb * DEC_SEQ, KV_W), F32)
    for new_rows, c_ref, o_ref in ((k_s, ck_ref, ko_ref), (v_s, cv_ref, vo_ref)):
        new_t = jnp.concatenate([new_rows, pad_rows], axis=0).T
        for b in range(sb):
            shifted = pltpu.roll(c_ref[b], keep, 1)
            placed = pltpu.roll(new_t, (keep - b * DEC_SEQ) % WINDOW, 1)
            o_ref[b] = jnp.where(key_lane < keep, shifted, placed)


def _sample_window_attn(q_s, k_s, v_s, ck_ref, cv_ref, sinks_ref, hmask):
    sb = STEP_SEQS
    m = sb * DEC_SEQ
    rows = N_Q_HEADS * DEC_SEQ
    qs = jnp.stack([_stack_heads_seq(q_s[:, g * MXU_DIM:(g + 1) * MXU_DIM], sb, hmask) for g in range(GROUP)],
                   axis=1)
    qs = qs.reshape(sb * rows, KV_W).astype(BF16)
    s_old = jnp.concatenate([_dot(qs[b * rows:(b + 1) * rows], ck_ref[b].astype(BF16)) for b in range(sb)], axis=0)
    s_new = _dot_nt(qs, k_s.astype(BF16))
    tq_old = lax.broadcasted_iota(jnp.int32, (rows, WINDOW), 0) % DEC_SEQ
    key_old = lax.broadcasted_iota(jnp.int32, (rows, WINDOW), 1)
    bias_old = jnp.where(key_old > tq_old, 0.0, NEG_INF)
    row_i = lax.broadcasted_iota(jnp.int32, (sb * rows, m), 0)
    col_i = lax.broadcasted_iota(jnp.int32, (sb * rows, m), 1)
    same_seq = (row_i // rows) == (col_i // DEC_SEQ)
    bias_new = jnp.where(same_seq, jnp.where(col_i % DEC_SEQ <= row_i % DEC_SEQ, 0.0, NEG_INF), NEG_INF)
    s_old = (s_old.reshape(sb, rows, WINDOW) + bias_old[None]).reshape(sb * rows, WINDOW)
    s_new = s_new + bias_new
    sink = jnp.concatenate(_sink_columns(sinks_ref, DEC_SEQ), axis=0)
    sink = jnp.concatenate([sink] * sb, axis=0)
    mx = jnp.maximum(jnp.maximum(jnp.max(s_old, axis=-1, keepdims=True), jnp.max(s_new, axis=-1, keepdims=True)), sink)
    p_old = jnp.exp2(s_old - mx)
    p_new = jnp.exp2(s_new - mx)
    den = (jnp.sum(p_old, axis=-1, keepdims=True) + jnp.sum(p_new, axis=-1, keepdims=True) + jnp.exp2(sink - mx))
    p_old = p_old.astype(BF16)
    o = jnp.concatenate([_dot_nt(p_old[b * rows:(b + 1) * rows], cv_ref[b].astype(BF16)) for b in range(sb)], axis=0)
    o = (o + _dot(p_new.astype(BF16), v_s.astype(BF16))) * (1.0 / den)
    o5 = o.reshape(sb, GROUP, HEADS_PER_CHUNK * DEC_SEQ, KV_W)
    return jnp.concatenate([_unstack_heads_seq(o5[:, g].reshape(sb * HEADS_PER_CHUNK * DEC_SEQ, KV_W), sb, hmask)
                            for g in range(GROUP)], axis=1)


def _step_kernel(sinks_ref, xp_ref, xs_ref, pref_ref, ck_ref, cv_ref, cmk_ref, cmv_ref,
                 rope_base_ref, rope_res_ref, rope_s_ref,
                 vec_ref, w_in_a_ref, wbd_ref, wcol_ref, w_out_a_ref, w_kv_ref,
                 wq_ref, wg_ref, wm_ref, wos_ref, wom_ref,
                 mk_ref, mv_ref, bseg_ref,
                 yp_ref, pool_p_ref, kop_ref, vop_ref, ys_ref, pool_s_ref, kos_ref, vos_ref,
                 ucarry, kprev, vprev, u_scr, d_scr):
    tq = PROMPT_TILE
    sb = STEP_SEQS
    m = sb * DEC_SEQ
    i = pl.program_id(0)

    @pl.when(i == 0)
    def _():
        ucarry[0] = jnp.zeros((CARRY_ROWS, POOL_W), F32)
        kprev[0] = jnp.zeros((WINDOW, KV_W), BF16)
        vprev[0] = jnp.zeros((WINDOW, KV_W), BF16)

    rd = i % 2
    wr = (i + 1) % 2

    hmask = _head_masks()
    first8 = _first8_mask()
    bseg = bseg_ref[...]
    x = jnp.concatenate([xp_ref[...], xs_ref[...].reshape(m, D_MODEL)], axis=0)

    base = rope_base_ref[i]
    cb, sb_ = base[0:1, :], base[1:2, :]
    cr, sr = rope_res_ref[0], rope_res_ref[1]
    cos_t = jnp.concatenate([cb * cr - sb_ * sr, rope_s_ref[0]], axis=0)
    sin_t = jnp.concatenate([sb_ * cr + cb * sr, rope_s_ref[1]], axis=0)

    xn = (_rms_unit(x) * _vec(vec_ref, "norm_a")).astype(BF16)
    u = _dot(xn, w_in_a_ref[:, 0:POOL_W])
    gp = _dot(xn, w_in_a_ref[:, POOL_W:2 * POOL_W])
    qm = _dot(xn, w_in_a_ref[:, 2 * POOL_W:2 * POOL_W + MEM_W])
    gm = _dot(xn, w_in_a_ref[:, 2 * POOL_W + MEM_W:])
    u_p = u[:tq]
    u_hist = jnp.concatenate([ucarry[rd], u_p], axis=0)
    win = _pool_window_sums(u_hist)[CARRY_ROWS:]
    pos = i * tq + lax.broadcasted_iota(jnp.int32, (tq, 1), 0)
    cnt = jnp.minimum((pos + 1).astype(F32), wcol_ref[...])
    d_p = win / cnt - u_p
    d_s = _sample_pool(u[tq:], pref_ref, pool_s_ref, wcol_ref[...], rd, u_scr, d_scr)
    d = jnp.concatenate([d_p, d_s], axis=0)
    yp = _pool_mix(d.astype(BF16), wbd_ref) * _vec(vec_ref, "pool_scale") * _silu(gp)
    qmn = _head_rms(qm, _vec(vec_ref, "mem_q_gain0"), bseg) * Q_SCALE
    ym = jnp.concatenate([_mem_attn(qmn[:tq], mk_ref[0], mv_ref[0], hmask),
                          _mem_attn_seqs(qmn[tq:], cmk_ref, cmv_ref, 0, sb, hmask)], axis=0) * _silu(gm)
    x1 = x + _dot(jnp.concatenate([yp, ym], axis=1).astype(BF16), w_out_a_ref[...])
    ucarry[wr] = u_p[tq - CARRY_ROWS:, :]

    r = _rms_unit(x1)
    kv = _dot((r * _vec(vec_ref, "kv_norm")).astype(BF16), w_kv_ref[...])
    k = _rope(_head_rms(kv[:, :KV_W], _vec(vec_ref, "k_gain"), bseg), cos_t, sin_t, first8)
    v = kv[:, KV_W:]
    k_p, v_p, k_s, v_s = k[:tq], v[:tq], k[tq:], v[tq:]
    k_all = jnp.concatenate([kprev[rd], k_p.astype(BF16)], axis=0)
    v_all = jnp.concatenate([vprev[rd], v_p.astype(BF16)], axis=0)
    _sample_cache_update(k_s, v_s, ck_ref, cv_ref, kos_ref, vos_ref)

    xb = (r * _vec(vec_ref, "norm_b")).astype(BF16)
    zq = _dot(xb, wq_ref[...])
    gq = _dot(xb, wg_ref[...])
    qm2 = _dot(xb, wm_ref[:, 0:MEM_W])
    gm2 = _dot(xb, wm_ref[:, MEM_W:])
    q = _rope(_head_rms(zq, _vec(vec_ref, "q_gain"), bseg), cos_t, sin_t, first8) * Q_SCALE

    qi = lax.broadcasted_iota(jnp.int32, (WINDOW, 2 * WINDOW), 0)
    ci = lax.broadcasted_iota(jnp.int32, (WINDOW, 2 * WINDOW), 1)
    band_bias = jnp.where(ci > qi, jnp.where(ci <= qi + WINDOW, 0.0, NEG_INF), NEG_INF)
    sink_cols = _sink_columns(sinks_ref, WINDOW)
    ys_blocks = []
    for b in range(tq // WINDOW):
        bias = band_bias
        if b == 0:
            key_pos = ci + (i * tq - WINDOW)
            bias = jnp.where(key_pos >= 0, band_bias, NEG_INF)
        qb = q[b * WINDOW:(b + 1) * WINDOW, :]
        kc = k_all[b * WINDOW:(b + 2) * WINDOW, :]
        vc = v_all[b * WINDOW:(b + 2) * WINDOW, :]
        ys_blocks.append(_swa_block(qb, kc, vc, bias, sink_cols, hmask))
    ys_blocks.append(_sample_window_attn(q[tq:], k_s, v_s, ck_ref, cv_ref, sinks_ref, hmask))
    ys = jnp.concatenate(ys_blocks, axis=0) * _silu(gq)
    qmn2 = _head_rms(qm2, _vec(vec_ref, "mem_q_gain1"), bseg) * Q_SCALE
    ym2 = jnp.concatenate([_mem_attn(qmn2[:tq], mk_ref[1], mv_ref[1], hmask),
                           _mem_attn_seqs(qmn2[tq:], cmk_ref, cmv_ref, 1, sb, hmask)], axis=0) * _silu(gm2)
    y = x1 + _dot(ys.astype(BF16), wos_ref[...]) + _dot(ym2.astype(BF16), wom_ref[...])
    yp_ref[...] = y[:tq]
    ys_ref[...] = y[tq:].reshape(sb, DEC_SEQ, D_MODEL)

    kprev[wr] = k_all[tq:, :]
    vprev[wr] = v_all[tq:, :]

    @pl.when(i == pl.num_programs(0) - 1)
    def _():
        kop_ref[...] = k_p[tq - WINDOW:, :].T
        vop_ref[...] = v_p[tq - WINDOW:, :].T
        pool_p_ref[...] = pltpu.roll(u_p[tq - CARRY_ROWS:, :], CARRY_ROWS - 1, 0)[0:POOL_PAD, :]


def _step_call(sinks, x_p, x_s, pref, ck, cv, cmk_t, cmv_t, rope_base, rope_res, rope_s, wts, mk, mv, bseg):
    tq = PROMPT_TILE
    sb = STEP_SEQS
    n = x_p.shape[0]
    nb = x_s.shape[0]
    assert n // tq == nb // sb
    row_spec = lambda w: pl.BlockSpec((tq, w), lambda i: (i, 0))
    seq3 = lambda a, b_: pl.BlockSpec((sb, a, b_), lambda i: (i, 0, 0))
    pool_s_spec = pl.BlockSpec((POOL_PAD, 2 * sb, POOL_W), lambda i: (0, i // 2, 0))
    cache_mem_spec = pl.BlockSpec((DEPTH, sb, MEM_W, N_MEM), lambda i: (0, i, 0, 0))
    in_specs = [pl.BlockSpec(memory_space=pltpu.SMEM), row_spec(D_MODEL),
                seq3(DEC_SEQ, D_MODEL), pool_s_spec, seq3(KV_W, WINDOW), seq3(KV_W, WINDOW),
                cache_mem_spec, cache_mem_spec,
                _const_spec(rope_base.shape), _const_spec(rope_res.shape), _const_spec(rope_s.shape)]
    in_specs += [_const_spec(w.shape) for w in wts]
    in_specs += [_const_spec(a.shape) for a in (mk, mv, bseg)]
    out_shape = [jax.ShapeDtypeStruct((n, D_MODEL), F32),
                 jax.ShapeDtypeStruct((POOL_PAD, POOL_W), F32),
                 jax.ShapeDtypeStruct((KV_W, WINDOW), F32),
                 jax.ShapeDtypeStruct((KV_W, WINDOW), F32),
                 jax.ShapeDtypeStruct((nb, DEC_SEQ, D_MODEL), F32),
                 jax.ShapeDtypeStruct((POOL_PAD, nb, POOL_W), F32),
                 jax.ShapeDtypeStruct((nb, KV_W, WINDOW), F32),
                 jax.ShapeDtypeStruct((nb, KV_W, WINDOW), F32)]
    out_specs = [row_spec(D_MODEL),
                 pl.BlockSpec((POOL_PAD, POOL_W), lambda i: (0, 0)),
                 pl.BlockSpec((KV_W, WINDOW), lambda i: (0, 0)),
                 pl.BlockSpec((KV_W, WINDOW), lambda i: (0, 0)),
                 seq3(DEC_SEQ, D_MODEL), pool_s_spec, seq3(KV_W, WINDOW), seq3(KV_W, WINDOW)]
    return pl.pallas_call(
        _step_kernel,
        grid=(n // tq,),
        in_specs=in_specs,
        out_specs=out_specs,
        out_shape=out_shape,
        scratch_shapes=[pltpu.VMEM((2, CARRY_ROWS, POOL_W), F32),
                        pltpu.VMEM((2, WINDOW, KV_W), BF16),
                        pltpu.VMEM((2, WINDOW, KV_W), BF16),
                        pltpu.VMEM((POOL_W // LANES, sb * DEC_SEQ, LANES), F32),
                        pltpu.VMEM((POOL_W // LANES, sb * DEC_SEQ, LANES), F32)],
        compiler_params=pltpu.CompilerParams(dimension_semantics=("arbitrary",),
                                             vmem_limit_bytes=STEP_VMEM_LIMIT),
        name="step",
    )(sinks, x_p, x_s, pref, ck, cv, cmk_t, cmv_t, rope_base, rope_res, rope_s, *wts, mk, mv, bseg)


PREP_ROWS = 128
HEAD_PAIR_BLOCKS = SWA_W // PREP_ROWS


def _cols_group_major(w):
    heads = [w[:, (kvh * GROUP + g) * HEAD_DIM:(kvh * GROUP + g + 1) * HEAD_DIM]
             for g in range(GROUP) for kvh in range(N_KV_HEADS)]
    return jnp.concatenate(heads, axis=1)


def _prep_kernel(w_in_a_ref, w_out_a_ref, w_kv_ref, w_in_b_ref, wob_a_ref, wob_b_ref, wob_m_ref, w_mem_ref,
                 in_a_ref, out_a_ref, kv_ref, wq_ref, wg_ref, wm_ref, wos_ref, wom_ref, wkt_ref, wvt_ref):
    in_a_ref[...] = w_in_a_ref[...].astype(BF16)
    out_a_ref[...] = w_out_a_ref[...].astype(BF16)
    kv_ref[...] = w_kv_ref[...].astype(BF16)
    w_in_b = w_in_b_ref[...]
    wq_ref[...] = _cols_group_major(w_in_b[:, :SWA_W]).astype(BF16)
    wg_ref[...] = _cols_group_major(w_in_b[:, SWA_W:2 * SWA_W]).astype(BF16)
    wm_ref[...] = w_in_b[:, 2 * SWA_W:].astype(BF16)
    wos_ref[...] = jnp.concatenate([wob_a_ref[0], wob_b_ref[0]], axis=0).astype(BF16)
    wom_ref[...] = wob_m_ref[...].astype(BF16)
    for l in range(DEPTH):
        w_mem = w_mem_ref[l]
        wkt_ref[l] = w_mem[:, :MEM_W].T.astype(BF16)
        wvt_ref[l] = w_mem[:, MEM_W:].T.astype(BF16)


def _prep_call(w_in_a, w_out_a, w_kv, w_in_b, w_out_b, w_mem_kv):
    rows = lambda w: pl.BlockSpec((PREP_ROWS, w), lambda i: (i, 0))

    def head_src(second):
        def index(i):
            j = jnp.minimum(i, HEAD_PAIR_BLOCKS - 1)
            return ((2 * (j % 2) + second) * GROUP + j // 2, 0, 0)
        return pl.BlockSpec((1, HEAD_DIM, D_MODEL), index)

    per_mem_block = (D_MODEL // PREP_ROWS) // (MEM_W // PREP_ROWS)
    in_specs = [rows(2 * POOL_W + 2 * MEM_W), rows(D_MODEL), rows(2 * KV_W), rows(2 * SWA_W + 2 * MEM_W),
                head_src(0), head_src(1),
                pl.BlockSpec((PREP_ROWS, D_MODEL), lambda i: (HEAD_PAIR_BLOCKS + i // per_mem_block, 0)),
                pl.BlockSpec((DEPTH, PREP_ROWS, 2 * MEM_W), lambda i: (0, i, 0))]
    out_specs = [rows(2 * POOL_W + 2 * MEM_W), rows(D_MODEL), rows(2 * KV_W), rows(SWA_W), rows(SWA_W),
                 rows(2 * MEM_W),
                 pl.BlockSpec((PREP_ROWS, D_MODEL), lambda i: (jnp.minimum(i, HEAD_PAIR_BLOCKS - 1), 0)),
                 pl.BlockSpec((PREP_ROWS, D_MODEL), lambda i: (i // per_mem_block, 0)),
                 pl.BlockSpec((DEPTH, MEM_W, PREP_ROWS), lambda i: (0, 0, i)),
                 pl.BlockSpec((DEPTH, MEM_W, PREP_ROWS), lambda i: (0, 0, i))]
    bf = lambda *shape: jax.ShapeDtypeStruct(shape, BF16)
    out_shape = [bf(D_MODEL, 2 * POOL_W + 2 * MEM_W), bf(D_MODEL, D_MODEL), bf(D_MODEL, 2 * KV_W),
                 bf(D_MODEL, SWA_W), bf(D_MODEL, SWA_W), bf(D_MODEL, 2 * MEM_W),
                 bf(SWA_W, D_MODEL), bf(MEM_W, D_MODEL),
                 bf(DEPTH, MEM_W, D_MODEL), bf(DEPTH, MEM_W, D_MODEL)]
    w_out_b_heads = w_out_b.reshape(D_MODEL // HEAD_DIM, HEAD_DIM, D_MODEL)
    return pl.pallas_call(
        _prep_kernel,
        grid=(D_MODEL // PREP_ROWS,),
        in_specs=in_specs,
        out_specs=out_specs,
        out_shape=out_shape,
        compiler_params=pltpu.CompilerParams(dimension_semantics=("arbitrary",)),
        name="prep",
    )(w_in_a, w_out_a, w_kv, w_in_b, w_out_b_heads, w_out_b_heads, w_out_b, w_mem_kv)


def _rope_lane_tables(pos):
    half = ROT_DIM // 2
    inv = (ROPE_THETA ** (-np.arange(half, dtype=np.float32) * 2.0 / ROT_DIM)).astype(np.float32)
    ang = (np.asarray(pos, np.float32)[:, None] * inv[None, :]).astype(np.float64)
    cos, sin = np.cos(ang), np.sin(ang)
    t = ang.shape[0]
    rest = HEAD_DIM - ROT_DIM
    cos64 = np.concatenate([cos, cos, np.ones((t, rest))], axis=1)
    sin64 = np.concatenate([-sin, sin, np.zeros((t, rest))], axis=1)
    reps = LANES // HEAD_DIM
    return np.tile(cos64, (1, reps)).astype(np.float32), np.tile(sin64, (1, reps)).astype(np.float32)


def _head_tile(g, width):
    return jnp.tile(g.astype(F32), width // HEAD_DIM)


def kernel(x_prompt, x_sample, state_pool, cache_swa_k, cache_swa_v, cache_mem_k, cache_mem_v, mem_prompt,
           norm_a, w_in_a, pool_mix_w, pool_scale, w_out_a, kv_norm, w_kv, k_norm,
           norm_b, w_in_b, q_norm, sinks, w_out_b, mem_norm, w_mem_kv, mem_q_norm, mem_k_norm):
    seg = np.arange(MXU_DIM) // HEAD_DIM
    bseg = jnp.asarray((seg[:, None] == seg[None, :]).astype(np.float32) / HEAD_DIM, BF16)
    wcol = jnp.asarray(np.repeat(np.asarray(POOL_WINDOWS, np.float32), POOL_GW).reshape(1, POOL_W))

    w_in_a_b, w_out_a_b, w_kv_b, wq, wg, wm, wos, wom, wk_t, wv_t = _prep_call(
        w_in_a[0], w_out_a[0], w_kv, w_in_b[0], w_out_b[0], w_mem_kv)
    sinks_p = sinks[0].astype(F32).reshape(N_KV_HEADS, GROUP).T.reshape(N_Q_HEADS) * LOG2E

    n_grp = len(POOL_WINDOWS)
    wbd = (pool_mix_w[0][:, :, None, :] * jnp.eye(n_grp, dtype=F32)[:, None, :, None]).reshape(
        POOL_W, POOL_W).astype(BF16)
    pieces = {"norm_a": norm_a[0], "pool_scale": pool_scale[0], "kv_norm": kv_norm,
              "k_gain": _head_tile(k_norm, KV_W), "norm_b": norm_b[0], "q_gain": _head_tile(q_norm[0], SWA_W),
              "mem_q_gain0": _head_tile(mem_q_norm[0], MEM_W), "mem_q_gain1": _head_tile(mem_q_norm[1], MEM_W)}
    vecs = jnp.concatenate([pieces[name].astype(F32) for name, _ in _VEC_WIDTHS]).reshape(1, -1)
    wts = (vecs, w_in_a_b, wbd, wcol, w_out_a_b, w_kv_b, wq, wg, wm, wos, wom)
    mkg_col = jnp.stack([jnp.tile(mem_k_norm[l].astype(F32), MEM_HEADS).reshape(MEM_W, 1) for l in range(DEPTH)])

    mk_t, mv_t, mk_b, mv_b = _memkv_call(mem_prompt[0], mem_norm.reshape(DEPTH, 1, D_MODEL), wk_t, wv_t, mkg_col)

    n_tiles = SEQ // PROMPT_TILE
    cb, sb_ = _rope_lane_tables(np.arange(n_tiles) * PROMPT_TILE)
    cr, sr = _rope_lane_tables(np.arange(PROMPT_TILE))
    rope_base = jnp.asarray(np.stack([cb, sb_], axis=1))
    rope_res = jnp.asarray(np.stack([cr, sr], axis=0))
    cos_s, sin_s = _rope_lane_tables(PAST_LEN + np.arange(DEC_SEQ))
    rope_s = jnp.asarray(np.stack([np.tile(cos_s, (STEP_SEQS, 1)), np.tile(sin_s, (STEP_SEQS, 1))], axis=0))
    pref = jnp.transpose(state_pool[0], (1, 0, 2))
    cmk_t = jnp.transpose(cache_mem_k, (0, 1, 3, 4, 2)).reshape(DEPTH, DEC_BATCH, MEM_W, N_MEM)
    cmv_t = jnp.transpose(cache_mem_v, (0, 1, 3, 4, 2)).reshape(DEPTH, DEC_BATCH, MEM_W, N_MEM)
    ck_t = jnp.transpose(cache_swa_k, (0, 2, 3, 1)).reshape(DEC_BATCH, KV_W, WINDOW)
    cv_t = jnp.transpose(cache_swa_v, (0, 2, 3, 1)).reshape(DEC_BATCH, KV_W, WINDOW)
    y_p, pool_p, k_p, v_p, y_s, pool_s, k_s, v_s = _step_call(
        sinks_p, x_prompt[0], x_sample, pref, ck_t, cv_t, cmk_t, cmv_t, rope_base, rope_res, rope_s,
        wts, mk_b, mv_b, bseg)

    def mem_out(a):
        return jnp.transpose(a.reshape(DEPTH, 1, MEM_HEADS, HEAD_DIM, N_MEM), (0, 1, 4, 2, 3))

    def swa_out_t(a):
        return jnp.transpose(a.reshape(-1, N_KV_HEADS, HEAD_DIM, WINDOW), (0, 3, 1, 2))

    return (y_p[None], y_s, pool_p[None, None], jnp.transpose(pool_s, (1, 0, 2))[None],
            swa_out_t(k_p[None]), swa_out_t(v_p[None]), swa_out_t(k_s), swa_out_t(v_s),
            mem_out(mk_t), mem_out(mv_t))
```

```python
import jax
import jax.numpy as jnp
import numpy as np
from jax import lax
from jax.experimental import pallas as pl
from jax.experimental.pallas import tpu as pltpu

D_MODEL = 1024
SEQ = 16384
DEPTH = 2
DEC_BATCH = 128
DEC_SEQ = 8
PAST_LEN = 16384
HEAD_DIM = 64
POOL_W = 768
POOL_WINDOWS = (2, 4, 8, 16)
POOL_GW = 192
POOL_PAD = 15
N_Q_HEADS = 12
N_KV_HEADS = 4
GROUP = 3
SWA_W = 768
KV_W = 256
WINDOW = 128
N_MEM = 256
MEM_HEADS = 4
MEM_W = 256
ROT_DIM = 16
ROPE_THETA = 500000.0
EPS = 1e-6

F32 = jnp.float32
BF16 = jnp.bfloat16
NEG_INF = float("-inf")
LOG2E = 1.4426950408889634
Q_SCALE = HEAD_DIM ** -0.5 * LOG2E

LANES = 128
SUBLANES = 8
MXU_DIM = 256
HEADS_PER_CHUNK = MXU_DIM // HEAD_DIM
CARRY_ROWS = 16
PROMPT_TILE = 512
STEP_SEQS = DEC_BATCH // (SEQ // PROMPT_TILE)
STEP_VMEM_LIMIT = 60 * 1024 * 1024


_VEC_WIDTHS = (("norm_a", D_MODEL), ("pool_scale", POOL_W), ("kv_norm", D_MODEL), ("k_gain", KV_W),
               ("norm_b", D_MODEL), ("q_gain", SWA_W), ("mem_q_gain0", MEM_W), ("mem_q_gain1", MEM_W))
_VEC_OFFSET = {}
for _name, _width in _VEC_WIDTHS:
    _VEC_OFFSET[_name] = (sum(w for _, w in _VEC_WIDTHS[:len(_VEC_OFFSET)]), _width)


def _vec(vec_ref, name):
    start, width = _VEC_OFFSET[name]
    return vec_ref[:, start:start + width]


def _dot(a, b):
    return jnp.dot(a, b, preferred_element_type=F32)


def _dot_nt(a, b):
    return lax.dot_general(a, b, (((1,), (1,)), ((), ())), preferred_element_type=F32)


def _rms_unit(x):
    return x * lax.rsqrt(jnp.mean(x * x, axis=-1, keepdims=True) + EPS)


def _silu(g):
    return g / (1.0 + jnp.exp(-g))


def _head_rms(y, gain, bseg):
    parts = []
    for c in range(y.shape[1] // MXU_DIM):
        yc = y[:, c * MXU_DIM:(c + 1) * MXU_DIM]
        ms = _dot((yc * yc).astype(BF16), bseg)
        parts.append(yc * lax.rsqrt(ms + EPS) * gain[:, c * MXU_DIM:(c + 1) * MXU_DIM])
    return parts[0] if len(parts) == 1 else jnp.concatenate(parts, axis=1)


def _rope(y, cos_t, sin_t, first8):
    parts = []
    for c in range(y.shape[1] // LANES):
        yc = y[:, c * LANES:(c + 1) * LANES]
        partner = jnp.where(first8, pltpu.roll(yc, LANES - 8, 1), pltpu.roll(yc, 8, 1))
        parts.append(yc * cos_t + partner * sin_t)
    return parts[0] if len(parts) == 1 else jnp.concatenate(parts, axis=1)


def _pool_window_sums(u_hist):
    def back(a, k):
        return pltpu.roll(a, k, 0)

    lane = lax.broadcasted_iota(jnp.int32, (1, LANES), 1)
    s2 = u_hist + back(u_hist, 1)
    t = s2[:, LANES:]
    s4 = t + back(t, 2)
    t = s4[:, 2 * LANES:]
    s8 = t + back(t, 4)
    t = s8[:, LANES:]
    s16 = t + back(t, 8)
    tiles = [
        s2[:, :LANES],
        jnp.where(lane < 64, s2[:, LANES:2 * LANES], s4[:, :LANES]),
        s4[:, LANES:2 * LANES],
        s8[:, :LANES],
        jnp.where(lane < 64, s8[:, LANES:2 * LANES], s16[:, :LANES]),
        s16[:, LANES:],
    ]
    return jnp.concatenate(tiles, axis=1)


def _pool_window_sums_planes(planes):
    n = len(planes)

    def doubled(prev, k, first):
        return [prev[j] + prev[j - k] if j >= first else None for j in range(n)]

    lane = lax.broadcasted_iota(jnp.int32, (1, LANES), 1)
    s2 = doubled(planes, 1, 1)
    s4 = doubled([None if a is None else a[:, LANES:] for a in s2], 2, 3)
    s8 = doubled([None if a is None else a[:, 2 * LANES:] for a in s4], 4, 7)
    s16 = doubled([None if a is None else a[:, LANES:] for a in s8], 8, POOL_PAD)
    out = []
    for j in range(POOL_PAD, n):
        out.append(jnp.concatenate([
            s2[j][:, :LANES],
            jnp.where(lane < 64, s2[j][:, LANES:2 * LANES], s4[j][:, :LANES]),
            s4[j][:, LANES:2 * LANES],
            s8[j][:, :LANES],
            jnp.where(lane < 64, s8[j][:, LANES:2 * LANES], s16[j][:, :LANES]),
            s16[j][:, LANES:],
        ], axis=1))
    return out


def _pool_mix(d, wbd_ref):
    lo, hi = MXU_DIM, 2 * MXU_DIM
    return jnp.concatenate([
        _dot(d[:, :hi], wbd_ref[:hi, :lo]),
        _dot(d, wbd_ref[:, lo:hi]),
        _dot(d[:, lo:], wbd_ref[lo:, hi:]),
    ], axis=1)


def _head_masks():
    lane = lax.broadcasted_iota(jnp.int32, (1, MXU_DIM), 1)
    return [(lane // HEAD_DIM) == j for j in range(HEADS_PER_CHUNK)]


def _first8_mask():
    lane = lax.broadcasted_iota(jnp.int32, (1, LANES), 1)
    return (lane % HEAD_DIM) < (ROT_DIM // 2)


def _stack_heads(q, hmask):
    return jnp.concatenate([jnp.where(m, q, 0.0) for m in hmask], axis=0).astype(BF16)


def _unstack_heads(o, hmask):
    m = o.shape[0] // len(hmask)
    acc = None
    for h, mask in enumerate(hmask):
        oh = jnp.where(mask, o[h * m:(h + 1) * m, :], 0.0)
        acc = oh if acc is None else acc + oh
    return acc


def _sink_columns(sinks_ref, rows_per_head):
    hrow = lax.broadcasted_iota(jnp.int32, (HEADS_PER_CHUNK * rows_per_head, 1), 0) // rows_per_head
    cols = []
    for c in range(SWA_W // MXU_DIM):
        col = jnp.zeros((HEADS_PER_CHUNK * rows_per_head, 1), F32)
        for jj in range(HEADS_PER_CHUNK):
            col = jnp.where(hrow == jj, sinks_ref[c * HEADS_PER_CHUNK + jj], col)
        cols.append(col)
    return cols


def _softmax_weights(s, sink):
    m = jnp.max(s, axis=-1, keepdims=True)
    if sink is not None:
        m = jnp.maximum(m, sink)
    p = jnp.exp2(s - m)
    den = jnp.sum(p, axis=-1, keepdims=True)
    if sink is not None:
        den = den + jnp.exp2(sink - m)
    return p.astype(BF16), 1.0 / den


def _mem_attn(q, k_t, v_t, hmask):
    p, rden = _softmax_weights(_dot(_stack_heads(q, hmask), k_t), None)
    return _unstack_heads(_dot_nt(p, v_t) * rden, hmask)


def _memkv_kernel(mem_ref, norm_ref, wk_ref, wv_ref, kgain_ref, mk_ref, mv_ref, mkb_ref, mvb_ref):
    xn = (_rms_unit(mem_ref[...]) * norm_ref[0]).astype(BF16)
    k_t = _dot_nt(wk_ref[0], xn)
    v_t = _dot_nt(wv_ref[0], xn)
    k3 = k_t.reshape(MEM_HEADS, HEAD_DIM, N_MEM)
    ms = jnp.mean(k3 * k3, axis=1, keepdims=True)
    k_t = (k3 * lax.rsqrt(ms + EPS)).reshape(MEM_W, N_MEM) * kgain_ref[0]
    mk_ref[0] = k_t
    mv_ref[0] = v_t
    mkb_ref[0] = k_t.astype(BF16)
    mvb_ref[0] = v_t.astype(BF16)


def _memkv_call(mem, mem_norm, wk_t, wv_t, kgain_col):
    const2 = lambda l: (0, 0)
    per_layer3 = lambda l: (l, 0, 0)
    out_f = jax.ShapeDtypeStruct((DEPTH, MEM_W, N_MEM), F32)
    out_b = jax.ShapeDtypeStruct((DEPTH, MEM_W, N_MEM), BF16)
    return pl.pallas_call(
        _memkv_kernel,
        grid=(DEPTH,),
        in_specs=[
            pl.BlockSpec((N_MEM, D_MODEL), const2),
            pl.BlockSpec((1, 1, D_MODEL), per_layer3),
            pl.BlockSpec((1, MEM_W, D_MODEL), per_layer3),
            pl.BlockSpec((1, MEM_W, D_MODEL), per_layer3),
            pl.BlockSpec((1, MEM_W, 1), per_layer3),
        ],
        out_specs=[pl.BlockSpec((1, MEM_W, N_MEM), per_layer3)] * 4,
        out_shape=[out_f, out_f, out_b, out_b],
        compiler_params=pltpu.CompilerParams(dimension_semantics=("arbitrary",)),
        name="memkv",
    )(mem, mem_norm, wk_t, wv_t, kgain_col)


def _swa_block(qb, kc, vc, bias, sink_cols, hmask):
    outs = []
    for g in range(GROUP):
        s = _dot_nt(_stack_heads(qb[:, g * MXU_DIM:(g + 1) * MXU_DIM], hmask), kc)
        s = (s.reshape(HEADS_PER_CHUNK, WINDOW, 2 * WINDOW) + bias[None]).reshape(HEADS_PER_CHUNK * WINDOW, 2 * WINDOW)
        p, rden = _softmax_weights(s, sink_cols[g])
        outs.append(_unstack_heads(_dot(p, vc) * rden, hmask))
    return jnp.concatenate(outs, axis=1)


def _const_spec(shape):
    nd = len(shape)
    return pl.BlockSpec(shape, lambda i: (0,) * nd, pipeline_mode=pl.Buffered(1))


def _stack_heads_seq(q, n_seq, hmask):
    q3 = q.reshape(n_seq, DEC_SEQ, MXU_DIM)
    return jnp.stack([jnp.where(m, q3, 0.0) for m in hmask], axis=1)


def _unstack_heads_seq(o, n_seq, hmask):
    o4 = o.reshape(n_seq, HEADS_PER_CHUNK, DEC_SEQ, MXU_DIM)
    acc = None
    for h, mask in enumerate(hmask):
        oh = jnp.where(mask, o4[:, h], 0.0)
        acc = oh if acc is None else acc + oh
    return acc.reshape(n_seq * DEC_SEQ, MXU_DIM)


def _mem_attn_seqs(q, k_ref, v_ref, layer, n_seq, hmask):
    rows = HEADS_PER_CHUNK * DEC_SEQ
    qs = _stack_heads_seq(q, n_seq, hmask).reshape(n_seq * rows, MXU_DIM).astype(BF16)
    s = jnp.concatenate([_dot(qs[b * rows:(b + 1) * rows], k_ref[layer, b].astype(BF16))
                         for b in range(n_seq)], axis=0)
    p, rden = _softmax_weights(s, None)
    o = jnp.concatenate([_dot_nt(p[b * rows:(b + 1) * rows], v_ref[layer, b].astype(BF16))
                         for b in range(n_seq)], axis=0)
    return _unstack_heads_seq(o * rden, n_seq, hmask)


def _sample_pool(u_s, pref_ref, pool_ref, wcol, parity, u_scr, d_scr):
    sb = STEP_SEQS
    n_lt = POOL_W // LANES
    for c in range(n_lt):
        u_scr[c] = u_s[:, c * LANES:(c + 1) * LANES]
    planes = [jnp.where(parity == 0, pref_ref[t, 0:sb, :], pref_ref[t, sb:2 * sb, :]) for t in range(POOL_PAD)]
    planes += [jnp.concatenate([u_scr[c, pl.ds(t, sb, stride=DEC_SEQ), :] for c in range(n_lt)], axis=1)
               for t in range(DEC_SEQ)]
    for t, win in enumerate(_pool_window_sums_planes(planes)):
        cnt = jnp.minimum(float(PAST_LEN + t + 1), wcol)
        d_t = win / cnt - planes[POOL_PAD + t]
        for c in range(n_lt):
            d_scr[c, pl.ds(t, sb, stride=DEC_SEQ), :] = d_t[:, c * LANES:(c + 1) * LANES]

    @pl.when(parity == 0)
    def _():
        for t in range(POOL_PAD):
            pool_ref[t, 0:sb, :] = planes[DEC_SEQ + t]

    @pl.when(parity == 1)
    def _():
        for t in range(POOL_PAD):
            pool_ref[t, sb:2 * sb, :] = planes[DEC_SEQ + t]

    return jnp.concatenate([d_scr[c] for c in range(n_lt)], axis=1)


def _sample_cache_update(k_s, v_s, ck_ref, cv_ref, ko_ref, vo_ref):
    sb = STEP_SEQS
    keep = WINDOW - DEC_SEQ
    key_lane = lax.broadcasted_iota(jnp.int32, (1, WINDOW), 1)
    pad_rows = jnp.zeros((WINDOW - sb * DEC_SEQ, KV_W), F32)
    for new_rows, c_ref, o_ref in ((k_s, ck_ref, ko_ref), (v_s, cv_ref, vo_ref)):
        new_t = jnp.concatenate([new_rows, pad_rows], axis=0).T
        for b in range(sb):
            shifted = pltpu.roll(c_ref[b], keep, 1)
            placed = pltpu.roll(new_t, (keep - b * DEC_SEQ) % WINDOW, 1)
            o_ref[b] = jnp.where(key_lane < keep, shifted, placed)


def _sample_window_attn(q_s, k_s, v_s, ck_ref, cv_ref, sinks_ref, hmask):
    sb = STEP_SEQS
    m = sb * DEC_SEQ
    rows = N_Q_HEADS * DEC_SEQ
    qs = jnp.stack([_stack_heads_seq(q_s[:, g * MXU_DIM:(g + 1) * MXU_DIM], sb, hmask) for g in range(GROUP)],
                   axis=1)
    qs = qs.reshape(sb * rows, KV_W).astype(BF16)
    s_old = jnp.concatenate([_dot(qs[b * rows:(b + 1) * rows], ck_ref[b].astype(BF16)) for b in range(sb)], axis=0)
    s_new = _dot_nt(qs, k_s.astype(BF16))
    tq_old = lax.broadcasted_iota(jnp.int32, (rows, WINDOW), 0) % DEC_SEQ
    key_old = lax.broadcasted_iota(jnp.int32, (rows, WINDOW), 1)
    bias_old = jnp.where(key_old > tq_old, 0.0, NEG_INF)
    row_i = lax.broadcasted_iota(jnp.int32, (sb * rows, m), 0)
    col_i = lax.broadcasted_iota(jnp.int32, (sb * rows, m), 1)
    same_seq = (row_i // rows) == (col_i // DEC_SEQ)
    bias_new = jnp.where(same_seq, jnp.where(col_i % DEC_SEQ <= row_i % DEC_SEQ, 0.0, NEG_INF), NEG_INF)
    s_old = (s_old.reshape(sb, rows, WINDOW) + bias_old[None]).reshape(sb * rows, WINDOW)
    s_new = s_new + bias_new
    sink = jnp.concatenate(_sink_columns(sinks_ref, DEC_SEQ), axis=0)
    sink = jnp.concatenate([sink] * sb, axis=0)
    mx = jnp.maximum(jnp.maximum(jnp.max(s_old, axis=-1, keepdims=True), jnp.max(s_new, axis=-1, keepdims=True)), sink)
    p_old = jnp.exp2(s_old - mx)
    p_new = jnp.exp2(s_new - mx)
    den = (jnp.sum(p_old, axis=-1, keepdims=True) + jnp.sum(p_new, axis=-1, keepdims=True) + jnp.exp2(sink - mx))
    p_old = p_old.astype(BF16)
    o = jnp.concatenate([_dot_nt(p_old[b * rows:(b + 1) * rows], cv_ref[b].astype(BF16)) for b in range(sb)], axis=0)
    o = (o + _dot(p_new.astype(BF16), v_s.astype(BF16))) * (1.0 / den)
    o5 = o.reshape(sb, GROUP, HEADS_PER_CHUNK * DEC_SEQ, KV_W)
    return jnp.concatenate([_unstack_heads_seq(o5[:, g].reshape(sb * HEADS_PER_CHUNK * DEC_SEQ, KV_W), sb, hmask)
                            for g in range(GROUP)], axis=1)


def _step_kernel(sinks_ref, xp_ref, xs_ref, pref_ref, ck_ref, cv_ref, cmk_ref, cmv_ref,
                 rope_base_ref, rope_res_ref, rope_s_ref,
                 vec_ref, w_in_a_ref, wbd_ref, wcol_ref, w_out_a_ref, w_kv_ref,
                 wq_ref, wg_ref, wm_ref, wos_ref, wom_ref,
                 mk_ref, mv_ref, bseg_ref,
                 yp_ref, pool_p_ref, kop_ref, vop_ref, ys_ref, pool_s_ref, kos_ref, vos_ref,
                 ucarry, kprev, vprev, u_scr, d_scr):
    tq = PROMPT_TILE
    sb = STEP_SEQS
    m = sb * DEC_SEQ
    i = pl.program_id(0)

    @pl.when(i == 0)
    def _():
        ucarry[0] = jnp.zeros((CARRY_ROWS, POOL_W), F32)
        kprev[0] = jnp.zeros((WINDOW, KV_W), BF16)
        vprev[0] = jnp.zeros((WINDOW, KV_W), BF16)

    rd = i % 2
    wr = (i + 1) % 2

    hmask = _head_masks()
    first8 = _first8_mask()
    bseg = bseg_ref[...]
    x = jnp.concatenate([xp_ref[...], xs_ref[...].reshape(m, D_MODEL)], axis=0)

    base = rope_base_ref[i]
    cb, sb_ = base[0:1, :], base[1:2, :]
    cr, sr = rope_res_ref[0], rope_res_ref[1]
    cos_t = jnp.concatenate([cb * cr - sb_ * sr, rope_s_ref[0]], axis=0)
    sin_t = jnp.concatenate([sb_ * cr + cb * sr, rope_s_ref[1]], axis=0)

    xn = (_rms_unit(x) * _vec(vec_ref, "norm_a")).astype(BF16)
    u = _dot(xn, w_in_a_ref[:, 0:POOL_W])
    gp = _dot(xn, w_in_a_ref[:, POOL_W:2 * POOL_W])
    qm = _dot(xn, w_in_a_ref[:, 2 * POOL_W:2 * POOL_W + MEM_W])
    gm = _dot(xn, w_in_a_ref[:, 2 * POOL_W + MEM_W:])
    u_p = u[:tq]
    u_hist = jnp.concatenate([ucarry[rd], u_p], axis=0)
    win = _pool_window_sums(u_hist)[CARRY_ROWS:]
    pos = i * tq + lax.broadcasted_iota(jnp.int32, (tq, 1), 0)
    cnt = jnp.minimum((pos + 1).astype(F32), wcol_ref[...])
    d_p = win / cnt - u_p
    d_s = _sample_pool(u[tq:], pref_ref, pool_s_ref, wcol_ref[...], rd, u_scr, d_scr)
    d = jnp.concatenate([d_p, d_s], axis=0)
    yp = _pool_mix(d.astype(BF16), wbd_ref) * _vec(vec_ref, "pool_scale") * _silu(gp)
    qmn = _head_rms(qm, _vec(vec_ref, "mem_q_gain0"), bseg) * Q_SCALE
    ym = jnp.concatenate([_mem_attn(qmn[:tq], mk_ref[0], mv_ref[0], hmask),
                          _mem_attn_seqs(qmn[tq:], cmk_ref, cmv_ref, 0, sb, hmask)], axis=0) * _silu(gm)
    x1 = x + _dot(jnp.concatenate([yp, ym], axis=1).astype(BF16), w_out_a_ref[...])
    ucarry[wr] = u_p[tq - CARRY_ROWS:, :]

    r = _rms_unit(x1)
    kv = _dot((r * _vec(vec_ref, "kv_norm")).astype(BF16), w_kv_ref[...])
    k = _rope(_head_rms(kv[:, :KV_W], _vec(vec_ref, "k_gain"), bseg), cos_t, sin_t, first8)
    v = kv[:, KV_W:]
    k_p, v_p, k_s, v_s = k[:tq], v[:tq], k[tq:], v[tq:]
    k_all = jnp.concatenate([kprev[rd], k_p.astype(BF16)], axis=0)
    v_all = jnp.concatenate([vprev[rd], v_p.astype(BF16)], axis=0)
    _sample_cache_update(k_s, v_s, ck_ref, cv_ref, kos_ref, vos_ref)

    xb = (r * _vec(vec_ref, "norm_b")).astype(BF16)
    zq = _dot(xb, wq_ref[...])
    gq = _dot(xb, wg_ref[...])
    qm2 = _dot(xb, wm_ref[:, 0:MEM_W])
    gm2 = _dot(xb, wm_ref[:, MEM_W:])
    q = _rope(_head_rms(zq, _vec(vec_ref, "q_gain"), bseg), cos_t, sin_t, first8) * Q_SCALE

    qi = lax.broadcasted_iota(jnp.int32, (WINDOW, 2 * WINDOW), 0)
    ci = lax.broadcasted_iota(jnp.int32, (WINDOW, 2 * WINDOW), 1)
    band_bias = jnp.where(ci > qi, jnp.where(ci <= qi + WINDOW, 0.0, NEG_INF), NEG_INF)
    sink_cols = _sink_columns(sinks_ref, WINDOW)
    ys_blocks = []
    for b in range(tq // WINDOW):
        bias = band_bias
        if b == 0:
            key_pos = ci + (i * tq - WINDOW)
            bias = jnp.where(key_pos >= 0, band_bias, NEG_INF)
        qb = q[b * WINDOW:(b + 1) * WINDOW, :]
        kc = k_all[b * WINDOW:(b + 2) * WINDOW, :]
        vc = v_all[b * WINDOW:(b + 2) * WINDOW, :]
        ys_blocks.append(_swa_block(qb, kc, vc, bias, sink_cols, hmask))
    ys_blocks.append(_sample_window_attn(q[tq:], k_s, v_s, ck_ref, cv_ref, sinks_ref, hmask))
    ys = jnp.concatenate(ys_blocks, axis=0) * _silu(gq)
    qmn2 = _head_rms(qm2, _vec(vec_ref, "mem_q_gain1"), bseg) * Q_SCALE
    ym2 = jnp.concatenate([_mem_attn(qmn2[:tq], mk_ref[1], mv_ref[1], hmask),
                           _mem_attn_seqs(qmn2[tq:], cmk_ref, cmv_ref, 1, sb, hmask)], axis=0) * _silu(gm2)
    y = x1 + _dot(ys.astype(BF16), wos_ref[...]) + _dot(ym2.astype(BF16), wom_ref[...])
    yp_ref[...] = y[:tq]
    ys_ref[...] = y[tq:].reshape(sb, DEC_SEQ, D_MODEL)

    kprev[wr] = k_all[tq:, :]
    vprev[wr] = v_all[tq:, :]

    @pl.when(i == pl.num_programs(0) - 1)
    def _():
        kop_ref[...] = k_p[tq - WINDOW:, :].T
        vop_ref[...] = v_p[tq - WINDOW:, :].T
        pool_p_ref[...] = pltpu.roll(u_p[tq - CARRY_ROWS:, :], CARRY_ROWS - 1, 0)[0:POOL_PAD, :]


def _step_call(sinks, x_p, x_s, pref, ck, cv, cmk_t, cmv_t, rope_base, rope_res, rope_s, wts, mk, mv, bseg):
    tq = PROMPT_TILE
    sb = STEP_SEQS
    n = x_p.shape[0]
    nb = x_s.shape[0]
    assert n // tq == nb // sb
    row_spec = lambda w: pl.BlockSpec((tq, w), lambda i: (i, 0))
    seq3 = lambda a, b_: pl.BlockSpec((sb, a, b_), lambda i: (i, 0, 0))
    pool_s_spec = pl.BlockSpec((POOL_PAD, 2 * sb, POOL_W), lambda i: (0, i // 2, 0))
    cache_mem_spec = pl.BlockSpec((DEPTH, sb, MEM_W, N_MEM), lambda i: (0, i, 0, 0))
    in_specs = [pl.BlockSpec(memory_space=pltpu.SMEM), row_spec(D_MODEL),
                seq3(DEC_SEQ, D_MODEL), pool_s_spec, seq3(KV_W, WINDOW), seq3(KV_W, WINDOW),
                cache_mem_spec, cache_mem_spec,
                _const_spec(rope_base.shape), _const_spec(rope_res.shape), _const_spec(rope_s.shape)]
    in_specs += [_const_spec(w.shape) for w in wts]
    in_specs += [_const_spec(a.shape) for a in (mk, mv, bseg)]
    out_shape = [jax.ShapeDtypeStruct((n, D_MODEL), F32),
                 jax.ShapeDtypeStruct((POOL_PAD, POOL_W), F32),
                 jax.ShapeDtypeStruct((KV_W, WINDOW), F32),
                 jax.ShapeDtypeStruct((KV_W, WINDOW), F32),
                 jax.ShapeDtypeStruct((nb, DEC_SEQ, D_MODEL), F32),
                 jax.ShapeDtypeStruct((POOL_PAD, nb, POOL_W), F32),
                 jax.ShapeDtypeStruct((nb, KV_W, WINDOW), F32),
                 jax.ShapeDtypeStruct((nb, KV_W, WINDOW), F32)]
    out_specs = [row_spec(D_MODEL),
                 pl.BlockSpec((POOL_PAD, POOL_W), lambda i: (0, 0)),
                 pl.BlockSpec((KV_W, WINDOW), lambda i: (0, 0)),
                 pl.BlockSpec((KV_W, WINDOW), lambda i: (0, 0)),
                 seq3(DEC_SEQ, D_MODEL), pool_s_spec, seq3(KV_W, WINDOW), seq3(KV_W, WINDOW)]
    return pl.pallas_call(
        _step_kernel,
        grid=(n // tq,),
        in_specs=in_specs,
        out_specs=out_specs,
        out_shape=out_shape,
        scratch_shapes=[pltpu.VMEM((2, CARRY_ROWS, POOL_W), F32),
                        pltpu.VMEM((2, WINDOW, KV_W), BF16),
                        pltpu.VMEM((2, WINDOW, KV_W), BF16),
                        pltpu.VMEM((POOL_W // LANES, sb * DEC_SEQ, LANES), F32),
                        pltpu.VMEM((POOL_W // LANES, sb * DEC_SEQ, LANES), F32)],
        compiler_params=pltpu.CompilerParams(dimension_semantics=("arbitrary",),
                                             vmem_limit_bytes=STEP_VMEM_LIMIT),
        name="step",
    )(sinks, x_p, x_s, pref, ck, cv, cmk_t, cmv_t, rope_base, rope_res, rope_s, *wts, mk, mv, bseg)


PREP_ROWS = N_KV_HEADS * HEAD_DIM
PREP_VMEM_LIMIT = 40 * 1024 * 1024


def _cols_group_major(w):
    heads = [w[:, (kvh * GROUP + g) * HEAD_DIM:(kvh * GROUP + g + 1) * HEAD_DIM]
             for g in range(GROUP) for kvh in range(N_KV_HEADS)]
    return jnp.concatenate(heads, axis=1)


def _prep_kernel(w_in_a_ref, w_out_a_ref, w_kv_ref, w_in_b_ref, wob_h0_ref, wob_h1_ref, wob_h2_ref, wob_h3_ref,
                 wob_m_ref, w_mem_ref, mix_ref,
                 in_a_ref, out_a_ref, kv_ref, wq_ref, wg_ref, wm_ref, wos_ref, wom_ref, wkt_ref, wvt_ref, wbd_ref):
    @pl.when(pl.program_id(0) == 0)
    def _():
        n_grp = len(POOL_WINDOWS)
        for g in range(n_grp):
            pieces = [jnp.zeros((POOL_GW, POOL_GW), F32)] * n_grp
            pieces[g] = mix_ref[g]
            wbd_ref[g * POOL_GW:(g + 1) * POOL_GW, :] = jnp.concatenate(pieces, axis=1).astype(BF16)

    in_a_ref[...] = w_in_a_ref[...].astype(BF16)
    out_a_ref[...] = w_out_a_ref[...].astype(BF16)
    kv_ref[...] = w_kv_ref[...].astype(BF16)
    w_in_b = w_in_b_ref[...]
    wq_ref[...] = _cols_group_major(w_in_b[:, :SWA_W]).astype(BF16)
    wg_ref[...] = _cols_group_major(w_in_b[:, SWA_W:2 * SWA_W]).astype(BF16)
    wm_ref[...] = w_in_b[:, 2 * SWA_W:].astype(BF16)
    wos_ref[...] = jnp.concatenate([wob_h0_ref[0], wob_h1_ref[0], wob_h2_ref[0], wob_h3_ref[0]],
                                   axis=0).astype(BF16)
    wom_ref[...] = wob_m_ref[...].astype(BF16)
    for l in range(DEPTH):
        w_mem = w_mem_ref[l]
        wkt_ref[l] = w_mem[:, :MEM_W].T.astype(BF16)
        wvt_ref[l] = w_mem[:, MEM_W:].T.astype(BF16)


def _prep_call(w_in_a, w_out_a, w_kv, w_in_b, w_out_b, w_mem_kv, pool_mix):
    rows = lambda w: pl.BlockSpec((PREP_ROWS, w), lambda i: (i, 0))

    def head_src(kvh):
        return pl.BlockSpec((1, HEAD_DIM, D_MODEL),
                            lambda i: (kvh * GROUP + jnp.minimum(i, GROUP - 1), 0, 0))

    in_specs = [rows(2 * POOL_W + 2 * MEM_W), rows(D_MODEL), rows(2 * KV_W), rows(2 * SWA_W + 2 * MEM_W)]
    in_specs += [head_src(kvh) for kvh in range(N_KV_HEADS)]
    in_specs += [pl.BlockSpec((PREP_ROWS, D_MODEL), lambda i: (SWA_W // PREP_ROWS, 0)),
                pl.BlockSpec((DEPTH, PREP_ROWS, 2 * MEM_W), lambda i: (0, i, 0)),
                pl.BlockSpec(pool_mix.shape, lambda i: (0, 0, 0))]
    out_specs = [rows(2 * POOL_W + 2 * MEM_W), rows(D_MODEL), rows(2 * KV_W), rows(SWA_W), rows(SWA_W),
                 rows(2 * MEM_W),
                 pl.BlockSpec((PREP_ROWS, D_MODEL), lambda i: (jnp.minimum(i, GROUP - 1), 0)),
                 pl.BlockSpec((PREP_ROWS, D_MODEL), lambda i: (0, 0)),
                 pl.BlockSpec((DEPTH, MEM_W, PREP_ROWS), lambda i: (0, 0, i)),
                 pl.BlockSpec((DEPTH, MEM_W, PREP_ROWS), lambda i: (0, 0, i)),
                 pl.BlockSpec((POOL_W, POOL_W), lambda i: (0, 0))]
    bf = lambda *shape: jax.ShapeDtypeStruct(shape, BF16)
    out_shape = [bf(D_MODEL, 2 * POOL_W + 2 * MEM_W), bf(D_MODEL, D_MODEL), bf(D_MODEL, 2 * KV_W),
                 bf(D_MODEL, SWA_W), bf(D_MODEL, SWA_W), bf(D_MODEL, 2 * MEM_W),
                 bf(SWA_W, D_MODEL), bf(MEM_W, D_MODEL),
                 bf(DEPTH, MEM_W, D_MODEL), bf(DEPTH, MEM_W, D_MODEL), bf(POOL_W, POOL_W)]
    w_out_b_heads = w_out_b.reshape(D_MODEL // HEAD_DIM, HEAD_DIM, D_MODEL)
    return pl.pallas_call(
        _prep_kernel,
        grid=(D_MODEL // PREP_ROWS,),
        in_specs=in_specs,
        out_specs=out_specs,
        out_shape=out_shape,
        compiler_params=pltpu.CompilerParams(dimension_semantics=("arbitrary",),
                                             vmem_limit_bytes=PREP_VMEM_LIMIT),
        name="prep",
    )(w_in_a, w_out_a, w_kv, w_in_b, *([w_out_b_heads] * N_KV_HEADS), w_out_b, w_mem_kv, pool_mix)


def _rope_lane_tables(pos):
    half = ROT_DIM // 2
    inv = (ROPE_THETA ** (-np.arange(half, dtype=np.float32) * 2.0 / ROT_DIM)).astype(np.float32)
    ang = (np.asarray(pos, np.float32)[:, None] * inv[None, :]).astype(np.float64)
    cos, sin = np.cos(ang), np.sin(ang)
    t = ang.shape[0]
    rest = HEAD_DIM - ROT_DIM
    cos64 = np.concatenate([cos, cos, np.ones((t, rest))], axis=1)
    sin64 = np.concatenate([-sin, sin, np.zeros((t, rest))], axis=1)
    reps = LANES // HEAD_DIM
    return np.tile(cos64, (1, reps)).astype(np.float32), np.tile(sin64, (1, reps)).astype(np.float32)


def _head_tile(g, width):
    return jnp.tile(g.astype(F32), width // HEAD_DIM)


def kernel(x_prompt, x_sample, state_pool, cache_swa_k, cache_swa_v, cache_mem_k, cache_mem_v, mem_prompt,
           norm_a, w_in_a, pool_mix_w, pool_scale, w_out_a, kv_norm, w_kv, k_norm,
           norm_b, w_in_b, q_norm, sinks, w_out_b, mem_norm, w_mem_kv, mem_q_norm, mem_k_norm):
    seg = np.arange(MXU_DIM) // HEAD_DIM
    bseg = jnp.asarray((seg[:, None] == seg[None, :]).astype(np.float32) / HEAD_DIM, BF16)
    wcol = jnp.asarray(np.repeat(np.asarray(POOL_WINDOWS, np.float32), POOL_GW).reshape(1, POOL_W))

    w_in_a_b, w_out_a_b, w_kv_b, wq, wg, wm, wos, wom, wk_t, wv_t, wbd = _prep_call(
        w_in_a[0], w_out_a[0], w_kv, w_in_b[0], w_out_b[0], w_mem_kv, pool_mix_w[0])
    sinks_p = sinks[0].astype(F32).reshape(N_KV_HEADS, GROUP).T.reshape(N_Q_HEADS) * LOG2E

    pieces = {"norm_a": norm_a[0], "pool_scale": pool_scale[0], "kv_norm": kv_norm,
              "k_gain": _head_tile(k_norm, KV_W), "norm_b": norm_b[0], "q_gain": _head_tile(q_norm[0], SWA_W),
              "mem_q_gain0": _head_tile(mem_q_norm[0], MEM_W), "mem_q_gain1": _head_tile(mem_q_norm[1], MEM_W)}
    vecs = jnp.concatenate([pieces[name].astype(F32) for name, _ in _VEC_WIDTHS]).reshape(1, -1)
    wts = (vecs, w_in_a_b, wbd, wcol, w_out_a_b, w_kv_b, wq, wg, wm, wos, wom)
    mkg_col = jnp.stack([jnp.tile(mem_k_norm[l].astype(F32), MEM_HEADS).reshape(MEM_W, 1) for l in range(DEPTH)])

    mk_t, mv_t, mk_b, mv_b = _memkv_call(mem_prompt[0], mem_norm.reshape(DEPTH, 1, D_MODEL), wk_t, wv_t, mkg_col)

    n_tiles = SEQ // PROMPT_TILE
    cb, sb_ = _rope_lane_tables(np.arange(n_tiles) * PROMPT_TILE)
    cr, sr = _rope_lane_tables(np.arange(PROMPT_TILE))
    rope_base = jnp.asarray(np.stack([cb, sb_], axis=1))
    rope_res = jnp.asarray(np.stack([cr, sr], axis=0))
    cos_s, sin_s = _rope_lane_tables(PAST_LEN + np.arange(DEC_SEQ))
    rope_s = jnp.asarray(np.stack([np.tile(cos_s, (STEP_SEQS, 1)), np.tile(sin_s, (STEP_SEQS, 1))], axis=0))
    pref = jnp.transpose(state_pool[0], (1, 0, 2))
    cmk_t = jnp.transpose(cache_mem_k, (0, 1, 3, 4, 2)).reshape(DEPTH, DEC_BATCH, MEM_W, N_MEM)
    cmv_t = jnp.transpose(cache_mem_v, (0, 1, 3, 4, 2)).reshape(DEPTH, DEC_BATCH, MEM_W, N_MEM)
    ck_t = jnp.transpose(cache_swa_k, (0, 2, 3, 1)).reshape(DEC_BATCH, KV_W, WINDOW)
    cv_t = jnp.transpose(cache_swa_v, (0, 2, 3, 1)).reshape(DEC_BATCH, KV_W, WINDOW)
    y_p, pool_p, k_p, v_p, y_s, pool_s, k_s, v_s = _step_call(
        sinks_p, x_prompt[0], x_sample, pref, ck_t, cv_t, cmk_t, cmv_t, rope_base, rope_res, rope_s,
        wts, mk_b, mv_b, bseg)

    def mem_out(a):
        return jnp.transpose(a.reshape(DEPTH, 1, MEM_HEADS, HEAD_DIM, N_MEM), (0, 1, 4, 2, 3))

    def swa_out_t(a):
        return jnp.transpose(a.reshape(-1, N_KV_HEADS, HEAD_DIM, WINDOW), (0, 3, 1, 2))

    return (y_p[None], y_s, pool_p[None, None], jnp.transpose(pool_s, (1, 0, 2))[None],
            swa_out_t(k_p[None]), swa_out_t(v_p[None]), swa_out_t(k_s), swa_out_t(v_s),
            mem_out(mk_t), mem_out(mv_t))
```

```python
import jax
import jax.numpy as jnp
import numpy as np
from jax import lax
from jax.experimental import pallas as pl
from jax.experimental.pallas import tpu as pltpu

D_MODEL = 1024
SEQ = 16384
DEPTH = 2
DEC_BATCH = 128
DEC_SEQ = 8
PAST_LEN = 16384
HEAD_DIM = 64
POOL_W = 768
POOL_WINDOWS = (2, 4, 8, 16)
POOL_GW = 192
POOL_PAD = 15
N_Q_HEADS = 12
N_KV_HEADS = 4
GROUP = 3
SWA_W = 768
KV_W = 256
WINDOW = 128
N_MEM = 256
MEM_HEADS = 4
MEM_W = 256
ROT_DIM = 16
ROPE_THETA = 500000.0
EPS = 1e-6

F32 = jnp.float32
BF16 = jnp.bfloat16
NEG_INF = float("-inf")
LOG2E = 1.4426950408889634
Q_SCALE = HEAD_DIM ** -0.5 * LOG2E

LANES = 128
SUBLANES = 8
MXU_DIM = 256
HEADS_PER_CHUNK = MXU_DIM // HEAD_DIM
CARRY_ROWS = 16
PROMPT_TILE = 512
STEP_SEQS = DEC_BATCH // (SEQ // PROMPT_TILE)
STEP_VMEM_LIMIT = 60 * 1024 * 1024


_VEC_WIDTHS = (("norm_a", D_MODEL), ("pool_scale", POOL_W), ("kv_norm", D_MODEL), ("k_gain", KV_W),
               ("norm_b", D_MODEL), ("q_gain", SWA_W), ("mem_q_gain0", MEM_W), ("mem_q_gain1", MEM_W))
_VEC_OFFSET = {}
for _name, _width in _VEC_WIDTHS:
    _VEC_OFFSET[_name] = (sum(w for _, w in _VEC_WIDTHS[:len(_VEC_OFFSET)]), _width)


def _vec(vec_ref, name):
    start, width = _VEC_OFFSET[name]
    return vec_ref[:, start:start + width]


def _dot(a, b):
    return jnp.dot(a, b, preferred_element_type=F32)


def _dot_nt(a, b):
    return lax.dot_general(a, b, (((1,), (1,)), ((), ())), preferred_element_type=F32)


def _rms_unit(x):
    return x * lax.rsqrt(jnp.mean(x * x, axis=-1, keepdims=True) + EPS)


def _silu(g):
    return g / (1.0 + jnp.exp(-g))


def _head_rms(y, gain, bseg):
    parts = []
    for c in range(y.shape[1] // MXU_DIM):
        yc = y[:, c * MXU_DIM:(c + 1) * MXU_DIM]
        ms = _dot((yc * yc).astype(BF16), bseg)
        parts.append(yc * lax.rsqrt(ms + EPS) * gain[:, c * MXU_DIM:(c + 1) * MXU_DIM])
    return parts[0] if len(parts) == 1 else jnp.concatenate(parts, axis=1)


def _rope(y, cos_t, sin_t, first8):
    parts = []
    for c in range(y.shape[1] // LANES):
        yc = y[:, c * LANES:(c + 1) * LANES]
        partner = jnp.where(first8, pltpu.roll(yc, LANES - 8, 1), pltpu.roll(yc, 8, 1))
        parts.append(yc * cos_t + partner * sin_t)
    return parts[0] if len(parts) == 1 else jnp.concatenate(parts, axis=1)


def _pool_window_sums(u_hist):
    def back(a, k):
        return pltpu.roll(a, k, 0)

    lane = lax.broadcasted_iota(jnp.int32, (1, LANES), 1)
    s2 = u_hist + back(u_hist, 1)
    t = s2[:, LANES:]
    s4 = t + back(t, 2)
    t = s4[:, 2 * LANES:]
    s8 = t + back(t, 4)
    t = s8[:, LANES:]
    s16 = t + back(t, 8)
    tiles = [
        s2[:, :LANES],
        jnp.where(lane < 64, s2[:, LANES:2 * LANES], s4[:, :LANES]),
        s4[:, LANES:2 * LANES],
        s8[:, :LANES],
        jnp.where(lane < 64, s8[:, LANES:2 * LANES], s16[:, :LANES]),
        s16[:, LANES:],
    ]
    return jnp.concatenate(tiles, axis=1)


def _pool_window_sums_planes(planes):
    n = len(planes)

    def doubled(prev, k, first):
        return [prev[j] + prev[j - k] if j >= first else None for j in range(n)]

    lane = lax.broadcasted_iota(jnp.int32, (1, LANES), 1)
    s2 = doubled(planes, 1, 1)
    s4 = doubled([None if a is None else a[:, LANES:] for a in s2], 2, 3)
    s8 = doubled([None if a is None else a[:, 2 * LANES:] for a in s4], 4, 7)
    s16 = doubled([None if a is None else a[:, LANES:] for a in s8], 8, POOL_PAD)
    out = []
    for j in range(POOL_PAD, n):
        out.append(jnp.concatenate([
            s2[j][:, :LANES],
            jnp.where(lane < 64, s2[j][:, LANES:2 * LANES], s4[j][:, :LANES]),
            s4[j][:, LANES:2 * LANES],
            s8[j][:, :LANES],
            jnp.where(lane < 64, s8[j][:, LANES:2 * LANES], s16[j][:, :LANES]),
            s16[j][:, LANES:],
        ], axis=1))
    return out


def _pool_mix(d, wbd_ref):
    lo, hi = MXU_DIM, 2 * MXU_DIM
    return jnp.concatenate([
        _dot(d[:, :hi], wbd_ref[:hi, :lo]),
        _dot(d, wbd_ref[:, lo:hi]),
        _dot(d[:, lo:], wbd_ref[lo:, hi:]),
    ], axis=1)


def _head_masks():
    lane = lax.broadcasted_iota(jnp.int32, (1, MXU_DIM), 1)
    return [(lane // HEAD_DIM) == j for j in range(HEADS_PER_CHUNK)]


def _first8_mask():
    lane = lax.broadcasted_iota(jnp.int32, (1, LANES), 1)
    return (lane % HEAD_DIM) < (ROT_DIM // 2)


def _stack_heads(q, hmask):
    return jnp.concatenate([jnp.where(m, q, 0.0) for m in hmask], axis=0).astype(BF16)


def _unstack_heads(o, hmask):
    m = o.shape[0] // len(hmask)
    acc = None
    for h, mask in enumerate(hmask):
        oh = jnp.where(mask, o[h * m:(h + 1) * m, :], 0.0)
        acc = oh if acc is None else acc + oh
    return acc


def _sink_columns(sinks_ref, rows_per_head):
    hrow = lax.broadcasted_iota(jnp.int32, (HEADS_PER_CHUNK * rows_per_head, 1), 0) // rows_per_head
    cols = []
    for c in range(SWA_W // MXU_DIM):
        col = jnp.zeros((HEADS_PER_CHUNK * rows_per_head, 1), F32)
        for jj in range(HEADS_PER_CHUNK):
            col = jnp.where(hrow == jj, sinks_ref[c * HEADS_PER_CHUNK + jj], col)
        cols.append(col)
    return cols


def _softmax_weights(s, sink):
    m = jnp.max(s, axis=-1, keepdims=True)
    if sink is not None:
        m = jnp.maximum(m, sink)
    p = jnp.exp2(s - m)
    den = jnp.sum(p, axis=-1, keepdims=True)
    if sink is not None:
        den = den + jnp.exp2(sink - m)
    return p.astype(BF16), 1.0 / den


def _mem_attn(q, k_t, v_t, hmask):
    p, rden = _softmax_weights(_dot(_stack_heads(q, hmask), k_t), None)
    return _unstack_heads(_dot_nt(p, v_t) * rden, hmask)


def _memkv_kernel(mem_ref, norm_ref, wk_ref, wv_ref, kgain_ref, mk_ref, mv_ref, mkb_ref, mvb_ref):
    xn = (_rms_unit(mem_ref[...]) * norm_ref[0]).astype(BF16)
    k_t = _dot_nt(wk_ref[0], xn)
    v_t = _dot_nt(wv_ref[0], xn)
    k3 = k_t.reshape(MEM_HEADS, HEAD_DIM, N_MEM)
    ms = jnp.mean(k3 * k3, axis=1, keepdims=True)
    k_t = (k3 * lax.rsqrt(ms + EPS)).reshape(MEM_W, N_MEM) * kgain_ref[0]
    mk_ref[0] = k_t
    mv_ref[0] = v_t
    mkb_ref[0] = k_t.astype(BF16)
    mvb_ref[0] = v_t.astype(BF16)


def _memkv_call(mem, mem_norm, wk_t, wv_t, kgain_col):
    const2 = lambda l: (0, 0)
    per_layer3 = lambda l: (l, 0, 0)
    out_f = jax.ShapeDtypeStruct((DEPTH, MEM_W, N_MEM), F32)
    out_b = jax.ShapeDtypeStruct((DEPTH, MEM_W, N_MEM), BF16)
    return pl.pallas_call(
        _memkv_kernel,
        grid=(DEPTH,),
        in_specs=[
            pl.BlockSpec((N_MEM, D_MODEL), const2),
            pl.BlockSpec((1, 1, D_MODEL), per_layer3),
            pl.BlockSpec((1, MEM_W, D_MODEL), per_layer3),
            pl.BlockSpec((1, MEM_W, D_MODEL), per_layer3),
            pl.BlockSpec((1, MEM_W, 1), per_layer3),
        ],
        out_specs=[pl.BlockSpec((1, MEM_W, N_MEM), per_layer3)] * 4,
        out_shape=[out_f, out_f, out_b, out_b],
        compiler_params=pltpu.CompilerParams(dimension_semantics=("arbitrary",)),
        name="memkv",
    )(mem, mem_norm, wk_t, wv_t, kgain_col)


def _swa_block(qb, kc, vc, bias, sink_cols, hmask):
    outs = []
    for g in range(GROUP):
        s = _dot_nt(_stack_heads(qb[:, g * MXU_DIM:(g + 1) * MXU_DIM], hmask), kc)
        s = (s.reshape(HEADS_PER_CHUNK, WINDOW, 2 * WINDOW) + bias[None]).reshape(HEADS_PER_CHUNK * WINDOW, 2 * WINDOW)
        p, rden = _softmax_weights(s, sink_cols[g])
        outs.append(_unstack_heads(_dot(p, vc) * rden, hmask))
    return jnp.concatenate(outs, axis=1)


def _const_spec(shape):
    del shape
    return pl.BlockSpec(memory_space=pltpu.VMEM)


def _stack_heads_seq(q, n_seq, hmask):
    q3 = q.reshape(n_seq, DEC_SEQ, MXU_DIM)
    return jnp.stack([jnp.where(m, q3, 0.0) for m in hmask], axis=1)


def _unstack_heads_seq(o, n_seq, hmask):
    o4 = o.reshape(n_seq, HEADS_PER_CHUNK, DEC_SEQ, MXU_DIM)
    acc = None
    for h, mask in enumerate(hmask):
        oh = jnp.where(mask, o4[:, h], 0.0)
        acc = oh if acc is None else acc + oh
    return acc.reshape(n_seq * DEC_SEQ, MXU_DIM)


def _mem_attn_seqs(q, k_ref, v_ref, layer, n_seq, hmask):
    rows = HEADS_PER_CHUNK * DEC_SEQ
    qs = _stack_heads_seq(q, n_seq, hmask).reshape(n_seq * rows, MXU_DIM).astype(BF16)
    s = jnp.concatenate([_dot(qs[b * rows:(b + 1) * rows], k_ref[layer, b].astype(BF16))
                         for b in range(n_seq)], axis=0)
    p, rden = _softmax_weights(s, None)
    o = jnp.concatenate([_dot_nt(p[b * rows:(b + 1) * rows], v_ref[layer, b].astype(BF16))
                         for b in range(n_seq)], axis=0)
    return _unstack_heads_seq(o * rden, n_seq, hmask)


def _sample_pool(u_s, pref_ref, pool_ref, wcol, parity, u_scr, d_scr):
    sb = STEP_SEQS
    n_lt = POOL_W // LANES
    for c in range(n_lt):
        u_scr[c] = u_s[:, c * LANES:(c + 1) * LANES]
    planes = [jnp.where(parity == 0, pref_ref[t, 0:sb, :], pref_ref[t, sb:2 * sb, :]) for t in range(POOL_PAD)]
    planes += [jnp.concatenate([u_scr[c, pl.ds(t, sb, stride=DEC_SEQ), :] for c in range(n_lt)], axis=1)
               for t in range(DEC_SEQ)]
    for t, win in enumerate(_pool_window_sums_planes(planes)):
        cnt = jnp.minimum(float(PAST_LEN + t + 1), wcol)
        d_t = win / cnt - planes[POOL_PAD + t]
        for c in range(n_lt):
            d_scr[c, pl.ds(t, sb, stride=DEC_SEQ), :] = d_t[:, c * LANES:(c + 1) * LANES]

    @pl.when(parity == 0)
    def _():
        for t in range(POOL_PAD):
            pool_ref[t, 0:sb, :] = planes[DEC_SEQ + t]

    @pl.when(parity == 1)
    def _():
        for t in range(POOL_PAD):
            pool_ref[t, sb:2 * sb, :] = planes[DEC_SEQ + t]

    return jnp.concatenate([d_scr[c] for c in range(n_lt)], axis=1)


def _sample_cache_update(k_s, v_s, ck_ref, cv_ref, ko_ref, vo_ref):
    sb = STEP_SEQS
    keep = WINDOW - DEC_SEQ
    key_lane = lax.broadcasted_iota(jnp.int32, (1, WINDOW), 1)
    pad_rows = jnp.zeros((WINDOW - sb * DEC_SEQ, KV_W), F32)
    for new_rows, c_ref, o_ref in ((k_s, ck_ref, ko_ref), (v_s, cv_ref, vo_ref)):
        new_t = jnp.concatenate([new_rows, pad_rows], axis=0).T
        for b in range(sb):
            shifted = pltpu.roll(c_ref[b], keep, 1)
            placed = pltpu.roll(new_t, (keep - b * DEC_SEQ) % WINDOW, 1)
            o_ref[b] = jnp.where(key_lane < keep, shifted, placed)


def _sample_window_attn(q_s, k_s, v_s, ck_ref, cv_ref, sinks_ref, hmask):
    sb = STEP_SEQS
    m = sb * DEC_SEQ
    rows = N_Q_HEADS * DEC_SEQ
    qs = jnp.stack([_stack_heads_seq(q_s[:, g * MXU_DIM:(g + 1) * MXU_DIM], sb, hmask) for g in range(GROUP)],
                   axis=1)
    qs = qs.reshape(sb * rows, KV_W).astype(BF16)
    s_old = jnp.concatenate([_dot(qs[b * rows:(b + 1) * rows], ck_ref[b].astype(BF16)) for b in range(sb)], axis=0)
    s_new = _dot_nt(qs, k_s.astype(BF16))
    tq_old = lax.broadcasted_iota(jnp.int32, (rows, WINDOW), 0) % DEC_SEQ
    key_old = lax.broadcasted_iota(jnp.int32, (rows, WINDOW), 1)
    bias_old = jnp.where(key_old > tq_old, 0.0, NEG_INF)
    row_i = lax.broadcasted_iota(jnp.int32, (sb * rows, m), 0)
    col_i = lax.broadcasted_iota(jnp.int32, (sb * rows, m), 1)
    same_seq = (row_i // rows) == (col_i // DEC_SEQ)
    bias_new = jnp.where(same_seq, jnp.where(col_i % DEC_SEQ <= row_i % DEC_SEQ, 0.0, NEG_INF), NEG_INF)
    s_old = (s_old.reshape(sb, rows, WINDOW) + bias_old[None]).reshape(sb * rows, WINDOW)
    s_new = s_new + bias_new
    sink = jnp.concatenate(_sink_columns(sinks_ref, DEC_SEQ), axis=0)
    sink = jnp.concatenate([sink] * sb, axis=0)
    mx = jnp.maximum(jnp.maximum(jnp.max(s_old, axis=-1, keepdims=True), jnp.max(s_new, axis=-1, keepdims=True)), sink)
    p_old = jnp.exp2(s_old - mx)
    p_new = jnp.exp2(s_new - mx)
    den = (jnp.sum(p_old, axis=-1, keepdims=True) + jnp.sum(p_new, axis=-1, keepdims=True) + jnp.exp2(sink - mx))
    p_old = p_old.astype(BF16)
    o = jnp.concatenate([_dot_nt(p_old[b * rows:(b + 1) * rows], cv_ref[b].astype(BF16)) for b in range(sb)], axis=0)
    o = (o + _dot(p_new.astype(BF16), v_s.astype(BF16))) * (1.0 / den)
    o5 = o.reshape(sb, GROUP, HEADS_PER_CHUNK * DEC_SEQ, KV_W)
    return jnp.concatenate([_unstack_heads_seq(o5[:, g].reshape(sb * HEADS_PER_CHUNK * DEC_SEQ, KV_W), sb, hmask)
                            for g in range(GROUP)], axis=1)


def _step_kernel(sinks_ref, xp_ref, xs_ref, pref_ref, ck_ref, cv_ref, cmk_ref, cmv_ref,
                 rope_base_ref, rope_res_ref, rope_s_ref,
                 vec_ref, w_in_a_ref, wbd_ref, wcol_ref, w_out_a_ref, w_kv_ref,
                 wq_ref, wg_ref, wm_ref, wos_ref, wom_ref,
                 mk_ref, mv_ref, bseg_ref,
                 yp_ref, pool_p_ref, kop_ref, vop_ref, ys_ref, pool_s_ref, kos_ref, vos_ref,
                 ucarry, kprev, vprev, u_scr, d_scr):
    tq = PROMPT_TILE
    sb = STEP_SEQS
    m = sb * DEC_SEQ
    i = pl.program_id(0)

    @pl.when(i == 0)
    def _():
        ucarry[0] = jnp.zeros((CARRY_ROWS, POOL_W), F32)
        kprev[0] = jnp.zeros((WINDOW, KV_W), BF16)
        vprev[0] = jnp.zeros((WINDOW, KV_W), BF16)

    rd = i % 2
    wr = (i + 1) % 2

    hmask = _head_masks()
    first8 = _first8_mask()
    bseg = bseg_ref[...]
    x = jnp.concatenate([xp_ref[...], xs_ref[...].reshape(m, D_MODEL)], axis=0)

    base = rope_base_ref[i]
    cb, sb_ = base[0:1, :], base[1:2, :]
    cr, sr = rope_res_ref[0], rope_res_ref[1]
    cos_t = jnp.concatenate([cb * cr - sb_ * sr, rope_s_ref[0]], axis=0)
    sin_t = jnp.concatenate([sb_ * cr + cb * sr, rope_s_ref[1]], axis=0)

    xn = (_rms_unit(x) * _vec(vec_ref, "norm_a")).astype(BF16)
    u = _dot(xn, w_in_a_ref[:, 0:POOL_W])
    gp = _dot(xn, w_in_a_ref[:, POOL_W:2 * POOL_W])
    qm = _dot(xn, w_in_a_ref[:, 2 * POOL_W:2 * POOL_W + MEM_W])
    gm = _dot(xn, w_in_a_ref[:, 2 * POOL_W + MEM_W:])
    u_p = u[:tq]
    u_hist = jnp.concatenate([ucarry[rd], u_p], axis=0)
    win = _pool_window_sums(u_hist)[CARRY_ROWS:]
    pos = i * tq + lax.broadcasted_iota(jnp.int32, (tq, 1), 0)
    cnt = jnp.minimum((pos + 1).astype(F32), wcol_ref[...])
    d_p = win / cnt - u_p
    d_s = _sample_pool(u[tq:], pref_ref, pool_s_ref, wcol_ref[...], rd, u_scr, d_scr)
    d = jnp.concatenate([d_p, d_s], axis=0)
    yp = _pool_mix(d.astype(BF16), wbd_ref) * _vec(vec_ref, "pool_scale") * _silu(gp)
    qmn = _head_rms(qm, _vec(vec_ref, "mem_q_gain0"), bseg) * Q_SCALE
    ym = jnp.concatenate([_mem_attn(qmn[:tq], mk_ref[0], mv_ref[0], hmask),
                          _mem_attn_seqs(qmn[tq:], cmk_ref, cmv_ref, 0, sb, hmask)], axis=0) * _silu(gm)
    x1 = x + _dot(jnp.concatenate([yp, ym], axis=1).astype(BF16), w_out_a_ref[...])
    ucarry[wr] = u_p[tq - CARRY_ROWS:, :]

    r = _rms_unit(x1)
    kv = _dot((r * _vec(vec_ref, "kv_norm")).astype(BF16), w_kv_ref[...])
    k = _rope(_head_rms(kv[:, :KV_W], _vec(vec_ref, "k_gain"), bseg), cos_t, sin_t, first8)
    v = kv[:, KV_W:]
    k_p, v_p, k_s, v_s = k[:tq], v[:tq], k[tq:], v[tq:]
    k_all = jnp.concatenate([kprev[rd], k_p.astype(BF16)], axis=0)
    v_all = jnp.concatenate([vprev[rd], v_p.astype(BF16)], axis=0)
    _sample_cache_update(k_s, v_s, ck_ref, cv_ref, kos_ref, vos_ref)

    xb = (r * _vec(vec_ref, "norm_b")).astype(BF16)
    zq = _dot(xb, wq_ref[...])
    gq = _dot(xb, wg_ref[...])
    qm2 = _dot(xb, wm_ref[:, 0:MEM_W])
    gm2 = _dot(xb, wm_ref[:, MEM_W:])
    q = _rope(_head_rms(zq, _vec(vec_ref, "q_gain"), bseg), cos_t, sin_t, first8) * Q_SCALE

    qi = lax.broadcasted_iota(jnp.int32, (WINDOW, 2 * WINDOW), 0)
    ci = lax.broadcasted_iota(jnp.int32, (WINDOW, 2 * WINDOW), 1)
    band_bias = jnp.where(ci > qi, jnp.where(ci <= qi + WINDOW, 0.0, NEG_INF), NEG_INF)
    sink_cols = _sink_columns(sinks_ref, WINDOW)
    ys_blocks = []
    for b in range(tq // WINDOW):
        bias = band_bias
        if b == 0:
            key_pos = ci + (i * tq - WINDOW)
            bias = jnp.where(key_pos >= 0, band_bias, NEG_INF)
        qb = q[b * WINDOW:(b + 1) * WINDOW, :]
        kc = k_all[b * WINDOW:(b + 2) * WINDOW, :]
        vc = v_all[b * WINDOW:(b + 2) * WINDOW, :]
        ys_blocks.append(_swa_block(qb, kc, vc, bias, sink_cols, hmask))
    ys_blocks.append(_sample_window_attn(q[tq:], k_s, v_s, ck_ref, cv_ref, sinks_ref, hmask))
    ys = jnp.concatenate(ys_blocks, axis=0) * _silu(gq)
    qmn2 = _head_rms(qm2, _vec(vec_ref, "mem_q_gain1"), bseg) * Q_SCALE
    ym2 = jnp.concatenate([_mem_attn(qmn2[:tq], mk_ref[1], mv_ref[1], hmask),
                           _mem_attn_seqs(qmn2[tq:], cmk_ref, cmv_ref, 1, sb, hmask)], axis=0) * _silu(gm2)
    y = x1 + _dot(ys.astype(BF16), wos_ref[...]) + _dot(ym2.astype(BF16), wom_ref[...])
    yp_ref[...] = y[:tq]
    ys_ref[...] = y[tq:].reshape(sb, DEC_SEQ, D_MODEL)

    kprev[wr] = k_all[tq:, :]
    vprev[wr] = v_all[tq:, :]

    @pl.when(i == pl.num_programs(0) - 1)
    def _():
        kop_ref[...] = k_p[tq - WINDOW:, :].T
        vop_ref[...] = v_p[tq - WINDOW:, :].T
        pool_p_ref[...] = pltpu.roll(u_p[tq - CARRY_ROWS:, :], CARRY_ROWS - 1, 0)[0:POOL_PAD, :]


def _step_call(sinks, x_p, x_s, pref, ck, cv, cmk_t, cmv_t, rope_base, rope_res, rope_s, wts, mk, mv, bseg):
    tq = PROMPT_TILE
    sb = STEP_SEQS
    n = x_p.shape[0]
    nb = x_s.shape[0]
    assert n // tq == nb // sb
    row_spec = lambda w: pl.BlockSpec((tq, w), lambda i: (i, 0))
    seq3 = lambda a, b_: pl.BlockSpec((sb, a, b_), lambda i: (i, 0, 0))
    pool_s_spec = pl.BlockSpec((POOL_PAD, 2 * sb, POOL_W), lambda i: (0, i // 2, 0))
    cache_mem_spec = pl.BlockSpec((DEPTH, sb, MEM_W, N_MEM), lambda i: (0, i, 0, 0))
    in_specs = [pl.BlockSpec(memory_space=pltpu.SMEM), row_spec(D_MODEL),
                seq3(DEC_SEQ, D_MODEL), pool_s_spec, seq3(KV_W, WINDOW), seq3(KV_W, WINDOW),
                cache_mem_spec, cache_mem_spec,
                _const_spec(rope_base.shape), _const_spec(rope_res.shape), _const_spec(rope_s.shape)]
    in_specs += [_const_spec(w.shape) for w in wts]
    in_specs += [_const_spec(a.shape) for a in (mk, mv, bseg)]
    out_shape = [jax.ShapeDtypeStruct((n, D_MODEL), F32),
                 jax.ShapeDtypeStruct((POOL_PAD, POOL_W), F32),
                 jax.ShapeDtypeStruct((KV_W, WINDOW), F32),
                 jax.ShapeDtypeStruct((KV_W, WINDOW), F32),
                 jax.ShapeDtypeStruct((nb, DEC_SEQ, D_MODEL), F32),
                 jax.ShapeDtypeStruct((POOL_PAD, nb, POOL_W), F32),
                 jax.ShapeDtypeStruct((nb, KV_W, WINDOW), F32),
                 jax.ShapeDtypeStruct((nb, KV_W, WINDOW), F32)]
    out_specs = [row_spec(D_MODEL),
                 pl.BlockSpec((POOL_PAD, POOL_W), lambda i: (0, 0)),
                 pl.BlockSpec((KV_W, WINDOW), lambda i: (0, 0)),
                 pl.BlockSpec((KV_W, WINDOW), lambda i: (0, 0)),
                 seq3(DEC_SEQ, D_MODEL), pool_s_spec, seq3(KV_W, WINDOW), seq3(KV_W, WINDOW)]
    return pl.pallas_call(
        _step_kernel,
        grid=(n // tq,),
        in_specs=in_specs,
        out_specs=out_specs,
        out_shape=out_shape,
        scratch_shapes=[pltpu.VMEM((2, CARRY_ROWS, POOL_W), F32),
                        pltpu.VMEM((2, WINDOW, KV_W), BF16),
                        pltpu.VMEM((2, WINDOW, KV_W), BF16),
                        pltpu.VMEM((POOL_W // LANES, sb * DEC_SEQ, LANES), F32),
                        pltpu.VMEM((POOL_W // LANES, sb * DEC_SEQ, LANES), F32)],
        compiler_params=pltpu.CompilerParams(dimension_semantics=("arbitrary",),
                                             vmem_limit_bytes=STEP_VMEM_LIMIT),
        name="step",
    )(sinks, x_p, x_s, pref, ck, cv, cmk_t, cmv_t, rope_base, rope_res, rope_s, *wts, mk, mv, bseg)


PREP_ROWS = N_KV_HEADS * HEAD_DIM
PREP_VMEM_LIMIT = 40 * 1024 * 1024


def _cols_group_major(w):
    heads = [w[:, (kvh * GROUP + g) * HEAD_DIM:(kvh * GROUP + g + 1) * HEAD_DIM]
             for g in range(GROUP) for kvh in range(N_KV_HEADS)]
    return jnp.concatenate(heads, axis=1)


def _prep_kernel(w_in_a_ref, w_out_a_ref, w_kv_ref, w_in_b_ref, wob_h0_ref, wob_h1_ref, wob_h2_ref, wob_h3_ref,
                 wob_m_ref, w_mem_ref, mix_ref,
                 in_a_ref, out_a_ref, kv_ref, wq_ref, wg_ref, wm_ref, wos_ref, wom_ref, wkt_ref, wvt_ref, wbd_ref):
    @pl.when(pl.program_id(0) == 0)
    def _():
        n_grp = len(POOL_WINDOWS)
        for g in range(n_grp):
            pieces = [jnp.zeros((POOL_GW, POOL_GW), F32)] * n_grp
            pieces[g] = mix_ref[g]
            wbd_ref[g * POOL_GW:(g + 1) * POOL_GW, :] = jnp.concatenate(pieces, axis=1).astype(BF16)

    in_a_ref[...] = w_in_a_ref[...].astype(BF16)
    out_a_ref[...] = w_out_a_ref[...].astype(BF16)
    kv_ref[...] = w_kv_ref[...].astype(BF16)
    w_in_b = w_in_b_ref[...]
    wq_ref[...] = _cols_group_major(w_in_b[:, :SWA_W]).astype(BF16)
    wg_ref[...] = _cols_group_major(w_in_b[:, SWA_W:2 * SWA_W]).astype(BF16)
    wm_ref[...] = w_in_b[:, 2 * SWA_W:].astype(BF16)
    wos_ref[...] = jnp.concatenate([wob_h0_ref[0], wob_h1_ref[0], wob_h2_ref[0], wob_h3_ref[0]],
                                   axis=0).astype(BF16)
    wom_ref[...] = wob_m_ref[...].astype(BF16)
    for l in range(DEPTH):
        w_mem = w_mem_ref[l]
        wkt_ref[l] = w_mem[:, :MEM_W].T.astype(BF16)
        wvt_ref[l] = w_mem[:, MEM_W:].T.astype(BF16)


def _prep_call(w_in_a, w_out_a, w_kv, w_in_b, w_out_b, w_mem_kv, pool_mix):
    rows = lambda w: pl.BlockSpec((PREP_ROWS, w), lambda i: (i, 0))

    def head_src(kvh):
        return pl.BlockSpec((1, HEAD_DIM, D_MODEL),
                            lambda i: (kvh * GROUP + jnp.minimum(i, GROUP - 1), 0, 0))

    in_specs = [rows(2 * POOL_W + 2 * MEM_W), rows(D_MODEL), rows(2 * KV_W), rows(2 * SWA_W + 2 * MEM_W)]
    in_specs += [head_src(kvh) for kvh in range(N_KV_HEADS)]
    in_specs += [pl.BlockSpec((PREP_ROWS, D_MODEL), lambda i: (SWA_W // PREP_ROWS, 0)),
                pl.BlockSpec((DEPTH, PREP_ROWS, 2 * MEM_W), lambda i: (0, i, 0)),
                pl.BlockSpec(pool_mix.shape, lambda i: (0, 0, 0))]
    out_specs = [rows(2 * POOL_W + 2 * MEM_W), rows(D_MODEL), rows(2 * KV_W), rows(SWA_W), rows(SWA_W),
                 rows(2 * MEM_W),
                 pl.BlockSpec((PREP_ROWS, D_MODEL), lambda i: (jnp.minimum(i, GROUP - 1), 0)),
                 pl.BlockSpec((PREP_ROWS, D_MODEL), lambda i: (0, 0)),
                 pl.BlockSpec((DEPTH, MEM_W, PREP_ROWS), lambda i: (0, 0, i)),
                 pl.BlockSpec((DEPTH, MEM_W, PREP_ROWS), lambda i: (0, 0, i)),
                 pl.BlockSpec((POOL_W, POOL_W), lambda i: (0, 0))]
    bf = lambda *shape: jax.ShapeDtypeStruct(shape, BF16)
    out_shape = [bf(D_MODEL, 2 * POOL_W + 2 * MEM_W), bf(D_MODEL, D_MODEL), bf(D_MODEL, 2 * KV_W),
                 bf(D_MODEL, SWA_W), bf(D_MODEL, SWA_W), bf(D_MODEL, 2 * MEM_W),
                 bf(SWA_W, D_MODEL), bf(MEM_W, D_MODEL),
                 bf(DEPTH, MEM_W, D_MODEL), bf(DEPTH, MEM_W, D_MODEL), bf(POOL_W, POOL_W)]
    w_out_b_heads = w_out_b.reshape(D_MODEL // HEAD_DIM, HEAD_DIM, D_MODEL)
    return pl.pallas_call(
        _prep_kernel,
        grid=(D_MODEL // PREP_ROWS,),
        in_specs=in_specs,
        out_specs=out_specs,
        out_shape=out_shape,
        compiler_params=pltpu.CompilerParams(dimension_semantics=("arbitrary",),
                                             vmem_limit_bytes=PREP_VMEM_LIMIT),
        name="prep",
    )(w_in_a, w_out_a, w_kv, w_in_b, *([w_out_b_heads] * N_KV_HEADS), w_out_b, w_mem_kv, pool_mix)


def _rope_lane_tables(pos):
    half = ROT_DIM // 2
    inv = (ROPE_THETA ** (-np.arange(half, dtype=np.float32) * 2.0 / ROT_DIM)).astype(np.float32)
    ang = (np.asarray(pos, np.float32)[:, None] * inv[None, :]).astype(np.float64)
    cos, sin = np.cos(ang), np.sin(ang)
    t = ang.shape[0]
    rest = HEAD_DIM - ROT_DIM
    cos64 = np.concatenate([cos, cos, np.ones((t, rest))], axis=1)
    sin64 = np.concatenate([-sin, sin, np.zeros((t, rest))], axis=1)
    reps = LANES // HEAD_DIM
    return np.tile(cos64, (1, reps)).astype(np.float32), np.tile(sin64, (1, reps)).astype(np.float32)


def _head_tile(g, width):
    return jnp.tile(g.astype(F32), width // HEAD_DIM)


def kernel(x_prompt, x_sample, state_pool, cache_swa_k, cache_swa_v, cache_mem_k, cache_mem_v, mem_prompt,
           norm_a, w_in_a, pool_mix_w, pool_scale, w_out_a, kv_norm, w_kv, k_norm,
           norm_b, w_in_b, q_norm, sinks, w_out_b, mem_norm, w_mem_kv, mem_q_norm, mem_k_norm):
    seg = np.arange(MXU_DIM) // HEAD_DIM
    bseg = jnp.asarray((seg[:, None] == seg[None, :]).astype(np.float32) / HEAD_DIM, BF16)
    wcol = jnp.asarray(np.repeat(np.asarray(POOL_WINDOWS, np.float32), POOL_GW).reshape(1, POOL_W))

    w_in_a_b, w_out_a_b, w_kv_b, wq, wg, wm, wos, wom, wk_t, wv_t, wbd = _prep_call(
        w_in_a[0], w_out_a[0], w_kv, w_in_b[0], w_out_b[0], w_mem_kv, pool_mix_w[0])
    sinks_p = sinks[0].astype(F32).reshape(N_KV_HEADS, GROUP).T.reshape(N_Q_HEADS) * LOG2E

    pieces = {"norm_a": norm_a[0], "pool_scale": pool_scale[0], "kv_norm": kv_norm,
              "k_gain": _head_tile(k_norm, KV_W), "norm_b": norm_b[0], "q_gain": _head_tile(q_norm[0], SWA_W),
              "mem_q_gain0": _head_tile(mem_q_norm[0], MEM_W), "mem_q_gain1": _head_tile(mem_q_norm[1], MEM_W)}
    vecs = jnp.concatenate([pieces[name].astype(F32) for name, _ in _VEC_WIDTHS]).reshape(1, -1)
    wts = (vecs, w_in_a_b, wbd, wcol, w_out_a_b, w_kv_b, wq, wg, wm, wos, wom)
    mkg_col = jnp.stack([jnp.tile(mem_k_norm[l].astype(F32), MEM_HEADS).reshape(MEM_W, 1) for l in range(DEPTH)])

    mk_t, mv_t, mk_b, mv_b = _memkv_call(mem_prompt[0], mem_norm.reshape(DEPTH, 1, D_MODEL), wk_t, wv_t, mkg_col)

    n_tiles = SEQ // PROMPT_TILE
    cb, sb_ = _rope_lane_tables(np.arange(n_tiles) * PROMPT_TILE)
    cr, sr = _rope_lane_tables(np.arange(PROMPT_TILE))
    rope_base = jnp.asarray(np.stack([cb, sb_], axis=1))
    rope_res = jnp.asarray(np.stack([cr, sr], axis=0))
    cos_s, sin_s = _rope_lane_tables(PAST_LEN + np.arange(DEC_SEQ))
    rope_s = jnp.asarray(np.stack([np.tile(cos_s, (STEP_SEQS, 1)), np.tile(sin_s, (STEP_SEQS, 1))], axis=0))
    pref = jnp.transpose(state_pool[0], (1, 0, 2))
    cmk_t = jnp.transpose(cache_mem_k, (0, 1, 3, 4, 2)).reshape(DEPTH, DEC_BATCH, MEM_W, N_MEM)
    cmv_t = jnp.transpose(cache_mem_v, (0, 1, 3, 4, 2)).reshape(DEPTH, DEC_BATCH, MEM_W, N_MEM)
    ck_t = jnp.transpose(cache_swa_k, (0, 2, 3, 1)).reshape(DEC_BATCH, KV_W, WINDOW)
    cv_t = jnp.transpose(cache_swa_v, (0, 2, 3, 1)).reshape(DEC_BATCH, KV_W, WINDOW)
    y_p, pool_p, k_p, v_p, y_s, pool_s, k_s, v_s = _step_call(
        sinks_p, x_prompt[0], x_sample, pref, ck_t, cv_t, cmk_t, cmv_t, rope_base, rope_res, rope_s,
        wts, mk_b, mv_b, bseg)

    def mem_out(a):
        return jnp.transpose(a.reshape(DEPTH, 1, MEM_HEADS, HEAD_DIM, N_MEM), (0, 1, 4, 2, 3))

    def swa_out_t(a):
        return jnp.transpose(a.reshape(-1, N_KV_HEADS, HEAD_DIM, WINDOW), (0, 3, 1, 2))

    return (y_p[None], y_s, pool_p[None, None], jnp.transpose(pool_s, (1, 0, 2))[None],
            swa_out_t(k_p[None]), swa_out_t(v_p[None]), swa_out_t(k_s), swa_out_t(v_s),
            mem_out(mk_t), mem_out(mv_t))
```

```python
import jax
import jax.numpy as jnp
import numpy as np
from jax import lax
from jax.experimental import pallas as pl
from jax.experimental.pallas import tpu as pltpu

D_MODEL = 1024
SEQ = 16384
DEPTH = 2
DEC_BATCH = 128
DEC_SEQ = 8
PAST_LEN = 16384
HEAD_DIM = 64
POOL_W = 768
POOL_WINDOWS = (2, 4, 8, 16)
POOL_GW = 192
POOL_PAD = 15
N_Q_HEADS = 12
N_KV_HEADS = 4
GROUP = 3
SWA_W = 768
KV_W = 256
WINDOW = 128
N_MEM = 256
MEM_HEADS = 4
MEM_W = 256
ROT_DIM = 16
ROPE_THETA = 500000.0
EPS = 1e-6

F32 = jnp.float32
BF16 = jnp.bfloat16
NEG_INF = float("-inf")
LOG2E = 1.4426950408889634
Q_SCALE = HEAD_DIM ** -0.5 * LOG2E

LANES = 128
SUBLANES = 8
MXU_DIM = 256
HEADS_PER_CHUNK = MXU_DIM // HEAD_DIM
CARRY_ROWS = 16
PROMPT_TILE = 512
STEP_SEQS = DEC_BATCH // (SEQ // PROMPT_TILE)
STEP_VMEM_LIMIT = 60 * 1024 * 1024


_VEC_WIDTHS = (("norm_a", D_MODEL), ("pool_scale", POOL_W), ("kv_norm", D_MODEL), ("k_gain", KV_W),
               ("norm_b", D_MODEL), ("q_gain", SWA_W), ("mem_q_gain0", MEM_W), ("mem_q_gain1", MEM_W))
_VEC_OFFSET = {}
for _name, _width in _VEC_WIDTHS:
    _VEC_OFFSET[_name] = (sum(w for _, w in _VEC_WIDTHS[:len(_VEC_OFFSET)]), _width)


def _vec(vec_ref, name):
    start, width = _VEC_OFFSET[name]
    return vec_ref[:, start:start + width]


_W_WIDTHS = (("in_a", 2 * POOL_W + 2 * MEM_W), ("out_a", D_MODEL), ("kv", 2 * KV_W),
             ("q", SWA_W), ("q_gate", SWA_W), ("mem_b", 2 * MEM_W))
_W_OFFSET = {}
for _name, _width in _W_WIDTHS:
    _W_OFFSET[_name] = (sum(w for _, w in _W_WIDTHS[:len(_W_OFFSET)]), _width)
W_ALL_COLS = sum(w for _, w in _W_WIDTHS)


def _wcols(w_ref, name, lo=0, hi=None):
    start, width = _W_OFFSET[name]
    return w_ref[:, start + lo:start + (width if hi is None else hi)]


def _dot(a, b):
    return jnp.dot(a, b, preferred_element_type=F32)


def _dot_nt(a, b):
    return lax.dot_general(a, b, (((1,), (1,)), ((), ())), preferred_element_type=F32)


def _rms_unit(x):
    return x * lax.rsqrt(jnp.mean(x * x, axis=-1, keepdims=True) + EPS)


def _silu(g):
    return g / (1.0 + jnp.exp(-g))


def _head_rms(y, gain, bseg):
    parts = []
    for c in range(y.shape[1] // MXU_DIM):
        yc = y[:, c * MXU_DIM:(c + 1) * MXU_DIM]
        ms = _dot((yc * yc).astype(BF16), bseg)
        parts.append(yc * lax.rsqrt(ms + EPS) * gain[:, c * MXU_DIM:(c + 1) * MXU_DIM])
    return parts[0] if len(parts) == 1 else jnp.concatenate(parts, axis=1)


def _rope(y, cos_t, sin_t, first8):
    parts = []
    for c in range(y.shape[1] // LANES):
        yc = y[:, c * LANES:(c + 1) * LANES]
        partner = jnp.where(first8, pltpu.roll(yc, LANES - 8, 1), pltpu.roll(yc, 8, 1))
        parts.append(yc * cos_t + partner * sin_t)
    return parts[0] if len(parts) == 1 else jnp.concatenate(parts, axis=1)


def _pool_window_sums(u_hist):
    def back(a, k):
        return pltpu.roll(a, k, 0)

    lane = lax.broadcasted_iota(jnp.int32, (1, LANES), 1)
    s2 = u_hist + back(u_hist, 1)
    t = s2[:, LANES:]
    s4 = t + back(t, 2)
    t = s4[:, 2 * LANES:]
    s8 = t + back(t, 4)
    t = s8[:, LANES:]
    s16 = t + back(t, 8)
    tiles = [
        s2[:, :LANES],
        jnp.where(lane < 64, s2[:, LANES:2 * LANES], s4[:, :LANES]),
        s4[:, LANES:2 * LANES],
        s8[:, :LANES],
        jnp.where(lane < 64, s8[:, LANES:2 * LANES], s16[:, :LANES]),
        s16[:, LANES:],
    ]
    return jnp.concatenate(tiles, axis=1)


def _pool_window_sums_planes(planes):
    n = len(planes)

    def doubled(prev, k, first):
        return [prev[j] + prev[j - k] if j >= first else None for j in range(n)]

    lane = lax.broadcasted_iota(jnp.int32, (1, LANES), 1)
    s2 = doubled(planes, 1, 1)
    s4 = doubled([None if a is None else a[:, LANES:] for a in s2], 2, 3)
    s8 = doubled([None if a is None else a[:, 2 * LANES:] for a in s4], 4, 7)
    s16 = doubled([None if a is None else a[:, LANES:] for a in s8], 8, POOL_PAD)
    out = []
    for j in range(POOL_PAD, n):
        out.append(jnp.concatenate([
            s2[j][:, :LANES],
            jnp.where(lane < 64, s2[j][:, LANES:2 * LANES], s4[j][:, :LANES]),
            s4[j][:, LANES:2 * LANES],
            s8[j][:, :LANES],
            jnp.where(lane < 64, s8[j][:, LANES:2 * LANES], s16[j][:, :LANES]),
            s16[j][:, LANES:],
        ], axis=1))
    return out


def _pool_mix(d, wbd_ref):
    lo, hi = MXU_DIM, 2 * MXU_DIM
    return jnp.concatenate([
        _dot(d[:, :hi], wbd_ref[:hi, :lo]),
        _dot(d, wbd_ref[:, lo:hi]),
        _dot(d[:, lo:], wbd_ref[lo:, hi:]),
    ], axis=1)


def _head_masks():
    lane = lax.broadcasted_iota(jnp.int32, (1, MXU_DIM), 1)
    return [(lane // HEAD_DIM) == j for j in range(HEADS_PER_CHUNK)]


def _first8_mask():
    lane = lax.broadcasted_iota(jnp.int32, (1, LANES), 1)
    return (lane % HEAD_DIM) < (ROT_DIM // 2)


def _stack_heads(q, hmask):
    return jnp.concatenate([jnp.where(m, q, 0.0) for m in hmask], axis=0).astype(BF16)


def _unstack_heads(o, hmask):
    m = o.shape[0] // len(hmask)
    acc = None
    for h, mask in enumerate(hmask):
        oh = jnp.where(mask, o[h * m:(h + 1) * m, :], 0.0)
        acc = oh if acc is None else acc + oh
    return acc


def _sink_columns(sinks_ref, rows_per_head):
    hrow = lax.broadcasted_iota(jnp.int32, (HEADS_PER_CHUNK * rows_per_head, 1), 0) // rows_per_head
    cols = []
    for c in range(SWA_W // MXU_DIM):
        col = jnp.zeros((HEADS_PER_CHUNK * rows_per_head, 1), F32)
        for jj in range(HEADS_PER_CHUNK):
            col = jnp.where(hrow == jj, sinks_ref[c * HEADS_PER_CHUNK + jj], col)
        cols.append(col)
    return cols


def _softmax_weights(s, sink):
    m = jnp.max(s, axis=-1, keepdims=True)
    if sink is not None:
        m = jnp.maximum(m, sink)
    p = jnp.exp2(s - m)
    den = jnp.sum(p, axis=-1, keepdims=True)
    if sink is not None:
        den = den + jnp.exp2(sink - m)
    return p.astype(BF16), 1.0 / den


def _mem_attn(q, k_t, v_t, hmask):
    p, rden = _softmax_weights(_dot(_stack_heads(q, hmask), k_t), None)
    return _unstack_heads(_dot_nt(p, v_t) * rden, hmask)


def _memkv_kernel(mem_ref, norm_ref, wk_ref, wv_ref, kgain_ref, mk_ref, mv_ref, mkvb_ref):
    xn = (_rms_unit(mem_ref[...]) * norm_ref[0]).astype(BF16)
    k_t = _dot_nt(wk_ref[0], xn)
    v_t = _dot_nt(wv_ref[0], xn)
    k3 = k_t.reshape(MEM_HEADS, HEAD_DIM, N_MEM)
    ms = jnp.mean(k3 * k3, axis=1, keepdims=True)
    k_t = (k3 * lax.rsqrt(ms + EPS)).reshape(MEM_W, N_MEM) * kgain_ref[0]
    mk_ref[0] = k_t
    mv_ref[0] = v_t
    mkvb_ref[0, 0] = k_t.astype(BF16)
    mkvb_ref[0, 1] = v_t.astype(BF16)


def _memkv_call(mem, mem_norm, wk_t, wv_t, kgain_col):
    const2 = lambda l: (0, 0)
    per_layer3 = lambda l: (l, 0, 0)
    out_f = jax.ShapeDtypeStruct((DEPTH, MEM_W, N_MEM), F32)
    out_b = jax.ShapeDtypeStruct((DEPTH, 2, MEM_W, N_MEM), BF16)
    return pl.pallas_call(
        _memkv_kernel,
        grid=(DEPTH,),
        in_specs=[
            pl.BlockSpec((N_MEM, D_MODEL), const2),
            pl.BlockSpec((1, 1, D_MODEL), per_layer3),
            pl.BlockSpec((1, MEM_W, D_MODEL), per_layer3),
            pl.BlockSpec((1, MEM_W, D_MODEL), per_layer3),
            pl.BlockSpec((1, MEM_W, 1), per_layer3),
        ],
        out_specs=[pl.BlockSpec((1, MEM_W, N_MEM), per_layer3), pl.BlockSpec((1, MEM_W, N_MEM), per_layer3),
                   pl.BlockSpec((1, 2, MEM_W, N_MEM), lambda l: (l, 0, 0, 0))],
        out_shape=[out_f, out_f, out_b],
        compiler_params=pltpu.CompilerParams(dimension_semantics=("arbitrary",)),
        name="memkv",
    )(mem, mem_norm, wk_t, wv_t, kgain_col)


def _swa_block(qb, kc, vc, bias, sink_cols, hmask):
    outs = []
    for g in range(GROUP):
        s = _dot_nt(_stack_heads(qb[:, g * MXU_DIM:(g + 1) * MXU_DIM], hmask), kc)
        s = (s.reshape(HEADS_PER_CHUNK, WINDOW, 2 * WINDOW) + bias[None]).reshape(HEADS_PER_CHUNK * WINDOW, 2 * WINDOW)
        p, rden = _softmax_weights(s, sink_cols[g])
        outs.append(_unstack_heads(_dot(p, vc) * rden, hmask))
    return jnp.concatenate(outs, axis=1)


def _const_spec(shape):
    nd = len(shape)
    return pl.BlockSpec(shape, lambda i: (0,) * nd, pipeline_mode=pl.Buffered(1))


def _stack_heads_seq(q, n_seq, hmask):
    q3 = q.reshape(n_seq, DEC_SEQ, MXU_DIM)
    return jnp.stack([jnp.where(m, q3, 0.0) for m in hmask], axis=1)


def _unstack_heads_seq(o, n_seq, hmask):
    o4 = o.reshape(n_seq, HEADS_PER_CHUNK, DEC_SEQ, MXU_DIM)
    acc = None
    for h, mask in enumerate(hmask):
        oh = jnp.where(mask, o4[:, h], 0.0)
        acc = oh if acc is None else acc + oh
    return acc.reshape(n_seq * DEC_SEQ, MXU_DIM)


def _mem_attn_seqs(q, k_ref, v_ref, layer, n_seq, hmask):
    rows = HEADS_PER_CHUNK * DEC_SEQ
    qs = _stack_heads_seq(q, n_seq, hmask).reshape(n_seq * rows, MXU_DIM).astype(BF16)
    s = jnp.concatenate([_dot(qs[b * rows:(b + 1) * rows], k_ref[layer, b].astype(BF16))
                         for b in range(n_seq)], axis=0)
    p, rden = _softmax_weights(s, None)
    o = jnp.concatenate([_dot_nt(p[b * rows:(b + 1) * rows], v_ref[layer, b].astype(BF16))
                         for b in range(n_seq)], axis=0)
    return _unstack_heads_seq(o * rden, n_seq, hmask)


def _sample_pool(u_s, pref_ref, pool_ref, wcol, parity, u_scr, d_scr):
    sb = STEP_SEQS
    n_lt = POOL_W // LANES
    for c in range(n_lt):
        u_scr[c] = u_s[:, c * LANES:(c + 1) * LANES]
    planes = [jnp.where(parity == 0, pref_ref[t, 0:sb, :], pref_ref[t, sb:2 * sb, :]) for t in range(POOL_PAD)]
    planes += [jnp.concatenate([u_scr[c, pl.ds(t, sb, stride=DEC_SEQ), :] for c in range(n_lt)], axis=1)
               for t in range(DEC_SEQ)]
    for t, win in enumerate(_pool_window_sums_planes(planes)):
        cnt = jnp.minimum(float(PAST_LEN + t + 1), wcol)
        d_t = win / cnt - planes[POOL_PAD + t]
        for c in range(n_lt):
            d_scr[c, pl.ds(t, sb, stride=DEC_SEQ), :] = d_t[:, c * LANES:(c + 1) * LANES]

    @pl.when(parity == 0)
    def _():
        for t in range(POOL_PAD):
            pool_ref[t, 0:sb, :] = planes[DEC_SEQ + t]

    @pl.when(parity == 1)
    def _():
        for t in range(POOL_PAD):
            pool_ref[t, sb:2 * sb, :] = planes[DEC_SEQ + t]

    return jnp.concatenate([d_scr[c] for c in range(n_lt)], axis=1)


def _sample_cache_update(k_s, v_s, ck_ref, cv_ref, ko_ref, vo_ref):
    sb = STEP_SEQS
    keep = WINDOW - DEC_SEQ
    key_lane = lax.broadcasted_iota(jnp.int32, (1, WINDOW), 1)
    pad_rows = jnp.zeros((WINDOW - sb * DEC_SEQ, KV_W), F32)
    for new_rows, c_ref, o_ref in ((k_s, ck_ref, ko_ref), (v_s, cv_ref, vo_ref)):
        new_t = jnp.concatenate([new_rows, pad_rows], axis=0).T
        for b in range(sb):
            shifted = pltpu.roll(c_ref[b], keep, 1)
            placed = pltpu.roll(new_t, (keep - b * DEC_SEQ) % WINDOW, 1)
            o_ref[b] = jnp.where(key_lane < keep, shifted, placed)


def _sample_window_attn(q_s, k_s, v_s, ck_ref, cv_ref, sinks_ref, hmask):
    sb = STEP_SEQS
    m = sb * DEC_SEQ
    rows = N_Q_HEADS * DEC_SEQ
    qs = jnp.stack([_stack_heads_seq(q_s[:, g * MXU_DIM:(g + 1) * MXU_DIM], sb, hmask) for g in range(GROUP)],
                   axis=1)
    qs = qs.reshape(sb * rows, KV_W).astype(BF16)
    s_old = jnp.concatenate([_dot(qs[b * rows:(b + 1) * rows], ck_ref[b].astype(BF16)) for b in range(sb)], axis=0)
    s_new = _dot_nt(qs, k_s.astype(BF16))
    tq_old = lax.broadcasted_iota(jnp.int32, (rows, WINDOW), 0) % DEC_SEQ
    key_old = lax.broadcasted_iota(jnp.int32, (rows, WINDOW), 1)
    bias_old = jnp.where(key_old > tq_old, 0.0, NEG_INF)
    row_i = lax.broadcasted_iota(jnp.int32, (sb * rows, m), 0)
    col_i = lax.broadcasted_iota(jnp.int32, (sb * rows, m), 1)
    same_seq = (row_i // rows) == (col_i // DEC_SEQ)
    bias_new = jnp.where(same_seq, jnp.where(col_i % DEC_SEQ <= row_i % DEC_SEQ, 0.0, NEG_INF), NEG_INF)
    s_old = (s_old.reshape(sb, rows, WINDOW) + bias_old[None]).reshape(sb * rows, WINDOW)
    s_new = s_new + bias_new
    sink = jnp.concatenate(_sink_columns(sinks_ref, DEC_SEQ), axis=0)
    sink = jnp.concatenate([sink] * sb, axis=0)
    mx = jnp.maximum(jnp.maximum(jnp.max(s_old, axis=-1, keepdims=True), jnp.max(s_new, axis=-1, keepdims=True)), sink)
    p_old = jnp.exp2(s_old - mx)
    p_new = jnp.exp2(s_new - mx)
    den = (jnp.sum(p_old, axis=-1, keepdims=True) + jnp.sum(p_new, axis=-1, keepdims=True) + jnp.exp2(sink - mx))
    p_old = p_old.astype(BF16)
    o = jnp.concatenate([_dot_nt(p_old[b * rows:(b + 1) * rows], cv_ref[b].astype(BF16)) for b in range(sb)], axis=0)
    o = (o + _dot(p_new.astype(BF16), v_s.astype(BF16))) * (1.0 / den)
    o5 = o.reshape(sb, GROUP, HEADS_PER_CHUNK * DEC_SEQ, KV_W)
    return jnp.concatenate([_unstack_heads_seq(o5[:, g].reshape(sb * HEADS_PER_CHUNK * DEC_SEQ, KV_W), sb, hmask)
                            for g in range(GROUP)], axis=1)


def _step_kernel(sinks_ref, xp_ref, xs_ref, pref_ref, ck_ref, cv_ref, cmk_ref, cmv_ref,
                 rope_base_ref, rope_res_ref, rope_s_ref,
                 vec_ref, w_ref, wo_b_ref, wbd_ref, wcol_ref, mkv_ref, bseg_ref,
                 yp_ref, pool_p_ref, kop_ref, vop_ref, ys_ref, pool_s_ref, kos_ref, vos_ref,
                 ucarry, kprev, vprev, u_scr, d_scr):
    tq = PROMPT_TILE
    sb = STEP_SEQS
    m = sb * DEC_SEQ
    i = pl.program_id(0)

    @pl.when(i == 0)
    def _():
        ucarry[0] = jnp.zeros((CARRY_ROWS, POOL_W), F32)
        kprev[0] = jnp.zeros((WINDOW, KV_W), BF16)
        vprev[0] = jnp.zeros((WINDOW, KV_W), BF16)

    rd = i % 2
    wr = (i + 1) % 2

    hmask = _head_masks()
    first8 = _first8_mask()
    bseg = bseg_ref[...]
    x = jnp.concatenate([xp_ref[...], xs_ref[...].reshape(m, D_MODEL)], axis=0)

    base = rope_base_ref[i]
    cb, sb_ = base[0:1, :], base[1:2, :]
    cr, sr = rope_res_ref[0], rope_res_ref[1]
    cos_t = jnp.concatenate([cb * cr - sb_ * sr, rope_s_ref[0]], axis=0)
    sin_t = jnp.concatenate([sb_ * cr + cb * sr, rope_s_ref[1]], axis=0)

    xn = (_rms_unit(x) * _vec(vec_ref, "norm_a")).astype(BF16)
    u = _dot(xn, _wcols(w_ref, "in_a", 0, POOL_W))
    gp = _dot(xn, _wcols(w_ref, "in_a", POOL_W, 2 * POOL_W))
    qm = _dot(xn, _wcols(w_ref, "in_a", 2 * POOL_W, 2 * POOL_W + MEM_W))
    gm = _dot(xn, _wcols(w_ref, "in_a", 2 * POOL_W + MEM_W))
    u_p = u[:tq]
    u_hist = jnp.concatenate([ucarry[rd], u_p], axis=0)
    win = _pool_window_sums(u_hist)[CARRY_ROWS:]
    pos = i * tq + lax.broadcasted_iota(jnp.int32, (tq, 1), 0)
    cnt = jnp.minimum((pos + 1).astype(F32), wcol_ref[...])
    d_p = win / cnt - u_p
    d_s = _sample_pool(u[tq:], pref_ref, pool_s_ref, wcol_ref[...], rd, u_scr, d_scr)
    d = jnp.concatenate([d_p, d_s], axis=0)
    yp = _pool_mix(d.astype(BF16), wbd_ref) * _vec(vec_ref, "pool_scale") * _silu(gp)
    qmn = _head_rms(qm, _vec(vec_ref, "mem_q_gain0"), bseg) * Q_SCALE
    ym = jnp.concatenate([_mem_attn(qmn[:tq], mkv_ref[0, 0], mkv_ref[0, 1], hmask),
                          _mem_attn_seqs(qmn[tq:], cmk_ref, cmv_ref, 0, sb, hmask)], axis=0) * _silu(gm)
    x1 = x + _dot(jnp.concatenate([yp, ym], axis=1).astype(BF16), _wcols(w_ref, "out_a"))
    ucarry[wr] = u_p[tq - CARRY_ROWS:, :]

    r = _rms_unit(x1)
    kv = _dot((r * _vec(vec_ref, "kv_norm")).astype(BF16), _wcols(w_ref, "kv"))
    k = _rope(_head_rms(kv[:, :KV_W], _vec(vec_ref, "k_gain"), bseg), cos_t, sin_t, first8)
    v = kv[:, KV_W:]
    k_p, v_p, k_s, v_s = k[:tq], v[:tq], k[tq:], v[tq:]
    k_all = jnp.concatenate([kprev[rd], k_p.astype(BF16)], axis=0)
    v_all = jnp.concatenate([vprev[rd], v_p.astype(BF16)], axis=0)
    _sample_cache_update(k_s, v_s, ck_ref, cv_ref, kos_ref, vos_ref)

    xb = (r * _vec(vec_ref, "norm_b")).astype(BF16)
    zq = _dot(xb, _wcols(w_ref, "q"))
    gq = _dot(xb, _wcols(w_ref, "q_gate"))
    qm2 = _dot(xb, _wcols(w_ref, "mem_b", 0, MEM_W))
    gm2 = _dot(xb, _wcols(w_ref, "mem_b", MEM_W))
    q = _rope(_head_rms(zq, _vec(vec_ref, "q_gain"), bseg), cos_t, sin_t, first8) * Q_SCALE

    qi = lax.broadcasted_iota(jnp.int32, (WINDOW, 2 * WINDOW), 0)
    ci = lax.broadcasted_iota(jnp.int32, (WINDOW, 2 * WINDOW), 1)
    band_bias = jnp.where(ci > qi, jnp.where(ci <= qi + WINDOW, 0.0, NEG_INF), NEG_INF)
    sink_cols = _sink_columns(sinks_ref, WINDOW)
    ys_blocks = []
    for b in range(tq // WINDOW):
        bias = band_bias
        if b == 0:
            key_pos = ci + (i * tq - WINDOW)
            bias = jnp.where(key_pos >= 0, band_bias, NEG_INF)
        qb = q[b * WINDOW:(b + 1) * WINDOW, :]
        kc = k_all[b * WINDOW:(b + 2) * WINDOW, :]
        vc = v_all[b * WINDOW:(b + 2) * WINDOW, :]
        ys_blocks.append(_swa_block(qb, kc, vc, bias, sink_cols, hmask))
    ys_blocks.append(_sample_window_attn(q[tq:], k_s, v_s, ck_ref, cv_ref, sinks_ref, hmask))
    ys = jnp.concatenate(ys_blocks, axis=0) * _silu(gq)
    qmn2 = _head_rms(qm2, _vec(vec_ref, "mem_q_gain1"), bseg) * Q_SCALE
    ym2 = jnp.concatenate([_mem_attn(qmn2[:tq], mkv_ref[1, 0], mkv_ref[1, 1], hmask),
                           _mem_attn_seqs(qmn2[tq:], cmk_ref, cmv_ref, 1, sb, hmask)], axis=0) * _silu(gm2)
    y = x1 + _dot(ys.astype(BF16), wo_b_ref[0:SWA_W, :]) + _dot(ym2.astype(BF16), wo_b_ref[SWA_W:, :])
    yp_ref[...] = y[:tq]
    ys_ref[...] = y[tq:].reshape(sb, DEC_SEQ, D_MODEL)

    kprev[wr] = k_all[tq:, :]
    vprev[wr] = v_all[tq:, :]

    @pl.when(i == pl.num_programs(0) - 1)
    def _():
        kop_ref[...] = k_p[tq - WINDOW:, :].T
        vop_ref[...] = v_p[tq - WINDOW:, :].T
        pool_p_ref[...] = pltpu.roll(u_p[tq - CARRY_ROWS:, :], CARRY_ROWS - 1, 0)[0:POOL_PAD, :]


def _step_call(sinks, x_p, x_s, pref, ck, cv, cmk_t, cmv_t, rope_base, rope_res, rope_s, wts):
    tq = PROMPT_TILE
    sb = STEP_SEQS
    n = x_p.shape[0]
    nb = x_s.shape[0]
    assert n // tq == nb // sb
    row_spec = lambda w: pl.BlockSpec((tq, w), lambda i: (i, 0))
    seq3 = lambda a, b_: pl.BlockSpec((sb, a, b_), lambda i: (i, 0, 0))
    pool_s_spec = pl.BlockSpec((POOL_PAD, 2 * sb, POOL_W), lambda i: (0, i // 2, 0))
    cache_mem_spec = pl.BlockSpec((DEPTH, sb, MEM_W, N_MEM), lambda i: (0, i, 0, 0))
    in_specs = [pl.BlockSpec(memory_space=pltpu.SMEM), row_spec(D_MODEL),
                seq3(DEC_SEQ, D_MODEL), pool_s_spec, seq3(KV_W, WINDOW), seq3(KV_W, WINDOW),
                cache_mem_spec, cache_mem_spec,
                _const_spec(rope_base.shape), _const_spec(rope_res.shape), _const_spec(rope_s.shape)]
    in_specs += [_const_spec(w.shape) for w in wts]
    out_shape = [jax.ShapeDtypeStruct((n, D_MODEL), F32),
                 jax.ShapeDtypeStruct((POOL_PAD, POOL_W), F32),
                 jax.ShapeDtypeStruct((KV_W, WINDOW), F32),
                 jax.ShapeDtypeStruct((KV_W, WINDOW), F32),
                 jax.ShapeDtypeStruct((nb, DEC_SEQ, D_MODEL), F32),
                 jax.ShapeDtypeStruct((POOL_PAD, nb, POOL_W), F32),
                 jax.ShapeDtypeStruct((nb, KV_W, WINDOW), F32),
                 jax.ShapeDtypeStruct((nb, KV_W, WINDOW), F32)]
    out_specs = [row_spec(D_MODEL),
                 pl.BlockSpec((POOL_PAD, POOL_W), lambda i: (0, 0)),
                 pl.BlockSpec((KV_W, WINDOW), lambda i: (0, 0)),
                 pl.BlockSpec((KV_W, WINDOW), lambda i: (0, 0)),
                 seq3(DEC_SEQ, D_MODEL), pool_s_spec, seq3(KV_W, WINDOW), seq3(KV_W, WINDOW)]
    return pl.pallas_call(
        _step_kernel,
        grid=(n // tq,),
        in_specs=in_specs,
        out_specs=out_specs,
        out_shape=out_shape,
        scratch_shapes=[pltpu.VMEM((2, CARRY_ROWS, POOL_W), F32),
                        pltpu.VMEM((2, WINDOW, KV_W), BF16),
                        pltpu.VMEM((2, WINDOW, KV_W), BF16),
                        pltpu.VMEM((POOL_W // LANES, sb * DEC_SEQ, LANES), F32),
                        pltpu.VMEM((POOL_W // LANES, sb * DEC_SEQ, LANES), F32)],
        compiler_params=pltpu.CompilerParams(dimension_semantics=("arbitrary",),
                                             vmem_limit_bytes=STEP_VMEM_LIMIT),
        name="step",
    )(sinks, x_p, x_s, pref, ck, cv, cmk_t, cmv_t, rope_base, rope_res, rope_s, *wts)


PREP_ROWS = N_KV_HEADS * HEAD_DIM
PREP_VMEM_LIMIT = 40 * 1024 * 1024


def _cols_group_major(w):
    heads = [w[:, (kvh * GROUP + g) * HEAD_DIM:(kvh * GROUP + g + 1) * HEAD_DIM]
             for g in range(GROUP) for kvh in range(N_KV_HEADS)]
    return jnp.concatenate(heads, axis=1)


def _prep_kernel(w_in_a_ref, w_out_a_ref, w_kv_ref, w_in_b_ref, wob_h0_ref, wob_h1_ref, wob_h2_ref, wob_h3_ref,
                 wob_m_ref, w_mem_ref, mix_ref,
                 w_ref, wo_b_ref, wkt_ref, wvt_ref, wbd_ref):
    i = pl.program_id(0)

    def put(name, value):
        start, width = _W_OFFSET[name]
        w_ref[:, start:start + width] = value.astype(BF16)

    @pl.when(i == 0)
    def _():
        n_grp = len(POOL_WINDOWS)
        for g in range(n_grp):
            pieces = [jnp.zeros((POOL_GW, POOL_GW), F32)] * n_grp
            pieces[g] = mix_ref[g]
            wbd_ref[g * POOL_GW:(g + 1) * POOL_GW, :] = jnp.concatenate(pieces, axis=1).astype(BF16)

    put("in_a", w_in_a_ref[...])
    put("out_a", w_out_a_ref[...])
    put("kv", w_kv_ref[...])
    w_in_b = w_in_b_ref[...]
    put("q", _cols_group_major(w_in_b[:, :SWA_W]))
    put("q_gate", _cols_group_major(w_in_b[:, SWA_W:2 * SWA_W]))
    put("mem_b", w_in_b[:, 2 * SWA_W:])
    heads = jnp.concatenate([wob_h0_ref[0], wob_h1_ref[0], wob_h2_ref[0], wob_h3_ref[0]], axis=0)
    wo_b_ref[...] = jnp.where(i < GROUP, heads, wob_m_ref[...]).astype(BF16)
    for l in range(DEPTH):
        w_mem = w_mem_ref[l]
        wkt_ref[l] = w_mem[:, :MEM_W].T.astype(BF16)
        wvt_ref[l] = w_mem[:, MEM_W:].T.astype(BF16)


def _prep_call(w_in_a, w_out_a, w_kv, w_in_b, w_out_b, w_mem_kv, pool_mix):
    rows = lambda w: pl.BlockSpec((PREP_ROWS, w), lambda i: (i, 0))

    def head_src(kvh):
        return pl.BlockSpec((1, HEAD_DIM, D_MODEL),
                            lambda i: (kvh * GROUP + jnp.minimum(i, GROUP - 1), 0, 0))

    in_specs = [rows(2 * POOL_W + 2 * MEM_W), rows(D_MODEL), rows(2 * KV_W), rows(2 * SWA_W + 2 * MEM_W)]
    in_specs += [head_src(kvh) for kvh in range(N_KV_HEADS)]
    in_specs += [pl.BlockSpec((PREP_ROWS, D_MODEL), lambda i: (SWA_W // PREP_ROWS, 0)),
                pl.BlockSpec((DEPTH, PREP_ROWS, 2 * MEM_W), lambda i: (0, i, 0)),
                pl.BlockSpec(pool_mix.shape, lambda i: (0, 0, 0))]
    out_specs = [rows(W_ALL_COLS), rows(D_MODEL),
                 pl.BlockSpec((DEPTH, MEM_W, PREP_ROWS), lambda i: (0, 0, i)),
                 pl.BlockSpec((DEPTH, MEM_W, PREP_ROWS), lambda i: (0, 0, i)),
                 pl.BlockSpec((POOL_W, POOL_W), lambda i: (0, 0))]
    bf = lambda *shape: jax.ShapeDtypeStruct(shape, BF16)
    out_shape = [bf(D_MODEL, W_ALL_COLS), bf(D_MODEL, D_MODEL),
                 bf(DEPTH, MEM_W, D_MODEL), bf(DEPTH, MEM_W, D_MODEL), bf(POOL_W, POOL_W)]
    w_out_b_heads = w_out_b.reshape(D_MODEL // HEAD_DIM, HEAD_DIM, D_MODEL)
    return pl.pallas_call(
        _prep_kernel,
        grid=(D_MODEL // PREP_ROWS,),
        in_specs=in_specs,
        out_specs=out_specs,
        out_shape=out_shape,
        compiler_params=pltpu.CompilerParams(dimension_semantics=("arbitrary",),
                                             vmem_limit_bytes=PREP_VMEM_LIMIT),
        name="prep",
    )(w_in_a, w_out_a, w_kv, w_in_b, *([w_out_b_heads] * N_KV_HEADS), w_out_b, w_mem_kv, pool_mix)


def _rope_lane_tables(pos):
    half = ROT_DIM // 2
    inv = (ROPE_THETA ** (-np.arange(half, dtype=np.float32) * 2.0 / ROT_DIM)).astype(np.float32)
    ang = (np.asarray(pos, np.float32)[:, None] * inv[None, :]).astype(np.float64)
    cos, sin = np.cos(ang), np.sin(ang)
    t = ang.shape[0]
    rest = HEAD_DIM - ROT_DIM
    cos64 = np.concatenate([cos, cos, np.ones((t, rest))], axis=1)
    sin64 = np.concatenate([-sin, sin, np.zeros((t, rest))], axis=1)
    reps = LANES // HEAD_DIM
    return np.tile(cos64, (1, reps)).astype(np.float32), np.tile(sin64, (1, reps)).astype(np.float32)


def _head_tile(g, width):
    return jnp.tile(g.astype(F32), width // HEAD_DIM)


def kernel(x_prompt, x_sample, state_pool, cache_swa_k, cache_swa_v, cache_mem_k, cache_mem_v, mem_prompt,
           norm_a, w_in_a, pool_mix_w, pool_scale, w_out_a, kv_norm, w_kv, k_norm,
           norm_b, w_in_b, q_norm, sinks, w_out_b, mem_norm, w_mem_kv, mem_q_norm, mem_k_norm):
    seg = np.arange(MXU_DIM) // HEAD_DIM
    bseg = jnp.asarray((seg[:, None] == seg[None, :]).astype(np.float32) / HEAD_DIM, BF16)
    wcol = jnp.asarray(np.repeat(np.asarray(POOL_WINDOWS, np.float32), POOL_GW).reshape(1, POOL_W))

    w_all, wo_b, wk_t, wv_t, wbd = _prep_call(
        w_in_a[0], w_out_a[0], w_kv, w_in_b[0], w_out_b[0], w_mem_kv, pool_mix_w[0])
    sinks_p = sinks[0].astype(F32).reshape(N_KV_HEADS, GROUP).T.reshape(N_Q_HEADS) * LOG2E

    pieces = {"norm_a": norm_a[0], "pool_scale": pool_scale[0], "kv_norm": kv_norm,
              "k_gain": _head_tile(k_norm, KV_W), "norm_b": norm_b[0], "q_gain": _head_tile(q_norm[0], SWA_W),
              "mem_q_gain0": _head_tile(mem_q_norm[0], MEM_W), "mem_q_gain1": _head_tile(mem_q_norm[1], MEM_W)}
    vecs = jnp.concatenate([pieces[name].astype(F32) for name, _ in _VEC_WIDTHS]).reshape(1, -1)
    mkg_col = jnp.stack([jnp.tile(mem_k_norm[l].astype(F32), MEM_HEADS).reshape(MEM_W, 1) for l in range(DEPTH)])

    mk_t, mv_t, mkv_b = _memkv_call(mem_prompt[0], mem_norm.reshape(DEPTH, 1, D_MODEL), wk_t, wv_t, mkg_col)
    wts = (vecs, w_all, wo_b, wbd, wcol, mkv_b, bseg)

    n_tiles = SEQ // PROMPT_TILE
    cb, sb_ = _rope_lane_tables(np.arange(n_tiles) * PROMPT_TILE)
    cr, sr = _rope_lane_tables(np.arange(PROMPT_TILE))
    rope_base = jnp.asarray(np.stack([cb, sb_], axis=1))
    rope_res = jnp.asarray(np.stack([cr, sr], axis=0))
    cos_s, sin_s = _rope_lane_tables(PAST_LEN + np.arange(DEC_SEQ))
    rope_s = jnp.asarray(np.stack([np.tile(cos_s, (STEP_SEQS, 1)), np.tile(sin_s, (STEP_SEQS, 1))], axis=0))
    pref = jnp.transpose(state_pool[0], (1, 0, 2))
    cmk_t = jnp.transpose(cache_mem_k, (0, 1, 3, 4, 2)).reshape(DEPTH, DEC_BATCH, MEM_W, N_MEM)
    cmv_t = jnp.transpose(cache_mem_v, (0, 1, 3, 4, 2)).reshape(DEPTH, DEC_BATCH, MEM_W, N_MEM)
    ck_t = jnp.transpose(cache_swa_k, (0, 2, 3, 1)).reshape(DEC_BATCH, KV_W, WINDOW)
    cv_t = jnp.transpose(cache_swa_v, (0, 2, 3, 1)).reshape(DEC_BATCH, KV_W, WINDOW)
    y_p, pool_p, k_p, v_p, y_s, pool_s, k_s, v_s = _step_call(
        sinks_p, x_prompt[0], x_sample, pref, ck_t, cv_t, cmk_t, cmv_t, rope_base, rope_res, rope_s, wts)

    def mem_out(a):
        return jnp.transpose(a.reshape(DEPTH, 1, MEM_HEADS, HEAD_DIM, N_MEM), (0, 1, 4, 2, 3))

    def swa_out_t(a):
        return jnp.transpose(a.reshape(-1, N_KV_HEADS, HEAD_DIM, WINDOW), (0, 3, 1, 2))

    return (y_p[None], y_s, pool_p[None, None], jnp.transpose(pool_s, (1, 0, 2))[None],
            swa_out_t(k_p[None]), swa_out_t(v_p[None]), swa_out_t(k_s), swa_out_t(v_s),
            mem_out(mk_t), mem_out(mv_t))
```

```python
import jax
import jax.numpy as jnp
import numpy as np
from jax import lax
from jax.experimental import pallas as pl
from jax.experimental.pallas import tpu as pltpu

D_MODEL = 1024
SEQ = 16384
DEPTH = 2
DEC_BATCH = 128
DEC_SEQ = 8
PAST_LEN = 16384
HEAD_DIM = 64
POOL_W = 768
POOL_WINDOWS = (2, 4, 8, 16)
POOL_GW = 192
POOL_PAD = 15
N_Q_HEADS = 12
N_KV_HEADS = 4
GROUP = 3
SWA_W = 768
KV_W = 256
WINDOW = 128
N_MEM = 256
MEM_HEADS = 4
MEM_W = 256
ROT_DIM = 16
ROPE_THETA = 500000.0
EPS = 1e-6

F32 = jnp.float32
BF16 = jnp.bfloat16
NEG_INF = float("-inf")
LOG2E = 1.4426950408889634
Q_SCALE = HEAD_DIM ** -0.5 * LOG2E

LANES = 128
SUBLANES = 8
MXU_DIM = 256
HEADS_PER_CHUNK = MXU_DIM // HEAD_DIM
CARRY_ROWS = 16
PROMPT_TILE = 512
STEP_SEQS = DEC_BATCH // (SEQ // PROMPT_TILE)
STEP_VMEM_LIMIT = 60 * 1024 * 1024


_VEC_WIDTHS = (("norm_a", D_MODEL), ("pool_scale", POOL_W), ("kv_norm", D_MODEL), ("k_gain", KV_W),
               ("norm_b", D_MODEL), ("q_gain", SWA_W), ("mem_q_gain0", MEM_W), ("mem_q_gain1", MEM_W))
_VEC_OFFSET = {}
for _name, _width in _VEC_WIDTHS:
    _VEC_OFFSET[_name] = (sum(w for _, w in _VEC_WIDTHS[:len(_VEC_OFFSET)]), _width)


def _vec(vec_ref, name):
    start, width = _VEC_OFFSET[name]
    return vec_ref[:, start:start + width]


_W_WIDTHS = (("in_a", 2 * POOL_W + 2 * MEM_W), ("out_a", D_MODEL), ("kv", 2 * KV_W),
             ("q", SWA_W), ("q_gate", SWA_W), ("mem_b", 2 * MEM_W))
_W_OFFSET = {}
for _name, _width in _W_WIDTHS:
    _W_OFFSET[_name] = (sum(w for _, w in _W_WIDTHS[:len(_W_OFFSET)]), _width)
W_ALL_COLS = sum(w for _, w in _W_WIDTHS)


def _wcols(w_ref, name, lo=0, hi=None):
    start, width = _W_OFFSET[name]
    return w_ref[:, start + lo:start + (width if hi is None else hi)]


def _dot(a, b):
    return jnp.dot(a, b, preferred_element_type=F32)


def _dot_nt(a, b):
    return lax.dot_general(a, b, (((1,), (1,)), ((), ())), preferred_element_type=F32)


def _rms_unit(x):
    return x * lax.rsqrt(jnp.mean(x * x, axis=-1, keepdims=True) + EPS)


def _silu(g):
    return g / (1.0 + jnp.exp(-g))


def _head_rms(y, gain, bseg):
    parts = []
    for c in range(y.shape[1] // MXU_DIM):
        yc = y[:, c * MXU_DIM:(c + 1) * MXU_DIM]
        ms = _dot((yc * yc).astype(BF16), bseg)
        parts.append(yc * lax.rsqrt(ms + EPS) * gain[:, c * MXU_DIM:(c + 1) * MXU_DIM])
    return parts[0] if len(parts) == 1 else jnp.concatenate(parts, axis=1)


def _rope(y, cos_t, sin_t, first8):
    parts = []
    for c in range(y.shape[1] // LANES):
        yc = y[:, c * LANES:(c + 1) * LANES]
        partner = jnp.where(first8, pltpu.roll(yc, LANES - 8, 1), pltpu.roll(yc, 8, 1))
        parts.append(yc * cos_t + partner * sin_t)
    return parts[0] if len(parts) == 1 else jnp.concatenate(parts, axis=1)


def _pool_window_sums(u_hist):
    def back(a, k):
        return pltpu.roll(a, k, 0)

    lane = lax.broadcasted_iota(jnp.int32, (1, LANES), 1)
    s2 = u_hist + back(u_hist, 1)
    t = s2[:, LANES:]
    s4 = t + back(t, 2)
    t = s4[:, 2 * LANES:]
    s8 = t + back(t, 4)
    t = s8[:, LANES:]
    s16 = t + back(t, 8)
    tiles = [
        s2[:, :LANES],
        jnp.where(lane < 64, s2[:, LANES:2 * LANES], s4[:, :LANES]),
        s4[:, LANES:2 * LANES],
        s8[:, :LANES],
        jnp.where(lane < 64, s8[:, LANES:2 * LANES], s16[:, :LANES]),
        s16[:, LANES:],
    ]
    return jnp.concatenate(tiles, axis=1)


def _pool_window_sums_planes(planes):
    n = len(planes)

    def doubled(prev, k, first):
        return [prev[j] + prev[j - k] if j >= first else None for j in range(n)]

    lane = lax.broadcasted_iota(jnp.int32, (1, LANES), 1)
    s2 = doubled(planes, 1, 1)
    s4 = doubled([None if a is None else a[:, LANES:] for a in s2], 2, 3)
    s8 = doubled([None if a is None else a[:, 2 * LANES:] for a in s4], 4, 7)
    s16 = doubled([None if a is None else a[:, LANES:] for a in s8], 8, POOL_PAD)
    out = []
    for j in range(POOL_PAD, n):
        out.append(jnp.concatenate([
            s2[j][:, :LANES],
            jnp.where(lane < 64, s2[j][:, LANES:2 * LANES], s4[j][:, :LANES]),
            s4[j][:, LANES:2 * LANES],
            s8[j][:, :LANES],
            jnp.where(lane < 64, s8[j][:, LANES:2 * LANES], s16[j][:, :LANES]),
            s16[j][:, LANES:],
        ], axis=1))
    return out


def _pool_mix(d, wbd_ref):
    lo, hi = MXU_DIM, 2 * MXU_DIM
    return jnp.concatenate([
        _dot(d[:, :hi], wbd_ref[:hi, :lo]),
        _dot(d, wbd_ref[:, lo:hi]),
        _dot(d[:, lo:], wbd_ref[lo:, hi:]),
    ], axis=1)


def _head_masks():
    lane = lax.broadcasted_iota(jnp.int32, (1, MXU_DIM), 1)
    return [(lane // HEAD_DIM) == j for j in range(HEADS_PER_CHUNK)]


def _first8_mask():
    lane = lax.broadcasted_iota(jnp.int32, (1, LANES), 1)
    return (lane % HEAD_DIM) < (ROT_DIM // 2)


def _stack_heads(q, hmask):
    return jnp.concatenate([jnp.where(m, q, 0.0) for m in hmask], axis=0).astype(BF16)


def _unstack_heads(o, hmask):
    n = len(hmask)
    m = o.shape[0] // n
    acc = o[(n - 1) * m:, :]
    for h in range(n - 2, -1, -1):
        acc = jnp.where(hmask[h], o[h * m:(h + 1) * m, :], acc)
    return acc


def _sink_columns(sinks_ref, rows_per_head):
    hrow = lax.broadcasted_iota(jnp.int32, (HEADS_PER_CHUNK * rows_per_head, 1), 0) // rows_per_head
    cols = []
    for c in range(SWA_W // MXU_DIM):
        col = jnp.zeros((HEADS_PER_CHUNK * rows_per_head, 1), F32)
        for jj in range(HEADS_PER_CHUNK):
            col = jnp.where(hrow == jj, sinks_ref[c * HEADS_PER_CHUNK + jj], col)
        cols.append(col)
    return cols


def _softmax_weights(s, sink):
    m = jnp.max(s, axis=-1, keepdims=True)
    if sink is not None:
        m = jnp.maximum(m, sink)
    p = jnp.exp2(s - m)
    den = jnp.sum(p, axis=-1, keepdims=True)
    if sink is not None:
        den = den + jnp.exp2(sink - m)
    return p.astype(BF16), 1.0 / den


def _mem_attn(q, k_t, v_t, hmask):
    p, rden = _softmax_weights(_dot(_stack_heads(q, hmask), k_t), None)
    return _unstack_heads(_dot_nt(p, v_t) * rden, hmask)


def _memkv_kernel(mem_ref, norm_ref, wk_ref, wv_ref, kgain_ref, mk_ref, mv_ref, mkvb_ref):
    xn = (_rms_unit(mem_ref[...]) * norm_ref[0]).astype(BF16)
    k_t = _dot_nt(wk_ref[0], xn)
    v_t = _dot_nt(wv_ref[0], xn)
    k3 = k_t.reshape(MEM_HEADS, HEAD_DIM, N_MEM)
    ms = jnp.mean(k3 * k3, axis=1, keepdims=True)
    k_t = (k3 * lax.rsqrt(ms + EPS)).reshape(MEM_W, N_MEM) * kgain_ref[0]
    mk_ref[0] = k_t
    mv_ref[0] = v_t
    mkvb_ref[0, 0] = k_t.astype(BF16)
    mkvb_ref[0, 1] = v_t.astype(BF16)


def _memkv_call(mem, mem_norm, wk_t, wv_t, kgain_col):
    const2 = lambda l: (0, 0)
    per_layer3 = lambda l: (l, 0, 0)
    out_f = jax.ShapeDtypeStruct((DEPTH, MEM_W, N_MEM), F32)
    out_b = jax.ShapeDtypeStruct((DEPTH, 2, MEM_W, N_MEM), BF16)
    return pl.pallas_call(
        _memkv_kernel,
        grid=(DEPTH,),
        in_specs=[
            pl.BlockSpec((N_MEM, D_MODEL), const2),
            pl.BlockSpec((1, 1, D_MODEL), per_layer3),
            pl.BlockSpec((1, MEM_W, D_MODEL), per_layer3),
            pl.BlockSpec((1, MEM_W, D_MODEL), per_layer3),
            pl.BlockSpec((1, MEM_W, 1), per_layer3),
        ],
        out_specs=[pl.BlockSpec((1, MEM_W, N_MEM), per_layer3), pl.BlockSpec((1, MEM_W, N_MEM), per_layer3),
                   pl.BlockSpec((1, 2, MEM_W, N_MEM), lambda l: (l, 0, 0, 0))],
        out_shape=[out_f, out_f, out_b],
        compiler_params=pltpu.CompilerParams(dimension_semantics=("arbitrary",)),
        name="memkv",
    )(mem, mem_norm, wk_t, wv_t, kgain_col)


def _swa_block(qb, kc, vc, bias, sink_cols, hmask):
    outs = []
    for g in range(GROUP):
        s = _dot_nt(_stack_heads(qb[:, g * MXU_DIM:(g + 1) * MXU_DIM], hmask), kc)
        s = (s.reshape(HEADS_PER_CHUNK, WINDOW, 2 * WINDOW) + bias[None]).reshape(HEADS_PER_CHUNK * WINDOW, 2 * WINDOW)
        p, rden = _softmax_weights(s, sink_cols[g])
        outs.append(_unstack_heads(_dot(p, vc) * rden, hmask))
    return jnp.concatenate(outs, axis=1)


def _const_spec(shape):
    nd = len(shape)
    return pl.BlockSpec(shape, lambda i: (0,) * nd, pipeline_mode=pl.Buffered(1))


def _stack_heads_seq(q, n_seq, hmask):
    q3 = q.reshape(n_seq, DEC_SEQ, MXU_DIM)
    return jnp.stack([jnp.where(m, q3, 0.0) for m in hmask], axis=1)


def _unstack_heads_seq(o, n_seq, hmask):
    o4 = o.reshape(n_seq, HEADS_PER_CHUNK, DEC_SEQ, MXU_DIM)
    acc = o4[:, HEADS_PER_CHUNK - 1]
    for h in range(HEADS_PER_CHUNK - 2, -1, -1):
        acc = jnp.where(hmask[h], o4[:, h], acc)
    return acc.reshape(n_seq * DEC_SEQ, MXU_DIM)


def _mem_attn_seqs(q, k_ref, v_ref, layer, n_seq, hmask):
    rows = HEADS_PER_CHUNK * DEC_SEQ
    qs = _stack_heads_seq(q, n_seq, hmask).reshape(n_seq * rows, MXU_DIM).astype(BF16)
    s = jnp.concatenate([_dot(qs[b * rows:(b + 1) * rows], k_ref[layer, b].astype(BF16))
                         for b in range(n_seq)], axis=0)
    p, rden = _softmax_weights(s, None)
    o = jnp.concatenate([_dot_nt(p[b * rows:(b + 1) * rows], v_ref[layer, b].astype(BF16))
                         for b in range(n_seq)], axis=0)
    return _unstack_heads_seq(o * rden, n_seq, hmask)


def _sample_pool(u_s, pref_ref, pool_ref, wcol, parity, u_scr, d_scr):
    sb = STEP_SEQS
    n_lt = POOL_W // LANES
    for c in range(n_lt):
        u_scr[c] = u_s[:, c * LANES:(c + 1) * LANES]
    planes = [jnp.where(parity == 0, pref_ref[t, 0:sb, :], pref_ref[t, sb:2 * sb, :]) for t in range(POOL_PAD)]
    planes += [jnp.concatenate([u_scr[c, pl.ds(t, sb, stride=DEC_SEQ), :] for c in range(n_lt)], axis=1)
               for t in range(DEC_SEQ)]
    for t, win in enumerate(_pool_window_sums_planes(planes)):
        cnt = jnp.minimum(float(PAST_LEN + t + 1), wcol)
        d_t = win / cnt - planes[POOL_PAD + t]
        for c in range(n_lt):
            d_scr[c, pl.ds(t, sb, stride=DEC_SEQ), :] = d_t[:, c * LANES:(c + 1) * LANES]

    @pl.when(parity == 0)
    def _():
        for t in range(POOL_PAD):
            pool_ref[t, 0:sb, :] = planes[DEC_SEQ + t]

    @pl.when(parity == 1)
    def _():
        for t in range(POOL_PAD):
            pool_ref[t, sb:2 * sb, :] = planes[DEC_SEQ + t]

    return jnp.concatenate([d_scr[c] for c in range(n_lt)], axis=1)


def _sample_cache_update(k_s, v_s, ck_ref, cv_ref, ko_ref, vo_ref):
    sb = STEP_SEQS
    keep = WINDOW - DEC_SEQ
    key_lane = lax.broadcasted_iota(jnp.int32, (1, WINDOW), 1)
    pad_rows = jnp.zeros((WINDOW - sb * DEC_SEQ, KV_W), F32)
    for new_rows, c_ref, o_ref in ((k_s, ck_ref, ko_ref), (v_s, cv_ref, vo_ref)):
        new_t = jnp.concatenate([new_rows, pad_rows], axis=0).T
        for b in range(sb):
            shifted = pltpu.roll(c_ref[b], keep, 1)
            placed = pltpu.roll(new_t, (keep - b * DEC_SEQ) % WINDOW, 1)
            o_ref[b] = jnp.where(key_lane < keep, shifted, placed)


def _sample_window_attn(q_s, k_s, v_s, ck_ref, cv_ref, sinks_ref, hmask):
    sb = STEP_SEQS
    m = sb * DEC_SEQ
    rows = N_Q_HEADS * DEC_SEQ
    qs = jnp.stack([_stack_heads_seq(q_s[:, g * MXU_DIM:(g + 1) * MXU_DIM], sb, hmask) for g in range(GROUP)],
                   axis=1)
    qs = qs.reshape(sb * rows, KV_W).astype(BF16)
    s_old = jnp.concatenate([_dot(qs[b * rows:(b + 1) * rows], ck_ref[b].astype(BF16)) for b in range(sb)], axis=0)
    s_new = _dot_nt(qs, k_s.astype(BF16))
    tq_old = lax.broadcasted_iota(jnp.int32, (rows, WINDOW), 0) % DEC_SEQ
    key_old = lax.broadcasted_iota(jnp.int32, (rows, WINDOW), 1)
    bias_old = jnp.where(key_old > tq_old, 0.0, NEG_INF)
    row_i = lax.broadcasted_iota(jnp.int32, (sb * rows, m), 0)
    col_i = lax.broadcasted_iota(jnp.int32, (sb * rows, m), 1)
    same_seq = (row_i // rows) == (col_i // DEC_SEQ)
    bias_new = jnp.where(same_seq, jnp.where(col_i % DEC_SEQ <= row_i % DEC_SEQ, 0.0, NEG_INF), NEG_INF)
    s_old = (s_old.reshape(sb, rows, WINDOW) + bias_old[None]).reshape(sb * rows, WINDOW)
    s_new = s_new + bias_new
    sink = jnp.concatenate(_sink_columns(sinks_ref, DEC_SEQ), axis=0)
    sink = jnp.concatenate([sink] * sb, axis=0)
    mx = jnp.maximum(jnp.maximum(jnp.max(s_old, axis=-1, keepdims=True), jnp.max(s_new, axis=-1, keepdims=True)), sink)
    p_old = jnp.exp2(s_old - mx)
    p_new = jnp.exp2(s_new - mx)
    den = (jnp.sum(p_old, axis=-1, keepdims=True) + jnp.sum(p_new, axis=-1, keepdims=True) + jnp.exp2(sink - mx))
    p_old = p_old.astype(BF16)
    o = jnp.concatenate([_dot_nt(p_old[b * rows:(b + 1) * rows], cv_ref[b].astype(BF16)) for b in range(sb)], axis=0)
    o = (o + _dot(p_new.astype(BF16), v_s.astype(BF16))) * (1.0 / den)
    o5 = o.reshape(sb, GROUP, HEADS_PER_CHUNK * DEC_SEQ, KV_W)
    return jnp.concatenate([_unstack_heads_seq(o5[:, g].reshape(sb * HEADS_PER_CHUNK * DEC_SEQ, KV_W), sb, hmask)
                            for g in range(GROUP)], axis=1)


def _step_kernel(sinks_ref, xp_ref, xs_ref, pref_ref, ck_ref, cv_ref, cmk_ref, cmv_ref,
                 rope_base_ref, rope_res_ref, rope_s_ref,
                 vec_ref, w_ref, wo_b_ref, wbd_ref, wcol_ref, mkv_ref, bseg_ref,
                 yp_ref, pool_p_ref, kop_ref, vop_ref, ys_ref, pool_s_ref, kos_ref, vos_ref,
                 ucarry, kprev, vprev, u_scr, d_scr):
    tq = PROMPT_TILE
    sb = STEP_SEQS
    m = sb * DEC_SEQ
    i = pl.program_id(0)

    @pl.when(i == 0)
    def _():
        ucarry[0] = jnp.zeros((CARRY_ROWS, POOL_W), F32)
        kprev[0] = jnp.zeros((WINDOW, KV_W), BF16)
        vprev[0] = jnp.zeros((WINDOW, KV_W), BF16)

    rd = i % 2
    wr = (i + 1) % 2

    hmask = _head_masks()
    first8 = _first8_mask()
    bseg = bseg_ref[...]
    x = jnp.concatenate([xp_ref[...], xs_ref[...].reshape(m, D_MODEL)], axis=0)

    base = rope_base_ref[i]
    cb, sb_ = base[0:1, :], base[1:2, :]
    cr, sr = rope_res_ref[0], rope_res_ref[1]
    cos_t = jnp.concatenate([cb * cr - sb_ * sr, rope_s_ref[0]], axis=0)
    sin_t = jnp.concatenate([sb_ * cr + cb * sr, rope_s_ref[1]], axis=0)

    xn = (_rms_unit(x) * _vec(vec_ref, "norm_a")).astype(BF16)
    u = _dot(xn, _wcols(w_ref, "in_a", 0, POOL_W))
    gp = _dot(xn, _wcols(w_ref, "in_a", POOL_W, 2 * POOL_W))
    qm = _dot(xn, _wcols(w_ref, "in_a", 2 * POOL_W, 2 * POOL_W + MEM_W))
    gm = _dot(xn, _wcols(w_ref, "in_a", 2 * POOL_W + MEM_W))
    u_p = u[:tq]
    u_hist = jnp.concatenate([ucarry[rd], u_p], axis=0)
    win = _pool_window_sums(u_hist)[CARRY_ROWS:]
    pos = i * tq + lax.broadcasted_iota(jnp.int32, (tq, 1), 0)
    cnt = jnp.minimum((pos + 1).astype(F32), wcol_ref[...])
    d_p = win / cnt - u_p
    d_s = _sample_pool(u[tq:], pref_ref, pool_s_ref, wcol_ref[...], rd, u_scr, d_scr)
    d = jnp.concatenate([d_p, d_s], axis=0)
    yp = _pool_mix(d.astype(BF16), wbd_ref) * _vec(vec_ref, "pool_scale") * _silu(gp)
    qmn = _head_rms(qm, _vec(vec_ref, "mem_q_gain0"), bseg) * Q_SCALE
    ym = jnp.concatenate([_mem_attn(qmn[:tq], mkv_ref[0, 0], mkv_ref[0, 1], hmask),
                          _mem_attn_seqs(qmn[tq:], cmk_ref, cmv_ref, 0, sb, hmask)], axis=0) * _silu(gm)
    x1 = x + _dot(jnp.concatenate([yp, ym], axis=1).astype(BF16), _wcols(w_ref, "out_a"))
    ucarry[wr] = u_p[tq - CARRY_ROWS:, :]

    r = _rms_unit(x1)
    kv = _dot((r * _vec(vec_ref, "kv_norm")).astype(BF16), _wcols(w_ref, "kv"))
    k = _rope(_head_rms(kv[:, :KV_W], _vec(vec_ref, "k_gain"), bseg), cos_t, sin_t, first8)
    v = kv[:, KV_W:]
    k_p, v_p, k_s, v_s = k[:tq], v[:tq], k[tq:], v[tq:]
    k_all = jnp.concatenate([kprev[rd], k_p.astype(BF16)], axis=0)
    v_all = jnp.concatenate([vprev[rd], v_p.astype(BF16)], axis=0)
    _sample_cache_update(k_s, v_s, ck_ref, cv_ref, kos_ref, vos_ref)

    xb = (r * _vec(vec_ref, "norm_b")).astype(BF16)
    zq = _dot(xb, _wcols(w_ref, "q"))
    gq = _dot(xb, _wcols(w_ref, "q_gate"))
    qm2 = _dot(xb, _wcols(w_ref, "mem_b", 0, MEM_W))
    gm2 = _dot(xb, _wcols(w_ref, "mem_b", MEM_W))
    q = _rope(_head_rms(zq, _vec(vec_ref, "q_gain"), bseg), cos_t, sin_t, first8) * Q_SCALE

    qi = lax.broadcasted_iota(jnp.int32, (WINDOW, 2 * WINDOW), 0)
    ci = lax.broadcasted_iota(jnp.int32, (WINDOW, 2 * WINDOW), 1)
    band_bias = jnp.where(ci > qi, jnp.where(ci <= qi + WINDOW, 0.0, NEG_INF), NEG_INF)
    sink_cols = _sink_columns(sinks_ref, WINDOW)
    ys_blocks = []
    for b in range(tq // WINDOW):
        bias = band_bias
        if b == 0:
            key_pos = ci + (i * tq - WINDOW)
            bias = jnp.where(key_pos >= 0, band_bias, NEG_INF)
        qb = q[b * WINDOW:(b + 1) * WINDOW, :]
        kc = k_all[b * WINDOW:(b + 2) * WINDOW, :]
        vc = v_all[b * WINDOW:(b + 2) * WINDOW, :]
        ys_blocks.append(_swa_block(qb, kc, vc, bias, sink_cols, hmask))
    ys_blocks.append(_sample_window_attn(q[tq:], k_s, v_s, ck_ref, cv_ref, sinks_ref, hmask))
    ys = jnp.concatenate(ys_blocks, axis=0) * _silu(gq)
    qmn2 = _head_rms(qm2, _vec(vec_ref, "mem_q_gain1"), bseg) * Q_SCALE
    ym2 = jnp.concatenate([_mem_attn(qmn2[:tq], mkv_ref[1, 0], mkv_ref[1, 1], hmask),
                           _mem_attn_seqs(qmn2[tq:], cmk_ref, cmv_ref, 1, sb, hmask)], axis=0) * _silu(gm2)
    y = x1 + _dot(ys.astype(BF16), wo_b_ref[0:SWA_W, :]) + _dot(ym2.astype(BF16), wo_b_ref[SWA_W:, :])
    yp_ref[...] = y[:tq]
    ys_ref[...] = y[tq:].reshape(sb, DEC_SEQ, D_MODEL)

    kprev[wr] = k_all[tq:, :]
    vprev[wr] = v_all[tq:, :]

    @pl.when(i == pl.num_programs(0) - 1)
    def _():
        kop_ref[...] = k_p[tq - WINDOW:, :].T
        vop_ref[...] = v_p[tq - WINDOW:, :].T
        pool_p_ref[...] = pltpu.roll(u_p[tq - CARRY_ROWS:, :], CARRY_ROWS - 1, 0)[0:POOL_PAD, :]


def _step_call(sinks, x_p, x_s, pref, ck, cv, cmk_t, cmv_t, rope_base, rope_res, rope_s, wts):
    tq = PROMPT_TILE
    sb = STEP_SEQS
    n = x_p.shape[0]
    nb = x_s.shape[0]
    assert n // tq == nb // sb
    row_spec = lambda w: pl.BlockSpec((tq, w), lambda i: (i, 0))
    seq3 = lambda a, b_: pl.BlockSpec((sb, a, b_), lambda i: (i, 0, 0))
    pool_s_spec = pl.BlockSpec((POOL_PAD, 2 * sb, POOL_W), lambda i: (0, i // 2, 0))
    cache_mem_spec = pl.BlockSpec((DEPTH, sb, MEM_W, N_MEM), lambda i: (0, i, 0, 0))
    in_specs = [pl.BlockSpec(memory_space=pltpu.SMEM), row_spec(D_MODEL),
                seq3(DEC_SEQ, D_MODEL), pool_s_spec, seq3(KV_W, WINDOW), seq3(KV_W, WINDOW),
                cache_mem_spec, cache_mem_spec,
                _const_spec(rope_base.shape), _const_spec(rope_res.shape), _const_spec(rope_s.shape)]
    in_specs += [_const_spec(w.shape) for w in wts]
    out_shape = [jax.ShapeDtypeStruct((n, D_MODEL), F32),
                 jax.ShapeDtypeStruct((POOL_PAD, POOL_W), F32),
                 jax.ShapeDtypeStruct((KV_W, WINDOW), F32),
                 jax.ShapeDtypeStruct((KV_W, WINDOW), F32),
                 jax.ShapeDtypeStruct((nb, DEC_SEQ, D_MODEL), F32),
                 jax.ShapeDtypeStruct((POOL_PAD, nb, POOL_W), F32),
                 jax.ShapeDtypeStruct((nb, KV_W, WINDOW), F32),
                 jax.ShapeDtypeStruct((nb, KV_W, WINDOW), F32)]
    out_specs = [row_spec(D_MODEL),
                 pl.BlockSpec((POOL_PAD, POOL_W), lambda i: (0, 0)),
                 pl.BlockSpec((KV_W, WINDOW), lambda i: (0, 0)),
                 pl.BlockSpec((KV_W, WINDOW), lambda i: (0, 0)),
                 seq3(DEC_SEQ, D_MODEL), pool_s_spec, seq3(KV_W, WINDOW), seq3(KV_W, WINDOW)]
    return pl.pallas_call(
        _step_kernel,
        grid=(n // tq,),
        in_specs=in_specs,
        out_specs=out_specs,
        out_shape=out_shape,
        scratch_shapes=[pltpu.VMEM((2, CARRY_ROWS, POOL_W), F32),
                        pltpu.VMEM((2, WINDOW, KV_W), BF16),
                        pltpu.VMEM((2, WINDOW, KV_W), BF16),
                        pltpu.VMEM((POOL_W // LANES, sb * DEC_SEQ, LANES), F32),
                        pltpu.VMEM((POOL_W // LANES, sb * DEC_SEQ, LANES), F32)],
        compiler_params=pltpu.CompilerParams(dimension_semantics=("arbitrary",),
                                             vmem_limit_bytes=STEP_VMEM_LIMIT),
        name="step",
    )(sinks, x_p, x_s, pref, ck, cv, cmk_t, cmv_t, rope_base, rope_res, rope_s, *wts)


PREP_ROWS = N_KV_HEADS * HEAD_DIM
PREP_VMEM_LIMIT = 40 * 1024 * 1024


def _cols_group_major(w):
    heads = [w[:, (kvh * GROUP + g) * HEAD_DIM:(kvh * GROUP + g + 1) * HEAD_DIM]
             for g in range(GROUP) for kvh in range(N_KV_HEADS)]
    return jnp.concatenate(heads, axis=1)


def _prep_kernel(w_in_a_ref, w_out_a_ref, w_kv_ref, w_in_b_ref, wob_h0_ref, wob_h1_ref, wob_h2_ref, wob_h3_ref,
                 wob_m_ref, w_mem_ref, mix_ref,
                 norm_a_ref, pool_scale_ref, kv_norm_ref, k_norm_ref, norm_b_ref, q_norm_ref, mem_q_norm_ref,
                 w_ref, wo_b_ref, wkt_ref, wvt_ref, wbd_ref, vec_ref):
    i = pl.program_id(0)

    def put(name, value):
        start, width = _W_OFFSET[name]
        w_ref[:, start:start + width] = value.astype(BF16)

    @pl.when(i == 0)
    def _():
        n_grp = len(POOL_WINDOWS)
        for g in range(n_grp):
            pieces = [jnp.zeros((POOL_GW, POOL_GW), F32)] * n_grp
            pieces[g] = mix_ref[g]
            wbd_ref[g * POOL_GW:(g + 1) * POOL_GW, :] = jnp.concatenate(pieces, axis=1).astype(BF16)

        def per_head(g_row, width):
            return jnp.concatenate([g_row] * (width // HEAD_DIM), axis=1)

        vec_pieces = {"norm_a": norm_a_ref[...], "pool_scale": pool_scale_ref[...], "kv_norm": kv_norm_ref[...],
                      "k_gain": per_head(k_norm_ref[...], KV_W), "norm_b": norm_b_ref[...],
                      "q_gain": per_head(q_norm_ref[...], SWA_W),
                      "mem_q_gain0": per_head(mem_q_norm_ref[0:1, :], MEM_W),
                      "mem_q_gain1": per_head(mem_q_norm_ref[1:2, :], MEM_W)}
        for name, (start, width) in _VEC_OFFSET.items():
            vec_ref[:, start:start + width] = vec_pieces[name]

    put("in_a", w_in_a_ref[...])
    put("out_a", w_out_a_ref[...])
    put("kv", w_kv_ref[...])
    w_in_b = w_in_b_ref[...]
    put("q", _cols_group_major(w_in_b[:, :SWA_W]))
    put("q_gate", _cols_group_major(w_in_b[:, SWA_W:2 * SWA_W]))
    put("mem_b", w_in_b[:, 2 * SWA_W:])
    heads = jnp.concatenate([wob_h0_ref[0], wob_h1_ref[0], wob_h2_ref[0], wob_h3_ref[0]], axis=0)
    wo_b_ref[...] = jnp.where(i < GROUP, heads, wob_m_ref[...]).astype(BF16)
    for l in range(DEPTH):
        w_mem = w_mem_ref[l]
        wkt_ref[l] = w_mem[:, :MEM_W].T.astype(BF16)
        wvt_ref[l] = w_mem[:, MEM_W:].T.astype(BF16)


def _prep_call(w_in_a, w_out_a, w_kv, w_in_b, w_out_b, w_mem_kv, pool_mix, small):
    rows = lambda w: pl.BlockSpec((PREP_ROWS, w), lambda i: (i, 0))

    def head_src(kvh):
        return pl.BlockSpec((1, HEAD_DIM, D_MODEL),
                            lambda i: (kvh * GROUP + jnp.minimum(i, GROUP - 1), 0, 0))

    in_specs = [rows(2 * POOL_W + 2 * MEM_W), rows(D_MODEL), rows(2 * KV_W), rows(2 * SWA_W + 2 * MEM_W)]
    in_specs += [head_src(kvh) for kvh in range(N_KV_HEADS)]
    in_specs += [pl.BlockSpec((PREP_ROWS, D_MODEL), lambda i: (SWA_W // PREP_ROWS, 0)),
                pl.BlockSpec((DEPTH, PREP_ROWS, 2 * MEM_W), lambda i: (0, i, 0)),
                pl.BlockSpec(pool_mix.shape, lambda i: (0, 0, 0))]
    in_specs += [pl.BlockSpec(a.shape, lambda i: (0, 0)) for a in small]
    n_vec = sum(w for _, w in _VEC_WIDTHS)
    out_specs = [rows(W_ALL_COLS), rows(D_MODEL),
                 pl.BlockSpec((DEPTH, MEM_W, PREP_ROWS), lambda i: (0, 0, i)),
                 pl.BlockSpec((DEPTH, MEM_W, PREP_ROWS), lambda i: (0, 0, i)),
                 pl.BlockSpec((POOL_W, POOL_W), lambda i: (0, 0)),
                 pl.BlockSpec((1, n_vec), lambda i: (0, 0))]
    bf = lambda *shape: jax.ShapeDtypeStruct(shape, BF16)
    out_shape = [bf(D_MODEL, W_ALL_COLS), bf(D_MODEL, D_MODEL),
                 bf(DEPTH, MEM_W, D_MODEL), bf(DEPTH, MEM_W, D_MODEL), bf(POOL_W, POOL_W),
                 jax.ShapeDtypeStruct((1, n_vec), F32)]
    w_out_b_heads = w_out_b.reshape(D_MODEL // HEAD_DIM, HEAD_DIM, D_MODEL)
    return pl.pallas_call(
        _prep_kernel,
        grid=(D_MODEL // PREP_ROWS,),
        in_specs=in_specs,
        out_specs=out_specs,
        out_shape=out_shape,
        compiler_params=pltpu.CompilerParams(dimension_semantics=("arbitrary",),
                                             vmem_limit_bytes=PREP_VMEM_LIMIT),
        name="prep",
    )(w_in_a, w_out_a, w_kv, w_in_b, *([w_out_b_heads] * N_KV_HEADS), w_out_b, w_mem_kv, pool_mix, *small)


def _rope_lane_tables(pos):
    half = ROT_DIM // 2
    inv = (ROPE_THETA ** (-np.arange(half, dtype=np.float32) * 2.0 / ROT_DIM)).astype(np.float32)
    ang = (np.asarray(pos, np.float32)[:, None] * inv[None, :]).astype(np.float64)
    cos, sin = np.cos(ang), np.sin(ang)
    t = ang.shape[0]
    rest = HEAD_DIM - ROT_DIM
    cos64 = np.concatenate([cos, cos, np.ones((t, rest))], axis=1)
    sin64 = np.concatenate([-sin, sin, np.zeros((t, rest))], axis=1)
    reps = LANES // HEAD_DIM
    return np.tile(cos64, (1, reps)).astype(np.float32), np.tile(sin64, (1, reps)).astype(np.float32)


def kernel(x_prompt, x_sample, state_pool, cache_swa_k, cache_swa_v, cache_mem_k, cache_mem_v, mem_prompt,
           norm_a, w_in_a, pool_mix_w, pool_scale, w_out_a, kv_norm, w_kv, k_norm,
           norm_b, w_in_b, q_norm, sinks, w_out_b, mem_norm, w_mem_kv, mem_q_norm, mem_k_norm):
    seg = np.arange(MXU_DIM) // HEAD_DIM
    bseg = jnp.asarray((seg[:, None] == seg[None, :]).astype(np.float32) / HEAD_DIM, BF16)
    wcol = jnp.asarray(np.repeat(np.asarray(POOL_WINDOWS, np.float32), POOL_GW).reshape(1, POOL_W))

    small = (norm_a, pool_scale, kv_norm.reshape(1, D_MODEL), k_norm.reshape(1, HEAD_DIM), norm_b, q_norm, mem_q_norm)
    w_all, wo_b, wk_t, wv_t, wbd, vecs = _prep_call(
        w_in_a[0], w_out_a[0], w_kv, w_in_b[0], w_out_b[0], w_mem_kv, pool_mix_w[0], small)
    sinks_p = sinks[0].astype(F32).reshape(N_KV_HEADS, GROUP).T.reshape(N_Q_HEADS) * LOG2E
    mkg_col = jnp.stack([jnp.tile(mem_k_norm[l].astype(F32), MEM_HEADS).reshape(MEM_W, 1) for l in range(DEPTH)])

    mk_t, mv_t, mkv_b = _memkv_call(mem_prompt[0], mem_norm.reshape(DEPTH, 1, D_MODEL), wk_t, wv_t, mkg_col)
    wts = (vecs, w_all, wo_b, wbd, wcol, mkv_b, bseg)

    n_tiles = SEQ // PROMPT_TILE
    cb, sb_ = _rope_lane_tables(np.arange(n_tiles) * PROMPT_TILE)
    cr, sr = _rope_lane_tables(np.arange(PROMPT_TILE))
    rope_base = jnp.asarray(np.stack([cb, sb_], axis=1))
    rope_res = jnp.asarray(np.stack([cr, sr], axis=0))
    cos_s, sin_s = _rope_lane_tables(PAST_LEN + np.arange(DEC_SEQ))
    rope_s = jnp.asarray(np.stack([np.tile(cos_s, (STEP_SEQS, 1)), np.tile(sin_s, (STEP_SEQS, 1))], axis=0))
    pref = jnp.transpose(state_pool[0], (1, 0, 2))
    cmk_t = jnp.transpose(cache_mem_k, (0, 1, 3, 4, 2)).reshape(DEPTH, DEC_BATCH, MEM_W, N_MEM)
    cmv_t = jnp.transpose(cache_mem_v, (0, 1, 3, 4, 2)).reshape(DEPTH, DEC_BATCH, MEM_W, N_MEM)
    ck_t = jnp.transpose(cache_swa_k, (0, 2, 3, 1)).reshape(DEC_BATCH, KV_W, WINDOW)
    cv_t = jnp.transpose(cache_swa_v, (0, 2, 3, 1)).reshape(DEC_BATCH, KV_W, WINDOW)
    y_p, pool_p, k_p, v_p, y_s, pool_s, k_s, v_s = _step_call(
        sinks_p, x_prompt[0], x_sample, pref, ck_t, cv_t, cmk_t, cmv_t, rope_base, rope_res, rope_s, wts)

    def mem_out(a):
        return jnp.transpose(a.reshape(DEPTH, 1, MEM_HEADS, HEAD_DIM, N_MEM), (0, 1, 4, 2, 3))

    def swa_out_t(a):
        return jnp.transpose(a.reshape(-1, N_KV_HEADS, HEAD_DIM, WINDOW), (0, 3, 1, 2))

    return (y_p[None], y_s, pool_p[None, None], jnp.transpose(pool_s, (1, 0, 2))[None],
            swa_out_t(k_p[None]), swa_out_t(v_p[None]), swa_out_t(k_s), swa_out_t(v_s),
            mem_out(mk_t), mem_out(mv_t))
```

```python
import jax
import jax.numpy as jnp
import numpy as np
from jax import lax
from jax.experimental import pallas as pl
from jax.experimental.pallas import tpu as pltpu

D_MODEL = 1024
SEQ = 16384
DEPTH = 2
DEC_BATCH = 128
DEC_SEQ = 8
PAST_LEN = 16384
HEAD_DIM = 64
POOL_W = 768
POOL_WINDOWS = (2, 4, 8, 16)
POOL_GW = 192
POOL_PAD = 15
N_Q_HEADS = 12
N_KV_HEADS = 4
GROUP = 3
SWA_W = 768
KV_W = 256
WINDOW = 128
N_MEM = 256
MEM_HEADS = 4
MEM_W = 256
ROT_DIM = 16
ROPE_THETA = 500000.0
EPS = 1e-6

F32 = jnp.float32
BF16 = jnp.bfloat16
NEG_INF = float("-inf")
LOG2E = 1.4426950408889634
Q_SCALE = HEAD_DIM ** -0.5 * LOG2E

LANES = 128
SUBLANES = 8
MXU_DIM = 256
HEADS_PER_CHUNK = MXU_DIM // HEAD_DIM
CARRY_ROWS = 16
PROMPT_TILE = 512
STEP_SEQS = DEC_BATCH // (SEQ // PROMPT_TILE)
STEP_VMEM_LIMIT = 60 * 1024 * 1024


_VEC_WIDTHS = (("norm_a", D_MODEL), ("pool_scale", POOL_W), ("kv_norm", D_MODEL), ("k_gain", KV_W),
               ("norm_b", D_MODEL), ("q_gain", SWA_W), ("mem_q_gain0", MEM_W), ("mem_q_gain1", MEM_W))
_VEC_OFFSET = {}
for _name, _width in _VEC_WIDTHS:
    _VEC_OFFSET[_name] = (sum(w for _, w in _VEC_WIDTHS[:len(_VEC_OFFSET)]), _width)


def _vec(vec_ref, name):
    start, width = _VEC_OFFSET[name]
    return vec_ref[:, start:start + width]


_W_WIDTHS = (("in_a", 2 * POOL_W + 2 * MEM_W), ("out_a", D_MODEL), ("kv", 2 * KV_W),
             ("q", SWA_W), ("q_gate", SWA_W), ("mem_b", 2 * MEM_W))
_W_OFFSET = {}
for _name, _width in _W_WIDTHS:
    _W_OFFSET[_name] = (sum(w for _, w in _W_WIDTHS[:len(_W_OFFSET)]), _width)
W_ALL_COLS = sum(w for _, w in _W_WIDTHS)


def _wcols(w_ref, name, lo=0, hi=None):
    start, width = _W_OFFSET[name]
    return w_ref[:, start + lo:start + (width if hi is None else hi)]


def _dot(a, b):
    return jnp.dot(a, b, preferred_element_type=F32)


def _dot_nt(a, b):
    return lax.dot_general(a, b, (((1,), (1,)), ((), ())), preferred_element_type=F32)


def _rms_unit(x):
    return x * lax.rsqrt(jnp.mean(x * x, axis=-1, keepdims=True) + EPS)


def _silu(g):
    return g / (1.0 + jnp.exp(-g))


def _head_rms(y, gain, bseg):
    parts = []
    for c in range(y.shape[1] // MXU_DIM):
        yc = y[:, c * MXU_DIM:(c + 1) * MXU_DIM]
        ms = _dot((yc * yc).astype(BF16), bseg)
        parts.append(yc * lax.rsqrt(ms + EPS) * gain[:, c * MXU_DIM:(c + 1) * MXU_DIM])
    return parts[0] if len(parts) == 1 else jnp.concatenate(parts, axis=1)


def _rope(y, cos_t, sin_t, first8):
    parts = []
    for c in range(y.shape[1] // LANES):
        yc = y[:, c * LANES:(c + 1) * LANES]
        partner = jnp.where(first8, pltpu.roll(yc, LANES - 8, 1), pltpu.roll(yc, 8, 1))
        parts.append(yc * cos_t + partner * sin_t)
    return parts[0] if len(parts) == 1 else jnp.concatenate(parts, axis=1)


def _pool_window_sums(u_hist):
    def back(a, k):
        return pltpu.roll(a, k, 0)

    lane = lax.broadcasted_iota(jnp.int32, (1, LANES), 1)
    s2 = u_hist + back(u_hist, 1)
    t = s2[:, LANES:]
    s4 = t + back(t, 2)
    t = s4[:, 2 * LANES:]
    s8 = t + back(t, 4)
    t = s8[:, LANES:]
    s16 = t + back(t, 8)
    tiles = [
        s2[:, :LANES],
        jnp.where(lane < 64, s2[:, LANES:2 * LANES], s4[:, :LANES]),
        s4[:, LANES:2 * LANES],
        s8[:, :LANES],
        jnp.where(lane < 64, s8[:, LANES:2 * LANES], s16[:, :LANES]),
        s16[:, LANES:],
    ]
    return jnp.concatenate(tiles, axis=1)


def _pool_window_sums_planes(planes):
    n = len(planes)

    def doubled(prev, k, first):
        return [prev[j] + prev[j - k] if j >= first else None for j in range(n)]

    lane = lax.broadcasted_iota(jnp.int32, (1, LANES), 1)
    s2 = doubled(planes, 1, 1)
    s4 = doubled([None if a is None else a[:, LANES:] for a in s2], 2, 3)
    s8 = doubled([None if a is None else a[:, 2 * LANES:] for a in s4], 4, 7)
    s16 = doubled([None if a is None else a[:, LANES:] for a in s8], 8, POOL_PAD)
    out = []
    for j in range(POOL_PAD, n):
        out.append(jnp.concatenate([
            s2[j][:, :LANES],
            jnp.where(lane < 64, s2[j][:, LANES:2 * LANES], s4[j][:, :LANES]),
            s4[j][:, LANES:2 * LANES],
            s8[j][:, :LANES],
            jnp.where(lane < 64, s8[j][:, LANES:2 * LANES], s16[j][:, :LANES]),
            s16[j][:, LANES:],
        ], axis=1))
    return out


def _pool_mix(d, wbd_ref):
    lo, hi = MXU_DIM, 2 * MXU_DIM
    return jnp.concatenate([
        _dot(d[:, :hi], wbd_ref[:hi, :lo]),
        _dot(d, wbd_ref[:, lo:hi]),
        _dot(d[:, lo:], wbd_ref[lo:, hi:]),
    ], axis=1)


def _head_masks():
    lane = lax.broadcasted_iota(jnp.int32, (1, MXU_DIM), 1)
    return [(lane // HEAD_DIM) == j for j in range(HEADS_PER_CHUNK)]


def _first8_mask():
    lane = lax.broadcasted_iota(jnp.int32, (1, LANES), 1)
    return (lane % HEAD_DIM) < (ROT_DIM // 2)


def _stack_heads(q, hmask):
    return jnp.concatenate([jnp.where(m, q, 0.0) for m in hmask], axis=0).astype(BF16)


def _unstack_heads(o, hmask):
    n = len(hmask)
    m = o.shape[0] // n
    acc = o[(n - 1) * m:, :]
    for h in range(n - 2, -1, -1):
        acc = jnp.where(hmask[h], o[h * m:(h + 1) * m, :], acc)
    return acc


def _sink_columns(sinks_ref, rows_per_head):
    hrow = lax.broadcasted_iota(jnp.int32, (HEADS_PER_CHUNK * rows_per_head, 1), 0) // rows_per_head
    cols = []
    for c in range(SWA_W // MXU_DIM):
        col = jnp.zeros((HEADS_PER_CHUNK * rows_per_head, 1), F32)
        for jj in range(HEADS_PER_CHUNK):
            col = jnp.where(hrow == jj, sinks_ref[c * HEADS_PER_CHUNK + jj], col)
        cols.append(col)
    return cols


def _softmax_weights(s, sink):
    m = jnp.max(s, axis=-1, keepdims=True)
    if sink is not None:
        m = jnp.maximum(m, sink)
    p = jnp.exp2(s - m)
    den = jnp.sum(p, axis=-1, keepdims=True)
    if sink is not None:
        den = den + jnp.exp2(sink - m)
    return p.astype(BF16), 1.0 / den


def _mem_attn(q, k_t, v_t, hmask):
    p, rden = _softmax_weights(_dot(_stack_heads(q, hmask), k_t), None)
    return _unstack_heads(_dot_nt(p, v_t) * rden, hmask)


def _swa_block(qb, kc, vc, bias, sink_cols, hmask):
    outs = []
    for g in range(GROUP):
        s = _dot_nt(_stack_heads(qb[:, g * MXU_DIM:(g + 1) * MXU_DIM], hmask), kc)
        s = (s.reshape(HEADS_PER_CHUNK, WINDOW, 2 * WINDOW) + bias[None]).reshape(HEADS_PER_CHUNK * WINDOW, 2 * WINDOW)
        p, rden = _softmax_weights(s, sink_cols[g])
        outs.append(_unstack_heads(_dot(p, vc) * rden, hmask))
    return jnp.concatenate(outs, axis=1)


def _const_spec(shape):
    nd = len(shape)
    return pl.BlockSpec(shape, lambda i: (0,) * nd, pipeline_mode=pl.Buffered(1))


def _stack_heads_seq(q, n_seq, hmask):
    q3 = q.reshape(n_seq, DEC_SEQ, MXU_DIM)
    return jnp.stack([jnp.where(m, q3, 0.0) for m in hmask], axis=1)


def _unstack_heads_seq(o, n_seq, hmask):
    o4 = o.reshape(n_seq, HEADS_PER_CHUNK, DEC_SEQ, MXU_DIM)
    acc = o4[:, HEADS_PER_CHUNK - 1]
    for h in range(HEADS_PER_CHUNK - 2, -1, -1):
        acc = jnp.where(hmask[h], o4[:, h], acc)
    return acc.reshape(n_seq * DEC_SEQ, MXU_DIM)


def _mem_attn_seqs(q, k_ref, v_ref, layer, n_seq, hmask):
    rows = HEADS_PER_CHUNK * DEC_SEQ
    qs = _stack_heads_seq(q, n_seq, hmask).reshape(n_seq * rows, MXU_DIM).astype(BF16)
    s = jnp.concatenate([_dot(qs[b * rows:(b + 1) * rows], k_ref[layer, b].astype(BF16))
                         for b in range(n_seq)], axis=0)
    p, rden = _softmax_weights(s, None)
    o = jnp.concatenate([_dot_nt(p[b * rows:(b + 1) * rows], v_ref[layer, b].astype(BF16))
                         for b in range(n_seq)], axis=0)
    return _unstack_heads_seq(o * rden, n_seq, hmask)


def _sample_pool(u_s, pref_ref, pool_ref, wcol, parity, u_scr, d_scr):
    sb = STEP_SEQS
    n_lt = POOL_W // LANES
    for c in range(n_lt):
        u_scr[c] = u_s[:, c * LANES:(c + 1) * LANES]
    planes = [jnp.where(parity == 0, pref_ref[t, 0:sb, :], pref_ref[t, sb:2 * sb, :]) for t in range(POOL_PAD)]
    planes += [jnp.concatenate([u_scr[c, pl.ds(t, sb, stride=DEC_SEQ), :] for c in range(n_lt)], axis=1)
               for t in range(DEC_SEQ)]
    for t, win in enumerate(_pool_window_sums_planes(planes)):
        cnt = jnp.minimum(float(PAST_LEN + t + 1), wcol)
        d_t = win / cnt - planes[POOL_PAD + t]
        for c in range(n_lt):
            d_scr[c, pl.ds(t, sb, stride=DEC_SEQ), :] = d_t[:, c * LANES:(c + 1) * LANES]

    @pl.when(parity == 0)
    def _():
        for t in range(POOL_PAD):
            pool_ref[t, 0:sb, :] = planes[DEC_SEQ + t]

    @pl.when(parity == 1)
    def _():
        for t in range(POOL_PAD):
            pool_ref[t, sb:2 * sb, :] = planes[DEC_SEQ + t]

    return jnp.concatenate([d_scr[c] for c in range(n_lt)], axis=1)


def _sample_cache_update(k_s, v_s, ck_ref, cv_ref, ko_ref, vo_ref):
    sb = STEP_SEQS
    keep = WINDOW - DEC_SEQ
    key_lane = lax.broadcasted_iota(jnp.int32, (1, WINDOW), 1)
    pad_rows = jnp.zeros((WINDOW - sb * DEC_SEQ, KV_W), F32)
    for new_rows, c_ref, o_ref in ((k_s, ck_ref, ko_ref), (v_s, cv_ref, vo_ref)):
        new_t = jnp.concatenate([new_rows, pad_rows], axis=0).T
        for b in range(sb):
            shifted = pltpu.roll(c_ref[b], keep, 1)
            placed = pltpu.roll(new_t, (keep - b * DEC_SEQ) % WINDOW, 1)
            o_ref[b] = jnp.where(key_lane < keep, shifted, placed)


def _sample_window_attn(q_s, k_s, v_s, ck_ref, cv_ref, sinks_ref, hmask):
    sb = STEP_SEQS
    m = sb * DEC_SEQ
    rows = N_Q_HEADS * DEC_SEQ
    qs = jnp.stack([_stack_heads_seq(q_s[:, g * MXU_DIM:(g + 1) * MXU_DIM], sb, hmask) for g in range(GROUP)],
                   axis=1)
    qs = qs.reshape(sb * rows, KV_W).astype(BF16)
    s_old = jnp.concatenate([_dot(qs[b * rows:(b + 1) * rows], ck_ref[b].astype(BF16)) for b in range(sb)], axis=0)
    s_new = _dot_nt(qs, k_s.astype(BF16))
    tq_old = lax.broadcasted_iota(jnp.int32, (rows, WINDOW), 0) % DEC_SEQ
    key_old = lax.broadcasted_iota(jnp.int32, (rows, WINDOW), 1)
    bias_old = jnp.where(key_old > tq_old, 0.0, NEG_INF)
    row_i = lax.broadcasted_iota(jnp.int32, (sb * rows, m), 0)
    col_i = lax.broadcasted_iota(jnp.int32, (sb * rows, m), 1)
    same_seq = (row_i // rows) == (col_i // DEC_SEQ)
    bias_new = jnp.where(same_seq, jnp.where(col_i % DEC_SEQ <= row_i % DEC_SEQ, 0.0, NEG_INF), NEG_INF)
    s_old = (s_old.reshape(sb, rows, WINDOW) + bias_old[None]).reshape(sb * rows, WINDOW)
    s_new = s_new + bias_new
    sink = jnp.concatenate(_sink_columns(sinks_ref, DEC_SEQ), axis=0)
    sink = jnp.concatenate([sink] * sb, axis=0)
    mx = jnp.maximum(jnp.maximum(jnp.max(s_old, axis=-1, keepdims=True), jnp.max(s_new, axis=-1, keepdims=True)), sink)
    p_old = jnp.exp2(s_old - mx)
    p_new = jnp.exp2(s_new - mx)
    den = (jnp.sum(p_old, axis=-1, keepdims=True) + jnp.sum(p_new, axis=-1, keepdims=True) + jnp.exp2(sink - mx))
    p_old = p_old.astype(BF16)
    o = jnp.concatenate([_dot_nt(p_old[b * rows:(b + 1) * rows], cv_ref[b].astype(BF16)) for b in range(sb)], axis=0)
    o = (o + _dot(p_new.astype(BF16), v_s.astype(BF16))) * (1.0 / den)
    o5 = o.reshape(sb, GROUP, HEADS_PER_CHUNK * DEC_SEQ, KV_W)
    return jnp.concatenate([_unstack_heads_seq(o5[:, g].reshape(sb * HEADS_PER_CHUNK * DEC_SEQ, KV_W), sb, hmask)
                            for g in range(GROUP)], axis=1)


def _step_kernel(sinks_ref, xp_ref, xs_ref, pref_ref, ck_ref, cv_ref, cmk_ref, cmv_ref,
                 rope_base_ref, rope_res_ref, rope_s_ref,
                 vec_ref, w_ref, wo_b_ref, wbd_ref, wcol_ref, mkv_ref, bseg_ref,
                 yp_ref, pool_p_ref, kop_ref, vop_ref, ys_ref, pool_s_ref, kos_ref, vos_ref,
                 ucarry, kprev, vprev, u_scr, d_scr):
    tq = PROMPT_TILE
    sb = STEP_SEQS
    m = sb * DEC_SEQ
    i = pl.program_id(0)

    @pl.when(i == 0)
    def _():
        ucarry[0] = jnp.zeros((CARRY_ROWS, POOL_W), F32)
        kprev[0] = jnp.zeros((WINDOW, KV_W), BF16)
        vprev[0] = jnp.zeros((WINDOW, KV_W), BF16)

    rd = i % 2
    wr = (i + 1) % 2

    hmask = _head_masks()
    first8 = _first8_mask()
    bseg = bseg_ref[...]
    x = jnp.concatenate([xp_ref[...], xs_ref[...].reshape(m, D_MODEL)], axis=0)

    base = rope_base_ref[i]
    cb, sb_ = base[0:1, :], base[1:2, :]
    cr, sr = rope_res_ref[0], rope_res_ref[1]
    cos_t = jnp.concatenate([cb * cr - sb_ * sr, rope_s_ref[0]], axis=0)
    sin_t = jnp.concatenate([sb_ * cr + cb * sr, rope_s_ref[1]], axis=0)

    xn = (_rms_unit(x) * _vec(vec_ref, "norm_a")).astype(BF16)
    u = _dot(xn, _wcols(w_ref, "in_a", 0, POOL_W))
    gp = _dot(xn, _wcols(w_ref, "in_a", POOL_W, 2 * POOL_W))
    qm = _dot(xn, _wcols(w_ref, "in_a", 2 * POOL_W, 2 * POOL_W + MEM_W))
    gm = _dot(xn, _wcols(w_ref, "in_a", 2 * POOL_W + MEM_W))
    u_p = u[:tq]
    u_hist = jnp.concatenate([ucarry[rd], u_p], axis=0)
    win = _pool_window_sums(u_hist)[CARRY_ROWS:]
    pos = i * tq + lax.broadcasted_iota(jnp.int32, (tq, 1), 0)
    cnt = jnp.minimum((pos + 1).astype(F32), wcol_ref[...])
    d_p = win / cnt - u_p
    d_s = _sample_pool(u[tq:], pref_ref, pool_s_ref, wcol_ref[...], rd, u_scr, d_scr)
    d = jnp.concatenate([d_p, d_s], axis=0)
    yp = _pool_mix(d.astype(BF16), wbd_ref) * _vec(vec_ref, "pool_scale") * _silu(gp)
    qmn = _head_rms(qm, _vec(vec_ref, "mem_q_gain0"), bseg) * Q_SCALE
    ym = jnp.concatenate([_mem_attn(qmn[:tq], mkv_ref[0, 0], mkv_ref[0, 1], hmask),
                          _mem_attn_seqs(qmn[tq:], cmk_ref, cmv_ref, 0, sb, hmask)], axis=0) * _silu(gm)
    x1 = x + _dot(jnp.concatenate([yp, ym], axis=1).astype(BF16), _wcols(w_ref, "out_a"))
    ucarry[wr] = u_p[tq - CARRY_ROWS:, :]

    r = _rms_unit(x1)
    kv = _dot((r * _vec(vec_ref, "kv_norm")).astype(BF16), _wcols(w_ref, "kv"))
    k = _rope(_head_rms(kv[:, :KV_W], _vec(vec_ref, "k_gain"), bseg), cos_t, sin_t, first8)
    v = kv[:, KV_W:]
    k_p, v_p, k_s, v_s = k[:tq], v[:tq], k[tq:], v[tq:]
    k_all = jnp.concatenate([kprev[rd], k_p.astype(BF16)], axis=0)
    v_all = jnp.concatenate([vprev[rd], v_p.astype(BF16)], axis=0)
    _sample_cache_update(k_s, v_s, ck_ref, cv_ref, kos_ref, vos_ref)

    xb = (r * _vec(vec_ref, "norm_b")).astype(BF16)
    zq = _dot(xb, _wcols(w_ref, "q"))
    gq = _dot(xb, _wcols(w_ref, "q_gate"))
    qm2 = _dot(xb, _wcols(w_ref, "mem_b", 0, MEM_W))
    gm2 = _dot(xb, _wcols(w_ref, "mem_b", MEM_W))
    q = _rope(_head_rms(zq, _vec(vec_ref, "q_gain"), bseg), cos_t, sin_t, first8) * Q_SCALE

    qi = lax.broadcasted_iota(jnp.int32, (WINDOW, 2 * WINDOW), 0)
    ci = lax.broadcasted_iota(jnp.int32, (WINDOW, 2 * WINDOW), 1)
    band_bias = jnp.where(ci > qi, jnp.where(ci <= qi + WINDOW, 0.0, NEG_INF), NEG_INF)
    sink_cols = _sink_columns(sinks_ref, WINDOW)
    ys_blocks = []
    for b in range(tq // WINDOW):
        bias = band_bias
        if b == 0:
            key_pos = ci + (i * tq - WINDOW)
            bias = jnp.where(key_pos >= 0, band_bias, NEG_INF)
        qb = q[b * WINDOW:(b + 1) * WINDOW, :]
        kc = k_all[b * WINDOW:(b + 2) * WINDOW, :]
        vc = v_all[b * WINDOW:(b + 2) * WINDOW, :]
        ys_blocks.append(_swa_block(qb, kc, vc, bias, sink_cols, hmask))
    ys_blocks.append(_sample_window_attn(q[tq:], k_s, v_s, ck_ref, cv_ref, sinks_ref, hmask))
    ys = jnp.concatenate(ys_blocks, axis=0) * _silu(gq)
    qmn2 = _head_rms(qm2, _vec(vec_ref, "mem_q_gain1"), bseg) * Q_SCALE
    ym2 = jnp.concatenate([_mem_attn(qmn2[:tq], mkv_ref[1, 0], mkv_ref[1, 1], hmask),
                           _mem_attn_seqs(qmn2[tq:], cmk_ref, cmv_ref, 1, sb, hmask)], axis=0) * _silu(gm2)
    y = x1 + _dot(ys.astype(BF16), wo_b_ref[0:SWA_W, :]) + _dot(ym2.astype(BF16), wo_b_ref[SWA_W:, :])
    yp_ref[...] = y[:tq]
    ys_ref[...] = y[tq:].reshape(sb, DEC_SEQ, D_MODEL)

    kprev[wr] = k_all[tq:, :]
    vprev[wr] = v_all[tq:, :]

    @pl.when(i == pl.num_programs(0) - 1)
    def _():
        kop_ref[...] = k_p[tq - WINDOW:, :].T
        vop_ref[...] = v_p[tq - WINDOW:, :].T
        pool_p_ref[...] = pltpu.roll(u_p[tq - CARRY_ROWS:, :], CARRY_ROWS - 1, 0)[0:POOL_PAD, :]


def _step_call(sinks, x_p, x_s, pref, ck, cv, cmk_t, cmv_t, rope_base, rope_res, rope_s, wts):
    tq = PROMPT_TILE
    sb = STEP_SEQS
    n = x_p.shape[0]
    nb = x_s.shape[0]
    assert n // tq == nb // sb
    row_spec = lambda w: pl.BlockSpec((tq, w), lambda i: (i, 0))
    seq3 = lambda a, b_: pl.BlockSpec((sb, a, b_), lambda i: (i, 0, 0))
    pool_s_spec = pl.BlockSpec((POOL_PAD, 2 * sb, POOL_W), lambda i: (0, i // 2, 0))
    cache_mem_spec = pl.BlockSpec((DEPTH, sb, MEM_W, N_MEM), lambda i: (0, i, 0, 0))
    in_specs = [pl.BlockSpec(memory_space=pltpu.SMEM), row_spec(D_MODEL),
                seq3(DEC_SEQ, D_MODEL), pool_s_spec, seq3(KV_W, WINDOW), seq3(KV_W, WINDOW),
                cache_mem_spec, cache_mem_spec,
                _const_spec(rope_base.shape), _const_spec(rope_res.shape), _const_spec(rope_s.shape)]
    in_specs += [_const_spec(w.shape) for w in wts]
    out_shape = [jax.ShapeDtypeStruct((n, D_MODEL), F32),
                 jax.ShapeDtypeStruct((POOL_PAD, POOL_W), F32),
                 jax.ShapeDtypeStruct((KV_W, WINDOW), F32),
                 jax.ShapeDtypeStruct((KV_W, WINDOW), F32),
                 jax.ShapeDtypeStruct((nb, DEC_SEQ, D_MODEL), F32),
                 jax.ShapeDtypeStruct((POOL_PAD, nb, POOL_W), F32),
                 jax.ShapeDtypeStruct((nb, KV_W, WINDOW), F32),
                 jax.ShapeDtypeStruct((nb, KV_W, WINDOW), F32)]
    out_specs = [row_spec(D_MODEL),
                 pl.BlockSpec((POOL_PAD, POOL_W), lambda i: (0, 0)),
                 pl.BlockSpec((KV_W, WINDOW), lambda i: (0, 0)),
                 pl.BlockSpec((KV_W, WINDOW), lambda i: (0, 0)),
                 seq3(DEC_SEQ, D_MODEL), pool_s_spec, seq3(KV_W, WINDOW), seq3(KV_W, WINDOW)]
    return pl.pallas_call(
        _step_kernel,
        grid=(n // tq,),
        in_specs=in_specs,
        out_specs=out_specs,
        out_shape=out_shape,
        scratch_shapes=[pltpu.VMEM((2, CARRY_ROWS, POOL_W), F32),
                        pltpu.VMEM((2, WINDOW, KV_W), BF16),
                        pltpu.VMEM((2, WINDOW, KV_W), BF16),
                        pltpu.VMEM((POOL_W // LANES, sb * DEC_SEQ, LANES), F32),
                        pltpu.VMEM((POOL_W // LANES, sb * DEC_SEQ, LANES), F32)],
        compiler_params=pltpu.CompilerParams(dimension_semantics=("arbitrary",),
                                             vmem_limit_bytes=STEP_VMEM_LIMIT),
        name="step",
    )(sinks, x_p, x_s, pref, ck, cv, cmk_t, cmv_t, rope_base, rope_res, rope_s, *wts)


PREP_ROWS = N_KV_HEADS * HEAD_DIM
PREP_VMEM_LIMIT = 40 * 1024 * 1024


def _cols_group_major(w):
    heads = [w[:, (kvh * GROUP + g) * HEAD_DIM:(kvh * GROUP + g + 1) * HEAD_DIM]
             for g in range(GROUP) for kvh in range(N_KV_HEADS)]
    return jnp.concatenate(heads, axis=1)


def _prep_kernel(w_in_a_ref, w_out_a_ref, w_kv_ref, w_in_b_ref, wob_h0_ref, wob_h1_ref, wob_h2_ref, wob_h3_ref,
                 wob_m_ref, w_mem_ref, mix_ref,
                 norm_a_ref, pool_scale_ref, kv_norm_ref, k_norm_ref, norm_b_ref, q_norm_ref, mem_q_norm_ref,
                 mem_ref, mem_norm_ref, mem_kgain_ref,
                 w_ref, wo_b_ref, wbd_ref, vec_ref, mk_ref, mv_ref, mkvb_ref,
                 mem_xn, mem_acc):
    i = pl.program_id(0)

    def put(name, value):
        start, width = _W_OFFSET[name]
        w_ref[:, start:start + width] = value.astype(BF16)

    @pl.when(i == 0)
    def _():
        n_grp = len(POOL_WINDOWS)
        for g in range(n_grp):
            pieces = [jnp.zeros((POOL_GW, POOL_GW), F32)] * n_grp
            pieces[g] = mix_ref[g]
            wbd_ref[g * POOL_GW:(g + 1) * POOL_GW, :] = jnp.concatenate(pieces, axis=1).astype(BF16)

        def per_head(g_row, width):
            return jnp.concatenate([g_row] * (width // HEAD_DIM), axis=1)

        vec_pieces = {"norm_a": norm_a_ref[...], "pool_scale": pool_scale_ref[...], "kv_norm": kv_norm_ref[...],
                      "k_gain": per_head(k_norm_ref[...], KV_W), "norm_b": norm_b_ref[...],
                      "q_gain": per_head(q_norm_ref[...], SWA_W),
                      "mem_q_gain0": per_head(mem_q_norm_ref[0:1, :], MEM_W),
                      "mem_q_gain1": per_head(mem_q_norm_ref[1:2, :], MEM_W)}
        for name, (start, width) in _VEC_OFFSET.items():
            vec_ref[:, start:start + width] = vec_pieces[name]

        mem_unit = _rms_unit(mem_ref[...])
        for l in range(DEPTH):
            mem_xn[l] = (mem_unit * mem_norm_ref[l:l + 1, :]).astype(BF16)
        mem_acc[...] = jnp.zeros_like(mem_acc)

    put("in_a", w_in_a_ref[...])
    put("out_a", w_out_a_ref[...])
    put("kv", w_kv_ref[...])
    w_in_b = w_in_b_ref[...]
    put("q", _cols_group_major(w_in_b[:, :SWA_W]))
    put("q_gate", _cols_group_major(w_in_b[:, SWA_W:2 * SWA_W]))
    put("mem_b", w_in_b[:, 2 * SWA_W:])
    heads = jnp.concatenate([wob_h0_ref[0], wob_h1_ref[0], wob_h2_ref[0], wob_h3_ref[0]], axis=0)
    wo_b_ref[...] = jnp.where(i < GROUP, heads, wob_m_ref[...]).astype(BF16)
    n_slabs = D_MODEL // PREP_ROWS
    for j in range(n_slabs):
        @pl.when(i == j)
        def _(j=j):
            for l in range(DEPTH):
                mem_acc[l] += _dot(mem_xn[l, :, j * PREP_ROWS:(j + 1) * PREP_ROWS], w_mem_ref[l].astype(BF16))

    @pl.when(i == n_slabs - 1)
    def _():
        for l in range(DEPTH):
            kv = mem_acc[l]
            k_t = kv[:, :MEM_W].T
            v_t = kv[:, MEM_W:].T
            k3 = k_t.reshape(MEM_HEADS, HEAD_DIM, N_MEM)
            ms = jnp.mean(k3 * k3, axis=1, keepdims=True)
            k_t = (k3 * lax.rsqrt(ms + EPS)).reshape(MEM_W, N_MEM) * mem_kgain_ref[l]
            mk_ref[l] = k_t
            mv_ref[l] = v_t
            mkvb_ref[l, 0] = k_t.astype(BF16)
            mkvb_ref[l, 1] = v_t.astype(BF16)


def _prep_call(w_in_a, w_out_a, w_kv, w_in_b, w_out_b, w_mem_kv, pool_mix, small, mem, mem_norm, mem_kgain_col):
    rows = lambda w: pl.BlockSpec((PREP_ROWS, w), lambda i: (i, 0))

    def head_src(kvh):
        return pl.BlockSpec((1, HEAD_DIM, D_MODEL),
                            lambda i: (kvh * GROUP + jnp.minimum(i, GROUP - 1), 0, 0))

    in_specs = [rows(2 * POOL_W + 2 * MEM_W), rows(D_MODEL), rows(2 * KV_W), rows(2 * SWA_W + 2 * MEM_W)]
    in_specs += [head_src(kvh) for kvh in range(N_KV_HEADS)]
    in_specs += [pl.BlockSpec((PREP_ROWS, D_MODEL), lambda i: (SWA_W // PREP_ROWS, 0)),
                pl.BlockSpec((DEPTH, PREP_ROWS, 2 * MEM_W), lambda i: (0, i, 0)),
                pl.BlockSpec(pool_mix.shape, lambda i: (0, 0, 0))]
    in_specs += [pl.BlockSpec(a.shape, lambda i: (0, 0)) for a in small]
    in_specs += [pl.BlockSpec(mem.shape, lambda i: (0, 0)), pl.BlockSpec(mem_norm.shape, lambda i: (0, 0)),
                 pl.BlockSpec(mem_kgain_col.shape, lambda i: (0, 0, 0))]
    n_vec = sum(w for _, w in _VEC_WIDTHS)
    mem_f = jax.ShapeDtypeStruct((DEPTH, MEM_W, N_MEM), F32)
    out_specs = [rows(W_ALL_COLS), rows(D_MODEL),
                 pl.BlockSpec((POOL_W, POOL_W), lambda i: (0, 0)),
                 pl.BlockSpec((1, n_vec), lambda i: (0, 0)),
                 pl.BlockSpec(mem_f.shape, lambda i: (0, 0, 0)), pl.BlockSpec(mem_f.shape, lambda i: (0, 0, 0)),
                 pl.BlockSpec((DEPTH, 2, MEM_W, N_MEM), lambda i: (0, 0, 0, 0))]
    bf = lambda *shape: jax.ShapeDtypeStruct(shape, BF16)
    out_shape = [bf(D_MODEL, W_ALL_COLS), bf(D_MODEL, D_MODEL), bf(POOL_W, POOL_W),
                 jax.ShapeDtypeStruct((1, n_vec), F32), mem_f, mem_f, bf(DEPTH, 2, MEM_W, N_MEM)]
    w_out_b_heads = w_out_b.reshape(D_MODEL // HEAD_DIM, HEAD_DIM, D_MODEL)
    return pl.pallas_call(
        _prep_kernel,
        grid=(D_MODEL // PREP_ROWS,),
        in_specs=in_specs,
        out_specs=out_specs,
        out_shape=out_shape,
        scratch_shapes=[pltpu.VMEM((DEPTH, N_MEM, D_MODEL), BF16), pltpu.VMEM((DEPTH, N_MEM, 2 * MEM_W), F32)],
        compiler_params=pltpu.CompilerParams(dimension_semantics=("arbitrary",),
                                             vmem_limit_bytes=PREP_VMEM_LIMIT),
        name="prep",
    )(w_in_a, w_out_a, w_kv, w_in_b, *([w_out_b_heads] * N_KV_HEADS), w_out_b, w_mem_kv, pool_mix, *small,
      mem, mem_norm, mem_kgain_col)


def _rope_lane_tables(pos):
    half = ROT_DIM // 2
    inv = (ROPE_THETA ** (-np.arange(half, dtype=np.float32) * 2.0 / ROT_DIM)).astype(np.float32)
    ang = (np.asarray(pos, np.float32)[:, None] * inv[None, :]).astype(np.float64)
    cos, sin = np.cos(ang), np.sin(ang)
    t = ang.shape[0]
    rest = HEAD_DIM - ROT_DIM
    cos64 = np.concatenate([cos, cos, np.ones((t, rest))], axis=1)
    sin64 = np.concatenate([-sin, sin, np.zeros((t, rest))], axis=1)
    reps = LANES // HEAD_DIM
    return np.tile(cos64, (1, reps)).astype(np.float32), np.tile(sin64, (1, reps)).astype(np.float32)


def kernel(x_prompt, x_sample, state_pool, cache_swa_k, cache_swa_v, cache_mem_k, cache_mem_v, mem_prompt,
           norm_a, w_in_a, pool_mix_w, pool_scale, w_out_a, kv_norm, w_kv, k_norm,
           norm_b, w_in_b, q_norm, sinks, w_out_b, mem_norm, w_mem_kv, mem_q_norm, mem_k_norm):
    seg = np.arange(MXU_DIM) // HEAD_DIM
    bseg = jnp.asarray((seg[:, None] == seg[None, :]).astype(np.float32) / HEAD_DIM, BF16)
    wcol = jnp.asarray(np.repeat(np.asarray(POOL_WINDOWS, np.float32), POOL_GW).reshape(1, POOL_W))

    small = (norm_a, pool_scale, kv_norm.reshape(1, D_MODEL), k_norm.reshape(1, HEAD_DIM), norm_b, q_norm, mem_q_norm)
    mkg_col = jnp.stack([jnp.tile(mem_k_norm[l].astype(F32), MEM_HEADS).reshape(MEM_W, 1) for l in range(DEPTH)])
    w_all, wo_b, wbd, vecs, mk_t, mv_t, mkv_b = _prep_call(
        w_in_a[0], w_out_a[0], w_kv, w_in_b[0], w_out_b[0], w_mem_kv, pool_mix_w[0], small,
        mem_prompt[0], mem_norm, mkg_col)
    sinks_p = sinks[0].astype(F32).reshape(N_KV_HEADS, GROUP).T.reshape(N_Q_HEADS) * LOG2E
    wts = (vecs, w_all, wo_b, wbd, wcol, mkv_b, bseg)

    n_tiles = SEQ // PROMPT_TILE
    cb, sb_ = _rope_lane_tables(np.arange(n_tiles) * PROMPT_TILE)
    cr, sr = _rope_lane_tables(np.arange(PROMPT_TILE))
    rope_base = jnp.asarray(np.stack([cb, sb_], axis=1))
    rope_res = jnp.asarray(np.stack([cr, sr], axis=0))
    cos_s, sin_s = _rope_lane_tables(PAST_LEN + np.arange(DEC_SEQ))
    rope_s = jnp.asarray(np.stack([np.tile(cos_s, (STEP_SEQS, 1)), np.tile(sin_s, (STEP_SEQS, 1))], axis=0))
    pref = jnp.transpose(state_pool[0], (1, 0, 2))
    cmk_t = jnp.transpose(cache_mem_k, (0, 1, 3, 4, 2)).reshape(DEPTH, DEC_BATCH, MEM_W, N_MEM)
    cmv_t = jnp.transpose(cache_mem_v, (0, 1, 3, 4, 2)).reshape(DEPTH, DEC_BATCH, MEM_W, N_MEM)
    ck_t = jnp.transpose(cache_swa_k, (0, 2, 3, 1)).reshape(DEC_BATCH, KV_W, WINDOW)
    cv_t = jnp.transpose(cache_swa_v, (0, 2, 3, 1)).reshape(DEC_BATCH, KV_W, WINDOW)
    y_p, pool_p, k_p, v_p, y_s, pool_s, k_s, v_s = _step_call(
        sinks_p, x_prompt[0], x_sample, pref, ck_t, cv_t, cmk_t, cmv_t, rope_base, rope_res, rope_s, wts)

    def mem_out(a):
        return jnp.transpose(a.reshape(DEPTH, 1, MEM_HEADS, HEAD_DIM, N_MEM), (0, 1, 4, 2, 3))

    def swa_out_t(a):
        return jnp.transpose(a.reshape(-1, N_KV_HEADS, HEAD_DIM, WINDOW), (0, 3, 1, 2))

    return (y_p[None], y_s, pool_p[None, None], jnp.transpose(pool_s, (1, 0, 2))[None],
            swa_out_t(k_p[None]), swa_out_t(v_p[None]), swa_out_t(k_s), swa_out_t(v_s),
            mem_out(mk_t), mem_out(mv_t))
```

```python
import jax
import jax.numpy as jnp
import numpy as np
from jax import lax
from jax.experimental import pallas as pl
from jax.experimental.pallas import tpu as pltpu

D_MODEL = 1024
SEQ = 16384
DEPTH = 2
DEC_BATCH = 128
DEC_SEQ = 8
PAST_LEN = 16384
HEAD_DIM = 64
POOL_W = 768
POOL_WINDOWS = (2, 4, 8, 16)
POOL_GW = 192
POOL_PAD = 15
N_Q_HEADS = 12
N_KV_HEADS = 4
GROUP = 3
SWA_W = 768
KV_W = 256
WINDOW = 128
N_MEM = 256
MEM_HEADS = 4
MEM_W = 256
ROT_DIM = 16
ROPE_THETA = 500000.0
EPS = 1e-6

F32 = jnp.float32
BF16 = jnp.bfloat16
NEG_INF = float("-inf")
LOG2E = 1.4426950408889634
Q_SCALE = HEAD_DIM ** -0.5 * LOG2E

LANES = 128
SUBLANES = 8
MXU_DIM = 256
HEADS_PER_CHUNK = MXU_DIM // HEAD_DIM
CARRY_ROWS = 16
PROMPT_TILE = 512
STEP_SEQS = DEC_BATCH // (SEQ // PROMPT_TILE)
STEP_VMEM_LIMIT = 60 * 1024 * 1024


_VEC_WIDTHS = (("norm_a", D_MODEL), ("pool_scale", POOL_W), ("kv_norm", D_MODEL), ("k_gain", KV_W),
               ("norm_b", D_MODEL), ("q_gain", SWA_W), ("mem_q_gain0", MEM_W), ("mem_q_gain1", MEM_W))
_VEC_OFFSET = {}
for _name, _width in _VEC_WIDTHS:
    _VEC_OFFSET[_name] = (sum(w for _, w in _VEC_WIDTHS[:len(_VEC_OFFSET)]), _width)


def _vec(vec_ref, name):
    start, width = _VEC_OFFSET[name]
    return vec_ref[:, start:start + width]


_W_WIDTHS = (("in_a", 2 * POOL_W + 2 * MEM_W), ("out_a", D_MODEL), ("kv", 2 * KV_W),
             ("q", SWA_W), ("q_gate", SWA_W), ("mem_b", 2 * MEM_W), ("out_b", D_MODEL))
_W_OFFSET = {}
for _name, _width in _W_WIDTHS:
    _W_OFFSET[_name] = (sum(w for _, w in _W_WIDTHS[:len(_W_OFFSET)]), _width)
W_ALL_COLS = sum(w for _, w in _W_WIDTHS)


def _wcols(w_ref, name, lo=0, hi=None):
    start, width = _W_OFFSET[name]
    return w_ref[:, start + lo:start + (width if hi is None else hi)]


def _dot(a, b):
    return jnp.dot(a, b, preferred_element_type=F32)


def _dot_nt(a, b):
    return lax.dot_general(a, b, (((1,), (1,)), ((), ())), preferred_element_type=F32)


def _rms_unit(x):
    return x * lax.rsqrt(jnp.mean(x * x, axis=-1, keepdims=True) + EPS)


def _silu(g):
    return g / (1.0 + jnp.exp(-g))


def _head_rms(y, gain, bseg):
    parts = []
    for c in range(y.shape[1] // MXU_DIM):
        yc = y[:, c * MXU_DIM:(c + 1) * MXU_DIM]
        ms = _dot((yc * yc).astype(BF16), bseg)
        parts.append(yc * lax.rsqrt(ms + EPS) * gain[:, c * MXU_DIM:(c + 1) * MXU_DIM])
    return parts[0] if len(parts) == 1 else jnp.concatenate(parts, axis=1)


def _rope(y, cos_t, sin_t, first8):
    parts = []
    for c in range(y.shape[1] // LANES):
        yc = y[:, c * LANES:(c + 1) * LANES]
        partner = jnp.where(first8, pltpu.roll(yc, LANES - 8, 1), pltpu.roll(yc, 8, 1))
        parts.append(yc * cos_t + partner * sin_t)
    return parts[0] if len(parts) == 1 else jnp.concatenate(parts, axis=1)


def _pool_window_sums(u_hist):
    def back(a, k):
        return pltpu.roll(a, k, 0)

    lane = lax.broadcasted_iota(jnp.int32, (1, LANES), 1)
    s2 = u_hist + back(u_hist, 1)
    t = s2[:, LANES:]
    s4 = t + back(t, 2)
    t = s4[:, 2 * LANES:]
    s8 = t + back(t, 4)
    t = s8[:, LANES:]
    s16 = t + back(t, 8)
    tiles = [
        s2[:, :LANES],
        jnp.where(lane < 64, s2[:, LANES:2 * LANES], s4[:, :LANES]),
        s4[:, LANES:2 * LANES],
        s8[:, :LANES],
        jnp.where(lane < 64, s8[:, LANES:2 * LANES], s16[:, :LANES]),
        s16[:, LANES:],
    ]
    return jnp.concatenate(tiles, axis=1)


def _pool_window_sums_planes(planes):
    n = len(planes)

    def doubled(prev, k, first):
        return [prev[j] + prev[j - k] if j >= first else None for j in range(n)]

    lane = lax.broadcasted_iota(jnp.int32, (1, LANES), 1)
    s2 = doubled(planes, 1, 1)
    s4 = doubled([None if a is None else a[:, LANES:] for a in s2], 2, 3)
    s8 = doubled([None if a is None else a[:, 2 * LANES:] for a in s4], 4, 7)
    s16 = doubled([None if a is None else a[:, LANES:] for a in s8], 8, POOL_PAD)
    out = []
    for j in range(POOL_PAD, n):
        out.append(jnp.concatenate([
            s2[j][:, :LANES],
            jnp.where(lane < 64, s2[j][:, LANES:2 * LANES], s4[j][:, :LANES]),
            s4[j][:, LANES:2 * LANES],
            s8[j][:, :LANES],
            jnp.where(lane < 64, s8[j][:, LANES:2 * LANES], s16[j][:, :LANES]),
            s16[j][:, LANES:],
        ], axis=1))
    return out


def _pool_mix(d, wbd_ref):
    lo, hi = MXU_DIM, 2 * MXU_DIM
    return jnp.concatenate([
        _dot(d[:, :hi], wbd_ref[:hi, :lo]),
        _dot(d, wbd_ref[:, lo:hi]),
        _dot(d[:, lo:], wbd_ref[lo:, hi:]),
    ], axis=1)


def _head_masks():
    lane = lax.broadcasted_iota(jnp.int32, (1, MXU_DIM), 1)
    return [(lane // HEAD_DIM) == j for j in range(HEADS_PER_CHUNK)]


def _first8_mask():
    lane = lax.broadcasted_iota(jnp.int32, (1, LANES), 1)
    return (lane % HEAD_DIM) < (ROT_DIM // 2)


def _stack_heads(q, hmask):
    return jnp.concatenate([jnp.where(m, q, 0.0) for m in hmask], axis=0).astype(BF16)


def _unstack_heads(o, hmask):
    n = len(hmask)
    m = o.shape[0] // n
    acc = o[(n - 1) * m:, :]
    for h in range(n - 2, -1, -1):
        acc = jnp.where(hmask[h], o[h * m:(h + 1) * m, :], acc)
    return acc


def _sink_columns(sinks_ref, rows_per_head):
    hrow = lax.broadcasted_iota(jnp.int32, (HEADS_PER_CHUNK * rows_per_head, 1), 0) // rows_per_head
    cols = []
    for c in range(SWA_W // MXU_DIM):
        col = jnp.zeros((HEADS_PER_CHUNK * rows_per_head, 1), F32)
        for jj in range(HEADS_PER_CHUNK):
            col = jnp.where(hrow == jj, sinks_ref[c * HEADS_PER_CHUNK + jj], col)
        cols.append(col)
    return cols


def _softmax_weights(s, sink):
    m = jnp.max(s, axis=-1, keepdims=True)
    if sink is not None:
        m = jnp.maximum(m, sink)
    p = jnp.exp2(s - m)
    den = jnp.sum(p, axis=-1, keepdims=True)
    if sink is not None:
        den = den + jnp.exp2(sink - m)
    return p.astype(BF16), 1.0 / den


def _mem_attn(q, k_t, v_t, hmask):
    p, rden = _softmax_weights(_dot(_stack_heads(q, hmask), k_t), None)
    return _unstack_heads(_dot_nt(p, v_t) * rden, hmask)


def _swa_block(qb, kc, vc, bias, sink_cols, hmask):
    outs = []
    for g in range(GROUP):
        s = _dot_nt(_stack_heads(qb[:, g * MXU_DIM:(g + 1) * MXU_DIM], hmask), kc)
        s = (s.reshape(HEADS_PER_CHUNK, WINDOW, 2 * WINDOW) + bias[None]).reshape(HEADS_PER_CHUNK * WINDOW, 2 * WINDOW)
        p, rden = _softmax_weights(s, sink_cols[g])
        outs.append(_unstack_heads(_dot(p, vc) * rden, hmask))
    return jnp.concatenate(outs, axis=1)


def _const_spec(shape):
    nd = len(shape)
    return pl.BlockSpec(shape, lambda i: (0,) * nd, pipeline_mode=pl.Buffered(1))


def _stack_heads_seq(q, n_seq, hmask):
    q3 = q.reshape(n_seq, DEC_SEQ, MXU_DIM)
    return jnp.stack([jnp.where(m, q3, 0.0) for m in hmask], axis=1)


def _unstack_heads_seq(o, n_seq, hmask):
    o4 = o.reshape(n_seq, HEADS_PER_CHUNK, DEC_SEQ, MXU_DIM)
    acc = o4[:, HEADS_PER_CHUNK - 1]
    for h in range(HEADS_PER_CHUNK - 2, -1, -1):
        acc = jnp.where(hmask[h], o4[:, h], acc)
    return acc.reshape(n_seq * DEC_SEQ, MXU_DIM)


def _mem_attn_seqs(q, k_ref, v_ref, layer, n_seq, hmask):
    rows = HEADS_PER_CHUNK * DEC_SEQ
    qs = _stack_heads_seq(q, n_seq, hmask).reshape(n_seq * rows, MXU_DIM).astype(BF16)
    s = jnp.concatenate([_dot(qs[b * rows:(b + 1) * rows], k_ref[layer, b].astype(BF16))
                         for b in range(n_seq)], axis=0)
    p, rden = _softmax_weights(s, None)
    o = jnp.concatenate([_dot_nt(p[b * rows:(b + 1) * rows], v_ref[layer, b].astype(BF16))
                         for b in range(n_seq)], axis=0)
    return _unstack_heads_seq(o * rden, n_seq, hmask)


def _sample_pool(u_s, pref_ref, pool_ref, wcol, parity, u_scr, d_scr):
    sb = STEP_SEQS
    n_lt = POOL_W // LANES
    for c in range(n_lt):
        u_scr[c] = u_s[:, c * LANES:(c + 1) * LANES]
    planes = [jnp.where(parity == 0, pref_ref[t, 0:sb, :], pref_ref[t, sb:2 * sb, :]) for t in range(POOL_PAD)]
    planes += [jnp.concatenate([u_scr[c, pl.ds(t, sb, stride=DEC_SEQ), :] for c in range(n_lt)], axis=1)
               for t in range(DEC_SEQ)]
    for t, win in enumerate(_pool_window_sums_planes(planes)):
        cnt = jnp.minimum(float(PAST_LEN + t + 1), wcol)
        d_t = win / cnt - planes[POOL_PAD + t]
        for c in range(n_lt):
            d_scr[c, pl.ds(t, sb, stride=DEC_SEQ), :] = d_t[:, c * LANES:(c + 1) * LANES]

    @pl.when(parity == 0)
    def _():
        for t in range(POOL_PAD):
            pool_ref[t, 0:sb, :] = planes[DEC_SEQ + t]

    @pl.when(parity == 1)
    def _():
        for t in range(POOL_PAD):
            pool_ref[t, sb:2 * sb, :] = planes[DEC_SEQ + t]

    return jnp.concatenate([d_scr[c] for c in range(n_lt)], axis=1)


def _sample_cache_update(k_s, v_s, ck_ref, cv_ref, ko_ref, vo_ref):
    sb = STEP_SEQS
    keep = WINDOW - DEC_SEQ
    key_lane = lax.broadcasted_iota(jnp.int32, (1, WINDOW), 1)
    pad_rows = jnp.zeros((WINDOW - sb * DEC_SEQ, KV_W), F32)
    for new_rows, c_ref, o_ref in ((k_s, ck_ref, ko_ref), (v_s, cv_ref, vo_ref)):
        new_t = jnp.concatenate([new_rows, pad_rows], axis=0).T
        for b in range(sb):
            shifted = pltpu.roll(c_ref[b], keep, 1)
            placed = pltpu.roll(new_t, (keep - b * DEC_SEQ) % WINDOW, 1)
            o_ref[b] = jnp.where(key_lane < keep, shifted, placed)


def _sample_window_attn(q_s, k_s, v_s, ck_ref, cv_ref, sinks_ref, hmask):
    sb = STEP_SEQS
    m = sb * DEC_SEQ
    rows = N_Q_HEADS * DEC_SEQ
    qs = jnp.stack([_stack_heads_seq(q_s[:, g * MXU_DIM:(g + 1) * MXU_DIM], sb, hmask) for g in range(GROUP)],
                   axis=1)
    qs = qs.reshape(sb * rows, KV_W).astype(BF16)
    s_old = jnp.concatenate([_dot(qs[b * rows:(b + 1) * rows], ck_ref[b].astype(BF16)) for b in range(sb)], axis=0)
    s_new = _dot_nt(qs, k_s.astype(BF16))
    tq_old = lax.broadcasted_iota(jnp.int32, (rows, WINDOW), 0) % DEC_SEQ
    key_old = lax.broadcasted_iota(jnp.int32, (rows, WINDOW), 1)
    bias_old = jnp.where(key_old > tq_old, 0.0, NEG_INF)
    row_i = lax.broadcasted_iota(jnp.int32, (sb * rows, m), 0)
    col_i = lax.broadcasted_iota(jnp.int32, (sb * rows, m), 1)
    same_seq = (row_i // rows) == (col_i // DEC_SEQ)
    bias_new = jnp.where(same_seq, jnp.where(col_i % DEC_SEQ <= row_i % DEC_SEQ, 0.0, NEG_INF), NEG_INF)
    s_old = (s_old.reshape(sb, rows, WINDOW) + bias_old[None]).reshape(sb * rows, WINDOW)
    s_new = s_new + bias_new
    sink = jnp.concatenate(_sink_columns(sinks_ref, DEC_SEQ), axis=0)
    sink = jnp.concatenate([sink] * sb, axis=0)
    mx = jnp.maximum(jnp.maximum(jnp.max(s_old, axis=-1, keepdims=True), jnp.max(s_new, axis=-1, keepdims=True)), sink)
    p_old = jnp.exp2(s_old - mx)
    p_new = jnp.exp2(s_new - mx)
    den = (jnp.sum(p_old, axis=-1, keepdims=True) + jnp.sum(p_new, axis=-1, keepdims=True) + jnp.exp2(sink - mx))
    p_old = p_old.astype(BF16)
    o = jnp.concatenate([_dot_nt(p_old[b * rows:(b + 1) * rows], cv_ref[b].astype(BF16)) for b in range(sb)], axis=0)
    o = (o + _dot(p_new.astype(BF16), v_s.astype(BF16))) * (1.0 / den)
    o5 = o.reshape(sb, GROUP, HEADS_PER_CHUNK * DEC_SEQ, KV_W)
    return jnp.concatenate([_unstack_heads_seq(o5[:, g].reshape(sb * HEADS_PER_CHUNK * DEC_SEQ, KV_W), sb, hmask)
                            for g in range(GROUP)], axis=1)


def _step_kernel(sinks_ref, xp_ref, xs_ref, pref_ref, ck_ref, cv_ref, cmk_ref, cmv_ref,
                 rope_base_ref, rope_res_ref, rope_s_ref,
                 vec_ref, w_ref, wbd_ref, wcol_ref, mkv_ref, bseg_ref,
                 yp_ref, pool_p_ref, kop_ref, vop_ref, ys_ref, pool_s_ref, kos_ref, vos_ref,
                 ucarry, kprev, vprev, u_scr, d_scr):
    tq = PROMPT_TILE
    sb = STEP_SEQS
    m = sb * DEC_SEQ
    i = pl.program_id(0)

    @pl.when(i == 0)
    def _():
        ucarry[0] = jnp.zeros((CARRY_ROWS, POOL_W), F32)
        kprev[0] = jnp.zeros((WINDOW, KV_W), BF16)
        vprev[0] = jnp.zeros((WINDOW, KV_W), BF16)

    rd = i % 2
    wr = (i + 1) % 2

    hmask = _head_masks()
    first8 = _first8_mask()
    bseg = bseg_ref[...]
    x = jnp.concatenate([xp_ref[...], xs_ref[...].reshape(m, D_MODEL)], axis=0)

    base = rope_base_ref[i]
    cb, sb_ = base[0:1, :], base[1:2, :]
    cr, sr = rope_res_ref[0], rope_res_ref[1]
    cos_t = jnp.concatenate([cb * cr - sb_ * sr, rope_s_ref[0]], axis=0)
    sin_t = jnp.concatenate([sb_ * cr + cb * sr, rope_s_ref[1]], axis=0)

    xn = (_rms_unit(x) * _vec(vec_ref, "norm_a")).astype(BF16)
    u = _dot(xn, _wcols(w_ref, "in_a", 0, POOL_W))
    gp = _dot(xn, _wcols(w_ref, "in_a", POOL_W, 2 * POOL_W))
    qm = _dot(xn, _wcols(w_ref, "in_a", 2 * POOL_W, 2 * POOL_W + MEM_W))
    gm = _dot(xn, _wcols(w_ref, "in_a", 2 * POOL_W + MEM_W))
    u_p = u[:tq]
    u_hist = jnp.concatenate([ucarry[rd], u_p], axis=0)
    win = _pool_window_sums(u_hist)[CARRY_ROWS:]
    pos = i * tq + lax.broadcasted_iota(jnp.int32, (tq, 1), 0)
    cnt = jnp.minimum((pos + 1).astype(F32), wcol_ref[...])
    d_p = win / cnt - u_p
    d_s = _sample_pool(u[tq:], pref_ref, pool_s_ref, wcol_ref[...], rd, u_scr, d_scr)
    d = jnp.concatenate([d_p, d_s], axis=0)
    yp = _pool_mix(d.astype(BF16), wbd_ref) * _vec(vec_ref, "pool_scale") * _silu(gp)
    qmn = _head_rms(qm, _vec(vec_ref, "mem_q_gain0"), bseg) * Q_SCALE
    ym = jnp.concatenate([_mem_attn(qmn[:tq], mkv_ref[0, 0], mkv_ref[0, 1], hmask),
                          _mem_attn_seqs(qmn[tq:], cmk_ref, cmv_ref, 0, sb, hmask)], axis=0) * _silu(gm)
    x1 = x + _dot(jnp.concatenate([yp, ym], axis=1).astype(BF16), _wcols(w_ref, "out_a"))
    ucarry[wr] = u_p[tq - CARRY_ROWS:, :]

    r = _rms_unit(x1)
    kv = _dot((r * _vec(vec_ref, "kv_norm")).astype(BF16), _wcols(w_ref, "kv"))
    k = _rope(_head_rms(kv[:, :KV_W], _vec(vec_ref, "k_gain"), bseg), cos_t, sin_t, first8)
    v = kv[:, KV_W:]
    k_p, v_p, k_s, v_s = k[:tq], v[:tq], k[tq:], v[tq:]
    k_all = jnp.concatenate([kprev[rd], k_p.astype(BF16)], axis=0)
    v_all = jnp.concatenate([vprev[rd], v_p.astype(BF16)], axis=0)
    _sample_cache_update(k_s, v_s, ck_ref, cv_ref, kos_ref, vos_ref)

    xb = (r * _vec(vec_ref, "norm_b")).astype(BF16)
    zq = _dot(xb, _wcols(w_ref, "q"))
    gq = _dot(xb, _wcols(w_ref, "q_gate"))
    qm2 = _dot(xb, _wcols(w_ref, "mem_b", 0, MEM_W))
    gm2 = _dot(xb, _wcols(w_ref, "mem_b", MEM_W))
    q = _rope(_head_rms(zq, _vec(vec_ref, "q_gain"), bseg), cos_t, sin_t, first8) * Q_SCALE

    qi = lax.broadcasted_iota(jnp.int32, (WINDOW, 2 * WINDOW), 0)
    ci = lax.broadcasted_iota(jnp.int32, (WINDOW, 2 * WINDOW), 1)
    band_bias = jnp.where(ci > qi, jnp.where(ci <= qi + WINDOW, 0.0, NEG_INF), NEG_INF)
    sink_cols = _sink_columns(sinks_ref, WINDOW)
    ys_blocks = []
    for b in range(tq // WINDOW):
        bias = band_bias
        if b == 0:
            key_pos = ci + (i * tq - WINDOW)
            bias = jnp.where(key_pos >= 0, band_bias, NEG_INF)
        qb = q[b * WINDOW:(b + 1) * WINDOW, :]
        kc = k_all[b * WINDOW:(b + 2) * WINDOW, :]
        vc = v_all[b * WINDOW:(b + 2) * WINDOW, :]
        ys_blocks.append(_swa_block(qb, kc, vc, bias, sink_cols, hmask))
    ys_blocks.append(_sample_window_attn(q[tq:], k_s, v_s, ck_ref, cv_ref, sinks_ref, hmask))
    ys = jnp.concatenate(ys_blocks, axis=0) * _silu(gq)
    qmn2 = _head_rms(qm2, _vec(vec_ref, "mem_q_gain1"), bseg) * Q_SCALE
    ym2 = jnp.concatenate([_mem_attn(qmn2[:tq], mkv_ref[1, 0], mkv_ref[1, 1], hmask),
                           _mem_attn_seqs(qmn2[tq:], cmk_ref, cmv_ref, 1, sb, hmask)], axis=0) * _silu(gm2)
    ob, ob_w = _W_OFFSET["out_b"]
    y = (x1 + _dot(ys.astype(BF16), w_ref[0:SWA_W, ob:ob + ob_w])
         + _dot(ym2.astype(BF16), w_ref[SWA_W:, ob:ob + ob_w]))
    yp_ref[...] = y[:tq]
    ys_ref[...] = y[tq:].reshape(sb, DEC_SEQ, D_MODEL)

    kprev[wr] = k_all[tq:, :]
    vprev[wr] = v_all[tq:, :]

    @pl.when(i == pl.num_programs(0) - 1)
    def _():
        kop_ref[...] = k_p[tq - WINDOW:, :].T
        vop_ref[...] = v_p[tq - WINDOW:, :].T
        pool_p_ref[...] = pltpu.roll(u_p[tq - CARRY_ROWS:, :], CARRY_ROWS - 1, 0)[0:POOL_PAD, :]


def _step_call(sinks, x_p, x_s, pref, ck, cv, cmk_t, cmv_t, rope_base, rope_res, rope_s, wts):
    tq = PROMPT_TILE
    sb = STEP_SEQS
    n = x_p.shape[0]
    nb = x_s.shape[0]
    assert n // tq == nb // sb
    row_spec = lambda w: pl.BlockSpec((tq, w), lambda i: (i, 0))
    seq3 = lambda a, b_: pl.BlockSpec((sb, a, b_), lambda i: (i, 0, 0))
    pool_s_spec = pl.BlockSpec((POOL_PAD, 2 * sb, POOL_W), lambda i: (0, i // 2, 0))
    cache_mem_spec = pl.BlockSpec((DEPTH, sb, MEM_W, N_MEM), lambda i: (0, i, 0, 0))
    in_specs = [pl.BlockSpec(memory_space=pltpu.SMEM), row_spec(D_MODEL),
                seq3(DEC_SEQ, D_MODEL), pool_s_spec, seq3(KV_W, WINDOW), seq3(KV_W, WINDOW),
                cache_mem_spec, cache_mem_spec,
                _const_spec(rope_base.shape), _const_spec(rope_res.shape), _const_spec(rope_s.shape)]
    in_specs += [_const_spec(w.shape) for w in wts]
    out_shape = [jax.ShapeDtypeStruct((n, D_MODEL), F32),
                 jax.ShapeDtypeStruct((POOL_PAD, POOL_W), F32),
                 jax.ShapeDtypeStruct((KV_W, WINDOW), F32),
                 jax.ShapeDtypeStruct((KV_W, WINDOW), F32),
                 jax.ShapeDtypeStruct((nb, DEC_SEQ, D_MODEL), F32),
                 jax.ShapeDtypeStruct((POOL_PAD, nb, POOL_W), F32),
                 jax.ShapeDtypeStruct((nb, KV_W, WINDOW), F32),
                 jax.ShapeDtypeStruct((nb, KV_W, WINDOW), F32)]
    out_specs = [row_spec(D_MODEL),
                 pl.BlockSpec((POOL_PAD, POOL_W), lambda i: (0, 0)),
                 pl.BlockSpec((KV_W, WINDOW), lambda i: (0, 0)),
                 pl.BlockSpec((KV_W, WINDOW), lambda i: (0, 0)),
                 seq3(DEC_SEQ, D_MODEL), pool_s_spec, seq3(KV_W, WINDOW), seq3(KV_W, WINDOW)]
    return pl.pallas_call(
        _step_kernel,
        grid=(n // tq,),
        in_specs=in_specs,
        out_specs=out_specs,
        out_shape=out_shape,
        scratch_shapes=[pltpu.VMEM((2, CARRY_ROWS, POOL_W), F32),
                        pltpu.VMEM((2, WINDOW, KV_W), BF16),
                        pltpu.VMEM((2, WINDOW, KV_W), BF16),
                        pltpu.VMEM((POOL_W // LANES, sb * DEC_SEQ, LANES), F32),
                        pltpu.VMEM((POOL_W // LANES, sb * DEC_SEQ, LANES), F32)],
        compiler_params=pltpu.CompilerParams(dimension_semantics=("arbitrary",),
                                             vmem_limit_bytes=STEP_VMEM_LIMIT),
        name="step",
    )(sinks, x_p, x_s, pref, ck, cv, cmk_t, cmv_t, rope_base, rope_res, rope_s, *wts)


PREP_ROWS = N_KV_HEADS * HEAD_DIM
PREP_VMEM_LIMIT = 40 * 1024 * 1024


def _cols_group_major(w):
    heads = [w[:, (kvh * GROUP + g) * HEAD_DIM:(kvh * GROUP + g + 1) * HEAD_DIM]
             for g in range(GROUP) for kvh in range(N_KV_HEADS)]
    return jnp.concatenate(heads, axis=1)


def _prep_kernel(w_in_a_ref, w_out_a_ref, w_kv_ref, w_in_b_ref, wob_h0_ref, wob_h1_ref, wob_h2_ref, wob_h3_ref,
                 wob_m_ref, w_mem_ref, mix_ref,
                 norm_a_ref, pool_scale_ref, kv_norm_ref, k_norm_ref, norm_b_ref, q_norm_ref, mem_q_norm_ref,
                 mem_ref, mem_norm_ref, mem_kgain_ref,
                 w_ref, wbd_ref, vec_ref, mk_ref, mv_ref, mkvb_ref,
                 mem_xn, mem_acc):
    i = pl.program_id(0)

    def put(name, value):
        start, width = _W_OFFSET[name]
        w_ref[:, start:start + width] = value.astype(BF16)

    @pl.when(i == 0)
    def _():
        n_grp = len(POOL_WINDOWS)
        for g in range(n_grp):
            pieces = [jnp.zeros((POOL_GW, POOL_GW), F32)] * n_grp
            pieces[g] = mix_ref[g]
            wbd_ref[g * POOL_GW:(g + 1) * POOL_GW, :] = jnp.concatenate(pieces, axis=1).astype(BF16)

        def per_head(g_row, width):
            return jnp.concatenate([g_row] * (width // HEAD_DIM), axis=1)

        vec_pieces = {"norm_a": norm_a_ref[...], "pool_scale": pool_scale_ref[...], "kv_norm": kv_norm_ref[...],
                      "k_gain": per_head(k_norm_ref[...], KV_W), "norm_b": norm_b_ref[...],
                      "q_gain": per_head(q_norm_ref[...], SWA_W),
                      "mem_q_gain0": per_head(mem_q_norm_ref[0:1, :], MEM_W),
                      "mem_q_gain1": per_head(mem_q_norm_ref[1:2, :], MEM_W)}
        for name, (start, width) in _VEC_OFFSET.items():
            vec_ref[:, start:start + width] = vec_pieces[name]

        mem_unit = _rms_unit(mem_ref[...])
        for l in range(DEPTH):
            mem_xn[l] = (mem_unit * mem_norm_ref[l:l + 1, :]).astype(BF16)
        mem_acc[...] = jnp.zeros_like(mem_acc)

    put("in_a", w_in_a_ref[...])
    put("out_a", w_out_a_ref[...])
    put("kv", w_kv_ref[...])
    w_in_b = w_in_b_ref[...]
    put("q", _cols_group_major(w_in_b[:, :SWA_W]))
    put("q_gate", _cols_group_major(w_in_b[:, SWA_W:2 * SWA_W]))
    put("mem_b", w_in_b[:, 2 * SWA_W:])
    heads = jnp.concatenate([wob_h0_ref[0], wob_h1_ref[0], wob_h2_ref[0], wob_h3_ref[0]], axis=0)
    put("out_b", jnp.where(i < GROUP, heads, wob_m_ref[...]))
    n_slabs = D_MODEL // PREP_ROWS
    for j in range(n_slabs):
        @pl.when(i == j)
        def _(j=j):
            for l in range(DEPTH):
                mem_acc[l] += _dot(mem_xn[l, :, j * PREP_ROWS:(j + 1) * PREP_ROWS], w_mem_ref[l].astype(BF16))

    @pl.when(i == n_slabs - 1)
    def _():
        for l in range(DEPTH):
            kv = mem_acc[l]
            k_t = kv[:, :MEM_W].T
            v_t = kv[:, MEM_W:].T
            k3 = k_t.reshape(MEM_HEADS, HEAD_DIM, N_MEM)
            ms = jnp.mean(k3 * k3, axis=1, keepdims=True)
            k_t = (k3 * lax.rsqrt(ms + EPS)).reshape(MEM_W, N_MEM) * mem_kgain_ref[l]
            mk_ref[l] = k_t
            mv_ref[l] = v_t
            mkvb_ref[l, 0] = k_t.astype(BF16)
            mkvb_ref[l, 1] = v_t.astype(BF16)


def _prep_call(w_in_a, w_out_a, w_kv, w_in_b, w_out_b, w_mem_kv, pool_mix, small, mem, mem_norm, mem_kgain_col):
    rows = lambda w: pl.BlockSpec((PREP_ROWS, w), lambda i: (i, 0))

    def head_src(kvh):
        return pl.BlockSpec((1, HEAD_DIM, D_MODEL),
                            lambda i: (kvh * GROUP + jnp.minimum(i, GROUP - 1), 0, 0))

    in_specs = [rows(2 * POOL_W + 2 * MEM_W), rows(D_MODEL), rows(2 * KV_W), rows(2 * SWA_W + 2 * MEM_W)]
    in_specs += [head_src(kvh) for kvh in range(N_KV_HEADS)]
    in_specs += [pl.BlockSpec((PREP_ROWS, D_MODEL), lambda i: (SWA_W // PREP_ROWS, 0)),
                pl.BlockSpec((DEPTH, PREP_ROWS, 2 * MEM_W), lambda i: (0, i, 0)),
                pl.BlockSpec(pool_mix.shape, lambda i: (0, 0, 0))]
    in_specs += [pl.BlockSpec(a.shape, lambda i: (0, 0)) for a in small]
    in_specs += [pl.BlockSpec(mem.shape, lambda i: (0, 0)), pl.BlockSpec(mem_norm.shape, lambda i: (0, 0)),
                 pl.BlockSpec(mem_kgain_col.shape, lambda i: (0, 0, 0))]
    n_vec = sum(w for _, w in _VEC_WIDTHS)
    mem_f = jax.ShapeDtypeStruct((DEPTH, MEM_W, N_MEM), F32)
    out_specs = [rows(W_ALL_COLS),
                 pl.BlockSpec((POOL_W, POOL_W), lambda i: (0, 0)),
                 pl.BlockSpec((1, n_vec), lambda i: (0, 0)),
                 pl.BlockSpec(mem_f.shape, lambda i: (0, 0, 0)), pl.BlockSpec(mem_f.shape, lambda i: (0, 0, 0)),
                 pl.BlockSpec((DEPTH, 2, MEM_W, N_MEM), lambda i: (0, 0, 0, 0))]
    bf = lambda *shape: jax.ShapeDtypeStruct(shape, BF16)
    out_shape = [bf(D_MODEL, W_ALL_COLS), bf(POOL_W, POOL_W),
                 jax.ShapeDtypeStruct((1, n_vec), F32), mem_f, mem_f, bf(DEPTH, 2, MEM_W, N_MEM)]
    w_out_b_heads = w_out_b.reshape(D_MODEL // HEAD_DIM, HEAD_DIM, D_MODEL)
    return pl.pallas_call(
        _prep_kernel,
        grid=(D_MODEL // PREP_ROWS,),
        in_specs=in_specs,
        out_specs=out_specs,
        out_shape=out_shape,
        scratch_shapes=[pltpu.VMEM((DEPTH, N_MEM, D_MODEL), BF16), pltpu.VMEM((DEPTH, N_MEM, 2 * MEM_W), F32)],
        compiler_params=pltpu.CompilerParams(dimension_semantics=("arbitrary",),
                                             vmem_limit_bytes=PREP_VMEM_LIMIT),
        name="prep",
    )(w_in_a, w_out_a, w_kv, w_in_b, *([w_out_b_heads] * N_KV_HEADS), w_out_b, w_mem_kv, pool_mix, *small,
      mem, mem_norm, mem_kgain_col)


def _rope_lane_tables(pos):
    half = ROT_DIM // 2
    inv = (ROPE_THETA ** (-np.arange(half, dtype=np.float32) * 2.0 / ROT_DIM)).astype(np.float32)
    ang = (np.asarray(pos, np.float32)[:, None] * inv[None, :]).astype(np.float64)
    cos, sin = np.cos(ang), np.sin(ang)
    t = ang.shape[0]
    rest = HEAD_DIM - ROT_DIM
    cos64 = np.concatenate([cos, cos, np.ones((t, rest))], axis=1)
    sin64 = np.concatenate([-sin, sin, np.zeros((t, rest))], axis=1)
    reps = LANES // HEAD_DIM
    return np.tile(cos64, (1, reps)).astype(np.float32), np.tile(sin64, (1, reps)).astype(np.float32)


def kernel(x_prompt, x_sample, state_pool, cache_swa_k, cache_swa_v, cache_mem_k, cache_mem_v, mem_prompt,
           norm_a, w_in_a, pool_mix_w, pool_scale, w_out_a, kv_norm, w_kv, k_norm,
           norm_b, w_in_b, q_norm, sinks, w_out_b, mem_norm, w_mem_kv, mem_q_norm, mem_k_norm):
    seg = np.arange(MXU_DIM) // HEAD_DIM
    bseg = jnp.asarray((seg[:, None] == seg[None, :]).astype(np.float32) / HEAD_DIM, BF16)
    wcol = jnp.asarray(np.repeat(np.asarray(POOL_WINDOWS, np.float32), POOL_GW).reshape(1, POOL_W))

    small = (norm_a, pool_scale, kv_norm.reshape(1, D_MODEL), k_norm.reshape(1, HEAD_DIM), norm_b, q_norm, mem_q_norm)
    mkg_col = jnp.stack([jnp.tile(mem_k_norm[l].astype(F32), MEM_HEADS).reshape(MEM_W, 1) for l in range(DEPTH)])
    w_all, wbd, vecs, mk_t, mv_t, mkv_b = _prep_call(
        w_in_a[0], w_out_a[0], w_kv, w_in_b[0], w_out_b[0], w_mem_kv, pool_mix_w[0], small,
        mem_prompt[0], mem_norm, mkg_col)
    sinks_p = sinks[0].astype(F32).reshape(N_KV_HEADS, GROUP).T.reshape(N_Q_HEADS) * LOG2E
    wts = (vecs, w_all, wbd, wcol, mkv_b, bseg)

    n_tiles = SEQ // PROMPT_TILE
    cb, sb_ = _rope_lane_tables(np.arange(n_tiles) * PROMPT_TILE)
    cr, sr = _rope_lane_tables(np.arange(PROMPT_TILE))
    rope_base = jnp.asarray(np.stack([cb, sb_], axis=1))
    rope_res = jnp.asarray(np.stack([cr, sr], axis=0))
    cos_s, sin_s = _rope_lane_tables(PAST_LEN + np.arange(DEC_SEQ))
    rope_s = jnp.asarray(np.stack([np.tile(cos_s, (STEP_SEQS, 1)), np.tile(sin_s, (STEP_SEQS, 1))], axis=0))
    pref = jnp.transpose(state_pool[0], (1, 0, 2))
    cmk_t = jnp.transpose(cache_mem_k, (0, 1, 3, 4, 2)).reshape(DEPTH, DEC_BATCH, MEM_W, N_MEM)
    cmv_t = jnp.transpose(cache_mem_v, (0, 1, 3, 4, 2)).reshape(DEPTH, DEC_BATCH, MEM_W, N_MEM)
    ck_t = jnp.transpose(cache_swa_k, (0, 2, 3, 1)).reshape(DEC_BATCH, KV_W, WINDOW)
    cv_t = jnp.transpose(cache_swa_v, (0, 2, 3, 1)).reshape(DEC_BATCH, KV_W, WINDOW)
    y_p, pool_p, k_p, v_p, y_s, pool_s, k_s, v_s = _step_call(
        sinks_p, x_prompt[0], x_sample, pref, ck_t, cv_t, cmk_t, cmv_t, rope_base, rope_res, rope_s, wts)

    def mem_out(a):
        return jnp.transpose(a.reshape(DEPTH, 1, MEM_HEADS, HEAD_DIM, N_MEM), (0, 1, 4, 2, 3))

    def swa_out_t(a):
        return jnp.transpose(a.reshape(-1, N_KV_HEADS, HEAD_DIM, WINDOW), (0, 3, 1, 2))

    return (y_p[None], y_s, pool_p[None, None], jnp.transpose(pool_s, (1, 0, 2))[None],
            swa_out_t(k_p[None]), swa_out_t(v_p[None]), swa_out_t(k_s), swa_out_t(v_s),
            mem_out(mk_t), mem_out(mv_t))
```

```python
import jax
import jax.numpy as jnp
import numpy as np
from jax import lax
from jax.experimental import pallas as pl
from jax.experimental.pallas import tpu as pltpu

D_MODEL = 1024
SEQ = 16384
DEPTH = 2
DEC_BATCH = 128
DEC_SEQ = 8
PAST_LEN = 16384
HEAD_DIM = 64
POOL_W = 768
POOL_WINDOWS = (2, 4, 8, 16)
POOL_GW = 192
POOL_PAD = 15
N_Q_HEADS = 12
N_KV_HEADS = 4
GROUP = 3
SWA_W = 768
KV_W = 256
WINDOW = 128
N_MEM = 256
MEM_HEADS = 4
MEM_W = 256
ROT_DIM = 16
ROPE_THETA = 500000.0
EPS = 1e-6

F32 = jnp.float32
BF16 = jnp.bfloat16
NEG_INF = float("-inf")
LOG2E = 1.4426950408889634
Q_SCALE = HEAD_DIM ** -0.5 * LOG2E

LANES = 128
SUBLANES = 8
MXU_DIM = 256
HEADS_PER_CHUNK = MXU_DIM // HEAD_DIM
CARRY_ROWS = 16
PROMPT_TILE = 512
STEP_SEQS = DEC_BATCH // (SEQ // PROMPT_TILE)
STEP_VMEM_LIMIT = 60 * 1024 * 1024


_VEC_WIDTHS = (("norm_a", D_MODEL), ("pool_scale", POOL_W), ("kv_norm", D_MODEL), ("k_gain", KV_W),
               ("norm_b", D_MODEL), ("q_gain", SWA_W), ("mem_q_gain0", MEM_W), ("mem_q_gain1", MEM_W))
_VEC_OFFSET = {}
for _name, _width in _VEC_WIDTHS:
    _VEC_OFFSET[_name] = (sum(w for _, w in _VEC_WIDTHS[:len(_VEC_OFFSET)]), _width)


def _vec(vec_ref, name):
    start, width = _VEC_OFFSET[name]
    return vec_ref[:, start:start + width]


_PAR_WIDTHS = (("norm_a", D_MODEL), ("pool_scale", POOL_W), ("kv_norm", D_MODEL), ("norm_b", D_MODEL),
               ("mem_norm0", D_MODEL), ("mem_norm1", D_MODEL), ("k_norm", HEAD_DIM), ("q_norm", HEAD_DIM),
               ("mem_q_norm0", HEAD_DIM), ("mem_q_norm1", HEAD_DIM), ("mem_k_norm0", HEAD_DIM),
               ("mem_k_norm1", HEAD_DIM))
_PAR_OFFSET = {}
_PAR_COLS = 0
for _name, _width in _PAR_WIDTHS:
    _PAR_OFFSET[_name] = (_PAR_COLS, _width)
    _PAR_COLS += -(-_width // LANES) * LANES


_W_WIDTHS = (("in_a", 2 * POOL_W + 2 * MEM_W), ("out_a", D_MODEL), ("kv", 2 * KV_W),
             ("q", SWA_W), ("q_gate", SWA_W), ("mem_b", 2 * MEM_W), ("out_b", D_MODEL))
_W_OFFSET = {}
for _name, _width in _W_WIDTHS:
    _W_OFFSET[_name] = (sum(w for _, w in _W_WIDTHS[:len(_W_OFFSET)]), _width)
W_ALL_COLS = sum(w for _, w in _W_WIDTHS)


def _wcols(w_ref, name, lo=0, hi=None):
    start, width = _W_OFFSET[name]
    return w_ref[:, start + lo:start + (width if hi is None else hi)]


def _dot(a, b):
    return jnp.dot(a, b, preferred_element_type=F32)


def _dot_nt(a, b):
    return lax.dot_general(a, b, (((1,), (1,)), ((), ())), preferred_element_type=F32)


def _rms_unit(x):
    return x * lax.rsqrt(jnp.mean(x * x, axis=-1, keepdims=True) + EPS)


def _silu(g):
    return g / (1.0 + jnp.exp(-g))


def _head_rms(y, gain, bseg):
    parts = []
    for c in range(y.shape[1] // MXU_DIM):
        yc = y[:, c * MXU_DIM:(c + 1) * MXU_DIM]
        ms = _dot((yc * yc).astype(BF16), bseg)
        parts.append(yc * lax.rsqrt(ms + EPS) * gain[:, c * MXU_DIM:(c + 1) * MXU_DIM])
    return parts[0] if len(parts) == 1 else jnp.concatenate(parts, axis=1)


def _rope(y, cos_t, sin_t, first8):
    parts = []
    for c in range(y.shape[1] // LANES):
        yc = y[:, c * LANES:(c + 1) * LANES]
        partner = jnp.where(first8, pltpu.roll(yc, LANES - 8, 1), pltpu.roll(yc, 8, 1))
        parts.append(yc * cos_t + partner * sin_t)
    return parts[0] if len(parts) == 1 else jnp.concatenate(parts, axis=1)


def _pool_window_sums(u_hist):
    def back(a, k):
        return pltpu.roll(a, k, 0)

    lane = lax.broadcasted_iota(jnp.int32, (1, LANES), 1)
    s2 = u_hist + back(u_hist, 1)
    t = s2[:, LANES:]
    s4 = t + back(t, 2)
    t = s4[:, 2 * LANES:]
    s8 = t + back(t, 4)
    t = s8[:, LANES:]
    s16 = t + back(t, 8)
    tiles = [
        s2[:, :LANES],
        jnp.where(lane < 64, s2[:, LANES:2 * LANES], s4[:, :LANES]),
        s4[:, LANES:2 * LANES],
        s8[:, :LANES],
        jnp.where(lane < 64, s8[:, LANES:2 * LANES], s16[:, :LANES]),
        s16[:, LANES:],
    ]
    return jnp.concatenate(tiles, axis=1)


def _pool_window_sums_planes(planes):
    n = len(planes)

    def doubled(prev, k, first):
        return [prev[j] + prev[j - k] if j >= first else None for j in range(n)]

    lane = lax.broadcasted_iota(jnp.int32, (1, LANES), 1)
    s2 = doubled(planes, 1, 1)
    s4 = doubled([None if a is None else a[:, LANES:] for a in s2], 2, 3)
    s8 = doubled([None if a is None else a[:, 2 * LANES:] for a in s4], 4, 7)
    s16 = doubled([None if a is None else a[:, LANES:] for a in s8], 8, POOL_PAD)
    out = []
    for j in range(POOL_PAD, n):
        out.append(jnp.concatenate([
            s2[j][:, :LANES],
            jnp.where(lane < 64, s2[j][:, LANES:2 * LANES], s4[j][:, :LANES]),
            s4[j][:, LANES:2 * LANES],
            s8[j][:, :LANES],
            jnp.where(lane < 64, s8[j][:, LANES:2 * LANES], s16[j][:, :LANES]),
            s16[j][:, LANES:],
        ], axis=1))
    return out


def _pool_mix(d, wbd_ref):
    lo, hi = MXU_DIM, 2 * MXU_DIM
    return jnp.concatenate([
        _dot(d[:, :hi], wbd_ref[:hi, :lo]),
        _dot(d, wbd_ref[:, lo:hi]),
        _dot(d[:, lo:], wbd_ref[lo:, hi:]),
    ], axis=1)


def _head_masks():
    lane = lax.broadcasted_iota(jnp.int32, (1, MXU_DIM), 1)
    return [(lane // HEAD_DIM) == j for j in range(HEADS_PER_CHUNK)]


def _first8_mask():
    lane = lax.broadcasted_iota(jnp.int32, (1, LANES), 1)
    return (lane % HEAD_DIM) < (ROT_DIM // 2)


def _stack_heads(q, hmask):
    return jnp.concatenate([jnp.where(m, q, 0.0) for m in hmask], axis=0).astype(BF16)


def _unstack_heads(o, hmask):
    n = len(hmask)
    m = o.shape[0] // n
    acc = o[(n - 1) * m:, :]
    for h in range(n - 2, -1, -1):
        acc = jnp.where(hmask[h], o[h * m:(h + 1) * m, :], acc)
    return acc


def _sink_columns(sinks_ref, rows_per_head):
    hrow = lax.broadcasted_iota(jnp.int32, (HEADS_PER_CHUNK * rows_per_head, 1), 0) // rows_per_head
    cols = []
    for c in range(SWA_W // MXU_DIM):
        col = jnp.zeros((HEADS_PER_CHUNK * rows_per_head, 1), F32)
        for jj in range(HEADS_PER_CHUNK):
            col = jnp.where(hrow == jj, sinks_ref[c * HEADS_PER_CHUNK + jj], col)
        cols.append(col)
    return cols


def _softmax_weights(s, sink):
    m = jnp.max(s, axis=-1, keepdims=True)
    if sink is not None:
        m = jnp.maximum(m, sink)
    p = jnp.exp2(s - m)
    den = jnp.sum(p, axis=-1, keepdims=True)
    if sink is not None:
        den = den + jnp.exp2(sink - m)
    return p.astype(BF16), 1.0 / den


def _mem_attn(q, k_t, v_t, hmask):
    p, rden = _softmax_weights(_dot(_stack_heads(q, hmask), k_t), None)
    return _unstack_heads(_dot_nt(p, v_t) * rden, hmask)


def _swa_block(qb, kc, vc, bias, sink_cols, hmask):
    outs = []
    for g in range(GROUP):
        s = _dot_nt(_stack_heads(qb[:, g * MXU_DIM:(g + 1) * MXU_DIM], hmask), kc)
        s = (s.reshape(HEADS_PER_CHUNK, WINDOW, 2 * WINDOW) + bias[None]).reshape(HEADS_PER_CHUNK * WINDOW, 2 * WINDOW)
        p, rden = _softmax_weights(s, sink_cols[g])
        outs.append(_unstack_heads(_dot(p, vc) * rden, hmask))
    return jnp.concatenate(outs, axis=1)


def _const_spec(shape):
    nd = len(shape)
    return pl.BlockSpec(shape, lambda i: (0,) * nd, pipeline_mode=pl.Buffered(1))


def _stack_heads_seq(q, n_seq, hmask):
    q3 = q.reshape(n_seq, DEC_SEQ, MXU_DIM)
    return jnp.stack([jnp.where(m, q3, 0.0) for m in hmask], axis=1)


def _unstack_heads_seq(o, n_seq, hmask):
    o4 = o.reshape(n_seq, HEADS_PER_CHUNK, DEC_SEQ, MXU_DIM)
    acc = o4[:, HEADS_PER_CHUNK - 1]
    for h in range(HEADS_PER_CHUNK - 2, -1, -1):
        acc = jnp.where(hmask[h], o4[:, h], acc)
    return acc.reshape(n_seq * DEC_SEQ, MXU_DIM)


def _mem_attn_seqs(q, k_ref, v_ref, layer, n_seq, hmask):
    rows = HEADS_PER_CHUNK * DEC_SEQ
    qs = _stack_heads_seq(q, n_seq, hmask).reshape(n_seq * rows, MXU_DIM).astype(BF16)
    s = jnp.concatenate([_dot(qs[b * rows:(b + 1) * rows], k_ref[layer, b].astype(BF16))
                         for b in range(n_seq)], axis=0)
    p, rden = _softmax_weights(s, None)
    o = jnp.concatenate([_dot_nt(p[b * rows:(b + 1) * rows], v_ref[layer, b].astype(BF16))
                         for b in range(n_seq)], axis=0)
    return _unstack_heads_seq(o * rden, n_seq, hmask)


def _sample_pool(u_s, pref_ref, pool_ref, wcol, parity, u_scr, d_scr):
    sb = STEP_SEQS
    n_lt = POOL_W // LANES
    for c in range(n_lt):
        u_scr[c] = u_s[:, c * LANES:(c + 1) * LANES]
    planes = [jnp.where(parity == 0, pref_ref[t, 0:sb, :], pref_ref[t, sb:2 * sb, :]) for t in range(POOL_PAD)]
    planes += [jnp.concatenate([u_scr[c, pl.ds(t, sb, stride=DEC_SEQ), :] for c in range(n_lt)], axis=1)
               for t in range(DEC_SEQ)]
    for t, win in enumerate(_pool_window_sums_planes(planes)):
        cnt = jnp.minimum(float(PAST_LEN + t + 1), wcol)
        d_t = win / cnt - planes[POOL_PAD + t]
        for c in range(n_lt):
            d_scr[c, pl.ds(t, sb, stride=DEC_SEQ), :] = d_t[:, c * LANES:(c + 1) * LANES]

    @pl.when(parity == 0)
    def _():
        for t in range(POOL_PAD):
            pool_ref[t, 0:sb, :] = planes[DEC_SEQ + t]

    @pl.when(parity == 1)
    def _():
        for t in range(POOL_PAD):
            pool_ref[t, sb:2 * sb, :] = planes[DEC_SEQ + t]

    return jnp.concatenate([d_scr[c] for c in range(n_lt)], axis=1)


def _sample_cache_update(k_s, v_s, ck_ref, cv_ref, ko_ref, vo_ref):
    sb = STEP_SEQS
    keep = WINDOW - DEC_SEQ
    key_lane = lax.broadcasted_iota(jnp.int32, (1, WINDOW), 1)
    pad_rows = jnp.zeros((WINDOW - sb * DEC_SEQ, KV_W), F32)
    for new_rows, c_ref, o_ref in ((k_s, ck_ref, ko_ref), (v_s, cv_ref, vo_ref)):
        new_t = jnp.concatenate([new_rows, pad_rows], axis=0).T
        for b in range(sb):
            shifted = pltpu.roll(c_ref[b], keep, 1)
            placed = pltpu.roll(new_t, (keep - b * DEC_SEQ) % WINDOW, 1)
            o_ref[b] = jnp.where(key_lane < keep, shifted, placed)


def _sample_window_attn(q_s, k_s, v_s, ck_ref, cv_ref, sinks_ref, hmask):
    sb = STEP_SEQS
    m = sb * DEC_SEQ
    rows = N_Q_HEADS * DEC_SEQ
    qs = jnp.stack([_stack_heads_seq(q_s[:, g * MXU_DIM:(g + 1) * MXU_DIM], sb, hmask) for g in range(GROUP)],
                   axis=1)
    qs = qs.reshape(sb * rows, KV_W).astype(BF16)
    s_old = jnp.concatenate([_dot(qs[b * rows:(b + 1) * rows], ck_ref[b].astype(BF16)) for b in range(sb)], axis=0)
    s_new = _dot_nt(qs, k_s.astype(BF16))
    tq_old = lax.broadcasted_iota(jnp.int32, (rows, WINDOW), 0) % DEC_SEQ
    key_old = lax.broadcasted_iota(jnp.int32, (rows, WINDOW), 1)
    bias_old = jnp.where(key_old > tq_old, 0.0, NEG_INF)
    row_i = lax.broadcasted_iota(jnp.int32, (sb * rows, m), 0)
    col_i = lax.broadcasted_iota(jnp.int32, (sb * rows, m), 1)
    same_seq = (row_i // rows) == (col_i // DEC_SEQ)
    bias_new = jnp.where(same_seq, jnp.where(col_i % DEC_SEQ <= row_i % DEC_SEQ, 0.0, NEG_INF), NEG_INF)
    s_old = (s_old.reshape(sb, rows, WINDOW) + bias_old[None]).reshape(sb * rows, WINDOW)
    s_new = s_new + bias_new
    sink = jnp.concatenate(_sink_columns(sinks_ref, DEC_SEQ), axis=0)
    sink = jnp.concatenate([sink] * sb, axis=0)
    mx = jnp.maximum(jnp.maximum(jnp.max(s_old, axis=-1, keepdims=True), jnp.max(s_new, axis=-1, keepdims=True)), sink)
    p_old = jnp.exp2(s_old - mx)
    p_new = jnp.exp2(s_new - mx)
    den = (jnp.sum(p_old, axis=-1, keepdims=True) + jnp.sum(p_new, axis=-1, keepdims=True) + jnp.exp2(sink - mx))
    p_old = p_old.astype(BF16)
    o = jnp.concatenate([_dot_nt(p_old[b * rows:(b + 1) * rows], cv_ref[b].astype(BF16)) for b in range(sb)], axis=0)
    o = (o + _dot(p_new.astype(BF16), v_s.astype(BF16))) * (1.0 / den)
    o5 = o.reshape(sb, GROUP, HEADS_PER_CHUNK * DEC_SEQ, KV_W)
    return jnp.concatenate([_unstack_heads_seq(o5[:, g].reshape(sb * HEADS_PER_CHUNK * DEC_SEQ, KV_W), sb, hmask)
                            for g in range(GROUP)], axis=1)


def _step_kernel(sinks_ref, xp_ref, xs_ref, pref_ref, ck_ref, cv_ref, cmk_ref, cmv_ref,
                 rope_base_ref, rope_res_ref, rope_s_ref,
                 vec_ref, w_ref, wbd_ref, wcol_ref, mkv_ref, bseg_ref,
                 yp_ref, pool_p_ref, kop_ref, vop_ref, ys_ref, pool_s_ref, kos_ref, vos_ref,
                 ucarry, kprev, vprev, u_scr, d_scr):
    tq = PROMPT_TILE
    sb = STEP_SEQS
    m = sb * DEC_SEQ
    i = pl.program_id(0)

    @pl.when(i == 0)
    def _():
        ucarry[0] = jnp.zeros((CARRY_ROWS, POOL_W), F32)
        kprev[0] = jnp.zeros((WINDOW, KV_W), BF16)
        vprev[0] = jnp.zeros((WINDOW, KV_W), BF16)

    rd = i % 2
    wr = (i + 1) % 2

    hmask = _head_masks()
    first8 = _first8_mask()
    bseg = bseg_ref[...]
    x = jnp.concatenate([xp_ref[...], xs_ref[...].reshape(m, D_MODEL)], axis=0)

    base = rope_base_ref[i]
    cb, sb_ = base[0:1, :], base[1:2, :]
    cr, sr = rope_res_ref[0], rope_res_ref[1]
    cos_t = jnp.concatenate([cb * cr - sb_ * sr, rope_s_ref[0]], axis=0)
    sin_t = jnp.concatenate([sb_ * cr + cb * sr, rope_s_ref[1]], axis=0)

    xn = (_rms_unit(x) * _vec(vec_ref, "norm_a")).astype(BF16)
    u = _dot(xn, _wcols(w_ref, "in_a", 0, POOL_W))
    gp = _dot(xn, _wcols(w_ref, "in_a", POOL_W, 2 * POOL_W))
    qm = _dot(xn, _wcols(w_ref, "in_a", 2 * POOL_W, 2 * POOL_W + MEM_W))
    gm = _dot(xn, _wcols(w_ref, "in_a", 2 * POOL_W + MEM_W))
    u_p = u[:tq]
    u_hist = jnp.concatenate([ucarry[rd], u_p], axis=0)
    win = _pool_window_sums(u_hist)[CARRY_ROWS:]
    pos = i * tq + lax.broadcasted_iota(jnp.int32, (tq, 1), 0)
    cnt = jnp.minimum((pos + 1).astype(F32), wcol_ref[...])
    d_p = win / cnt - u_p
    d_s = _sample_pool(u[tq:], pref_ref, pool_s_ref, wcol_ref[...], rd, u_scr, d_scr)
    d = jnp.concatenate([d_p, d_s], axis=0)
    yp = _pool_mix(d.astype(BF16), wbd_ref) * _vec(vec_ref, "pool_scale") * _silu(gp)
    qmn = _head_rms(qm, _vec(vec_ref, "mem_q_gain0"), bseg) * Q_SCALE
    ym = jnp.concatenate([_mem_attn(qmn[:tq], mkv_ref[0, 0], mkv_ref[0, 1], hmask),
                          _mem_attn_seqs(qmn[tq:], cmk_ref, cmv_ref, 0, sb, hmask)], axis=0) * _silu(gm)
    x1 = x + _dot(jnp.concatenate([yp, ym], axis=1).astype(BF16), _wcols(w_ref, "out_a"))
    ucarry[wr] = u_p[tq - CARRY_ROWS:, :]

    r = _rms_unit(x1)
    kv = _dot((r * _vec(vec_ref, "kv_norm")).astype(BF16), _wcols(w_ref, "kv"))
    k = _rope(_head_rms(kv[:, :KV_W], _vec(vec_ref, "k_gain"), bseg), cos_t, sin_t, first8)
    v = kv[:, KV_W:]
    k_p, v_p, k_s, v_s = k[:tq], v[:tq], k[tq:], v[tq:]
    k_all = jnp.concatenate([kprev[rd], k_p.astype(BF16)], axis=0)
    v_all = jnp.concatenate([vprev[rd], v_p.astype(BF16)], axis=0)
    _sample_cache_update(k_s, v_s, ck_ref, cv_ref, kos_ref, vos_ref)

    xb = (r * _vec(vec_ref, "norm_b")).astype(BF16)
    zq = _dot(xb, _wcols(w_ref, "q"))
    gq = _dot(xb, _wcols(w_ref, "q_gate"))
    qm2 = _dot(xb, _wcols(w_ref, "mem_b", 0, MEM_W))
    gm2 = _dot(xb, _wcols(w_ref, "mem_b", MEM_W))
    q = _rope(_head_rms(zq, _vec(vec_ref, "q_gain"), bseg), cos_t, sin_t, first8) * Q_SCALE

    qi = lax.broadcasted_iota(jnp.int32, (WINDOW, 2 * WINDOW), 0)
    ci = lax.broadcasted_iota(jnp.int32, (WINDOW, 2 * WINDOW), 1)
    band_bias = jnp.where(ci > qi, jnp.where(ci <= qi + WINDOW, 0.0, NEG_INF), NEG_INF)
    sink_cols = _sink_columns(sinks_ref, WINDOW)
    ys_blocks = []
    for b in range(tq // WINDOW):
        bias = band_bias
        if b == 0:
            key_pos = ci + (i * tq - WINDOW)
            bias = jnp.where(key_pos >= 0, band_bias, NEG_INF)
        qb = q[b * WINDOW:(b + 1) * WINDOW, :]
        kc = k_all[b * WINDOW:(b + 2) * WINDOW, :]
        vc = v_all[b * WINDOW:(b + 2) * WINDOW, :]
        ys_blocks.append(_swa_block(qb, kc, vc, bias, sink_cols, hmask))
    ys_blocks.append(_sample_window_attn(q[tq:], k_s, v_s, ck_ref, cv_ref, sinks_ref, hmask))
    ys = jnp.concatenate(ys_blocks, axis=0) * _silu(gq)
    qmn2 = _head_rms(qm2, _vec(vec_ref, "mem_q_gain1"), bseg) * Q_SCALE
    ym2 = jnp.concatenate([_mem_attn(qmn2[:tq], mkv_ref[1, 0], mkv_ref[1, 1], hmask),
                           _mem_attn_seqs(qmn2[tq:], cmk_ref, cmv_ref, 1, sb, hmask)], axis=0) * _silu(gm2)
    ob, ob_w = _W_OFFSET["out_b"]
    y = (x1 + _dot(ys.astype(BF16), w_ref[0:SWA_W, ob:ob + ob_w])
         + _dot(ym2.astype(BF16), w_ref[SWA_W:, ob:ob + ob_w]))
    yp_ref[...] = y[:tq]
    ys_ref[...] = y[tq:].reshape(sb, DEC_SEQ, D_MODEL)

    kprev[wr] = k_all[tq:, :]
    vprev[wr] = v_all[tq:, :]

    @pl.when(i == pl.num_programs(0) - 1)
    def _():
        kop_ref[...] = k_p[tq - WINDOW:, :].T
        vop_ref[...] = v_p[tq - WINDOW:, :].T
        pool_p_ref[...] = pltpu.roll(u_p[tq - CARRY_ROWS:, :], CARRY_ROWS - 1, 0)[0:POOL_PAD, :]


def _step_call(sinks, x_p, x_s, pref, ck, cv, cmk_t, cmv_t, rope_base, rope_res, rope_s, wts):
    tq = PROMPT_TILE
    sb = STEP_SEQS
    n = x_p.shape[0]
    nb = x_s.shape[0]
    assert n // tq == nb // sb
    row_spec = lambda w: pl.BlockSpec((tq, w), lambda i: (i, 0))
    seq3 = lambda a, b_: pl.BlockSpec((sb, a, b_), lambda i: (i, 0, 0))
    pool_s_spec = pl.BlockSpec((POOL_PAD, 2 * sb, POOL_W), lambda i: (0, i // 2, 0))
    cache_mem_spec = pl.BlockSpec((DEPTH, sb, MEM_W, N_MEM), lambda i: (0, i, 0, 0))
    in_specs = [pl.BlockSpec(memory_space=pltpu.SMEM), row_spec(D_MODEL),
                seq3(DEC_SEQ, D_MODEL), pool_s_spec, seq3(KV_W, WINDOW), seq3(KV_W, WINDOW),
                cache_mem_spec, cache_mem_spec,
                _const_spec(rope_base.shape), _const_spec(rope_res.shape), _const_spec(rope_s.shape)]
    in_specs += [_const_spec(w.shape) for w in wts]
    out_shape = [jax.ShapeDtypeStruct((n, D_MODEL), F32),
                 jax.ShapeDtypeStruct((POOL_PAD, POOL_W), F32),
                 jax.ShapeDtypeStruct((KV_W, WINDOW), F32),
                 jax.ShapeDtypeStruct((KV_W, WINDOW), F32),
                 jax.ShapeDtypeStruct((nb, DEC_SEQ, D_MODEL), F32),
                 jax.ShapeDtypeStruct((POOL_PAD, nb, POOL_W), F32),
                 jax.ShapeDtypeStruct((nb, KV_W, WINDOW), F32),
                 jax.ShapeDtypeStruct((nb, KV_W, WINDOW), F32)]
    out_specs = [row_spec(D_MODEL),
                 pl.BlockSpec((POOL_PAD, POOL_W), lambda i: (0, 0)),
                 pl.BlockSpec((KV_W, WINDOW), lambda i: (0, 0)),
                 pl.BlockSpec((KV_W, WINDOW), lambda i: (0, 0)),
                 seq3(DEC_SEQ, D_MODEL), pool_s_spec, seq3(KV_W, WINDOW), seq3(KV_W, WINDOW)]
    return pl.pallas_call(
        _step_kernel,
        grid=(n // tq,),
        in_specs=in_specs,
        out_specs=out_specs,
        out_shape=out_shape,
        scratch_shapes=[pltpu.VMEM((2, CARRY_ROWS, POOL_W), F32),
                        pltpu.VMEM((2, WINDOW, KV_W), BF16),
                        pltpu.VMEM((2, WINDOW, KV_W), BF16),
                        pltpu.VMEM((POOL_W // LANES, sb * DEC_SEQ, LANES), F32),
                        pltpu.VMEM((POOL_W // LANES, sb * DEC_SEQ, LANES), F32)],
        compiler_params=pltpu.CompilerParams(dimension_semantics=("arbitrary",),
                                             vmem_limit_bytes=STEP_VMEM_LIMIT),
        name="step",
    )(sinks, x_p, x_s, pref, ck, cv, cmk_t, cmv_t, rope_base, rope_res, rope_s, *wts)


PREP_ROWS = N_KV_HEADS * HEAD_DIM
PREP_VMEM_LIMIT = 40 * 1024 * 1024


def _cols_group_major(w):
    heads = [w[:, (kvh * GROUP + g) * HEAD_DIM:(kvh * GROUP + g + 1) * HEAD_DIM]
             for g in range(GROUP) for kvh in range(N_KV_HEADS)]
    return jnp.concatenate(heads, axis=1)


def _prep_kernel(w_in_a_ref, w_out_a_ref, w_kv_ref, w_in_b_ref, wob_h0_ref, wob_h1_ref, wob_h2_ref, wob_h3_ref,
                 wob_m_ref, w_mem_ref, mix_ref,
                 par_ref, mem_ref,
                 w_ref, wbd_ref, vec_ref, mk_ref, mv_ref, mkvb_ref,
                 mem_xn, mem_acc):
    i = pl.program_id(0)

    def put(name, value):
        start, width = _W_OFFSET[name]
        w_ref[:, start:start + width] = value.astype(BF16)

    def par(name):
        start, width = _PAR_OFFSET[name]
        return par_ref[:, start:start + width]

    def per_head(g_row, width):
        return jnp.concatenate([g_row] * (width // HEAD_DIM), axis=1)

    @pl.when(i == 0)
    def _():
        n_grp = len(POOL_WINDOWS)
        for g in range(n_grp):
            pieces = [jnp.zeros((POOL_GW, POOL_GW), F32)] * n_grp
            pieces[g] = mix_ref[g]
            wbd_ref[g * POOL_GW:(g + 1) * POOL_GW, :] = jnp.concatenate(pieces, axis=1).astype(BF16)

        vec_pieces = {"norm_a": par("norm_a"), "pool_scale": par("pool_scale"), "kv_norm": par("kv_norm"),
                      "k_gain": per_head(par("k_norm"), KV_W), "norm_b": par("norm_b"),
                      "q_gain": per_head(par("q_norm"), SWA_W),
                      "mem_q_gain0": per_head(par("mem_q_norm0"), MEM_W),
                      "mem_q_gain1": per_head(par("mem_q_norm1"), MEM_W)}
        for name, (start, width) in _VEC_OFFSET.items():
            vec_ref[:, start:start + width] = vec_pieces[name]

        mem_unit = _rms_unit(mem_ref[...])
        for l in range(DEPTH):
            mem_xn[l] = (mem_unit * par("mem_norm%d" % l)).astype(BF16)
        mem_acc[...] = jnp.zeros_like(mem_acc)

    put("in_a", w_in_a_ref[...])
    put("out_a", w_out_a_ref[...])
    put("kv", w_kv_ref[...])
    w_in_b = w_in_b_ref[...]
    put("q", _cols_group_major(w_in_b[:, :SWA_W]))
    put("q_gate", _cols_group_major(w_in_b[:, SWA_W:2 * SWA_W]))
    put("mem_b", w_in_b[:, 2 * SWA_W:])
    heads = jnp.concatenate([wob_h0_ref[0], wob_h1_ref[0], wob_h2_ref[0], wob_h3_ref[0]], axis=0)
    put("out_b", jnp.where(i < GROUP, heads, wob_m_ref[...]))
    n_slabs = D_MODEL // PREP_ROWS
    for j in range(n_slabs):
        @pl.when(i == j)
        def _(j=j):
            for l in range(DEPTH):
                mem_acc[l] += _dot(mem_xn[l, :, j * PREP_ROWS:(j + 1) * PREP_ROWS], w_mem_ref[l].astype(BF16))

    @pl.when(i == n_slabs - 1)
    def _():
        for l in range(DEPTH):
            kv = mem_acc[l]
            k = kv[:, :MEM_W]
            v_t = kv[:, MEM_W:].T
            k3 = k.T.reshape(MEM_HEADS, HEAD_DIM, N_MEM)
            ms = jnp.mean(k3 * k3, axis=1, keepdims=True)
            k_gained = (k * per_head(par("mem_k_norm%d" % l), MEM_W)).T.reshape(MEM_HEADS, HEAD_DIM, N_MEM)
            k_t = (k_gained * lax.rsqrt(ms + EPS)).reshape(MEM_W, N_MEM)
            mk_ref[l] = k_t
            mv_ref[l] = v_t
            mkvb_ref[l, 0] = k_t.astype(BF16)
            mkvb_ref[l, 1] = v_t.astype(BF16)


def _prep_call(w_in_a, w_out_a, w_kv, w_in_b, w_out_b, w_mem_kv, pool_mix, params, mem):
    rows = lambda w: pl.BlockSpec((PREP_ROWS, w), lambda i: (i, 0))

    def head_src(kvh):
        return pl.BlockSpec((1, HEAD_DIM, D_MODEL),
                            lambda i: (kvh * GROUP + jnp.minimum(i, GROUP - 1), 0, 0))

    in_specs = [rows(2 * POOL_W + 2 * MEM_W), rows(D_MODEL), rows(2 * KV_W), rows(2 * SWA_W + 2 * MEM_W)]
    in_specs += [head_src(kvh) for kvh in range(N_KV_HEADS)]
    in_specs += [pl.BlockSpec((PREP_ROWS, D_MODEL), lambda i: (SWA_W // PREP_ROWS, 0)),
                pl.BlockSpec((DEPTH, PREP_ROWS, 2 * MEM_W), lambda i: (0, i, 0)),
                pl.BlockSpec(pool_mix.shape, lambda i: (0, 0, 0))]
    in_specs += [pl.BlockSpec(params.shape, lambda i: (0, 0)), pl.BlockSpec(mem.shape, lambda i: (0, 0))]
    n_vec = sum(w for _, w in _VEC_WIDTHS)
    mem_f = jax.ShapeDtypeStruct((DEPTH, MEM_W, N_MEM), F32)
    out_specs = [rows(W_ALL_COLS),
                 pl.BlockSpec((POOL_W, POOL_W), lambda i: (0, 0)),
                 pl.BlockSpec((1, n_vec), lambda i: (0, 0)),
                 pl.BlockSpec(mem_f.shape, lambda i: (0, 0, 0)), pl.BlockSpec(mem_f.shape, lambda i: (0, 0, 0)),
                 pl.BlockSpec((DEPTH, 2, MEM_W, N_MEM), lambda i: (0, 0, 0, 0))]
    bf = lambda *shape: jax.ShapeDtypeStruct(shape, BF16)
    out_shape = [bf(D_MODEL, W_ALL_COLS), bf(POOL_W, POOL_W),
                 jax.ShapeDtypeStruct((1, n_vec), F32), mem_f, mem_f, bf(DEPTH, 2, MEM_W, N_MEM)]
    w_out_b_heads = w_out_b.reshape(D_MODEL // HEAD_DIM, HEAD_DIM, D_MODEL)
    return pl.pallas_call(
        _prep_kernel,
        grid=(D_MODEL // PREP_ROWS,),
        in_specs=in_specs,
        out_specs=out_specs,
        out_shape=out_shape,
        scratch_shapes=[pltpu.VMEM((DEPTH, N_MEM, D_MODEL), BF16), pltpu.VMEM((DEPTH, N_MEM, 2 * MEM_W), F32)],
        compiler_params=pltpu.CompilerParams(dimension_semantics=("arbitrary",),
                                             vmem_limit_bytes=PREP_VMEM_LIMIT),
        name="prep",
    )(w_in_a, w_out_a, w_kv, w_in_b, *([w_out_b_heads] * N_KV_HEADS), w_out_b, w_mem_kv, pool_mix, params, mem)


def _rope_lane_tables(pos):
    half = ROT_DIM // 2
    inv = (ROPE_THETA ** (-np.arange(half, dtype=np.float32) * 2.0 / ROT_DIM)).astype(np.float32)
    ang = (np.asarray(pos, np.float32)[:, None] * inv[None, :]).astype(np.float64)
    cos, sin = np.cos(ang), np.sin(ang)
    t = ang.shape[0]
    rest = HEAD_DIM - ROT_DIM
    cos64 = np.concatenate([cos, cos, np.ones((t, rest))], axis=1)
    sin64 = np.concatenate([-sin, sin, np.zeros((t, rest))], axis=1)
    reps = LANES // HEAD_DIM
    return np.tile(cos64, (1, reps)).astype(np.float32), np.tile(sin64, (1, reps)).astype(np.float32)


def kernel(x_prompt, x_sample, state_pool, cache_swa_k, cache_swa_v, cache_mem_k, cache_mem_v, mem_prompt,
           norm_a, w_in_a, pool_mix_w, pool_scale, w_out_a, kv_norm, w_kv, k_norm,
           norm_b, w_in_b, q_norm, sinks, w_out_b, mem_norm, w_mem_kv, mem_q_norm, mem_k_norm):
    seg = np.arange(MXU_DIM) // HEAD_DIM
    bseg = jnp.asarray((seg[:, None] == seg[None, :]).astype(np.float32) / HEAD_DIM, BF16)
    wcol = jnp.asarray(np.repeat(np.asarray(POOL_WINDOWS, np.float32), POOL_GW).reshape(1, POOL_W))

    raw = {"norm_a": norm_a[0], "pool_scale": pool_scale[0], "kv_norm": kv_norm, "norm_b": norm_b[0],
           "mem_norm0": mem_norm[0], "mem_norm1": mem_norm[1], "k_norm": k_norm, "q_norm": q_norm[0],
           "mem_q_norm0": mem_q_norm[0], "mem_q_norm1": mem_q_norm[1],
           "mem_k_norm0": mem_k_norm[0], "mem_k_norm1": mem_k_norm[1]}
    params = jnp.concatenate([jnp.pad(raw[name].astype(F32), (0, -width % LANES)) for name, width in _PAR_WIDTHS])
    w_all, wbd, vecs, mk_t, mv_t, mkv_b = _prep_call(
        w_in_a[0], w_out_a[0], w_kv, w_in_b[0], w_out_b[0], w_mem_kv, pool_mix_w[0], params.reshape(1, _PAR_COLS),
        mem_prompt[0])
    sinks_p = sinks[0].astype(F32).reshape(N_KV_HEADS, GROUP).T.reshape(N_Q_HEADS) * LOG2E
    wts = (vecs, w_all, wbd, wcol, mkv_b, bseg)

    n_tiles = SEQ // PROMPT_TILE
    cb, sb_ = _rope_lane_tables(np.arange(n_tiles) * PROMPT_TILE)
    cr, sr = _rope_lane_tables(np.arange(PROMPT_TILE))
    rope_base = jnp.asarray(np.stack([cb, sb_], axis=1))
    rope_res = jnp.asarray(np.stack([cr, sr], axis=0))
    cos_s, sin_s = _rope_lane_tables(PAST_LEN + np.arange(DEC_SEQ))
    rope_s = jnp.asarray(np.stack([np.tile(cos_s, (STEP_SEQS, 1)), np.tile(sin_s, (STEP_SEQS, 1))], axis=0))
    pref = jnp.transpose(state_pool[0], (1, 0, 2))
    cmk_t = jnp.transpose(cache_mem_k, (0, 1, 3, 4, 2)).reshape(DEPTH, DEC_BATCH, MEM_W, N_MEM)
    cmv_t = jnp.transpose(cache_mem_v, (0, 1, 3, 4, 2)).reshape(DEPTH, DEC_BATCH, MEM_W, N_MEM)
    ck_t = jnp.transpose(cache_swa_k, (0, 2, 3, 1)).reshape(DEC_BATCH, KV_W, WINDOW)
    cv_t = jnp.transpose(cache_swa_v, (0, 2, 3, 1)).reshape(DEC_BATCH, KV_W, WINDOW)
    y_p, pool_p, k_p, v_p, y_s, pool_s, k_s, v_s = _step_call(
        sinks_p, x_prompt[0], x_sample, pref, ck_t, cv_t, cmk_t, cmv_t, rope_base, rope_res, rope_s, wts)

    def mem_out(a):
        return jnp.transpose(a.reshape(DEPTH, 1, MEM_HEADS, HEAD_DIM, N_MEM), (0, 1, 4, 2, 3))

    def swa_out_t(a):
        return jnp.transpose(a.reshape(-1, N_KV_HEADS, HEAD_DIM, WINDOW), (0, 3, 1, 2))

    return (y_p[None], y_s, pool_p[None, None], jnp.transpose(pool_s, (1, 0, 2))[None],
            swa_out_t(k_p[None]), swa_out_t(v_p[None]), swa_out_t(k_s), swa_out_t(v_s),
            mem_out(mk_t), mem_out(mv_t))
```

```python
import jax
import jax.numpy as jnp
import numpy as np
from jax import lax
from jax.experimental import pallas as pl
from jax.experimental.pallas import tpu as pltpu

D_MODEL = 1024
SEQ = 16384
DEPTH = 2
DEC_BATCH = 128
DEC_SEQ = 8
PAST_LEN = 16384
HEAD_DIM = 64
POOL_W = 768
POOL_WINDOWS = (2, 4, 8, 16)
POOL_GW = 192
POOL_PAD = 15
N_Q_HEADS = 12
N_KV_HEADS = 4
GROUP = 3
SWA_W = 768
KV_W = 256
WINDOW = 128
N_MEM = 256
MEM_HEADS = 4
MEM_W = 256
ROT_DIM = 16
ROPE_THETA = 500000.0
EPS = 1e-6

F32 = jnp.float32
BF16 = jnp.bfloat16
NEG_INF = float("-inf")
LOG2E = 1.4426950408889634
Q_SCALE = HEAD_DIM ** -0.5 * LOG2E

LANES = 128
SUBLANES = 8
MXU_DIM = 256
HEADS_PER_CHUNK = MXU_DIM // HEAD_DIM
CARRY_ROWS = 16
PROMPT_TILE = 512
STEP_SEQS = DEC_BATCH // (SEQ // PROMPT_TILE)
STEP_VMEM_LIMIT = 60 * 1024 * 1024


_VEC_WIDTHS = (("norm_a", D_MODEL), ("pool_scale", POOL_W), ("kv_norm", D_MODEL), ("k_gain", KV_W),
               ("norm_b", D_MODEL), ("q_gain", SWA_W), ("mem_q_gain0", MEM_W), ("mem_q_gain1", MEM_W))
_VEC_OFFSET = {}
for _name, _width in _VEC_WIDTHS:
    _VEC_OFFSET[_name] = (sum(w for _, w in _VEC_WIDTHS[:len(_VEC_OFFSET)]), _width)


def _vec(vec_ref, name):
    start, width = _VEC_OFFSET[name]
    return vec_ref[:, start:start + width]


_PAR_WIDTHS = (("norm_a", D_MODEL), ("pool_scale", POOL_W), ("kv_norm", D_MODEL), ("norm_b", D_MODEL),
               ("mem_norm0", D_MODEL), ("mem_norm1", D_MODEL), ("k_norm", HEAD_DIM), ("q_norm", HEAD_DIM),
               ("mem_q_norm0", HEAD_DIM), ("mem_q_norm1", HEAD_DIM), ("mem_k_norm0", HEAD_DIM),
               ("mem_k_norm1", HEAD_DIM))
_PAR_OFFSET = {}
_PAR_COLS = 0
for _name, _width in _PAR_WIDTHS:
    _PAR_OFFSET[_name] = (_PAR_COLS, _width)
    _PAR_COLS += _width


_W_WIDTHS = (("in_a", 2 * POOL_W + 2 * MEM_W), ("out_a", D_MODEL), ("kv", 2 * KV_W),
             ("q", SWA_W), ("q_gate", SWA_W), ("mem_b", 2 * MEM_W), ("out_b", D_MODEL))
_W_OFFSET = {}
for _name, _width in _W_WIDTHS:
    _W_OFFSET[_name] = (sum(w for _, w in _W_WIDTHS[:len(_W_OFFSET)]), _width)
W_ALL_COLS = sum(w for _, w in _W_WIDTHS)


def _wcols(w_ref, name, lo=0, hi=None):
    start, width = _W_OFFSET[name]
    return w_ref[:, start + lo:start + (width if hi is None else hi)]


def _dot(a, b):
    return jnp.dot(a, b, preferred_element_type=F32)


def _dot_nt(a, b):
    return lax.dot_general(a, b, (((1,), (1,)), ((), ())), preferred_element_type=F32)


def _rms_unit(x):
    return x * lax.rsqrt(jnp.mean(x * x, axis=-1, keepdims=True) + EPS)


def _silu(g):
    return g / (1.0 + jnp.exp(-g))


def _head_rms(y, gain, bseg):
    parts = []
    for c in range(y.shape[1] // MXU_DIM):
        yc = y[:, c * MXU_DIM:(c + 1) * MXU_DIM]
        ms = _dot((yc * yc).astype(BF16), bseg)
        parts.append(yc * lax.rsqrt(ms + EPS) * gain[:, c * MXU_DIM:(c + 1) * MXU_DIM])
    return parts[0] if len(parts) == 1 else jnp.concatenate(parts, axis=1)


def _rope(y, cos_t, sin_t, first8):
    parts = []
    for c in range(y.shape[1] // LANES):
        yc = y[:, c * LANES:(c + 1) * LANES]
        partner = jnp.where(first8, pltpu.roll(yc, LANES - 8, 1), pltpu.roll(yc, 8, 1))
        parts.append(yc * cos_t + partner * sin_t)
    return parts[0] if len(parts) == 1 else jnp.concatenate(parts, axis=1)


def _pool_window_sums(u_hist):
    def back(a, k):
        return pltpu.roll(a, k, 0)

    lane = lax.broadcasted_iota(jnp.int32, (1, LANES), 1)
    s2 = u_hist + back(u_hist, 1)
    t = s2[:, LANES:]
    s4 = t + back(t, 2)
    t = s4[:, 2 * LANES:]
    s8 = t + back(t, 4)
    t = s8[:, LANES:]
    s16 = t + back(t, 8)
    tiles = [
        s2[:, :LANES],
        jnp.where(lane < 64, s2[:, LANES:2 * LANES], s4[:, :LANES]),
        s4[:, LANES:2 * LANES],
        s8[:, :LANES],
        jnp.where(lane < 64, s8[:, LANES:2 * LANES], s16[:, :LANES]),
        s16[:, LANES:],
    ]
    return jnp.concatenate(tiles, axis=1)


def _pool_window_sums_planes(planes):
    n = len(planes)

    def doubled(prev, k, first):
        return [prev[j] + prev[j - k] if j >= first else None for j in range(n)]

    lane = lax.broadcasted_iota(jnp.int32, (1, LANES), 1)
    s2 = doubled(planes, 1, 1)
    s4 = doubled([None if a is None else a[:, LANES:] for a in s2], 2, 3)
    s8 = doubled([None if a is None else a[:, 2 * LANES:] for a in s4], 4, 7)
    s16 = doubled([None if a is None else a[:, LANES:] for a in s8], 8, POOL_PAD)
    out = []
    for j in range(POOL_PAD, n):
        out.append(jnp.concatenate([
            s2[j][:, :LANES],
            jnp.where(lane < 64, s2[j][:, LANES:2 * LANES], s4[j][:, :LANES]),
            s4[j][:, LANES:2 * LANES],
            s8[j][:, :LANES],
            jnp.where(lane < 64, s8[j][:, LANES:2 * LANES], s16[j][:, :LANES]),
            s16[j][:, LANES:],
        ], axis=1))
    return out


def _pool_mix(d, wbd_ref):
    lo, hi = MXU_DIM, 2 * MXU_DIM
    return jnp.concatenate([
        _dot(d[:, :hi], wbd_ref[:hi, :lo]),
        _dot(d, wbd_ref[:, lo:hi]),
        _dot(d[:, lo:], wbd_ref[lo:, hi:]),
    ], axis=1)


def _head_masks():
    lane = lax.broadcasted_iota(jnp.int32, (1, MXU_DIM), 1)
    return [(lane // HEAD_DIM) == j for j in range(HEADS_PER_CHUNK)]


def _first8_mask():
    lane = lax.broadcasted_iota(jnp.int32, (1, LANES), 1)
    return (lane % HEAD_DIM) < (ROT_DIM // 2)


def _stack_heads(q, hmask):
    return jnp.concatenate([jnp.where(m, q, 0.0) for m in hmask], axis=0).astype(BF16)


def _unstack_heads(o, hmask):
    n = len(hmask)
    m = o.shape[0] // n
    acc = o[(n - 1) * m:, :]
    for h in range(n - 2, -1, -1):
        acc = jnp.where(hmask[h], o[h * m:(h + 1) * m, :], acc)
    return acc


def _sink_columns(sinks_ref, rows_per_head):
    hrow = lax.broadcasted_iota(jnp.int32, (HEADS_PER_CHUNK * rows_per_head, 1), 0) // rows_per_head
    cols = []
    for c in range(SWA_W // MXU_DIM):
        col = jnp.zeros((HEADS_PER_CHUNK * rows_per_head, 1), F32)
        for jj in range(HEADS_PER_CHUNK):
            col = jnp.where(hrow == jj, sinks_ref[jj * GROUP + c] * LOG2E, col)
        cols.append(col)
    return cols


def _softmax_weights(s, sink):
    m = jnp.max(s, axis=-1, keepdims=True)
    if sink is not None:
        m = jnp.maximum(m, sink)
    p = jnp.exp2(s - m)
    den = jnp.sum(p, axis=-1, keepdims=True)
    if sink is not None:
        den = den + jnp.exp2(sink - m)
    return p.astype(BF16), 1.0 / den


def _mem_attn(q, k_t, v_t, hmask):
    p, rden = _softmax_weights(_dot(_stack_heads(q, hmask), k_t), None)
    return _unstack_heads(_dot_nt(p, v_t) * rden, hmask)


def _swa_block(qb, kc, vc, bias, sink_cols, hmask):
    outs = []
    for g in range(GROUP):
        s = _dot_nt(_stack_heads(qb[:, g * MXU_DIM:(g + 1) * MXU_DIM], hmask), kc)
        s = (s.reshape(HEADS_PER_CHUNK, WINDOW, 2 * WINDOW) + bias[None]).reshape(HEADS_PER_CHUNK * WINDOW, 2 * WINDOW)
        p, rden = _softmax_weights(s, sink_cols[g])
        outs.append(_unstack_heads(_dot(p, vc) * rden, hmask))
    return jnp.concatenate(outs, axis=1)


def _const_spec(shape):
    nd = len(shape)
    return pl.BlockSpec(shape, lambda i: (0,) * nd, pipeline_mode=pl.Buffered(1))


def _stack_heads_seq(q, n_seq, hmask):
    q3 = q.reshape(n_seq, DEC_SEQ, MXU_DIM)
    return jnp.stack([jnp.where(m, q3, 0.0) for m in hmask], axis=1)


def _unstack_heads_seq(o, n_seq, hmask):
    o4 = o.reshape(n_seq, HEADS_PER_CHUNK, DEC_SEQ, MXU_DIM)
    acc = o4[:, HEADS_PER_CHUNK - 1]
    for h in range(HEADS_PER_CHUNK - 2, -1, -1):
        acc = jnp.where(hmask[h], o4[:, h], acc)
    return acc.reshape(n_seq * DEC_SEQ, MXU_DIM)


def _mem_attn_seqs(q, k_ref, v_ref, layer, n_seq, hmask):
    rows = HEADS_PER_CHUNK * DEC_SEQ
    qs = _stack_heads_seq(q, n_seq, hmask).reshape(n_seq * rows, MXU_DIM).astype(BF16)
    s = jnp.concatenate([_dot(qs[b * rows:(b + 1) * rows], k_ref[layer, b].astype(BF16))
                         for b in range(n_seq)], axis=0)
    p, rden = _softmax_weights(s, None)
    o = jnp.concatenate([_dot_nt(p[b * rows:(b + 1) * rows], v_ref[layer, b].astype(BF16))
                         for b in range(n_seq)], axis=0)
    return _unstack_heads_seq(o * rden, n_seq, hmask)


def _sample_pool(u_s, pref_ref, pool_ref, wcol, parity, u_scr, d_scr):
    sb = STEP_SEQS
    n_lt = POOL_W // LANES
    for c in range(n_lt):
        u_scr[c] = u_s[:, c * LANES:(c + 1) * LANES]
    planes = [jnp.where(parity == 0, pref_ref[t, 0:sb, :], pref_ref[t, sb:2 * sb, :]) for t in range(POOL_PAD)]
    planes += [jnp.concatenate([u_scr[c, pl.ds(t, sb, stride=DEC_SEQ), :] for c in range(n_lt)], axis=1)
               for t in range(DEC_SEQ)]
    for t, win in enumerate(_pool_window_sums_planes(planes)):
        cnt = jnp.minimum(float(PAST_LEN + t + 1), wcol)
        d_t = win / cnt - planes[POOL_PAD + t]
        for c in range(n_lt):
            d_scr[c, pl.ds(t, sb, stride=DEC_SEQ), :] = d_t[:, c * LANES:(c + 1) * LANES]

    @pl.when(parity == 0)
    def _():
        for t in range(POOL_PAD):
            pool_ref[t, 0:sb, :] = planes[DEC_SEQ + t]

    @pl.when(parity == 1)
    def _():
        for t in range(POOL_PAD):
            pool_ref[t, sb:2 * sb, :] = planes[DEC_SEQ + t]

    return jnp.concatenate([d_scr[c] for c in range(n_lt)], axis=1)


def _sample_cache_update(k_s, v_s, ck_ref, cv_ref, ko_ref, vo_ref):
    sb = STEP_SEQS
    keep = WINDOW - DEC_SEQ
    key_lane = lax.broadcasted_iota(jnp.int32, (1, WINDOW), 1)
    pad_rows = jnp.zeros((WINDOW - sb * DEC_SEQ, KV_W), F32)
    for new_rows, c_ref, o_ref in ((k_s, ck_ref, ko_ref), (v_s, cv_ref, vo_ref)):
        new_t = jnp.concatenate([new_rows, pad_rows], axis=0).T
        for b in range(sb):
            shifted = pltpu.roll(c_ref[b], keep, 1)
            placed = pltpu.roll(new_t, (keep - b * DEC_SEQ) % WINDOW, 1)
            o_ref[b] = jnp.where(key_lane < keep, shifted, placed)


def _sample_window_attn(q_s, k_s, v_s, ck_ref, cv_ref, sinks_ref, hmask):
    sb = STEP_SEQS
    m = sb * DEC_SEQ
    rows = N_Q_HEADS * DEC_SEQ
    qs = jnp.stack([_stack_heads_seq(q_s[:, g * MXU_DIM:(g + 1) * MXU_DIM], sb, hmask) for g in range(GROUP)],
                   axis=1)
    qs = qs.reshape(sb * rows, KV_W).astype(BF16)
    s_old = jnp.concatenate([_dot(qs[b * rows:(b + 1) * rows], ck_ref[b].astype(BF16)) for b in range(sb)], axis=0)
    s_new = _dot_nt(qs, k_s.astype(BF16))
    tq_old = lax.broadcasted_iota(jnp.int32, (rows, WINDOW), 0) % DEC_SEQ
    key_old = lax.broadcasted_iota(jnp.int32, (rows, WINDOW), 1)
    bias_old = jnp.where(key_old > tq_old, 0.0, NEG_INF)
    row_i = lax.broadcasted_iota(jnp.int32, (sb * rows, m), 0)
    col_i = lax.broadcasted_iota(jnp.int32, (sb * rows, m), 1)
    same_seq = (row_i // rows) == (col_i // DEC_SEQ)
    bias_new = jnp.where(same_seq, jnp.where(col_i % DEC_SEQ <= row_i % DEC_SEQ, 0.0, NEG_INF), NEG_INF)
    s_old = (s_old.reshape(sb, rows, WINDOW) + bias_old[None]).reshape(sb * rows, WINDOW)
    s_new = s_new + bias_new
    sink = jnp.concatenate(_sink_columns(sinks_ref, DEC_SEQ), axis=0)
    sink = jnp.concatenate([sink] * sb, axis=0)
    mx = jnp.maximum(jnp.maximum(jnp.max(s_old, axis=-1, keepdims=True), jnp.max(s_new, axis=-1, keepdims=True)), sink)
    p_old = jnp.exp2(s_old - mx)
    p_new = jnp.exp2(s_new - mx)
    den = (jnp.sum(p_old, axis=-1, keepdims=True) + jnp.sum(p_new, axis=-1, keepdims=True) + jnp.exp2(sink - mx))
    p_old = p_old.astype(BF16)
    o = jnp.concatenate([_dot_nt(p_old[b * rows:(b + 1) * rows], cv_ref[b].astype(BF16)) for b in range(sb)], axis=0)
    o = (o + _dot(p_new.astype(BF16), v_s.astype(BF16))) * (1.0 / den)
    o5 = o.reshape(sb, GROUP, HEADS_PER_CHUNK * DEC_SEQ, KV_W)
    return jnp.concatenate([_unstack_heads_seq(o5[:, g].reshape(sb * HEADS_PER_CHUNK * DEC_SEQ, KV_W), sb, hmask)
                            for g in range(GROUP)], axis=1)


def _step_kernel(sinks_ref, xp_ref, xs_ref, pref_ref, ck_ref, cv_ref, cmk_ref, cmv_ref,
                 rope_base_ref, rope_res_ref, rope_s_ref,
                 vec_ref, w_ref, wbd_ref, wcol_ref, mkv_ref, bseg_ref,
                 yp_ref, pool_p_ref, kop_ref, vop_ref, ys_ref, pool_s_ref, kos_ref, vos_ref,
                 ucarry, kprev, vprev, u_scr, d_scr):
    tq = PROMPT_TILE
    sb = STEP_SEQS
    m = sb * DEC_SEQ
    i = pl.program_id(0)

    @pl.when(i == 0)
    def _():
        ucarry[0] = jnp.zeros((CARRY_ROWS, POOL_W), F32)
        kprev[0] = jnp.zeros((WINDOW, KV_W), BF16)
        vprev[0] = jnp.zeros((WINDOW, KV_W), BF16)

    rd = i % 2
    wr = (i + 1) % 2

    hmask = _head_masks()
    first8 = _first8_mask()
    bseg = bseg_ref[...]
    x = jnp.concatenate([xp_ref[...], xs_ref[...].reshape(m, D_MODEL)], axis=0)

    base = rope_base_ref[i]
    cb, sb_ = base[0:1, :], base[1:2, :]
    cr, sr = rope_res_ref[0], rope_res_ref[1]
    cos_t = jnp.concatenate([cb * cr - sb_ * sr, rope_s_ref[0]], axis=0)
    sin_t = jnp.concatenate([sb_ * cr + cb * sr, rope_s_ref[1]], axis=0)

    xn = (_rms_unit(x) * _vec(vec_ref, "norm_a")).astype(BF16)
    u = _dot(xn, _wcols(w_ref, "in_a", 0, POOL_W))
    gp = _dot(xn, _wcols(w_ref, "in_a", POOL_W, 2 * POOL_W))
    qm = _dot(xn, _wcols(w_ref, "in_a", 2 * POOL_W, 2 * POOL_W + MEM_W))
    gm = _dot(xn, _wcols(w_ref, "in_a", 2 * POOL_W + MEM_W))
    u_p = u[:tq]
    u_hist = jnp.concatenate([ucarry[rd], u_p], axis=0)
    win = _pool_window_sums(u_hist)[CARRY_ROWS:]
    pos = i * tq + lax.broadcasted_iota(jnp.int32, (tq, 1), 0)
    cnt = jnp.minimum((pos + 1).astype(F32), wcol_ref[...])
    d_p = win / cnt - u_p
    d_s = _sample_pool(u[tq:], pref_ref, pool_s_ref, wcol_ref[...], rd, u_scr, d_scr)
    d = jnp.concatenate([d_p, d_s], axis=0)
    yp = _pool_mix(d.astype(BF16), wbd_ref) * _vec(vec_ref, "pool_scale") * _silu(gp)
    qmn = _head_rms(qm, _vec(vec_ref, "mem_q_gain0"), bseg) * Q_SCALE
    ym = jnp.concatenate([_mem_attn(qmn[:tq], mkv_ref[0, 0], mkv_ref[0, 1], hmask),
                          _mem_attn_seqs(qmn[tq:], cmk_ref, cmv_ref, 0, sb, hmask)], axis=0) * _silu(gm)
    x1 = x + _dot(jnp.concatenate([yp, ym], axis=1).astype(BF16), _wcols(w_ref, "out_a"))
    ucarry[wr] = u_p[tq - CARRY_ROWS:, :]

    r = _rms_unit(x1)
    kv = _dot((r * _vec(vec_ref, "kv_norm")).astype(BF16), _wcols(w_ref, "kv"))
    k = _rope(_head_rms(kv[:, :KV_W], _vec(vec_ref, "k_gain"), bseg), cos_t, sin_t, first8)
    v = kv[:, KV_W:]
    k_p, v_p, k_s, v_s = k[:tq], v[:tq], k[tq:], v[tq:]
    k_all = jnp.concatenate([kprev[rd], k_p.astype(BF16)], axis=0)
    v_all = jnp.concatenate([vprev[rd], v_p.astype(BF16)], axis=0)
    _sample_cache_update(k_s, v_s, ck_ref, cv_ref, kos_ref, vos_ref)

    xb = (r * _vec(vec_ref, "norm_b")).astype(BF16)
    zq = _dot(xb, _wcols(w_ref, "q"))
    gq = _dot(xb, _wcols(w_ref, "q_gate"))
    qm2 = _dot(xb, _wcols(w_ref, "mem_b", 0, MEM_W))
    gm2 = _dot(xb, _wcols(w_ref, "mem_b", MEM_W))
    q = _rope(_head_rms(zq, _vec(vec_ref, "q_gain"), bseg), cos_t, sin_t, first8) * Q_SCALE

    qi = lax.broadcasted_iota(jnp.int32, (WINDOW, 2 * WINDOW), 0)
    ci = lax.broadcasted_iota(jnp.int32, (WINDOW, 2 * WINDOW), 1)
    band_bias = jnp.where(ci > qi, jnp.where(ci <= qi + WINDOW, 0.0, NEG_INF), NEG_INF)
    sink_cols = _sink_columns(sinks_ref, WINDOW)
    ys_blocks = []
    for b in range(tq // WINDOW):
        bias = band_bias
        if b == 0:
            key_pos = ci + (i * tq - WINDOW)
            bias = jnp.where(key_pos >= 0, band_bias, NEG_INF)
        qb = q[b * WINDOW:(b + 1) * WINDOW, :]
        kc = k_all[b * WINDOW:(b + 2) * WINDOW, :]
        vc = v_all[b * WINDOW:(b + 2) * WINDOW, :]
        ys_blocks.append(_swa_block(qb, kc, vc, bias, sink_cols, hmask))
    ys_blocks.append(_sample_window_attn(q[tq:], k_s, v_s, ck_ref, cv_ref, sinks_ref, hmask))
    ys = jnp.concatenate(ys_blocks, axis=0) * _silu(gq)
    qmn2 = _head_rms(qm2, _vec(vec_ref, "mem_q_gain1"), bseg) * Q_SCALE
    ym2 = jnp.concatenate([_mem_attn(qmn2[:tq], mkv_ref[1, 0], mkv_ref[1, 1], hmask),
                           _mem_attn_seqs(qmn2[tq:], cmk_ref, cmv_ref, 1, sb, hmask)], axis=0) * _silu(gm2)
    ob, ob_w = _W_OFFSET["out_b"]
    y = (x1 + _dot(ys.astype(BF16), w_ref[0:SWA_W, ob:ob + ob_w])
         + _dot(ym2.astype(BF16), w_ref[SWA_W:, ob:ob + ob_w]))
    yp_ref[...] = y[:tq]
    ys_ref[...] = y[tq:].reshape(sb, DEC_SEQ, D_MODEL)

    kprev[wr] = k_all[tq:, :]
    vprev[wr] = v_all[tq:, :]

    @pl.when(i == pl.num_programs(0) - 1)
    def _():
        kop_ref[...] = k_p[tq - WINDOW:, :].T
        vop_ref[...] = v_p[tq - WINDOW:, :].T
        pool_p_ref[...] = pltpu.roll(u_p[tq - CARRY_ROWS:, :], CARRY_ROWS - 1, 0)[0:POOL_PAD, :]


def _step_call(sinks, x_p, x_s, pref, ck, cv, cmk_t, cmv_t, rope_base, rope_res, rope_s, wts):
    tq = PROMPT_TILE
    sb = STEP_SEQS
    n = x_p.shape[0]
    nb = x_s.shape[0]
    assert n // tq == nb // sb
    row_spec = lambda w: pl.BlockSpec((tq, w), lambda i: (i, 0))
    seq3 = lambda a, b_: pl.BlockSpec((sb, a, b_), lambda i: (i, 0, 0))
    pool_s_spec = pl.BlockSpec((POOL_PAD, 2 * sb, POOL_W), lambda i: (0, i // 2, 0))
    cache_mem_spec = pl.BlockSpec((DEPTH, sb, MEM_W, N_MEM), lambda i: (0, i, 0, 0))
    in_specs = [pl.BlockSpec(memory_space=pltpu.SMEM), row_spec(D_MODEL),
                seq3(DEC_SEQ, D_MODEL), pool_s_spec, seq3(KV_W, WINDOW), seq3(KV_W, WINDOW),
                cache_mem_spec, cache_mem_spec,
                _const_spec(rope_base.shape), _const_spec(rope_res.shape), _const_spec(rope_s.shape)]
    in_specs += [_const_spec(w.shape) for w in wts]
    out_shape = [jax.ShapeDtypeStruct((n, D_MODEL), F32),
                 jax.ShapeDtypeStruct((POOL_PAD, POOL_W), F32),
                 jax.ShapeDtypeStruct((KV_W, WINDOW), F32),
                 jax.ShapeDtypeStruct((KV_W, WINDOW), F32),
                 jax.ShapeDtypeStruct((nb, DEC_SEQ, D_MODEL), F32),
                 jax.ShapeDtypeStruct((POOL_PAD, nb, POOL_W), F32),
                 jax.ShapeDtypeStruct((nb, KV_W, WINDOW), F32),
                 jax.ShapeDtypeStruct((nb, KV_W, WINDOW), F32)]
    out_specs = [row_spec(D_MODEL),
                 pl.BlockSpec((POOL_PAD, POOL_W), lambda i: (0, 0)),
                 pl.BlockSpec((KV_W, WINDOW), lambda i: (0, 0)),
                 pl.BlockSpec((KV_W, WINDOW), lambda i: (0, 0)),
                 seq3(DEC_SEQ, D_MODEL), pool_s_spec, seq3(KV_W, WINDOW), seq3(KV_W, WINDOW)]
    return pl.pallas_call(
        _step_kernel,
        grid=(n // tq,),
        in_specs=in_specs,
        out_specs=out_specs,
        out_shape=out_shape,
        scratch_shapes=[pltpu.VMEM((2, CARRY_ROWS, POOL_W), F32),
                        pltpu.VMEM((2, WINDOW, KV_W), BF16),
                        pltpu.VMEM((2, WINDOW, KV_W), BF16),
                        pltpu.VMEM((POOL_W // LANES, sb * DEC_SEQ, LANES), F32),
                        pltpu.VMEM((POOL_W // LANES, sb * DEC_SEQ, LANES), F32)],
        compiler_params=pltpu.CompilerParams(dimension_semantics=("arbitrary",),
                                             vmem_limit_bytes=STEP_VMEM_LIMIT),
        name="step",
    )(sinks, x_p, x_s, pref, ck, cv, cmk_t, cmv_t, rope_base, rope_res, rope_s, *wts)


PREP_ROWS = N_KV_HEADS * HEAD_DIM
PREP_VMEM_LIMIT = 40 * 1024 * 1024


def _cols_group_major(w):
    heads = [w[:, (kvh * GROUP + g) * HEAD_DIM:(kvh * GROUP + g + 1) * HEAD_DIM]
             for g in range(GROUP) for kvh in range(N_KV_HEADS)]
    return jnp.concatenate(heads, axis=1)


def _prep_kernel(w_in_a_ref, w_out_a_ref, w_kv_ref, w_in_b_ref, wob_h0_ref, wob_h1_ref, wob_h2_ref, wob_h3_ref,
                 wob_m_ref, w_mem_ref, mix_ref,
                 par_ref, mem_ref,
                 w_ref, wbd_ref, vec_ref, mk_ref, mv_ref, mkvb_ref,
                 mem_xn, mem_acc):
    i = pl.program_id(0)

    def put(name, value):
        start, width = _W_OFFSET[name]
        w_ref[:, start:start + width] = value.astype(BF16)

    def par(name):
        start, width = _PAR_OFFSET[name]
        return par_ref[:, start:start + width]

    def per_head(g_row, width):
        return jnp.concatenate([g_row] * (width // HEAD_DIM), axis=1)

    @pl.when(i == 0)
    def _():
        n_grp = len(POOL_WINDOWS)
        for g in range(n_grp):
            pieces = [jnp.zeros((POOL_GW, POOL_GW), F32)] * n_grp
            pieces[g] = mix_ref[g]
            wbd_ref[g * POOL_GW:(g + 1) * POOL_GW, :] = jnp.concatenate(pieces, axis=1).astype(BF16)

        vec_pieces = {"norm_a": par("norm_a"), "pool_scale": par("pool_scale"), "kv_norm": par("kv_norm"),
                      "k_gain": per_head(par("k_norm"), KV_W), "norm_b": par("norm_b"),
                      "q_gain": per_head(par("q_norm"), SWA_W),
                      "mem_q_gain0": per_head(par("mem_q_norm0"), MEM_W),
                      "mem_q_gain1": per_head(par("mem_q_norm1"), MEM_W)}
        for name, (start, width) in _VEC_OFFSET.items():
            vec_ref[:, start:start + width] = vec_pieces[name]

        mem_unit = _rms_unit(mem_ref[...])
        for l in range(DEPTH):
            mem_xn[l] = (mem_unit * par("mem_norm%d" % l)).astype(BF16)
        mem_acc[...] = jnp.zeros_like(mem_acc)

    put("in_a", w_in_a_ref[...])
    put("out_a", w_out_a_ref[...])
    put("kv", w_kv_ref[...])
    w_in_b = w_in_b_ref[...]
    put("q", _cols_group_major(w_in_b[:, :SWA_W]))
    put("q_gate", _cols_group_major(w_in_b[:, SWA_W:2 * SWA_W]))
    put("mem_b", w_in_b[:, 2 * SWA_W:])
    heads = jnp.concatenate([wob_h0_ref[0], wob_h1_ref[0], wob_h2_ref[0], wob_h3_ref[0]], axis=0)
    put("out_b", jnp.where(i < GROUP, heads, wob_m_ref[...]))
    n_slabs = D_MODEL // PREP_ROWS
    for j in range(n_slabs):
        @pl.when(i == j)
        def _(j=j):
            for l in range(DEPTH):
                mem_acc[l] += _dot(mem_xn[l, :, j * PREP_ROWS:(j + 1) * PREP_ROWS], w_mem_ref[l].astype(BF16))

    @pl.when(i == n_slabs - 1)
    def _():
        for l in range(DEPTH):
            kv = mem_acc[l]
            k = kv[:, :MEM_W]
            v_t = kv[:, MEM_W:].T
            k3 = k.T.reshape(MEM_HEADS, HEAD_DIM, N_MEM)
            ms = jnp.mean(k3 * k3, axis=1, keepdims=True)
            k_gained = (k * per_head(par("mem_k_norm%d" % l), MEM_W)).T.reshape(MEM_HEADS, HEAD_DIM, N_MEM)
            k_t = (k_gained * lax.rsqrt(ms + EPS)).reshape(MEM_W, N_MEM)
            mk_ref[l] = k_t
            mv_ref[l] = v_t
            mkvb_ref[l, 0] = k_t.astype(BF16)
            mkvb_ref[l, 1] = v_t.astype(BF16)


def _prep_call(w_in_a, w_out_a, w_kv, w_in_b, w_out_b, w_mem_kv, pool_mix, params, mem):
    rows = lambda w: pl.BlockSpec((PREP_ROWS, w), lambda i: (i, 0))

    def head_src(kvh):
        return pl.BlockSpec((1, HEAD_DIM, D_MODEL),
                            lambda i: (kvh * GROUP + jnp.minimum(i, GROUP - 1), 0, 0))

    in_specs = [rows(2 * POOL_W + 2 * MEM_W), rows(D_MODEL), rows(2 * KV_W), rows(2 * SWA_W + 2 * MEM_W)]
    in_specs += [head_src(kvh) for kvh in range(N_KV_HEADS)]
    in_specs += [pl.BlockSpec((PREP_ROWS, D_MODEL), lambda i: (SWA_W // PREP_ROWS, 0)),
                pl.BlockSpec((DEPTH, PREP_ROWS, 2 * MEM_W), lambda i: (0, i, 0)),
                pl.BlockSpec(pool_mix.shape, lambda i: (0, 0, 0))]
    in_specs += [pl.BlockSpec(params.shape, lambda i: (0, 0)), pl.BlockSpec(mem.shape, lambda i: (0, 0))]
    n_vec = sum(w for _, w in _VEC_WIDTHS)
    mem_f = jax.ShapeDtypeStruct((DEPTH, MEM_W, N_MEM), F32)
    out_specs = [rows(W_ALL_COLS),
                 pl.BlockSpec((POOL_W, POOL_W), lambda i: (0, 0)),
                 pl.BlockSpec((1, n_vec), lambda i: (0, 0)),
                 pl.BlockSpec(mem_f.shape, lambda i: (0, 0, 0)), pl.BlockSpec(mem_f.shape, lambda i: (0, 0, 0)),
                 pl.BlockSpec((DEPTH, 2, MEM_W, N_MEM), lambda i: (0, 0, 0, 0))]
    bf = lambda *shape: jax.ShapeDtypeStruct(shape, BF16)
    out_shape = [bf(D_MODEL, W_ALL_COLS), bf(POOL_W, POOL_W),
                 jax.ShapeDtypeStruct((1, n_vec), F32), mem_f, mem_f, bf(DEPTH, 2, MEM_W, N_MEM)]
    w_out_b_heads = w_out_b.reshape(D_MODEL // HEAD_DIM, HEAD_DIM, D_MODEL)
    return pl.pallas_call(
        _prep_kernel,
        grid=(D_MODEL // PREP_ROWS,),
        in_specs=in_specs,
        out_specs=out_specs,
        out_shape=out_shape,
        scratch_shapes=[pltpu.VMEM((DEPTH, N_MEM, D_MODEL), BF16), pltpu.VMEM((DEPTH, N_MEM, 2 * MEM_W), F32)],
        compiler_params=pltpu.CompilerParams(dimension_semantics=("arbitrary",),
                                             vmem_limit_bytes=PREP_VMEM_LIMIT),
        name="prep",
    )(w_in_a, w_out_a, w_kv, w_in_b, *([w_out_b_heads] * N_KV_HEADS), w_out_b, w_mem_kv, pool_mix, params, mem)


def _rope_lane_tables(pos):
    half = ROT_DIM // 2
    inv = (ROPE_THETA ** (-np.arange(half, dtype=np.float32) * 2.0 / ROT_DIM)).astype(np.float32)
    ang = (np.asarray(pos, np.float32)[:, None] * inv[None, :]).astype(np.float64)
    cos, sin = np.cos(ang), np.sin(ang)
    t = ang.shape[0]
    rest = HEAD_DIM - ROT_DIM
    cos64 = np.concatenate([cos, cos, np.ones((t, rest))], axis=1)
    sin64 = np.concatenate([-sin, sin, np.zeros((t, rest))], axis=1)
    reps = LANES // HEAD_DIM
    return np.tile(cos64, (1, reps)).astype(np.float32), np.tile(sin64, (1, reps)).astype(np.float32)


def kernel(x_prompt, x_sample, state_pool, cache_swa_k, cache_swa_v, cache_mem_k, cache_mem_v, mem_prompt,
           norm_a, w_in_a, pool_mix_w, pool_scale, w_out_a, kv_norm, w_kv, k_norm,
           norm_b, w_in_b, q_norm, sinks, w_out_b, mem_norm, w_mem_kv, mem_q_norm, mem_k_norm):
    seg = np.arange(MXU_DIM) // HEAD_DIM
    bseg = jnp.asarray((seg[:, None] == seg[None, :]).astype(np.float32) / HEAD_DIM, BF16)
    wcol = jnp.asarray(np.repeat(np.asarray(POOL_WINDOWS, np.float32), POOL_GW).reshape(1, POOL_W))

    raw = {"norm_a": norm_a[0], "pool_scale": pool_scale[0], "kv_norm": kv_norm, "norm_b": norm_b[0],
           "mem_norm0": mem_norm[0], "mem_norm1": mem_norm[1], "k_norm": k_norm, "q_norm": q_norm[0],
           "mem_q_norm0": mem_q_norm[0], "mem_q_norm1": mem_q_norm[1],
           "mem_k_norm0": mem_k_norm[0], "mem_k_norm1": mem_k_norm[1]}
    params = jnp.concatenate([raw[name].astype(F32) for name, _ in _PAR_WIDTHS])
    w_all, wbd, vecs, mk_t, mv_t, mkv_b = _prep_call(
        w_in_a[0], w_out_a[0], w_kv, w_in_b[0], w_out_b[0], w_mem_kv, pool_mix_w[0], params.reshape(1, _PAR_COLS),
        mem_prompt[0])
    wts = (vecs, w_all, wbd, wcol, mkv_b, bseg)

    n_tiles = SEQ // PROMPT_TILE
    cb, sb_ = _rope_lane_tables(np.arange(n_tiles) * PROMPT_TILE)
    cr, sr = _rope_lane_tables(np.arange(PROMPT_TILE))
    rope_base = jnp.asarray(np.stack([cb, sb_], axis=1))
    rope_res = jnp.asarray(np.stack([cr, sr], axis=0))
    cos_s, sin_s = _rope_lane_tables(PAST_LEN + np.arange(DEC_SEQ))
    rope_s = jnp.asarray(np.stack([np.tile(cos_s, (STEP_SEQS, 1)), np.tile(sin_s, (STEP_SEQS, 1))], axis=0))
    pref = jnp.transpose(state_pool[0], (1, 0, 2))
    cmk_t = jnp.transpose(cache_mem_k, (0, 1, 3, 4, 2)).reshape(DEPTH, DEC_BATCH, MEM_W, N_MEM)
    cmv_t = jnp.transpose(cache_mem_v, (0, 1, 3, 4, 2)).reshape(DEPTH, DEC_BATCH, MEM_W, N_MEM)
    ck_t = jnp.transpose(cache_swa_k, (0, 2, 3, 1)).reshape(DEC_BATCH, KV_W, WINDOW)
    cv_t = jnp.transpose(cache_swa_v, (0, 2, 3, 1)).reshape(DEC_BATCH, KV_W, WINDOW)
    y_p, pool_p, k_p, v_p, y_s, pool_s, k_s, v_s = _step_call(
        sinks[0].astype(F32), x_prompt[0], x_sample, pref, ck_t, cv_t, cmk_t, cmv_t, rope_base, rope_res, rope_s, wts)

    def mem_out(a):
        return jnp.transpose(a.reshape(DEPTH, 1, MEM_HEADS, HEAD_DIM, N_MEM), (0, 1, 4, 2, 3))

    def swa_out_t(a):
        return jnp.transpose(a.reshape(-1, N_KV_HEADS, HEAD_DIM, WINDOW), (0, 3, 1, 2))

    return (y_p[None], y_s, pool_p[None, None], jnp.transpose(pool_s, (1, 0, 2))[None],
            swa_out_t(k_p[None]), swa_out_t(v_p[None]), swa_out_t(k_s), swa_out_t(v_s),
            mem_out(mk_t), mem_out(mv_t))
```

```python
import jax
import jax.numpy as jnp
import numpy as np
from jax import lax
from jax.experimental import pallas as pl
from jax.experimental.pallas import tpu as pltpu

D_MODEL = 1024
SEQ = 16384
DEPTH = 2
DEC_BATCH = 128
DEC_SEQ = 8
PAST_LEN = 16384
HEAD_DIM = 64
POOL_W = 768
POOL_WINDOWS = (2, 4, 8, 16)
POOL_GW = 192
POOL_PAD = 15
N_Q_HEADS = 12
N_KV_HEADS = 4
GROUP = 3
SWA_W = 768
KV_W = 256
WINDOW = 128
N_MEM = 256
MEM_HEADS = 4
MEM_W = 256
ROT_DIM = 16
ROPE_THETA = 500000.0
EPS = 1e-6

F32 = jnp.float32
BF16 = jnp.bfloat16
NEG_INF = float("-inf")
LOG2E = 1.4426950408889634
Q_SCALE = HEAD_DIM ** -0.5 * LOG2E

LANES = 128
MXU_DIM = 256
HEADS_PER_CHUNK = MXU_DIM // HEAD_DIM
CARRY_ROWS = 16
PROMPT_TILE = 512
STEP_SEQS = DEC_BATCH // (SEQ // PROMPT_TILE)
STEP_VMEM_LIMIT = 60 * 1024 * 1024


_VEC_WIDTHS = (("norm_a", D_MODEL), ("pool_scale", POOL_W), ("kv_norm", D_MODEL), ("k_gain", KV_W),
               ("norm_b", D_MODEL), ("q_gain", SWA_W), ("mem_q_gain0", MEM_W), ("mem_q_gain1", MEM_W))
_VEC_OFFSET = {}
for _name, _width in _VEC_WIDTHS:
    _VEC_OFFSET[_name] = (sum(w for _, w in _VEC_WIDTHS[:len(_VEC_OFFSET)]), _width)


def _vec(vec_ref, name):
    start, width = _VEC_OFFSET[name]
    return vec_ref[:, start:start + width]


_PAR_WIDTHS = (("norm_a", D_MODEL), ("pool_scale", POOL_W), ("kv_norm", D_MODEL), ("norm_b", D_MODEL),
               ("mem_norm0", D_MODEL), ("mem_norm1", D_MODEL), ("k_norm", HEAD_DIM), ("q_norm", HEAD_DIM),
               ("mem_q_norm0", HEAD_DIM), ("mem_q_norm1", HEAD_DIM), ("mem_k_norm0", HEAD_DIM),
               ("mem_k_norm1", HEAD_DIM))
_PAR_OFFSET = {}
_PAR_COLS = 0
for _name, _width in _PAR_WIDTHS:
    _PAR_OFFSET[_name] = (_PAR_COLS, _width)
    _PAR_COLS += _width


_W_WIDTHS = (("in_a", 2 * POOL_W + 2 * MEM_W), ("out_a", D_MODEL), ("kv", 2 * KV_W),
             ("q", SWA_W), ("q_gate", SWA_W), ("mem_b", 2 * MEM_W), ("out_b", D_MODEL))
_W_OFFSET = {}
for _name, _width in _W_WIDTHS:
    _W_OFFSET[_name] = (sum(w for _, w in _W_WIDTHS[:len(_W_OFFSET)]), _width)
W_ALL_COLS = sum(w for _, w in _W_WIDTHS)


def _wcols(w_ref, name, lo=0, hi=None):
    start, width = _W_OFFSET[name]
    return w_ref[:, start + lo:start + (width if hi is None else hi)]


def _dot(a, b):
    return jnp.dot(a, b, preferred_element_type=F32)


def _dot_nt(a, b):
    return lax.dot_general(a, b, (((1,), (1,)), ((), ())), preferred_element_type=F32)


def _rms_unit(x):
    return x * lax.rsqrt(jnp.mean(x * x, axis=-1, keepdims=True) + EPS)


def _silu(g):
    return g * (0.5 + 0.5 * jnp.tanh(0.5 * g))


def _head_rms(y, gain, bseg):
    parts = []
    for c in range(y.shape[1] // MXU_DIM):
        yc = y[:, c * MXU_DIM:(c + 1) * MXU_DIM]
        ms = _dot((yc * yc).astype(BF16), bseg)
        parts.append(yc * lax.rsqrt(ms + EPS) * gain[:, c * MXU_DIM:(c + 1) * MXU_DIM])
    return parts[0] if len(parts) == 1 else jnp.concatenate(parts, axis=1)


def _rope(y, cos_t, sin_t, first8):
    parts = []
    for c in range(y.shape[1] // LANES):
        yc = y[:, c * LANES:(c + 1) * LANES]
        partner = jnp.where(first8, pltpu.roll(yc, LANES - 8, 1), pltpu.roll(yc, 8, 1))
        parts.append(yc * cos_t + partner * sin_t)
    return parts[0] if len(parts) == 1 else jnp.concatenate(parts, axis=1)


def _pool_window_sums(u_hist):
    def back(a, k):
        return pltpu.roll(a, k, 0)

    lane = lax.broadcasted_iota(jnp.int32, (1, LANES), 1)
    s2 = u_hist + back(u_hist, 1)
    t = s2[:, LANES:]
    s4 = t + back(t, 2)
    t = s4[:, 2 * LANES:]
    s8 = t + back(t, 4)
    t = s8[:, LANES:]
    s16 = t + back(t, 8)
    tiles = [
        s2[:, :LANES],
        jnp.where(lane < 64, s2[:, LANES:2 * LANES], s4[:, :LANES]),
        s4[:, LANES:2 * LANES],
        s8[:, :LANES],
        jnp.where(lane < 64, s8[:, LANES:2 * LANES], s16[:, :LANES]),
        s16[:, LANES:],
    ]
    return jnp.concatenate(tiles, axis=1)


def _pool_window_sums_planes(planes):
    n = len(planes)

    def doubled(prev, k, first):
        return [prev[j] + prev[j - k] if j >= first else None for j in range(n)]

    lane = lax.broadcasted_iota(jnp.int32, (1, LANES), 1)
    s2 = doubled(planes, 1, 1)
    s4 = doubled([None if a is None else a[:, LANES:] for a in s2], 2, 3)
    s8 = doubled([None if a is None else a[:, 2 * LANES:] for a in s4], 4, 7)
    s16 = doubled([None if a is None else a[:, LANES:] for a in s8], 8, POOL_PAD)
    out = []
    for j in range(POOL_PAD, n):
        out.append(jnp.concatenate([
            s2[j][:, :LANES],
            jnp.where(lane < 64, s2[j][:, LANES:2 * LANES], s4[j][:, :LANES]),
            s4[j][:, LANES:2 * LANES],
            s8[j][:, :LANES],
            jnp.where(lane < 64, s8[j][:, LANES:2 * LANES], s16[j][:, :LANES]),
            s16[j][:, LANES:],
        ], axis=1))
    return out


def _pool_mix(d, wbd_ref):
    lo, hi = MXU_DIM, 2 * MXU_DIM
    return jnp.concatenate([
        _dot(d[:, :hi], wbd_ref[:hi, :lo]),
        _dot(d, wbd_ref[:, lo:hi]),
        _dot(d[:, lo:], wbd_ref[lo:, hi:]),
    ], axis=1)


def _head_masks():
    lane = lax.broadcasted_iota(jnp.int32, (1, MXU_DIM), 1)
    return [(lane // HEAD_DIM) == j for j in range(HEADS_PER_CHUNK)]


def _first8_mask():
    lane = lax.broadcasted_iota(jnp.int32, (1, LANES), 1)
    return (lane % HEAD_DIM) < (ROT_DIM // 2)


def _stack_heads(q, hmask):
    return jnp.concatenate([jnp.where(m, q, 0.0) for m in hmask], axis=0).astype(BF16)


def _unstack_heads(o, hmask):
    n = len(hmask)
    m = o.shape[0] // n
    acc = o[(n - 1) * m:, :]
    for h in range(n - 2, -1, -1):
        acc = jnp.where(hmask[h], o[h * m:(h + 1) * m, :], acc)
    return acc


def _sink_columns(sinks_ref, rows_per_head):
    hrow = lax.broadcasted_iota(jnp.int32, (HEADS_PER_CHUNK * rows_per_head, 1), 0) // rows_per_head
    cols = []
    for c in range(SWA_W // MXU_DIM):
        col = jnp.zeros((HEADS_PER_CHUNK * rows_per_head, 1), F32)
        for jj in range(HEADS_PER_CHUNK):
            col = jnp.where(hrow == jj, sinks_ref[jj * GROUP + c] * LOG2E, col)
        cols.append(col)
    return cols


def _softmax_weights(s, sink):
    m = jnp.max(s, axis=-1, keepdims=True)
    if sink is not None:
        m = jnp.maximum(m, sink)
    p = jnp.exp2(s - m)
    den = jnp.sum(p, axis=-1, keepdims=True)
    if sink is not None:
        den = den + jnp.exp2(sink - m)
    return p.astype(BF16), 1.0 / den


def _mem_attn(q, k_t, v_t, hmask):
    p, rden = _softmax_weights(_dot(_stack_heads(q, hmask), k_t), None)
    return _unstack_heads(_dot_nt(p, v_t) * rden, hmask)


def _swa_block(qb, kc, vc, bias, sink_cols, hmask):
    outs = []
    for g in range(GROUP):
        s = _dot_nt(_stack_heads(qb[:, g * MXU_DIM:(g + 1) * MXU_DIM], hmask), kc)
        s = (s.reshape(HEADS_PER_CHUNK, WINDOW, 2 * WINDOW) + bias[None]).reshape(HEADS_PER_CHUNK * WINDOW, 2 * WINDOW)
        p, rden = _softmax_weights(s, sink_cols[g])
        outs.append(_unstack_heads(_dot(p, vc) * rden, hmask))
    return jnp.concatenate(outs, axis=1)


def _const_spec(shape):
    nd = len(shape)
    return pl.BlockSpec(shape, lambda i: (0,) * nd, pipeline_mode=pl.Buffered(1))


def _stack_heads_seq(q, n_seq, hmask):
    q3 = q.reshape(n_seq, DEC_SEQ, MXU_DIM)
    return jnp.stack([jnp.where(m, q3, 0.0) for m in hmask], axis=1)


def _unstack_heads_seq(o, n_seq, hmask):
    o4 = o.reshape(n_seq, HEADS_PER_CHUNK, DEC_SEQ, MXU_DIM)
    acc = o4[:, HEADS_PER_CHUNK - 1]
    for h in range(HEADS_PER_CHUNK - 2, -1, -1):
        acc = jnp.where(hmask[h], o4[:, h], acc)
    return acc.reshape(n_seq * DEC_SEQ, MXU_DIM)


def _mem_attn_seqs(q, k_ref, v_ref, layer, n_seq, hmask):
    rows = HEADS_PER_CHUNK * DEC_SEQ
    qs = _stack_heads_seq(q, n_seq, hmask).reshape(n_seq * rows, MXU_DIM).astype(BF16)
    s = jnp.concatenate([_dot(qs[b * rows:(b + 1) * rows], k_ref[layer, b].astype(BF16))
                         for b in range(n_seq)], axis=0)
    p, rden = _softmax_weights(s, None)
    o = jnp.concatenate([_dot_nt(p[b * rows:(b + 1) * rows], v_ref[layer, b].astype(BF16))
                         for b in range(n_seq)], axis=0)
    return _unstack_heads_seq(o * rden, n_seq, hmask)


def _sample_pool(u_s, pref_ref, pool_ref, wcol, parity, u_scr, d_scr):
    sb = STEP_SEQS
    n_lt = POOL_W // LANES
    for c in range(n_lt):
        u_scr[c] = u_s[:, c * LANES:(c + 1) * LANES]
    planes = [jnp.where(parity == 0, pref_ref[t, 0:sb, :], pref_ref[t, sb:2 * sb, :]) for t in range(POOL_PAD)]
    planes += [jnp.concatenate([u_scr[c, pl.ds(t, sb, stride=DEC_SEQ), :] for c in range(n_lt)], axis=1)
               for t in range(DEC_SEQ)]
    for t, win in enumerate(_pool_window_sums_planes(planes)):
        cnt = jnp.minimum(float(PAST_LEN + t + 1), wcol)
        d_t = win / cnt - planes[POOL_PAD + t]
        for c in range(n_lt):
            d_scr[c, pl.ds(t, sb, stride=DEC_SEQ), :] = d_t[:, c * LANES:(c + 1) * LANES]

    @pl.when(parity == 0)
    def _():
        for t in range(POOL_PAD):
            pool_ref[t, 0:sb, :] = planes[DEC_SEQ + t]

    @pl.when(parity == 1)
    def _():
        for t in range(POOL_PAD):
            pool_ref[t, sb:2 * sb, :] = planes[DEC_SEQ + t]

    return jnp.concatenate([d_scr[c] for c in range(n_lt)], axis=1)


def _sample_cache_update(k_s, v_s, ck_ref, cv_ref, ko_ref, vo_ref):
    sb = STEP_SEQS
    keep = WINDOW - DEC_SEQ
    key_lane = lax.broadcasted_iota(jnp.int32, (1, WINDOW), 1)
    pad_rows = jnp.zeros((WINDOW - sb * DEC_SEQ, KV_W), F32)
    for new_rows, c_ref, o_ref in ((k_s, ck_ref, ko_ref), (v_s, cv_ref, vo_ref)):
        new_t = jnp.concatenate([new_rows, pad_rows], axis=0).T
        for b in range(sb):
            shifted = pltpu.roll(c_ref[b], keep, 1)
            placed = pltpu.roll(new_t, (keep - b * DEC_SEQ) % WINDOW, 1)
            o_ref[b] = jnp.where(key_lane < keep, shifted, placed)


def _sample_window_attn(q_s, k_s, v_s, ck_ref, cv_ref, sinks_ref, hmask):
    sb = STEP_SEQS
    m = sb * DEC_SEQ
    rows = N_Q_HEADS * DEC_SEQ
    qs = jnp.stack([_stack_heads_seq(q_s[:, g * MXU_DIM:(g + 1) * MXU_DIM], sb, hmask) for g in range(GROUP)],
                   axis=1)
    qs = qs.reshape(sb * rows, KV_W).astype(BF16)
    s_old = jnp.concatenate([_dot(qs[b * rows:(b + 1) * rows], ck_ref[b].astype(BF16)) for b in range(sb)], axis=0)
    s_new = _dot_nt(qs, k_s.astype(BF16))
    tq_old = lax.broadcasted_iota(jnp.int32, (rows, WINDOW), 0) % DEC_SEQ
    key_old = lax.broadcasted_iota(jnp.int32, (rows, WINDOW), 1)
    bias_old = jnp.where(key_old > tq_old, 0.0, NEG_INF)
    row_i = lax.broadcasted_iota(jnp.int32, (sb * rows, m), 0)
    col_i = lax.broadcasted_iota(jnp.int32, (sb * rows, m), 1)
    same_seq = (row_i // rows) == (col_i // DEC_SEQ)
    bias_new = jnp.where(same_seq, jnp.where(col_i % DEC_SEQ <= row_i % DEC_SEQ, 0.0, NEG_INF), NEG_INF)
    s_old = (s_old.reshape(sb, rows, WINDOW) + bias_old[None]).reshape(sb * rows, WINDOW)
    s_new = s_new + bias_new
    sink = jnp.concatenate(_sink_columns(sinks_ref, DEC_SEQ), axis=0)
    sink = jnp.concatenate([sink] * sb, axis=0)
    mx = jnp.maximum(jnp.maximum(jnp.max(s_old, axis=-1, keepdims=True), jnp.max(s_new, axis=-1, keepdims=True)), sink)
    p_old = jnp.exp2(s_old - mx)
    p_new = jnp.exp2(s_new - mx)
    den = (jnp.sum(p_old, axis=-1, keepdims=True) + jnp.sum(p_new, axis=-1, keepdims=True) + jnp.exp2(sink - mx))
    p_old = p_old.astype(BF16)
    o = jnp.concatenate([_dot_nt(p_old[b * rows:(b + 1) * rows], cv_ref[b].astype(BF16)) for b in range(sb)], axis=0)
    o = (o + _dot(p_new.astype(BF16), v_s.astype(BF16))) * (1.0 / den)
    o5 = o.reshape(sb, GROUP, HEADS_PER_CHUNK * DEC_SEQ, KV_W)
    return jnp.concatenate([_unstack_heads_seq(o5[:, g].reshape(sb * HEADS_PER_CHUNK * DEC_SEQ, KV_W), sb, hmask)
                            for g in range(GROUP)], axis=1)


def _step_kernel(sinks_ref, xp_ref, xs_ref, pref_ref, ck_ref, cv_ref, cmk_ref, cmv_ref,
                 rope_base_ref, rope_res_ref, rope_s_ref,
                 vec_ref, w_ref, wbd_ref, wcol_ref, mkv_ref, bseg_ref,
                 yp_ref, pool_p_ref, kop_ref, vop_ref, ys_ref, pool_s_ref, kos_ref, vos_ref,
                 ucarry, kprev, vprev, u_scr, d_scr):
    tq = PROMPT_TILE
    sb = STEP_SEQS
    m = sb * DEC_SEQ
    i = pl.program_id(0)

    @pl.when(i == 0)
    def _():
        ucarry[0] = jnp.zeros((CARRY_ROWS, POOL_W), F32)
        kprev[0] = jnp.zeros((WINDOW, KV_W), BF16)
        vprev[0] = jnp.zeros((WINDOW, KV_W), BF16)

    rd = i % 2
    wr = (i + 1) % 2

    hmask = _head_masks()
    first8 = _first8_mask()
    bseg = bseg_ref[...]
    x = jnp.concatenate([xp_ref[...], xs_ref[...].reshape(m, D_MODEL)], axis=0)

    base = rope_base_ref[i]
    cb, sb_ = base[0:1, :], base[1:2, :]
    cr, sr = rope_res_ref[0], rope_res_ref[1]
    cos_t = jnp.concatenate([cb * cr - sb_ * sr, rope_s_ref[0]], axis=0)
    sin_t = jnp.concatenate([sb_ * cr + cb * sr, rope_s_ref[1]], axis=0)

    xn = (_rms_unit(x) * _vec(vec_ref, "norm_a")).astype(BF16)
    u = _dot(xn, _wcols(w_ref, "in_a", 0, POOL_W))
    gp = _dot(xn, _wcols(w_ref, "in_a", POOL_W, 2 * POOL_W))
    qm = _dot(xn, _wcols(w_ref, "in_a", 2 * POOL_W, 2 * POOL_W + MEM_W))
    gm = _dot(xn, _wcols(w_ref, "in_a", 2 * POOL_W + MEM_W))
    u_p = u[:tq]
    u_hist = jnp.concatenate([ucarry[rd], u_p], axis=0)
    win = _pool_window_sums(u_hist)[CARRY_ROWS:]
    pos = i * tq + lax.broadcasted_iota(jnp.int32, (tq, 1), 0)
    cnt = jnp.minimum((pos + 1).astype(F32), wcol_ref[...])
    d_p = win / cnt - u_p
    d_s = _sample_pool(u[tq:], pref_ref, pool_s_ref, wcol_ref[...], rd, u_scr, d_scr)
    d = jnp.concatenate([d_p, d_s], axis=0)
    yp = _pool_mix(d.astype(BF16), wbd_ref) * _vec(vec_ref, "pool_scale") * _silu(gp)
    qmn = _head_rms(qm, _vec(vec_ref, "mem_q_gain0"), bseg) * Q_SCALE
    ym = jnp.concatenate([_mem_attn(qmn[:tq], mkv_ref[0, 0], mkv_ref[0, 1], hmask),
                          _mem_attn_seqs(qmn[tq:], cmk_ref, cmv_ref, 0, sb, hmask)], axis=0) * _silu(gm)
    x1 = x + _dot(jnp.concatenate([yp, ym], axis=1).astype(BF16), _wcols(w_ref, "out_a"))
    ucarry[wr] = u_p[tq - CARRY_ROWS:, :]

    r = _rms_unit(x1)
    kv = _dot((r * _vec(vec_ref, "kv_norm")).astype(BF16), _wcols(w_ref, "kv"))
    k = _rope(_head_rms(kv[:, :KV_W], _vec(vec_ref, "k_gain"), bseg), cos_t, sin_t, first8)
    v = kv[:, KV_W:]
    k_p, v_p, k_s, v_s = k[:tq], v[:tq], k[tq:], v[tq:]
    k_all = jnp.concatenate([kprev[rd], k_p.astype(BF16)], axis=0)
    v_all = jnp.concatenate([vprev[rd], v_p.astype(BF16)], axis=0)
    _sample_cache_update(k_s, v_s, ck_ref, cv_ref, kos_ref, vos_ref)

    xb = (r * _vec(vec_ref, "norm_b")).astype(BF16)
    zq = _dot(xb, _wcols(w_ref, "q"))
    gq = _dot(xb, _wcols(w_ref, "q_gate"))
    qm2 = _dot(xb, _wcols(w_ref, "mem_b", 0, MEM_W))
    gm2 = _dot(xb, _wcols(w_ref, "mem_b", MEM_W))
    q = _rope(_head_rms(zq, _vec(vec_ref, "q_gain"), bseg), cos_t, sin_t, first8) * Q_SCALE

    qi = lax.broadcasted_iota(jnp.int32, (WINDOW, 2 * WINDOW), 0)
    ci = lax.broadcasted_iota(jnp.int32, (WINDOW, 2 * WINDOW), 1)
    band_bias = jnp.where(ci > qi, jnp.where(ci <= qi + WINDOW, 0.0, NEG_INF), NEG_INF)
    sink_cols = _sink_columns(sinks_ref, WINDOW)
    ys_blocks = []
    for b in range(tq // WINDOW):
        bias = band_bias
        if b == 0:
            key_pos = ci + (i * tq - WINDOW)
            bias = jnp.where(key_pos >= 0, band_bias, NEG_INF)
        qb = q[b * WINDOW:(b + 1) * WINDOW, :]
        kc = k_all[b * WINDOW:(b + 2) * WINDOW, :]
        vc = v_all[b * WINDOW:(b + 2) * WINDOW, :]
        ys_blocks.append(_swa_block(qb, kc, vc, bias, sink_cols, hmask))
    ys_blocks.append(_sample_window_attn(q[tq:], k_s, v_s, ck_ref, cv_ref, sinks_ref, hmask))
    ys = jnp.concatenate(ys_blocks, axis=0) * _silu(gq)
    qmn2 = _head_rms(qm2, _vec(vec_ref, "mem_q_gain1"), bseg) * Q_SCALE
    ym2 = jnp.concatenate([_mem_attn(qmn2[:tq], mkv_ref[1, 0], mkv_ref[1, 1], hmask),
                           _mem_attn_seqs(qmn2[tq:], cmk_ref, cmv_ref, 1, sb, hmask)], axis=0) * _silu(gm2)
    ob, ob_w = _W_OFFSET["out_b"]
    y = (x1 + _dot(ys.astype(BF16), w_ref[0:SWA_W, ob:ob + ob_w])
         + _dot(ym2.astype(BF16), w_ref[SWA_W:, ob:ob + ob_w]))
    yp_ref[...] = y[:tq]
    ys_ref[...] = y[tq:].reshape(sb, DEC_SEQ, D_MODEL)

    kprev[wr] = k_all[tq:, :]
    vprev[wr] = v_all[tq:, :]

    @pl.when(i == pl.num_programs(0) - 1)
    def _():
        kop_ref[...] = k_p[tq - WINDOW:, :].T
        vop_ref[...] = v_p[tq - WINDOW:, :].T
        pool_p_ref[...] = pltpu.roll(u_p[tq - CARRY_ROWS:, :], CARRY_ROWS - 1, 0)[0:POOL_PAD, :]


def _step_call(sinks, x_p, x_s, pref, ck, cv, cmk_t, cmv_t, rope_base, rope_res, rope_s, wts):
    tq = PROMPT_TILE
    sb = STEP_SEQS
    n = x_p.shape[0]
    nb = x_s.shape[0]
    assert n // tq == nb // sb
    row_spec = lambda w: pl.BlockSpec((tq, w), lambda i: (i, 0))
    seq3 = lambda a, b_: pl.BlockSpec((sb, a, b_), lambda i: (i, 0, 0))
    pool_s_spec = pl.BlockSpec((POOL_PAD, 2 * sb, POOL_W), lambda i: (0, i // 2, 0))
    cache_mem_spec = pl.BlockSpec((DEPTH, sb, MEM_W, N_MEM), lambda i: (0, i, 0, 0))
    in_specs = [pl.BlockSpec(memory_space=pltpu.SMEM), row_spec(D_MODEL),
                seq3(DEC_SEQ, D_MODEL), pool_s_spec, seq3(KV_W, WINDOW), seq3(KV_W, WINDOW),
                cache_mem_spec, cache_mem_spec,
                _const_spec(rope_base.shape), _const_spec(rope_res.shape), _const_spec(rope_s.shape)]
    in_specs += [_const_spec(w.shape) for w in wts]
    out_shape = [jax.ShapeDtypeStruct((n, D_MODEL), F32),
                 jax.ShapeDtypeStruct((POOL_PAD, POOL_W), F32),
                 jax.ShapeDtypeStruct((KV_W, WINDOW), F32),
                 jax.ShapeDtypeStruct((KV_W, WINDOW), F32),
                 jax.ShapeDtypeStruct((nb, DEC_SEQ, D_MODEL), F32),
                 jax.ShapeDtypeStruct((POOL_PAD, nb, POOL_W), F32),
                 jax.ShapeDtypeStruct((nb, KV_W, WINDOW), F32),
                 jax.ShapeDtypeStruct((nb, KV_W, WINDOW), F32)]
    out_specs = [row_spec(D_MODEL),
                 pl.BlockSpec((POOL_PAD, POOL_W), lambda i: (0, 0)),
                 pl.BlockSpec((KV_W, WINDOW), lambda i: (0, 0)),
                 pl.BlockSpec((KV_W, WINDOW), lambda i: (0, 0)),
                 seq3(DEC_SEQ, D_MODEL), pool_s_spec, seq3(KV_W, WINDOW), seq3(KV_W, WINDOW)]
    return pl.pallas_call(
        _step_kernel,
        grid=(n // tq,),
        in_specs=in_specs,
        out_specs=out_specs,
        out_shape=out_shape,
        scratch_shapes=[pltpu.VMEM((2, CARRY_ROWS, POOL_W), F32),
                        pltpu.VMEM((2, WINDOW, KV_W), BF16),
                        pltpu.VMEM((2, WINDOW, KV_W), BF16),
                        pltpu.VMEM((POOL_W // LANES, sb * DEC_SEQ, LANES), F32),
                        pltpu.VMEM((POOL_W // LANES, sb * DEC_SEQ, LANES), F32)],
        compiler_params=pltpu.CompilerParams(dimension_semantics=("arbitrary",),
                                             vmem_limit_bytes=STEP_VMEM_LIMIT),
        name="step",
    )(sinks, x_p, x_s, pref, ck, cv, cmk_t, cmv_t, rope_base, rope_res, rope_s, *wts)


PREP_ROWS = N_KV_HEADS * HEAD_DIM
PREP_VMEM_LIMIT = 40 * 1024 * 1024


def _cols_group_major(w):
    heads = [w[:, (kvh * GROUP + g) * HEAD_DIM:(kvh * GROUP + g + 1) * HEAD_DIM]
             for g in range(GROUP) for kvh in range(N_KV_HEADS)]
    return jnp.concatenate(heads, axis=1)


def _prep_kernel(w_in_a_ref, w_out_a_ref, w_kv_ref, w_in_b_ref, wob_h0_ref, wob_h1_ref, wob_h2_ref, wob_h3_ref,
                 wob_m_ref, w_mem_ref, mix_ref,
                 par_ref, mem_ref,
                 w_ref, wbd_ref, vec_ref, mk_ref, mv_ref, mkvb_ref,
                 mem_xn, mem_acc):
    i = pl.program_id(0)

    def put(name, value):
        start, width = _W_OFFSET[name]
        w_ref[:, start:start + width] = value.astype(BF16)

    def par(name):
        start, width = _PAR_OFFSET[name]
        return par_ref[:, start:start + width]

    def per_head(g_row, width):
        return jnp.concatenate([g_row] * (width // HEAD_DIM), axis=1)

    @pl.when(i == 0)
    def _():
        n_grp = len(POOL_WINDOWS)
        for g in range(n_grp):
            pieces = [jnp.zeros((POOL_GW, POOL_GW), F32)] * n_grp
            pieces[g] = mix_ref[g]
            wbd_ref[g * POOL_GW:(g + 1) * POOL_GW, :] = jnp.concatenate(pieces, axis=1).astype(BF16)

        vec_pieces = {"norm_a": par("norm_a"), "pool_scale": par("pool_scale"), "kv_norm": par("kv_norm"),
                      "k_gain": per_head(par("k_norm"), KV_W), "norm_b": par("norm_b"),
                      "q_gain": per_head(par("q_norm"), SWA_W),
                      "mem_q_gain0": per_head(par("mem_q_norm0"), MEM_W),
                      "mem_q_gain1": per_head(par("mem_q_norm1"), MEM_W)}
        for name, (start, width) in _VEC_OFFSET.items():
            vec_ref[:, start:start + width] = vec_pieces[name]

        mem_unit = _rms_unit(mem_ref[...])
        for l in range(DEPTH):
            mem_xn[l] = (mem_unit * par("mem_norm%d" % l)).astype(BF16)
        mem_acc[...] = jnp.zeros_like(mem_acc)

    put("in_a", w_in_a_ref[...])
    put("out_a", w_out_a_ref[...])
    put("kv", w_kv_ref[...])
    w_in_b = w_in_b_ref[...]
    put("q", _cols_group_major(w_in_b[:, :SWA_W]))
    put("q_gate", _cols_group_major(w_in_b[:, SWA_W:2 * SWA_W]))
    put("mem_b", w_in_b[:, 2 * SWA_W:])
    heads = jnp.concatenate([wob_h0_ref[0], wob_h1_ref[0], wob_h2_ref[0], wob_h3_ref[0]], axis=0)
    put("out_b", jnp.where(i < GROUP, heads, wob_m_ref[...]))
    n_slabs = D_MODEL // PREP_ROWS
    for j in range(n_slabs):
        @pl.when(i == j)
        def _(j=j):
            for l in range(DEPTH):
                mem_acc[l] += _dot(mem_xn[l, :, j * PREP_ROWS:(j + 1) * PREP_ROWS], w_mem_ref[l].astype(BF16))

    @pl.when(i == n_slabs - 1)
    def _():
        for l in range(DEPTH):
            kv = mem_acc[l]
            k = kv[:, :MEM_W]
            v_t = kv[:, MEM_W:].T
            k3 = k.T.reshape(MEM_HEADS, HEAD_DIM, N_MEM)
            ms = jnp.mean(k3 * k3, axis=1, keepdims=True)
            k_gained = (k * per_head(par("mem_k_norm%d" % l), MEM_W)).T.reshape(MEM_HEADS, HEAD_DIM, N_MEM)
            k_t = (k_gained * lax.rsqrt(ms + EPS)).reshape(MEM_W, N_MEM)
            mk_ref[l] = k_t
            mv_ref[l] = v_t
            mkvb_ref[l, 0] = k_t.astype(BF16)
            mkvb_ref[l, 1] = v_t.astype(BF16)


def _prep_call(w_in_a, w_out_a, w_kv, w_in_b, w_out_b, w_mem_kv, pool_mix, params, mem):
    rows = lambda w: pl.BlockSpec((PREP_ROWS, w), lambda i: (i, 0))

    def head_src(kvh):
        return pl.BlockSpec((1, HEAD_DIM, D_MODEL),
                            lambda i: (kvh * GROUP + jnp.minimum(i, GROUP - 1), 0, 0))

    in_specs = [rows(2 * POOL_W + 2 * MEM_W), rows(D_MODEL), rows(2 * KV_W), rows(2 * SWA_W + 2 * MEM_W)]
    in_specs += [head_src(kvh) for kvh in range(N_KV_HEADS)]
    in_specs += [pl.BlockSpec((PREP_ROWS, D_MODEL), lambda i: (SWA_W // PREP_ROWS, 0)),
                pl.BlockSpec((DEPTH, PREP_ROWS, 2 * MEM_W), lambda i: (0, i, 0)),
                pl.BlockSpec(pool_mix.shape, lambda i: (0, 0, 0))]
    in_specs += [pl.BlockSpec(params.shape, lambda i: (0, 0)), pl.BlockSpec(mem.shape, lambda i: (0, 0))]
    n_vec = sum(w for _, w in _VEC_WIDTHS)
    mem_f = jax.ShapeDtypeStruct((DEPTH, MEM_W, N_MEM), F32)
    out_specs = [rows(W_ALL_COLS),
                 pl.BlockSpec((POOL_W, POOL_W), lambda i: (0, 0)),
                 pl.BlockSpec((1, n_vec), lambda i: (0, 0)),
                 pl.BlockSpec(mem_f.shape, lambda i: (0, 0, 0)), pl.BlockSpec(mem_f.shape, lambda i: (0, 0, 0)),
                 pl.BlockSpec((DEPTH, 2, MEM_W, N_MEM), lambda i: (0, 0, 0, 0))]
    bf = lambda *shape: jax.ShapeDtypeStruct(shape, BF16)
    out_shape = [bf(D_MODEL, W_ALL_COLS), bf(POOL_W, POOL_W),
                 jax.ShapeDtypeStruct((1, n_vec), F32), mem_f, mem_f, bf(DEPTH, 2, MEM_W, N_MEM)]
    w_out_b_heads = w_out_b.reshape(D_MODEL // HEAD_DIM, HEAD_DIM, D_MODEL)
    return pl.pallas_call(
        _prep_kernel,
        grid=(D_MODEL // PREP_ROWS,),
        in_specs=in_specs,
        out_specs=out_specs,
        out_shape=out_shape,
        scratch_shapes=[pltpu.VMEM((DEPTH, N_MEM, D_MODEL), BF16), pltpu.VMEM((DEPTH, N_MEM, 2 * MEM_W), F32)],
        compiler_params=pltpu.CompilerParams(dimension_semantics=("arbitrary",),
                                             vmem_limit_bytes=PREP_VMEM_LIMIT),
        name="prep",
    )(w_in_a, w_out_a, w_kv, w_in_b, *([w_out_b_heads] * N_KV_HEADS), w_out_b, w_mem_kv, pool_mix, params, mem)


def _rope_lane_tables(pos):
    half = ROT_DIM // 2
    inv = (ROPE_THETA ** (-np.arange(half, dtype=np.float32) * 2.0 / ROT_DIM)).astype(np.float32)
    ang = (np.asarray(pos, np.float32)[:, None] * inv[None, :]).astype(np.float64)
    cos, sin = np.cos(ang), np.sin(ang)
    t = ang.shape[0]
    rest = HEAD_DIM - ROT_DIM
    cos64 = np.concatenate([cos, cos, np.ones((t, rest))], axis=1)
    sin64 = np.concatenate([-sin, sin, np.zeros((t, rest))], axis=1)
    reps = LANES // HEAD_DIM
    return np.tile(cos64, (1, reps)).astype(np.float32), np.tile(sin64, (1, reps)).astype(np.float32)


def kernel(x_prompt, x_sample, state_pool, cache_swa_k, cache_swa_v, cache_mem_k, cache_mem_v, mem_prompt,
           norm_a, w_in_a, pool_mix_w, pool_scale, w_out_a, kv_norm, w_kv, k_norm,
           norm_b, w_in_b, q_norm, sinks, w_out_b, mem_norm, w_mem_kv, mem_q_norm, mem_k_norm):
    seg = np.arange(MXU_DIM) // HEAD_DIM
    bseg = jnp.asarray((seg[:, None] == seg[None, :]).astype(np.float32) / HEAD_DIM, BF16)
    wcol = jnp.asarray(np.repeat(np.asarray(POOL_WINDOWS, np.float32), POOL_GW).reshape(1, POOL_W))

    raw = {"norm_a": norm_a[0], "pool_scale": pool_scale[0], "kv_norm": kv_norm, "norm_b": norm_b[0],
           "mem_norm0": mem_norm[0], "mem_norm1": mem_norm[1], "k_norm": k_norm, "q_norm": q_norm[0],
           "mem_q_norm0": mem_q_norm[0], "mem_q_norm1": mem_q_norm[1],
           "mem_k_norm0": mem_k_norm[0], "mem_k_norm1": mem_k_norm[1]}
    params = jnp.concatenate([raw[name].astype(F32) for name, _ in _PAR_WIDTHS])
    w_all, wbd, vecs, mk_t, mv_t, mkv_b = _prep_call(
        w_in_a[0], w_out_a[0], w_kv, w_in_b[0], w_out_b[0], w_mem_kv, pool_mix_w[0], params.reshape(1, _PAR_COLS),
        mem_prompt[0])
    wts = (vecs, w_all, wbd, wcol, mkv_b, bseg)

    n_tiles = SEQ // PROMPT_TILE
    cb, sb_ = _rope_lane_tables(np.arange(n_tiles) * PROMPT_TILE)
    cr, sr = _rope_lane_tables(np.arange(PROMPT_TILE))
    rope_base = jnp.asarray(np.stack([cb, sb_], axis=1))
    rope_res = jnp.asarray(np.stack([cr, sr], axis=0))
    cos_s, sin_s = _rope_lane_tables(PAST_LEN + np.arange(DEC_SEQ))
    rope_s = jnp.asarray(np.stack([np.tile(cos_s, (STEP_SEQS, 1)), np.tile(sin_s, (STEP_SEQS, 1))], axis=0))
    pref = jnp.transpose(state_pool[0], (1, 0, 2))
    cmk_t = jnp.transpose(cache_mem_k, (0, 1, 3, 4, 2)).reshape(DEPTH, DEC_BATCH, MEM_W, N_MEM)
    cmv_t = jnp.transpose(cache_mem_v, (0, 1, 3, 4, 2)).reshape(DEPTH, DEC_BATCH, MEM_W, N_MEM)
    ck_t = jnp.transpose(cache_swa_k, (0, 2, 3, 1)).reshape(DEC_BATCH, KV_W, WINDOW)
    cv_t = jnp.transpose(cache_swa_v, (0, 2, 3, 1)).reshape(DEC_BATCH, KV_W, WINDOW)
    y_p, pool_p, k_p, v_p, y_s, pool_s, k_s, v_s = _step_call(
        sinks[0].astype(F32), x_prompt[0], x_sample, pref, ck_t, cv_t, cmk_t, cmv_t, rope_base, rope_res, rope_s, wts)

    def mem_out(a):
        return jnp.transpose(a.reshape(DEPTH, 1, MEM_HEADS, HEAD_DIM, N_MEM), (0, 1, 4, 2, 3))

    def swa_out_t(a):
        return jnp.transpose(a.reshape(-1, N_KV_HEADS, HEAD_DIM, WINDOW), (0, 3, 1, 2))

    return (y_p[None], y_s, pool_p[None, None], jnp.transpose(pool_s, (1, 0, 2))[None],
            swa_out_t(k_p[None]), swa_out_t(v_p[None]), swa_out_t(k_s), swa_out_t(v_s),
            mem_out(mk_t), mem_out(mv_t))
```

```python
import jax
import jax.numpy as jnp
import numpy as np
from jax import lax
from jax.experimental import pallas as pl
from jax.experimental.pallas import tpu as pltpu

D_MODEL = 1024
SEQ = 16384
DEPTH = 2
DEC_BATCH = 128
DEC_SEQ = 8
PAST_LEN = 16384
HEAD_DIM = 64
POOL_W = 768
POOL_WINDOWS = (2, 4, 8, 16)
POOL_GW = 192
POOL_PAD = 15
N_Q_HEADS = 12
N_KV_HEADS = 4
GROUP = 3
SWA_W = 768
KV_W = 256
WINDOW = 128
N_MEM = 256
MEM_HEADS = 4
MEM_W = 256
ROT_DIM = 16
ROPE_THETA = 500000.0
EPS = 1e-6

F32 = jnp.float32
BF16 = jnp.bfloat16
NEG_INF = float("-inf")
LOG2E = 1.4426950408889634
Q_SCALE = HEAD_DIM ** -0.5 * LOG2E

LANES = 128
MXU_DIM = 256
HEADS_PER_CHUNK = MXU_DIM // HEAD_DIM
CARRY_ROWS = 16
PROMPT_TILE = 512
STEP_SEQS = DEC_BATCH // (SEQ // PROMPT_TILE)
STEP_VMEM_LIMIT = 60 * 1024 * 1024


_VEC_WIDTHS = (("norm_a", D_MODEL), ("pool_scale", POOL_W), ("kv_norm", D_MODEL), ("k_gain", KV_W),
               ("norm_b", D_MODEL), ("q_gain", SWA_W), ("mem_q_gain0", MEM_W), ("mem_q_gain1", MEM_W))
_VEC_OFFSET = {}
for _name, _width in _VEC_WIDTHS:
    _VEC_OFFSET[_name] = (sum(w for _, w in _VEC_WIDTHS[:len(_VEC_OFFSET)]), _width)


def _vec(vec_ref, name):
    start, width = _VEC_OFFSET[name]
    return vec_ref[:, start:start + width]


_PAR_WIDTHS = (("norm_a", D_MODEL), ("pool_scale", POOL_W), ("kv_norm", D_MODEL), ("norm_b", D_MODEL),
               ("mem_norm0", D_MODEL), ("mem_norm1", D_MODEL), ("k_norm", HEAD_DIM), ("q_norm", HEAD_DIM),
               ("mem_q_norm0", HEAD_DIM), ("mem_q_norm1", HEAD_DIM), ("mem_k_norm0", HEAD_DIM),
               ("mem_k_norm1", HEAD_DIM))
_PAR_OFFSET = {}
_PAR_COLS = 0
for _name, _width in _PAR_WIDTHS:
    _PAR_OFFSET[_name] = (_PAR_COLS, _width)
    _PAR_COLS += _width


_W_WIDTHS = (("in_a", 2 * POOL_W + 2 * MEM_W), ("out_a", D_MODEL), ("kv", 2 * KV_W),
             ("q", SWA_W), ("q_gate", SWA_W), ("mem_b", 2 * MEM_W), ("out_b", D_MODEL))
_W_OFFSET = {}
for _name, _width in _W_WIDTHS:
    _W_OFFSET[_name] = (sum(w for _, w in _W_WIDTHS[:len(_W_OFFSET)]), _width)
W_ALL_COLS = sum(w for _, w in _W_WIDTHS)


def _wcols(w_ref, name, lo=0, hi=None):
    start, width = _W_OFFSET[name]
    return w_ref[:, start + lo:start + (width if hi is None else hi)]


def _dot(a, b):
    return jnp.dot(a, b, preferred_element_type=F32)


def _dot_nt(a, b):
    return lax.dot_general(a, b, (((1,), (1,)), ((), ())), preferred_element_type=F32)


def _rms_unit(x):
    return x * lax.rsqrt(jnp.mean(x * x, axis=-1, keepdims=True) + EPS)


def _silu(g):
    return g * (0.5 + 0.5 * jnp.tanh(0.5 * g))


def _head_rms(y, gain, bseg):
    parts = []
    for c in range(y.shape[1] // MXU_DIM):
        yc = y[:, c * MXU_DIM:(c + 1) * MXU_DIM]
        ms = _dot((yc * yc).astype(BF16), bseg)
        parts.append(yc * lax.rsqrt(ms + EPS) * gain[:, c * MXU_DIM:(c + 1) * MXU_DIM])
    return parts[0] if len(parts) == 1 else jnp.concatenate(parts, axis=1)


def _rope(y, cos_t, sin_t, first8):
    parts = []
    for c in range(y.shape[1] // LANES):
        yc = y[:, c * LANES:(c + 1) * LANES]
        partner = jnp.where(first8, pltpu.roll(yc, LANES - 8, 1), pltpu.roll(yc, 8, 1))
        parts.append(yc * cos_t + partner * sin_t)
    return parts[0] if len(parts) == 1 else jnp.concatenate(parts, axis=1)


def _pool_window_sums(u_hist):
    def back(a, k):
        return pltpu.roll(a, k, 0)

    lane = lax.broadcasted_iota(jnp.int32, (1, LANES), 1)
    s2 = u_hist + back(u_hist, 1)
    t = s2[:, LANES:]
    s4 = t + back(t, 2)
    t = s4[:, 2 * LANES:]
    s8 = t + back(t, 4)
    t = s8[:, LANES:]
    s16 = t + back(t, 8)
    tiles = [
        s2[:, :LANES],
        jnp.where(lane < 64, s2[:, LANES:2 * LANES], s4[:, :LANES]),
        s4[:, LANES:2 * LANES],
        s8[:, :LANES],
        jnp.where(lane < 64, s8[:, LANES:2 * LANES], s16[:, :LANES]),
        s16[:, LANES:],
    ]
    return jnp.concatenate(tiles, axis=1)


def _pool_window_sums_planes(planes):
    n = len(planes)

    def doubled(prev, k, first):
        return [prev[j] + prev[j - k] if j >= first else None for j in range(n)]

    lane = lax.broadcasted_iota(jnp.int32, (1, LANES), 1)
    s2 = doubled(planes, 1, 1)
    s4 = doubled([None if a is None else a[:, LANES:] for a in s2], 2, 3)
    s8 = doubled([None if a is None else a[:, 2 * LANES:] for a in s4], 4, 7)
    s16 = doubled([None if a is None else a[:, LANES:] for a in s8], 8, POOL_PAD)
    out = []
    for j in range(POOL_PAD, n):
        out.append(jnp.concatenate([
            s2[j][:, :LANES],
            jnp.where(lane < 64, s2[j][:, LANES:2 * LANES], s4[j][:, :LANES]),
            s4[j][:, LANES:2 * LANES],
            s8[j][:, :LANES],
            jnp.where(lane < 64, s8[j][:, LANES:2 * LANES], s16[j][:, :LANES]),
            s16[j][:, LANES:],
        ], axis=1))
    return out


def _pool_mix(d, wbd_ref):
    lo, hi = MXU_DIM, 2 * MXU_DIM
    return [_dot(d[:, :hi], wbd_ref[:hi, :lo]), _dot(d, wbd_ref[:, lo:hi]), _dot(d[:, lo:], wbd_ref[lo:, hi:])]


def _gated(tiles, x_bf16, w_ref, gate_name, gate_base, scale=None):
    out = []
    for j, t in enumerate(tiles):
        cols = slice(j * MXU_DIM, (j + 1) * MXU_DIM)
        gate = _dot(x_bf16, _wcols(w_ref, gate_name, gate_base + cols.start, gate_base + cols.stop))
        if scale is not None:
            t = t * scale[:, cols]
        out.append(t * _silu(gate))
    return out[0] if len(out) == 1 else jnp.concatenate(out, axis=1)


def _head_masks():
    lane = lax.broadcasted_iota(jnp.int32, (1, MXU_DIM), 1)
    return [(lane // HEAD_DIM) == j for j in range(HEADS_PER_CHUNK)]


def _first8_mask():
    lane = lax.broadcasted_iota(jnp.int32, (1, LANES), 1)
    return (lane % HEAD_DIM) < (ROT_DIM // 2)


def _stack_heads(q, hmask):
    return jnp.concatenate([jnp.where(m, q, 0.0) for m in hmask], axis=0).astype(BF16)


def _unstack_heads(o, hmask):
    n = len(hmask)
    m = o.shape[0] // n
    acc = o[(n - 1) * m:, :]
    for h in range(n - 2, -1, -1):
        acc = jnp.where(hmask[h], o[h * m:(h + 1) * m, :], acc)
    return acc


def _sink_columns(sinks_ref, rows_per_head):
    hrow = lax.broadcasted_iota(jnp.int32, (HEADS_PER_CHUNK * rows_per_head, 1), 0) // rows_per_head
    cols = []
    for c in range(SWA_W // MXU_DIM):
        col = jnp.zeros((HEADS_PER_CHUNK * rows_per_head, 1), F32)
        for jj in range(HEADS_PER_CHUNK):
            col = jnp.where(hrow == jj, sinks_ref[jj * GROUP + c] * LOG2E, col)
        cols.append(col)
    return cols


def _softmax_weights(s, sink):
    m = jnp.max(s, axis=-1, keepdims=True)
    if sink is not None:
        m = jnp.maximum(m, sink)
    p = jnp.exp2(s - m)
    den = jnp.sum(p, axis=-1, keepdims=True)
    if sink is not None:
        den = den + jnp.exp2(sink - m)
    return p.astype(BF16), 1.0 / den


def _mem_attn(q, k_t, v_t, hmask):
    p, rden = _softmax_weights(_dot(_stack_heads(q, hmask), k_t), None)
    return _unstack_heads(_dot_nt(p, v_t) * rden, hmask)


def _swa_block(qb, kc, vc, bias, sink_cols, hmask):
    outs = []
    for g in range(GROUP):
        s = _dot_nt(_stack_heads(qb[:, g * MXU_DIM:(g + 1) * MXU_DIM], hmask), kc)
        s = (s.reshape(HEADS_PER_CHUNK, WINDOW, 2 * WINDOW) + bias[None]).reshape(HEADS_PER_CHUNK * WINDOW, 2 * WINDOW)
        p, rden = _softmax_weights(s, sink_cols[g])
        outs.append(_unstack_heads(_dot(p, vc) * rden, hmask))
    return jnp.concatenate(outs, axis=1)


def _const_spec(shape):
    nd = len(shape)
    return pl.BlockSpec(shape, lambda i: (0,) * nd, pipeline_mode=pl.Buffered(1))


def _stack_heads_seq(q, n_seq, hmask):
    q3 = q.reshape(n_seq, DEC_SEQ, MXU_DIM)
    return jnp.stack([jnp.where(m, q3, 0.0) for m in hmask], axis=1)


def _unstack_heads_seq(o, n_seq, hmask):
    o4 = o.reshape(n_seq, HEADS_PER_CHUNK, DEC_SEQ, MXU_DIM)
    acc = o4[:, HEADS_PER_CHUNK - 1]
    for h in range(HEADS_PER_CHUNK - 2, -1, -1):
        acc = jnp.where(hmask[h], o4[:, h], acc)
    return acc.reshape(n_seq * DEC_SEQ, MXU_DIM)


def _mem_attn_seqs(q, k_ref, v_ref, layer, n_seq, hmask):
    rows = HEADS_PER_CHUNK * DEC_SEQ
    qs = _stack_heads_seq(q, n_seq, hmask).reshape(n_seq * rows, MXU_DIM).astype(BF16)
    s = jnp.concatenate([_dot(qs[b * rows:(b + 1) * rows], k_ref[layer, b].astype(BF16))
                         for b in range(n_seq)], axis=0)
    p, rden = _softmax_weights(s, None)
    o = jnp.concatenate([_dot_nt(p[b * rows:(b + 1) * rows], v_ref[layer, b].astype(BF16))
                         for b in range(n_seq)], axis=0)
    return _unstack_heads_seq(o * rden, n_seq, hmask)


def _sample_pool(u_s, pref_ref, pool_ref, wcol, parity, u_scr, d_scr):
    sb = STEP_SEQS
    n_lt = POOL_W // LANES
    for c in range(n_lt):
        u_scr[c] = u_s[:, c * LANES:(c + 1) * LANES]
    planes = [jnp.where(parity == 0, pref_ref[t, 0:sb, :], pref_ref[t, sb:2 * sb, :]) for t in range(POOL_PAD)]
    planes += [jnp.concatenate([u_scr[c, pl.ds(t, sb, stride=DEC_SEQ), :] for c in range(n_lt)], axis=1)
               for t in range(DEC_SEQ)]
    for t, win in enumerate(_pool_window_sums_planes(planes)):
        cnt = jnp.minimum(float(PAST_LEN + t + 1), wcol)
        d_t = win / cnt - planes[POOL_PAD + t]
        for c in range(n_lt):
            d_scr[c, pl.ds(t, sb, stride=DEC_SEQ), :] = d_t[:, c * LANES:(c + 1) * LANES]

    @pl.when(parity == 0)
    def _():
        for t in range(POOL_PAD):
            pool_ref[t, 0:sb, :] = planes[DEC_SEQ + t]

    @pl.when(parity == 1)
    def _():
        for t in range(POOL_PAD):
            pool_ref[t, sb:2 * sb, :] = planes[DEC_SEQ + t]

    return jnp.concatenate([d_scr[c] for c in range(n_lt)], axis=1)


def _sample_cache_update(k_s, v_s, ck_ref, cv_ref, ko_ref, vo_ref):
    sb = STEP_SEQS
    keep = WINDOW - DEC_SEQ
    key_lane = lax.broadcasted_iota(jnp.int32, (1, WINDOW), 1)
    pad_rows = jnp.zeros((WINDOW - sb * DEC_SEQ, KV_W), F32)
    for new_rows, c_ref, o_ref in ((k_s, ck_ref, ko_ref), (v_s, cv_ref, vo_ref)):
        new_t = jnp.concatenate([new_rows, pad_rows], axis=0).T
        for b in range(sb):
            shifted = pltpu.roll(c_ref[b], keep, 1)
            placed = pltpu.roll(new_t, (keep - b * DEC_SEQ) % WINDOW, 1)
            o_ref[b] = jnp.where(key_lane < keep, shifted, placed)


def _sample_window_attn(q_s, k_s, v_s, ck_ref, cv_ref, sinks_ref, hmask):
    sb = STEP_SEQS
    m = sb * DEC_SEQ
    rows = N_Q_HEADS * DEC_SEQ
    qs = jnp.stack([_stack_heads_seq(q_s[:, g * MXU_DIM:(g + 1) * MXU_DIM], sb, hmask) for g in range(GROUP)],
                   axis=1)
    qs = qs.reshape(sb * rows, KV_W).astype(BF16)
    s_old = jnp.concatenate([_dot(qs[b * rows:(b + 1) * rows], ck_ref[b].astype(BF16)) for b in range(sb)], axis=0)
    s_new = _dot_nt(qs, k_s.astype(BF16))
    tq_old = lax.broadcasted_iota(jnp.int32, (rows, WINDOW), 0) % DEC_SEQ
    key_old = lax.broadcasted_iota(jnp.int32, (rows, WINDOW), 1)
    bias_old = jnp.where(key_old > tq_old, 0.0, NEG_INF)
    row_i = lax.broadcasted_iota(jnp.int32, (sb * rows, m), 0)
    col_i = lax.broadcasted_iota(jnp.int32, (sb * rows, m), 1)
    same_seq = (row_i // rows) == (col_i // DEC_SEQ)
    bias_new = jnp.where(same_seq, jnp.where(col_i % DEC_SEQ <= row_i % DEC_SEQ, 0.0, NEG_INF), NEG_INF)
    s_old = (s_old.reshape(sb, rows, WINDOW) + bias_old[None]).reshape(sb * rows, WINDOW)
    s_new = s_new + bias_new
    sink = jnp.concatenate(_sink_columns(sinks_ref, DEC_SEQ), axis=0)
    sink = jnp.concatenate([sink] * sb, axis=0)
    mx = jnp.maximum(jnp.maximum(jnp.max(s_old, axis=-1, keepdims=True), jnp.max(s_new, axis=-1, keepdims=True)), sink)
    p_old = jnp.exp2(s_old - mx)
    p_new = jnp.exp2(s_new - mx)
    den = (jnp.sum(p_old, axis=-1, keepdims=True) + jnp.sum(p_new, axis=-1, keepdims=True) + jnp.exp2(sink - mx))
    p_old = p_old.astype(BF16)
    o = jnp.concatenate([_dot_nt(p_old[b * rows:(b + 1) * rows], cv_ref[b].astype(BF16)) for b in range(sb)], axis=0)
    o = (o + _dot(p_new.astype(BF16), v_s.astype(BF16))) * (1.0 / den)
    o5 = o.reshape(sb, GROUP, HEADS_PER_CHUNK * DEC_SEQ, KV_W)
    return jnp.concatenate([_unstack_heads_seq(o5[:, g].reshape(sb * HEADS_PER_CHUNK * DEC_SEQ, KV_W), sb, hmask)
                            for g in range(GROUP)], axis=1)


def _step_kernel(sinks_ref, xp_ref, xs_ref, pref_ref, ck_ref, cv_ref, cmk_ref, cmv_ref,
                 rope_base_ref, rope_res_ref, rope_s_ref,
                 vec_ref, w_ref, wbd_ref, wcol_ref, mkv_ref, bseg_ref,
                 yp_ref, pool_p_ref, kop_ref, vop_ref, ys_ref, pool_s_ref, kos_ref, vos_ref,
                 ucarry, kprev, vprev, u_scr, d_scr):
    tq = PROMPT_TILE
    sb = STEP_SEQS
    m = sb * DEC_SEQ
    i = pl.program_id(0)

    @pl.when(i == 0)
    def _():
        ucarry[0] = jnp.zeros((CARRY_ROWS, POOL_W), F32)
        kprev[0] = jnp.zeros((WINDOW, KV_W), BF16)
        vprev[0] = jnp.zeros((WINDOW, KV_W), BF16)

    rd = i % 2
    wr = (i + 1) % 2

    hmask = _head_masks()
    first8 = _first8_mask()
    bseg = bseg_ref[...]
    x = jnp.concatenate([xp_ref[...], xs_ref[...].reshape(m, D_MODEL)], axis=0)

    base = rope_base_ref[i]
    cb, sb_ = base[0:1, :], base[1:2, :]
    cr, sr = rope_res_ref[0], rope_res_ref[1]
    cos_t = jnp.concatenate([cb * cr - sb_ * sr, rope_s_ref[0]], axis=0)
    sin_t = jnp.concatenate([sb_ * cr + cb * sr, rope_s_ref[1]], axis=0)

    xn = (_rms_unit(x) * _vec(vec_ref, "norm_a")).astype(BF16)
    u = _dot(xn, _wcols(w_ref, "in_a", 0, POOL_W))
    qm = _dot(xn, _wcols(w_ref, "in_a", 2 * POOL_W, 2 * POOL_W + MEM_W))
    u_p = u[:tq]
    u_hist = jnp.concatenate([ucarry[rd], u_p], axis=0)
    win = _pool_window_sums(u_hist)[CARRY_ROWS:]
    pos = i * tq + lax.broadcasted_iota(jnp.int32, (tq, 1), 0)
    cnt = jnp.minimum((pos + 1).astype(F32), wcol_ref[...])
    d_p = win / cnt - u_p
    d_s = _sample_pool(u[tq:], pref_ref, pool_s_ref, wcol_ref[...], rd, u_scr, d_scr)
    d = jnp.concatenate([d_p, d_s], axis=0)
    yp = _gated(_pool_mix(d.astype(BF16), wbd_ref), xn, w_ref, "in_a", POOL_W, _vec(vec_ref, "pool_scale"))
    qmn = _head_rms(qm, _vec(vec_ref, "mem_q_gain0"), bseg) * Q_SCALE
    ym = jnp.concatenate([_mem_attn(qmn[:tq], mkv_ref[0, 0], mkv_ref[0, 1], hmask),
                          _mem_attn_seqs(qmn[tq:], cmk_ref, cmv_ref, 0, sb, hmask)], axis=0)
    ym = _gated([ym], xn, w_ref, "in_a", 2 * POOL_W + MEM_W)
    x1 = x + _dot(jnp.concatenate([yp, ym], axis=1).astype(BF16), _wcols(w_ref, "out_a"))
    ucarry[wr] = u_p[tq - CARRY_ROWS:, :]

    r = _rms_unit(x1)
    kv = _dot((r * _vec(vec_ref, "kv_norm")).astype(BF16), _wcols(w_ref, "kv"))
    k = _rope(_head_rms(kv[:, :KV_W], _vec(vec_ref, "k_gain"), bseg), cos_t, sin_t, first8)
    v = kv[:, KV_W:]
    k_p, v_p, k_s, v_s = k[:tq], v[:tq], k[tq:], v[tq:]
    k_all = jnp.concatenate([kprev[rd], k_p.astype(BF16)], axis=0)
    v_all = jnp.concatenate([vprev[rd], v_p.astype(BF16)], axis=0)
    _sample_cache_update(k_s, v_s, ck_ref, cv_ref, kos_ref, vos_ref)

    xb = (r * _vec(vec_ref, "norm_b")).astype(BF16)
    zq = _dot(xb, _wcols(w_ref, "q"))
    qm2 = _dot(xb, _wcols(w_ref, "mem_b", 0, MEM_W))
    q =_rope(_head_rms(zq, _vec(vec_ref, "q_gain"), bseg), cos_t, sin_t, first8) * Q_SCALE

    qi = lax.broadcasted_iota(jnp.int32, (WINDOW, 2 * WINDOW), 0)
    ci = lax.broadcasted_iota(jnp.int32, (WINDOW, 2 * WINDOW), 1)
    band_bias = jnp.where(ci > qi, jnp.where(ci <= qi + WINDOW, 0.0, NEG_INF), NEG_INF)
    sink_cols = _sink_columns(sinks_ref, WINDOW)
    ys_blocks = []
    for b in range(tq // WINDOW):
        bias = band_bias
        if b == 0:
            key_pos = ci + (i * tq - WINDOW)
            bias = jnp.where(key_pos >= 0, band_bias, NEG_INF)
        qb = q[b * WINDOW:(b + 1) * WINDOW, :]
        kc = k_all[b * WINDOW:(b + 2) * WINDOW, :]
        vc = v_all[b * WINDOW:(b + 2) * WINDOW, :]
        ys_blocks.append(_swa_block(qb, kc, vc, bias, sink_cols, hmask))
    ys_blocks.append(_sample_window_attn(q[tq:], k_s, v_s, ck_ref, cv_ref, sinks_ref, hmask))
    ys = jnp.concatenate(ys_blocks, axis=0)
    ys = _gated([ys[:, g * MXU_DIM:(g + 1) * MXU_DIM] for g in range(GROUP)], xb, w_ref, "q_gate", 0)
    qmn2 = _head_rms(qm2, _vec(vec_ref, "mem_q_gain1"), bseg) * Q_SCALE
    ym2 = jnp.concatenate([_mem_attn(qmn2[:tq], mkv_ref[1, 0], mkv_ref[1, 1], hmask),
                           _mem_attn_seqs(qmn2[tq:], cmk_ref, cmv_ref, 1, sb, hmask)], axis=0)
    ym2 = _gated([ym2], xb, w_ref, "mem_b", MEM_W)
    ob, ob_w = _W_OFFSET["out_b"]
    y = (x1 + _dot(ys.astype(BF16), w_ref[0:SWA_W, ob:ob + ob_w])
         + _dot(ym2.astype(BF16), w_ref[SWA_W:, ob:ob + ob_w]))
    yp_ref[...] = y[:tq]
    ys_ref[...] = y[tq:].reshape(sb, DEC_SEQ, D_MODEL)

    kprev[wr] = k_all[tq:, :]
    vprev[wr] = v_all[tq:, :]

    @pl.when(i == pl.num_programs(0) - 1)
    def _():
        kop_ref[...] = k_p[tq - WINDOW:, :].T
        vop_ref[...] = v_p[tq - WINDOW:, :].T
        pool_p_ref[...] = pltpu.roll(u_p[tq - CARRY_ROWS:, :], CARRY_ROWS - 1, 0)[0:POOL_PAD, :]


def _step_call(sinks, x_p, x_s, pref, ck, cv, cmk_t, cmv_t, rope_base, rope_res, rope_s, wts):
    tq = PROMPT_TILE
    sb = STEP_SEQS
    n = x_p.shape[0]
    nb = x_s.shape[0]
    assert n // tq == nb // sb
    row_spec = lambda w: pl.BlockSpec((tq, w), lambda i: (i, 0))
    seq3 = lambda a, b_: pl.BlockSpec((sb, a, b_), lambda i: (i, 0, 0))
    pool_s_spec = pl.BlockSpec((POOL_PAD, 2 * sb, POOL_W), lambda i: (0, i // 2, 0))
    cache_mem_spec = pl.BlockSpec((DEPTH, sb, MEM_W, N_MEM), lambda i: (0, i, 0, 0))
    in_specs = [pl.BlockSpec(memory_space=pltpu.SMEM), row_spec(D_MODEL),
                seq3(DEC_SEQ, D_MODEL), pool_s_spec, seq3(KV_W, WINDOW), seq3(KV_W, WINDOW),
                cache_mem_spec, cache_mem_spec,
                _const_spec(rope_base.shape), _const_spec(rope_res.shape), _const_spec(rope_s.shape)]
    in_specs += [_const_spec(w.shape) for w in wts]
    out_shape = [jax.ShapeDtypeStruct((n, D_MODEL), F32),
                 jax.ShapeDtypeStruct((POOL_PAD, POOL_W), F32),
                 jax.ShapeDtypeStruct((KV_W, WINDOW), F32),
                 jax.ShapeDtypeStruct((KV_W, WINDOW), F32),
                 jax.ShapeDtypeStruct((nb, DEC_SEQ, D_MODEL), F32),
                 jax.ShapeDtypeStruct((POOL_PAD, nb, POOL_W), F32),
                 jax.ShapeDtypeStruct((nb, KV_W, WINDOW), F32),
                 jax.ShapeDtypeStruct((nb, KV_W, WINDOW), F32)]
    out_specs = [row_spec(D_MODEL),
                 pl.BlockSpec((POOL_PAD, POOL_W), lambda i: (0, 0)),
                 pl.BlockSpec((KV_W, WINDOW), lambda i: (0, 0)),
                 pl.BlockSpec((KV_W, WINDOW), lambda i: (0, 0)),
                 seq3(DEC_SEQ, D_MODEL), pool_s_spec, seq3(KV_W, WINDOW), seq3(KV_W, WINDOW)]
    return pl.pallas_call(
        _step_kernel,
        grid=(n // tq,),
        in_specs=in_specs,
        out_specs=out_specs,
        out_shape=out_shape,
        scratch_shapes=[pltpu.VMEM((2, CARRY_ROWS, POOL_W), F32),
                        pltpu.VMEM((2, WINDOW, KV_W), BF16),
                        pltpu.VMEM((2, WINDOW, KV_W), BF16),
                        pltpu.VMEM((POOL_W // LANES, sb * DEC_SEQ, LANES), F32),
                        pltpu.VMEM((POOL_W // LANES, sb * DEC_SEQ, LANES), F32)],
        compiler_params=pltpu.CompilerParams(dimension_semantics=("arbitrary",),
                                             vmem_limit_bytes=STEP_VMEM_LIMIT),
        name="step",
    )(sinks, x_p, x_s, pref, ck, cv, cmk_t, cmv_t, rope_base, rope_res, rope_s, *wts)


PREP_ROWS = N_KV_HEADS * HEAD_DIM
PREP_VMEM_LIMIT = 40 * 1024 * 1024


def _cols_group_major(w):
    heads = [w[:, (kvh * GROUP + g) * HEAD_DIM:(kvh * GROUP + g + 1) * HEAD_DIM]
             for g in range(GROUP) for kvh in range(N_KV_HEADS)]
    return jnp.concatenate(heads, axis=1)


def _prep_kernel(w_in_a_ref, w_out_a_ref, w_kv_ref, w_in_b_ref, wob_h0_ref, wob_h1_ref, wob_h2_ref, wob_h3_ref,
                 wob_m_ref, w_mem_ref, mix_ref,
                 par_ref, mem_ref,
                 w_ref, wbd_ref, vec_ref, mk_ref, mv_ref, mkvb_ref,
                 mem_xn, mem_acc):
    i = pl.program_id(0)

    def put(name, value):
        start, width = _W_OFFSET[name]
        w_ref[:, start:start + width] = value.astype(BF16)

    def par(name):
        start, width = _PAR_OFFSET[name]
        return par_ref[:, start:start + width]

    def per_head(g_row, width):
        return jnp.concatenate([g_row] * (width // HEAD_DIM), axis=1)

    @pl.when(i == 0)
    def _():
        n_grp = len(POOL_WINDOWS)
        for g in range(n_grp):
            pieces = [jnp.zeros((POOL_GW, POOL_GW), F32)] * n_grp
            pieces[g] = mix_ref[g]
            wbd_ref[g * POOL_GW:(g + 1) * POOL_GW, :] = jnp.concatenate(pieces, axis=1).astype(BF16)

        vec_pieces = {"norm_a": par("norm_a"), "pool_scale": par("pool_scale"), "kv_norm": par("kv_norm"),
                      "k_gain": per_head(par("k_norm"), KV_W), "norm_b": par("norm_b"),
                      "q_gain": per_head(par("q_norm"), SWA_W),
                      "mem_q_gain0": per_head(par("mem_q_norm0"), MEM_W),
                      "mem_q_gain1": per_head(par("mem_q_norm1"), MEM_W)}
        for name, (start, width) in _VEC_OFFSET.items():
            vec_ref[:, start:start + width] = vec_pieces[name]

        mem_unit = _rms_unit(mem_ref[...])
        for l in range(DEPTH):
            mem_xn[l] = (mem_unit * par("mem_norm%d" % l)).astype(BF16)
        mem_acc[...] = jnp.zeros_like(mem_acc)

    put("in_a", w_in_a_ref[...])
    put("out_a", w_out_a_ref[...])
    put("kv", w_kv_ref[...])
    w_in_b = w_in_b_ref[...]
    put("q", _cols_group_major(w_in_b[:, :SWA_W]))
    put("q_gate", _cols_group_major(w_in_b[:, SWA_W:2 * SWA_W]))
    put("mem_b", w_in_b[:, 2 * SWA_W:])
    heads = jnp.concatenate([wob_h0_ref[0], wob_h1_ref[0], wob_h2_ref[0], wob_h3_ref[0]], axis=0)
    put("out_b", jnp.where(i < GROUP, heads, wob_m_ref[...]))
    n_slabs = D_MODEL // PREP_ROWS
    for j in range(n_slabs):
        @pl.when(i == j)
        def _(j=j):
            for l in range(DEPTH):
                mem_acc[l] += _dot(mem_xn[l, :, j * PREP_ROWS:(j + 1) * PREP_ROWS], w_mem_ref[l].astype(BF16))

    @pl.when(i == n_slabs - 1)
    def _():
        for l in range(DEPTH):
            kv = mem_acc[l]
            k = kv[:, :MEM_W]
            v_t = kv[:, MEM_W:].T
            k3 = k.T.reshape(MEM_HEADS, HEAD_DIM, N_MEM)
            ms = jnp.mean(k3 * k3, axis=1, keepdims=True)
            k_gained = (k * per_head(par("mem_k_norm%d" % l), MEM_W)).T.reshape(MEM_HEADS, HEAD_DIM, N_MEM)
            k_t = (k_gained * lax.rsqrt(ms + EPS)).reshape(MEM_W, N_MEM)
            mk_ref[l] = k_t
            mv_ref[l] = v_t
            mkvb_ref[l, 0] = k_t.astype(BF16)
            mkvb_ref[l, 1] = v_t.astype(BF16)


def _prep_call(w_in_a, w_out_a, w_kv, w_in_b, w_out_b, w_mem_kv, pool_mix, params, mem):
    rows = lambda w: pl.BlockSpec((PREP_ROWS, w), lambda i: (i, 0))

    def head_src(kvh):
        return pl.BlockSpec((1, HEAD_DIM, D_MODEL),
                            lambda i: (kvh * GROUP + jnp.minimum(i, GROUP - 1), 0, 0))

    in_specs = [rows(2 * POOL_W + 2 * MEM_W), rows(D_MODEL), rows(2 * KV_W), rows(2 * SWA_W + 2 * MEM_W)]
    in_specs += [head_src(kvh) for kvh in range(N_KV_HEADS)]
    in_specs += [pl.BlockSpec((PREP_ROWS, D_MODEL), lambda i: (SWA_W // PREP_ROWS, 0)),
                pl.BlockSpec((DEPTH, PREP_ROWS, 2 * MEM_W), lambda i: (0, i, 0)),
                pl.BlockSpec(pool_mix.shape, lambda i: (0, 0, 0))]
    in_specs += [pl.BlockSpec(params.shape, lambda i: (0, 0)), pl.BlockSpec(mem.shape, lambda i: (0, 0))]
    n_vec = sum(w for _, w in _VEC_WIDTHS)
    mem_f = jax.ShapeDtypeStruct((DEPTH, MEM_W, N_MEM), F32)
    out_specs = [rows(W_ALL_COLS),
                 pl.BlockSpec((POOL_W, POOL_W), lambda i: (0, 0)),
                 pl.BlockSpec((1, n_vec), lambda i: (0, 0)),
                 pl.BlockSpec(mem_f.shape, lambda i: (0, 0, 0)), pl.BlockSpec(mem_f.shape, lambda i: (0, 0, 0)),
                 pl.BlockSpec((DEPTH, 2, MEM_W, N_MEM), lambda i: (0, 0, 0, 0))]
    bf = lambda *shape: jax.ShapeDtypeStruct(shape, BF16)
    out_shape = [bf(D_MODEL, W_ALL_COLS), bf(POOL_W, POOL_W),
                 jax.ShapeDtypeStruct((1, n_vec), F32), mem_f, mem_f, bf(DEPTH, 2, MEM_W, N_MEM)]
    w_out_b_heads = w_out_b.reshape(D_MODEL // HEAD_DIM, HEAD_DIM, D_MODEL)
    return pl.pallas_call(
        _prep_kernel,
        grid=(D_MODEL // PREP_ROWS,),
        in_specs=in_specs,
        out_specs=out_specs,
        out_shape=out_shape,
        scratch_shapes=[pltpu.VMEM((DEPTH, N_MEM, D_MODEL), BF16), pltpu.VMEM((DEPTH, N_MEM, 2 * MEM_W), F32)],
        compiler_params=pltpu.CompilerParams(dimension_semantics=("arbitrary",),
                                             vmem_limit_bytes=PREP_VMEM_LIMIT),
        name="prep",
    )(w_in_a, w_out_a, w_kv, w_in_b, *([w_out_b_heads] * N_KV_HEADS), w_out_b, w_mem_kv, pool_mix, params, mem)


def _rope_lane_tables(pos):
    half = ROT_DIM // 2
    inv = (ROPE_THETA ** (-np.arange(half, dtype=np.float32) * 2.0 / ROT_DIM)).astype(np.float32)
    ang = (np.asarray(pos, np.float32)[:, None] * inv[None, :]).astype(np.float64)
    cos, sin = np.cos(ang), np.sin(ang)
    t = ang.shape[0]
    rest = HEAD_DIM - ROT_DIM
    cos64 = np.concatenate([cos, cos, np.ones((t, rest))], axis=1)
    sin64 = np.concatenate([-sin, sin, np.zeros((t, rest))], axis=1)
    reps = LANES // HEAD_DIM
    return np.tile(cos64, (1, reps)).astype(np.float32), np.tile(sin64, (1, reps)).astype(np.float32)


def kernel(x_prompt, x_sample, state_pool, cache_swa_k, cache_swa_v, cache_mem_k, cache_mem_v, mem_prompt,
           norm_a, w_in_a, pool_mix_w, pool_scale, w_out_a, kv_norm, w_kv, k_norm,
           norm_b, w_in_b, q_norm, sinks, w_out_b, mem_norm, w_mem_kv, mem_q_norm, mem_k_norm):
    seg = np.arange(MXU_DIM) // HEAD_DIM
    bseg = jnp.asarray((seg[:, None] == seg[None, :]).astype(np.float32) / HEAD_DIM, BF16)
    wcol = jnp.asarray(np.repeat(np.asarray(POOL_WINDOWS, np.float32), POOL_GW).reshape(1, POOL_W))

    raw = {"norm_a": norm_a[0], "pool_scale": pool_scale[0], "kv_norm": kv_norm, "norm_b": norm_b[0],
           "mem_norm0": mem_norm[0], "mem_norm1": mem_norm[1], "k_norm": k_norm, "q_norm": q_norm[0],
           "mem_q_norm0": mem_q_norm[0], "mem_q_norm1": mem_q_norm[1],
           "mem_k_norm0": mem_k_norm[0], "mem_k_norm1": mem_k_norm[1]}
    params = jnp.concatenate([raw[name].astype(F32) for name, _ in _PAR_WIDTHS])
    w_all, wbd, vecs, mk_t, mv_t, mkv_b = _prep_call(
        w_in_a[0], w_out_a[0], w_kv, w_in_b[0], w_out_b[0], w_mem_kv, pool_mix_w[0], params.reshape(1, _PAR_COLS),
        mem_prompt[0])
    wts = (vecs, w_all, wbd, wcol, mkv_b, bseg)

    n_tiles = SEQ // PROMPT_TILE
    cb, sb_ = _rope_lane_tables(np.arange(n_tiles) * PROMPT_TILE)
    cr, sr = _rope_lane_tables(np.arange(PROMPT_TILE))
    rope_base = jnp.asarray(np.stack([cb, sb_], axis=1))
    rope_res = jnp.asarray(np.stack([cr, sr], axis=0))
    cos_s, sin_s = _rope_lane_tables(PAST_LEN + np.arange(DEC_SEQ))
    rope_s = jnp.asarray(np.stack([np.tile(cos_s, (STEP_SEQS, 1)), np.tile(sin_s, (STEP_SEQS, 1))], axis=0))
    pref = jnp.transpose(state_pool[0], (1, 0, 2))
    cmk_t = jnp.transpose(cache_mem_k, (0, 1, 3, 4, 2)).reshape(DEPTH, DEC_BATCH, MEM_W, N_MEM)
    cmv_t = jnp.transpose(cache_mem_v, (0, 1, 3, 4, 2)).reshape(DEPTH, DEC_BATCH, MEM_W, N_MEM)
    ck_t = jnp.transpose(cache_swa_k, (0, 2, 3, 1)).reshape(DEC_BATCH, KV_W, WINDOW)
    cv_t = jnp.transpose(cache_swa_v, (0, 2, 3, 1)).reshape(DEC_BATCH, KV_W, WINDOW)
    y_p, pool_p, k_p, v_p, y_s, pool_s, k_s, v_s = _step_call(
        sinks[0].astype(F32), x_prompt[0], x_sample, pref, ck_t, cv_t, cmk_t, cmv_t, rope_base, rope_res, rope_s, wts)

    def mem_out(a):
        return jnp.transpose(a.reshape(DEPTH, 1, MEM_HEADS, HEAD_DIM, N_MEM), (0, 1, 4, 2, 3))

    def swa_out_t(a):
        return jnp.transpose(a.reshape(-1, N_KV_HEADS, HEAD_DIM, WINDOW), (0, 3, 1, 2))

    return (y_p[None], y_s, pool_p[None, None], jnp.transpose(pool_s, (1, 0, 2))[None],
            swa_out_t(k_p[None]), swa_out_t(v_p[None]), swa_out_t(k_s), swa_out_t(v_s),
            mem_out(mk_t), mem_out(mv_t))
```

```python
import jax
import jax.numpy as jnp
import numpy as np
from jax import lax
from jax.experimental import pallas as pl
from jax.experimental.pallas import tpu as pltpu

D_MODEL = 1024
SEQ = 16384
DEPTH = 2
DEC_BATCH = 128
DEC_SEQ = 8
PAST_LEN = 16384
HEAD_DIM = 64
POOL_W = 768
POOL_WINDOWS = (2, 4, 8, 16)
POOL_GW = 192
POOL_PAD = 15
N_Q_HEADS = 12
N_KV_HEADS = 4
GROUP = 3
SWA_W = 768
KV_W = 256
WINDOW = 128
N_MEM = 256
MEM_HEADS = 4
MEM_W = 256
ROT_DIM = 16
ROPE_THETA = 500000.0
EPS = 1e-6

F32 = jnp.float32
BF16 = jnp.bfloat16
NEG_INF = float("-inf")
LOG2E = 1.4426950408889634
Q_SCALE = HEAD_DIM ** -0.5 * LOG2E

LANES = 128
MXU_DIM = 256
HEADS_PER_CHUNK = MXU_DIM // HEAD_DIM
CARRY_ROWS = 16
PROMPT_TILE = 512
STEP_SEQS = DEC_BATCH // (SEQ // PROMPT_TILE)
STEP_VMEM_LIMIT = 60 * 1024 * 1024


_VEC_WIDTHS = (("norm_a", D_MODEL), ("pool_scale", POOL_W), ("kv_norm", D_MODEL), ("k_gain", KV_W),
               ("norm_b", D_MODEL), ("q_gain", SWA_W), ("mem_q_gain0", MEM_W), ("mem_q_gain1", MEM_W))
_VEC_OFFSET = {}
for _name, _width in _VEC_WIDTHS:
    _VEC_OFFSET[_name] = (sum(w for _, w in _VEC_WIDTHS[:len(_VEC_OFFSET)]), _width)


def _vec(vec_ref, name):
    start, width = _VEC_OFFSET[name]
    return vec_ref[:, start:start + width]


_PAR_WIDTHS = (("norm_a", D_MODEL), ("pool_scale", POOL_W), ("kv_norm", D_MODEL), ("norm_b", D_MODEL),
               ("mem_norm0", D_MODEL), ("mem_norm1", D_MODEL), ("k_norm", HEAD_DIM), ("q_norm", HEAD_DIM),
               ("mem_q_norm0", HEAD_DIM), ("mem_q_norm1", HEAD_DIM), ("mem_k_norm0", HEAD_DIM),
               ("mem_k_norm1", HEAD_DIM))
_PAR_OFFSET = {}
_PAR_COLS = 0
for _name, _width in _PAR_WIDTHS:
    _PAR_OFFSET[_name] = (_PAR_COLS, _width)
    _PAR_COLS += _width


_W_WIDTHS = (("in_a", 2 * POOL_W + 2 * MEM_W), ("out_a", D_MODEL), ("kv", 2 * KV_W),
             ("q", SWA_W), ("q_gate", SWA_W), ("mem_b", 2 * MEM_W), ("out_b", D_MODEL))
_W_OFFSET = {}
for _name, _width in _W_WIDTHS:
    _W_OFFSET[_name] = (sum(w for _, w in _W_WIDTHS[:len(_W_OFFSET)]), _width)
W_ALL_COLS = sum(w for _, w in _W_WIDTHS)


def _wcols(w_ref, name, lo=0, hi=None):
    start, width = _W_OFFSET[name]
    return w_ref[:, start + lo:start + (width if hi is None else hi)]


def _dot(a, b):
    return jnp.dot(a, b, preferred_element_type=F32)


def _dot_nt(a, b):
    return lax.dot_general(a, b, (((1,), (1,)), ((), ())), preferred_element_type=F32)


def _rms_unit(x):
    return x * lax.rsqrt(jnp.mean(x * x, axis=-1, keepdims=True) + EPS)


def _silu(g):
    return g * (0.5 + 0.5 * jnp.tanh(0.5 * g))


def _head_rms(y, gain, bseg):
    parts = []
    for c in range(y.shape[1] // MXU_DIM):
        yc = y[:, c * MXU_DIM:(c + 1) * MXU_DIM]
        ms = _dot((yc * yc).astype(BF16), bseg)
        parts.append(yc * lax.rsqrt(ms + EPS) * gain[:, c * MXU_DIM:(c + 1) * MXU_DIM])
    return parts[0] if len(parts) == 1 else jnp.concatenate(parts, axis=1)


def _rope(y, cos_t, sin_t, first8):
    parts = []
    for c in range(y.shape[1] // LANES):
        yc = y[:, c * LANES:(c + 1) * LANES]
        partner = jnp.where(first8, pltpu.roll(yc, LANES - 8, 1), pltpu.roll(yc, 8, 1))
        parts.append(yc * cos_t + partner * sin_t)
    return parts[0] if len(parts) == 1 else jnp.concatenate(parts, axis=1)


def _pool_window_sums(u_hist):
    def back(a, k):
        return pltpu.roll(a, k, 0)

    lane = lax.broadcasted_iota(jnp.int32, (1, LANES), 1)
    s2 = u_hist + back(u_hist, 1)
    t = s2[:, LANES:]
    s4 = t + back(t, 2)
    t = s4[:, 2 * LANES:]
    s8 = t + back(t, 4)
    t = s8[:, LANES:]
    s16 = t + back(t, 8)
    tiles = [
        s2[:, :LANES],
        jnp.where(lane < 64, s2[:, LANES:2 * LANES], s4[:, :LANES]),
        s4[:, LANES:2 * LANES],
        s8[:, :LANES],
        jnp.where(lane < 64, s8[:, LANES:2 * LANES], s16[:, :LANES]),
        s16[:, LANES:],
    ]
    return jnp.concatenate(tiles, axis=1)


def _pool_window_sums_planes(planes):
    n = len(planes)

    def doubled(prev, k, first):
        return [prev[j] + prev[j - k] if j >= first else None for j in range(n)]

    lane = lax.broadcasted_iota(jnp.int32, (1, LANES), 1)
    s2 = doubled(planes, 1, 1)
    s4 = doubled([None if a is None else a[:, LANES:] for a in s2], 2, 3)
    s8 = doubled([None if a is None else a[:, 2 * LANES:] for a in s4], 4, 7)
    s16 = doubled([None if a is None else a[:, LANES:] for a in s8], 8, POOL_PAD)
    out = []
    for j in range(POOL_PAD, n):
        out.append(jnp.concatenate([
            s2[j][:, :LANES],
            jnp.where(lane < 64, s2[j][:, LANES:2 * LANES], s4[j][:, :LANES]),
            s4[j][:, LANES:2 * LANES],
            s8[j][:, :LANES],
            jnp.where(lane < 64, s8[j][:, LANES:2 * LANES], s16[j][:, :LANES]),
            s16[j][:, LANES:],
        ], axis=1))
    return out


def _pool_mix(d, wbd_ref):
    lo, hi = MXU_DIM, 2 * MXU_DIM
    return jnp.concatenate([
        _dot(d[:, :hi], wbd_ref[:hi, :lo]),
        _dot(d, wbd_ref[:, lo:hi]),
        _dot(d[:, lo:], wbd_ref[lo:, hi:]),
    ], axis=1)


def _head_masks():
    lane = lax.broadcasted_iota(jnp.int32, (1, MXU_DIM), 1)
    return [(lane // HEAD_DIM) == j for j in range(HEADS_PER_CHUNK)]


def _first8_mask():
    lane = lax.broadcasted_iota(jnp.int32, (1, LANES), 1)
    return (lane % HEAD_DIM) < (ROT_DIM // 2)


def _stack_heads(q, hmask):
    return jnp.concatenate([jnp.where(m, q, 0.0) for m in hmask], axis=0).astype(BF16)


def _unstack_heads(o, hmask):
    n = len(hmask)
    m = o.shape[0] // n
    acc = o[(n - 1) * m:, :]
    for h in range(n - 2, -1, -1):
        acc = jnp.where(hmask[h], o[h * m:(h + 1) * m, :], acc)
    return acc


def _sink_columns(sinks_ref, rows_per_head):
    hrow = lax.broadcasted_iota(jnp.int32, (HEADS_PER_CHUNK * rows_per_head, 1), 0) // rows_per_head
    cols = []
    for c in range(SWA_W // MXU_DIM):
        col = jnp.zeros((HEADS_PER_CHUNK * rows_per_head, 1), F32)
        for jj in range(HEADS_PER_CHUNK):
            col = jnp.where(hrow == jj, sinks_ref[jj * GROUP + c] * LOG2E, col)
        cols.append(col)
    return cols


def _softmax_weights(s, sink):
    m = jnp.max(s, axis=-1, keepdims=True)
    if sink is not None:
        m = jnp.maximum(m, sink)
    p = jnp.exp2(s - m)
    den = jnp.sum(p, axis=-1, keepdims=True)
    if sink is not None:
        den = den + jnp.exp2(sink - m)
    return p.astype(BF16), 1.0 / den


def _mem_attn(q, k_t, v_t, hmask):
    p, rden = _softmax_weights(_dot(_stack_heads(q, hmask), k_t), None)
    return _unstack_heads(_dot_nt(p, v_t) * rden, hmask)


def _swa_block(qb, kc, vc, bias, sink_cols, hmask):
    outs = []
    for g in range(GROUP):
        s = _dot_nt(_stack_heads(qb[:, g * MXU_DIM:(g + 1) * MXU_DIM], hmask), kc)
        s = (s.reshape(HEADS_PER_CHUNK, WINDOW, 2 * WINDOW) + bias[None]).reshape(HEADS_PER_CHUNK * WINDOW, 2 * WINDOW)
        p, rden = _softmax_weights(s, sink_cols[g])
        outs.append(_unstack_heads(_dot(p, vc) * rden, hmask))
    return jnp.concatenate(outs, axis=1)


def _const_spec(shape):
    nd = len(shape)
    return pl.BlockSpec(shape, lambda i: (0,) * nd, pipeline_mode=pl.Buffered(1))


def _stack_heads_seq(q, n_seq, hmask):
    q3 = q.reshape(n_seq, DEC_SEQ, MXU_DIM)
    return jnp.stack([jnp.where(m, q3, 0.0) for m in hmask], axis=1)


def _unstack_heads_seq(o, n_seq, hmask):
    o4 = o.reshape(n_seq, HEADS_PER_CHUNK, DEC_SEQ, MXU_DIM)
    acc = o4[:, HEADS_PER_CHUNK - 1]
    for h in range(HEADS_PER_CHUNK - 2, -1, -1):
        acc = jnp.where(hmask[h], o4[:, h], acc)
    return acc.reshape(n_seq * DEC_SEQ, MXU_DIM)


def _mem_attn_seqs(q, k_ref, v_ref, layer, n_seq, hmask):
    rows = HEADS_PER_CHUNK * DEC_SEQ
    qs = _stack_heads_seq(q, n_seq, hmask).reshape(n_seq * rows, MXU_DIM).astype(BF16)
    s = jnp.concatenate([_dot(qs[b * rows:(b + 1) * rows], k_ref[layer, b].astype(BF16))
                         for b in range(n_seq)], axis=0)
    p, rden = _softmax_weights(s, None)
    o = jnp.concatenate([_dot_nt(p[b * rows:(b + 1) * rows], v_ref[layer, b].astype(BF16))
                         for b in range(n_seq)], axis=0)
    return _unstack_heads_seq(o * rden, n_seq, hmask)


def _sample_pool(u_s, pref_ref, pool_ref, wcol, parity, u_scr, d_scr):
    sb = STEP_SEQS
    n_lt = POOL_W // LANES
    for c in range(n_lt):
        u_scr[c] = u_s[:, c * LANES:(c + 1) * LANES]
    planes = [jnp.where(parity == 0, pref_ref[t, 0:sb, :], pref_ref[t, sb:2 * sb, :]) for t in range(POOL_PAD)]
    planes += [jnp.concatenate([u_scr[c, pl.ds(t, sb, stride=DEC_SEQ), :] for c in range(n_lt)], axis=1)
               for t in range(DEC_SEQ)]
    for t, win in enumerate(_pool_window_sums_planes(planes)):
        cnt = jnp.minimum(float(PAST_LEN + t + 1), wcol)
        d_t = win / cnt - planes[POOL_PAD + t]
        for c in range(n_lt):
            d_scr[c, pl.ds(t, sb, stride=DEC_SEQ), :] = d_t[:, c * LANES:(c + 1) * LANES]

    @pl.when(parity == 0)
    def _():
        for t in range(POOL_PAD):
            pool_ref[t, 0:sb, :] = planes[DEC_SEQ + t]

    @pl.when(parity == 1)
    def _():
        for t in range(POOL_PAD):
            pool_ref[t, sb:2 * sb, :] = planes[DEC_SEQ + t]

    return jnp.concatenate([d_scr[c] for c in range(n_lt)], axis=1)


def _sample_cache_update(k_s, v_s, ck_ref, cv_ref, ko_ref, vo_ref):
    sb = STEP_SEQS
    keep = WINDOW - DEC_SEQ
    key_lane = lax.broadcasted_iota(jnp.int32, (1, WINDOW), 1)
    pad_rows = jnp.zeros((WINDOW - sb * DEC_SEQ, KV_W), F32)
    for new_rows, c_ref, o_ref in ((k_s, ck_ref, ko_ref), (v_s, cv_ref, vo_ref)):
        new_t = jnp.concatenate([new_rows, pad_rows], axis=0).T
        for b in range(sb):
            shifted = pltpu.roll(c_ref[b], keep, 1)
            placed = pltpu.roll(new_t, (keep - b * DEC_SEQ) % WINDOW, 1)
            o_ref[b] = jnp.where(key_lane < keep, shifted, placed)


def _sample_window_attn(q_s, k_s, v_s, ck_ref, cv_ref, sinks_ref, hmask):
    sb = STEP_SEQS
    m = sb * DEC_SEQ
    rows = N_Q_HEADS * DEC_SEQ
    qs = jnp.stack([_stack_heads_seq(q_s[:, g * MXU_DIM:(g + 1) * MXU_DIM], sb, hmask) for g in range(GROUP)],
                   axis=1)
    qs = qs.reshape(sb * rows, KV_W).astype(BF16)
    s_old = jnp.concatenate([_dot(qs[b * rows:(b + 1) * rows], ck_ref[b].astype(BF16)) for b in range(sb)], axis=0)
    s_new = _dot_nt(qs, k_s.astype(BF16))
    tq_old = lax.broadcasted_iota(jnp.int32, (rows, WINDOW), 0) % DEC_SEQ
    key_old = lax.broadcasted_iota(jnp.int32, (rows, WINDOW), 1)
    bias_old = jnp.where(key_old > tq_old, 0.0, NEG_INF)
    row_i = lax.broadcasted_iota(jnp.int32, (sb * rows, m), 0)
    col_i = lax.broadcasted_iota(jnp.int32, (sb * rows, m), 1)
    same_seq = (row_i // rows) == (col_i // DEC_SEQ)
    bias_new = jnp.where(same_seq, jnp.where(col_i % DEC_SEQ <= row_i % DEC_SEQ, 0.0, NEG_INF), NEG_INF)
    s_old = (s_old.reshape(sb, rows, WINDOW) + bias_old[None]).reshape(sb * rows, WINDOW)
    s_new = s_new + bias_new
    sink = jnp.concatenate(_sink_columns(sinks_ref, DEC_SEQ), axis=0)
    sink = jnp.concatenate([sink] * sb, axis=0)
    mx = jnp.maximum(jnp.maximum(jnp.max(s_old, axis=-1, keepdims=True), jnp.max(s_new, axis=-1, keepdims=True)), sink)
    p_old = jnp.exp2(s_old - mx)
    p_new = jnp.exp2(s_new - mx)
    den = (jnp.sum(p_old, axis=-1, keepdims=True) + jnp.sum(p_new, axis=-1, keepdims=True) + jnp.exp2(sink - mx))
    p_old = p_old.astype(BF16)
    o = jnp.concatenate([_dot_nt(p_old[b * rows:(b + 1) * rows], cv_ref[b].astype(BF16)) for b in range(sb)], axis=0)
    o = (o + _dot(p_new.astype(BF16), v_s.astype(BF16))) * (1.0 / den)
    o5 = o.reshape(sb, GROUP, HEADS_PER_CHUNK * DEC_SEQ, KV_W)
    return jnp.concatenate([_unstack_heads_seq(o5[:, g].reshape(sb * HEADS_PER_CHUNK * DEC_SEQ, KV_W), sb, hmask)
                            for g in range(GROUP)], axis=1)


def _step_kernel(sinks_ref, xp_ref, xs_ref, pref_ref, ck_ref, cv_ref, cmk_ref, cmv_ref,
                 rope_base_ref, rope_res_ref, rope_s_ref,
                 vec_ref, w_ref, wbd_ref, wcol_ref, mkv_ref, bseg_ref,
                 yp_ref, pool_p_ref, kop_ref, vop_ref, ys_ref, pool_s_ref, kos_ref, vos_ref,
                 ucarry, kprev, vprev, u_scr, d_scr):
    tq = PROMPT_TILE
    sb = STEP_SEQS
    m = sb * DEC_SEQ
    i = pl.program_id(0)

    @pl.when(i == 0)
    def _():
        ucarry[0] = jnp.zeros((CARRY_ROWS, POOL_W), F32)
        kprev[0] = jnp.zeros((WINDOW, KV_W), BF16)
        vprev[0] = jnp.zeros((WINDOW, KV_W), BF16)

    rd = i % 2
    wr = (i + 1) % 2

    hmask = _head_masks()
    first8 = _first8_mask()
    bseg = bseg_ref[...]
    x = jnp.concatenate([xp_ref[...], xs_ref[...].reshape(m, D_MODEL)], axis=0)

    base = rope_base_ref[i]
    cb, sb_ = base[0:1, :], base[1:2, :]
    cr, sr = rope_res_ref[0], rope_res_ref[1]
    cos_t = jnp.concatenate([cb * cr - sb_ * sr, rope_s_ref[0]], axis=0)
    sin_t = jnp.concatenate([sb_ * cr + cb * sr, rope_s_ref[1]], axis=0)

    xn = (_rms_unit(x) * _vec(vec_ref, "norm_a")).astype(BF16)
    u = _dot(xn, _wcols(w_ref, "in_a", 0, POOL_W))
    sgp = _silu(_dot(xn, _wcols(w_ref, "in_a", POOL_W, 2 * POOL_W))).astype(BF16)
    qm = _dot(xn, _wcols(w_ref, "in_a", 2 * POOL_W, 2 * POOL_W + MEM_W))
    sgm = _silu(_dot(xn, _wcols(w_ref, "in_a", 2 * POOL_W + MEM_W))).astype(BF16)
    u_p = u[:tq]
    u_hist = jnp.concatenate([ucarry[rd], u_p], axis=0)
    win = _pool_window_sums(u_hist)[CARRY_ROWS:]
    pos = i * tq + lax.broadcasted_iota(jnp.int32, (tq, 1), 0)
    cnt = jnp.minimum((pos + 1).astype(F32), wcol_ref[...])
    d_p = win / cnt - u_p
    d_s = _sample_pool(u[tq:], pref_ref, pool_s_ref, wcol_ref[...], rd, u_scr, d_scr)
    d = jnp.concatenate([d_p, d_s], axis=0)
    yp = _pool_mix(d.astype(BF16), wbd_ref) * _vec(vec_ref, "pool_scale") * sgp.astype(F32)
    qmn = _head_rms(qm, _vec(vec_ref, "mem_q_gain0"), bseg) * Q_SCALE
    ym = jnp.concatenate([_mem_attn(qmn[:tq], mkv_ref[0, 0], mkv_ref[0, 1], hmask),
                          _mem_attn_seqs(qmn[tq:], cmk_ref, cmv_ref, 0, sb, hmask)], axis=0) * sgm.astype(F32)
    x1 = x + _dot(jnp.concatenate([yp, ym], axis=1).astype(BF16), _wcols(w_ref, "out_a"))
    ucarry[wr] = u_p[tq - CARRY_ROWS:, :]

    r = _rms_unit(x1)
    kv = _dot((r * _vec(vec_ref, "kv_norm")).astype(BF16), _wcols(w_ref, "kv"))
    k = _rope(_head_rms(kv[:, :KV_W], _vec(vec_ref, "k_gain"), bseg), cos_t, sin_t, first8)
    v = kv[:, KV_W:]
    k_p, v_p, k_s, v_s = k[:tq], v[:tq], k[tq:], v[tq:]
    k_all = jnp.concatenate([kprev[rd], k_p.astype(BF16)], axis=0)
    v_all = jnp.concatenate([vprev[rd], v_p.astype(BF16)], axis=0)
    _sample_cache_update(k_s, v_s, ck_ref, cv_ref, kos_ref, vos_ref)

    xb = (r * _vec(vec_ref, "norm_b")).astype(BF16)
    zq = _dot(xb, _wcols(w_ref, "q"))
    sgq = _silu(_dot(xb, _wcols(w_ref, "q_gate"))).astype(BF16)
    qm2 = _dot(xb, _wcols(w_ref, "mem_b", 0, MEM_W))
    sgm2 = _silu(_dot(xb, _wcols(w_ref, "mem_b", MEM_W))).astype(BF16)
    q = _rope(_head_rms(zq, _vec(vec_ref, "q_gain"), bseg), cos_t, sin_t, first8) * Q_SCALE

    qi = lax.broadcasted_iota(jnp.int32, (WINDOW, 2 * WINDOW), 0)
    ci = lax.broadcasted_iota(jnp.int32, (WINDOW, 2 * WINDOW), 1)
    band_bias = jnp.where(ci > qi, jnp.where(ci <= qi + WINDOW, 0.0, NEG_INF), NEG_INF)
    sink_cols = _sink_columns(sinks_ref, WINDOW)
    ys_blocks = []
    for b in range(tq // WINDOW):
        bias = band_bias
        if b == 0:
            key_pos = ci + (i * tq - WINDOW)
            bias = jnp.where(key_pos >= 0, band_bias, NEG_INF)
        qb = q[b * WINDOW:(b + 1) * WINDOW, :]
        kc = k_all[b * WINDOW:(b + 2) * WINDOW, :]
        vc = v_all[b * WINDOW:(b + 2) * WINDOW, :]
        ys_blocks.append(_swa_block(qb, kc, vc, bias, sink_cols, hmask))
    ys_blocks.append(_sample_window_attn(q[tq:], k_s, v_s, ck_ref, cv_ref, sinks_ref, hmask))
    ys = jnp.concatenate(ys_blocks, axis=0) * sgq.astype(F32)
    qmn2 = _head_rms(qm2, _vec(vec_ref, "mem_q_gain1"), bseg) * Q_SCALE
    ym2 = jnp.concatenate([_mem_attn(qmn2[:tq], mkv_ref[1, 0], mkv_ref[1, 1], hmask),
                           _mem_attn_seqs(qmn2[tq:], cmk_ref, cmv_ref, 1, sb, hmask)], axis=0) * sgm2.astype(F32)
    ob, ob_w = _W_OFFSET["out_b"]
    y = (x1 + _dot(ys.astype(BF16), w_ref[0:SWA_W, ob:ob + ob_w])
         + _dot(ym2.astype(BF16), w_ref[SWA_W:, ob:ob + ob_w]))
    yp_ref[...] = y[:tq]
    ys_ref[...] = y[tq:].reshape(sb, DEC_SEQ, D_MODEL)

    kprev[wr] = k_all[tq:, :]
    vprev[wr] = v_all[tq:, :]

    @pl.when(i == pl.num_programs(0) - 1)
    def _():
        kop_ref[...] = k_p[tq - WINDOW:, :].T
        vop_ref[...] = v_p[tq - WINDOW:, :].T
        pool_p_ref[...] = pltpu.roll(u_p[tq - CARRY_ROWS:, :], CARRY_ROWS - 1, 0)[0:POOL_PAD, :]


def _step_call(sinks, x_p, x_s, pref, ck, cv, cmk_t, cmv_t, rope_base, rope_res, rope_s, wts):
    tq = PROMPT_TILE
    sb = STEP_SEQS
    n = x_p.shape[0]
    nb = x_s.shape[0]
    assert n // tq == nb // sb
    row_spec = lambda w: pl.BlockSpec((tq, w), lambda i: (i, 0))
    seq3 = lambda a, b_: pl.BlockSpec((sb, a, b_), lambda i: (i, 0, 0))
    pool_s_spec = pl.BlockSpec((POOL_PAD, 2 * sb, POOL_W), lambda i: (0, i // 2, 0))
    cache_mem_spec = pl.BlockSpec((DEPTH, sb, MEM_W, N_MEM), lambda i: (0, i, 0, 0))
    in_specs = [pl.BlockSpec(memory_space=pltpu.SMEM), row_spec(D_MODEL),
                seq3(DEC_SEQ, D_MODEL), pool_s_spec, seq3(KV_W, WINDOW), seq3(KV_W, WINDOW),
                cache_mem_spec, cache_mem_spec,
                _const_spec(rope_base.shape), _const_spec(rope_res.shape), _const_spec(rope_s.shape)]
    in_specs += [_const_spec(w.shape) for w in wts]
    out_shape = [jax.ShapeDtypeStruct((n, D_MODEL), F32),
                 jax.ShapeDtypeStruct((POOL_PAD, POOL_W), F32),
                 jax.ShapeDtypeStruct((KV_W, WINDOW), F32),
                 jax.ShapeDtypeStruct((KV_W, WINDOW), F32),
                 jax.ShapeDtypeStruct((nb, DEC_SEQ, D_MODEL), F32),
                 jax.ShapeDtypeStruct((POOL_PAD, nb, POOL_W), F32),
                 jax.ShapeDtypeStruct((nb, KV_W, WINDOW), F32),
                 jax.ShapeDtypeStruct((nb, KV_W, WINDOW), F32)]
    out_specs = [row_spec(D_MODEL),
                 pl.BlockSpec((POOL_PAD, POOL_W), lambda i: (0, 0)),
                 pl.BlockSpec((KV_W, WINDOW), lambda i: (0, 0)),
                 pl.BlockSpec((KV_W, WINDOW), lambda i: (0, 0)),
                 seq3(DEC_SEQ, D_MODEL), pool_s_spec, seq3(KV_W, WINDOW), seq3(KV_W, WINDOW)]
    return pl.pallas_call(
        _step_kernel,
        grid=(n // tq,),
        in_specs=in_specs,
        out_specs=out_specs,
        out_shape=out_shape,
        scratch_shapes=[pltpu.VMEM((2, CARRY_ROWS, POOL_W), F32),
                        pltpu.VMEM((2, WINDOW, KV_W), BF16),
                        pltpu.VMEM((2, WINDOW, KV_W), BF16),
                        pltpu.VMEM((POOL_W // LANES, sb * DEC_SEQ, LANES), F32),
                        pltpu.VMEM((POOL_W // LANES, sb * DEC_SEQ, LANES), F32)],
        compiler_params=pltpu.CompilerParams(dimension_semantics=("arbitrary",),
                                             vmem_limit_bytes=STEP_VMEM_LIMIT),
        name="step",
    )(sinks, x_p, x_s, pref, ck, cv, cmk_t, cmv_t, rope_base, rope_res, rope_s, *wts)


PREP_ROWS = N_KV_HEADS * HEAD_DIM
PREP_VMEM_LIMIT = 40 * 1024 * 1024


def _cols_group_major(w):
    heads = [w[:, (kvh * GROUP + g) * HEAD_DIM:(kvh * GROUP + g + 1) * HEAD_DIM]
             for g in range(GROUP) for kvh in range(N_KV_HEADS)]
    return jnp.concatenate(heads, axis=1)


def _prep_kernel(w_in_a_ref, w_out_a_ref, w_kv_ref, w_in_b_ref, wob_h0_ref, wob_h1_ref, wob_h2_ref, wob_h3_ref,
                 wob_m_ref, w_mem_ref, mix_ref,
                 par_ref, mem_ref,
                 w_ref, wbd_ref, vec_ref, mk_ref, mv_ref, mkvb_ref,
                 mem_xn, mem_acc):
    i = pl.program_id(0)

    def put(name, value):
        start, width = _W_OFFSET[name]
        w_ref[:, start:start + width] = value.astype(BF16)

    def par(name):
        start, width = _PAR_OFFSET[name]
        return par_ref[:, start:start + width]

    def per_head(g_row, width):
        return jnp.concatenate([g_row] * (width // HEAD_DIM), axis=1)

    @pl.when(i == 0)
    def _():
        n_grp = len(POOL_WINDOWS)
        for g in range(n_grp):
            pieces = [jnp.zeros((POOL_GW, POOL_GW), F32)] * n_grp
            pieces[g] = mix_ref[g]
            wbd_ref[g * POOL_GW:(g + 1) * POOL_GW, :] = jnp.concatenate(pieces, axis=1).astype(BF16)

        vec_pieces = {"norm_a": par("norm_a"), "pool_scale": par("pool_scale"), "kv_norm": par("kv_norm"),
                      "k_gain": per_head(par("k_norm"), KV_W), "norm_b": par("norm_b"),
                      "q_gain": per_head(par("q_norm"), SWA_W),
                      "mem_q_gain0": per_head(par("mem_q_norm0"), MEM_W),
                      "mem_q_gain1": per_head(par("mem_q_norm1"), MEM_W)}
        for name, (start, width) in _VEC_OFFSET.items():
            vec_ref[:, start:start + width] = vec_pieces[name]

        mem_unit = _rms_unit(mem_ref[...])
        for l in range(DEPTH):
            mem_xn[l] = (mem_unit * par("mem_norm%d" % l)).astype(BF16)
        mem_acc[...] = jnp.zeros_like(mem_acc)

    put("in_a", w_in_a_ref[...])
    put("out_a", w_out_a_ref[...])
    put("kv", w_kv_ref[...])
    w_in_b = w_in_b_ref[...]
    put("q", _cols_group_major(w_in_b[:, :SWA_W]))
    put("q_gate", _cols_group_major(w_in_b[:, SWA_W:2 * SWA_W]))
    put("mem_b", w_in_b[:, 2 * SWA_W:])
    heads = jnp.concatenate([wob_h0_ref[0], wob_h1_ref[0], wob_h2_ref[0], wob_h3_ref[0]], axis=0)
    put("out_b", jnp.where(i < GROUP, heads, wob_m_ref[...]))
    n_slabs = D_MODEL // PREP_ROWS
    for j in range(n_slabs):
        @pl.when(i == j)
        def _(j=j):
            for l in range(DEPTH):
                mem_acc[l] += _dot(mem_xn[l, :, j * PREP_ROWS:(j + 1) * PREP_ROWS], w_mem_ref[l].astype(BF16))

    @pl.when(i == n_slabs - 1)
    def _():
        for l in range(DEPTH):
            kv = mem_acc[l]
            k = kv[:, :MEM_W]
            v_t = kv[:, MEM_W:].T
            k3 = k.T.reshape(MEM_HEADS, HEAD_DIM, N_MEM)
            ms = jnp.mean(k3 * k3, axis=1, keepdims=True)
            k_gained = (k * per_head(par("mem_k_norm%d" % l), MEM_W)).T.reshape(MEM_HEADS, HEAD_DIM, N_MEM)
            k_t = (k_gained * lax.rsqrt(ms + EPS)).reshape(MEM_W, N_MEM)
            mk_ref[l] = k_t
            mv_ref[l] = v_t
            mkvb_ref[l, 0] = k_t.astype(BF16)
            mkvb_ref[l, 1] = v_t.astype(BF16)


def _prep_call(w_in_a, w_out_a, w_kv, w_in_b, w_out_b, w_mem_kv, pool_mix, params, mem):
    rows = lambda w: pl.BlockSpec((PREP_ROWS, w), lambda i: (i, 0))

    def head_src(kvh):
        return pl.BlockSpec((1, HEAD_DIM, D_MODEL),
                            lambda i: (kvh * GROUP + jnp.minimum(i, GROUP - 1), 0, 0))

    in_specs = [rows(2 * POOL_W + 2 * MEM_W), rows(D_MODEL), rows(2 * KV_W), rows(2 * SWA_W + 2 * MEM_W)]
    in_specs += [head_src(kvh) for kvh in range(N_KV_HEADS)]
    in_specs += [pl.BlockSpec((PREP_ROWS, D_MODEL), lambda i: (SWA_W // PREP_ROWS, 0)),
                pl.BlockSpec((DEPTH, PREP_ROWS, 2 * MEM_W), lambda i: (0, i, 0)),
                pl.BlockSpec(pool_mix.shape, lambda i: (0, 0, 0))]
    in_specs += [pl.BlockSpec(params.shape, lambda i: (0, 0)), pl.BlockSpec(mem.shape, lambda i: (0, 0))]
    n_vec = sum(w for _, w in _VEC_WIDTHS)
    mem_f = jax.ShapeDtypeStruct((DEPTH, MEM_W, N_MEM), F32)
    out_specs = [rows(W_ALL_COLS),
                 pl.BlockSpec((POOL_W, POOL_W), lambda i: (0, 0)),
                 pl.BlockSpec((1, n_vec), lambda i: (0, 0)),
                 pl.BlockSpec(mem_f.shape, lambda i: (0, 0, 0)), pl.BlockSpec(mem_f.shape, lambda i: (0, 0, 0)),
                 pl.BlockSpec((DEPTH, 2, MEM_W, N_MEM), lambda i: (0, 0, 0, 0))]
    bf = lambda *shape: jax.ShapeDtypeStruct(shape, BF16)
    out_shape = [bf(D_MODEL, W_ALL_COLS), bf(POOL_W, POOL_W),
                 jax.ShapeDtypeStruct((1, n_vec), F32), mem_f, mem_f, bf(DEPTH, 2, MEM_W, N_MEM)]
    w_out_b_heads = w_out_b.reshape(D_MODEL // HEAD_DIM, HEAD_DIM, D_MODEL)
    return pl.pallas_call(
        _prep_kernel,
        grid=(D_MODEL // PREP_ROWS,),
        in_specs=in_specs,
        out_specs=out_specs,
        out_shape=out_shape,
        scratch_shapes=[pltpu.VMEM((DEPTH, N_MEM, D_MODEL), BF16), pltpu.VMEM((DEPTH, N_MEM, 2 * MEM_W), F32)],
        compiler_params=pltpu.CompilerParams(dimension_semantics=("arbitrary",),
                                             vmem_limit_bytes=PREP_VMEM_LIMIT),
        name="prep",
    )(w_in_a, w_out_a, w_kv, w_in_b, *([w_out_b_heads] * N_KV_HEADS), w_out_b, w_mem_kv, pool_mix, params, mem)


def _rope_lane_tables(pos):
    half = ROT_DIM // 2
    inv = (ROPE_THETA ** (-np.arange(half, dtype=np.float32) * 2.0 / ROT_DIM)).astype(np.float32)
    ang = (np.asarray(pos, np.float32)[:, None] * inv[None, :]).astype(np.float64)
    cos, sin = np.cos(ang), np.sin(ang)
    t = ang.shape[0]
    rest = HEAD_DIM - ROT_DIM
    cos64 = np.concatenate([cos, cos, np.ones((t, rest))], axis=1)
    sin64 = np.concatenate([-sin, sin, np.zeros((t, rest))], axis=1)
    reps = LANES // HEAD_DIM
    return np.tile(cos64, (1, reps)).astype(np.float32), np.tile(sin64, (1, reps)).astype(np.float32)


def kernel(x_prompt, x_sample, state_pool, cache_swa_k, cache_swa_v, cache_mem_k, cache_mem_v, mem_prompt,
           norm_a, w_in_a, pool_mix_w, pool_scale, w_out_a, kv_norm, w_kv, k_norm,
           norm_b, w_in_b, q_norm, sinks, w_out_b, mem_norm, w_mem_kv, mem_q_norm, mem_k_norm):
    seg = np.arange(MXU_DIM) // HEAD_DIM
    bseg = jnp.asarray((seg[:, None] == seg[None, :]).astype(np.float32) / HEAD_DIM, BF16)
    wcol = jnp.asarray(np.repeat(np.asarray(POOL_WINDOWS, np.float32), POOL_GW).reshape(1, POOL_W))

    raw = {"norm_a": norm_a[0], "pool_scale": pool_scale[0], "kv_norm": kv_norm, "norm_b": norm_b[0],
           "mem_norm0": mem_norm[0], "mem_norm1": mem_norm[1], "k_norm": k_norm, "q_norm": q_norm[0],
           "mem_q_norm0": mem_q_norm[0], "mem_q_norm1": mem_q_norm[1],
           "mem_k_norm0": mem_k_norm[0], "mem_k_norm1": mem_k_norm[1]}
    params = jnp.concatenate([raw[name].astype(F32) for name, _ in _PAR_WIDTHS])
    w_all, wbd, vecs, mk_t, mv_t, mkv_b = _prep_call(
        w_in_a[0], w_out_a[0], w_kv, w_in_b[0], w_out_b[0], w_mem_kv, pool_mix_w[0], params.reshape(1, _PAR_COLS),
        mem_prompt[0])
    wts = (vecs, w_all, wbd, wcol, mkv_b, bseg)

    n_tiles = SEQ // PROMPT_TILE
    cb, sb_ = _rope_lane_tables(np.arange(n_tiles) * PROMPT_TILE)
    cr, sr = _rope_lane_tables(np.arange(PROMPT_TILE))
    rope_base = jnp.asarray(np.stack([cb, sb_], axis=1))
    rope_res = jnp.asarray(np.stack([cr, sr], axis=0))
    cos_s, sin_s = _rope_lane_tables(PAST_LEN + np.arange(DEC_SEQ))
    rope_s = jnp.asarray(np.stack([np.tile(cos_s, (STEP_SEQS, 1)), np.tile(sin_s, (STEP_SEQS, 1))], axis=0))
    pref = jnp.transpose(state_pool[0], (1, 0, 2))
    cmk_t = jnp.transpose(cache_mem_k, (0, 1, 3, 4, 2)).reshape(DEPTH, DEC_BATCH, MEM_W, N_MEM)
    cmv_t = jnp.transpose(cache_mem_v, (0, 1, 3, 4, 2)).reshape(DEPTH, DEC_BATCH, MEM_W, N_MEM)
    ck_t = jnp.transpose(cache_swa_k, (0, 2, 3, 1)).reshape(DEC_BATCH, KV_W, WINDOW)
    cv_t = jnp.transpose(cache_swa_v, (0, 2, 3, 1)).reshape(DEC_BATCH, KV_W, WINDOW)
    y_p, pool_p, k_p, v_p, y_s, pool_s, k_s, v_s = _step_call(
        sinks[0].astype(F32), x_prompt[0], x_sample, pref, ck_t, cv_t, cmk_t, cmv_t, rope_base, rope_res, rope_s, wts)

    def mem_out(a):
        return jnp.transpose(a.reshape(DEPTH, 1, MEM_HEADS, HEAD_DIM, N_MEM), (0, 1, 4, 2, 3))

    def swa_out_t(a):
        return jnp.transpose(a.reshape(-1, N_KV_HEADS, HEAD_DIM, WINDOW), (0, 3, 1, 2))

    return (y_p[None], y_s, pool_p[None, None], jnp.transpose(pool_s, (1, 0, 2))[None],
            swa_out_t(k_p[None]), swa_out_t(v_p[None]), swa_out_t(k_s), swa_out_t(v_s),
            mem_out(mk_t), mem_out(mv_t))
```

```python
import jax
import jax.numpy as jnp
import numpy as np
from jax import lax
from jax.experimental import pallas as pl
from jax.experimental.pallas import tpu as pltpu

D_MODEL = 1024
SEQ = 16384
DEPTH = 2
DEC_BATCH = 128
DEC_SEQ = 8
PAST_LEN = 16384
HEAD_DIM = 64
POOL_W = 768
POOL_WINDOWS = (2, 4, 8, 16)
POOL_GW = 192
POOL_PAD = 15
N_Q_HEADS = 12
N_KV_HEADS = 4
GROUP = 3
SWA_W = 768
KV_W = 256
WINDOW = 128
N_MEM = 256
MEM_HEADS = 4
MEM_W = 256
ROT_DIM = 16
ROPE_THETA = 500000.0
EPS = 1e-6

F32 = jnp.float32
BF16 = jnp.bfloat16
NEG_INF = float("-inf")
LOG2E = 1.4426950408889634
Q_SCALE = HEAD_DIM ** -0.5 * LOG2E

LANES = 128
MXU_DIM = 256
HEADS_PER_CHUNK = MXU_DIM // HEAD_DIM
CARRY_ROWS = 16
PROMPT_TILE = 512
STEP_SEQS = DEC_BATCH // (SEQ // PROMPT_TILE)
STEP_VMEM_LIMIT = 60 * 1024 * 1024


_VEC_WIDTHS = (("norm_a", D_MODEL), ("pool_scale", POOL_W), ("kv_norm", D_MODEL), ("k_gain", KV_W),
               ("norm_b", D_MODEL), ("q_gain", SWA_W), ("mem_q_gain0", MEM_W), ("mem_q_gain1", MEM_W))
_VEC_OFFSET = {}
for _name, _width in _VEC_WIDTHS:
    _VEC_OFFSET[_name] = (sum(w for _, w in _VEC_WIDTHS[:len(_VEC_OFFSET)]), _width)


def _vec(vec_ref, name):
    start, width = _VEC_OFFSET[name]
    return vec_ref[:, start:start + width]


_PAR_WIDTHS = (("norm_a", D_MODEL), ("pool_scale", POOL_W), ("kv_norm", D_MODEL), ("norm_b", D_MODEL),
               ("mem_norm0", D_MODEL), ("mem_norm1", D_MODEL), ("k_norm", HEAD_DIM), ("q_norm", HEAD_DIM),
               ("mem_q_norm0", HEAD_DIM), ("mem_q_norm1", HEAD_DIM), ("mem_k_norm0", HEAD_DIM),
               ("mem_k_norm1", HEAD_DIM))
_PAR_OFFSET = {}
_PAR_COLS = 0
for _name, _width in _PAR_WIDTHS:
    _PAR_OFFSET[_name] = (_PAR_COLS, _width)
    _PAR_COLS += _width


_W_WIDTHS = (("in_a", 2 * POOL_W + 2 * MEM_W), ("out_a", D_MODEL), ("kv", 2 * KV_W),
             ("q", SWA_W), ("q_gate", SWA_W), ("mem_b", 2 * MEM_W), ("out_b", D_MODEL))
_W_OFFSET = {}
for _name, _width in _W_WIDTHS:
    _W_OFFSET[_name] = (sum(w for _, w in _W_WIDTHS[:len(_W_OFFSET)]), _width)
W_ALL_COLS = sum(w for _, w in _W_WIDTHS)


def _wcols(w_ref, name, lo=0, hi=None):
    start, width = _W_OFFSET[name]
    return w_ref[:, start + lo:start + (width if hi is None else hi)]


def _dot(a, b):
    return jnp.dot(a, b, preferred_element_type=F32)


def _dot_nt(a, b):
    return lax.dot_general(a, b, (((1,), (1,)), ((), ())), preferred_element_type=F32)


def _rms_unit(x):
    return x * lax.rsqrt(jnp.mean(x * x, axis=-1, keepdims=True) + EPS)


def _silu(g):
    return g * (0.5 + 0.5 * jnp.tanh(0.5 * g))


def _head_rms(y, gain, bseg):
    parts = []
    for c in range(y.shape[1] // MXU_DIM):
        yc = y[:, c * MXU_DIM:(c + 1) * MXU_DIM]
        ms = _dot((yc * yc).astype(BF16), bseg)
        parts.append(yc * lax.rsqrt(ms + EPS) * gain[:, c * MXU_DIM:(c + 1) * MXU_DIM])
    return parts[0] if len(parts) == 1 else jnp.concatenate(parts, axis=1)


def _rope(y, cos_t, sin_t, first8):
    parts = []
    for c in range(y.shape[1] // LANES):
        yc = y[:, c * LANES:(c + 1) * LANES]
        partner = jnp.where(first8, pltpu.roll(yc, LANES - 8, 1), pltpu.roll(yc, 8, 1))
        parts.append(yc * cos_t + partner * sin_t)
    return parts[0] if len(parts) == 1 else jnp.concatenate(parts, axis=1)


def _pool_window_sums(u_hist):
    def back(a, k):
        return pltpu.roll(a, k, 0)

    lane = lax.broadcasted_iota(jnp.int32, (1, LANES), 1)
    s2 = u_hist + back(u_hist, 1)
    t = s2[:, LANES:]
    s4 = t + back(t, 2)
    t = s4[:, 2 * LANES:]
    s8 = t + back(t, 4)
    t = s8[:, LANES:]
    s16 = t + back(t, 8)
    tiles = [
        s2[:, :LANES],
        jnp.where(lane < 64, s2[:, LANES:2 * LANES], s4[:, :LANES]),
        s4[:, LANES:2 * LANES],
        s8[:, :LANES],
        jnp.where(lane < 64, s8[:, LANES:2 * LANES], s16[:, :LANES]),
        s16[:, LANES:],
    ]
    return jnp.concatenate(tiles, axis=1)


def _pool_window_sums_planes(planes):
    n = len(planes)

    def doubled(prev, k, first):
        return [prev[j] + prev[j - k] if j >= first else None for j in range(n)]

    lane = lax.broadcasted_iota(jnp.int32, (1, LANES), 1)
    s2 = doubled(planes, 1, 1)
    s4 = doubled([None if a is None else a[:, LANES:] for a in s2], 2, 3)
    s8 = doubled([None if a is None else a[:, 2 * LANES:] for a in s4], 4, 7)
    s16 = doubled([None if a is None else a[:, LANES:] for a in s8], 8, POOL_PAD)
    out = []
    for j in range(POOL_PAD, n):
        out.append(jnp.concatenate([
            s2[j][:, :LANES],
            jnp.where(lane < 64, s2[j][:, LANES:2 * LANES], s4[j][:, :LANES]),
            s4[j][:, LANES:2 * LANES],
            s8[j][:, :LANES],
            jnp.where(lane < 64, s8[j][:, LANES:2 * LANES], s16[j][:, :LANES]),
            s16[j][:, LANES:],
        ], axis=1))
    return out


def _pool_mix(d, wbd_ref):
    lo, hi = MXU_DIM, 2 * MXU_DIM
    return jnp.concatenate([
        _dot(d[:, :hi], wbd_ref[:hi, :lo]),
        _dot(d, wbd_ref[:, lo:hi]),
        _dot(d[:, lo:], wbd_ref[lo:, hi:]),
    ], axis=1)


def _head_masks():
    lane = lax.broadcasted_iota(jnp.int32, (1, MXU_DIM), 1)
    return [(lane // HEAD_DIM) == j for j in range(HEADS_PER_CHUNK)]


def _first8_mask():
    lane = lax.broadcasted_iota(jnp.int32, (1, LANES), 1)
    return (lane % HEAD_DIM) < (ROT_DIM // 2)


def _stack_heads(q, hmask):
    return jnp.concatenate([jnp.where(m, q, 0.0) for m in hmask], axis=0).astype(BF16)


def _unstack_heads(o, hmask):
    n = len(hmask)
    m = o.shape[0] // n
    acc = o[(n - 1) * m:, :]
    for h in range(n - 2, -1, -1):
        acc = jnp.where(hmask[h], o[h * m:(h + 1) * m, :], acc)
    return acc


def _sink_columns(sinks_ref, rows_per_head):
    hrow = lax.broadcasted_iota(jnp.int32, (HEADS_PER_CHUNK * rows_per_head, 1), 0) // rows_per_head
    cols = []
    for c in range(SWA_W // MXU_DIM):
        col = jnp.zeros((HEADS_PER_CHUNK * rows_per_head, 1), F32)
        for jj in range(HEADS_PER_CHUNK):
            col = jnp.where(hrow == jj, sinks_ref[jj * GROUP + c] * LOG2E, col)
        cols.append(col)
    return cols


def _softmax_weights(s, sink):
    m = jnp.max(s, axis=-1, keepdims=True)
    if sink is not None:
        m = jnp.maximum(m, sink)
    p = jnp.exp2(s - m)
    den = jnp.sum(p, axis=-1, keepdims=True)
    if sink is not None:
        den = den + jnp.exp2(sink - m)
    return p.astype(BF16), 1.0 / den


def _mem_attn(q, k_t, v_t, hmask):
    p, rden = _softmax_weights(_dot(_stack_heads(q, hmask), k_t), None)
    return _unstack_heads(_dot_nt(p, v_t) * rden, hmask)


def _swa_block(qb, kc, vc, bias, sink_cols, hmask):
    outs = []
    for g in range(GROUP):
        s = _dot_nt(_stack_heads(qb[:, g * MXU_DIM:(g + 1) * MXU_DIM], hmask), kc)
        s = (s.reshape(HEADS_PER_CHUNK, WINDOW, 2 * WINDOW) + bias[None]).reshape(HEADS_PER_CHUNK * WINDOW, 2 * WINDOW)
        p, rden = _softmax_weights(s, sink_cols[g])
        outs.append(_unstack_heads(_dot(p, vc) * rden, hmask))
    return jnp.concatenate(outs, axis=1)


def _const_spec(shape):
    nd = len(shape)
    return pl.BlockSpec(shape, lambda i: (0,) * nd, pipeline_mode=pl.Buffered(1))


def _stack_heads_seq(q, n_seq, hmask):
    q3 = q.reshape(n_seq, DEC_SEQ, MXU_DIM)
    return jnp.stack([jnp.where(m, q3, 0.0) for m in hmask], axis=1)


def _unstack_heads_seq(o, n_seq, hmask):
    o4 = o.reshape(n_seq, HEADS_PER_CHUNK, DEC_SEQ, MXU_DIM)
    acc = o4[:, HEADS_PER_CHUNK - 1]
    for h in range(HEADS_PER_CHUNK - 2, -1, -1):
        acc = jnp.where(hmask[h], o4[:, h], acc)
    return acc.reshape(n_seq * DEC_SEQ, MXU_DIM)


def _mem_attn_seqs(q, k_ref, v_ref, layer, n_seq, hmask):
    rows = HEADS_PER_CHUNK * DEC_SEQ
    qs = _stack_heads_seq(q, n_seq, hmask).reshape(n_seq * rows, MXU_DIM).astype(BF16)
    s = jnp.concatenate([_dot(qs[b * rows:(b + 1) * rows], k_ref[layer, b].astype(BF16))
                         for b in range(n_seq)], axis=0)
    p, rden = _softmax_weights(s, None)
    o = jnp.concatenate([_dot_nt(p[b * rows:(b + 1) * rows], v_ref[layer, b].astype(BF16))
                         for b in range(n_seq)], axis=0)
    return _unstack_heads_seq(o * rden, n_seq, hmask)


def _mem_attn_both(q, kt_p, vt_p, k_ref, v_ref, layer, n_prompt, n_seq, hmask):
    rows = HEADS_PER_CHUNK * DEC_SEQ
    n = HEADS_PER_CHUNK * n_prompt
    qs = _stack_heads_seq(q[n_prompt:], n_seq, hmask).reshape(n_seq * rows, MXU_DIM).astype(BF16)
    s = jnp.concatenate([_dot(_stack_heads(q[:n_prompt], hmask), kt_p)]
                        + [_dot(qs[b * rows:(b + 1) * rows], k_ref[layer, b].astype(BF16)) for b in range(n_seq)],
                        axis=0)
    p, rden = _softmax_weights(s, None)
    o_p = _dot_nt(p[:n], vt_p) * rden[:n]
    o_s = jnp.concatenate([_dot_nt(p[n + b * rows:n + (b + 1) * rows], v_ref[layer, b].astype(BF16))
                           for b in range(n_seq)], axis=0) * rden[n:]
    return jnp.concatenate([_unstack_heads(o_p, hmask), _unstack_heads_seq(o_s, n_seq, hmask)], axis=0)


def _sample_pool(u_s, pref_ref, pool_ref, wcol, parity, u_scr, d_scr):
    sb = STEP_SEQS
    n_lt = POOL_W // LANES
    for c in range(n_lt):
        u_scr[c] = u_s[:, c * LANES:(c + 1) * LANES]
    planes = [jnp.where(parity == 0, pref_ref[t, 0:sb, :], pref_ref[t, sb:2 * sb, :]) for t in range(POOL_PAD)]
    planes += [jnp.concatenate([u_scr[c, pl.ds(t, sb, stride=DEC_SEQ), :] for c in range(n_lt)], axis=1)
               for t in range(DEC_SEQ)]
    for t, win in enumerate(_pool_window_sums_planes(planes)):
        cnt = jnp.minimum(float(PAST_LEN + t + 1), wcol)
        d_t = win / cnt - planes[POOL_PAD + t]
        for c in range(n_lt):
            d_scr[c, pl.ds(t, sb, stride=DEC_SEQ), :] = d_t[:, c * LANES:(c + 1) * LANES]

    @pl.when(parity == 0)
    def _():
        for t in range(POOL_PAD):
            pool_ref[t, 0:sb, :] = planes[DEC_SEQ + t]

    @pl.when(parity == 1)
    def _():
        for t in range(POOL_PAD):
            pool_ref[t, sb:2 * sb, :] = planes[DEC_SEQ + t]

    return jnp.concatenate([d_scr[c] for c in range(n_lt)], axis=1)


def _sample_cache_update(k_s, v_s, ck_ref, cv_ref, ko_ref, vo_ref):
    sb = STEP_SEQS
    keep = WINDOW - DEC_SEQ
    key_lane = lax.broadcasted_iota(jnp.int32, (1, WINDOW), 1)
    pad_rows = jnp.zeros((WINDOW - sb * DEC_SEQ, KV_W), F32)
    for new_rows, c_ref, o_ref in ((k_s, ck_ref, ko_ref), (v_s, cv_ref, vo_ref)):
        new_t = jnp.concatenate([new_rows, pad_rows], axis=0).T
        for b in range(sb):
            shifted = pltpu.roll(c_ref[b], keep, 1)
            placed = pltpu.roll(new_t, (keep - b * DEC_SEQ) % WINDOW, 1)
            o_ref[b] = jnp.where(key_lane < keep, shifted, placed)


def _sample_window_attn(q_s, k_s, v_s, ck_ref, cv_ref, sinks_ref, hmask):
    sb = STEP_SEQS
    m = sb * DEC_SEQ
    rows = N_Q_HEADS * DEC_SEQ
    qs = jnp.stack([_stack_heads_seq(q_s[:, g * MXU_DIM:(g + 1) * MXU_DIM], sb, hmask) for g in range(GROUP)],
                   axis=1)
    qs = qs.reshape(sb * rows, KV_W).astype(BF16)
    s_old = jnp.concatenate([_dot(qs[b * rows:(b + 1) * rows], ck_ref[b].astype(BF16)) for b in range(sb)], axis=0)
    s_new = _dot_nt(qs, k_s.astype(BF16))
    tq_old = lax.broadcasted_iota(jnp.int32, (rows, WINDOW), 0) % DEC_SEQ
    key_old = lax.broadcasted_iota(jnp.int32, (rows, WINDOW), 1)
    bias_old = jnp.where(key_old > tq_old, 0.0, NEG_INF)
    row_i = lax.broadcasted_iota(jnp.int32, (sb * rows, m), 0)
    col_i = lax.broadcasted_iota(jnp.int32, (sb * rows, m), 1)
    same_seq = (row_i // rows) == (col_i // DEC_SEQ)
    bias_new = jnp.where(same_seq, jnp.where(col_i % DEC_SEQ <= row_i % DEC_SEQ, 0.0, NEG_INF), NEG_INF)
    s_old = (s_old.reshape(sb, rows, WINDOW) + bias_old[None]).reshape(sb * rows, WINDOW)
    s_new = s_new + bias_new
    sink = jnp.concatenate(_sink_columns(sinks_ref, DEC_SEQ), axis=0)
    sink = jnp.concatenate([sink] * sb, axis=0)
    mx = jnp.maximum(jnp.maximum(jnp.max(s_old, axis=-1, keepdims=True), jnp.max(s_new, axis=-1, keepdims=True)), sink)
    p_old = jnp.exp2(s_old - mx)
    p_new = jnp.exp2(s_new - mx)
    den = (jnp.sum(p_old, axis=-1, keepdims=True) + jnp.sum(p_new, axis=-1, keepdims=True) + jnp.exp2(sink - mx))
    p_old = p_old.astype(BF16)
    o = jnp.concatenate([_dot_nt(p_old[b * rows:(b + 1) * rows], cv_ref[b].astype(BF16)) for b in range(sb)], axis=0)
    o = (o + _dot(p_new.astype(BF16), v_s.astype(BF16))) * (1.0 / den)
    o5 = o.reshape(sb, GROUP, HEADS_PER_CHUNK * DEC_SEQ, KV_W)
    return jnp.concatenate([_unstack_heads_seq(o5[:, g].reshape(sb * HEADS_PER_CHUNK * DEC_SEQ, KV_W), sb, hmask)
                            for g in range(GROUP)], axis=1)


def _step_kernel(sinks_ref, xp_ref, xs_ref, pref_ref, ck_ref, cv_ref, cmk_ref, cmv_ref,
                 rope_base_ref, rope_res_ref, rope_s_ref,
                 vec_ref, w_ref, wbd_ref, wcol_ref, mkv_ref, bseg_ref,
                 yp_ref, pool_p_ref, kop_ref, vop_ref, ys_ref, pool_s_ref, kos_ref, vos_ref,
                 ucarry, kprev, vprev, u_scr, d_scr):
    tq = PROMPT_TILE
    sb = STEP_SEQS
    m = sb * DEC_SEQ
    i = pl.program_id(0)

    @pl.when(i == 0)
    def _():
        ucarry[0] = jnp.zeros((CARRY_ROWS, POOL_W), F32)
        kprev[0] = jnp.zeros((WINDOW, KV_W), BF16)
        vprev[0] = jnp.zeros((WINDOW, KV_W), BF16)

    rd = i % 2
    wr = (i + 1) % 2

    hmask = _head_masks()
    first8 = _first8_mask()
    bseg = bseg_ref[...]
    x = jnp.concatenate([xp_ref[...], xs_ref[...].reshape(m, D_MODEL)], axis=0)

    base = rope_base_ref[i]
    cb, sb_ = base[0:1, :], base[1:2, :]
    cr, sr = rope_res_ref[0], rope_res_ref[1]
    cos_t = jnp.concatenate([cb * cr - sb_ * sr, rope_s_ref[0]], axis=0)
    sin_t = jnp.concatenate([sb_ * cr + cb * sr, rope_s_ref[1]], axis=0)

    xn = (_rms_unit(x) * _vec(vec_ref, "norm_a")).astype(BF16)
    u = _dot(xn, _wcols(w_ref, "in_a", 0, POOL_W))
    gp = _dot(xn, _wcols(w_ref, "in_a", POOL_W, 2 * POOL_W))
    qm = _dot(xn, _wcols(w_ref, "in_a", 2 * POOL_W, 2 * POOL_W + MEM_W))
    gm = _dot(xn, _wcols(w_ref, "in_a", 2 * POOL_W + MEM_W))
    u_p = u[:tq]
    u_hist = jnp.concatenate([ucarry[rd], u_p], axis=0)
    win = _pool_window_sums(u_hist)[CARRY_ROWS:]
    pos = i * tq + lax.broadcasted_iota(jnp.int32, (tq, 1), 0)
    cnt = jnp.minimum((pos + 1).astype(F32), wcol_ref[...])
    d_p = win / cnt - u_p
    d_s = _sample_pool(u[tq:], pref_ref, pool_s_ref, wcol_ref[...], rd, u_scr, d_scr)
    d = jnp.concatenate([d_p, d_s], axis=0)
    yp = _pool_mix(d.astype(BF16), wbd_ref) * _vec(vec_ref, "pool_scale") * _silu(gp)
    qmn = _head_rms(qm, _vec(vec_ref, "mem_q_gain0"), bseg) * Q_SCALE
    ym = _mem_attn_both(qmn, mkv_ref[0, 0], mkv_ref[0, 1], cmk_ref, cmv_ref, 0, tq, sb, hmask) * _silu(gm)
    x1 = x + _dot(jnp.concatenate([yp, ym], axis=1).astype(BF16), _wcols(w_ref, "out_a"))
    ucarry[wr] = u_p[tq - CARRY_ROWS:, :]

    r = _rms_unit(x1)
    kv = _dot((r * _vec(vec_ref, "kv_norm")).astype(BF16), _wcols(w_ref, "kv"))
    k = _rope(_head_rms(kv[:, :KV_W], _vec(vec_ref, "k_gain"), bseg), cos_t, sin_t, first8)
    v = kv[:, KV_W:]
    k_p, v_p, k_s, v_s = k[:tq], v[:tq], k[tq:], v[tq:]
    k_all = jnp.concatenate([kprev[rd], k_p.astype(BF16)], axis=0)
    v_all = jnp.concatenate([vprev[rd], v_p.astype(BF16)], axis=0)
    _sample_cache_update(k_s, v_s, ck_ref, cv_ref, kos_ref, vos_ref)

    xb = (r * _vec(vec_ref, "norm_b")).astype(BF16)
    zq = _dot(xb, _wcols(w_ref, "q"))
    gq = _dot(xb, _wcols(w_ref, "q_gate"))
    qm2 = _dot(xb, _wcols(w_ref, "mem_b", 0, MEM_W))
    gm2 = _dot(xb, _wcols(w_ref, "mem_b", MEM_W))
    q = _rope(_head_rms(zq, _vec(vec_ref, "q_gain"), bseg), cos_t, sin_t, first8) * Q_SCALE

    qi = lax.broadcasted_iota(jnp.int32, (WINDOW, 2 * WINDOW), 0)
    ci = lax.broadcasted_iota(jnp.int32, (WINDOW, 2 * WINDOW), 1)
    band_bias = jnp.where(ci > qi, jnp.where(ci <= qi + WINDOW, 0.0, NEG_INF), NEG_INF)
    sink_cols = _sink_columns(sinks_ref, WINDOW)
    ys_blocks = []
    for b in range(tq // WINDOW):
        bias = band_bias
        if b == 0:
            key_pos = ci + (i * tq - WINDOW)
            bias = jnp.where(key_pos >= 0, band_bias, NEG_INF)
        qb = q[b * WINDOW:(b + 1) * WINDOW, :]
        kc = k_all[b * WINDOW:(b + 2) * WINDOW, :]
        vc = v_all[b * WINDOW:(b + 2) * WINDOW, :]
        ys_blocks.append(_swa_block(qb, kc, vc, bias, sink_cols, hmask))
    ys_blocks.append(_sample_window_attn(q[tq:], k_s, v_s, ck_ref, cv_ref, sinks_ref, hmask))
    ys = jnp.concatenate(ys_blocks, axis=0) * _silu(gq)
    qmn2 = _head_rms(qm2, _vec(vec_ref, "mem_q_gain1"), bseg) * Q_SCALE
    ym2 = _mem_attn_both(qmn2, mkv_ref[1, 0], mkv_ref[1, 1], cmk_ref, cmv_ref, 1, tq, sb, hmask) * _silu(gm2)
    ob, ob_w = _W_OFFSET["out_b"]
    y = (x1 + _dot(ys.astype(BF16), w_ref[0:SWA_W, ob:ob + ob_w])
         + _dot(ym2.astype(BF16), w_ref[SWA_W:, ob:ob + ob_w]))
    yp_ref[...] = y[:tq]
    ys_ref[...] = y[tq:].reshape(sb, DEC_SEQ, D_MODEL)

    kprev[wr] = k_all[tq:, :]
    vprev[wr] = v_all[tq:, :]

    @pl.when(i == pl.num_programs(0) - 1)
    def _():
        kop_ref[...] = k_p[tq - WINDOW:, :].T
        vop_ref[...] = v_p[tq - WINDOW:, :].T
        pool_p_ref[...] = pltpu.roll(u_p[tq - CARRY_ROWS:, :], CARRY_ROWS - 1, 0)[0:POOL_PAD, :]


def _step_call(sinks, x_p, x_s, pref, ck, cv, cmk_t, cmv_t, rope_base, rope_res, rope_s, wts):
    tq = PROMPT_TILE
    sb = STEP_SEQS
    n = x_p.shape[0]
    nb = x_s.shape[0]
    assert n // tq == nb // sb
    row_spec = lambda w: pl.BlockSpec((tq, w), lambda i: (i, 0))
    seq3 = lambda a, b_: pl.BlockSpec((sb, a, b_), lambda i: (i, 0, 0))
    pool_s_spec = pl.BlockSpec((POOL_PAD, 2 * sb, POOL_W), lambda i: (0, i // 2, 0))
    cache_mem_spec = pl.BlockSpec((DEPTH, sb, MEM_W, N_MEM), lambda i: (0, i, 0, 0))
    in_specs = [pl.BlockSpec(memory_space=pltpu.SMEM), row_spec(D_MODEL),
                seq3(DEC_SEQ, D_MODEL), pool_s_spec, seq3(KV_W, WINDOW), seq3(KV_W, WINDOW),
                cache_mem_spec, cache_mem_spec,
                _const_spec(rope_base.shape), _const_spec(rope_res.shape), _const_spec(rope_s.shape)]
    in_specs += [_const_spec(w.shape) for w in wts]
    out_shape = [jax.ShapeDtypeStruct((n, D_MODEL), F32),
                 jax.ShapeDtypeStruct((POOL_PAD, POOL_W), F32),
                 jax.ShapeDtypeStruct((KV_W, WINDOW), F32),
                 jax.ShapeDtypeStruct((KV_W, WINDOW), F32),
                 jax.ShapeDtypeStruct((nb, DEC_SEQ, D_MODEL), F32),
                 jax.ShapeDtypeStruct((POOL_PAD, nb, POOL_W), F32),
                 jax.ShapeDtypeStruct((nb, KV_W, WINDOW), F32),
                 jax.ShapeDtypeStruct((nb, KV_W, WINDOW), F32)]
    out_specs = [row_spec(D_MODEL),
                 pl.BlockSpec((POOL_PAD, POOL_W), lambda i: (0, 0)),
                 pl.BlockSpec((KV_W, WINDOW), lambda i: (0, 0)),
                 pl.BlockSpec((KV_W, WINDOW), lambda i: (0, 0)),
                 seq3(DEC_SEQ, D_MODEL), pool_s_spec, seq3(KV_W, WINDOW), seq3(KV_W, WINDOW)]
    return pl.pallas_call(
        _step_kernel,
        grid=(n // tq,),
        in_specs=in_specs,
        out_specs=out_specs,
        out_shape=out_shape,
        scratch_shapes=[pltpu.VMEM((2, CARRY_ROWS, POOL_W), F32),
                        pltpu.VMEM((2, WINDOW, KV_W), BF16),
                        pltpu.VMEM((2, WINDOW, KV_W), BF16),
                        pltpu.VMEM((POOL_W // LANES, sb * DEC_SEQ, LANES), F32),
                        pltpu.VMEM((POOL_W // LANES, sb * DEC_SEQ, LANES), F32)],
        compiler_params=pltpu.CompilerParams(dimension_semantics=("arbitrary",),
                                             vmem_limit_bytes=STEP_VMEM_LIMIT),
        name="step",
    )(sinks, x_p, x_s, pref, ck, cv, cmk_t, cmv_t, rope_base, rope_res, rope_s, *wts)


PREP_ROWS = N_KV_HEADS * HEAD_DIM
PREP_VMEM_LIMIT = 40 * 1024 * 1024


def _cols_group_major(w):
    heads = [w[:, (kvh * GROUP + g) * HEAD_DIM:(kvh * GROUP + g + 1) * HEAD_DIM]
             for g in range(GROUP) for kvh in range(N_KV_HEADS)]
    return jnp.concatenate(heads, axis=1)


def _prep_kernel(w_in_a_ref, w_out_a_ref, w_kv_ref, w_in_b_ref, wob_h0_ref, wob_h1_ref, wob_h2_ref, wob_h3_ref,
                 wob_m_ref, w_mem_ref, mix_ref,
                 par_ref, mem_ref,
                 w_ref, wbd_ref, vec_ref, mk_ref, mv_ref, mkvb_ref,
                 mem_xn, mem_acc):
    i = pl.program_id(0)

    def put(name, value):
        start, width = _W_OFFSET[name]
        w_ref[:, start:start + width] = value.astype(BF16)

    def par(name):
        start, width = _PAR_OFFSET[name]
        return par_ref[:, start:start + width]

    def per_head(g_row, width):
        return jnp.concatenate([g_row] * (width // HEAD_DIM), axis=1)

    @pl.when(i == 0)
    def _():
        n_grp = len(POOL_WINDOWS)
        for g in range(n_grp):
            pieces = [jnp.zeros((POOL_GW, POOL_GW), F32)] * n_grp
            pieces[g] = mix_ref[g]
            wbd_ref[g * POOL_GW:(g + 1) * POOL_GW, :] = jnp.concatenate(pieces, axis=1).astype(BF16)

        vec_pieces = {"norm_a": par("norm_a"), "pool_scale": par("pool_scale"), "kv_norm": par("kv_norm"),
                      "k_gain": per_head(par("k_norm"), KV_W), "norm_b": par("norm_b"),
                      "q_gain": per_head(par("q_norm"), SWA_W),
                      "mem_q_gain0": per_head(par("mem_q_norm0"), MEM_W),
                      "mem_q_gain1": per_head(par("mem_q_norm1"), MEM_W)}
        for name, (start, width) in _VEC_OFFSET.items():
            vec_ref[:, start:start + width] = vec_pieces[name]

        mem_unit = _rms_unit(mem_ref[...])
        for l in range(DEPTH):
            mem_xn[l] = (mem_unit * par("mem_norm%d" % l)).astype(BF16)
        mem_acc[...] = jnp.zeros_like(mem_acc)

    put("in_a", w_in_a_ref[...])
    put("out_a", w_out_a_ref[...])
    put("kv", w_kv_ref[...])
    w_in_b = w_in_b_ref[...]
    put("q", _cols_group_major(w_in_b[:, :SWA_W]))
    put("q_gate", _cols_group_major(w_in_b[:, SWA_W:2 * SWA_W]))
    put("mem_b", w_in_b[:, 2 * SWA_W:])
    heads = jnp.concatenate([wob_h0_ref[0], wob_h1_ref[0], wob_h2_ref[0], wob_h3_ref[0]], axis=0)
    put("out_b", jnp.where(i < GROUP, heads, wob_m_ref[...]))
    n_slabs = D_MODEL // PREP_ROWS
    for j in range(n_slabs):
        @pl.when(i == j)
        def _(j=j):
            for l in range(DEPTH):
                mem_acc[l] += _dot(mem_xn[l, :, j * PREP_ROWS:(j + 1) * PREP_ROWS], w_mem_ref[l].astype(BF16))

    @pl.when(i == n_slabs - 1)
    def _():
        for l in range(DEPTH):
            kv = mem_acc[l]
            k = kv[:, :MEM_W]
            v_t = kv[:, MEM_W:].T
            k3 = k.T.reshape(MEM_HEADS, HEAD_DIM, N_MEM)
            ms = jnp.mean(k3 * k3, axis=1, keepdims=True)
            k_gained = (k * per_head(par("mem_k_norm%d" % l), MEM_W)).T.reshape(MEM_HEADS, HEAD_DIM, N_MEM)
            k_t = (k_gained * lax.rsqrt(ms + EPS)).reshape(MEM_W, N_MEM)
            mk_ref[l] = k_t
            mv_ref[l] = v_t
            mkvb_ref[l, 0] = k_t.astype(BF16)
            mkvb_ref[l, 1] = v_t.astype(BF16)


def _prep_call(w_in_a, w_out_a, w_kv, w_in_b, w_out_b, w_mem_kv, pool_mix, params, mem):
    rows = lambda w: pl.BlockSpec((PREP_ROWS, w), lambda i: (i, 0))

    def head_src(kvh):
        return pl.BlockSpec((1, HEAD_DIM, D_MODEL),
                            lambda i: (kvh * GROUP + jnp.minimum(i, GROUP - 1), 0, 0))

    in_specs = [rows(2 * POOL_W + 2 * MEM_W), rows(D_MODEL), rows(2 * KV_W), rows(2 * SWA_W + 2 * MEM_W)]
    in_specs += [head_src(kvh) for kvh in range(N_KV_HEADS)]
    in_specs += [pl.BlockSpec((PREP_ROWS, D_MODEL), lambda i: (SWA_W // PREP_ROWS, 0)),
                pl.BlockSpec((DEPTH, PREP_ROWS, 2 * MEM_W), lambda i: (0, i, 0)),
                pl.BlockSpec(pool_mix.shape, lambda i: (0, 0, 0))]
    in_specs += [pl.BlockSpec(params.shape, lambda i: (0, 0)), pl.BlockSpec(mem.shape, lambda i: (0, 0))]
    n_vec = sum(w for _, w in _VEC_WIDTHS)
    mem_f = jax.ShapeDtypeStruct((DEPTH, MEM_W, N_MEM), F32)
    out_specs = [rows(W_ALL_COLS),
                 pl.BlockSpec((POOL_W, POOL_W), lambda i: (0, 0)),
                 pl.BlockSpec((1, n_vec), lambda i: (0, 0)),
                 pl.BlockSpec(mem_f.shape, lambda i: (0, 0, 0)), pl.BlockSpec(mem_f.shape, lambda i: (0, 0, 0)),
                 pl.BlockSpec((DEPTH, 2, MEM_W, N_MEM), lambda i: (0, 0, 0, 0))]
    bf = lambda *shape: jax.ShapeDtypeStruct(shape, BF16)
    out_shape = [bf(D_MODEL, W_ALL_COLS), bf(POOL_W, POOL_W),
                 jax.ShapeDtypeStruct((1, n_vec), F32), mem_f, mem_f, bf(DEPTH, 2, MEM_W, N_MEM)]
    w_out_b_heads = w_out_b.reshape(D_MODEL // HEAD_DIM, HEAD_DIM, D_MODEL)
    return pl.pallas_call(
        _prep_kernel,
        grid=(D_MODEL // PREP_ROWS,),
        in_specs=in_specs,
        out_specs=out_specs,
        out_shape=out_shape,
        scratch_shapes=[pltpu.VMEM((DEPTH, N_MEM, D_MODEL), BF16), pltpu.VMEM((DEPTH, N_MEM, 2 * MEM_W), F32)],
        compiler_params=pltpu.CompilerParams(dimension_semantics=("arbitrary",),
                                             vmem_limit_bytes=PREP_VMEM_LIMIT),
        name="prep",
    )(w_in_a, w_out_a, w_kv, w_in_b, *([w_out_b_heads] * N_KV_HEADS), w_out_b, w_mem_kv, pool_mix, params, mem)


def _rope_lane_tables(pos):
    half = ROT_DIM // 2
    inv = (ROPE_THETA ** (-np.arange(half, dtype=np.float32) * 2.0 / ROT_DIM)).astype(np.float32)
    ang = (np.asarray(pos, np.float32)[:, None] * inv[None, :]).astype(np.float64)
    cos, sin = np.cos(ang), np.sin(ang)
    t = ang.shape[0]
    rest = HEAD_DIM - ROT_DIM
    cos64 = np.concatenate([cos, cos, np.ones((t, rest))], axis=1)
    sin64 = np.concatenate([-sin, sin, np.zeros((t, rest))], axis=1)
    reps = LANES // HEAD_DIM
    return np.tile(cos64, (1, reps)).astype(np.float32), np.tile(sin64, (1, reps)).astype(np.float32)


def kernel(x_prompt, x_sample, state_pool, cache_swa_k, cache_swa_v, cache_mem_k, cache_mem_v, mem_prompt,
           norm_a, w_in_a, pool_mix_w, pool_scale, w_out_a, kv_norm, w_kv, k_norm,
           norm_b, w_in_b, q_norm, sinks, w_out_b, mem_norm, w_mem_kv, mem_q_norm, mem_k_norm):
    seg = np.arange(MXU_DIM) // HEAD_DIM
    bseg = jnp.asarray((seg[:, None] == seg[None, :]).astype(np.float32) / HEAD_DIM, BF16)
    wcol = jnp.asarray(np.repeat(np.asarray(POOL_WINDOWS, np.float32), POOL_GW).reshape(1, POOL_W))

    raw = {"norm_a": norm_a[0], "pool_scale": pool_scale[0], "kv_norm": kv_norm, "norm_b": norm_b[0],
           "mem_norm0": mem_norm[0], "mem_norm1": mem_norm[1], "k_norm": k_norm, "q_norm": q_norm[0],
           "mem_q_norm0": mem_q_norm[0], "mem_q_norm1": mem_q_norm[1],
           "mem_k_norm0": mem_k_norm[0], "mem_k_norm1": mem_k_norm[1]}
    params = jnp.concatenate([raw[name].astype(F32) for name, _ in _PAR_WIDTHS])
    w_all, wbd, vecs, mk_t, mv_t, mkv_b = _prep_call(
        w_in_a[0], w_out_a[0], w_kv, w_in_b[0], w_out_b[0], w_mem_kv, pool_mix_w[0], params.reshape(1, _PAR_COLS),
        mem_prompt[0])
    wts = (vecs, w_all, wbd, wcol, mkv_b, bseg)

    n_tiles = SEQ // PROMPT_TILE
    cb, sb_ = _rope_lane_tables(np.arange(n_tiles) * PROMPT_TILE)
    cr, sr = _rope_lane_tables(np.arange(PROMPT_TILE))
    rope_base = jnp.asarray(np.stack([cb, sb_], axis=1))
    rope_res = jnp.asarray(np.stack([cr, sr], axis=0))
    cos_s, sin_s = _rope_lane_tables(PAST_LEN + np.arange(DEC_SEQ))
    rope_s = jnp.asarray(np.stack([np.tile(cos_s, (STEP_SEQS, 1)), np.tile(sin_s, (STEP_SEQS, 1))], axis=0))
    pref = jnp.transpose(state_pool[0], (1, 0, 2))
    cmk_t = jnp.transpose(cache_mem_k, (0, 1, 3, 4, 2)).reshape(DEPTH, DEC_BATCH, MEM_W, N_MEM)
    cmv_t = jnp.transpose(cache_mem_v, (0, 1, 3, 4, 2)).reshape(DEPTH, DEC_BATCH, MEM_W, N_MEM)
    ck_t = jnp.transpose(cache_swa_k, (0, 2, 3, 1)).reshape(DEC_BATCH, KV_W, WINDOW)
    cv_t = jnp.transpose(cache_swa_v, (0, 2, 3, 1)).reshape(DEC_BATCH, KV_W, WINDOW)
    y_p, pool_p, k_p, v_p, y_s, pool_s, k_s, v_s = _step_call(
        sinks[0].astype(F32), x_prompt[0], x_sample, pref, ck_t, cv_t, cmk_t, cmv_t, rope_base, rope_res, rope_s, wts)

    def mem_out(a):
        return jnp.transpose(a.reshape(DEPTH, 1, MEM_HEADS, HEAD_DIM, N_MEM), (0, 1, 4, 2, 3))

    def swa_out_t(a):
        return jnp.transpose(a.reshape(-1, N_KV_HEADS, HEAD_DIM, WINDOW), (0, 3, 1, 2))

    return (y_p[None], y_s, pool_p[None, None], jnp.transpose(pool_s, (1, 0, 2))[None],
            swa_out_t(k_p[None]), swa_out_t(v_p[None]), swa_out_t(k_s), swa_out_t(v_s),
            mem_out(mk_t), mem_out(mv_t))
```
